```python
import math
import jax, jax.numpy as jnp
from jax import lax
import numpy as np

D_MODEL = 2048
BATCH = 16
SEQ = 2048
DEPTH = 2

MIX_WIDTH = D_MODEL
A_HEAD_DIM = 64
A_WIDTH = 3 * D_MODEL // 8
A_Q_HEADS = A_WIDTH // A_HEAD_DIM
A_KV_HEADS = A_Q_HEADS // 3
A_GROUP = A_Q_HEADS // A_KV_HEADS
A_KV_WIDTH = A_KV_HEADS * A_HEAD_DIM
WINDOW = 128
ATTN_BLOCK = 128
ROPE_THETA = 500000.0
ROT_DIM = A_HEAD_DIM // 4
B_WIDTH = D_MODEL // 4
B_CONV_WIDTH = 31
C_HEAD_DIM = 128
C_HEADS = (MIX_WIDTH - A_WIDTH - B_WIDTH) // C_HEAD_DIM
C_WIDTH = C_HEADS * C_HEAD_DIM
C_CONV_WIDTH = 4
CHUNK = 64

EPS = 1e-6
MAX_POS_OFFSET = 4096

COL_SPLITS = (A_WIDTH, A_KV_WIDTH, A_KV_WIDTH, A_WIDTH,
              2 * B_WIDTH, B_WIDTH,
              3 * C_WIDTH, C_HEADS, C_HEADS, C_WIDTH)
IN_COLS = sum(COL_SPLITS)

kernel_name = "hybrid_swa_conformer_gdn_parallel"


def _split_cols(p):
    idx = []
    s = 0
    for n in COL_SPLITS[:-1]:
        s += n
        idx.append(s)
    return jnp.split(p, idx, axis=-1)


def rms_norm(x, w):
    x32 = x.astype(jnp.float32)
    y = x32 * lax.rsqrt(jnp.mean(x32 * x32, axis=-1, keepdims=True) + EPS)
    return (y * w.astype(jnp.float32)).astype(x.dtype)


def layer_norm(x, w, b):
    x32 = x.astype(jnp.float32)
    mu = jnp.mean(x32, axis=-1, keepdims=True)
    var = jnp.mean(jnp.square(x32 - mu), axis=-1, keepdims=True)
    y = (x32 - mu) * lax.rsqrt(var + EPS)
    return (y * w.astype(jnp.float32) + b.astype(jnp.float32)).astype(x.dtype)


def l2_norm(x):
    return x * lax.rsqrt(jnp.sum(x * x, axis=-1, keepdims=True) + EPS)


def causal_depthwise_conv(x, w):
    K, C = w.shape
    return lax.conv_general_dilated(
        x, w[:, None, :].astype(x.dtype), window_strides=(1,), padding=[(K - 1, 0)],
        dimension_numbers=('NWC', 'WIO', 'NWC'), feature_group_count=C)


def rope_tables(positions):
    inv_freq = ROPE_THETA ** (-jnp.arange(0, ROT_DIM, 2, dtype=jnp.float32) / ROT_DIM)
    ang = positions.astype(jnp.float32)[..., None] * inv_freq
    return jnp.cos(ang)[:, :, None, :], jnp.sin(ang)[:, :, None, :]


def apply_partial_rope(x, cos, sin):
    half = ROT_DIM // 2
    x1 = x[..., :half].astype(jnp.float32)
    x2 = x[..., half:ROT_DIM].astype(jnp.float32)
    rot = jnp.concatenate([x1 * cos - x2 * sin, x2 * cos + x1 * sin], axis=-1)
    return jnp.concatenate([rot.astype(x.dtype), x[..., ROT_DIM:]], axis=-1)


def sliding_window_sink_attention(q, k, v, sinks):
    bsz, T = q.shape[0], q.shape[1]
    nb = T // ATTN_BLOCK
    qb = q.reshape(bsz, nb, ATTN_BLOCK, A_KV_HEADS, A_GROUP, A_HEAD_DIM)
    kb = k.reshape(bsz, nb, ATTN_BLOCK, A_KV_HEADS, A_HEAD_DIM)
    vb = v.reshape(bsz, nb, ATTN_BLOCK, A_KV_HEADS, A_HEAD_DIM)
    prev = lambda t: jnp.concatenate([jnp.zeros_like(t[:, :1]), t[:, :-1]], axis=1)
    kk = jnp.concatenate([prev(kb), kb], axis=2)
    vv = jnp.concatenate([prev(vb), vb], axis=2)
    s = jnp.einsum('bnqhgd,bnkhd->bnhgqk', qb, kk).astype(jnp.float32) * (A_HEAD_DIM ** -0.5)
    qi = jnp.arange(ATTN_BLOCK)[:, None]
    kj = jnp.arange(2 * ATTN_BLOCK)[None, :]
    dist = qi + ATTN_BLOCK - kj
    band = (dist >= 0) & (dist < WINDOW)
    not_pad = (jnp.arange(nb)[:, None, None] > 0) | (kj[None] >= ATTN_BLOCK)
    valid = band[None] & not_pad
    s = jnp.where(valid[None, :, None, None], s, -jnp.inf)
    sink = sinks.reshape(A_KV_HEADS, A_GROUP).astype(jnp.float32)[None, None, :, :, None, None]
    sink = jnp.broadcast_to(sink, s.shape[:-1] + (1,))
    p = jax.nn.softmax(jnp.concatenate([s, sink], axis=-1), axis=-1)[..., :-1]
    o = jnp.einsum('bnhgqk,bnkhd->bnqhgd', p.astype(v.dtype), vv)
    return o.reshape(bsz, T, A_Q_HEADS * A_HEAD_DIM)


def conformer_conv_module(u, conv_w, conv_b, ln_w, ln_b, pw_w, pw_b):
    a, gt = jnp.split(u, 2, axis=-1)
    h = a * jax.nn.sigmoid(gt)
    h = causal_depthwise_conv(h, conv_w) + conv_b.astype(h.dtype)
    h = jax.nn.silu(layer_norm(h, ln_w, ln_b))
    return h @ pw_w + pw_b


def gated_delta_rule_chunked(q, k, v, g, beta):
    bsz, T, H, Dk = q.shape
    Dv = v.shape[-1]
    n = T // CHUNK
    q = q * (Dk ** -0.5)

    def to_chunks(t):
        return t.reshape(bsz, n, CHUNK, H, t.shape[-1]).transpose(0, 1, 3, 2, 4)

    qc, kc, vc = to_chunks(q), to_chunks(k), to_chunks(v)
    bc = beta.reshape(bsz, n, CHUNK, H).transpose(0, 1, 3, 2)
    gc = jnp.cumsum(g.reshape(bsz, n, CHUNK, H).transpose(0, 1, 3, 2), axis=-1)
    idx = jnp.arange(CHUNK)
    incl = idx[:, None] >= idx[None, :]
    strict = idx[:, None] > idx[None, :]
    decay = jnp.exp(jnp.where(incl, gc[..., :, None] - gc[..., None, :], -jnp.inf))
    kb = kc * bc[..., None]
    lower = jnp.where(strict, jnp.einsum('bnhid,bnhjd->bnhij', kb, kc) * decay, 0.0)
    eye = jnp.eye(CHUNK, dtype=lower.dtype)
    rhs = jnp.concatenate([vc * bc[..., None], kb * jnp.exp(gc)[..., None]], axis=-1)
    sol = lax.linalg.triangular_solve(lower + eye, rhs, left_side=True, lower=True,
                                      unit_diagonal=True)
    u, w = sol[..., :Dv], sol[..., Dv:]
    intra = jnp.where(incl, jnp.einsum('bnhid,bnhjd->bnhij', qc, kc) * decay, 0.0)

    def step(S, xs):
        q_i, k_i, u_i, w_i, g_i, a_i = xs
        v_new = u_i - jnp.einsum('bhck,bhkv->bhcv', w_i, S)
        o_i = (jnp.einsum('bhck,bhkv->bhcv', q_i * jnp.exp(g_i)[..., None], S)
               + jnp.einsum('bhij,bhjv->bhiv', a_i, v_new))
        g_last = g_i[..., -1:]
        S = (S * jnp.exp(g_last)[..., None]
             + jnp.einsum('bhck,bhcv->bhkv', k_i * jnp.exp(g_last - g_i)[..., None], v_new))
        return S, o_i

    S0 = jnp.zeros((bsz, H, Dk, Dv), q.dtype)
    xs = tuple(jnp.moveaxis(t, 1, 0) for t in (qc, kc, u, w, gc, intra))
    _, o = lax.scan(step, S0, xs)
    return o.transpose(1, 0, 3, 2, 4).reshape(bsz, T, H, Dv)


def gated_deltanet(qkv_raw, b_raw, a_raw, conv_w, a_log, dt_bias):
    bsz, T = qkv_raw.shape[0], qkv_raw.shape[1]
    qkv = jax.nn.silu(causal_depthwise_conv(qkv_raw, conv_w)).astype(jnp.float32)
    q, k, v = jnp.split(qkv, 3, axis=-1)
    q = l2_norm(q.reshape(bsz, T, C_HEADS, C_HEAD_DIM))
    k = l2_norm(k.reshape(bsz, T, C_HEADS, C_HEAD_DIM))
    v = v.reshape(bsz, T, C_HEADS, C_HEAD_DIM)
    beta = jax.nn.sigmoid(b_raw.astype(jnp.float32))
    g = -jnp.exp(a_log.astype(jnp.float32)) * jax.nn.softplus(
        a_raw.astype(jnp.float32) + dt_bias.astype(jnp.float32))
    return gated_delta_rule_chunked(q, k, v, g, beta)


def _fwd_setup_inputs(seed: int = 0) -> dict:
    key = jax.random.key(seed)
    ks = jax.random.split(key, 24)
    nrm = jax.random.normal
    x = nrm(ks[0], (BATCH, SEQ, D_MODEL), jnp.float32)
    offset = jax.random.randint(ks[1], (BATCH, 1), 0, MAX_POS_OFFSET, dtype=jnp.int32)
    positions = (offset + jnp.arange(SEQ, dtype=jnp.int32)[None, :]).astype(jnp.int32)
    norm_w = 1.0 + 0.02 * nrm(ks[2], (DEPTH, D_MODEL))
    w_in = nrm(ks[3], (DEPTH, D_MODEL, IN_COLS)) * D_MODEL ** -0.5
    q_norm_w = 1.0 + 0.02 * nrm(ks[4], (DEPTH, A_HEAD_DIM))
    k_norm_w = 1.0 + 0.02 * nrm(ks[5], (DEPTH, A_HEAD_DIM))
    sinks = nrm(ks[6], (DEPTH, A_Q_HEADS))
    b_conv_w = nrm(ks[7], (DEPTH, B_CONV_WIDTH, B_WIDTH)) * B_CONV_WIDTH ** -0.5
    b_conv_b = 0.02 * nrm(ks[8], (DEPTH, B_WIDTH))
    b_ln_w = 1.0 + 0.02 * nrm(ks[9], (DEPTH, B_WIDTH))
    b_ln_b = 0.02 * nrm(ks[10], (DEPTH, B_WIDTH))
    b_pw_w = nrm(ks[11], (DEPTH, B_WIDTH, B_WIDTH)) * B_WIDTH ** -0.5
    b_pw_b = 0.02 * nrm(ks[12], (DEPTH, B_WIDTH))
    c_conv_w = nrm(ks[13], (DEPTH, C_CONV_WIDTH, 3 * C_WIDTH)) * C_CONV_WIDTH ** -0.5
    c_a_log = jnp.log(jax.random.uniform(ks[14], (DEPTH, C_HEADS), minval=1.0, maxval=16.0))
    dt = jnp.exp(jax.random.uniform(ks[15], (DEPTH, C_HEADS),
                                    minval=math.log(1e-3), maxval=math.log(1e-1)))
    c_dt_bias = dt + jnp.log(-jnp.expm1(-dt))
    c_onorm_w = 1.0 + 0.02 * nrm(ks[16], (DEPTH, C_HEAD_DIM))
    w_out = nrm(ks[17], (DEPTH, MIX_WIDTH, D_MODEL)) * MIX_WIDTH ** -0.5
    return {"x": x, "positions": positions, "norm_w": norm_w, "w_in": w_in,
            "q_norm_w": q_norm_w, "k_norm_w": k_norm_w, "sinks": sinks,
            "b_conv_w": b_conv_w, "b_conv_b": b_conv_b, "b_ln_w": b_ln_w,
            "b_ln_b": b_ln_b, "b_pw_w": b_pw_w, "b_pw_b": b_pw_b,
            "c_conv_w": c_conv_w, "c_a_log": c_a_log, "c_dt_bias": c_dt_bias,
            "c_onorm_w": c_onorm_w, "w_out": w_out}


def _fwd_reference(x, positions, norm_w, w_in, q_norm_w, k_norm_w, sinks, b_conv_w, b_conv_b,
              b_ln_w, b_ln_b, b_pw_w, b_pw_b, c_conv_w, c_a_log, c_dt_bias, c_onorm_w,
              w_out):
    bsz, T = x.shape[0], x.shape[1]
    cos, sin = rope_tables(positions)
    for l in range(DEPTH):
        h = rms_norm(x, norm_w[l])
        p = h @ w_in[l]
        qa, ka, va, za, ub, zb, qkv_c, b_c, a_c, zc = _split_cols(p)
        qa = apply_partial_rope(rms_norm(qa.reshape(bsz, T, A_Q_HEADS, A_HEAD_DIM), q_norm_w[l]), cos, sin)
        ka = apply_partial_rope(rms_norm(ka.reshape(bsz, T, A_KV_HEADS, A_HEAD_DIM), k_norm_w[l]), cos, sin)
        va = va.reshape(bsz, T, A_KV_HEADS, A_HEAD_DIM)
        oa = sliding_window_sink_attention(qa, ka, va, sinks[l]) * jax.nn.silu(za)
        ob = conformer_conv_module(ub, b_conv_w[l], b_conv_b[l], b_ln_w[l], b_ln_b[l],
                                   b_pw_w[l], b_pw_b[l]) * jax.nn.silu(zb)
        oc = gated_deltanet(qkv_c, b_c, a_c, c_conv_w[l], c_a_log[l], c_dt_bias[l]).astype(x.dtype)
        oc = rms_norm(oc, c_onorm_w[l]) * jax.nn.silu(zc.reshape(bsz, T, C_HEADS, C_HEAD_DIM))
        oc = oc.reshape(bsz, T, C_WIDTH)
        y = jnp.concatenate([oa, ob, oc], axis=-1)
        x = x + y @ w_out[l]
    return x


import jax as _jax
import jax.numpy as _jnp

TWIN_FORMAT = 'train_step'
FWD_PARAMS = ['x', 'positions', 'norm_w', 'w_in', 'q_norm_w', 'k_norm_w', 'sinks', 'b_conv_w', 'b_conv_b', 'b_ln_w', 'b_ln_b', 'b_pw_w', 'b_pw_b', 'c_conv_w', 'c_a_log', 'c_dt_bias', 'c_onorm_w', 'w_out']
TWIN_WEIGHTS = ['norm_w', 'w_in', 'q_norm_w', 'k_norm_w', 'sinks', 'b_conv_w', 'b_conv_b', 'b_ln_w', 'b_ln_b', 'b_pw_w', 'b_pw_b', 'c_conv_w', 'c_a_log', 'c_dt_bias', 'c_onorm_w', 'w_out']
TWIN_DIFF_INPUT = 'x'
TWIN_INPUTS = ['x', 'positions', 'norm_w', 'w_in', 'q_norm_w', 'k_norm_w', 'sinks', 'b_conv_w', 'b_conv_b', 'b_ln_w', 'b_ln_b', 'b_pw_w', 'b_pw_b', 'c_conv_w', 'c_a_log', 'c_dt_bias', 'c_onorm_w', 'w_out', 'loss_target', 'm_norm_w', 'm_w_in', 'm_q_norm_w', 'm_k_norm_w', 'm_sinks', 'm_b_conv_w', 'm_b_conv_b', 'm_b_ln_w', 'm_b_ln_b', 'm_b_pw_w', 'm_b_pw_b', 'm_c_conv_w', 'm_c_a_log', 'm_c_dt_bias', 'm_c_onorm_w', 'm_w_out', 'v_norm_w', 'v_w_in', 'v_q_norm_w', 'v_k_norm_w', 'v_sinks', 'v_b_conv_w', 'v_b_conv_b', 'v_b_ln_w', 'v_b_ln_b', 'v_b_pw_w', 'v_b_pw_b', 'v_c_conv_w', 'v_c_a_log', 'v_c_dt_bias', 'v_c_onorm_w', 'v_w_out']
TWIN_OUTPUTS = ['loss', 'grad_x', 'grad_norm_w', 'grad_w_in', 'grad_q_norm_w', 'grad_k_norm_w', 'grad_sinks', 'grad_b_conv_w', 'grad_b_conv_b', 'grad_b_ln_w', 'grad_b_ln_b', 'grad_b_pw_w', 'grad_b_pw_b', 'grad_c_conv_w', 'grad_c_a_log', 'grad_c_dt_bias', 'grad_c_onorm_w', 'grad_w_out', 'delta_norm_w', 'delta_w_in', 'delta_q_norm_w', 'delta_k_norm_w', 'delta_sinks', 'delta_b_conv_w', 'delta_b_conv_b', 'delta_b_ln_w', 'delta_b_ln_b', 'delta_b_pw_w', 'delta_b_pw_b', 'delta_c_conv_w', 'delta_c_a_log', 'delta_c_dt_bias', 'delta_c_onorm_w', 'delta_w_out', 'new_m_norm_w', 'new_m_w_in', 'new_m_q_norm_w', 'new_m_k_norm_w', 'new_m_sinks', 'new_m_b_conv_w', 'new_m_b_conv_b', 'new_m_b_ln_w', 'new_m_b_ln_b', 'new_m_b_pw_w', 'new_m_b_pw_b', 'new_m_c_conv_w', 'new_m_c_a_log', 'new_m_c_dt_bias', 'new_m_c_onorm_w', 'new_m_w_out', 'new_v_norm_w', 'new_v_w_in', 'new_v_q_norm_w', 'new_v_k_norm_w', 'new_v_sinks', 'new_v_b_conv_w', 'new_v_b_conv_b', 'new_v_b_ln_w', 'new_v_b_ln_b', 'new_v_b_pw_w', 'new_v_b_pw_b', 'new_v_c_conv_w', 'new_v_c_a_log', 'new_v_c_dt_bias', 'new_v_c_onorm_w', 'new_v_w_out']
TWIN_LEAF_KINDS = {'loss': 'loss', 'grad_x': 'grad_x', 'grad_norm_w': 'grad_w', 'grad_w_in': 'grad_w', 'grad_q_norm_w': 'grad_w', 'grad_k_norm_w': 'grad_w', 'grad_sinks': 'grad_w', 'grad_b_conv_w': 'grad_w', 'grad_b_conv_b': 'grad_w', 'grad_b_ln_w': 'grad_w', 'grad_b_ln_b': 'grad_w', 'grad_b_pw_w': 'grad_w', 'grad_b_pw_b': 'grad_w', 'grad_c_conv_w': 'grad_w', 'grad_c_a_log': 'grad_w', 'grad_c_dt_bias': 'grad_w', 'grad_c_onorm_w': 'grad_w', 'grad_w_out': 'grad_w', 'delta_norm_w': 'delta_w', 'delta_w_in': 'delta_w', 'delta_q_norm_w': 'delta_w', 'delta_k_norm_w': 'delta_w', 'delta_sinks': 'delta_w', 'delta_b_conv_w': 'delta_w', 'delta_b_conv_b': 'delta_w', 'delta_b_ln_w': 'delta_w', 'delta_b_ln_b': 'delta_w', 'delta_b_pw_w': 'delta_w', 'delta_b_pw_b': 'delta_w', 'delta_c_conv_w': 'delta_w', 'delta_c_a_log': 'delta_w', 'delta_c_dt_bias': 'delta_w', 'delta_c_onorm_w': 'delta_w', 'delta_w_out': 'delta_w', 'new_m_norm_w': 'new_m', 'new_m_w_in': 'new_m', 'new_m_q_norm_w': 'new_m', 'new_m_k_norm_w': 'new_m', 'new_m_sinks': 'new_m', 'new_m_b_conv_w': 'new_m', 'new_m_b_conv_b': 'new_m', 'new_m_b_ln_w': 'new_m', 'new_m_b_ln_b': 'new_m', 'new_m_b_pw_w': 'new_m', 'new_m_b_pw_b': 'new_m', 'new_m_c_conv_w': 'new_m', 'new_m_c_a_log': 'new_m', 'new_m_c_dt_bias': 'new_m', 'new_m_c_onorm_w': 'new_m', 'new_m_w_out': 'new_m', 'new_v_norm_w': 'new_v', 'new_v_w_in': 'new_v', 'new_v_q_norm_w': 'new_v', 'new_v_k_norm_w': 'new_v', 'new_v_sinks': 'new_v', 'new_v_b_conv_w': 'new_v', 'new_v_b_conv_b': 'new_v', 'new_v_b_ln_w': 'new_v', 'new_v_b_ln_b': 'new_v', 'new_v_b_pw_w': 'new_v', 'new_v_b_pw_b': 'new_v', 'new_v_c_conv_w': 'new_v', 'new_v_c_a_log': 'new_v', 'new_v_c_dt_bias': 'new_v', 'new_v_c_onorm_w': 'new_v', 'new_v_w_out': 'new_v'}


def _forward(args):
    return _fwd_reference(*[args[k] for k in FWD_PARAMS])


def _output_shape():
    out = _jax.eval_shape(lambda: _forward(_fwd_setup_inputs(0)))
    return out.shape, out.dtype

N_MICROBATCH = 1
ADAM_LR = 0.001
ADAM_B1 = 0.9
ADAM_B2 = 0.999
ADAM_EPS = 1e-08
ADAM_WD = 0.01
ADAM_STEP = 10
PER_EXAMPLE_BATCH_AXIS = {'x': 0, 'positions': 0, 'loss_target': 0}
SHARED_INPUTS = []
_WEIGHT_DTYPES = {'norm_w': _jnp.float32, 'w_in': _jnp.float32, 'q_norm_w': _jnp.float32, 'k_norm_w': _jnp.float32, 'sinks': _jnp.float32, 'b_conv_w': _jnp.float32, 'b_conv_b': _jnp.float32, 'b_ln_w': _jnp.float32, 'b_ln_b': _jnp.float32, 'b_pw_w': _jnp.float32, 'b_pw_b': _jnp.float32, 'c_conv_w': _jnp.float32, 'c_a_log': _jnp.float32, 'c_dt_bias': _jnp.float32, 'c_onorm_w': _jnp.float32, 'w_out': _jnp.float32}
MOMENT_SCALE = {'norm_w': 3.048991e+00, 'w_in': 1.265716e-01, 'q_norm_w': 7.840152e-01, 'k_norm_w': 7.797080e-01, 'sinks': 1.535446e-01, 'b_conv_w': 9.348740e-02, 'b_conv_b': 1.197927e+00, 'b_ln_w': 2.496238e+00, 'b_ln_b': 1.704958e+00, 'b_pw_w': 2.358072e-01, 'b_pw_b': 1.310024e+00, 'c_conv_w': 2.326997e-01, 'c_a_log': 1.100640e+01, 'c_dt_bias': 1.054497e+01, 'c_onorm_w': 3.527851e+01, 'w_out': 2.411973e-01}


def _to_microbatches(a, axis):
    t = _jnp.moveaxis(a, axis, 0)
    t = t.reshape((N_MICROBATCH, t.shape[0] // N_MICROBATCH) + t.shape[1:])
    return _jnp.moveaxis(t, 1, axis + 1)


def setup_inputs(seed: int = 0) -> dict:
    inp = _fwd_setup_inputs(seed)
    key = _jax.random.fold_in(_jax.random.key(seed), 7919)
    shape, _ = _output_shape()
    out = dict(inp)
    out["loss_target"] = _jax.random.normal(_jax.random.fold_in(key, 0), shape, _jnp.float32)
    for i, name in enumerate(TWIN_WEIGHTS):
        w = inp[name].astype(_jnp.float32)
        if MOMENT_SCALE is None:
            s = _jnp.sqrt(_jnp.mean(_jnp.square(w)) + 1e-30)
        else:
            s = MOMENT_SCALE[name]
        km, kv = _jax.random.split(_jax.random.fold_in(key, i + 1))
        out[name] = w
        out["m_" + name] = s * _jax.random.normal(km, w.shape, _jnp.float32)
        out["v_" + name] = (s * s) * _jax.random.uniform(kv, w.shape, _jnp.float32, 0.5, 1.5)
    if N_MICROBATCH > 1:
        for name, axis in PER_EXAMPLE_BATCH_AXIS.items():
            out[name] = _to_microbatches(out[name], axis)
    return {'x': out['x'], 'positions': out['positions'], 'norm_w': out['norm_w'], 'w_in': out['w_in'], 'q_norm_w': out['q_norm_w'], 'k_norm_w': out['k_norm_w'], 'sinks': out['sinks'], 'b_conv_w': out['b_conv_w'], 'b_conv_b': out['b_conv_b'], 'b_ln_w': out['b_ln_w'], 'b_ln_b': out['b_ln_b'], 'b_pw_w': out['b_pw_w'], 'b_pw_b': out['b_pw_b'], 'c_conv_w': out['c_conv_w'], 'c_a_log': out['c_a_log'], 'c_dt_bias': out['c_dt_bias'], 'c_onorm_w': out['c_onorm_w'], 'w_out': out['w_out'], 'loss_target': out['loss_target'], 'm_norm_w': out['m_norm_w'], 'm_w_in': out['m_w_in'], 'm_q_norm_w': out['m_q_norm_w'], 'm_k_norm_w': out['m_k_norm_w'], 'm_sinks': out['m_sinks'], 'm_b_conv_w': out['m_b_conv_w'], 'm_b_conv_b': out['m_b_conv_b'], 'm_b_ln_w': out['m_b_ln_w'], 'm_b_ln_b': out['m_b_ln_b'], 'm_b_pw_w': out['m_b_pw_w'], 'm_b_pw_b': out['m_b_pw_b'], 'm_c_conv_w': out['m_c_conv_w'], 'm_c_a_log': out['m_c_a_log'], 'm_c_dt_bias': out['m_c_dt_bias'], 'm_c_onorm_w': out['m_c_onorm_w'], 'm_w_out': out['m_w_out'], 'v_norm_w': out['v_norm_w'], 'v_w_in': out['v_w_in'], 'v_q_norm_w': out['v_q_norm_w'], 'v_k_norm_w': out['v_k_norm_w'], 'v_sinks': out['v_sinks'], 'v_b_conv_w': out['v_b_conv_w'], 'v_b_conv_b': out['v_b_conv_b'], 'v_b_ln_w': out['v_b_ln_w'], 'v_b_ln_b': out['v_b_ln_b'], 'v_b_pw_w': out['v_b_pw_w'], 'v_b_pw_b': out['v_b_pw_b'], 'v_c_conv_w': out['v_c_conv_w'], 'v_c_a_log': out['v_c_a_log'], 'v_c_dt_bias': out['v_c_dt_bias'], 'v_c_onorm_w': out['v_c_onorm_w'], 'v_w_out': out['v_w_out']}


def _loss(weights, diff, rest, loss_target):
    with _jax.named_scope("forward"):
        args = {**rest, TWIN_DIFF_INPUT: diff, **{k: w.astype(_WEIGHT_DTYPES[k]) for k, w in weights.items()}}
        y = _forward(args)
    with _jax.named_scope("loss_head"):
        err = _jnp.square(y.astype(_jnp.float32) - loss_target)
        return 0.5 * _jnp.sum(_jnp.mean(err, axis=-1)) if err.ndim else 0.5 * err


def _adamw(w, g, m, v):
    m = ADAM_B1 * m + (1.0 - ADAM_B1) * g
    v = ADAM_B2 * v + (1.0 - ADAM_B2) * _jnp.square(g)
    m_hat = m / (1.0 - ADAM_B1 ** ADAM_STEP)
    v_hat = v / (1.0 - ADAM_B2 ** ADAM_STEP)
    delta = -ADAM_LR * (m_hat / (_jnp.sqrt(v_hat) + ADAM_EPS) + ADAM_WD * w)
    return delta, m, v


def reference(x, positions, norm_w, w_in, q_norm_w, k_norm_w, sinks, b_conv_w, b_conv_b, b_ln_w, b_ln_b, b_pw_w, b_pw_b, c_conv_w, c_a_log, c_dt_bias, c_onorm_w, w_out, loss_target, m_norm_w, m_w_in, m_q_norm_w, m_k_norm_w, m_sinks, m_b_conv_w, m_b_conv_b, m_b_ln_w, m_b_ln_b, m_b_pw_w, m_b_pw_b, m_c_conv_w, m_c_a_log, m_c_dt_bias, m_c_onorm_w, m_w_out, v_norm_w, v_w_in, v_q_norm_w, v_k_norm_w, v_sinks, v_b_conv_w, v_b_conv_b, v_b_ln_w, v_b_ln_b, v_b_pw_w, v_b_pw_b, v_c_conv_w, v_c_a_log, v_c_dt_bias, v_c_onorm_w, v_w_out):
    given = dict(x=x, positions=positions, norm_w=norm_w, w_in=w_in, q_norm_w=q_norm_w, k_norm_w=k_norm_w, sinks=sinks, b_conv_w=b_conv_w, b_conv_b=b_conv_b, b_ln_w=b_ln_w, b_ln_b=b_ln_b, b_pw_w=b_pw_w, b_pw_b=b_pw_b, c_conv_w=c_conv_w, c_a_log=c_a_log, c_dt_bias=c_dt_bias, c_onorm_w=c_onorm_w, w_out=w_out, loss_target=loss_target, m_norm_w=m_norm_w, m_w_in=m_w_in, m_q_norm_w=m_q_norm_w, m_k_norm_w=m_k_norm_w, m_sinks=m_sinks, m_b_conv_w=m_b_conv_w, m_b_conv_b=m_b_conv_b, m_b_ln_w=m_b_ln_w, m_b_ln_b=m_b_ln_b, m_b_pw_w=m_b_pw_w, m_b_pw_b=m_b_pw_b, m_c_conv_w=m_c_conv_w, m_c_a_log=m_c_a_log, m_c_dt_bias=m_c_dt_bias, m_c_onorm_w=m_c_onorm_w, m_w_out=m_w_out, v_norm_w=v_norm_w, v_w_in=v_w_in, v_q_norm_w=v_q_norm_w, v_k_norm_w=v_k_norm_w, v_sinks=v_sinks, v_b_conv_w=v_b_conv_w, v_b_conv_b=v_b_conv_b, v_b_ln_w=v_b_ln_w, v_b_ln_b=v_b_ln_b, v_b_pw_w=v_b_pw_w, v_b_pw_b=v_b_pw_b, v_c_conv_w=v_c_conv_w, v_c_a_log=v_c_a_log, v_c_dt_bias=v_c_dt_bias, v_c_onorm_w=v_c_onorm_w, v_w_out=v_w_out)
    weights = {n: given[n] for n in TWIN_WEIGHTS}
    shared = {n: given[n] for n in SHARED_INPUTS}
    per_example = {n: given[n] for n in ['x', 'positions']}
    grad_fn = _jax.value_and_grad(_loss, argnums=(0, 1))

    def one_microbatch(ex, loss_target):
        ex = dict(ex)
        diff = ex.pop(TWIN_DIFF_INPUT)
        return grad_fn(weights, diff, {**shared, **ex}, loss_target)

    if N_MICROBATCH == 1:
        loss, (grad_w, grad_x) = one_microbatch(per_example, given["loss_target"])
    else:
        def body(carry, xs):
            loss_sum, grad_sum = carry
            l_k, (gw_k, gx_k) = one_microbatch(xs[0], xs[1])
            with _jax.named_scope("update"):
                return (loss_sum + l_k, _jax.tree.map(_jnp.add, grad_sum, gw_k)), gx_k

        init = (_jnp.zeros((), _jnp.float32), _jax.tree.map(_jnp.zeros_like, weights))
        (loss, grad_w), grad_x = _jax.lax.scan(body, init, (per_example, given["loss_target"]))
    with _jax.named_scope("update"):
        delta_w, new_m, new_v = {}, {}, {}
        for n in TWIN_WEIGHTS:
            delta_w[n], new_m[n], new_v[n] = _adamw(weights[n], grad_w[n], given["m_" + n], given["v_" + n])
    return (loss, grad_x, *[grad_w[n] for n in TWIN_WEIGHTS], *[delta_w[n] for n in TWIN_WEIGHTS],
            *[new_m[n] for n in TWIN_WEIGHTS], *[new_v[n] for n in TWIN_WEIGHTS])
```

```python
import functools
import math

import numpy as np
import jax
import jax.numpy as jnp
from jax import lax
from jax.experimental import pallas as pl
from jax.experimental.pallas import tpu as pltpu

F32 = jnp.float32
BF16 = jnp.bfloat16
HI = lax.Precision.HIGHEST
MESH = pl.DeviceIdType.MESH

DEPTH = 2
A_HEAD = 64
A_GROUP = 3
ATTN_BLOCK = 128
ROT_DIM = 16
ROPE_THETA = 500000.0
B_CONV = 31
B_HALO = 32
C_HEAD = 128
C_CONV = 4
C_HALO = 8
CHUNK = 64
EPS = 1e-6
LANE = 128

ADAM_LR = 0.001
ADAM_B1 = 0.9
ADAM_B2 = 0.999
ADAM_EPS = 1e-08
ADAM_WD = 0.01
ADAM_STEP = 10

VMEM_LIMIT = 56 * 1024 * 1024

WEIGHTS = ['norm_w', 'w_in', 'q_norm_w', 'k_norm_w', 'sinks', 'b_conv_w', 'b_conv_b', 'b_ln_w', 'b_ln_b',
           'b_pw_w', 'b_pw_b', 'c_conv_w', 'c_a_log', 'c_dt_bias', 'c_onorm_w', 'w_out']


class Cfg:
    def __init__(self, d_model=2048, seq=2048):
        self.D = d_model
        self.T = seq
        self.AW = 3 * d_model // 8
        self.AQH = self.AW // A_HEAD
        self.AKH = self.AQH // A_GROUP
        self.AKW = self.AKH * A_HEAD
        self.BW = d_model // 4
        self.CH = (d_model - self.AW - self.BW) // C_HEAD
        self.CW = self.CH * C_HEAD
        AW, AKW, BW, CW, CH = self.AW, self.AKW, self.BW, self.CW, self.CH
        orig = [('qa', AW), ('ka', AKW), ('va', AKW), ('za', AW), ('ub', 2 * BW), ('zb', BW),
                ('qc', CW), ('kc', CW), ('vc', CW), ('bc', CH), ('ac', CH), ('zc', CW)]
        self.orig = {}
        off = 0
        for n, w in orig:
            self.orig[n] = (off, w)
            off += w
        self.IN_COLS = off
        order = ['qa', 'za', 'qc', 'kc', 'vc', 'zc', 'ka', 'va', 'ub', 'zb', 'bc', 'ac']
        self.order = order
        self.g = {}
        off = 0
        for n in order:
            w = self.orig[n][1]
            wp = LANE if n in ('bc', 'ac') else w
            assert off % wp == 0, (n, off, wp)
            self.g[n] = (off, wp)
            off += wp
        self.WP = off

    def blk(self, name):
        off, w = self.g[name]
        return off // w


def _cparams(sem, vmem=VMEM_LIMIT):
    return pltpu.CompilerParams(dimension_semantics=sem, vmem_limit_bytes=vmem)


def _silu(x):
    return x * jax.nn.sigmoid(x)


def _rope_matrix():
    i = lax.broadcasted_iota(jnp.int32, (A_HEAD, A_HEAD), 0)
    j = lax.broadcasted_iota(jnp.int32, (A_HEAD, A_HEAD), 1)
    half = ROT_DIM // 2
    neg = (j < half) & (i == j + half)
    pos = (j >= half) & (j < ROT_DIM) & (i == j - half)
    return jnp.where(neg, -1.0, jnp.where(pos, 1.0, 0.0)).astype(F32)


def _norm_rope(xh, w, cos, sin, rot):
    y = xh * lax.rsqrt(jnp.mean(xh * xh, axis=-1, keepdims=True) + EPS) * w
    return y * cos + jnp.dot(y, rot, precision=HI, preferred_element_type=F32) * sin


def attn_block(cfg, first, q, za, kc, vc, cosc, sinc, kp, vp, cosp, sinp, qnw, knw, sinks_row):
    rot = _rope_matrix()
    blk = ATTN_BLOCK
    qi = lax.broadcasted_iota(jnp.int32, (blk, 2 * blk), 0)
    kj = lax.broadcasted_iota(jnp.int32, (blk, 2 * blk), 1)
    dist = qi + blk - kj
    valid = (dist >= 0) & (dist < blk) & (jnp.logical_not(first) | (kj >= blk))
    cos2 = jnp.concatenate([cosp, cosc], axis=0)
    sin2 = jnp.concatenate([sinp, sinc], axis=0)
    outs = []
    for hk in range(cfg.AKH):
        sl = slice(A_HEAD * hk, A_HEAD * (hk + 1))
        k2 = jnp.concatenate([kp[:, sl], kc[:, sl]], axis=0)
        v2 = jnp.concatenate([vp[:, sl], vc[:, sl]], axis=0)
        k2 = _norm_rope(k2, knw, cos2, sin2, rot).astype(BF16)
        v2 = v2.astype(BF16)
        for g in range(A_GROUP):
            hq = hk * A_GROUP + g
            ql = slice(A_HEAD * hq, A_HEAD * (hq + 1))
            qh = _norm_rope(q[:, ql], qnw, cosc, sinc, rot).astype(BF16)
            s = lax.dot_general(qh, k2, (((1,), (1,)), ((), ())), preferred_element_type=F32) * (A_HEAD ** -0.5)
            s = jnp.where(valid, s, -1e30)
            sink = sinks_row[:, A_HEAD * hq:A_HEAD * hq + 1]
            m = jnp.maximum(jnp.max(s, axis=-1, keepdims=True), sink)
            e = jnp.exp(s - m)
            den = jnp.sum(e, axis=-1, keepdims=True) + jnp.exp(sink - m)
            p = (e / den).astype(BF16)
            o = jnp.dot(p, v2, preferred_element_type=F32)
            outs.append(o * _silu(za[:, ql]))
    return (jnp.concatenate(outs, axis=1),)


def conv_block(cfg, first, u, zb, uh, cw, cb, lw, lb, pw, pb):
    BW = cfg.BW
    tb = u.shape[0]
    uu = jnp.concatenate([uh, u], axis=0)
    h = uu[:, :BW] * jax.nn.sigmoid(uu[:, BW:])
    row = lax.broadcasted_iota(jnp.int32, h.shape, 0)
    h = jnp.where(first & (row < B_HALO), 0.0, h)
    acc = jnp.zeros((tb, BW), F32) + cb
    base = B_HALO - (B_CONV - 1)
    for k in range(B_CONV):
        acc = acc + cw[k:k + 1, :] * h[base + k:base + k + tb, :]
    mu = jnp.mean(acc, axis=-1, keepdims=True)
    var = jnp.mean(jnp.square(acc - mu), axis=-1, keepdims=True)
    y = (acc - mu) * lax.rsqrt(var + EPS) * lw + lb
    s = _silu(y)
    o = jnp.dot(s.astype(BF16), pw.astype(BF16), preferred_element_type=F32) + pb
    return (o * _silu(zb),)


def gdn_prep_block(cfg, first, xq, xk, xv, braw, araw, hq, hk, hv, cw, alog, dtb):
    CW = cfg.CW
    tb = xq.shape[0]
    outs = []
    for idx, (x, xh) in enumerate(((xq, hq), (xk, hk), (xv, hv))):
        xx = jnp.concatenate([jnp.where(first, 0.0, xh), x], axis=0)
        w = cw[:, idx * CW:(idx + 1) * CW]
        acc = jnp.zeros((tb, CW), F32)
        base = C_HALO - (C_CONV - 1)
        for k in range(C_CONV):
            acc = acc + w[k:k + 1, :] * xx[base + k:base + k + tb, :]
        y = _silu(acc)
        if idx < 2:
            parts = []
            for h in range(cfg.CH):
                yh = y[:, C_HEAD * h:C_HEAD * (h + 1)]
                parts.append(yh * lax.rsqrt(jnp.sum(yh * yh, axis=-1, keepdims=True) + EPS))
            y = jnp.concatenate(parts, axis=1)
        outs.append(y)
    beta = jax.nn.sigmoid(braw)
    g = -jnp.exp(alog) * jax.nn.softplus(araw + dtb)
    return outs[0], outs[1], outs[2], g, beta


def gdn_intra_block(cfg, first, qn, kn, v, g, beta):
    c = CHUNK
    i = lax.broadcasted_iota(jnp.int32, (c, c), 0)
    j = lax.broadcasted_iota(jnp.int32, (c, c), 1)
    incl = i >= j
    strict = i > j
    eye = (i == j).astype(F32)
    gc_all = jnp.dot(incl.astype(F32), g, precision=HI, preferred_element_type=F32)
    us, ws, qgs, kds, intras = [], [], [], [], []
    for h in range(cfg.CH):
        sl = slice(C_HEAD * h, C_HEAD * (h + 1))
        gch = gc_all[:, h:h + 1]
        bh = beta[:, h:h + 1]
        a = jnp.broadcast_to(gch, (c, c))
        diff = jnp.where(incl, a - a.T, 0.0)
        decay = jnp.where(incl, jnp.exp(diff), 0.0)
        q = qn[:, sl] * (C_HEAD ** -0.5)
        k = kn[:, sl]
        kb = k * bh
        kkT = lax.dot_general(kb, k, (((1,), (1,)), ((), ())), precision=HI, preferred_element_type=F32)
        low = jnp.where(strict, kkT * decay, 0.0)
        pw = low
        inv = eye - low
        for _ in range(5):
            pw = jnp.dot(pw, pw, precision=HI, preferred_element_type=F32)
            inv = inv + jnp.dot(inv, pw, precision=HI, preferred_element_type=F32)
        rhs = jnp.concatenate([v[:, sl] * bh, kb * jnp.exp(gch)], axis=1)
        sol = jnp.dot(inv, rhs, precision=HI, preferred_element_type=F32)
        us.append(sol[:, :C_HEAD])
        ws.append(sol[:, C_HEAD:])
        qkT = lax.dot_general(q, k, (((1,), (1,)), ((), ())), precision=HI, preferred_element_type=F32)
        intras.append(jnp.where(incl, qkT * decay, 0.0))
        qgs.append(q * jnp.exp(gch))
        kds.append(k * jnp.exp(gch[c - 1:c, :] - gch))
    glast = jnp.broadcast_to(gc_all[c - 1:c, :], gc_all.shape)
    cat = lambda xs: jnp.concatenate(xs, axis=1)
    return cat(us), cat(ws), cat(qgs), cat(kds), cat(intras), glast


def gdn_state_step(S, u, w, qg, kd, intra, glast_h):
    v_new = u - jnp.dot(w, S, precision=HI, preferred_element_type=F32)
    o = (jnp.dot(qg, S, precision=HI, preferred_element_type=F32)
         + jnp.dot(intra, v_new, precision=HI, preferred_element_type=F32))
    S_next = S * jnp.exp(glast_h) + lax.dot_general(kd, v_new, (((0,), (0,)), ((), ())), precision=HI,
                                                     preferred_element_type=F32)
    return o, S_next


def gdn_out_block(cfg, first, o, zc, onw):
    parts = []
    for h in range(cfg.CH):
        sl = slice(C_HEAD * h, C_HEAD * (h + 1))
        oh = o[:, sl]
        y = oh * lax.rsqrt(jnp.mean(oh * oh, axis=-1, keepdims=True) + EPS) * onw
        parts.append(y * _silu(zc[:, sl]))
    return (jnp.concatenate(parts, axis=1),)


def rms_block(x, nw):
    return x * lax.rsqrt(jnp.mean(x * x, axis=-1, keepdims=True) + EPS) * nw


class Row:
    def __init__(self, arr, width, colblk=0, grad=None):
        self.arr, self.width, self.colblk, self.grad = arr, width, colblk, grad


class Halo:
    def __init__(self, arr, width, colblk, hr, tie=None):
        self.arr, self.width, self.colblk, self.hr, self.tie = arr, width, colblk, hr, tie


def _row_specs(tb, rows, halos, params, pos):
    specs = [pl.BlockSpec((tb, r.width), lambda i, cb=r.colblk: (pos(i), cb)) for r in rows]
    specs += [pl.BlockSpec((h.hr, h.width),
                           lambda i, cb=h.colblk, m=tb // h.hr: (jnp.maximum(pos(i) * m - 1, 0), cb))
              for h in halos]
    specs += [pl.BlockSpec(p.shape, lambda i: (0, 0)) for p in params]
    return specs


def rb_fwd(name, fn, n, tb, bps, rows, halos, params, outs):
    nr, nh, npar = len(rows), len(halos), len(params)

    def body(*refs):
        ins = refs[:nr + nh + npar]
        o_refs = refs[nr + nh + npar:]
        first = (pl.program_id(0) % bps) == 0
        res = fn(first, *[r[...] for r in ins])
        for ref, val in zip(o_refs, res):
            ref[...] = val.astype(ref.dtype)

    return pl.pallas_call(
        body, name=name, grid=(n // tb,),
        in_specs=_row_specs(tb, rows, halos, params, lambda i: i),
        out_specs=[pl.BlockSpec((tb, w), lambda i: (i, 0)) for w, _ in outs],
        out_shape=[jax.ShapeDtypeStruct((n, w), dt) for w, dt in outs],
        compiler_params=_cparams(("parallel",)),
    )(*[r.arr for r in rows], *[h.arr for h in halos], *params)


def rb_bwd(name, fn, n, tb, bps, rows, halos, params, douts, param_grads):
    nr, nh, npar, nd = len(rows), len(halos), len(params), len(douts)
    nblk = n // tb
    grow = [k for k, r in enumerate(rows) if r.grad is not None]
    ghalo = [k for k, h in enumerate(halos) if h.tie is not None]
    gpar = [k for k, f in enumerate(param_grads) if f]
    pos = lambda i: nblk - 1 - i

    def body(*refs):
        ins = refs[:nr + nh + npar]
        d_refs = refs[nr + nh + npar:nr + nh + npar + nd]
        rest = refs[nr + nh + npar + nd:]
        grow_refs = rest[:len(grow)]
        gpar_refs = rest[len(grow):len(grow) + len(gpar)]
        carry_refs = rest[len(grow) + len(gpar):]
        i = pl.program_id(0)
        first = (pos(i) % bps) == 0
        vals = [r[...] for r in ins]
        diff_idx = grow + [nr + k for k in ghalo] + [nr + nh + k for k in gpar]

        def f(*dargs):
            full = list(vals)
            for k, a in zip(diff_idx, dargs):
                full[k] = a
            return fn(first, *full)

        res, vjp = jax.vjp(f, *[vals[k] for k in diff_idx])
        grads = vjp(tuple(d[...].astype(r.dtype) for d, r in zip(d_refs, res)))
        g_rows = list(grads[:len(grow)])
        g_halos = grads[len(grow):len(grow) + len(ghalo)]
        g_pars = grads[len(grow) + len(ghalo):]

        @pl.when(i == 0)
        def _():
            for c in carry_refs:
                c[...] = jnp.zeros_like(c)
            for p in gpar_refs:
                p[...] = jnp.zeros_like(p)

        for k, ref in enumerate(grow_refs):
            ref[...] = g_rows[k].astype(ref.dtype)
        for ci, hk in enumerate(ghalo):
            h = halos[hk]
            k = grow.index(h.tie)
            tail = g_rows[k][tb - h.hr:, :] + carry_refs[ci][...]
            grow_refs[k][tb - h.hr:, :] = tail.astype(grow_refs[k].dtype)
            carry_refs[ci][...] = g_halos[ci]
        for ref, gp in zip(gpar_refs, g_pars):
            ref[...] += gp

    out_specs = [pl.BlockSpec((tb, rows[k].width), lambda i: (pos(i), 0)) for k in grow]
    out_specs += [pl.BlockSpec(params[k].shape, lambda i: (0, 0)) for k in gpar]
    out_shape = [jax.ShapeDtypeStruct((n, rows[k].width), rows[k].grad) for k in grow]
    out_shape += [jax.ShapeDtypeStruct(params[k].shape, F32) for k in gpar]
    in_specs = _row_specs(tb, rows, halos, params, pos)
    in_specs += [pl.BlockSpec((tb, d.shape[1]), lambda i: (pos(i), 0)) for d in douts]
    return pl.pallas_call(
        body, name=name, grid=(nblk,), in_specs=in_specs, out_specs=out_specs, out_shape=out_shape,
        scratch_shapes=[pltpu.VMEM((halos[k].hr, halos[k].width), F32) for k in ghalo],
        compiler_params=_cparams(("arbitrary",)),
    )(*[r.arr for r in rows], *[h.arr for h in halos], *params, *douts)


_DIMS = {'nn': (((1,), (0,)), ((), ())), 'nt': (((1,), (1,)), ((), ())), 'tn': (((0,), (0,)), ((), ()))}


def matmul(name, a, b, mode, tm, tn, tk, out_dtype=F32, add=None):
    if mode == 'tn':
        K, M = a.shape
    else:
        M, K = a.shape
    N = b.shape[0] if mode == 'nt' else b.shape[1]
    tm, tn, tk = min(tm, M), min(tn, N), min(tk, K)
    assert M % tm == 0 and N % tn == 0 and K % tk == 0, (name, M, N, K, tm, tn, tk)
    nk = K // tk
    a_spec = pl.BlockSpec((tk, tm), lambda i, j, k: (k, i)) if mode == 'tn' else pl.BlockSpec((tm, tk), lambda i, j, k: (i, k))
    b_spec = pl.BlockSpec((tn, tk), lambda i, j, k: (j, k)) if mode == 'nt' else pl.BlockSpec((tk, tn), lambda i, j, k: (k, j))
    o_spec = pl.BlockSpec((tm, tn), lambda i, j, k: (i, j))
    has_add = add is not None

    def body(*refs):
        a_ref, b_ref = refs[0], refs[1]
        add_ref = refs[2] if has_add else None
        o_ref, acc = refs[-2], refs[-1]
        k = pl.program_id(2)

        @pl.when(k == 0)
        def _():
            acc[...] = jnp.zeros_like(acc)

        acc[...] += lax.dot_general(a_ref[...].astype(BF16), b_ref[...].astype(BF16), _DIMS[mode],
                                    preferred_element_type=F32)

        @pl.when(k == nk - 1)
        def _():
            r = acc[...]
            if has_add:
                r = r + add_ref[...]
            o_ref[...] = r.astype(o_ref.dtype)

    ins = [a, b] + ([add] if has_add else [])
    in_specs = [a_spec, b_spec] + ([o_spec] if has_add else [])
    return pl.pallas_call(
        body, name=name, grid=(M // tm, N // tn, nk), in_specs=in_specs, out_specs=o_spec,
        out_shape=jax.ShapeDtypeStruct((M, N), out_dtype),
        scratch_shapes=[pltpu.VMEM((tm, tn), F32)],
        compiler_params=_cparams(("parallel", "parallel", "arbitrary")),
    )(*ins)


def norm_in_proj(name, x, nw, wp, tm, tn):
    n, d = x.shape
    wpc = wp.shape[1]
    tm, tn = min(tm, n), min(tn, wpc)
    assert n % tm == 0 and wpc % tn == 0

    def body(x_ref, nw_ref, w_ref, p_ref, h_ref):
        @pl.when(pl.program_id(1) == 0)
        def _():
            h_ref[...] = rms_block(x_ref[...], nw_ref[...]).astype(BF16)

        p_ref[...] = jnp.dot(h_ref[...], w_ref[...], preferred_element_type=F32)

    return pl.pallas_call(
        body, name=name, grid=(n // tm, wpc // tn),
        in_specs=[pl.BlockSpec((tm, d), lambda i, j: (i, 0)), pl.BlockSpec((1, d), lambda i, j: (0, 0)),
                  pl.BlockSpec((d, tn), lambda i, j: (0, j))],
        out_specs=[pl.BlockSpec((tm, tn), lambda i, j: (i, j)), pl.BlockSpec((tm, d), lambda i, j: (i, 0))],
        out_shape=[jax.ShapeDtypeStruct((n, wpc), F32), jax.ShapeDtypeStruct((n, d), BF16)],
        compiler_params=_cparams(("parallel", "arbitrary")),
    )(x, nw, wp)


def norm_bwd(name, x, nw, dh, dres, tb):
    n, d = x.shape
    tb = min(tb, n)

    def body(x_ref, nw_ref, dh_ref, dres_ref, dx_ref, dnw_ref):
        @pl.when(pl.program_id(0) == 0)
        def _():
            dnw_ref[...] = jnp.zeros_like(dnw_ref)

        _, vjp = jax.vjp(rms_block, x_ref[...], nw_ref[...])
        dx, dnw = vjp(dh_ref[...])
        dx_ref[...] = dx + dres_ref[...]
        dnw_ref[...] += dnw

    row = pl.BlockSpec((tb, d), lambda i: (i, 0))
    par = pl.BlockSpec((1, d), lambda i: (0, 0))
    return pl.pallas_call(
        body, name=name, grid=(n // tb,), in_specs=[row, par, row, row], out_specs=[row, par],
        out_shape=[jax.ShapeDtypeStruct((n, d), F32), jax.ShapeDtypeStruct((1, d), F32)],
        compiler_params=_cparams(("arbitrary",)),
    )(x, nw, dh, dres)


def loss_grad(name, y, target, tb):
    n, d = y.shape
    tb = min(tb, n)

    def body(y_ref, t_ref, dy_ref, loss_ref):
        @pl.when(pl.program_id(0) == 0)
        def _():
            loss_ref[...] = jnp.zeros_like(loss_ref)

        err = y_ref[...] - t_ref[...]
        dy_ref[...] = err * (1.0 / d)
        part = 0.5 * jnp.sum(jnp.mean(err * err, axis=-1, keepdims=True), axis=0, keepdims=True)
        loss_ref[...] += jnp.broadcast_to(part, loss_ref.shape)

    row = pl.BlockSpec((tb, d), lambda i: (i, 0))
    dy, loss = pl.pallas_call(
        body, name=name, grid=(n // tb,), in_specs=[row, row],
        out_specs=[row, pl.BlockSpec((8, LANE), lambda i: (0, 0))],
        out_shape=[jax.ShapeDtypeStruct((n, d), F32), jax.ShapeDtypeStruct((8, LANE), F32)],
        compiler_params=_cparams(("arbitrary",)),
    )(y, target)
    return dy, loss[0, 0]


def rope_tables(name, pos_col, inv_freq_row):
    n = pos_col.shape[0]

    def body(p_ref, f_ref, c_ref, s_ref):
        ang = p_ref[...].astype(F32) * f_ref[...]
        lane = lax.broadcasted_iota(jnp.int32, ang.shape, 1)
        c_ref[...] = jnp.where(lane < ROT_DIM, jnp.cos(ang), 1.0)
        s_ref[...] = jnp.where(lane < ROT_DIM, jnp.sin(ang), 0.0)

    return pl.pallas_call(
        body, name=name, out_shape=[jax.ShapeDtypeStruct((n, A_HEAD), F32)] * 2,
    )(pos_col, inv_freq_row)


def gdn_scan_fwd(name, cfg, nseq, u, w, qg, kd, intra, glast):
    CH, CW, T = cfg.CH, cfg.CW, cfg.T
    nc = T // CHUNK

    def body(u_ref, w_ref, qg_ref, kd_ref, a_ref, gl_ref, o_ref, sin_ref, s_ref):
        @pl.when(pl.program_id(1) == 0)
        def _():
            s_ref[...] = jnp.zeros_like(s_ref)

        outs = []
        for h in range(CH):
            sl = slice(C_HEAD * h, C_HEAD * (h + 1))
            S = s_ref[h]
            sin_ref[0, h] = S
            o, S_next = gdn_state_step(S, u_ref[:, sl], w_ref[:, sl], qg_ref[:, sl], kd_ref[:, sl],
                                       a_ref[:, CHUNK * h:CHUNK * (h + 1)], gl_ref[0:1, h:h + 1])
            s_ref[h] = S_next
            outs.append(o)
        o_ref[...] = jnp.concatenate(outs, axis=1)

    row = lambda wd: pl.BlockSpec((CHUNK, wd), lambda b, c: (b * nc + c, 0))
    return pl.pallas_call(
        body, name=name, grid=(nseq, nc),
        in_specs=[row(CW), row(CW), row(CW), row(CW), row(CH * CHUNK), row(LANE)],
        out_specs=[row(CW), pl.BlockSpec((1, CH, C_HEAD, C_HEAD), lambda b, c: (b * nc + c, 0, 0, 0))],
        out_shape=[jax.ShapeDtypeStruct((nseq * T, CW), F32),
                   jax.ShapeDtypeStruct((nseq * nc, CH, C_HEAD, C_HEAD), F32)],
        scratch_shapes=[pltpu.VMEM((CH, C_HEAD, C_HEAD), F32)],
        compiler_params=_cparams(("parallel", "arbitrary")),
    )(u, w, qg, kd, intra, glast)


def gdn_scan_bwd(name, cfg, nseq, u, w, qg, kd, intra, glast, s_in, do):
    CH, CW, T = cfg.CH, cfg.CW, cfg.T
    nc = T // CHUNK

    def body(u_ref, w_ref, qg_ref, kd_ref, a_ref, gl_ref, sin_ref, do_ref,
             du_ref, dw_ref, dqg_ref, dkd_ref, da_ref, dgl_ref, ds_ref):
        @pl.when(pl.program_id(1) == 0)
        def _():
            ds_ref[...] = jnp.zeros_like(ds_ref)

        dus, dws, dqgs, dkds, das = [], [], [], [], []
        lane = lax.broadcasted_iota(jnp.int32, (CHUNK, LANE), 1)
        rowi = lax.broadcasted_iota(jnp.int32, (CHUNK, LANE), 0)
        dgl = jnp.zeros((CHUNK, LANE), F32)
        for h in range(CH):
            sl = slice(C_HEAD * h, C_HEAD * (h + 1))
            args = (sin_ref[0, h], u_ref[:, sl], w_ref[:, sl], qg_ref[:, sl], kd_ref[:, sl],
                    a_ref[:, CHUNK * h:CHUNK * (h + 1)], gl_ref[0:1, h:h + 1])
            _, vjp = jax.vjp(gdn_state_step, *args)
            dS, du, dw, dqg, dkd, da, dg = vjp((do_ref[:, sl], ds_ref[h]))
            ds_ref[h] = dS
            dus.append(du)
            dws.append(dw)
            dqgs.append(dqg)
            dkds.append(dkd)
            das.append(da)
            dgl = dgl + jnp.where((lane == h) & (rowi == 0), dg, 0.0)
        cat = lambda xs: jnp.concatenate(xs, axis=1)
        du_ref[...] = cat(dus)
        dw_ref[...] = cat(dws)
        dqg_ref[...] = cat(dqgs)
        dkd_ref[...] = cat(dkds)
        da_ref[...] = cat(das)
        dgl_ref[...] = dgl

    pos = lambda b, c: b * nc + nc - 1 - c
    row = lambda wd: pl.BlockSpec((CHUNK, wd), lambda b, c: (pos(b, c), 0))
    widths = [CW, CW, CW, CW, CH * CHUNK, LANE]
    return pl.pallas_call(
        body, name=name, grid=(nseq, nc),
        in_specs=[row(x) for x in widths]
        + [pl.BlockSpec((1, CH, C_HEAD, C_HEAD), lambda b, c: (pos(b, c), 0, 0, 0)), row(CW)],
        out_specs=[row(x) for x in widths],
        out_shape=[jax.ShapeDtypeStruct((nseq * T, x), F32) for x in widths],
        scratch_shapes=[pltpu.VMEM((CH, C_HEAD, C_HEAD), F32)],
        compiler_params=_cparams(("parallel", "arbitrary")),
    )(u, w, qg, kd, intra, glast, s_in, do)


def _tile(total, cap, unit=LANE):
    best = None
    for t in range(unit, min(cap, total) + 1, unit):
        if total % t == 0:
            best = t
    assert best is not None, (total, cap, unit)
    return best


def _pad_lanes(v, width=LANE):
    return jnp.pad(v.reshape(1, -1), ((0, 0), (0, width - v.shape[-1])))


def permute_w_in(cfg, w):
    parts = []
    for n in cfg.order:
        off, wd = cfg.orig[n]
        blk = w[:, off:off + wd]
        if cfg.g[n][1] != wd:
            blk = jnp.pad(blk, ((0, 0), (0, cfg.g[n][1] - wd)))
        parts.append(blk)
    return jnp.concatenate(parts, axis=1)


def unpermute_w_in(cfg, wp):
    parts = []
    for n, (off, wd) in cfg.orig.items():
        parts.append(wp[:, cfg.g[n][0]:cfg.g[n][0] + wd])
    return jnp.concatenate(parts, axis=1)


def _layer_params(cfg, prm, l):
    return dict(
        nw=prm['norm_w'][l].reshape(1, -1),
        qnw=prm['q_norm_w'][l].reshape(1, -1), knw=prm['k_norm_w'][l].reshape(1, -1),
        sinks_row=jnp.repeat(prm['sinks'][l], A_HEAD).reshape(1, -1),
        cw=prm['b_conv_w'][l], cb=prm['b_conv_b'][l].reshape(1, -1),
        lw=prm['b_ln_w'][l].reshape(1, -1), lb=prm['b_ln_b'][l].reshape(1, -1),
        pw=prm['b_pw_w'][l], pb=prm['b_pw_b'][l].reshape(1, -1),
        ccw=prm['c_conv_w'][l], alog=_pad_lanes(prm['c_a_log'][l]), dtb=_pad_lanes(prm['c_dt_bias'][l]),
        onw=prm['c_onorm_w'][l].reshape(1, -1),
    )


def _attn_io(cfg, p, cos, sin, grads):
    gq = BF16 if grads else None
    rows = [Row(p, cfg.AW, cfg.blk('qa'), gq), Row(p, cfg.AW, cfg.blk('za'), gq),
            Row(p, cfg.AKW, cfg.blk('ka'), gq), Row(p, cfg.AKW, cfg.blk('va'), gq),
            Row(cos, A_HEAD), Row(sin, A_HEAD)]
    halos = [Halo(p, cfg.AKW, cfg.blk('ka'), ATTN_BLOCK, 2 if grads else None),
             Halo(p, cfg.AKW, cfg.blk('va'), ATTN_BLOCK, 3 if grads else None),
             Halo(cos, A_HEAD, 0, ATTN_BLOCK), Halo(sin, A_HEAD, 0, ATTN_BLOCK)]
    return rows, halos


def _conv_io(cfg, p, grads):
    gq = BF16 if grads else None
    rows = [Row(p, 2 * cfg.BW, cfg.blk('ub'), gq), Row(p, cfg.BW, cfg.blk('zb'), gq)]
    halos = [Halo(p, 2 * cfg.BW, cfg.blk('ub'), B_HALO, 0 if grads else None)]
    return rows, halos


def _prep_io(cfg, p, grads):
    gq = BF16 if grads else None
    rows = [Row(p, cfg.CW, cfg.blk(n), gq) for n in ('qc', 'kc', 'vc')]
    rows += [Row(p, LANE, cfg.blk('bc'), gq), Row(p, LANE, cfg.blk('ac'), gq)]
    halos = [Halo(p, cfg.CW, cfg.blk(n), C_HALO, k if grads else None) for k, n in enumerate(('qc', 'kc', 'vc'))]
    return rows, halos


TB_CONV = 128
TB_PREP = 256
TB_OUT = 256


def layer_forward(cfg, l, x, lp, wp, wo, cos, sin):
    n = x.shape[0]
    nseq = n // cfg.T
    T = cfg.T
    p, h = norm_in_proj(f"in_proj_{l}", x, lp['nw'], wp, 512, _tile(cfg.WP, 1152))
    rows, halos = _attn_io(cfg, p, cos, sin, False)
    (oa,) = rb_fwd(f"attn_fwd_{l}", functools.partial(attn_block, cfg), n, ATTN_BLOCK, T // ATTN_BLOCK, rows, halos,
                   [lp['qnw'], lp['knw'], lp['sinks_row']], [(cfg.AW, BF16)])
    rows, halos = _conv_io(cfg, p, False)
    tbb = min(TB_CONV, T)
    (ob,) = rb_fwd(f"conv_fwd_{l}", functools.partial(conv_block, cfg), n, tbb, T // tbb, rows, halos,
                   [lp['cw'], lp['cb'], lp['lw'], lp['lb'], lp['pw'], lp['pb']], [(cfg.BW, BF16)])
    rows, halos = _prep_io(cfg, p, False)
    tbp = min(TB_PREP, T)
    qn, kn, v, g, beta = rb_fwd(f"gdn_prep_fwd_{l}", functools.partial(gdn_prep_block, cfg), n, tbp, T // tbp, rows,
                                halos, [lp['ccw'], lp['alog'], lp['dtb']],
                                [(cfg.CW, F32)] * 3 + [(LANE, F32)] * 2)
    intra_outs = rb_fwd(f"gdn_intra_fwd_{l}", functools.partial(gdn_intra_block, cfg), n, CHUNK, T // CHUNK,
                        [Row(qn, cfg.CW), Row(kn, cfg.CW), Row(v, cfg.CW), Row(g, LANE), Row(beta, LANE)], [], [],
                        [(cfg.CW, F32)] * 4 + [(cfg.CH * CHUNK, F32), (LANE, F32)])
    o, s_in = gdn_scan_fwd(f"gdn_scan_fwd_{l}", cfg, nseq, *intra_outs)
    tbo = min(TB_OUT, T)
    (oc,) = rb_fwd(f"gdn_out_fwd_{l}", functools.partial(gdn_out_block, cfg), n, tbo, T // tbo,
                   [Row(o, cfg.CW), Row(p, cfg.CW, cfg.blk('zc'))], [], [lp['onw']], [(cfg.CW, BF16)])
    y = jnp.concatenate([oa, ob, oc], axis=1)
    x_next = matmul(f"out_proj_{l}", y, wo, 'nn', 512, 512, cfg.D, add=x)
    saved = dict(x=x, p=p, h=h, y=y, qn=qn, kn=kn, v=v, g=g, beta=beta, intra_outs=intra_outs, s_in=s_in, o=o)
    return x_next, saved


def layer_backward(cfg, l, dxn, sv, lp, wp, wo, cos, sin):
    n = dxn.shape[0]
    nseq = n // cfg.T
    T = cfg.T
    p = sv['p']
    AW, BW, CW = cfg.AW, cfg.BW, cfg.CW
    dy = matmul(f"dy_{l}", dxn, wo, 'nt', 512, 512, cfg.D)
    dwo = matmul(f"dwo_{l}", sv['y'], dxn, 'tn', 512, 512, 1024)
    doa, dob, doc = dy[:, :AW], dy[:, AW:AW + BW], dy[:, AW + BW:]
    tbo = min(TB_OUT, T)
    do, dzc, donw = rb_bwd(f"gdn_out_bwd_{l}", functools.partial(gdn_out_block, cfg), n, tbo, T // tbo,
                           [Row(sv['o'], CW, 0, F32), Row(p, CW, cfg.blk('zc'), BF16)], [], [lp['onw']], [doc], [True])
    dintra = gdn_scan_bwd(f"gdn_scan_bwd_{l}", cfg, nseq, *sv['intra_outs'], sv['s_in'], do)
    dqn, dkn, dv, dg, dbeta = rb_bwd(
        f"gdn_intra_bwd_{l}", functools.partial(gdn_intra_block, cfg), n, CHUNK, T // CHUNK,
        [Row(sv['qn'], CW, 0, F32), Row(sv['kn'], CW, 0, F32), Row(sv['v'], CW, 0, F32), Row(sv['g'], LANE, 0, F32),
         Row(sv['beta'], LANE, 0, F32)], [], [], list(dintra), [])
    rows, halos = _prep_io(cfg, p, True)
    tbp = min(TB_PREP, T)
    dqc, dkc, dvc, dbc, dac, dccw, dalog, ddtb = rb_bwd(
        f"gdn_prep_bwd_{l}", functools.partial(gdn_prep_block, cfg), n, tbp, T // tbp, rows, halos,
        [lp['ccw'], lp['alog'], lp['dtb']], [dqn, dkn, dv, dg, dbeta], [True] * 3)
    rows, halos = _conv_io(cfg, p, True)
    tbb = min(TB_CONV, T)
    dub, dzb, dcw, dcb, dlw, dlb, dpw, dpb = rb_bwd(
        f"conv_bwd_{l}", functools.partial(conv_block, cfg), n, tbb, T // tbb, rows, halos,
        [lp['cw'], lp['cb'], lp['lw'], lp['lb'], lp['pw'], lp['pb']], [dob], [True] * 6)
    rows, halos = _attn_io(cfg, p, cos, sin, True)
    dqa, dza, dka, dva, dqnw, dknw, dsinks_row = rb_bwd(
        f"attn_bwd_{l}", functools.partial(attn_block, cfg), n, ATTN_BLOCK, T // ATTN_BLOCK, rows, halos,
        [lp['qnw'], lp['knw'], lp['sinks_row']], [doa], [True] * 3)
    dgroups = dict(qa=dqa, za=dza, qc=dqc, kc=dkc, vc=dvc, zc=dzc, ka=dka, va=dva, ub=dub, zb=dzb, bc=dbc, ac=dac)
    dp = jnp.concatenate([dgroups[k] for k in cfg.order], axis=1)
    dh = matmul(f"dh_{l}", dp, wp, 'nt', 512, 512, _tile(cfg.WP, 2304))
    dwp = matmul(f"dwp_{l}", sv['h'], dp, 'tn', 512, _tile(cfg.WP, 1152), 1024)
    dx, dnw = norm_bwd(f"norm_bwd_{l}", sv['x'], lp['nw'], dh, dxn, 256)
    grads = dict(
        norm_w=dnw[0], w_in_perm=dwp, q_norm_w=dqnw[0], k_norm_w=dknw[0],
        sinks=dsinks_row.reshape(cfg.AQH, A_HEAD)[:, 0],
        b_conv_w=dcw, b_conv_b=dcb[0], b_ln_w=dlw[0], b_ln_b=dlb[0], b_pw_w=dpw, b_pw_b=dpb[0],
        c_conv_w=dccw, c_a_log=dalog[0, :cfg.CH], c_dt_bias=ddtb[0, :cfg.CH], c_onorm_w=donw[0], w_out=dwo)
    return dx, grads


def local_step(cfg, x, positions, prm, wps, wos, target):
    nseq = x.shape[0]
    n = nseq * cfg.T
    x2 = x.reshape(n, cfg.D)
    inv_freq = ROPE_THETA ** (-np.arange(0, ROT_DIM, 2, dtype=np.float32) / ROT_DIM)
    freq_row = np.zeros((1, A_HEAD), np.float32)
    freq_row[0, :ROT_DIM] = np.concatenate([inv_freq, inv_freq])
    cos, sin = rope_tables("rope_tables", positions.reshape(n, 1), jnp.asarray(freq_row))
    lps = [_layer_params(cfg, prm, l) for l in range(DEPTH)]
    saved = []
    xl = x2
    for l in range(DEPTH):
        xl, sv = layer_forward(cfg, l, xl, lps[l], wps[l], wos[l], cos, sin)
        saved.append(sv)
    dx, loss = loss_grad("loss_grad", xl, target.reshape(n, cfg.D), 256)
    grads = [None] * DEPTH
    for l in reversed(range(DEPTH)):
        dx, grads[l] = layer_backward(cfg, l, dx, saved[l], lps[l], wps[l], wos[l], cos, sin)
    return loss, dx.reshape(x.shape), grads


ANY = pl.BlockSpec(memory_space=pl.ANY)
N_CHIPS = 4
N_DEV = 8


def _place():
    return lax.axis_index("x"), lax.axis_index("y"), lax.axis_index("c")


def _other_chips(x, y):
    return [(1 - x, y), (x, 1 - y), (1 - x, 1 - y)]


def _remote(src, dst, send, recv, to):
    return pltpu.make_async_remote_copy(src_ref=src, dst_ref=dst, send_sem=send, recv_sem=recv, device_id=to,
                                        device_id_type=MESH)


def all_gather_chips(name, arrs):
    n = len(arrs)

    def body(*refs):
        ins, outs = refs[:n], refs[n:2 * n]
        send, recv, loc = refs[2 * n:]
        x, y, c = _place()
        me = 2 * x + y
        chips = _other_chips(x, y)
        sib = (x, y, 1 - c)
        own = [pltpu.make_async_copy(ins[i], outs[i].at[me], loc.at[i]) for i in range(n)]
        for cp in own:
            cp.start()
        first = [_remote(ins[i].at[c], outs[i].at[me, c], send.at[i, j], recv.at[i, j], (cx, cy, c))
                 for i in range(n) for j, (cx, cy) in enumerate(chips)]
        for cp in first:
            cp.start()
        passed = []
        for i in range(n):
            for j, (cx, cy) in enumerate(chips):
                blk = outs[i].at[2 * cx + cy, c]
                _remote(blk, blk, send.at[i, j], recv.at[i, j], (x, y, c)).wait_recv()
                cp = _remote(blk, blk, send.at[i, 3 + j], recv.at[i, 3 + j], sib)
                cp.start()
                passed.append(cp)
        for i in range(n):
            for j, (cx, cy) in enumerate(chips):
                blk = outs[i].at[2 * cx + cy, 1 - c]
                _remote(blk, blk, send.at[i, 3 + j], recv.at[i, 3 + j], sib).wait_recv()
        for cp in first + passed:
            cp.wait_send()
        for cp in own:
            cp.wait()

    return pl.pallas_call(
        body, name=name, in_specs=[ANY] * n, out_specs=[ANY] * n,
        out_shape=[jax.ShapeDtypeStruct((N_CHIPS,) + a.shape, a.dtype) for a in arrs],
        scratch_shapes=[pltpu.SemaphoreType.DMA((n, 6)), pltpu.SemaphoreType.DMA((n, 6)), pltpu.SemaphoreType.DMA((n,))],
    )(*arrs)


def swap_layers(name, arrs):
    n = len(arrs)

    def body(*refs):
        ins, outs = refs[:n], refs[n:2 * n]
        send, recv = refs[2 * n:]
        x, y, c = _place()
        cps = [_remote(ins[i].at[1 - c], outs[i], send.at[i], recv.at[i], (x, y, 1 - c)) for i in range(n)]
        for cp in cps:
            cp.start()
        for cp in cps:
            cp.wait()

    return pl.pallas_call(
        body, name=name, in_specs=[ANY] * n, out_specs=[ANY] * n,
        out_shape=[jax.ShapeDtypeStruct(a.shape[1:], a.dtype) for a in arrs],
        scratch_shapes=[pltpu.SemaphoreType.DMA((n,)), pltpu.SemaphoreType.DMA((n,))],
    )(*arrs)


def scatter_chips(name, arrs):
    n = len(arrs)

    def body(*refs):
        ins, outs = refs[:n], refs[n:2 * n]
        send, recv, loc = refs[2 * n:]
        x, y, c = _place()
        me = 2 * x + y
        chips = _other_chips(x, y)
        own = [pltpu.make_async_copy(ins[i].at[me], outs[i].at[me], loc.at[i]) for i in range(n)]
        for cp in own:
            cp.start()
        cps = [_remote(ins[i].at[2 * cx + cy], outs[i].at[me], send.at[i, j], recv.at[i, j], (cx, cy, c))
               for i in range(n) for j, (cx, cy) in enumerate(chips)]
        for cp in cps:
            cp.start()
        for i in range(n):
            for j, (cx, cy) in enumerate(chips):
                blk = outs[i].at[2 * cx + cy]
                _remote(blk, blk, send.at[i, j], recv.at[i, j], (x, y, c)).wait_recv()
        for cp in cps:
            cp.wait_send()
        for cp in own:
            cp.wait()

    return pl.pallas_call(
        body, name=name, in_specs=[ANY] * n, out_specs=[ANY] * n,
        out_shape=[jax.ShapeDtypeStruct(a.shape, a.dtype) for a in arrs],
        scratch_shapes=[pltpu.SemaphoreType.DMA((n, 3)), pltpu.SemaphoreType.DMA((n, 3)), pltpu.SemaphoreType.DMA((n,))],
    )(*arrs)


def share_layers(name, arrs):
    n = len(arrs)

    def body(*refs):
        ins, outs = refs[:n], refs[n:2 * n]
        send, recv, loc = refs[2 * n:]
        x, y, c = _place()
        own = [pltpu.make_async_copy(ins[i], outs[i].at[c], loc.at[i]) for i in range(n)]
        for cp in own:
            cp.start()
        cps = [_remote(ins[i], outs[i].at[c], send.at[i], recv.at[i], (x, y, 1 - c)) for i in range(n)]
        for cp in cps:
            cp.start()
        for i in range(n):
            blk = outs[i].at[1 - c]
            _remote(blk, blk, send.at[i], recv.at[i], (x, y, c)).wait_recv()
        for cp in cps:
            cp.wait_send()
        for cp in own:
            cp.wait()

    return pl.pallas_call(
        body, name=name, in_specs=[ANY] * n, out_specs=[ANY] * n,
        out_shape=[jax.ShapeDtypeStruct((2,) + a.shape, a.dtype) for a in arrs],
        scratch_shapes=[pltpu.SemaphoreType.DMA((n,)), pltpu.SemaphoreType.DMA((n,)), pltpu.SemaphoreType.DMA((n,))],
    )(*arrs)


def all_reduce_small(name, packed):
    r = packed.shape[0]

    def body(in_ref, out_ref, buf, send, recv):
        x, y, c = _place()
        me = 4 * x + 2 * y + c
        buf[me] = in_ref[...]
        flips = [(fx, fy, fc) for fx in (0, 1) for fy in (0, 1) for fc in (0, 1) if (fx, fy, fc) != (0, 0, 0)]
        peers = [((x + fx) % 2, (y + fy) % 2, (c + fc) % 2) for fx, fy, fc in flips]
        cps = [_remote(in_ref, buf.at[me], send.at[k], recv.at[k], peer) for k, peer in enumerate(peers)]
        for cp in cps:
            cp.start()
        for k, (px, py, pc) in enumerate(peers):
            blk = buf.at[4 * px + 2 * py + pc]
            _remote(blk, blk, send.at[k], recv.at[k], (x, y, c)).wait_recv()
        for cp in cps:
            cp.wait_send()
        acc = buf[0]
        for d in range(1, N_DEV):
            acc = acc + buf[d]
        out_ref[...] = acc

    vm = pl.BlockSpec(memory_space=pltpu.VMEM)
    return pl.pallas_call(
        body, name=name, in_specs=[vm], out_specs=vm, out_shape=jax.ShapeDtypeStruct(packed.shape, F32),
        scratch_shapes=[pltpu.VMEM((N_DEV, r, LANE), F32), pltpu.SemaphoreType.DMA((N_DEV - 1,)),
                        pltpu.SemaphoreType.DMA((N_DEV - 1,))],
    )(packed)


def add_own_layer(name, g, a, c_idx, tr):
    _, nch, r, cc = g.shape
    tr = min(tr, r)

    def body(c_ref, g_ref, a_ref, o_ref):
        o_ref[...] = g_ref[0] + a_ref[...]

    return pl.pallas_call(
        body, name=name,
        grid_spec=pltpu.PrefetchScalarGridSpec(
            num_scalar_prefetch=1, grid=(nch, r // tr),
            in_specs=[pl.BlockSpec((1, 1, tr, cc), lambda j, i, c_ref: (c_ref[0], j, i, 0)),
                      pl.BlockSpec((1, tr, cc), lambda j, i, c_ref: (j, i, 0))],
            out_specs=pl.BlockSpec((1, tr, cc), lambda j, i, c_ref: (j, i, 0))),
        out_shape=jax.ShapeDtypeStruct(a.shape, F32),
        compiler_params=_cparams(("parallel", "parallel")),
    )(c_idx, g, a)


def sum_chips(name, b, tr):
    nch, r, cc = b.shape
    tr = min(tr, r)

    def body(b_ref, o_ref):
        acc = b_ref[0]
        for k in range(1, nch):
            acc = acc + b_ref[k]
        o_ref[...] = acc

    return pl.pallas_call(
        body, name=name, grid=(r // tr,),
        in_specs=[pl.BlockSpec((nch, tr, cc), lambda i: (0, i, 0))],
        out_specs=pl.BlockSpec((tr, cc), lambda i: (i, 0)),
        out_shape=jax.ShapeDtypeStruct((r, cc), F32),
        compiler_params=_cparams(("parallel",)),
    )(b)


def reduce_scatter_grads(arrs, c_idx):
    tags = range(len(arrs))
    recv = swap_layers("rs_swap_layers", arrs)
    part = [add_own_layer(f"rs_add_sibling_{t}", g, a, c_idx, 128) for t, g, a in zip(tags, arrs, recv)]
    got = scatter_chips("rs_scatter_chips", part)
    red = [sum_chips(f"rs_sum_chips_{t}", b, 128) for t, b in zip(tags, got)]
    return share_layers("rs_share_layers", red)


def adamw(name, w, g, m, v, tb):
    r, cc = w.shape
    tb = min(tb, r)
    assert r % tb == 0

    def body(w_ref, g_ref, m_ref, v_ref, d_ref, mo_ref, vo_ref):
        g = g_ref[...]
        m = ADAM_B1 * m_ref[...] + (1.0 - ADAM_B1) * g
        v = ADAM_B2 * v_ref[...] + (1.0 - ADAM_B2) * jnp.square(g)
        m_hat = m / (1.0 - ADAM_B1 ** ADAM_STEP)
        v_hat = v / (1.0 - ADAM_B2 ** ADAM_STEP)
        d_ref[...] = -ADAM_LR * (m_hat / (jnp.sqrt(v_hat) + ADAM_EPS) + ADAM_WD * w_ref[...])
        mo_ref[...] = m
        vo_ref[...] = v

    spec = pl.BlockSpec((tb, cc), lambda i: (i, 0))
    return pl.pallas_call(
        body, name=name, grid=(r // tb,), in_specs=[spec] * 4, out_specs=[spec] * 3,
        out_shape=[jax.ShapeDtypeStruct((r, cc), F32)] * 3,
        compiler_params=_cparams(("parallel",)),
    )(w, g, m, v)


def _pack(arrs):
    flat = jnp.concatenate([a.reshape(-1).astype(F32) for a in arrs])
    pad = (-flat.shape[0]) % (8 * LANE)
    return jnp.pad(flat, (0, pad)).reshape(-1, LANE)


def _unpack(packed, shapes):
    flat = packed.reshape(-1)
    out, off = [], 0
    for s in shapes:
        size = math.prod(s)
        out.append(flat[off:off + size].reshape(s))
        off += size
    return out


BIG = ('w_in', 'w_out', 'b_pw_w')
SMALL = tuple(k for k in WEIGHTS if k not in BIG)
CHIP_SHARDED_SMALL = {'b_conv_w': 2, 'c_conv_w': 2}


def kernel(x, positions, norm_w, w_in, q_norm_w, k_norm_w, sinks, b_conv_w, b_conv_b, b_ln_w, b_ln_b, b_pw_w, b_pw_b, c_conv_w, c_a_log, c_dt_bias, c_onorm_w, w_out, loss_target, m_norm_w, m_w_in, m_q_norm_w, m_k_norm_w, m_sinks, m_b_conv_w, m_b_conv_b, m_b_ln_w, m_b_ln_b, m_b_pw_w, m_b_pw_b, m_c_conv_w, m_c_a_log, m_c_dt_bias, m_c_onorm_w, m_w_out, v_norm_w, v_w_in, v_q_norm_w, v_k_norm_w, v_sinks, v_b_conv_w, v_b_conv_b, v_b_ln_w, v_b_ln_b, v_b_pw_w, v_b_pw_b, v_c_conv_w, v_c_a_log, v_c_dt_bias, v_c_onorm_w, v_w_out):
    cfg = Cfg(x.shape[-1], x.shape[-2])
    w = dict(norm_w=norm_w, w_in=w_in, q_norm_w=q_norm_w, k_norm_w=k_norm_w, sinks=sinks, b_conv_w=b_conv_w,
             b_conv_b=b_conv_b, b_ln_w=b_ln_w, b_ln_b=b_ln_b, b_pw_w=b_pw_w, b_pw_b=b_pw_b, c_conv_w=c_conv_w,
             c_a_log=c_a_log, c_dt_bias=c_dt_bias, c_onorm_w=c_onorm_w, w_out=w_out)
    m = dict(norm_w=m_norm_w, w_in=m_w_in, q_norm_w=m_q_norm_w, k_norm_w=m_k_norm_w, sinks=m_sinks,
             b_conv_w=m_b_conv_w, b_conv_b=m_b_conv_b, b_ln_w=m_b_ln_w, b_ln_b=m_b_ln_b, b_pw_w=m_b_pw_w,
             b_pw_b=m_b_pw_b, c_conv_w=m_c_conv_w, c_a_log=m_c_a_log, c_dt_bias=m_c_dt_bias, c_onorm_w=m_c_onorm_w,
             w_out=m_w_out)
    v = dict(norm_w=v_norm_w, w_in=v_w_in, q_norm_w=v_q_norm_w, k_norm_w=v_k_norm_w, sinks=v_sinks,
             b_conv_w=v_b_conv_w, b_conv_b=v_b_conv_b, b_ln_w=v_b_ln_w, b_ln_b=v_b_ln_b, b_pw_w=v_b_pw_w,
             b_pw_b=v_b_pw_b, c_conv_w=v_c_conv_w, c_a_log=v_c_a_log, c_dt_bias=v_c_dt_bias, c_onorm_w=v_c_onorm_w,
             w_out=v_w_out)
    chip = 2 * lax.axis_index("x") + lax.axis_index("y")
    c_idx = lax.axis_index("c").astype(jnp.int32).reshape(1)

    g_in, g_out, g_pw, g_bcw, g_ccw = all_gather_chips(
        "gather_weights", [w_in.astype(BF16), w_out.astype(BF16), b_pw_w, b_conv_w, c_conv_w])
    full = dict(w)
    full['b_pw_w'] = jnp.stack([g_pw[:, l].reshape(cfg.BW, cfg.BW) for l in range(DEPTH)])
    full['b_conv_w'] = jnp.stack([jnp.concatenate(list(g_bcw[:, l]), axis=1) for l in range(DEPTH)])
    full['c_conv_w'] = jnp.stack([jnp.concatenate(list(g_ccw[:, l]), axis=1) for l in range(DEPTH)])
    wps = [permute_w_in(cfg, jnp.concatenate(list(g_in[:, l]), axis=1)) for l in range(DEPTH)]
    wos = [g_out[:, l].reshape(cfg.D, cfg.D) for l in range(DEPTH)]

    loss_local, grad_x, grads = local_step(cfg, x, positions, full, wps, wos, loss_target)
    loss = lax.psum(loss_local, ("x", "y", "c"))

    shard_cols = cfg.IN_COLS // N_CHIPS
    p_in = jnp.stack([unpermute_w_in(cfg, grads[l]['w_in_perm']).reshape(cfg.D, N_CHIPS, shard_cols).transpose(1, 0, 2)
                      for l in range(DEPTH)])
    p_out = jnp.stack([grads[l]['w_out'].reshape(N_CHIPS, cfg.D // N_CHIPS, cfg.D) for l in range(DEPTH)])
    p_pw = jnp.stack([grads[l]['b_pw_w'].reshape(N_CHIPS, cfg.BW // N_CHIPS, cfg.BW) for l in range(DEPTH)])
    r_in, r_out, r_pw = reduce_scatter_grads([p_in, p_out, p_pw], c_idx)
    small_parts = [jnp.stack([grads[l][k] for l in range(DEPTH)]) for k in SMALL]
    small_red = _unpack(all_reduce_small("all_reduce_small", _pack(small_parts)), [a.shape for a in small_parts])
    g = dict(w_in=r_in, w_out=r_out, b_pw_w=r_pw)
    for k, a in zip(SMALL, small_red):
        if k in CHIP_SHARDED_SMALL:
            ax = CHIP_SHARDED_SMALL[k]
            width = a.shape[ax] // N_CHIPS
            a = lax.dynamic_slice_in_dim(a, chip * width, width, axis=ax)
        g[k] = a

    delta, new_m, new_v = {}, {}, {}
    for k in BIG:
        cols = w[k].shape[-1]
        outs = adamw(f"adamw_{k}", w[k].reshape(-1, cols), g[k].reshape(-1, cols), m[k].reshape(-1, cols),
                     v[k].reshape(-1, cols), 128)
        delta[k], new_m[k], new_v[k] = [o.reshape(w[k].shape) for o in outs]
    shapes = [w[k].shape for k in SMALL]
    outs = adamw("adamw_small", _pack([w[k] for k in SMALL]), _pack([g[k] for k in SMALL]),
                 _pack([m[k] for k in SMALL]), _pack([v[k] for k in SMALL]), 4096)
    for name_, packed in zip(('delta', 'm', 'v'), outs):
        for k, a in zip(SMALL, _unpack(packed, shapes)):
            {'delta': delta, 'm': new_m, 'v': new_v}[name_][k] = a
    return (loss, grad_x, *[g[k] for k in WEIGHTS], *[delta[k] for k in WEIGHTS], *[new_m[k] for k in WEIGHTS],
            *[new_v[k] for k in WEIGHTS])
```

```python
import functools
import math

import numpy as np
import jax
import jax.numpy as jnp
from jax import lax
from jax.experimental import pallas as pl
from jax.experimental.pallas import tpu as pltpu

F32 = jnp.float32
BF16 = jnp.bfloat16
HI = lax.Precision.HIGHEST
MESH = pl.DeviceIdType.MESH

DEPTH = 2
A_HEAD = 64
A_GROUP = 3
ATTN_BLOCK = 128
ROT_DIM = 16
ROPE_THETA = 500000.0
B_CONV = 31
B_HALO = 32
C_HEAD = 128
C_CONV = 4
C_HALO = 8
CHUNK = 64
EPS = 1e-6
LANE = 128

ADAM_LR = 0.001
ADAM_B1 = 0.9
ADAM_B2 = 0.999
ADAM_EPS = 1e-08
ADAM_WD = 0.01
ADAM_STEP = 10

VMEM_LIMIT = 56 * 1024 * 1024

WEIGHTS = ['norm_w', 'w_in', 'q_norm_w', 'k_norm_w', 'sinks', 'b_conv_w', 'b_conv_b', 'b_ln_w', 'b_ln_b',
           'b_pw_w', 'b_pw_b', 'c_conv_w', 'c_a_log', 'c_dt_bias', 'c_onorm_w', 'w_out']


class Cfg:
    def __init__(self, d_model=2048, seq=2048):
        self.D = d_model
        self.T = seq
        self.AW = 3 * d_model // 8
        self.AQH = self.AW // A_HEAD
        self.AKH = self.AQH // A_GROUP
        self.AKW = self.AKH * A_HEAD
        self.BW = d_model // 4
        self.CH = (d_model - self.AW - self.BW) // C_HEAD
        self.CW = self.CH * C_HEAD
        AW, AKW, BW, CW, CH = self.AW, self.AKW, self.BW, self.CW, self.CH
        orig = [('qa', AW), ('ka', AKW), ('va', AKW), ('za', AW), ('ub', 2 * BW), ('zb', BW),
                ('qc', CW), ('kc', CW), ('vc', CW), ('bc', CH), ('ac', CH), ('zc', CW)]
        self.orig = {}
        off = 0
        for n, w in orig:
            self.orig[n] = (off, w)
            off += w
        self.IN_COLS = off
        order = ['qa', 'za', 'qc', 'kc', 'vc', 'zc', 'ka', 'va', 'ub', 'zb', 'bc', 'ac']
        self.order = order
        self.g = {}
        off = 0
        for n in order:
            w = self.orig[n][1]
            wp = LANE if n in ('bc', 'ac') else w
            assert off % wp == 0, (n, off, wp)
            self.g[n] = (off, wp)
            off += wp
        self.WP = off

    def blk(self, name):
        off, w = self.g[name]
        return off // w


def _cparams(sem, vmem=VMEM_LIMIT):
    return pltpu.CompilerParams(dimension_semantics=sem, vmem_limit_bytes=vmem)


def _silu(x):
    return x * jax.nn.sigmoid(x)


def _norm_rope(xh, w, cos, sin):
    y = xh * lax.rsqrt(jnp.mean(xh * xh, axis=-1, keepdims=True) + EPS) * w
    half = ROT_DIM // 2
    lane = lax.broadcasted_iota(jnp.int32, y.shape, 1)
    rot = jnp.where(lane < half, -jnp.roll(y, -half, axis=1), jnp.roll(y, half, axis=1))
    return y * cos + rot * sin


def attn_block(cfg, first, q, za, kc, vc, cosc, sinc, kp, vp, cosp, sinp, qnw, knw, sinks_row):
    blk = ATTN_BLOCK
    qi = lax.broadcasted_iota(jnp.int32, (blk, 2 * blk), 0)
    kj = lax.broadcasted_iota(jnp.int32, (blk, 2 * blk), 1)
    dist = qi + blk - kj
    valid = (dist >= 0) & (dist < blk) & (jnp.logical_not(first) | (kj >= blk))
    cos2 = jnp.concatenate([cosp, cosc], axis=0)
    sin2 = jnp.concatenate([sinp, sinc], axis=0)
    outs = []
    for hk in range(cfg.AKH):
        sl = slice(A_HEAD * hk, A_HEAD * (hk + 1))
        k2 = jnp.concatenate([kp[:, sl], kc[:, sl]], axis=0)
        v2 = jnp.concatenate([vp[:, sl], vc[:, sl]], axis=0)
        k2 = _norm_rope(k2, knw, cos2, sin2).astype(BF16)
        v2 = v2.astype(BF16)
        for g in range(A_GROUP):
            hq = hk * A_GROUP + g
            ql = slice(A_HEAD * hq, A_HEAD * (hq + 1))
            qh = _norm_rope(q[:, ql], qnw, cosc, sinc).astype(BF16)
            s = lax.dot_general(qh, k2, (((1,), (1,)), ((), ())), preferred_element_type=F32) * (A_HEAD ** -0.5)
            s = jnp.where(valid, s, -1e30)
            sink = sinks_row[:, A_HEAD * hq:A_HEAD * hq + 1]
            m = jnp.maximum(jnp.max(s, axis=-1, keepdims=True), sink)
            e = jnp.exp(s - m)
            den = jnp.sum(e, axis=-1, keepdims=True) + jnp.exp(sink - m)
            p = (e / den).astype(BF16)
            o = jnp.dot(p, v2, preferred_element_type=F32)
            outs.append(o * _silu(za[:, ql]))
    return (jnp.concatenate(outs, axis=1),)


def conv_block(cfg, first, u, zb, uh, cw, cb, lw, lb, pw, pb):
    BW = cfg.BW
    tb = u.shape[0]
    uu = jnp.concatenate([uh, u], axis=0)
    h = uu[:, :BW] * jax.nn.sigmoid(uu[:, BW:])
    row = lax.broadcasted_iota(jnp.int32, h.shape, 0)
    h = jnp.where(first & (row < B_HALO), 0.0, h)
    acc = jnp.zeros((tb, BW), F32) + cb
    base = B_HALO - (B_CONV - 1)
    for k in range(B_CONV):
        acc = acc + cw[k:k + 1, :] * h[base + k:base + k + tb, :]
    mu = jnp.mean(acc, axis=-1, keepdims=True)
    var = jnp.mean(jnp.square(acc - mu), axis=-1, keepdims=True)
    y = (acc - mu) * lax.rsqrt(var + EPS) * lw + lb
    s = _silu(y)
    o = jnp.dot(s.astype(BF16), pw.astype(BF16), preferred_element_type=F32) + pb
    return (o * _silu(zb),)


def gdn_prep_block(cfg, first, xq, xk, xv, braw, araw, hq, hk, hv, cw, alog, dtb):
    CW = cfg.CW
    tb = xq.shape[0]
    outs = []
    for idx, (x, xh) in enumerate(((xq, hq), (xk, hk), (xv, hv))):
        xx = jnp.concatenate([jnp.where(first, 0.0, xh), x], axis=0)
        w = cw[:, idx * CW:(idx + 1) * CW]
        acc = jnp.zeros((tb, CW), F32)
        base = C_HALO - (C_CONV - 1)
        for k in range(C_CONV):
            acc = acc + w[k:k + 1, :] * xx[base + k:base + k + tb, :]
        y = _silu(acc)
        if idx < 2:
            parts = []
            for h in range(cfg.CH):
                yh = y[:, C_HEAD * h:C_HEAD * (h + 1)]
                parts.append(yh * lax.rsqrt(jnp.sum(yh * yh, axis=-1, keepdims=True) + EPS))
            y = jnp.concatenate(parts, axis=1)
        outs.append(y)
    beta = jax.nn.sigmoid(braw)
    g = -jnp.exp(alog) * jax.nn.softplus(araw + dtb)
    return outs[0], outs[1], outs[2], g, beta


def _inverse_unit_lower(low, eye):
    pw = low
    inv = eye - low
    for _ in range(5):
        pwb = pw.astype(BF16)
        pw = jnp.dot(pwb, pwb, preferred_element_type=F32)
        inv = inv + jnp.dot(inv.astype(BF16), pw.astype(BF16), preferred_element_type=F32)
    ax = inv + jnp.dot(low, inv, precision=HI, preferred_element_type=F32)
    return inv + jnp.dot(inv, eye - ax, precision=HI, preferred_element_type=F32)


@jax.custom_vjp
def _saved_inverse(low, inv):
    return inv


def _saved_inverse_fwd(low, inv):
    return inv, inv


def _saved_inverse_bwd(inv, d):
    t = lax.dot_general(inv, d, (((0,), (0,)), ((), ())), precision=HI, preferred_element_type=F32)
    dlow = -lax.dot_general(t, inv, (((1,), (1,)), ((), ())), precision=HI, preferred_element_type=F32)
    return dlow, jnp.zeros_like(inv)


_saved_inverse.defvjp(_saved_inverse_fwd, _saved_inverse_bwd)


def gdn_intra_block(cfg, first, qn, kn, v, g, beta, inv_saved=None):
    c = CHUNK
    i = lax.broadcasted_iota(jnp.int32, (c, c), 0)
    j = lax.broadcasted_iota(jnp.int32, (c, c), 1)
    incl = i >= j
    strict = i > j
    eye = (i == j).astype(F32)
    gc_all = jnp.dot(incl.astype(F32), g, precision=HI, preferred_element_type=F32)
    us, ws, qgs, kds, intras, invs = [], [], [], [], [], []
    for h in range(cfg.CH):
        sl = slice(C_HEAD * h, C_HEAD * (h + 1))
        gch = gc_all[:, h:h + 1]
        bh = beta[:, h:h + 1]
        a = jnp.broadcast_to(gch, (c, c))
        diff = jnp.where(incl, a - a.T, 0.0)
        decay = jnp.where(incl, jnp.exp(diff), 0.0)
        q = qn[:, sl] * (C_HEAD ** -0.5)
        k = kn[:, sl]
        kb = k * bh
        kkT = lax.dot_general(kb, k, (((1,), (1,)), ((), ())), precision=HI, preferred_element_type=F32)
        low = jnp.where(strict, kkT * decay, 0.0)
        if inv_saved is None:
            inv = _inverse_unit_lower(low, eye)
        else:
            inv = _saved_inverse(low, inv_saved[:, c * h:c * (h + 1)])
        invs.append(inv)
        rhs = jnp.concatenate([v[:, sl] * bh, kb * jnp.exp(gch)], axis=1)
        sol = jnp.dot(inv, rhs, precision=HI, preferred_element_type=F32)
        us.append(sol[:, :C_HEAD])
        ws.append(sol[:, C_HEAD:])
        qkT = lax.dot_general(q, k, (((1,), (1,)), ((), ())), precision=HI, preferred_element_type=F32)
        intras.append(jnp.where(incl, qkT * decay, 0.0))
        qgs.append(q * jnp.exp(gch))
        kds.append(k * jnp.exp(gch[c - 1:c, :] - gch))
    glast = jnp.broadcast_to(gc_all[c - 1:c, :], gc_all.shape)
    cat = lambda xs: jnp.concatenate(xs, axis=1)
    outs = (cat(us), cat(ws), cat(qgs), cat(kds), cat(intras), glast)
    return outs + (cat(invs),) if inv_saved is None else outs


def gdn_state_step(S, u, w, qg, kd, intra, glast_h):
    v_new = u - jnp.dot(w, S, precision=HI, preferred_element_type=F32)
    o = (jnp.dot(qg, S, precision=HI, preferred_element_type=F32)
         + jnp.dot(intra, v_new, precision=HI, preferred_element_type=F32))
    S_next = S * jnp.exp(glast_h) + lax.dot_general(kd, v_new, (((0,), (0,)), ((), ())), precision=HI,
                                                     preferred_element_type=F32)
    return o, S_next


def gdn_out_block(cfg, first, o, zc, onw):
    parts = []
    for h in range(cfg.CH):
        sl = slice(C_HEAD * h, C_HEAD * (h + 1))
        oh = o[:, sl]
        y = oh * lax.rsqrt(jnp.mean(oh * oh, axis=-1, keepdims=True) + EPS) * onw
        parts.append(y * _silu(zc[:, sl]))
    return (jnp.concatenate(parts, axis=1),)


def rms_block(x, nw):
    return x * lax.rsqrt(jnp.mean(x * x, axis=-1, keepdims=True) + EPS) * nw


class Row:
    def __init__(self, arr, width, colblk=0, grad=None):
        self.arr, self.width, self.colblk, self.grad = arr, width, colblk, grad


class Halo:
    def __init__(self, arr, width, colblk, hr, tie=None):
        self.arr, self.width, self.colblk, self.hr, self.tie = arr, width, colblk, hr, tie


def _row_specs(tb, rows, halos, params, pos):
    specs = [pl.BlockSpec((tb, r.width), lambda i, cb=r.colblk: (pos(i), cb)) for r in rows]
    specs += [pl.BlockSpec((h.hr, h.width),
                           lambda i, cb=h.colblk, m=tb // h.hr: (jnp.maximum(pos(i) * m - 1, 0), cb))
              for h in halos]
    specs += [pl.BlockSpec(p.shape, lambda i: (0, 0)) for p in params]
    return specs


def rb_fwd(name, fn, n, tb, bps, rows, halos, params, outs):
    nr, nh, npar = len(rows), len(halos), len(params)

    def body(*refs):
        ins = refs[:nr + nh + npar]
        o_refs = refs[nr + nh + npar:]
        first = (pl.program_id(0) % bps) == 0
        res = fn(first, *[r[...] for r in ins])
        for ref, val in zip(o_refs, res):
            ref[...] = val.astype(ref.dtype)

    return pl.pallas_call(
        body, name=name, grid=(n // tb,),
        in_specs=_row_specs(tb, rows, halos, params, lambda i: i),
        out_specs=[pl.BlockSpec((tb, w), lambda i: (i, 0)) for w, _ in outs],
        out_shape=[jax.ShapeDtypeStruct((n, w), dt) for w, dt in outs],
        compiler_params=_cparams(("parallel",)),
    )(*[r.arr for r in rows], *[h.arr for h in halos], *params)


def rb_bwd(name, fn, n, tb, bps, rows, halos, params, douts, param_grads):
    nr, nh, npar, nd = len(rows), len(halos), len(params), len(douts)
    nblk = n // tb
    grow = [k for k, r in enumerate(rows) if r.grad is not None]
    ghalo = [k for k, h in enumerate(halos) if h.tie is not None]
    gpar = [k for k, f in enumerate(param_grads) if f]
    pos = lambda i: nblk - 1 - i

    def body(*refs):
        ins = refs[:nr + nh + npar]
        d_refs = refs[nr + nh + npar:nr + nh + npar + nd]
        rest = refs[nr + nh + npar + nd:]
        grow_refs = rest[:len(grow)]
        gpar_refs = rest[len(grow):len(grow) + len(gpar)]
        carry_refs = rest[len(grow) + len(gpar):]
        i = pl.program_id(0)
        first = (pos(i) % bps) == 0
        vals = [r[...] for r in ins]
        diff_idx = grow + [nr + k for k in ghalo] + [nr + nh + k for k in gpar]

        def f(*dargs):
            full = list(vals)
            for k, a in zip(diff_idx, dargs):
                full[k] = a
            return fn(first, *full)

        res, vjp = jax.vjp(f, *[vals[k] for k in diff_idx])
        grads = vjp(tuple(d[...].astype(r.dtype) for d, r in zip(d_refs, res)))
        g_rows = list(grads[:len(grow)])
        g_halos = grads[len(grow):len(grow) + len(ghalo)]
        g_pars = grads[len(grow) + len(ghalo):]

        @pl.when(i == 0)
        def _():
            for c in carry_refs:
                c[...] = jnp.zeros_like(c)
            for p in gpar_refs:
                p[...] = jnp.zeros_like(p)

        for k, ref in enumerate(grow_refs):
            ref[...] = g_rows[k].astype(ref.dtype)
        for ci, hk in enumerate(ghalo):
            h = halos[hk]
            k = grow.index(h.tie)
            tail = g_rows[k][tb - h.hr:, :] + carry_refs[ci][...]
            grow_refs[k][tb - h.hr:, :] = tail.astype(grow_refs[k].dtype)
            carry_refs[ci][...] = g_halos[ci]
        for ref, gp in zip(gpar_refs, g_pars):
            ref[...] += gp

    out_specs = [pl.BlockSpec((tb, rows[k].width), lambda i: (pos(i), 0)) for k in grow]
    out_specs += [pl.BlockSpec(params[k].shape, lambda i: (0, 0)) for k in gpar]
    out_shape = [jax.ShapeDtypeStruct((n, rows[k].width), rows[k].grad) for k in grow]
    out_shape += [jax.ShapeDtypeStruct(params[k].shape, F32) for k in gpar]
    in_specs = _row_specs(tb, rows, halos, params, pos)
    in_specs += [pl.BlockSpec((tb, d.shape[1]), lambda i: (pos(i), 0)) for d in douts]
    return pl.pallas_call(
        body, name=name, grid=(nblk,), in_specs=in_specs, out_specs=out_specs, out_shape=out_shape,
        scratch_shapes=[pltpu.VMEM((halos[k].hr, halos[k].width), F32) for k in ghalo],
        compiler_params=_cparams(("arbitrary",)),
    )(*[r.arr for r in rows], *[h.arr for h in halos], *params, *douts)


_DIMS = {'nn': (((1,), (0,)), ((), ())), 'nt': (((1,), (1,)), ((), ())), 'tn': (((0,), (0,)), ((), ()))}


def matmul(name, a, b, mode, tm, tn, tk, out_dtype=F32, add=None):
    if mode == 'tn':
        K, M = a.shape
    else:
        M, K = a.shape
    N = b.shape[0] if mode == 'nt' else b.shape[1]
    tm, tn, tk = min(tm, M), min(tn, N), min(tk, K)
    assert M % tm == 0 and N % tn == 0 and K % tk == 0, (name, M, N, K, tm, tn, tk)
    nk = K // tk
    a_spec = pl.BlockSpec((tk, tm), lambda i, j, k: (k, i)) if mode == 'tn' else pl.BlockSpec((tm, tk), lambda i, j, k: (i, k))
    b_spec = pl.BlockSpec((tn, tk), lambda i, j, k: (j, k)) if mode == 'nt' else pl.BlockSpec((tk, tn), lambda i, j, k: (k, j))
    o_spec = pl.BlockSpec((tm, tn), lambda i, j, k: (i, j))
    has_add = add is not None

    def body(*refs):
        a_ref, b_ref = refs[0], refs[1]
        add_ref = refs[2] if has_add else None
        o_ref, acc = refs[-2], refs[-1]
        k = pl.program_id(2)

        @pl.when(k == 0)
        def _():
            acc[...] = jnp.zeros_like(acc)

        acc[...] += lax.dot_general(a_ref[...].astype(BF16), b_ref[...].astype(BF16), _DIMS[mode],
                                    preferred_element_type=F32)

        @pl.when(k == nk - 1)
        def _():
            r = acc[...]
            if has_add:
                r = r + add_ref[...]
            o_ref[...] = r.astype(o_ref.dtype)

    ins = [a, b] + ([add] if has_add else [])
    in_specs = [a_spec, b_spec] + ([o_spec] if has_add else [])
    return pl.pallas_call(
        body, name=name, grid=(M // tm, N // tn, nk), in_specs=in_specs, out_specs=o_spec,
        out_shape=jax.ShapeDtypeStruct((M, N), out_dtype),
        scratch_shapes=[pltpu.VMEM((tm, tn), F32)],
        compiler_params=_cparams(("parallel", "parallel", "arbitrary")),
    )(*ins)


def norm_in_proj(name, x, nw, wp, tm, tn):
    n, d = x.shape
    wpc = wp.shape[1]
    tm, tn = min(tm, n), min(tn, wpc)
    assert n % tm == 0 and wpc % tn == 0

    def body(x_ref, nw_ref, w_ref, p_ref, h_ref):
        @pl.when(pl.program_id(1) == 0)
        def _():
            h_ref[...] = rms_block(x_ref[...], nw_ref[...]).astype(BF16)

        p_ref[...] = jnp.dot(h_ref[...], w_ref[...], preferred_element_type=F32)

    return pl.pallas_call(
        body, name=name, grid=(n // tm, wpc // tn),
        in_specs=[pl.BlockSpec((tm, d), lambda i, j: (i, 0)), pl.BlockSpec((1, d), lambda i, j: (0, 0)),
                  pl.BlockSpec((d, tn), lambda i, j: (0, j))],
        out_specs=[pl.BlockSpec((tm, tn), lambda i, j: (i, j)), pl.BlockSpec((tm, d), lambda i, j: (i, 0))],
        out_shape=[jax.ShapeDtypeStruct((n, wpc), F32), jax.ShapeDtypeStruct((n, d), BF16)],
        compiler_params=_cparams(("parallel", "arbitrary")),
    )(x, nw, wp)


def norm_bwd(name, x, nw, dh, dres, tb):
    n, d = x.shape
    tb = min(tb, n)

    def body(x_ref, nw_ref, dh_ref, dres_ref, dx_ref, dnw_ref):
        @pl.when(pl.program_id(0) == 0)
        def _():
            dnw_ref[...] = jnp.zeros_like(dnw_ref)

        _, vjp = jax.vjp(rms_block, x_ref[...], nw_ref[...])
        dx, dnw = vjp(dh_ref[...])
        dx_ref[...] = dx + dres_ref[...]
        dnw_ref[...] += dnw

    row = pl.BlockSpec((tb, d), lambda i: (i, 0))
    par = pl.BlockSpec((1, d), lambda i: (0, 0))
    return pl.pallas_call(
        body, name=name, grid=(n // tb,), in_specs=[row, par, row, row], out_specs=[row, par],
        out_shape=[jax.ShapeDtypeStruct((n, d), F32), jax.ShapeDtypeStruct((1, d), F32)],
        compiler_params=_cparams(("arbitrary",)),
    )(x, nw, dh, dres)


def loss_grad(name, y, target, tb):
    n, d = y.shape
    tb = min(tb, n)

    def body(y_ref, t_ref, dy_ref, loss_ref):
        @pl.when(pl.program_id(0) == 0)
        def _():
            loss_ref[...] = jnp.zeros_like(loss_ref)

        err = y_ref[...] - t_ref[...]
        dy_ref[...] = err * (1.0 / d)
        part = 0.5 * jnp.sum(jnp.mean(err * err, axis=-1, keepdims=True), axis=0, keepdims=True)
        loss_ref[...] += jnp.broadcast_to(part, loss_ref.shape)

    row = pl.BlockSpec((tb, d), lambda i: (i, 0))
    dy, loss = pl.pallas_call(
        body, name=name, grid=(n // tb,), in_specs=[row, row],
        out_specs=[row, pl.BlockSpec((8, LANE), lambda i: (0, 0))],
        out_shape=[jax.ShapeDtypeStruct((n, d), F32), jax.ShapeDtypeStruct((8, LANE), F32)],
        compiler_params=_cparams(("arbitrary",)),
    )(y, target)
    return dy, loss[0, 0]


def rope_tables(name, pos_col, inv_freq_row):
    n = pos_col.shape[0]

    def body(p_ref, f_ref, c_ref, s_ref):
        ang = p_ref[...].astype(F32) * f_ref[...]
        lane = lax.broadcasted_iota(jnp.int32, ang.shape, 1)
        c_ref[...] = jnp.where(lane < ROT_DIM, jnp.cos(ang), 1.0)
        s_ref[...] = jnp.where(lane < ROT_DIM, jnp.sin(ang), 0.0)

    return pl.pallas_call(
        body, name=name, out_shape=[jax.ShapeDtypeStruct((n, A_HEAD), F32)] * 2,
    )(pos_col, inv_freq_row)


def gdn_scan_fwd(name, cfg, nseq, u, w, qg, kd, intra, glast):
    CH, CW, T = cfg.CH, cfg.CW, cfg.T
    nc = T // CHUNK

    def body(u_ref, w_ref, qg_ref, kd_ref, a_ref, gl_ref, o_ref, sin_ref, s_ref):
        @pl.when(pl.program_id(1) == 0)
        def _():
            s_ref[...] = jnp.zeros_like(s_ref)

        outs = []
        for h in range(CH):
            sl = slice(C_HEAD * h, C_HEAD * (h + 1))
            S = s_ref[h]
            sin_ref[0, h] = S
            o, S_next = gdn_state_step(S, u_ref[:, sl], w_ref[:, sl], qg_ref[:, sl], kd_ref[:, sl],
                                       a_ref[:, CHUNK * h:CHUNK * (h + 1)], gl_ref[0:1, h:h + 1])
            s_ref[h] = S_next
            outs.append(o)
        o_ref[...] = jnp.concatenate(outs, axis=1)

    row = lambda wd: pl.BlockSpec((CHUNK, wd), lambda b, c: (b * nc + c, 0))
    return pl.pallas_call(
        body, name=name, grid=(nseq, nc),
        in_specs=[row(CW), row(CW), row(CW), row(CW), row(CH * CHUNK), row(LANE)],
        out_specs=[row(CW), pl.BlockSpec((1, CH, C_HEAD, C_HEAD), lambda b, c: (b * nc + c, 0, 0, 0))],
        out_shape=[jax.ShapeDtypeStruct((nseq * T, CW), F32),
                   jax.ShapeDtypeStruct((nseq * nc, CH, C_HEAD, C_HEAD), F32)],
        scratch_shapes=[pltpu.VMEM((CH, C_HEAD, C_HEAD), F32)],
        compiler_params=_cparams(("parallel", "arbitrary")),
    )(u, w, qg, kd, intra, glast)


def gdn_scan_bwd(name, cfg, nseq, u, w, qg, kd, intra, glast, s_in, do):
    CH, CW, T = cfg.CH, cfg.CW, cfg.T
    nc = T // CHUNK

    def body(u_ref, w_ref, qg_ref, kd_ref, a_ref, gl_ref, sin_ref, do_ref,
             du_ref, dw_ref, dqg_ref, dkd_ref, da_ref, dgl_ref, ds_ref):
        @pl.when(pl.program_id(1) == 0)
        def _():
            ds_ref[...] = jnp.zeros_like(ds_ref)

        dus, dws, dqgs, dkds, das = [], [], [], [], []
        lane = lax.broadcasted_iota(jnp.int32, (CHUNK, LANE), 1)
        rowi = lax.broadcasted_iota(jnp.int32, (CHUNK, LANE), 0)
        dgl = jnp.zeros((CHUNK, LANE), F32)
        for h in range(CH):
            sl = slice(C_HEAD * h, C_HEAD * (h + 1))
            args = (sin_ref[0, h], u_ref[:, sl], w_ref[:, sl], qg_ref[:, sl], kd_ref[:, sl],
                    a_ref[:, CHUNK * h:CHUNK * (h + 1)], gl_ref[0:1, h:h + 1])
            _, vjp = jax.vjp(gdn_state_step, *args)
            dS, du, dw, dqg, dkd, da, dg = vjp((do_ref[:, sl], ds_ref[h]))
            ds_ref[h] = dS
            dus.append(du)
            dws.append(dw)
            dqgs.append(dqg)
            dkds.append(dkd)
            das.append(da)
            dgl = dgl + jnp.where((lane == h) & (rowi == 0), dg, 0.0)
        cat = lambda xs: jnp.concatenate(xs, axis=1)
        du_ref[...] = cat(dus)
        dw_ref[...] = cat(dws)
        dqg_ref[...] = cat(dqgs)
        dkd_ref[...] = cat(dkds)
        da_ref[...] = cat(das)
        dgl_ref[...] = dgl

    pos = lambda b, c: b * nc + nc - 1 - c
    row = lambda wd: pl.BlockSpec((CHUNK, wd), lambda b, c: (pos(b, c), 0))
    widths = [CW, CW, CW, CW, CH * CHUNK, LANE]
    return pl.pallas_call(
        body, name=name, grid=(nseq, nc),
        in_specs=[row(x) for x in widths]
        + [pl.BlockSpec((1, CH, C_HEAD, C_HEAD), lambda b, c: (pos(b, c), 0, 0, 0)), row(CW)],
        out_specs=[row(x) for x in widths],
        out_shape=[jax.ShapeDtypeStruct((nseq * T, x), F32) for x in widths],
        scratch_shapes=[pltpu.VMEM((CH, C_HEAD, C_HEAD), F32)],
        compiler_params=_cparams(("parallel", "arbitrary")),
    )(u, w, qg, kd, intra, glast, s_in, do)


def _tile(total, cap, unit=LANE):
    best = None
    for t in range(unit, min(cap, total) + 1, unit):
        if total % t == 0:
            best = t
    assert best is not None, (total, cap, unit)
    return best


def _pad_lanes(v, width=LANE):
    return jnp.pad(v.reshape(1, -1), ((0, 0), (0, width - v.shape[-1])))


def permute_w_in(cfg, w):
    parts = []
    for n in cfg.order:
        off, wd = cfg.orig[n]
        blk = w[:, off:off + wd]
        if cfg.g[n][1] != wd:
            blk = jnp.pad(blk, ((0, 0), (0, cfg.g[n][1] - wd)))
        parts.append(blk)
    return jnp.concatenate(parts, axis=1)


def unpermute_w_in(cfg, wp):
    parts = []
    for n, (off, wd) in cfg.orig.items():
        parts.append(wp[:, cfg.g[n][0]:cfg.g[n][0] + wd])
    return jnp.concatenate(parts, axis=1)


def _layer_params(cfg, prm, l):
    return dict(
        nw=prm['norm_w'][l].reshape(1, -1),
        qnw=prm['q_norm_w'][l].reshape(1, -1), knw=prm['k_norm_w'][l].reshape(1, -1),
        sinks_row=jnp.repeat(prm['sinks'][l], A_HEAD).reshape(1, -1),
        cw=prm['b_conv_w'][l], cb=prm['b_conv_b'][l].reshape(1, -1),
        lw=prm['b_ln_w'][l].reshape(1, -1), lb=prm['b_ln_b'][l].reshape(1, -1),
        pw=prm['b_pw_w'][l], pb=prm['b_pw_b'][l].reshape(1, -1),
        ccw=prm['c_conv_w'][l], alog=_pad_lanes(prm['c_a_log'][l]), dtb=_pad_lanes(prm['c_dt_bias'][l]),
        onw=prm['c_onorm_w'][l].reshape(1, -1),
    )


def _attn_io(cfg, p, cos, sin, grads):
    gq = BF16 if grads else None
    rows = [Row(p, cfg.AW, cfg.blk('qa'), gq), Row(p, cfg.AW, cfg.blk('za'), gq),
            Row(p, cfg.AKW, cfg.blk('ka'), gq), Row(p, cfg.AKW, cfg.blk('va'), gq),
            Row(cos, A_HEAD), Row(sin, A_HEAD)]
    halos = [Halo(p, cfg.AKW, cfg.blk('ka'), ATTN_BLOCK, 2 if grads else None),
             Halo(p, cfg.AKW, cfg.blk('va'), ATTN_BLOCK, 3 if grads else None),
             Halo(cos, A_HEAD, 0, ATTN_BLOCK), Halo(sin, A_HEAD, 0, ATTN_BLOCK)]
    return rows, halos


def _conv_io(cfg, p, grads):
    gq = BF16 if grads else None
    rows = [Row(p, 2 * cfg.BW, cfg.blk('ub'), gq), Row(p, cfg.BW, cfg.blk('zb'), gq)]
    halos = [Halo(p, 2 * cfg.BW, cfg.blk('ub'), B_HALO, 0 if grads else None)]
    return rows, halos


def _prep_io(cfg, p, grads):
    gq = BF16 if grads else None
    rows = [Row(p, cfg.CW, cfg.blk(n), gq) for n in ('qc', 'kc', 'vc')]
    rows += [Row(p, LANE, cfg.blk('bc'), gq), Row(p, LANE, cfg.blk('ac'), gq)]
    halos = [Halo(p, cfg.CW, cfg.blk(n), C_HALO, k if grads else None) for k, n in enumerate(('qc', 'kc', 'vc'))]
    return rows, halos


TB_CONV = 128
TB_PREP = 256
TB_OUT = 256


def layer_forward(cfg, l, x, lp, wp, wo, cos, sin):
    n = x.shape[0]
    nseq = n // cfg.T
    T = cfg.T
    p, h = norm_in_proj(f"in_proj_{l}", x, lp['nw'], wp, 512, _tile(cfg.WP, 1152))
    rows, halos = _attn_io(cfg, p, cos, sin, False)
    (oa,) = rb_fwd(f"attn_fwd_{l}", functools.partial(attn_block, cfg), n, ATTN_BLOCK, T // ATTN_BLOCK, rows, halos,
                   [lp['qnw'], lp['knw'], lp['sinks_row']], [(cfg.AW, BF16)])
    rows, halos = _conv_io(cfg, p, False)
    tbb = min(TB_CONV, T)
    (ob,) = rb_fwd(f"conv_fwd_{l}", functools.partial(conv_block, cfg), n, tbb, T // tbb, rows, halos,
                   [lp['cw'], lp['cb'], lp['lw'], lp['lb'], lp['pw'], lp['pb']], [(cfg.BW, BF16)])
    rows, halos = _prep_io(cfg, p, False)
    tbp = min(TB_PREP, T)
    qn, kn, v, g, beta = rb_fwd(f"gdn_prep_fwd_{l}", functools.partial(gdn_prep_block, cfg), n, tbp, T // tbp, rows,
                                halos, [lp['ccw'], lp['alog'], lp['dtb']],
                                [(cfg.CW, F32)] * 3 + [(LANE, F32)] * 2)
    intra_outs = rb_fwd(f"gdn_intra_fwd_{l}", functools.partial(gdn_intra_block, cfg), n, CHUNK, T // CHUNK,
                        [Row(qn, cfg.CW), Row(kn, cfg.CW), Row(v, cfg.CW), Row(g, LANE), Row(beta, LANE)], [], [],
                        [(cfg.CW, F32)] * 4 + [(cfg.CH * CHUNK, F32), (LANE, F32), (cfg.CH * CHUNK, F32)])
    intra_outs, inv = intra_outs[:6], intra_outs[6]
    o, s_in = gdn_scan_fwd(f"gdn_scan_fwd_{l}", cfg, nseq, *intra_outs)
    tbo = min(TB_OUT, T)
    (oc,) = rb_fwd(f"gdn_out_fwd_{l}", functools.partial(gdn_out_block, cfg), n, tbo, T // tbo,
                   [Row(o, cfg.CW), Row(p, cfg.CW, cfg.blk('zc'))], [], [lp['onw']], [(cfg.CW, BF16)])
    y = jnp.concatenate([oa, ob, oc], axis=1)
    x_next = matmul(f"out_proj_{l}", y, wo, 'nn', 512, 512, cfg.D, add=x)
    saved = dict(x=x, p=p, h=h, y=y, qn=qn, kn=kn, v=v, g=g, beta=beta, intra_outs=intra_outs, inv=inv, s_in=s_in, o=o)
    return x_next, saved


def layer_backward(cfg, l, dxn, sv, lp, wp, wo, cos, sin):
    n = dxn.shape[0]
    nseq = n // cfg.T
    T = cfg.T
    p = sv['p']
    AW, BW, CW = cfg.AW, cfg.BW, cfg.CW
    dy = matmul(f"dy_{l}", dxn, wo, 'nt', 512, 512, cfg.D)
    dwo = matmul(f"dwo_{l}", sv['y'], dxn, 'tn', 512, 512, 1024)
    doa, dob, doc = dy[:, :AW], dy[:, AW:AW + BW], dy[:, AW + BW:]
    tbo = min(TB_OUT, T)
    do, dzc, donw = rb_bwd(f"gdn_out_bwd_{l}", functools.partial(gdn_out_block, cfg), n, tbo, T // tbo,
                           [Row(sv['o'], CW, 0, F32), Row(p, CW, cfg.blk('zc'), BF16)], [], [lp['onw']], [doc], [True])
    dintra = gdn_scan_bwd(f"gdn_scan_bwd_{l}", cfg, nseq, *sv['intra_outs'], sv['s_in'], do)
    dqn, dkn, dv, dg, dbeta = rb_bwd(
        f"gdn_intra_bwd_{l}", functools.partial(gdn_intra_block, cfg), n, CHUNK, T // CHUNK,
        [Row(sv['qn'], CW, 0, F32), Row(sv['kn'], CW, 0, F32), Row(sv['v'], CW, 0, F32), Row(sv['g'], LANE, 0, F32),
         Row(sv['beta'], LANE, 0, F32), Row(sv['inv'], cfg.CH * CHUNK)], [], [], list(dintra), [])
    rows, halos = _prep_io(cfg, p, True)
    tbp = min(TB_PREP, T)
    dqc, dkc, dvc, dbc, dac, dccw, dalog, ddtb = rb_bwd(
        f"gdn_prep_bwd_{l}", functools.partial(gdn_prep_block, cfg), n, tbp, T // tbp, rows, halos,
        [lp['ccw'], lp['alog'], lp['dtb']], [dqn, dkn, dv, dg, dbeta], [True] * 3)
    rows, halos = _conv_io(cfg, p, True)
    tbb = min(TB_CONV, T)
    dub, dzb, dcw, dcb, dlw, dlb, dpw, dpb = rb_bwd(
        f"conv_bwd_{l}", functools.partial(conv_block, cfg), n, tbb, T // tbb, rows, halos,
        [lp['cw'], lp['cb'], lp['lw'], lp['lb'], lp['pw'], lp['pb']], [dob], [True] * 6)
    rows, halos = _attn_io(cfg, p, cos, sin, True)
    dqa, dza, dka, dva, dqnw, dknw, dsinks_row = rb_bwd(
        f"attn_bwd_{l}", functools.partial(attn_block, cfg), n, ATTN_BLOCK, T // ATTN_BLOCK, rows, halos,
        [lp['qnw'], lp['knw'], lp['sinks_row']], [doa], [True] * 3)
    dgroups = dict(qa=dqa, za=dza, qc=dqc, kc=dkc, vc=dvc, zc=dzc, ka=dka, va=dva, ub=dub, zb=dzb, bc=dbc, ac=dac)
    dp = jnp.concatenate([dgroups[k] for k in cfg.order], axis=1)
    dh = matmul(f"dh_{l}", dp, wp, 'nt', 512, 512, _tile(cfg.WP, 2304))
    dwp = matmul(f"dwp_{l}", sv['h'], dp, 'tn', 512, _tile(cfg.WP, 1152), 1024)
    dx, dnw = norm_bwd(f"norm_bwd_{l}", sv['x'], lp['nw'], dh, dxn, 256)
    grads = dict(
        norm_w=dnw[0], w_in_perm=dwp, q_norm_w=dqnw[0], k_norm_w=dknw[0],
        sinks=dsinks_row.reshape(cfg.AQH, A_HEAD)[:, 0],
        b_conv_w=dcw, b_conv_b=dcb[0], b_ln_w=dlw[0], b_ln_b=dlb[0], b_pw_w=dpw, b_pw_b=dpb[0],
        c_conv_w=dccw, c_a_log=dalog[0, :cfg.CH], c_dt_bias=ddtb[0, :cfg.CH], c_onorm_w=donw[0], w_out=dwo)
    return dx, grads


def local_step(cfg, x, positions, prm, wps, wos, target):
    nseq = x.shape[0]
    n = nseq * cfg.T
    x2 = x.reshape(n, cfg.D)
    inv_freq = ROPE_THETA ** (-np.arange(0, ROT_DIM, 2, dtype=np.float32) / ROT_DIM)
    freq_row = np.zeros((1, A_HEAD), np.float32)
    freq_row[0, :ROT_DIM] = np.concatenate([inv_freq, inv_freq])
    cos, sin = rope_tables("rope_tables", positions.reshape(n, 1), jnp.asarray(freq_row))
    lps = [_layer_params(cfg, prm, l) for l in range(DEPTH)]
    saved = []
    xl = x2
    for l in range(DEPTH):
        xl, sv = layer_forward(cfg, l, xl, lps[l], wps[l], wos[l], cos, sin)
        saved.append(sv)
    dx, loss = loss_grad("loss_grad", xl, target.reshape(n, cfg.D), 256)
    grads = [None] * DEPTH
    for l in reversed(range(DEPTH)):
        dx, grads[l] = layer_backward(cfg, l, dx, saved[l], lps[l], wps[l], wos[l], cos, sin)
    return loss, dx.reshape(x.shape), grads


ANY = pl.BlockSpec(memory_space=pl.ANY)
N_CHIPS = 4
N_DEV = 8


def _place():
    return lax.axis_index("x"), lax.axis_index("y"), lax.axis_index("c")


def _other_chips(x, y):
    return [(1 - x, y), (x, 1 - y), (1 - x, 1 - y)]


def _remote(src, dst, send, recv, to):
    return pltpu.make_async_remote_copy(src_ref=src, dst_ref=dst, send_sem=send, recv_sem=recv, device_id=to,
                                        device_id_type=MESH)


def all_gather_chips(name, arrs):
    n = len(arrs)

    def body(*refs):
        ins, outs = refs[:n], refs[n:2 * n]
        send, recv, loc = refs[2 * n:]
        x, y, c = _place()
        me = 2 * x + y
        chips = _other_chips(x, y)
        sib = (x, y, 1 - c)
        own = [pltpu.make_async_copy(ins[i], outs[i].at[me], loc.at[i]) for i in range(n)]
        for cp in own:
            cp.start()
        first = [_remote(ins[i].at[c], outs[i].at[me, c], send.at[i, j], recv.at[i, j], (cx, cy, c))
                 for i in range(n) for j, (cx, cy) in enumerate(chips)]
        for cp in first:
            cp.start()
        passed = []
        for i in range(n):
            for j, (cx, cy) in enumerate(chips):
                blk = outs[i].at[2 * cx + cy, c]
                _remote(blk, blk, send.at[i, j], recv.at[i, j], (x, y, c)).wait_recv()
                cp = _remote(blk, blk, send.at[i, 3 + j], recv.at[i, 3 + j], sib)
                cp.start()
                passed.append(cp)
        for i in range(n):
            for j, (cx, cy) in enumerate(chips):
                blk = outs[i].at[2 * cx + cy, 1 - c]
                _remote(blk, blk, send.at[i, 3 + j], recv.at[i, 3 + j], sib).wait_recv()
        for cp in first + passed:
            cp.wait_send()
        for cp in own:
            cp.wait()

    return pl.pallas_call(
        body, name=name, in_specs=[ANY] * n, out_specs=[ANY] * n,
        out_shape=[jax.ShapeDtypeStruct((N_CHIPS,) + a.shape, a.dtype) for a in arrs],
        scratch_shapes=[pltpu.SemaphoreType.DMA((n, 6)), pltpu.SemaphoreType.DMA((n, 6)), pltpu.SemaphoreType.DMA((n,))],
    )(*arrs)


def swap_layers(name, arrs):
    n = len(arrs)

    def body(*refs):
        ins, outs = refs[:n], refs[n:2 * n]
        send, recv = refs[2 * n:]
        x, y, c = _place()
        cps = [_remote(ins[i].at[1 - c], outs[i], send.at[i], recv.at[i], (x, y, 1 - c)) for i in range(n)]
        for cp in cps:
            cp.start()
        for cp in cps:
            cp.wait()

    return pl.pallas_call(
        body, name=name, in_specs=[ANY] * n, out_specs=[ANY] * n,
        out_shape=[jax.ShapeDtypeStruct(a.shape[1:], a.dtype) for a in arrs],
        scratch_shapes=[pltpu.SemaphoreType.DMA((n,)), pltpu.SemaphoreType.DMA((n,))],
    )(*arrs)


def scatter_chips(name, arrs):
    n = len(arrs)

    def body(*refs):
        ins, outs = refs[:n], refs[n:2 * n]
        send, recv, loc = refs[2 * n:]
        x, y, c = _place()
        me = 2 * x + y
        chips = _other_chips(x, y)
        own = [pltpu.make_async_copy(ins[i].at[me], outs[i].at[me], loc.at[i]) for i in range(n)]
        for cp in own:
            cp.start()
        cps = [_remote(ins[i].at[2 * cx + cy], outs[i].at[me], send.at[i, j], recv.at[i, j], (cx, cy, c))
               for i in range(n) for j, (cx, cy) in enumerate(chips)]
        for cp in cps:
            cp.start()
        for i in range(n):
            for j, (cx, cy) in enumerate(chips):
                blk = outs[i].at[2 * cx + cy]
                _remote(blk, blk, send.at[i, j], recv.at[i, j], (x, y, c)).wait_recv()
        for cp in cps:
            cp.wait_send()
        for cp in own:
            cp.wait()

    return pl.pallas_call(
        body, name=name, in_specs=[ANY] * n, out_specs=[ANY] * n,
        out_shape=[jax.ShapeDtypeStruct(a.shape, a.dtype) for a in arrs],
        scratch_shapes=[pltpu.SemaphoreType.DMA((n, 3)), pltpu.SemaphoreType.DMA((n, 3)), pltpu.SemaphoreType.DMA((n,))],
    )(*arrs)


def share_layers(name, arrs):
    n = len(arrs)

    def body(*refs):
        ins, outs = refs[:n], refs[n:2 * n]
        send, recv, loc = refs[2 * n:]
        x, y, c = _place()
        own = [pltpu.make_async_copy(ins[i], outs[i].at[c], loc.at[i]) for i in range(n)]
        for cp in own:
            cp.start()
        cps = [_remote(ins[i], outs[i].at[c], send.at[i], recv.at[i], (x, y, 1 - c)) for i in range(n)]
        for cp in cps:
            cp.start()
        for i in range(n):
            blk = outs[i].at[1 - c]
            _remote(blk, blk, send.at[i], recv.at[i], (x, y, c)).wait_recv()
        for cp in cps:
            cp.wait_send()
        for cp in own:
            cp.wait()

    return pl.pallas_call(
        body, name=name, in_specs=[ANY] * n, out_specs=[ANY] * n,
        out_shape=[jax.ShapeDtypeStruct((2,) + a.shape, a.dtype) for a in arrs],
        scratch_shapes=[pltpu.SemaphoreType.DMA((n,)), pltpu.SemaphoreType.DMA((n,)), pltpu.SemaphoreType.DMA((n,))],
    )(*arrs)


def all_reduce_small(name, packed):
    r = packed.shape[0]

    def body(in_ref, out_ref, buf, send, recv):
        x, y, c = _place()
        me = 4 * x + 2 * y + c
        buf[me] = in_ref[...]
        flips = [(fx, fy, fc) for fx in (0, 1) for fy in (0, 1) for fc in (0, 1) if (fx, fy, fc) != (0, 0, 0)]
        peers = [((x + fx) % 2, (y + fy) % 2, (c + fc) % 2) for fx, fy, fc in flips]
        cps = [_remote(in_ref, buf.at[me], send.at[k], recv.at[k], peer) for k, peer in enumerate(peers)]
        for cp in cps:
            cp.start()
        for k, (px, py, pc) in enumerate(peers):
            blk = buf.at[4 * px + 2 * py + pc]
            _remote(blk, blk, send.at[k], recv.at[k], (x, y, c)).wait_recv()
        for cp in cps:
            cp.wait_send()
        acc = buf[0]
        for d in range(1, N_DEV):
            acc = acc + buf[d]
        out_ref[...] = acc

    vm = pl.BlockSpec(memory_space=pltpu.VMEM)
    return pl.pallas_call(
        body, name=name, in_specs=[vm], out_specs=vm, out_shape=jax.ShapeDtypeStruct(packed.shape, F32),
        scratch_shapes=[pltpu.VMEM((N_DEV, r, LANE), F32), pltpu.SemaphoreType.DMA((N_DEV - 1,)),
                        pltpu.SemaphoreType.DMA((N_DEV - 1,))],
    )(packed)


def add_own_layer(name, g, a, c_idx, tr):
    _, nch, r, cc = g.shape
    tr = min(tr, r)

    def body(c_ref, g_ref, a_ref, o_ref):
        o_ref[...] = (g_ref[0] + a_ref[...]).astype(o_ref.dtype)

    return pl.pallas_call(
        body, name=name,
        grid_spec=pltpu.PrefetchScalarGridSpec(
            num_scalar_prefetch=1, grid=(nch, r // tr),
            in_specs=[pl.BlockSpec((1, 1, tr, cc), lambda j, i, c_ref: (c_ref[0], j, i, 0)),
                      pl.BlockSpec((1, tr, cc), lambda j, i, c_ref: (j, i, 0))],
            out_specs=pl.BlockSpec((1, tr, cc), lambda j, i, c_ref: (j, i, 0))),
        out_shape=jax.ShapeDtypeStruct(a.shape, BF16),
        compiler_params=_cparams(("parallel", "parallel")),
    )(c_idx, g, a)


def sum_chips(name, b, tr):
    nch, r, cc = b.shape
    tr = min(tr, r)

    def body(b_ref, o_ref):
        acc = b_ref[0].astype(F32)
        for k in range(1, nch):
            acc = acc + b_ref[k].astype(F32)
        o_ref[...] = acc

    return pl.pallas_call(
        body, name=name, grid=(r // tr,),
        in_specs=[pl.BlockSpec((nch, tr, cc), lambda i: (0, i, 0))],
        out_specs=pl.BlockSpec((tr, cc), lambda i: (i, 0)),
        out_shape=jax.ShapeDtypeStruct((r, cc), F32),
        compiler_params=_cparams(("parallel",)),
    )(b)


def reduce_scatter_grads(arrs, c_idx):
    tags = range(len(arrs))
    recv = swap_layers("rs_swap_layers", arrs)
    part = [add_own_layer(f"rs_add_sibling_{t}", g, a, c_idx, 128) for t, g, a in zip(tags, arrs, recv)]
    got = scatter_chips("rs_scatter_chips", part)
    red = [sum_chips(f"rs_sum_chips_{t}", b, 128) for t, b in zip(tags, got)]
    return share_layers("rs_share_layers", red)


def adamw(name, w, g, m, v, tb):
    r, cc = w.shape
    tb = min(tb, r)
    assert r % tb == 0

    def body(w_ref, g_ref, m_ref, v_ref, d_ref, mo_ref, vo_ref):
        g = g_ref[...]
        m = ADAM_B1 * m_ref[...] + (1.0 - ADAM_B1) * g
        v = ADAM_B2 * v_ref[...] + (1.0 - ADAM_B2) * jnp.square(g)
        m_hat = m / (1.0 - ADAM_B1 ** ADAM_STEP)
        v_hat = v / (1.0 - ADAM_B2 ** ADAM_STEP)
        d_ref[...] = -ADAM_LR * (m_hat / (jnp.sqrt(v_hat) + ADAM_EPS) + ADAM_WD * w_ref[...])
        mo_ref[...] = m
        vo_ref[...] = v

    spec = pl.BlockSpec((tb, cc), lambda i: (i, 0))
    return pl.pallas_call(
        body, name=name, grid=(r // tb,), in_specs=[spec] * 4, out_specs=[spec] * 3,
        out_shape=[jax.ShapeDtypeStruct((r, cc), F32)] * 3,
        compiler_params=_cparams(("parallel",)),
    )(w, g, m, v)


def _pack(arrs):
    flat = jnp.concatenate([a.reshape(-1).astype(F32) for a in arrs])
    pad = (-flat.shape[0]) % (8 * LANE)
    return jnp.pad(flat, (0, pad)).reshape(-1, LANE)


def _unpack(packed, shapes):
    flat = packed.reshape(-1)
    out, off = [], 0
    for s in shapes:
        size = math.prod(s)
        out.append(flat[off:off + size].reshape(s))
        off += size
    return out


BIG = ('w_in', 'w_out', 'b_pw_w')
SMALL = tuple(k for k in WEIGHTS if k not in BIG)
CHIP_SHARDED_SMALL = {'b_conv_w': 2, 'c_conv_w': 2}


def kernel(x, positions, norm_w, w_in, q_norm_w, k_norm_w, sinks, b_conv_w, b_conv_b, b_ln_w, b_ln_b, b_pw_w, b_pw_b, c_conv_w, c_a_log, c_dt_bias, c_onorm_w, w_out, loss_target, m_norm_w, m_w_in, m_q_norm_w, m_k_norm_w, m_sinks, m_b_conv_w, m_b_conv_b, m_b_ln_w, m_b_ln_b, m_b_pw_w, m_b_pw_b, m_c_conv_w, m_c_a_log, m_c_dt_bias, m_c_onorm_w, m_w_out, v_norm_w, v_w_in, v_q_norm_w, v_k_norm_w, v_sinks, v_b_conv_w, v_b_conv_b, v_b_ln_w, v_b_ln_b, v_b_pw_w, v_b_pw_b, v_c_conv_w, v_c_a_log, v_c_dt_bias, v_c_onorm_w, v_w_out):
    cfg = Cfg(x.shape[-1], x.shape[-2])
    w = dict(norm_w=norm_w, w_in=w_in, q_norm_w=q_norm_w, k_norm_w=k_norm_w, sinks=sinks, b_conv_w=b_conv_w,
             b_conv_b=b_conv_b, b_ln_w=b_ln_w, b_ln_b=b_ln_b, b_pw_w=b_pw_w, b_pw_b=b_pw_b, c_conv_w=c_conv_w,
             c_a_log=c_a_log, c_dt_bias=c_dt_bias, c_onorm_w=c_onorm_w, w_out=w_out)
    m = dict(norm_w=m_norm_w, w_in=m_w_in, q_norm_w=m_q_norm_w, k_norm_w=m_k_norm_w, sinks=m_sinks,
             b_conv_w=m_b_conv_w, b_conv_b=m_b_conv_b, b_ln_w=m_b_ln_w, b_ln_b=m_b_ln_b, b_pw_w=m_b_pw_w,
             b_pw_b=m_b_pw_b, c_conv_w=m_c_conv_w, c_a_log=m_c_a_log, c_dt_bias=m_c_dt_bias, c_onorm_w=m_c_onorm_w,
             w_out=m_w_out)
    v = dict(norm_w=v_norm_w, w_in=v_w_in, q_norm_w=v_q_norm_w, k_norm_w=v_k_norm_w, sinks=v_sinks,
             b_conv_w=v_b_conv_w, b_conv_b=v_b_conv_b, b_ln_w=v_b_ln_w, b_ln_b=v_b_ln_b, b_pw_w=v_b_pw_w,
             b_pw_b=v_b_pw_b, c_conv_w=v_c_conv_w, c_a_log=v_c_a_log, c_dt_bias=v_c_dt_bias, c_onorm_w=v_c_onorm_w,
             w_out=v_w_out)
    chip = 2 * lax.axis_index("x") + lax.axis_index("y")
    c_idx = lax.axis_index("c").astype(jnp.int32).reshape(1)

    g_in, g_out, g_pw, g_bcw, g_ccw = all_gather_chips(
        "gather_weights", [w_in.astype(BF16), w_out.astype(BF16), b_pw_w, b_conv_w, c_conv_w])
    full = dict(w)
    full['b_pw_w'] = jnp.stack([g_pw[:, l].reshape(cfg.BW, cfg.BW) for l in range(DEPTH)])
    full['b_conv_w'] = jnp.stack([jnp.concatenate(list(g_bcw[:, l]), axis=1) for l in range(DEPTH)])
    full['c_conv_w'] = jnp.stack([jnp.concatenate(list(g_ccw[:, l]), axis=1) for l in range(DEPTH)])
    wps = [permute_w_in(cfg, jnp.concatenate(list(g_in[:, l]), axis=1)) for l in range(DEPTH)]
    wos = [g_out[:, l].reshape(cfg.D, cfg.D) for l in range(DEPTH)]

    loss_local, grad_x, grads = local_step(cfg, x, positions, full, wps, wos, loss_target)
    loss = lax.psum(loss_local, ("x", "y", "c"))

    shard_cols = cfg.IN_COLS // N_CHIPS
    p_in = jnp.stack([unpermute_w_in(cfg, grads[l]['w_in_perm']).reshape(cfg.D, N_CHIPS, shard_cols).transpose(1, 0, 2)
                      for l in range(DEPTH)])
    p_out = jnp.stack([grads[l]['w_out'].reshape(N_CHIPS, cfg.D // N_CHIPS, cfg.D) for l in range(DEPTH)])
    p_pw = jnp.stack([grads[l]['b_pw_w'].reshape(N_CHIPS, cfg.BW // N_CHIPS, cfg.BW) for l in range(DEPTH)])
    r_in, r_out, r_pw = reduce_scatter_grads([p_in, p_out, p_pw], c_idx)
    small_parts = [jnp.stack([grads[l][k] for l in range(DEPTH)]) for k in SMALL]
    small_red = _unpack(all_reduce_small("all_reduce_small", _pack(small_parts)), [a.shape for a in small_parts])
    g = dict(w_in=r_in, w_out=r_out, b_pw_w=r_pw)
    for k, a in zip(SMALL, small_red):
        if k in CHIP_SHARDED_SMALL:
            ax = CHIP_SHARDED_SMALL[k]
            width = a.shape[ax] // N_CHIPS
            a = lax.dynamic_slice_in_dim(a, chip * width, width, axis=ax)
        g[k] = a

    delta, new_m, new_v = {}, {}, {}
    for k in BIG:
        cols = w[k].shape[-1]
        outs = adamw(f"adamw_{k}", w[k].reshape(-1, cols), g[k].reshape(-1, cols), m[k].reshape(-1, cols),
                     v[k].reshape(-1, cols), 128)
        delta[k], new_m[k], new_v[k] = [o.reshape(w[k].shape) for o in outs]
    shapes = [w[k].shape for k in SMALL]
    outs = adamw("adamw_small", _pack([w[k] for k in SMALL]), _pack([g[k] for k in SMALL]),
                 _pack([m[k] for k in SMALL]), _pack([v[k] for k in SMALL]), 4096)
    for name_, packed in zip(('delta', 'm', 'v'), outs):
        for k, a in zip(SMALL, _unpack(packed, shapes)):
            {'delta': delta, 'm': new_m, 'v': new_v}[name_][k] = a
    return (loss, grad_x, *[g[k] for k in WEIGHTS], *[delta[k] for k in WEIGHTS], *[new_m[k] for k in WEIGHTS],
            *[new_v[k] for k in WEIGHTS])
```

```python
import functools
import math

import numpy as np
import jax
import jax.numpy as jnp
from jax import lax
from jax.experimental import pallas as pl
from jax.experimental.pallas import tpu as pltpu

F32 = jnp.float32
BF16 = jnp.bfloat16
HI = lax.Precision.HIGHEST
MESH = pl.DeviceIdType.MESH

DEPTH = 2
A_HEAD = 64
A_GROUP = 3
ATTN_BLOCK = 128
ROT_DIM = 16
ROPE_THETA = 500000.0
B_CONV = 31
B_HALO = 32
C_HEAD = 128
C_CONV = 4
C_HALO = 8
CHUNK = 64
EPS = 1e-6
LANE = 128

ADAM_LR = 0.001
ADAM_B1 = 0.9
ADAM_B2 = 0.999
ADAM_EPS = 1e-08
ADAM_WD = 0.01
ADAM_STEP = 10

VMEM_LIMIT = 56 * 1024 * 1024

WEIGHTS = ['norm_w', 'w_in', 'q_norm_w', 'k_norm_w', 'sinks', 'b_conv_w', 'b_conv_b', 'b_ln_w', 'b_ln_b',
           'b_pw_w', 'b_pw_b', 'c_conv_w', 'c_a_log', 'c_dt_bias', 'c_onorm_w', 'w_out']


class Cfg:
    def __init__(self, d_model=2048, seq=2048):
        self.D = d_model
        self.T = seq
        self.AW = 3 * d_model // 8
        self.AQH = self.AW // A_HEAD
        self.AKH = self.AQH // A_GROUP
        self.AKW = self.AKH * A_HEAD
        self.BW = d_model // 4
        self.CH = (d_model - self.AW - self.BW) // C_HEAD
        self.CW = self.CH * C_HEAD
        AW, AKW, BW, CW, CH = self.AW, self.AKW, self.BW, self.CW, self.CH
        orig = [('qa', AW), ('ka', AKW), ('va', AKW), ('za', AW), ('ub', 2 * BW), ('zb', BW),
                ('qc', CW), ('kc', CW), ('vc', CW), ('bc', CH), ('ac', CH), ('zc', CW)]
        self.orig = {}
        off = 0
        for n, w in orig:
            self.orig[n] = (off, w)
            off += w
        self.IN_COLS = off
        order = ['qa', 'za', 'qc', 'kc', 'vc', 'zc', 'ka', 'va', 'ub', 'zb', 'bc', 'ac']
        self.order = order
        self.g = {}
        off = 0
        for n in order:
            w = self.orig[n][1]
            wp = LANE if n in ('bc', 'ac') else w
            assert off % wp == 0, (n, off, wp)
            self.g[n] = (off, wp)
            off += wp
        self.WP = off

    def blk(self, name):
        off, w = self.g[name]
        return off // w


def _cparams(sem, vmem=VMEM_LIMIT):
    return pltpu.CompilerParams(dimension_semantics=sem, vmem_limit_bytes=vmem)


def _silu(x):
    return x * jax.nn.sigmoid(x)


def _bdot(a, b, ca, cb, precision=HI):
    return lax.dot_general(a, b, (((ca,), (cb,)), ((0,), (0,))), precision=precision, preferred_element_type=F32)


def _rope_matrix(nb):
    i = lax.broadcasted_iota(jnp.int32, (nb, A_HEAD, A_HEAD), 1)
    j = lax.broadcasted_iota(jnp.int32, (nb, A_HEAD, A_HEAD), 2)
    half = ROT_DIM // 2
    neg = (j < half) & (i == j + half)
    pos = (j >= half) & (j < ROT_DIM) & (i == j - half)
    return jnp.where(neg, -1.0, jnp.where(pos, 1.0, 0.0)).astype(F32)


def _norm_rope(xh, w, cos, sin):
    y = xh * lax.rsqrt(jnp.mean(xh * xh, axis=-1, keepdims=True) + EPS) * w
    return y * cos + _bdot(y, _rope_matrix(xh.shape[0]), 2, 1) * sin


def attn_block(cfg, first, q, za, kc, vc, cosc, sinc, kp, vp, cosp, sinp, qnw, knw, sinks_row):
    blk = ATTN_BLOCK
    nq, nk = cfg.AQH, cfg.AKH
    qi = lax.broadcasted_iota(jnp.int32, (blk, 2 * blk), 0)
    kj = lax.broadcasted_iota(jnp.int32, (blk, 2 * blk), 1)
    dist = qi + blk - kj
    valid = ((dist >= 0) & (dist < blk) & (jnp.logical_not(first) | (kj >= blk)))[None]
    cos2 = jnp.concatenate([cosp, cosc], axis=0)
    sin2 = jnp.concatenate([sinp, sinc], axis=0)
    head = lambda x, h: x[:, A_HEAD * h:A_HEAD * (h + 1)]
    k2 = jnp.stack([jnp.concatenate([head(kp, h), head(kc, h)], axis=0) for h in range(nk)], axis=0)
    v2 = jnp.stack([jnp.concatenate([head(vp, h), head(vc, h)], axis=0) for h in range(nk)], axis=0)
    k2 = _norm_rope(k2, knw[None], cos2[None], sin2[None]).astype(BF16)
    v2 = v2.astype(BF16)
    k2 = jnp.stack([k2[h // A_GROUP] for h in range(nq)], axis=0)
    v2 = jnp.stack([v2[h // A_GROUP] for h in range(nq)], axis=0)
    qh = jnp.stack([head(q, h) for h in range(nq)], axis=0)
    qh = _norm_rope(qh, qnw[None], cosc[None], sinc[None]).astype(BF16)
    s = _bdot(qh, k2, 2, 2, None) * (A_HEAD ** -0.5)
    s = jnp.where(valid, s, -1e30)
    sink = jnp.stack([sinks_row[:, A_HEAD * h:A_HEAD * h + 1] for h in range(nq)], axis=0)
    m = jnp.maximum(jnp.max(s, axis=-1, keepdims=True), sink)
    e = jnp.exp(s - m)
    den = jnp.sum(e, axis=-1, keepdims=True) + jnp.exp(sink - m)
    o = _bdot((e / den).astype(BF16), v2, 2, 1, None)
    return (jnp.concatenate([o[h] for h in range(nq)], axis=1) * _silu(za),)


def conv_block(cfg, first, u, zb, uh, cw, cb, lw, lb, pw, pb):
    BW = cfg.BW
    tb = u.shape[0]
    uu = jnp.concatenate([uh, u], axis=0)
    h = uu[:, :BW] * jax.nn.sigmoid(uu[:, BW:])
    row = lax.broadcasted_iota(jnp.int32, h.shape, 0)
    h = jnp.where(first & (row < B_HALO), 0.0, h)
    acc = jnp.zeros((tb, BW), F32) + cb
    base = B_HALO - (B_CONV - 1)
    for k in range(B_CONV):
        acc = acc + cw[k:k + 1, :] * h[base + k:base + k + tb, :]
    mu = jnp.mean(acc, axis=-1, keepdims=True)
    var = jnp.mean(jnp.square(acc - mu), axis=-1, keepdims=True)
    y = (acc - mu) * lax.rsqrt(var + EPS) * lw + lb
    s = _silu(y)
    o = jnp.dot(s.astype(BF16), pw.astype(BF16), preferred_element_type=F32) + pb
    return (o * _silu(zb),)


def gdn_prep_block(cfg, first, xq, xk, xv, braw, araw, hq, hk, hv, cw, alog, dtb):
    CW = cfg.CW
    tb = xq.shape[0]
    outs = []
    for idx, (x, xh) in enumerate(((xq, hq), (xk, hk), (xv, hv))):
        xx = jnp.concatenate([jnp.where(first, 0.0, xh), x], axis=0)
        w = cw[:, idx * CW:(idx + 1) * CW]
        acc = jnp.zeros((tb, CW), F32)
        base = C_HALO - (C_CONV - 1)
        for k in range(C_CONV):
            acc = acc + w[k:k + 1, :] * xx[base + k:base + k + tb, :]
        y = _silu(acc)
        if idx < 2:
            parts = []
            for h in range(cfg.CH):
                yh = y[:, C_HEAD * h:C_HEAD * (h + 1)]
                parts.append(yh * lax.rsqrt(jnp.sum(yh * yh, axis=-1, keepdims=True) + EPS))
            y = jnp.concatenate(parts, axis=1)
        outs.append(y)
    beta = jax.nn.sigmoid(braw)
    g = -jnp.exp(alog) * jax.nn.softplus(araw + dtb)
    return outs[0], outs[1], outs[2], g, beta


def _inverse_unit_lower(low, eye):
    pw = low
    inv = eye - low
    for _ in range(5):
        pwb = pw.astype(BF16)
        pw = _bdot(pwb, pwb, 2, 1, None)
        inv = inv + _bdot(inv.astype(BF16), pw.astype(BF16), 2, 1, None)
    ax = inv + _bdot(low, inv, 2, 1)
    return inv + _bdot(inv, eye - ax, 2, 1)


@jax.custom_vjp
def _saved_inverse(low, inv):
    return inv


def _saved_inverse_fwd(low, inv):
    return inv, inv


def _saved_inverse_bwd(inv, d):
    dlow = -_bdot(_bdot(inv, d, 1, 1), inv, 2, 2)
    return dlow, jnp.zeros_like(inv)


_saved_inverse.defvjp(_saved_inverse_fwd, _saved_inverse_bwd)


def gdn_intra_rows(cfg, first, qn, kn, v, g, beta, inv_saved=None):
    c = CHUNK
    CH = cfg.CH
    nchunk = qn.shape[0] // c
    i = lax.broadcasted_iota(jnp.int32, (c, c), 0)
    j = lax.broadcasted_iota(jnp.int32, (c, c), 1)
    incl = (i >= j)[None]
    strict = (i > j)[None]
    eye = (i == j).astype(F32)[None]
    tri = (i >= j).astype(F32)
    rows = [slice(c * ci, c * (ci + 1)) for ci in range(nchunk)]
    gcs = [jnp.dot(tri, g[r], precision=HI, preferred_element_type=F32) for r in rows]
    pairs = [(ci, h) for ci in range(nchunk) for h in range(CH)]
    heads = lambda x, wd: jnp.stack([x[rows[ci], wd * h:wd * (h + 1)] for ci, h in pairs], axis=0)
    gch = jnp.stack([gcs[ci][:, h:h + 1] for ci, h in pairs], axis=0)
    bh = jnp.stack([beta[rows[ci], h:h + 1] for ci, h in pairs], axis=0)
    q = heads(qn, C_HEAD) * (C_HEAD ** -0.5)
    k = heads(kn, C_HEAD)
    vv = heads(v, C_HEAD)
    a = jnp.broadcast_to(gch, (len(pairs), c, c))
    diff = jnp.where(incl, a - jnp.swapaxes(a, 1, 2), 0.0)
    decay = jnp.where(incl, jnp.exp(diff), 0.0)
    kb = k * bh
    low = jnp.where(strict, _bdot(kb, k, 2, 2) * decay, 0.0)
    if inv_saved is None:
        inv = _inverse_unit_lower(low, eye)
    else:
        inv = _saved_inverse(low, heads(inv_saved, c))
    eg = jnp.exp(gch)
    sol = _bdot(inv, jnp.concatenate([vv * bh, kb * eg], axis=2), 2, 1)
    intra = jnp.where(incl, _bdot(q, k, 2, 2) * decay, 0.0)
    qg = q * eg
    kd = k * jnp.exp(gch[:, c - 1:c, :] - gch)
    glast = jnp.concatenate([jnp.broadcast_to(gc[c - 1:c, :], gc.shape) for gc in gcs], axis=0)

    def unstack(x):
        return jnp.concatenate([jnp.concatenate([x[ci * CH + h] for h in range(CH)], axis=1) for ci in range(nchunk)],
                               axis=0)

    outs = (unstack(sol[:, :, :C_HEAD]), unstack(sol[:, :, C_HEAD:]), unstack(qg), unstack(kd), unstack(intra), glast)
    return outs + (unstack(inv),) if inv_saved is None else outs


def gdn_state_step(S, u, w, qg, kd, intra, glast):
    v_new = u - _bdot(w, S, 2, 1)
    o = _bdot(qg, S, 2, 1) + _bdot(intra, v_new, 2, 1)
    S_next = S * jnp.exp(glast) + _bdot(kd, v_new, 1, 1)
    return o, S_next


def gdn_out_block(cfg, first, o, zc, onw):
    parts = []
    for h in range(cfg.CH):
        sl = slice(C_HEAD * h, C_HEAD * (h + 1))
        oh = o[:, sl]
        y = oh * lax.rsqrt(jnp.mean(oh * oh, axis=-1, keepdims=True) + EPS) * onw
        parts.append(y * _silu(zc[:, sl]))
    return (jnp.concatenate(parts, axis=1),)


def rms_block(x, nw):
    return x * lax.rsqrt(jnp.mean(x * x, axis=-1, keepdims=True) + EPS) * nw


class Row:
    def __init__(self, arr, width, colblk=0, grad=None):
        self.arr, self.width, self.colblk, self.grad = arr, width, colblk, grad


class Halo:
    def __init__(self, arr, width, colblk, hr, tie=None):
        self.arr, self.width, self.colblk, self.hr, self.tie = arr, width, colblk, hr, tie


def _row_specs(tb, rows, halos, params, pos):
    specs = [pl.BlockSpec((tb, r.width), lambda i, cb=r.colblk: (pos(i), cb)) for r in rows]
    specs += [pl.BlockSpec((h.hr, h.width),
                           lambda i, cb=h.colblk, m=tb // h.hr: (jnp.maximum(pos(i) * m - 1, 0), cb))
              for h in halos]
    specs += [pl.BlockSpec(p.shape, lambda i: (0, 0)) for p in params]
    return specs


def rb_fwd(name, fn, n, tb, bps, rows, halos, params, outs):
    nr, nh, npar = len(rows), len(halos), len(params)

    def body(*refs):
        ins = refs[:nr + nh + npar]
        o_refs = refs[nr + nh + npar:]
        first = (pl.program_id(0) % bps) == 0
        res = fn(first, *[r[...] for r in ins])
        for ref, val in zip(o_refs, res):
            ref[...] = val.astype(ref.dtype)

    return pl.pallas_call(
        body, name=name, grid=(n // tb,),
        in_specs=_row_specs(tb, rows, halos, params, lambda i: i),
        out_specs=[pl.BlockSpec((tb, w), lambda i: (i, 0)) for w, _ in outs],
        out_shape=[jax.ShapeDtypeStruct((n, w), dt) for w, dt in outs],
        compiler_params=_cparams(("parallel",)),
    )(*[r.arr for r in rows], *[h.arr for h in halos], *params)


def rb_bwd(name, fn, n, tb, bps, rows, halos, params, douts, param_grads):
    nr, nh, npar, nd = len(rows), len(halos), len(params), len(douts)
    nblk = n // tb
    grow = [k for k, r in enumerate(rows) if r.grad is not None]
    ghalo = [k for k, h in enumerate(halos) if h.tie is not None]
    gpar = [k for k, f in enumerate(param_grads) if f]
    pos = lambda i: nblk - 1 - i

    def body(*refs):
        ins = refs[:nr + nh + npar]
        d_refs = refs[nr + nh + npar:nr + nh + npar + nd]
        rest = refs[nr + nh + npar + nd:]
        grow_refs = rest[:len(grow)]
        gpar_refs = rest[len(grow):len(grow) + len(gpar)]
        carry_refs = rest[len(grow) + len(gpar):]
        i = pl.program_id(0)
        first = (pos(i) % bps) == 0
        vals = [r[...] for r in ins]
        diff_idx = grow + [nr + k for k in ghalo] + [nr + nh + k for k in gpar]

        def f(*dargs):
            full = list(vals)
            for k, a in zip(diff_idx, dargs):
                full[k] = a
            return fn(first, *full)

        res, vjp = jax.vjp(f, *[vals[k] for k in diff_idx])
        grads = vjp(tuple(d[...].astype(r.dtype) for d, r in zip(d_refs, res)))
        g_rows = list(grads[:len(grow)])
        g_halos = grads[len(grow):len(grow) + len(ghalo)]
        g_pars = grads[len(grow) + len(ghalo):]

        @pl.when(i == 0)
        def _():
            for c in carry_refs:
                c[...] = jnp.zeros_like(c)
            for p in gpar_refs:
                p[...] = jnp.zeros_like(p)

        for k, ref in enumerate(grow_refs):
            ref[...] = g_rows[k].astype(ref.dtype)
        for ci, hk in enumerate(ghalo):
            h = halos[hk]
            k = grow.index(h.tie)
            tail = g_rows[k][tb - h.hr:, :] + carry_refs[ci][...]
            grow_refs[k][tb - h.hr:, :] = tail.astype(grow_refs[k].dtype)
            carry_refs[ci][...] = g_halos[ci]
        for ref, gp in zip(gpar_refs, g_pars):
            ref[...] += gp

    out_specs = [pl.BlockSpec((tb, rows[k].width), lambda i: (pos(i), 0)) for k in grow]
    out_specs += [pl.BlockSpec(params[k].shape, lambda i: (0, 0)) for k in gpar]
    out_shape = [jax.ShapeDtypeStruct((n, rows[k].width), rows[k].grad) for k in grow]
    out_shape += [jax.ShapeDtypeStruct(params[k].shape, F32) for k in gpar]
    in_specs = _row_specs(tb, rows, halos, params, pos)
    in_specs += [pl.BlockSpec((tb, d.shape[1]), lambda i: (pos(i), 0)) for d in douts]
    return pl.pallas_call(
        body, name=name, grid=(nblk,), in_specs=in_specs, out_specs=out_specs, out_shape=out_shape,
        scratch_shapes=[pltpu.VMEM((halos[k].hr, halos[k].width), F32) for k in ghalo],
        compiler_params=_cparams(("arbitrary",)),
    )(*[r.arr for r in rows], *[h.arr for h in halos], *params, *douts)


_DIMS = {'nn': (((1,), (0,)), ((), ())), 'nt': (((1,), (1,)), ((), ())), 'tn': (((0,), (0,)), ((), ()))}


def matmul(name, a, b, mode, tm, tn, tk, out_dtype=F32, add=None):
    if mode == 'tn':
        K, M = a.shape
    else:
        M, K = a.shape
    N = b.shape[0] if mode == 'nt' else b.shape[1]
    tm, tn, tk = min(tm, M), min(tn, N), min(tk, K)
    assert M % tm == 0 and N % tn == 0 and K % tk == 0, (name, M, N, K, tm, tn, tk)
    nk = K // tk
    a_spec = pl.BlockSpec((tk, tm), lambda i, j, k: (k, i)) if mode == 'tn' else pl.BlockSpec((tm, tk), lambda i, j, k: (i, k))
    b_spec = pl.BlockSpec((tn, tk), lambda i, j, k: (j, k)) if mode == 'nt' else pl.BlockSpec((tk, tn), lambda i, j, k: (k, j))
    o_spec = pl.BlockSpec((tm, tn), lambda i, j, k: (i, j))
    has_add = add is not None

    def body(*refs):
        a_ref, b_ref = refs[0], refs[1]
        add_ref = refs[2] if has_add else None
        o_ref, acc = refs[-2], refs[-1]
        k = pl.program_id(2)

        @pl.when(k == 0)
        def _():
            acc[...] = jnp.zeros_like(acc)

        acc[...] += lax.dot_general(a_ref[...].astype(BF16), b_ref[...].astype(BF16), _DIMS[mode],
                                    preferred_element_type=F32)

        @pl.when(k == nk - 1)
        def _():
            r = acc[...]
            if has_add:
                r = r + add_ref[...]
            o_ref[...] = r.astype(o_ref.dtype)

    ins = [a, b] + ([add] if has_add else [])
    in_specs = [a_spec, b_spec] + ([o_spec] if has_add else [])
    return pl.pallas_call(
        body, name=name, grid=(M // tm, N // tn, nk), in_specs=in_specs, out_specs=o_spec,
        out_shape=jax.ShapeDtypeStruct((M, N), out_dtype),
        scratch_shapes=[pltpu.VMEM((tm, tn), F32)],
        compiler_params=_cparams(("parallel", "parallel", "arbitrary")),
    )(*ins)


def norm_in_proj(name, x, nw, wp, tm, tn):
    n, d = x.shape
    wpc = wp.shape[1]
    tm, tn = min(tm, n), min(tn, wpc)
    assert n % tm == 0 and wpc % tn == 0

    def body(x_ref, nw_ref, w_ref, p_ref, h_ref):
        @pl.when(pl.program_id(1) == 0)
        def _():
            h_ref[...] = rms_block(x_ref[...], nw_ref[...]).astype(BF16)

        p_ref[...] = jnp.dot(h_ref[...], w_ref[...], preferred_element_type=F32)

    return pl.pallas_call(
        body, name=name, grid=(n // tm, wpc // tn),
        in_specs=[pl.BlockSpec((tm, d), lambda i, j: (i, 0)), pl.BlockSpec((1, d), lambda i, j: (0, 0)),
                  pl.BlockSpec((d, tn), lambda i, j: (0, j))],
        out_specs=[pl.BlockSpec((tm, tn), lambda i, j: (i, j)), pl.BlockSpec((tm, d), lambda i, j: (i, 0))],
        out_shape=[jax.ShapeDtypeStruct((n, wpc), F32), jax.ShapeDtypeStruct((n, d), BF16)],
        compiler_params=_cparams(("parallel", "arbitrary")),
    )(x, nw, wp)


def norm_bwd(name, x, nw, dh, dres, tb):
    n, d = x.shape
    tb = min(tb, n)

    def body(x_ref, nw_ref, dh_ref, dres_ref, dx_ref, dnw_ref):
        @pl.when(pl.program_id(0) == 0)
        def _():
            dnw_ref[...] = jnp.zeros_like(dnw_ref)

        _, vjp = jax.vjp(rms_block, x_ref[...], nw_ref[...])
        dx, dnw = vjp(dh_ref[...])
        dx_ref[...] = dx + dres_ref[...]
        dnw_ref[...] += dnw

    row = pl.BlockSpec((tb, d), lambda i: (i, 0))
    par = pl.BlockSpec((1, d), lambda i: (0, 0))
    return pl.pallas_call(
        body, name=name, grid=(n // tb,), in_specs=[row, par, row, row], out_specs=[row, par],
        out_shape=[jax.ShapeDtypeStruct((n, d), F32), jax.ShapeDtypeStruct((1, d), F32)],
        compiler_params=_cparams(("arbitrary",)),
    )(x, nw, dh, dres)


def loss_grad(name, y, target, tb):
    n, d = y.shape
    tb = min(tb, n)

    def body(y_ref, t_ref, dy_ref, loss_ref):
        @pl.when(pl.program_id(0) == 0)
        def _():
            loss_ref[...] = jnp.zeros_like(loss_ref)

        err = y_ref[...] - t_ref[...]
        dy_ref[...] = err * (1.0 / d)
        part = 0.5 * jnp.sum(jnp.mean(err * err, axis=-1, keepdims=True), axis=0, keepdims=True)
        loss_ref[...] += jnp.broadcast_to(part, loss_ref.shape)

    row = pl.BlockSpec((tb, d), lambda i: (i, 0))
    dy, loss = pl.pallas_call(
        body, name=name, grid=(n // tb,), in_specs=[row, row],
        out_specs=[row, pl.BlockSpec((8, LANE), lambda i: (0, 0))],
        out_shape=[jax.ShapeDtypeStruct((n, d), F32), jax.ShapeDtypeStruct((8, LANE), F32)],
        compiler_params=_cparams(("arbitrary",)),
    )(y, target)
    return dy, loss[0, 0]


def rope_tables(name, pos_col, inv_freq_row):
    n = pos_col.shape[0]

    def body(p_ref, f_ref, c_ref, s_ref):
        ang = p_ref[...].astype(F32) * f_ref[...]
        lane = lax.broadcasted_iota(jnp.int32, ang.shape, 1)
        c_ref[...] = jnp.where(lane < ROT_DIM, jnp.cos(ang), 1.0)
        s_ref[...] = jnp.where(lane < ROT_DIM, jnp.sin(ang), 0.0)

    return pl.pallas_call(
        body, name=name, out_shape=[jax.ShapeDtypeStruct((n, A_HEAD), F32)] * 2,
    )(pos_col, inv_freq_row)


def _scan_operands(cfg, nseq, u_ref, w_ref, qg_ref, kd_ref, a_ref, gl_ref):
    pairs = [(b, h) for b in range(nseq) for h in range(cfg.CH)]
    st = lambda r, wd: jnp.stack([r[b, :, wd * h:wd * (h + 1)] for b, h in pairs], axis=0)
    gl = jnp.stack([gl_ref[b, 0:1, h:h + 1] for b, h in pairs], axis=0)
    return st(u_ref, C_HEAD), st(w_ref, C_HEAD), st(qg_ref, C_HEAD), st(kd_ref, C_HEAD), st(a_ref, CHUNK), gl


def gdn_scan_fwd(name, cfg, nseq, u, w, qg, kd, intra, glast):
    CH, CW, T = cfg.CH, cfg.CW, cfg.T
    nc = T // CHUNK

    def body(u_ref, w_ref, qg_ref, kd_ref, a_ref, gl_ref, o_ref, sin_ref, s_ref):
        @pl.when(pl.program_id(0) == 0)
        def _():
            s_ref[...] = jnp.zeros_like(s_ref)

        S = s_ref[...]
        for b in range(nseq):
            sin_ref[b, 0] = S[b * CH:(b + 1) * CH]
        o, S_next = gdn_state_step(S, *_scan_operands(cfg, nseq, u_ref, w_ref, qg_ref, kd_ref, a_ref, gl_ref))
        s_ref[...] = S_next
        for b in range(nseq):
            o_ref[b] = jnp.concatenate([o[b * CH + h] for h in range(CH)], axis=1)

    row = lambda wd: pl.BlockSpec((nseq, CHUNK, wd), lambda c: (0, c, 0))
    widths = [CW, CW, CW, CW, CH * CHUNK, LANE]
    o, s_in = pl.pallas_call(
        body, name=name, grid=(nc,),
        in_specs=[row(x) for x in widths],
        out_specs=[row(CW), pl.BlockSpec((nseq, 1, CH, C_HEAD, C_HEAD), lambda c: (0, c, 0, 0, 0))],
        out_shape=[jax.ShapeDtypeStruct((nseq, T, CW), F32),
                   jax.ShapeDtypeStruct((nseq, nc, CH, C_HEAD, C_HEAD), F32)],
        scratch_shapes=[pltpu.VMEM((nseq * CH, C_HEAD, C_HEAD), F32)],
        compiler_params=_cparams(("arbitrary",)),
    )(*[a.reshape(nseq, T, a.shape[1]) for a in (u, w, qg, kd, intra, glast)])
    return o.reshape(nseq * T, CW), s_in


def gdn_scan_bwd(name, cfg, nseq, u, w, qg, kd, intra, glast, s_in, do):
    CH, CW, T = cfg.CH, cfg.CW, cfg.T
    nc = T // CHUNK

    def body(u_ref, w_ref, qg_ref, kd_ref, a_ref, gl_ref, sin_ref, do_ref,
             du_ref, dw_ref, dqg_ref, dkd_ref, da_ref, dgl_ref, ds_ref):
        @pl.when(pl.program_id(0) == 0)
        def _():
            ds_ref[...] = jnp.zeros_like(ds_ref)

        S = jnp.concatenate([sin_ref[b, 0] for b in range(nseq)], axis=0)
        dout = jnp.stack([do_ref[b, :, C_HEAD * h:C_HEAD * (h + 1)] for b in range(nseq) for h in range(CH)], axis=0)
        _, vjp = jax.vjp(gdn_state_step, S, *_scan_operands(cfg, nseq, u_ref, w_ref, qg_ref, kd_ref, a_ref, gl_ref))
        dS, du, dw, dqg, dkd, da, dg = vjp((dout, ds_ref[...]))
        ds_ref[...] = dS
        lane = lax.broadcasted_iota(jnp.int32, (CHUNK, LANE), 1)
        rowi = lax.broadcasted_iota(jnp.int32, (CHUNK, LANE), 0)
        for b in range(nseq):
            cat = lambda x: jnp.concatenate([x[b * CH + h] for h in range(CH)], axis=1)
            du_ref[b] = cat(du)
            dw_ref[b] = cat(dw)
            dqg_ref[b] = cat(dqg)
            dkd_ref[b] = cat(dkd)
            da_ref[b] = cat(da)
            dgl = jnp.zeros((CHUNK, LANE), F32)
            for h in range(CH):
                dgl = dgl + jnp.where((lane == h) & (rowi == 0), dg[b * CH + h], 0.0)
            dgl_ref[b] = dgl

    row = lambda wd: pl.BlockSpec((nseq, CHUNK, wd), lambda c: (0, nc - 1 - c, 0))
    widths = [CW, CW, CW, CW, CH * CHUNK, LANE]
    outs = pl.pallas_call(
        body, name=name, grid=(nc,),
        in_specs=[row(x) for x in widths]
        + [pl.BlockSpec((nseq, 1, CH, C_HEAD, C_HEAD), lambda c: (0, nc - 1 - c, 0, 0, 0)), row(CW)],
        out_specs=[row(x) for x in widths],
        out_shape=[jax.ShapeDtypeStruct((nseq, T, x), F32) for x in widths],
        scratch_shapes=[pltpu.VMEM((nseq * CH, C_HEAD, C_HEAD), F32)],
        compiler_params=_cparams(("arbitrary",)),
    )(*[a.reshape(nseq, T, a.shape[1]) for a in (u, w, qg, kd, intra, glast)], s_in, do.reshape(nseq, T, CW))
    return [o.reshape(nseq * T, o.shape[2]) for o in outs]


def _tile(total, cap, unit=LANE):
    best = None
    for t in range(unit, min(cap, total) + 1, unit):
        if total % t == 0:
            best = t
    assert best is not None, (total, cap, unit)
    return best


def _pad_lanes(v, width=LANE):
    return jnp.pad(v.reshape(1, -1), ((0, 0), (0, width - v.shape[-1])))


def permute_w_in(cfg, w):
    parts = []
    for n in cfg.order:
        off, wd = cfg.orig[n]
        blk = w[:, off:off + wd]
        if cfg.g[n][1] != wd:
            blk = jnp.pad(blk, ((0, 0), (0, cfg.g[n][1] - wd)))
        parts.append(blk)
    return jnp.concatenate(parts, axis=1)


def unpermute_w_in(cfg, wp):
    parts = []
    for n, (off, wd) in cfg.orig.items():
        parts.append(wp[:, cfg.g[n][0]:cfg.g[n][0] + wd])
    return jnp.concatenate(parts, axis=1)


def _layer_params(cfg, prm, l):
    return dict(
        nw=prm['norm_w'][l].reshape(1, -1),
        qnw=prm['q_norm_w'][l].reshape(1, -1), knw=prm['k_norm_w'][l].reshape(1, -1),
        sinks_row=jnp.repeat(prm['sinks'][l], A_HEAD).reshape(1, -1),
        cw=prm['b_conv_w'][l], cb=prm['b_conv_b'][l].reshape(1, -1),
        lw=prm['b_ln_w'][l].reshape(1, -1), lb=prm['b_ln_b'][l].reshape(1, -1),
        pw=prm['b_pw_w'][l], pb=prm['b_pw_b'][l].reshape(1, -1),
        ccw=prm['c_conv_w'][l], alog=_pad_lanes(prm['c_a_log'][l]), dtb=_pad_lanes(prm['c_dt_bias'][l]),
        onw=prm['c_onorm_w'][l].reshape(1, -1),
    )


def _attn_io(cfg, p, cos, sin, grads):
    gq = BF16 if grads else None
    rows = [Row(p, cfg.AW, cfg.blk('qa'), gq), Row(p, cfg.AW, cfg.blk('za'), gq),
            Row(p, cfg.AKW, cfg.blk('ka'), gq), Row(p, cfg.AKW, cfg.blk('va'), gq),
            Row(cos, A_HEAD), Row(sin, A_HEAD)]
    halos = [Halo(p, cfg.AKW, cfg.blk('ka'), ATTN_BLOCK, 2 if grads else None),
             Halo(p, cfg.AKW, cfg.blk('va'), ATTN_BLOCK, 3 if grads else None),
             Halo(cos, A_HEAD, 0, ATTN_BLOCK), Halo(sin, A_HEAD, 0, ATTN_BLOCK)]
    return rows, halos


def _conv_io(cfg, p, grads):
    gq = BF16 if grads else None
    rows = [Row(p, 2 * cfg.BW, cfg.blk('ub'), gq), Row(p, cfg.BW, cfg.blk('zb'), gq)]
    halos = [Halo(p, 2 * cfg.BW, cfg.blk('ub'), B_HALO, 0 if grads else None)]
    return rows, halos


def _prep_io(cfg, p, grads):
    gq = BF16 if grads else None
    rows = [Row(p, cfg.CW, cfg.blk(n), gq) for n in ('qc', 'kc', 'vc')]
    rows += [Row(p, LANE, cfg.blk('bc'), gq), Row(p, LANE, cfg.blk('ac'), gq)]
    halos = [Halo(p, cfg.CW, cfg.blk(n), C_HALO, k if grads else None) for k, n in enumerate(('qc', 'kc', 'vc'))]
    return rows, halos


TB_CONV = 128
TB_PREP = 256
TB_OUT = 256
TB_INTRA_FWD = 128
TB_INTRA_BWD = 64


def layer_forward(cfg, l, x, lp, wp, wo, cos, sin):
    n = x.shape[0]
    nseq = n // cfg.T
    T = cfg.T
    p, h = norm_in_proj(f"in_proj_{l}", x, lp['nw'], wp, 512, _tile(cfg.WP, 1152))
    rows, halos = _attn_io(cfg, p, cos, sin, False)
    (oa,) = rb_fwd(f"attn_fwd_{l}", functools.partial(attn_block, cfg), n, ATTN_BLOCK, T // ATTN_BLOCK, rows, halos,
                   [lp['qnw'], lp['knw'], lp['sinks_row']], [(cfg.AW, BF16)])
    rows, halos = _conv_io(cfg, p, False)
    tbb = min(TB_CONV, T)
    (ob,) = rb_fwd(f"conv_fwd_{l}", functools.partial(conv_block, cfg), n, tbb, T // tbb, rows, halos,
                   [lp['cw'], lp['cb'], lp['lw'], lp['lb'], lp['pw'], lp['pb']], [(cfg.BW, BF16)])
    rows, halos = _prep_io(cfg, p, False)
    tbp = min(TB_PREP, T)
    qn, kn, v, g, beta = rb_fwd(f"gdn_prep_fwd_{l}", functools.partial(gdn_prep_block, cfg), n, tbp, T // tbp, rows,
                                halos, [lp['ccw'], lp['alog'], lp['dtb']],
                                [(cfg.CW, F32)] * 3 + [(LANE, F32)] * 2)
    intra_outs = rb_fwd(f"gdn_intra_fwd_{l}", functools.partial(gdn_intra_rows, cfg), n, TB_INTRA_FWD, T // TB_INTRA_FWD,
                        [Row(qn, cfg.CW), Row(kn, cfg.CW), Row(v, cfg.CW), Row(g, LANE), Row(beta, LANE)], [], [],
                        [(cfg.CW, F32)] * 4 + [(cfg.CH * CHUNK, F32), (LANE, F32), (cfg.CH * CHUNK, F32)])
    intra_outs, inv = intra_outs[:6], intra_outs[6]
    o, s_in = gdn_scan_fwd(f"gdn_scan_fwd_{l}", cfg, nseq, *intra_outs)
    tbo = min(TB_OUT, T)
    (oc,) = rb_fwd(f"gdn_out_fwd_{l}", functools.partial(gdn_out_block, cfg), n, tbo, T // tbo,
                   [Row(o, cfg.CW), Row(p, cfg.CW, cfg.blk('zc'))], [], [lp['onw']], [(cfg.CW, BF16)])
    y = jnp.concatenate([oa, ob, oc], axis=1)
    x_next = matmul(f"out_proj_{l}", y, wo, 'nn', 512, 512, cfg.D, add=x)
    saved = dict(x=x, p=p, h=h, y=y, qn=qn, kn=kn, v=v, g=g, beta=beta, intra_outs=intra_outs, inv=inv, s_in=s_in, o=o)
    return x_next, saved


def layer_backward(cfg, l, dxn, sv, lp, wp, wo, cos, sin):
    n = dxn.shape[0]
    nseq = n // cfg.T
    T = cfg.T
    p = sv['p']
    AW, BW, CW = cfg.AW, cfg.BW, cfg.CW
    dy = matmul(f"dy_{l}", dxn, wo, 'nt', 512, 512, cfg.D)
    dwo = matmul(f"dwo_{l}", sv['y'], dxn, 'tn', 512, 512, 1024)
    doa, dob, doc = dy[:, :AW], dy[:, AW:AW + BW], dy[:, AW + BW:]
    tbo = min(TB_OUT, T)
    do, dzc, donw = rb_bwd(f"gdn_out_bwd_{l}", functools.partial(gdn_out_block, cfg), n, tbo, T // tbo,
                           [Row(sv['o'], CW, 0, F32), Row(p, CW, cfg.blk('zc'), BF16)], [], [lp['onw']], [doc], [True])
    dintra = gdn_scan_bwd(f"gdn_scan_bwd_{l}", cfg, nseq, *sv['intra_outs'], sv['s_in'], do)
    dqn, dkn, dv, dg, dbeta = rb_bwd(
        f"gdn_intra_bwd_{l}", functools.partial(gdn_intra_rows, cfg), n, TB_INTRA_BWD, T // TB_INTRA_BWD,
        [Row(sv['qn'], CW, 0, F32), Row(sv['kn'], CW, 0, F32), Row(sv['v'], CW, 0, F32), Row(sv['g'], LANE, 0, F32),
         Row(sv['beta'], LANE, 0, F32), Row(sv['inv'], cfg.CH * CHUNK)], [], [], list(dintra), [])
    rows, halos = _prep_io(cfg, p, True)
    tbp = min(TB_PREP, T)
    dqc, dkc, dvc, dbc, dac, dccw, dalog, ddtb = rb_bwd(
        f"gdn_prep_bwd_{l}", functools.partial(gdn_prep_block, cfg), n, tbp, T // tbp, rows, halos,
        [lp['ccw'], lp['alog'], lp['dtb']], [dqn, dkn, dv, dg, dbeta], [True] * 3)
    rows, halos = _conv_io(cfg, p, True)
    tbb = min(TB_CONV, T)
    dub, dzb, dcw, dcb, dlw, dlb, dpw, dpb = rb_bwd(
        f"conv_bwd_{l}", functools.partial(conv_block, cfg), n, tbb, T // tbb, rows, halos,
        [lp['cw'], lp['cb'], lp['lw'], lp['lb'], lp['pw'], lp['pb']], [dob], [True] * 6)
    rows, halos = _attn_io(cfg, p, cos, sin, True)
    dqa, dza, dka, dva, dqnw, dknw, dsinks_row = rb_bwd(
        f"attn_bwd_{l}", functools.partial(attn_block, cfg), n, ATTN_BLOCK, T // ATTN_BLOCK, rows, halos,
        [lp['qnw'], lp['knw'], lp['sinks_row']], [doa], [True] * 3)
    dgroups = dict(qa=dqa, za=dza, qc=dqc, kc=dkc, vc=dvc, zc=dzc, ka=dka, va=dva, ub=dub, zb=dzb, bc=dbc, ac=dac)
    dp = jnp.concatenate([dgroups[k] for k in cfg.order], axis=1)
    dh = matmul(f"dh_{l}", dp, wp, 'nt', 512, 512, _tile(cfg.WP, 2304))
    dwp = matmul(f"dwp_{l}", sv['h'], dp, 'tn', 512, _tile(cfg.WP, 1152), 1024)
    dx, dnw = norm_bwd(f"norm_bwd_{l}", sv['x'], lp['nw'], dh, dxn, 256)
    grads = dict(
        norm_w=dnw[0], w_in_perm=dwp, q_norm_w=dqnw[0], k_norm_w=dknw[0],
        sinks=dsinks_row.reshape(cfg.AQH, A_HEAD)[:, 0],
        b_conv_w=dcw, b_conv_b=dcb[0], b_ln_w=dlw[0], b_ln_b=dlb[0], b_pw_w=dpw, b_pw_b=dpb[0],
        c_conv_w=dccw, c_a_log=dalog[0, :cfg.CH], c_dt_bias=ddtb[0, :cfg.CH], c_onorm_w=donw[0], w_out=dwo)
    return dx, grads


def local_step(cfg, x, positions, prm, wps, wos, target):
    nseq = x.shape[0]
    n = nseq * cfg.T
    x2 = x.reshape(n, cfg.D)
    inv_freq = ROPE_THETA ** (-np.arange(0, ROT_DIM, 2, dtype=np.float32) / ROT_DIM)
    freq_row = np.zeros((1, A_HEAD), np.float32)
    freq_row[0, :ROT_DIM] = np.concatenate([inv_freq, inv_freq])
    cos, sin = rope_tables("rope_tables", positions.reshape(n, 1), jnp.asarray(freq_row))
    lps = [_layer_params(cfg, prm, l) for l in range(DEPTH)]
    saved = []
    xl = x2
    for l in range(DEPTH):
        xl, sv = layer_forward(cfg, l, xl, lps[l], wps[l], wos[l], cos, sin)
        saved.append(sv)
    dx, loss = loss_grad("loss_grad", xl, target.reshape(n, cfg.D), 256)
    grads = [None] * DEPTH
    for l in reversed(range(DEPTH)):
        dx, grads[l] = layer_backward(cfg, l, dx, saved[l], lps[l], wps[l], wos[l], cos, sin)
    return loss, dx.reshape(x.shape), grads


ANY = pl.BlockSpec(memory_space=pl.ANY)
N_CHIPS = 4
N_DEV = 8


def _place():
    return lax.axis_index("x"), lax.axis_index("y"), lax.axis_index("c")


def _other_chips(x, y):
    return [(1 - x, y), (x, 1 - y), (1 - x, 1 - y)]


def _remote(src, dst, send, recv, to):
    return pltpu.make_async_remote_copy(src_ref=src, dst_ref=dst, send_sem=send, recv_sem=recv, device_id=to,
                                        device_id_type=MESH)


def all_gather_chips(name, arrs):
    n = len(arrs)

    def body(*refs):
        ins, outs = refs[:n], refs[n:2 * n]
        send, recv = refs[2 * n:]
        x, y, c = _place()
        me = 2 * x + y
        chips = _other_chips(x, y)
        sib = (x, y, 1 - c)
        first =[_remote(ins[i].at[c], outs[i].at[me, c], send.at[i, j], recv.at[i, j], (cx, cy, c))
                 for i in range(n) for j, (cx, cy) in enumerate(chips)]
        for cp in first:
            cp.start()
        passed = []
        for i in range(n):
            for j, (cx, cy) in enumerate(chips):
                blk = outs[i].at[2 * cx + cy, c]
                _remote(blk, blk, send.at[i, j], recv.at[i, j], (x, y, c)).wait_recv()
                cp = _remote(blk, blk, send.at[i, 3 + j], recv.at[i, 3 + j], sib)
                cp.start()
                passed.append(cp)
        for i in range(n):
            for j, (cx, cy) in enumerate(chips):
                blk = outs[i].at[2 * cx + cy, 1 - c]
                _remote(blk, blk, send.at[i, 3 + j], recv.at[i, 3 + j], sib).wait_recv()
        for cp in first + passed:
            cp.wait_send()

    outs = pl.pallas_call(
        body, name=name, in_specs=[ANY] * n, out_specs=[ANY] * n,
        out_shape=[jax.ShapeDtypeStruct((N_CHIPS,) + a.shape, a.dtype) for a in arrs],
        scratch_shapes=[pltpu.SemaphoreType.DMA((n, 6)), pltpu.SemaphoreType.DMA((n, 6))],
    )(*arrs)
    me = 2 * lax.axis_index("x") + lax.axis_index("y")
    return [lax.dynamic_update_index_in_dim(o, a, me, 0) for o, a in zip(outs, arrs)]


def swap_layers(name, arrs):
    n = len(arrs)

    def body(*refs):
        ins, outs = refs[:n], refs[n:2 * n]
        send, recv = refs[2 * n:]
        x, y, c = _place()
        cps = [_remote(ins[i].at[1 - c], outs[i], send.at[i], recv.at[i], (x, y, 1 - c)) for i in range(n)]
        for cp in cps:
            cp.start()
        for cp in cps:
            cp.wait()

    return pl.pallas_call(
        body, name=name, in_specs=[ANY] * n, out_specs=[ANY] * n,
        out_shape=[jax.ShapeDtypeStruct(a.shape[1:], a.dtype) for a in arrs],
        scratch_shapes=[pltpu.SemaphoreType.DMA((n,)), pltpu.SemaphoreType.DMA((n,))],
    )(*arrs)


def scatter_chips(name, arrs):
    n = len(arrs)

    def body(*refs):
        ins, outs = refs[:n], refs[n:2 * n]
        send, recv = refs[2 * n:]
        x, y, c = _place()
        chips = _other_chips(x, y)
        cps = [_remote(ins[i].at[2 * cx + cy], outs[i].at[j], send.at[i, j], recv.at[i, j], (cx, cy, c))
               for i in range(n) for j, (cx, cy) in enumerate(chips)]
        for cp in cps:
            cp.start()
        for i in range(n):
            for j in range(3):
                blk = outs[i].at[j]
                _remote(blk, blk, send.at[i, j], recv.at[i, j], (x, y, c)).wait_recv()
        for cp in cps:
            cp.wait_send()

    return pl.pallas_call(
        body, name=name, in_specs=[ANY] * n, out_specs=[ANY] * n,
        out_shape=[jax.ShapeDtypeStruct((3,) + a.shape[1:], a.dtype) for a in arrs],
        scratch_shapes=[pltpu.SemaphoreType.DMA((n, 3)), pltpu.SemaphoreType.DMA((n, 3))],
    )(*arrs)


def share_layers(name, arrs):
    n = len(arrs)

    def body(*refs):
        outs = refs[n:2 * n]
        send, recv = refs[2 * n:]
        x, y, c = _place()
        cps = [_remote(outs[i].at[c], outs[i].at[c], send.at[i], recv.at[i], (x, y, 1 - c)) for i in range(n)]
        for cp in cps:
            cp.start()
        for i in range(n):
            blk = outs[i].at[1 - c]
            _remote(blk, blk, send.at[i], recv.at[i], (x, y, c)).wait_recv()
        for cp in cps:
            cp.wait_send()

    return pl.pallas_call(
        body, name=name, in_specs=[ANY] * n, out_specs=[ANY] * n,
        out_shape=[jax.ShapeDtypeStruct(a.shape, a.dtype) for a in arrs],
        input_output_aliases={i: i for i in range(n)},
        scratch_shapes=[pltpu.SemaphoreType.DMA((n,)), pltpu.SemaphoreType.DMA((n,))],
    )(*arrs)


def all_reduce_small(name, packed):
    r = packed.shape[0]

    def body(in_ref, out_ref, buf, send, recv):
        x, y, c = _place()
        me = 4 * x + 2 * y + c
        buf[me] = in_ref[...]
        flips = [(fx, fy, fc) for fx in (0, 1) for fy in (0, 1) for fc in (0, 1) if (fx, fy, fc) != (0, 0, 0)]
        peers = [((x + fx) % 2, (y + fy) % 2, (c + fc) % 2) for fx, fy, fc in flips]
        cps = [_remote(in_ref, buf.at[me], send.at[k], recv.at[k], peer) for k, peer in enumerate(peers)]
        for cp in cps:
            cp.start()
        for k, (px, py, pc) in enumerate(peers):
            blk = buf.at[4 * px + 2 * py + pc]
            _remote(blk, blk, send.at[k], recv.at[k], (x, y, c)).wait_recv()
        for cp in cps:
            cp.wait_send()
        acc = buf[0]
        for d in range(1, N_DEV):
            acc = acc + buf[d]
        out_ref[...] = acc

    vm = pl.BlockSpec(memory_space=pltpu.VMEM)
    return pl.pallas_call(
        body, name=name, in_specs=[vm], out_specs=vm, out_shape=jax.ShapeDtypeStruct(packed.shape, F32),
        scratch_shapes=[pltpu.VMEM((N_DEV, r, LANE), F32), pltpu.SemaphoreType.DMA((N_DEV - 1,)),
                        pltpu.SemaphoreType.DMA((N_DEV - 1,))],
    )(packed)


def add_own_layer(name, g, a, c_idx, tr):
    _, nch, r, cc = g.shape
    tr = min(tr, r)

    def body(c_ref, g_ref, a_ref, o_ref):
        o_ref[...] = (g_ref[0] + a_ref[...]).astype(o_ref.dtype)

    return pl.pallas_call(
        body, name=name,
        grid_spec=pltpu.PrefetchScalarGridSpec(
            num_scalar_prefetch=1, grid=(nch, r // tr),
            in_specs=[pl.BlockSpec((1, 1, tr, cc), lambda j, i, c_ref: (c_ref[0], j, i, 0)),
                      pl.BlockSpec((1, tr, cc), lambda j, i, c_ref: (j, i, 0))],
            out_specs=pl.BlockSpec((1, tr, cc), lambda j, i, c_ref: (j, i, 0))),
        out_shape=jax.ShapeDtypeStruct(a.shape, BF16),
        compiler_params=_cparams(("parallel", "parallel")),
    )(c_idx, g, a)


def sum_chips(name, p, b, idx, tr):
    _, r, cc = p.shape
    tr = min(tr, r)

    def body(idx_ref, p_ref, b_ref, o_ref):
        acc = p_ref[0].astype(F32)
        for k in range(3):
            acc = acc + b_ref[k].astype(F32)
        o_ref[0] = acc

    return pl.pallas_call(
        body, name=name,
        grid_spec=pltpu.PrefetchScalarGridSpec(
            num_scalar_prefetch=1, grid=(r // tr,),
            in_specs=[pl.BlockSpec((1, tr, cc), lambda i, s: (s[0], i, 0)),
                      pl.BlockSpec((3, tr, cc), lambda i, s: (0, i, 0))],
            out_specs=pl.BlockSpec((1, tr, cc), lambda i, s: (s[1], i, 0))),
        out_shape=jax.ShapeDtypeStruct((2, r, cc), F32),
        compiler_params=_cparams(("parallel",)),
    )(idx, p, b)


def reduce_scatter_grads(arrs, chip, c_idx):
    tags = range(len(arrs))
    idx = jnp.concatenate([chip.astype(jnp.int32).reshape(1), c_idx])
    recv = swap_layers("rs_swap_layers", arrs)
    part = [add_own_layer(f"rs_add_sibling_{t}", g, a, c_idx, 128) for t, g, a in zip(tags, arrs, recv)]
    got = scatter_chips("rs_scatter_chips", part)
    red = [sum_chips(f"rs_sum_chips_{t}", p, b, idx, 128) for t, p, b in zip(tags, part, got)]
    return share_layers("rs_share_layers", red)


def adamw(name, w, g, m, v, tb):
    r, cc = w.shape
    tb = min(tb, r)
    assert r % tb == 0

    def body(w_ref, g_ref, m_ref, v_ref, d_ref, mo_ref, vo_ref):
        g = g_ref[...]
        m = ADAM_B1 * m_ref[...] + (1.0 - ADAM_B1) * g
        v = ADAM_B2 * v_ref[...] + (1.0 - ADAM_B2) * jnp.square(g)
        m_hat = m / (1.0 - ADAM_B1 ** ADAM_STEP)
        v_hat = v / (1.0 - ADAM_B2 ** ADAM_STEP)
        d_ref[...] = -ADAM_LR * (m_hat / (jnp.sqrt(v_hat) + ADAM_EPS) + ADAM_WD * w_ref[...])
        mo_ref[...] = m
        vo_ref[...] = v

    spec = pl.BlockSpec((tb, cc), lambda i: (i, 0))
    return pl.pallas_call(
        body, name=name, grid=(r // tb,), in_specs=[spec] * 4, out_specs=[spec] * 3,
        out_shape=[jax.ShapeDtypeStruct((r, cc), F32)] * 3,
        compiler_params=_cparams(("parallel",)),
    )(w, g, m, v)


def _pack(arrs):
    flat = jnp.concatenate([a.reshape(-1).astype(F32) for a in arrs])
    pad = (-flat.shape[0]) % (8 * LANE)
    return jnp.pad(flat, (0, pad)).reshape(-1, LANE)


def _unpack(packed, shapes):
    flat = packed.reshape(-1)
    out, off = [], 0
    for s in shapes:
        size = math.prod(s)
        out.append(flat[off:off + size].reshape(s))
        off += size
    return out


BIG = ('w_in', 'w_out', 'b_pw_w')
SMALL = tuple(k for k in WEIGHTS if k not in BIG)
CHIP_SHARDED_SMALL = {'b_conv_w': 2, 'c_conv_w': 2}


def kernel(x, positions, norm_w, w_in, q_norm_w, k_norm_w, sinks, b_conv_w, b_conv_b, b_ln_w, b_ln_b, b_pw_w, b_pw_b, c_conv_w, c_a_log, c_dt_bias, c_onorm_w, w_out, loss_target, m_norm_w, m_w_in, m_q_norm_w, m_k_norm_w, m_sinks, m_b_conv_w, m_b_conv_b, m_b_ln_w, m_b_ln_b, m_b_pw_w, m_b_pw_b, m_c_conv_w, m_c_a_log, m_c_dt_bias, m_c_onorm_w, m_w_out, v_norm_w, v_w_in, v_q_norm_w, v_k_norm_w, v_sinks, v_b_conv_w, v_b_conv_b, v_b_ln_w, v_b_ln_b, v_b_pw_w, v_b_pw_b, v_c_conv_w, v_c_a_log, v_c_dt_bias, v_c_onorm_w, v_w_out):
    cfg = Cfg(x.shape[-1], x.shape[-2])
    w = dict(norm_w=norm_w, w_in=w_in, q_norm_w=q_norm_w, k_norm_w=k_norm_w, sinks=sinks, b_conv_w=b_conv_w,
             b_conv_b=b_conv_b, b_ln_w=b_ln_w, b_ln_b=b_ln_b, b_pw_w=b_pw_w, b_pw_b=b_pw_b, c_conv_w=c_conv_w,
             c_a_log=c_a_log, c_dt_bias=c_dt_bias, c_onorm_w=c_onorm_w, w_out=w_out)
    m = dict(norm_w=m_norm_w, w_in=m_w_in, q_norm_w=m_q_norm_w, k_norm_w=m_k_norm_w, sinks=m_sinks,
             b_conv_w=m_b_conv_w, b_conv_b=m_b_conv_b, b_ln_w=m_b_ln_w, b_ln_b=m_b_ln_b, b_pw_w=m_b_pw_w,
             b_pw_b=m_b_pw_b, c_conv_w=m_c_conv_w, c_a_log=m_c_a_log, c_dt_bias=m_c_dt_bias, c_onorm_w=m_c_onorm_w,
             w_out=m_w_out)
    v = dict(norm_w=v_norm_w, w_in=v_w_in, q_norm_w=v_q_norm_w, k_norm_w=v_k_norm_w, sinks=v_sinks,
             b_conv_w=v_b_conv_w, b_conv_b=v_b_conv_b, b_ln_w=v_b_ln_w, b_ln_b=v_b_ln_b, b_pw_w=v_b_pw_w,
             b_pw_b=v_b_pw_b, c_conv_w=v_c_conv_w, c_a_log=v_c_a_log, c_dt_bias=v_c_dt_bias, c_onorm_w=v_c_onorm_w,
             w_out=v_w_out)
    chip = 2 * lax.axis_index("x") + lax.axis_index("y")
    c_idx = lax.axis_index("c").astype(jnp.int32).reshape(1)

    g_in, g_out, g_pw, g_bcw, g_ccw = all_gather_chips(
        "gather_weights", [w_in.astype(BF16), w_out.astype(BF16), b_pw_w, b_conv_w, c_conv_w])
    full = dict(w)
    full['b_pw_w'] = jnp.stack([g_pw[:, l].reshape(cfg.BW, cfg.BW) for l in range(DEPTH)])
    full['b_conv_w'] = jnp.stack([jnp.concatenate(list(g_bcw[:, l]), axis=1) for l in range(DEPTH)])
    full['c_conv_w'] = jnp.stack([jnp.concatenate(list(g_ccw[:, l]), axis=1) for l in range(DEPTH)])
    wps = [permute_w_in(cfg, jnp.concatenate(list(g_in[:, l]), axis=1)) for l in range(DEPTH)]
    wos = [g_out[:, l].reshape(cfg.D, cfg.D) for l in range(DEPTH)]

    loss_local, grad_x, grads = local_step(cfg, x, positions, full, wps, wos, loss_target)
    loss = lax.psum(loss_local, ("x", "y", "c"))

    shard_cols = cfg.IN_COLS // N_CHIPS
    p_in = jnp.stack([unpermute_w_in(cfg, grads[l]['w_in_perm']).reshape(cfg.D, N_CHIPS, shard_cols).transpose(1, 0, 2)
                      for l in range(DEPTH)])
    p_out = jnp.stack([grads[l]['w_out'].reshape(N_CHIPS, cfg.D // N_CHIPS, cfg.D) for l in range(DEPTH)])
    p_pw = jnp.stack([grads[l]['b_pw_w'].reshape(N_CHIPS, cfg.BW // N_CHIPS, cfg.BW) for l in range(DEPTH)])
    r_in, r_out, r_pw = reduce_scatter_grads([p_in, p_out, p_pw], chip, c_idx)
    small_parts = [jnp.stack([grads[l][k] for l in range(DEPTH)]) for k in SMALL]
    small_red = _unpack(all_reduce_small("all_reduce_small", _pack(small_parts)), [a.shape for a in small_parts])
    g = dict(w_in=r_in, w_out=r_out, b_pw_w=r_pw)
    for k, a in zip(SMALL, small_red):
        if k in CHIP_SHARDED_SMALL:
            ax = CHIP_SHARDED_SMALL[k]
            width = a.shape[ax] // N_CHIPS
            a = lax.dynamic_slice_in_dim(a, chip * width, width, axis=ax)
        g[k] = a

    delta, new_m, new_v = {}, {}, {}
    for k in BIG:
        cols = w[k].shape[-1]
        outs = adamw(f"adamw_{k}", w[k].reshape(-1, cols), g[k].reshape(-1, cols), m[k].reshape(-1, cols),
                     v[k].reshape(-1, cols), 128)
        delta[k], new_m[k], new_v[k] = [o.reshape(w[k].shape) for o in outs]
    shapes = [w[k].shape for k in SMALL]
    outs = adamw("adamw_small", _pack([w[k] for k in SMALL]), _pack([g[k] for k in SMALL]),
                 _pack([m[k] for k in SMALL]), _pack([v[k] for k in SMALL]), 4096)
    for name_, packed in zip(('delta', 'm', 'v'), outs):
        for k, a in zip(SMALL, _unpack(packed, shapes)):
            {'delta': delta, 'm': new_m, 'v': new_v}[name_][k] = a
    return (loss, grad_x, *[g[k] for k in WEIGHTS], *[delta[k] for k in WEIGHTS], *[new_m[k] for k in WEIGHTS],
            *[new_v[k] for k in WEIGHTS])
```

```python
import functools
import math

import numpy as np
import jax
import jax.numpy as jnp
from jax import lax
from jax.experimental import pallas as pl
from jax.experimental.pallas import tpu as pltpu

F32 = jnp.float32
BF16 = jnp.bfloat16
HI = lax.Precision.HIGHEST
MESH = pl.DeviceIdType.MESH

DEPTH = 2
A_HEAD = 64
A_GROUP = 3
ATTN_BLOCK = 128
ROT_DIM = 16
ROPE_THETA = 500000.0
B_CONV = 31
B_HALO = 32
C_HEAD = 128
C_CONV = 4
C_HALO = 8
CHUNK = 64
EPS = 1e-6
LANE = 128

ADAM_LR = 0.001
ADAM_B1 = 0.9
ADAM_B2 = 0.999
ADAM_EPS = 1e-08
ADAM_WD = 0.01
ADAM_STEP = 10

VMEM_LIMIT = 56 * 1024 * 1024

WEIGHTS = ['norm_w', 'w_in', 'q_norm_w', 'k_norm_w', 'sinks', 'b_conv_w', 'b_conv_b', 'b_ln_w', 'b_ln_b',
           'b_pw_w', 'b_pw_b', 'c_conv_w', 'c_a_log', 'c_dt_bias', 'c_onorm_w', 'w_out']


class Cfg:
    def __init__(self, d_model=2048, seq=2048):
        self.D = d_model
        self.T = seq
        self.AW = 3 * d_model // 8
        self.AQH = self.AW // A_HEAD
        self.AKH = self.AQH // A_GROUP
        self.AKW = self.AKH * A_HEAD
        self.BW = d_model // 4
        self.CH = (d_model - self.AW - self.BW) // C_HEAD
        self.CW = self.CH * C_HEAD
        AW, AKW, BW, CW, CH = self.AW, self.AKW, self.BW, self.CW, self.CH
        orig = [('qa', AW), ('ka', AKW), ('va', AKW), ('za', AW), ('ub', 2 * BW), ('zb', BW),
                ('qc', CW), ('kc', CW), ('vc', CW), ('bc', CH), ('ac', CH), ('zc', CW)]
        self.orig = {}
        off = 0
        for n, w in orig:
            self.orig[n] = (off, w)
            off += w
        self.IN_COLS = off
        order = ['qa', 'za', 'qc', 'kc', 'vc', 'zc', 'ka', 'va', 'ub', 'zb', 'bc', 'ac']
        self.order = order
        self.g = {}
        off = 0
        for n in order:
            w = self.orig[n][1]
            wp = LANE if n in ('bc', 'ac') else w
            assert off % wp == 0, (n, off, wp)
            self.g[n] = (off, wp)
            off += wp
        self.WP = off

    def blk(self, name):
        off, w = self.g[name]
        return off // w


def _cparams(sem, vmem=VMEM_LIMIT):
    return pltpu.CompilerParams(dimension_semantics=sem, vmem_limit_bytes=vmem)


def _silu(x):
    return x * jax.nn.sigmoid(x)


ANY = pl.BlockSpec(memory_space=pl.ANY)


class Comm:
    def __init__(self, ins, out_shapes, sems, start, finish, aliases=None):
        self.ins, self.out_shapes, self.sems = list(ins), list(out_shapes), list(sems)
        self.start, self.finish, self.aliases = start, finish, dict(aliases or {})


def call_with_comm(body, name, grid, in_specs, out_specs, out_shape, scratch_shapes, semantics, args, comm=None):
    in_specs, out_specs, out_shape, scratch_shapes = list(in_specs), list(out_specs), list(out_shape), list(scratch_shapes)
    if comm is None:
        outs = pl.pallas_call(body, name=name, grid=grid, in_specs=in_specs, out_specs=out_specs, out_shape=out_shape,
                              scratch_shapes=scratch_shapes, compiler_params=_cparams(semantics))(*args)
        return list(outs), []
    ni, no, ns = len(in_specs), len(out_specs), len(scratch_shapes)
    nci, nco = len(comm.ins), len(comm.out_shapes)

    def wrapped(*refs):
        h_in, c_in = refs[:ni], refs[ni:ni + nci]
        h_out, c_out = refs[ni + nci:ni + nci + no], refs[ni + nci + no:ni + nci + no + nco]
        h_scr, c_sems = refs[ni + nci + no + nco:ni + nci + no + nco + ns], refs[ni + nci + no + nco + ns:]
        ids = [pl.program_id(d) for d in range(len(grid))]
        first = functools.reduce(jnp.logical_and, [i == 0 for i in ids])
        last = functools.reduce(jnp.logical_and, [i == g - 1 for i, g in zip(ids, grid)])

        @pl.when(first)
        def _():
            comm.start(c_in, c_out, c_sems)

        body(*h_in, *h_out, *h_scr)

        @pl.when(last)
        def _():
            comm.finish(c_in, c_out, c_sems)

    outs = pl.pallas_call(
        wrapped, name=name, grid=grid, in_specs=in_specs + [ANY] * nci, out_specs=out_specs + [ANY] * nco,
        out_shape=out_shape + comm.out_shapes, scratch_shapes=scratch_shapes + comm.sems,
        input_output_aliases={ni + k: no + v for k, v in comm.aliases.items()},
        compiler_params=_cparams(("arbitrary",) * len(grid)),
    )(*args, *comm.ins)
    return list(outs[:no]), list(outs[no:])


def run_comm(name, comm):
    nci, nco = len(comm.ins), len(comm.out_shapes)

    def body(*refs):
        c_in, c_out, c_sems = refs[:nci], refs[nci:nci + nco], refs[nci + nco:]
        comm.start(c_in, c_out, c_sems)
        comm.finish(c_in, c_out, c_sems)

    return pl.pallas_call(
        body, name=name, in_specs=[ANY] * nci, out_specs=[ANY] * nco, out_shape=comm.out_shapes,
        scratch_shapes=comm.sems, input_output_aliases=comm.aliases,
    )(*comm.ins)


def _bdot(a, b, ca, cb, precision=HI):
    return lax.dot_general(a, b, (((ca,), (cb,)), ((0,), (0,))), precision=precision, preferred_element_type=F32)


def _rope_matrix(nb):
    i = lax.broadcasted_iota(jnp.int32, (nb, A_HEAD, A_HEAD), 1)
    j = lax.broadcasted_iota(jnp.int32, (nb, A_HEAD, A_HEAD), 2)
    half = ROT_DIM // 2
    neg = (j < half) & (i == j + half)
    pos = (j >= half) & (j < ROT_DIM) & (i == j - half)
    return jnp.where(neg, -1.0, jnp.where(pos, 1.0, 0.0)).astype(F32)


def _norm_rope(xh, w, cos, sin):
    y = xh * lax.rsqrt(jnp.mean(xh * xh, axis=-1, keepdims=True) + EPS) * w
    return y * cos + _bdot(y, _rope_matrix(xh.shape[0]), 2, 1) * sin


def attn_block(cfg, first, q, za, kc, vc, cosc, sinc, kp, vp, cosp, sinp, qnw, knw, sinks_row):
    blk = ATTN_BLOCK
    nq, nk = cfg.AQH, cfg.AKH
    qi = lax.broadcasted_iota(jnp.int32, (blk, 2 * blk), 0)
    kj = lax.broadcasted_iota(jnp.int32, (blk, 2 * blk), 1)
    dist = qi + blk - kj
    valid = ((dist >= 0) & (dist < blk) & (jnp.logical_not(first) | (kj >= blk)))[None]
    cos2 = jnp.concatenate([cosp, cosc], axis=0)
    sin2 = jnp.concatenate([sinp, sinc], axis=0)
    head = lambda x, h: x[:, A_HEAD * h:A_HEAD * (h + 1)]
    k2 = jnp.stack([jnp.concatenate([head(kp, h), head(kc, h)], axis=0) for h in range(nk)], axis=0)
    v2 = jnp.stack([jnp.concatenate([head(vp, h), head(vc, h)], axis=0) for h in range(nk)], axis=0)
    k2 = _norm_rope(k2, knw[None], cos2[None], sin2[None]).astype(BF16)
    v2 = v2.astype(BF16)
    k2 = jnp.stack([k2[h // A_GROUP] for h in range(nq)], axis=0)
    v2 = jnp.stack([v2[h // A_GROUP] for h in range(nq)], axis=0)
    qh = jnp.stack([head(q, h) for h in range(nq)], axis=0)
    qh = _norm_rope(qh, qnw[None], cosc[None], sinc[None]).astype(BF16)
    s = _bdot(qh, k2, 2, 2, None) * (A_HEAD ** -0.5)
    s = jnp.where(valid, s, -1e30)
    sink = jnp.stack([sinks_row[:, A_HEAD * h:A_HEAD * h + 1] for h in range(nq)], axis=0)
    m = jnp.maximum(jnp.max(s, axis=-1, keepdims=True), sink)
    e = jnp.exp(s - m)
    den = jnp.sum(e, axis=-1, keepdims=True) + jnp.exp(sink - m)
    o = _bdot((e / den).astype(BF16), v2, 2, 1, None)
    return (jnp.concatenate([o[h] for h in range(nq)], axis=1) * _silu(za),)


def conv_block(cfg, first, u, zb, uh, cw, cb, lw, lb, pw, pb):
    BW = cfg.BW
    tb = u.shape[0]
    uu = jnp.concatenate([uh, u], axis=0)
    h = uu[:, :BW] * jax.nn.sigmoid(uu[:, BW:])
    row = lax.broadcasted_iota(jnp.int32, h.shape, 0)
    h = jnp.where(first & (row < B_HALO), 0.0, h)
    acc = jnp.zeros((tb, BW), F32) + cb
    base = B_HALO - (B_CONV - 1)
    for k in range(B_CONV):
        acc = acc + cw[k:k + 1, :] * h[base + k:base + k + tb, :]
    mu = jnp.mean(acc, axis=-1, keepdims=True)
    var = jnp.mean(jnp.square(acc - mu), axis=-1, keepdims=True)
    y = (acc - mu) * lax.rsqrt(var + EPS) * lw + lb
    s = _silu(y)
    o = jnp.dot(s.astype(BF16), pw.astype(BF16), preferred_element_type=F32) + pb
    return (o * _silu(zb),)


def gdn_prep_block(cfg, first, xq, xk, xv, braw, araw, hq, hk, hv, cw, alog, dtb):
    CW = cfg.CW
    tb = xq.shape[0]
    outs = []
    for idx, (x, xh) in enumerate(((xq, hq), (xk, hk), (xv, hv))):
        xx = jnp.concatenate([jnp.where(first, 0.0, xh), x], axis=0)
        w = cw[:, idx * CW:(idx + 1) * CW]
        acc = jnp.zeros((tb, CW), F32)
        base = C_HALO - (C_CONV - 1)
        for k in range(C_CONV):
            acc = acc + w[k:k + 1, :] * xx[base + k:base + k + tb, :]
        y = _silu(acc)
        if idx < 2:
            parts = []
            for h in range(cfg.CH):
                yh = y[:, C_HEAD * h:C_HEAD * (h + 1)]
                parts.append(yh * lax.rsqrt(jnp.sum(yh * yh, axis=-1, keepdims=True) + EPS))
            y = jnp.concatenate(parts, axis=1)
        outs.append(y)
    beta = jax.nn.sigmoid(braw)
    g = -jnp.exp(alog) * jax.nn.softplus(araw + dtb)
    return outs[0], outs[1], outs[2], g, beta


def _inverse_unit_lower(low, eye):
    pw = low
    inv = eye - low
    for _ in range(5):
        pwb = pw.astype(BF16)
        pw = _bdot(pwb, pwb, 2, 1, None)
        inv = inv + _bdot(inv.astype(BF16), pw.astype(BF16), 2, 1, None)
    ax = inv + _bdot(low, inv, 2, 1)
    return inv + _bdot(inv, eye - ax, 2, 1)


@jax.custom_vjp
def _saved_inverse(low, inv):
    return inv


def _saved_inverse_fwd(low, inv):
    return inv, inv


def _saved_inverse_bwd(inv, d):
    dlow = -_bdot(_bdot(inv, d, 1, 1), inv, 2, 2)
    return dlow, jnp.zeros_like(inv)


_saved_inverse.defvjp(_saved_inverse_fwd, _saved_inverse_bwd)


def gdn_intra_rows(cfg, first, qn, kn, v, g, beta, inv_saved=None):
    c = CHUNK
    CH = cfg.CH
    nchunk = qn.shape[0] // c
    i = lax.broadcasted_iota(jnp.int32, (c, c), 0)
    j = lax.broadcasted_iota(jnp.int32, (c, c), 1)
    incl = (i >= j)[None]
    strict = (i > j)[None]
    eye = (i == j).astype(F32)[None]
    tri = (i >= j).astype(F32)
    rows = [slice(c * ci, c * (ci + 1)) for ci in range(nchunk)]
    gcs = [jnp.dot(tri, g[r], precision=HI, preferred_element_type=F32) for r in rows]
    pairs = [(ci, h) for ci in range(nchunk) for h in range(CH)]
    heads = lambda x, wd: jnp.stack([x[rows[ci], wd * h:wd * (h + 1)] for ci, h in pairs], axis=0)
    gch = jnp.stack([gcs[ci][:, h:h + 1] for ci, h in pairs], axis=0)
    bh = jnp.stack([beta[rows[ci], h:h + 1] for ci, h in pairs], axis=0)
    q = heads(qn, C_HEAD) * (C_HEAD ** -0.5)
    k = heads(kn, C_HEAD)
    vv = heads(v, C_HEAD)
    a = jnp.broadcast_to(gch, (len(pairs), c, c))
    diff = jnp.where(incl, a - jnp.swapaxes(a, 1, 2), 0.0)
    decay = jnp.where(incl, jnp.exp(diff), 0.0)
    kb = k * bh
    low = jnp.where(strict, _bdot(kb, k, 2, 2) * decay, 0.0)
    if inv_saved is None:
        inv = _inverse_unit_lower(low, eye)
    else:
        inv = _saved_inverse(low, heads(inv_saved, c))
    eg = jnp.exp(gch)
    sol = _bdot(inv, jnp.concatenate([vv * bh, kb * eg], axis=2), 2, 1)
    intra = jnp.where(incl, _bdot(q, k, 2, 2) * decay, 0.0)
    qg = q * eg
    kd = k * jnp.exp(gch[:, c - 1:c, :] - gch)
    glast = jnp.concatenate([jnp.broadcast_to(gc[c - 1:c, :], gc.shape) for gc in gcs], axis=0)

    def unstack(x):
        return jnp.concatenate([jnp.concatenate([x[ci * CH + h] for h in range(CH)], axis=1) for ci in range(nchunk)],
                               axis=0)

    outs = (unstack(sol[:, :, :C_HEAD]), unstack(sol[:, :, C_HEAD:]), unstack(qg), unstack(kd), unstack(intra), glast)
    return outs + (unstack(inv),) if inv_saved is None else outs


def gdn_state_step(S, u, w, qg, kd, intra, glast):
    v_new = u - _bdot(w, S, 2, 1)
    o = _bdot(qg, S, 2, 1) + _bdot(intra, v_new, 2, 1)
    S_next = S * jnp.exp(glast) + _bdot(kd, v_new, 1, 1)
    return o, S_next


def gdn_out_block(cfg, first, o, zc, onw):
    parts = []
    for h in range(cfg.CH):
        sl = slice(C_HEAD * h, C_HEAD * (h + 1))
        oh = o[:, sl]
        y = oh * lax.rsqrt(jnp.mean(oh * oh, axis=-1, keepdims=True) + EPS) * onw
        parts.append(y * _silu(zc[:, sl]))
    return (jnp.concatenate(parts, axis=1),)


def rms_block(x, nw):
    return x * lax.rsqrt(jnp.mean(x * x, axis=-1, keepdims=True) + EPS) * nw


class Row:
    def __init__(self, arr, width, colblk=0, grad=None):
        self.arr, self.width, self.colblk, self.grad = arr, width, colblk, grad


class Halo:
    def __init__(self, arr, width, colblk, hr, tie=None):
        self.arr, self.width, self.colblk, self.hr, self.tie = arr, width, colblk, hr, tie


def _row_specs(tb, rows, halos, params, pos):
    specs = [pl.BlockSpec((tb, r.width), lambda i, cb=r.colblk: (pos(i), cb)) for r in rows]
    specs += [pl.BlockSpec((h.hr, h.width),
                           lambda i, cb=h.colblk, m=tb // h.hr: (jnp.maximum(pos(i) * m - 1, 0), cb))
              for h in halos]
    specs += [pl.BlockSpec(p.shape, lambda i: (0, 0)) for p in params]
    return specs


def rb_fwd(name, fn, n, tb, bps, rows, halos, params, outs):
    nr, nh, npar = len(rows), len(halos), len(params)

    def body(*refs):
        ins = refs[:nr + nh + npar]
        o_refs = refs[nr + nh + npar:]
        first = (pl.program_id(0) % bps) == 0
        res = fn(first, *[r[...] for r in ins])
        for ref, val in zip(o_refs, res):
            ref[...] = val.astype(ref.dtype)

    return pl.pallas_call(
        body, name=name, grid=(n // tb,),
        in_specs=_row_specs(tb, rows, halos, params, lambda i: i),
        out_specs=[pl.BlockSpec((tb, w), lambda i: (i, 0)) for w, _ in outs],
        out_shape=[jax.ShapeDtypeStruct((n, w), dt) for w, dt in outs],
        compiler_params=_cparams(("parallel",)),
    )(*[r.arr for r in rows], *[h.arr for h in halos], *params)


def rb_bwd(name, fn, n, tb, bps, rows, halos, params, douts, param_grads):
    nr, nh, npar, nd = len(rows), len(halos), len(params), len(douts)
    nblk = n // tb
    grow = [k for k, r in enumerate(rows) if r.grad is not None]
    ghalo = [k for k, h in enumerate(halos) if h.tie is not None]
    gpar = [k for k, f in enumerate(param_grads) if f]
    pos = lambda i: nblk - 1 - i

    def body(*refs):
        ins = refs[:nr + nh + npar]
        d_refs = refs[nr + nh + npar:nr + nh + npar + nd]
        rest = refs[nr + nh + npar + nd:]
        grow_refs = rest[:len(grow)]
        gpar_refs = rest[len(grow):len(grow) + len(gpar)]
        carry_refs = rest[len(grow) + len(gpar):]
        i = pl.program_id(0)
        first = (pos(i) % bps) == 0
        vals = [r[...] for r in ins]
        diff_idx = grow + [nr + k for k in ghalo] + [nr + nh + k for k in gpar]

        def f(*dargs):
            full = list(vals)
            for k, a in zip(diff_idx, dargs):
                full[k] = a
            return fn(first, *full)

        res, vjp = jax.vjp(f, *[vals[k] for k in diff_idx])
        grads = vjp(tuple(d[...].astype(r.dtype) for d, r in zip(d_refs, res)))
        g_rows = list(grads[:len(grow)])
        g_halos = grads[len(grow):len(grow) + len(ghalo)]
        g_pars = grads[len(grow) + len(ghalo):]

        @pl.when(i == 0)
        def _():
            for c in carry_refs:
                c[...] = jnp.zeros_like(c)
            for p in gpar_refs:
                p[...] = jnp.zeros_like(p)

        for k, ref in enumerate(grow_refs):
            ref[...] = g_rows[k].astype(ref.dtype)
        for ci, hk in enumerate(ghalo):
            h = halos[hk]
            k = grow.index(h.tie)
            tail = g_rows[k][tb - h.hr:, :] + carry_refs[ci][...]
            grow_refs[k][tb - h.hr:, :] = tail.astype(grow_refs[k].dtype)
            carry_refs[ci][...] = g_halos[ci]
        for ref, gp in zip(gpar_refs, g_pars):
            ref[...] += gp

    out_specs = [pl.BlockSpec((tb, rows[k].width), lambda i: (pos(i), 0)) for k in grow]
    out_specs += [pl.BlockSpec(params[k].shape, lambda i: (0, 0)) for k in gpar]
    out_shape = [jax.ShapeDtypeStruct((n, rows[k].width), rows[k].grad) for k in grow]
    out_shape += [jax.ShapeDtypeStruct(params[k].shape, F32) for k in gpar]
    in_specs = _row_specs(tb, rows, halos, params, pos)
    in_specs += [pl.BlockSpec((tb, d.shape[1]), lambda i: (pos(i), 0)) for d in douts]
    return pl.pallas_call(
        body, name=name, grid=(nblk,), in_specs=in_specs, out_specs=out_specs, out_shape=out_shape,
        scratch_shapes=[pltpu.VMEM((halos[k].hr, halos[k].width), F32) for k in ghalo],
        compiler_params=_cparams(("arbitrary",)),
    )(*[r.arr for r in rows], *[h.arr for h in halos], *params, *douts)


_DIMS = {'nn': (((1,), (0,)), ((), ())), 'nt': (((1,), (1,)), ((), ())), 'tn': (((0,), (0,)), ((), ()))}


def matmul(name, a, b, mode, tm, tn, tk, out_dtype=F32, add=None, comm=None):
    if mode == 'tn':
        K, M = a.shape
    else:
        M, K = a.shape
    N = b.shape[0] if mode == 'nt' else b.shape[1]
    tm, tn, tk = min(tm, M), min(tn, N), min(tk, K)
    assert M % tm == 0 and N % tn == 0 and K % tk == 0, (name, M, N, K, tm, tn, tk)
    nk = K // tk
    a_spec = pl.BlockSpec((tk, tm), lambda i, j, k: (k, i)) if mode == 'tn' else pl.BlockSpec((tm, tk), lambda i, j, k: (i, k))
    b_spec = pl.BlockSpec((tn, tk), lambda i, j, k: (j, k)) if mode == 'nt' else pl.BlockSpec((tk, tn), lambda i, j, k: (k, j))
    o_spec = pl.BlockSpec((tm, tn), lambda i, j, k: (i, j))
    has_add = add is not None

    def body(*refs):
        a_ref, b_ref = refs[0], refs[1]
        add_ref = refs[2] if has_add else None
        o_ref = refs[-1]
        k = pl.program_id(2)
        part = lax.dot_general(a_ref[...].astype(BF16), b_ref[...].astype(BF16), _DIMS[mode], preferred_element_type=F32)

        @pl.when(k == 0)
        def _():
            o_ref[...] = ((part + add_ref[...]) if has_add else part).astype(o_ref.dtype)

        if nk > 1:
            @pl.when(k > 0)
            def _():
                o_ref[...] += part

    assert nk == 1 or out_dtype == F32
    ins = [a, b] + ([add] if has_add else [])
    in_specs = [a_spec, b_spec] + ([o_spec] if has_add else [])
    outs, couts = call_with_comm(body, name, (M // tm, N // tn, nk), in_specs, [o_spec],
                                 [jax.ShapeDtypeStruct((M, N), out_dtype)], [], ("parallel", "parallel", "arbitrary"),
                                 ins, comm)
    return (outs[0], couts) if comm is not None else outs[0]


def norm_in_proj(name, x, nw, wp, tm, tn, comm=None):
    n, d = x.shape
    wpc = wp.shape[1]
    tm, tn = min(tm, n), min(tn, wpc)
    assert n % tm == 0 and wpc % tn == 0

    def body(x_ref, nw_ref, w_ref, p_ref, h_ref):
        @pl.when(pl.program_id(1) == 0)
        def _():
            h_ref[...] = rms_block(x_ref[...], nw_ref[...]).astype(BF16)

        p_ref[...] = jnp.dot(h_ref[...], w_ref[...], preferred_element_type=F32)

    outs, couts = call_with_comm(
        body, name, (n // tm, wpc // tn),
        [pl.BlockSpec((tm, d), lambda i, j: (i, 0)), pl.BlockSpec((1, d), lambda i, j: (0, 0)),
         pl.BlockSpec((d, tn), lambda i, j: (0, j))],
        [pl.BlockSpec((tm, tn), lambda i, j: (i, j)), pl.BlockSpec((tm, d), lambda i, j: (i, 0))],
        [jax.ShapeDtypeStruct((n, wpc), F32), jax.ShapeDtypeStruct((n, d), BF16)], [], ("parallel", "arbitrary"),
        [x, nw, wp], comm)
    return (outs[0], outs[1], couts) if comm is not None else (outs[0], outs[1])


def norm_bwd(name, x, nw, dh, dres, tb):
    n, d = x.shape
    tb = min(tb, n)

    def body(x_ref, nw_ref, dh_ref, dres_ref, dx_ref, dnw_ref):
        @pl.when(pl.program_id(0) == 0)
        def _():
            dnw_ref[...] = jnp.zeros_like(dnw_ref)

        _, vjp = jax.vjp(rms_block, x_ref[...], nw_ref[...])
        dx, dnw = vjp(dh_ref[...])
        dx_ref[...] = dx + dres_ref[...]
        dnw_ref[...] += dnw

    row = pl.BlockSpec((tb, d), lambda i: (i, 0))
    par = pl.BlockSpec((1, d), lambda i: (0, 0))
    return pl.pallas_call(
        body, name=name, grid=(n // tb,), in_specs=[row, par, row, row], out_specs=[row, par],
        out_shape=[jax.ShapeDtypeStruct((n, d), F32), jax.ShapeDtypeStruct((1, d), F32)],
        compiler_params=_cparams(("arbitrary",)),
    )(x, nw, dh, dres)


def loss_grad(name, y, target, tb):
    n, d = y.shape
    tb = min(tb, n)

    def body(y_ref, t_ref, dy_ref, loss_ref):
        @pl.when(pl.program_id(0) == 0)
        def _():
            loss_ref[...] = jnp.zeros_like(loss_ref)

        err = y_ref[...] - t_ref[...]
        dy_ref[...] = err * (1.0 / d)
        part = 0.5 * jnp.sum(jnp.mean(err * err, axis=-1, keepdims=True), axis=0, keepdims=True)
        loss_ref[...] += jnp.broadcast_to(part, loss_ref.shape)

    row = pl.BlockSpec((tb, d), lambda i: (i, 0))
    dy, loss = pl.pallas_call(
        body, name=name, grid=(n // tb,), in_specs=[row, row],
        out_specs=[row, pl.BlockSpec((8, LANE), lambda i: (0, 0))],
        out_shape=[jax.ShapeDtypeStruct((n, d), F32), jax.ShapeDtypeStruct((8, LANE), F32)],
        compiler_params=_cparams(("arbitrary",)),
    )(y, target)
    return dy, loss[0, 0]


def rope_tables(name, pos_col, inv_freq_row):
    n = pos_col.shape[0]

    def body(p_ref, f_ref, c_ref, s_ref):
        ang = p_ref[...].astype(F32) * f_ref[...]
        lane = lax.broadcasted_iota(jnp.int32, ang.shape, 1)
        c_ref[...] = jnp.where(lane < ROT_DIM, jnp.cos(ang), 1.0)
        s_ref[...] = jnp.where(lane < ROT_DIM, jnp.sin(ang), 0.0)

    return pl.pallas_call(
        body, name=name, out_shape=[jax.ShapeDtypeStruct((n, A_HEAD), F32)] * 2,
    )(pos_col, inv_freq_row)


def _scan_operands(cfg, nseq, u_ref, w_ref, qg_ref, kd_ref, a_ref, gl_ref):
    pairs = [(b, h) for b in range(nseq) for h in range(cfg.CH)]
    st = lambda r, wd: jnp.stack([r[b, :, wd * h:wd * (h + 1)] for b, h in pairs], axis=0)
    gl = jnp.stack([gl_ref[b, 0:1, h:h + 1] for b, h in pairs], axis=0)
    return st(u_ref, C_HEAD), st(w_ref, C_HEAD), st(qg_ref, C_HEAD), st(kd_ref, C_HEAD), st(a_ref, CHUNK), gl


def gdn_scan_fwd(name, cfg, nseq, u, w, qg, kd, intra, glast):
    CH, CW, T = cfg.CH, cfg.CW, cfg.T
    nc = T // CHUNK

    def body(u_ref, w_ref, qg_ref, kd_ref, a_ref, gl_ref, o_ref, sin_ref, s_ref):
        @pl.when(pl.program_id(0) == 0)
        def _():
            s_ref[...] = jnp.zeros_like(s_ref)

        S = s_ref[...]
        for b in range(nseq):
            sin_ref[b, 0] = S[b * CH:(b + 1) * CH]
        o, S_next = gdn_state_step(S, *_scan_operands(cfg, nseq, u_ref, w_ref, qg_ref, kd_ref, a_ref, gl_ref))
        s_ref[...] = S_next
        for b in range(nseq):
            o_ref[b] = jnp.concatenate([o[b * CH + h] for h in range(CH)], axis=1)

    row = lambda wd: pl.BlockSpec((nseq, CHUNK, wd), lambda c: (0, c, 0))
    widths = [CW, CW, CW, CW, CH * CHUNK, LANE]
    o, s_in = pl.pallas_call(
        body, name=name, grid=(nc,),
        in_specs=[row(x) for x in widths],
        out_specs=[row(CW), pl.BlockSpec((nseq, 1, CH, C_HEAD, C_HEAD), lambda c: (0, c, 0, 0, 0))],
        out_shape=[jax.ShapeDtypeStruct((nseq, T, CW), F32),
                   jax.ShapeDtypeStruct((nseq, nc, CH, C_HEAD, C_HEAD), F32)],
        scratch_shapes=[pltpu.VMEM((nseq * CH, C_HEAD, C_HEAD), F32)],
        compiler_params=_cparams(("arbitrary",)),
    )(*[a.reshape(nseq, T, a.shape[1]) for a in (u, w, qg, kd, intra, glast)])
    return o.reshape(nseq * T, CW), s_in


def gdn_scan_bwd(name, cfg, nseq, u, w, qg, kd, intra, glast, s_in, do):
    CH, CW, T = cfg.CH, cfg.CW, cfg.T
    nc = T // CHUNK

    def body(u_ref, w_ref, qg_ref, kd_ref, a_ref, gl_ref, sin_ref, do_ref,
             du_ref, dw_ref, dqg_ref, dkd_ref, da_ref, dgl_ref, ds_ref):
        @pl.when(pl.program_id(0) == 0)
        def _():
            ds_ref[...] = jnp.zeros_like(ds_ref)

        S = jnp.concatenate([sin_ref[b, 0] for b in range(nseq)], axis=0)
        dout = jnp.stack([do_ref[b, :, C_HEAD * h:C_HEAD * (h + 1)] for b in range(nseq) for h in range(CH)], axis=0)
        _, vjp = jax.vjp(gdn_state_step, S, *_scan_operands(cfg, nseq, u_ref, w_ref, qg_ref, kd_ref, a_ref, gl_ref))
        dS, du, dw, dqg, dkd, da, dg = vjp((dout, ds_ref[...]))
        ds_ref[...] = dS
        lane = lax.broadcasted_iota(jnp.int32, (CHUNK, LANE), 1)
        rowi = lax.broadcasted_iota(jnp.int32, (CHUNK, LANE), 0)
        for b in range(nseq):
            cat = lambda x: jnp.concatenate([x[b * CH + h] for h in range(CH)], axis=1)
            du_ref[b] = cat(du)
            dw_ref[b] = cat(dw)
            dqg_ref[b] = cat(dqg)
            dkd_ref[b] = cat(dkd)
            da_ref[b] = cat(da)
            dgl = jnp.zeros((CHUNK, LANE), F32)
            for h in range(CH):
                dgl = dgl + jnp.where((lane == h) & (rowi == 0), dg[b * CH + h], 0.0)
            dgl_ref[b] = dgl

    row = lambda wd: pl.BlockSpec((nseq, CHUNK, wd), lambda c: (0, nc - 1 - c, 0))
    widths = [CW, CW, CW, CW, CH * CHUNK, LANE]
    outs = pl.pallas_call(
        body, name=name, grid=(nc,),
        in_specs=[row(x) for x in widths]
        + [pl.BlockSpec((nseq, 1, CH, C_HEAD, C_HEAD), lambda c: (0, nc - 1 - c, 0, 0, 0)), row(CW)],
        out_specs=[row(x) for x in widths],
        out_shape=[jax.ShapeDtypeStruct((nseq, T, x), F32) for x in widths],
        scratch_shapes=[pltpu.VMEM((nseq * CH, C_HEAD, C_HEAD), F32)],
        compiler_params=_cparams(("arbitrary",)),
    )(*[a.reshape(nseq, T, a.shape[1]) for a in (u, w, qg, kd, intra, glast)], s_in, do.reshape(nseq, T, CW))
    return [o.reshape(nseq * T, o.shape[2]) for o in outs]


def _tile(total, cap, unit=LANE):
    best = None
    for t in range(unit, min(cap, total) + 1, unit):
        if total % t == 0:
            best = t
    assert best is not None, (total, cap, unit)
    return best


def _pad_lanes(v, width=LANE):
    return jnp.pad(v.reshape(1, -1), ((0, 0), (0, width - v.shape[-1])))


def permute_w_in(cfg, w):
    parts = []
    for n in cfg.order:
        off, wd = cfg.orig[n]
        blk = w[:, off:off + wd]
        if cfg.g[n][1] != wd:
            blk = jnp.pad(blk, ((0, 0), (0, cfg.g[n][1] - wd)))
        parts.append(blk)
    return jnp.concatenate(parts, axis=1)


def unpermute_w_in(cfg, wp):
    parts = []
    for n, (off, wd) in cfg.orig.items():
        parts.append(wp[:, cfg.g[n][0]:cfg.g[n][0] + wd])
    return jnp.concatenate(parts, axis=1)


def _layer_params(cfg, prm):
    return dict(
        nw=prm['norm_w'].reshape(1, -1),
        qnw=prm['q_norm_w'].reshape(1, -1), knw=prm['k_norm_w'].reshape(1, -1),
        sinks_row=jnp.repeat(prm['sinks'], A_HEAD).reshape(1, -1),
        cw=prm['b_conv_w'], cb=prm['b_conv_b'].reshape(1, -1),
        lw=prm['b_ln_w'].reshape(1, -1), lb=prm['b_ln_b'].reshape(1, -1),
        pw=prm['b_pw_w'], pb=prm['b_pw_b'].reshape(1, -1),
        ccw=prm['c_conv_w'], alog=_pad_lanes(prm['c_a_log']), dtb=_pad_lanes(prm['c_dt_bias']),
        onw=prm['c_onorm_w'].reshape(1, -1),
    )


def _attn_io(cfg, p, cos, sin, grads):
    gq = BF16 if grads else None
    rows = [Row(p, cfg.AW, cfg.blk('qa'), gq), Row(p, cfg.AW, cfg.blk('za'), gq),
            Row(p, cfg.AKW, cfg.blk('ka'), gq), Row(p, cfg.AKW, cfg.blk('va'), gq),
            Row(cos, A_HEAD), Row(sin, A_HEAD)]
    halos = [Halo(p, cfg.AKW, cfg.blk('ka'), ATTN_BLOCK, 2 if grads else None),
             Halo(p, cfg.AKW, cfg.blk('va'), ATTN_BLOCK, 3 if grads else None),
             Halo(cos, A_HEAD, 0, ATTN_BLOCK), Halo(sin, A_HEAD, 0, ATTN_BLOCK)]
    return rows, halos


def _conv_io(cfg, p, grads):
    gq = BF16 if grads else None
    rows = [Row(p, 2 * cfg.BW, cfg.blk('ub'), gq), Row(p, cfg.BW, cfg.blk('zb'), gq)]
    halos = [Halo(p, 2 * cfg.BW, cfg.blk('ub'), B_HALO, 0 if grads else None)]
    return rows, halos


def _prep_io(cfg, p, grads):
    gq = BF16 if grads else None
    rows = [Row(p, cfg.CW, cfg.blk(n), gq) for n in ('qc', 'kc', 'vc')]
    rows += [Row(p, LANE, cfg.blk('bc'), gq), Row(p, LANE, cfg.blk('ac'), gq)]
    halos = [Halo(p, cfg.CW, cfg.blk(n), C_HALO, k if grads else None) for k, n in enumerate(('qc', 'kc', 'vc'))]
    return rows, halos


TB_CONV = 128
TB_PREP = 256
TB_OUT = 256
TB_INTRA_FWD = 128
TB_INTRA_BWD = 128


def layer_forward(cfg, l, x, lp, wp, wo, cos, sin, comm=None):
    n = x.shape[0]
    nseq = n // cfg.T
    T = cfg.T
    p, h, *carried = norm_in_proj(f"in_proj_{l}", x, lp['nw'], wp, 1024, _tile(cfg.WP, 768), comm=comm)
    rows, halos = _attn_io(cfg, p, cos, sin, False)
    (oa,) = rb_fwd(f"attn_fwd_{l}", functools.partial(attn_block, cfg), n, ATTN_BLOCK, T // ATTN_BLOCK, rows, halos,
                   [lp['qnw'], lp['knw'], lp['sinks_row']], [(cfg.AW, BF16)])
    rows, halos = _conv_io(cfg, p, False)
    tbb = min(TB_CONV, T)
    (ob,) = rb_fwd(f"conv_fwd_{l}", functools.partial(conv_block, cfg), n, tbb, T // tbb, rows, halos,
                   [lp['cw'], lp['cb'], lp['lw'], lp['lb'], lp['pw'], lp['pb']], [(cfg.BW, BF16)])
    rows, halos = _prep_io(cfg, p, False)
    tbp = min(TB_PREP, T)
    qn, kn, v, g, beta = rb_fwd(f"gdn_prep_fwd_{l}", functools.partial(gdn_prep_block, cfg), n, tbp, T // tbp, rows,
                                halos, [lp['ccw'], lp['alog'], lp['dtb']],
                                [(cfg.CW, F32)] * 3 + [(LANE, F32)] * 2)
    intra_outs = rb_fwd(f"gdn_intra_fwd_{l}", functools.partial(gdn_intra_rows, cfg), n, TB_INTRA_FWD, T // TB_INTRA_FWD,
                        [Row(qn, cfg.CW), Row(kn, cfg.CW), Row(v, cfg.CW), Row(g, LANE), Row(beta, LANE)], [], [],
                        [(cfg.CW, F32)] * 4 + [(cfg.CH * CHUNK, F32), (LANE, F32), (cfg.CH * CHUNK, F32)])
    intra_outs, inv = intra_outs[:6], intra_outs[6]
    o, s_in = gdn_scan_fwd(f"gdn_scan_fwd_{l}", cfg, nseq, *intra_outs)
    tbo = min(TB_OUT, T)
    (oc,) = rb_fwd(f"gdn_out_fwd_{l}", functools.partial(gdn_out_block, cfg), n, tbo, T // tbo,
                   [Row(o, cfg.CW), Row(p, cfg.CW, cfg.blk('zc'))], [], [lp['onw']], [(cfg.CW, BF16)])
    y = jnp.concatenate([oa, ob, oc], axis=1)
    x_next = matmul(f"out_proj_{l}", y, wo, 'nn', 1024, 1024, cfg.D, add=x)
    saved = dict(x=x, p=p, h=h, y=y, qn=qn, kn=kn, v=v, g=g, beta=beta, intra_outs=intra_outs, inv=inv, s_in=s_in, o=o)
    return x_next, saved, (carried[0] if carried else None)


def layer_backward(cfg, l, dxn, sv, lp, wp, wo, cos, sin, rs=None):
    n = dxn.shape[0]
    nseq = n // cfg.T
    T = cfg.T
    p = sv['p']
    AW, BW, CW = cfg.AW, cfg.BW, cfg.CW
    if rs is None:
        dy = matmul(f"dy_{l}", dxn, wo, 'nt', 1024, 1024, cfg.D)
    else:
        dy, received = matmul(f"dy_{l}", dxn, wo, 'nt', 1024, 1024, cfg.D, comm=rs.swap())
    dwo = matmul(f"dwo_{l}", sv['y'], dxn, 'tn', 1024, 1024, 2048)
    doa, dob, doc = dy[:, :AW], dy[:, AW:AW + BW], dy[:, AW + BW:]
    tbo = min(TB_OUT, T)
    do, dzc, donw = rb_bwd(f"gdn_out_bwd_{l}", functools.partial(gdn_out_block, cfg), n, tbo, T // tbo,
                           [Row(sv['o'], CW, 0, F32), Row(p, CW, cfg.blk('zc'), BF16)], [], [lp['onw']], [doc], [True])
    dintra = gdn_scan_bwd(f"gdn_scan_bwd_{l}", cfg, nseq, *sv['intra_outs'], sv['s_in'], do)
    dqn, dkn, dv, dg, dbeta = rb_bwd(
        f"gdn_intra_bwd_{l}", functools.partial(gdn_intra_rows, cfg), n, TB_INTRA_BWD, T // TB_INTRA_BWD,
        [Row(sv['qn'], CW, 0, F32), Row(sv['kn'], CW, 0, F32), Row(sv['v'], CW, 0, F32), Row(sv['g'], LANE, 0, F32),
         Row(sv['beta'], LANE, 0, F32), Row(sv['inv'], cfg.CH * CHUNK)], [], [], list(dintra), [])
    rows, halos = _prep_io(cfg, p, True)
    tbp = min(TB_PREP, T)
    dqc, dkc, dvc, dbc, dac, dccw, dalog, ddtb = rb_bwd(
        f"gdn_prep_bwd_{l}", functools.partial(gdn_prep_block, cfg), n, tbp, T // tbp, rows, halos,
        [lp['ccw'], lp['alog'], lp['dtb']], [dqn, dkn, dv, dg, dbeta], [True] * 3)
    rows, halos = _conv_io(cfg, p, True)
    tbb = min(TB_CONV, T)
    dub, dzb, dcw, dcb, dlw, dlb, dpw, dpb = rb_bwd(
        f"conv_bwd_{l}", functools.partial(conv_block, cfg), n, tbb, T // tbb, rows, halos,
        [lp['cw'], lp['cb'], lp['lw'], lp['lb'], lp['pw'], lp['pb']], [dob], [True] * 6)
    rows, halos = _attn_io(cfg, p, cos, sin, True)
    dqa, dza, dka, dva, dqnw, dknw, dsinks_row = rb_bwd(
        f"attn_bwd_{l}", functools.partial(attn_block, cfg), n, ATTN_BLOCK, T // ATTN_BLOCK, rows, halos,
        [lp['qnw'], lp['knw'], lp['sinks_row']], [doa], [True] * 3)
    dgroups = dict(qa=dqa, za=dza, qc=dqc, kc=dkc, vc=dvc, zc=dzc, ka=dka, va=dva, ub=dub, zb=dzb, bc=dbc, ac=dac)
    dp = jnp.concatenate([dgroups[k] for k in cfg.order], axis=1)
    if rs is None:
        dh, got = matmul(f"dh_{l}", dp, wp, 'nt', 1024, 1024, _tile(cfg.WP, 2304)), None
    else:
        dh, got = matmul(f"dh_{l}", dp, wp, 'nt', 1024, 1024, _tile(cfg.WP, 2304), comm=rs.scatter(received))
    dwp = matmul(f"dwp_{l}", sv['h'], dp, 'tn', 1024, _tile(cfg.WP, 1152), 2048)
    dx, dnw = norm_bwd(f"norm_bwd_{l}", sv['x'], lp['nw'], dh, dxn, 256)
    grads = dict(
        norm_w=dnw[0], w_in_perm=dwp, q_norm_w=dqnw[0], k_norm_w=dknw[0],
        sinks=dsinks_row.reshape(cfg.AQH, A_HEAD)[:, 0],
        b_conv_w=dcw, b_conv_b=dcb[0], b_ln_w=dlw[0], b_ln_b=dlb[0], b_pw_w=dpw, b_pw_b=dpb[0],
        c_conv_w=dccw, c_a_log=dalog[0, :cfg.CH], c_dt_bias=ddtb[0, :cfg.CH], c_onorm_w=donw[0], w_out=dwo)
    return dx, grads, got


def rope_for(cfg, positions):
    n = positions.size
    inv_freq = ROPE_THETA ** (-np.arange(0, ROT_DIM, 2, dtype=np.float32) / ROT_DIM)
    freq_row = np.zeros((1, A_HEAD), np.float32)
    freq_row[0, :ROT_DIM] = np.concatenate([inv_freq, inv_freq])
    return rope_tables("rope_tables", positions.reshape(n, 1), jnp.asarray(freq_row))


def local_step(cfg, x, positions, prm, wps, wos, target):
    nseq = x.shape[0]
    n = nseq * cfg.T
    cos, sin = rope_for(cfg, positions)
    lps = [_layer_params(cfg, {k: v[l] for k, v in prm.items()}) for l in range(DEPTH)]
    saved = []
    xl = x.reshape(n, cfg.D)
    for l in range(DEPTH):
        xl, sv, _ = layer_forward(cfg, l, xl, lps[l], wps[l], wos[l], cos, sin)
        saved.append(sv)
    dx, loss = loss_grad("loss_grad", xl, target.reshape(n, cfg.D), 256)
    grads = [None] * DEPTH
    for l in reversed(range(DEPTH)):
        dx, grads[l], _ = layer_backward(cfg, l, dx, saved[l], lps[l], wps[l], wos[l], cos, sin)
    return loss, dx.reshape(x.shape), grads


N_CHIPS = 4
N_DEV = 8


def _place():
    return lax.axis_index("x"), lax.axis_index("y"), lax.axis_index("c")


def _other_chips(x, y):
    return [(1 - x, y), (x, 1 - y), (1 - x, 1 - y)]


def _remote(src, dst, send, recv, to):
    return pltpu.make_async_remote_copy(src_ref=src, dst_ref=dst, send_sem=send, recv_sem=recv, device_id=to,
                                        device_id_type=MESH)


def gather_comm(arrs):
    n = len(arrs)

    def first_copies(ins, outs, send, recv):
        x, y, c = _place()
        me = 2 * x + y
        return [_remote(ins[i].at[c], outs[i].at[me, c], send.at[i, j], recv.at[i, j], (cx, cy, c))
                for i in range(n) for j, (cx, cy) in enumerate(_other_chips(x, y))]

    def start(ins, outs, sems):
        for cp in first_copies(ins, outs, *sems):
            cp.start()

    def finish(ins, outs, sems):
        send, recv = sems
        x, y, c = _place()
        chips = _other_chips(x, y)
        sib = (x, y, 1 - c)
        passed = []
        for i in range(n):
            for j, (cx, cy) in enumerate(chips):
                blk = outs[i].at[2 * cx + cy, c]
                _remote(blk, blk, send.at[i, j], recv.at[i, j], (x, y, c)).wait_recv()
                cp = _remote(blk, blk, send.at[i, 3 + j], recv.at[i, 3 + j], sib)
                cp.start()
                passed.append(cp)
        for i in range(n):
            for j, (cx, cy) in enumerate(chips):
                blk = outs[i].at[2 * cx + cy, 1 - c]
                _remote(blk, blk, send.at[i, 3 + j], recv.at[i, 3 + j], sib).wait_recv()
        for cp in first_copies(ins, outs, send, recv) + passed:
            cp.wait_send()

    return Comm(arrs, [jax.ShapeDtypeStruct((N_CHIPS,) + a.shape, a.dtype) for a in arrs],
                [pltpu.SemaphoreType.DMA((n, 6)), pltpu.SemaphoreType.DMA((n, 6))], start, finish)


def fill_own(gathered, arrs):
    me = 2 * lax.axis_index("x") + lax.axis_index("y")
    return [lax.dynamic_update_index_in_dim(o, a, me, 0) for o, a in zip(gathered, arrs)]


def swap_comm(arrs):
    n = len(arrs)

    def copies(ins, outs, send, recv):
        x, y, c = _place()
        return [_remote(ins[i].at[:, 1 - c], outs[i], send.at[i], recv.at[i], (x, y, 1 - c)) for i in range(n)]

    def start(ins, outs, sems):
        for cp in copies(ins, outs, *sems):
            cp.start()

    def finish(ins, outs, sems):
        for cp in copies(ins, outs, *sems):
            cp.wait()

    return Comm(arrs, [jax.ShapeDtypeStruct((a.shape[0],) + a.shape[2:], a.dtype) for a in arrs],
                [pltpu.SemaphoreType.DMA((n,)), pltpu.SemaphoreType.DMA((n,))], start, finish)


def scatter_comm(arrs):
    n = len(arrs)

    def copies(ins, outs, send, recv):
        x, y, c = _place()
        return [_remote(ins[i].at[2 * cx + cy], outs[i].at[j], send.at[i, j], recv.at[i, j], (cx, cy, c))
                for i in range(n) for j, (cx, cy) in enumerate(_other_chips(x, y))]

    def start(ins, outs, sems):
        for cp in copies(ins, outs, *sems):
            cp.start()

    def finish(ins, outs, sems):
        send, recv = sems
        x, y, c = _place()
        for i in range(n):
            for j in range(3):
                blk = outs[i].at[j]
                _remote(blk, blk, send.at[i, j], recv.at[i, j], (x, y, c)).wait_recv()
        for cp in copies(ins, outs, send, recv):
            cp.wait_send()

    return Comm(arrs, [jax.ShapeDtypeStruct((3,) + a.shape[1:], a.dtype) for a in arrs],
                [pltpu.SemaphoreType.DMA((n, 3)), pltpu.SemaphoreType.DMA((n, 3))], start, finish)


def share_comm(arrs):
    n = len(arrs)

    def copies(outs, send, recv):
        x, y, c = _place()
        return [_remote(outs[i].at[c], outs[i].at[c], send.at[i], recv.at[i], (x, y, 1 - c)) for i in range(n)]

    def start(ins, outs, sems):
        for cp in copies(outs, *sems):
            cp.start()

    def finish(ins, outs, sems):
        send, recv = sems
        x, y, c = _place()
        for i in range(n):
            blk = outs[i].at[1 - c]
            _remote(blk, blk, send.at[i], recv.at[i], (x, y, c)).wait_recv()
        for cp in copies(outs, send, recv):
            cp.wait_send()

    return Comm(arrs, [jax.ShapeDtypeStruct(a.shape, a.dtype) for a in arrs],
                [pltpu.SemaphoreType.DMA((n,)), pltpu.SemaphoreType.DMA((n,))], start, finish,
                aliases={i: i for i in range(n)})


def all_reduce_small(name, packed):
    r = packed.shape[0]

    def body(in_ref, out_ref, buf, send, recv):
        x, y, c = _place()
        me = 4 * x + 2 * y + c
        buf[me] = in_ref[...]
        flips = [(fx, fy, fc) for fx in (0, 1) for fy in (0, 1) for fc in (0, 1) if (fx, fy, fc) != (0, 0, 0)]
        peers = [((x + fx) % 2, (y + fy) % 2, (c + fc) % 2) for fx, fy, fc in flips]
        cps = [_remote(in_ref, buf.at[me], send.at[k], recv.at[k], peer) for k, peer in enumerate(peers)]
        for cp in cps:
            cp.start()
        for k, (px, py, pc) in enumerate(peers):
            blk = buf.at[4 * px + 2 * py + pc]
            _remote(blk, blk, send.at[k], recv.at[k], (x, y, c)).wait_recv()
        for cp in cps:
            cp.wait_send()
        acc = buf[0]
        for d in range(1, N_DEV):
            acc = acc + buf[d]
        out_ref[...] = acc

    vm = pl.BlockSpec(memory_space=pltpu.VMEM)
    return pl.pallas_call(
        body, name=name, in_specs=[vm], out_specs=vm, out_shape=jax.ShapeDtypeStruct(packed.shape, F32),
        scratch_shapes=[pltpu.VMEM((N_DEV, r, LANE), F32), pltpu.SemaphoreType.DMA((N_DEV - 1,)),
                        pltpu.SemaphoreType.DMA((N_DEV - 1,))],
    )(packed)


def add_own_half(name, g, a, c_idx, tr):
    nch, _, r, cc = g.shape
    tr = min(tr, r)

    def body(c_ref, g_ref, a_ref, o_ref):
        o_ref[...] = (g_ref[0] + a_ref[...]).astype(o_ref.dtype)

    return pl.pallas_call(
        body, name=name,
        grid_spec=pltpu.PrefetchScalarGridSpec(
            num_scalar_prefetch=1, grid=(nch, r // tr),
            in_specs=[pl.BlockSpec((1, 1, tr, cc), lambda j, i, c_ref: (j, c_ref[0], i, 0)),
                      pl.BlockSpec((1, tr, cc), lambda j, i, c_ref: (j, i, 0))],
            out_specs=pl.BlockSpec((1, tr, cc), lambda j, i, c_ref: (j, i, 0))),
        out_shape=jax.ShapeDtypeStruct(a.shape, BF16),
        compiler_params=_cparams(("parallel", "parallel")),
    )(c_idx, g, a)


def sum_chips(name, p, b, idx, tr):
    _, r, cc = p.shape
    tr = min(tr, r)

    def body(idx_ref, p_ref, b_ref, o_ref):
        acc = p_ref[0].astype(F32)
        for k in range(3):
            acc = acc + b_ref[k].astype(F32)
        o_ref[0] = acc

    return pl.pallas_call(
        body, name=name,
        grid_spec=pltpu.PrefetchScalarGridSpec(
            num_scalar_prefetch=1, grid=(r // tr,),
            in_specs=[pl.BlockSpec((1, tr, cc), lambda i, s: (s[0], i, 0)),
                      pl.BlockSpec((3, tr, cc), lambda i, s: (0, i, 0))],
            out_specs=pl.BlockSpec((1, tr, cc), lambda i, s: (s[1], i, 0))),
        out_shape=jax.ShapeDtypeStruct((2, r, cc), F32),
        compiler_params=_cparams(("parallel",)),
    )(idx, p, b)


class GradReduce:
    def __init__(self, tag, parts, chip, c_idx):
        self.tag, self.c_idx = tag, c_idx
        self.parts = [p.reshape(p.shape[0], 2, p.shape[1] // 2, p.shape[2]) for p in parts]
        self.idx = jnp.concatenate([chip.astype(jnp.int32).reshape(1), c_idx])

    def swap(self):
        return swap_comm(self.parts)

    def scatter(self, received):
        self.part = [add_own_half(f"rs{self.tag}_add_sibling_{t}", g, a, self.c_idx, 128)
                     for t, (g, a) in enumerate(zip(self.parts, received))]
        return scatter_comm(self.part)

    def finish(self, got):
        red = [sum_chips(f"rs{self.tag}_sum_chips_{t}", p, b, self.idx, 128) for t, (p, b) in enumerate(zip(self.part, got))]
        out = run_comm(f"rs{self.tag}_share_halves", share_comm(red))
        return [o.reshape(-1, o.shape[-1]) for o in out]

    def run(self):
        received = run_comm(f"rs{self.tag}_swap_halves", self.swap())
        got = run_comm(f"rs{self.tag}_scatter_chips", self.scatter(received))
        return self.finish(got)


def adamw(name, w, g, m, v, tb):
    r, cc = w.shape
    tb = min(tb, r)
    assert r % tb == 0

    def body(w_ref, g_ref, m_ref, v_ref, d_ref, mo_ref, vo_ref):
        g = g_ref[...]
        m = ADAM_B1 * m_ref[...] + (1.0 - ADAM_B1) * g
        v = ADAM_B2 * v_ref[...] + (1.0 - ADAM_B2) * jnp.square(g)
        m_hat = m / (1.0 - ADAM_B1 ** ADAM_STEP)
        v_hat = v / (1.0 - ADAM_B2 ** ADAM_STEP)
        d_ref[...] = -ADAM_LR * (m_hat / (jnp.sqrt(v_hat) + ADAM_EPS) + ADAM_WD * w_ref[...])
        mo_ref[...] = m
        vo_ref[...] = v

    spec = pl.BlockSpec((tb, cc), lambda i: (i, 0))
    return pl.pallas_call(
        body, name=name, grid=(r // tb,), in_specs=[spec] * 4, out_specs=[spec] * 3,
        out_shape=[jax.ShapeDtypeStruct((r, cc), F32)] * 3,
        compiler_params=_cparams(("parallel",)),
    )(w, g, m, v)


def adamw_layers(name, w, g0, g1, m, v, tb):
    _, r, cc = w.shape
    tb = min(tb, r)
    nb = r // tb

    def body(w_ref, g0_ref, g1_ref, m_ref, v_ref, g_ref, d_ref, mo_ref, vo_ref):
        g = jnp.where(pl.program_id(0) == 0, g0_ref[...], g1_ref[...])
        m = ADAM_B1 * m_ref[0] + (1.0 - ADAM_B1) * g
        v = ADAM_B2 * v_ref[0] + (1.0 - ADAM_B2) * jnp.square(g)
        m_hat = m / (1.0 - ADAM_B1 ** ADAM_STEP)
        v_hat = v / (1.0 - ADAM_B2 ** ADAM_STEP)
        g_ref[0] = g
        d_ref[0] = -ADAM_LR * (m_hat / (jnp.sqrt(v_hat) + ADAM_EPS) + ADAM_WD * w_ref[0])
        mo_ref[0] = m
        vo_ref[0] = v

    spec = pl.BlockSpec((1, tb, cc), lambda l, i: (l, i, 0))
    g0_spec = pl.BlockSpec((tb, cc), lambda l, i: (jnp.where(l == 0, i, nb - 1), 0))
    g1_spec = pl.BlockSpec((tb, cc), lambda l, i: (jnp.where(l == 1, i, 0), 0))
    return pl.pallas_call(
        body, name=name, grid=(2, nb), in_specs=[spec, g0_spec, g1_spec, spec, spec], out_specs=[spec] * 4,
        out_shape=[jax.ShapeDtypeStruct(w.shape, F32)] * 4,
        compiler_params=_cparams(("arbitrary", "arbitrary")),
    )(w, g0, g1, m, v)


def _pack(arrs):
    flat = jnp.concatenate([a.reshape(-1).astype(F32) for a in arrs])
    pad = (-flat.shape[0]) % (8 * LANE)
    return jnp.pad(flat, (0, pad)).reshape(-1, LANE)


def _unpack(packed, shapes):
    flat = packed.reshape(-1)
    out, off = [], 0
    for s in shapes:
        size = math.prod(s)
        out.append(flat[off:off + size].reshape(s))
        off += size
    return out


BIG = ('w_in', 'w_out', 'b_pw_w')
SMALL = tuple(k for k in WEIGHTS if k not in BIG)
CHIP_SHARDED_SMALL = {'b_conv_w': 2, 'c_conv_w': 2}


def kernel(x, positions, norm_w, w_in, q_norm_w, k_norm_w, sinks, b_conv_w, b_conv_b, b_ln_w, b_ln_b, b_pw_w, b_pw_b, c_conv_w, c_a_log, c_dt_bias, c_onorm_w, w_out, loss_target, m_norm_w, m_w_in, m_q_norm_w, m_k_norm_w, m_sinks, m_b_conv_w, m_b_conv_b, m_b_ln_w, m_b_ln_b, m_b_pw_w, m_b_pw_b, m_c_conv_w, m_c_a_log, m_c_dt_bias, m_c_onorm_w, m_w_out, v_norm_w, v_w_in, v_q_norm_w, v_k_norm_w, v_sinks, v_b_conv_w, v_b_conv_b, v_b_ln_w, v_b_ln_b, v_b_pw_w, v_b_pw_b, v_c_conv_w, v_c_a_log, v_c_dt_bias, v_c_onorm_w, v_w_out):
    cfg = Cfg(x.shape[-1], x.shape[-2])
    w = dict(norm_w=norm_w, w_in=w_in, q_norm_w=q_norm_w, k_norm_w=k_norm_w, sinks=sinks, b_conv_w=b_conv_w,
             b_conv_b=b_conv_b, b_ln_w=b_ln_w, b_ln_b=b_ln_b, b_pw_w=b_pw_w, b_pw_b=b_pw_b, c_conv_w=c_conv_w,
             c_a_log=c_a_log, c_dt_bias=c_dt_bias, c_onorm_w=c_onorm_w, w_out=w_out)
    m = dict(norm_w=m_norm_w, w_in=m_w_in, q_norm_w=m_q_norm_w, k_norm_w=m_k_norm_w, sinks=m_sinks,
             b_conv_w=m_b_conv_w, b_conv_b=m_b_conv_b, b_ln_w=m_b_ln_w, b_ln_b=m_b_ln_b, b_pw_w=m_b_pw_w,
             b_pw_b=m_b_pw_b, c_conv_w=m_c_conv_w, c_a_log=m_c_a_log, c_dt_bias=m_c_dt_bias, c_onorm_w=m_c_onorm_w,
             w_out=m_w_out)
    v = dict(norm_w=v_norm_w, w_in=v_w_in, q_norm_w=v_q_norm_w, k_norm_w=v_k_norm_w, sinks=v_sinks,
             b_conv_w=v_b_conv_w, b_conv_b=v_b_conv_b, b_ln_w=v_b_ln_w, b_ln_b=v_b_ln_b, b_pw_w=v_b_pw_w,
             b_pw_b=v_b_pw_b, c_conv_w=v_c_conv_w, c_a_log=v_c_a_log, c_dt_bias=v_c_dt_bias, c_onorm_w=v_c_onorm_w,
             w_out=v_w_out)
    chip = 2 * lax.axis_index("x") + lax.axis_index("y")
    c_idx = lax.axis_index("c").astype(jnp.int32).reshape(1)
    D, T = cfg.D, cfg.T
    nseq = x.shape[0]
    n = nseq * T
    halves = lambda a: a.reshape((2, a.shape[0] // 2) + a.shape[1:])
    w_in_b, w_out_b = w_in.astype(BF16), w_out.astype(BF16)
    per_layer = lambda l: [halves(w_in_b[l]), halves(w_out_b[l]), halves(b_pw_w[l])]

    def full_weights(g_in, g_out, g_pw):
        cols = jnp.concatenate(list(g_in.reshape(N_CHIPS, D, -1)), axis=1)
        return permute_w_in(cfg, cols), g_out.reshape(D, D), g_pw.reshape(cfg.BW, cfg.BW)

    def layer_prm(l, pw_full):
        prm = {k: w[k][l] for k in SMALL}
        prm['b_pw_w'] = pw_full
        prm['b_conv_w'] = jnp.concatenate(list(g_bcw[:, l]), axis=1)
        prm['c_conv_w'] = jnp.concatenate(list(g_ccw[:, l]), axis=1)
        return _layer_params(cfg, prm)

    first = per_layer(0) + [b_conv_w, c_conv_w]
    g_in0, g_out0, g_pw0, g_bcw, g_ccw = fill_own(run_comm("gather_weights_0", gather_comm(first)), first)
    wp0, wo0, pw0 = full_weights(g_in0, g_out0, g_pw0)
    cos, sin = rope_for(cfg, positions)
    lp0 = layer_prm(0, pw0)
    second = per_layer(1)
    x1, sv0, gathered = layer_forward(cfg, 0, x.reshape(n, D), lp0, wp0, wo0, cos, sin, comm=gather_comm(second))
    wp1, wo1, pw1 = full_weights(*fill_own(gathered, second))
    lp1 = layer_prm(1, pw1)
    x2, sv1, _ = layer_forward(cfg, 1, x1, lp1, wp1, wo1, cos, sin)
    dx2, loss_local = loss_grad("loss_grad", x2, loss_target.reshape(n, D), 256)
    loss = lax.psum(loss_local, ("x", "y", "c"))

    def partials(gr):
        shard_cols = cfg.IN_COLS // N_CHIPS
        return [unpermute_w_in(cfg, gr['w_in_perm']).reshape(D, N_CHIPS, shard_cols).transpose(1, 0, 2),
                gr['w_out'].reshape(N_CHIPS, D // N_CHIPS, D), gr['b_pw_w'].reshape(N_CHIPS, cfg.BW // N_CHIPS, cfg.BW)]

    dx1, gr1, _ = layer_backward(cfg, 1, dx2, sv1, lp1, wp1, wo1, cos, sin)
    rs1 = GradReduce(1, partials(gr1), chip, c_idx)
    dx0, gr0, got1 = layer_backward(cfg, 0, dx1, sv0, lp0, wp0, wo0, cos, sin, rs=rs1)
    red1 = rs1.finish(got1)
    red0 = GradReduce(0, partials(gr0), chip, c_idx).run()
    grad_x = dx0.reshape(x.shape)
    grads = [gr0, gr1]

    small_parts = [jnp.stack([grads[l][k] for l in range(DEPTH)]) for k in SMALL]
    small_red = _unpack(all_reduce_small("all_reduce_small", _pack(small_parts)), [a.shape for a in small_parts])
    g = {}
    for k, a in zip(SMALL, small_red):
        if k in CHIP_SHARDED_SMALL:
            ax = CHIP_SHARDED_SMALL[k]
            width = a.shape[ax] // N_CHIPS
            a = lax.dynamic_slice_in_dim(a, chip * width, width, axis=ax)
        g[k] = a

    delta, new_m, new_v = {}, {}, {}
    for k, g0, g1 in zip(BIG, red0, red1):
        g[k], delta[k], new_m[k], new_v[k] = adamw_layers(f"adamw_{k}", w[k], g0, g1, m[k], v[k], 128)
    shapes = [w[k].shape for k in SMALL]
    outs = adamw("adamw_small", _pack([w[k] for k in SMALL]), _pack([g[k] for k in SMALL]),
                 _pack([m[k] for k in SMALL]), _pack([v[k] for k in SMALL]), 4096)
    for name_, packed in zip(('delta', 'm', 'v'), outs):
        for k, a in zip(SMALL, _unpack(packed, shapes)):
            {'delta': delta, 'm': new_m, 'v': new_v}[name_][k] = a
    return (loss, grad_x, *[g[k] for k in WEIGHTS], *[delta[k] for k in WEIGHTS], *[new_m[k] for k in WEIGHTS],
            *[new_v[k] for k in WEIGHTS])
```

```python
import functools
import math

import numpy as np
import jax
import jax.numpy as jnp
from jax import lax
from jax.experimental import pallas as pl
from jax.experimental.pallas import tpu as pltpu

F32 = jnp.float32
BF16 = jnp.bfloat16
HI = lax.Precision.HIGHEST
MESH = pl.DeviceIdType.MESH

DEPTH = 2
A_HEAD = 64
A_GROUP = 3
ATTN_BLOCK = 128
ROT_DIM = 16
ROPE_THETA = 500000.0
B_CONV = 31
B_HALO = 32
C_HEAD = 128
C_CONV = 4
C_HALO = 8
CHUNK = 64
EPS = 1e-6
LANE = 128

ADAM_LR = 0.001
ADAM_B1 = 0.9
ADAM_B2 = 0.999
ADAM_EPS = 1e-08
ADAM_WD = 0.01
ADAM_STEP = 10

VMEM_LIMIT = 56 * 1024 * 1024

WEIGHTS = ['norm_w', 'w_in', 'q_norm_w', 'k_norm_w', 'sinks', 'b_conv_w', 'b_conv_b', 'b_ln_w', 'b_ln_b',
           'b_pw_w', 'b_pw_b', 'c_conv_w', 'c_a_log', 'c_dt_bias', 'c_onorm_w', 'w_out']


class Cfg:
    def __init__(self, d_model=2048, seq=2048):
        self.D = d_model
        self.T = seq
        self.AW = 3 * d_model // 8
        self.AQH = self.AW // A_HEAD
        self.AKH = self.AQH // A_GROUP
        self.AKW = self.AKH * A_HEAD
        self.BW = d_model // 4
        self.CH = (d_model - self.AW - self.BW) // C_HEAD
        self.CW = self.CH * C_HEAD
        AW, AKW, BW, CW, CH = self.AW, self.AKW, self.BW, self.CW, self.CH
        orig = [('qa', AW), ('ka', AKW), ('va', AKW), ('za', AW), ('ub', 2 * BW), ('zb', BW),
                ('qc', CW), ('kc', CW), ('vc', CW), ('bc', CH), ('ac', CH), ('zc', CW)]
        self.orig = {}
        off = 0
        for n, w in orig:
            self.orig[n] = (off, w)
            off += w
        self.IN_COLS = off
        order = ['qa', 'za', 'qc', 'kc', 'vc', 'zc', 'ka', 'va', 'ub', 'zb', 'bc', 'ac']
        self.order = order
        self.g = {}
        off = 0
        for n in order:
            w = self.orig[n][1]
            wp = LANE if n in ('bc', 'ac') else w
            assert off % wp == 0, (n, off, wp)
            self.g[n] = (off, wp)
            off += wp
        self.WP = off

    def blk(self, name):
        off, w = self.g[name]
        return off // w


def _cparams(sem, vmem=VMEM_LIMIT):
    return pltpu.CompilerParams(dimension_semantics=sem, vmem_limit_bytes=vmem)


def _silu(x):
    return x * jax.nn.sigmoid(x)


ANY = pl.BlockSpec(memory_space=pl.ANY)


class Comm:
    def __init__(self, ins, out_shapes, sems, start, finish, aliases=None):
        self.ins, self.out_shapes, self.sems = list(ins), list(out_shapes), list(sems)
        self.start, self.finish, self.aliases = start, finish, dict(aliases or {})


def call_with_comm(body, name, grid, in_specs, out_specs, out_shape, scratch_shapes, semantics, args, comm=None):
    in_specs, out_specs, out_shape, scratch_shapes = list(in_specs), list(out_specs), list(out_shape), list(scratch_shapes)
    if comm is None:
        outs = pl.pallas_call(body, name=name, grid=grid, in_specs=in_specs, out_specs=out_specs, out_shape=out_shape,
                              scratch_shapes=scratch_shapes, compiler_params=_cparams(semantics))(*args)
        return list(outs), []
    ni, no, ns = len(in_specs), len(out_specs), len(scratch_shapes)
    nci, nco = len(comm.ins), len(comm.out_shapes)

    def wrapped(*refs):
        h_in, c_in = refs[:ni], refs[ni:ni + nci]
        h_out, c_out = refs[ni + nci:ni + nci + no], refs[ni + nci + no:ni + nci + no + nco]
        h_scr, c_sems = refs[ni + nci + no + nco:ni + nci + no + nco + ns], refs[ni + nci + no + nco + ns:]
        ids = [pl.program_id(d) for d in range(len(grid))]
        first = functools.reduce(jnp.logical_and, [i == 0 for i in ids])
        last = functools.reduce(jnp.logical_and, [i == g - 1 for i, g in zip(ids, grid)])

        @pl.when(first)
        def _():
            comm.start(c_in, c_out, c_sems)

        body(*h_in, *h_out, *h_scr)

        @pl.when(last)
        def _():
            comm.finish(c_in, c_out, c_sems)

    outs = pl.pallas_call(
        wrapped, name=name, grid=grid, in_specs=in_specs + [ANY] * nci, out_specs=out_specs + [ANY] * nco,
        out_shape=out_shape + comm.out_shapes, scratch_shapes=scratch_shapes + comm.sems,
        input_output_aliases={ni + k: no + v for k, v in comm.aliases.items()},
        compiler_params=_cparams(("arbitrary",) * len(grid)),
    )(*args, *comm.ins)
    return list(outs[:no]), list(outs[no:])


def run_comm(name, comm):
    nci, nco = len(comm.ins), len(comm.out_shapes)

    def body(*refs):
        c_in, c_out, c_sems = refs[:nci], refs[nci:nci + nco], refs[nci + nco:]
        comm.start(c_in, c_out, c_sems)
        comm.finish(c_in, c_out, c_sems)

    return pl.pallas_call(
        body, name=name, in_specs=[ANY] * nci, out_specs=[ANY] * nco, out_shape=comm.out_shapes,
        scratch_shapes=comm.sems, input_output_aliases=comm.aliases,
    )(*comm.ins)


def _bdot(a, b, ca, cb, precision=HI):
    dims = (((ca,), (cb,)), ((0,), (0,)))
    if precision is HI and a.dtype == F32:
        ah = a.astype(BF16)
        bh = b.astype(BF16)
        al = (a - ah.astype(F32)).astype(BF16)
        bl = (b - bh.astype(F32)).astype(BF16)
        dg = lambda p, q: lax.dot_general(p, q, dims, preferred_element_type=F32)
        return dg(ah, bh) + (dg(ah, bl) + dg(al, bh))
    return lax.dot_general(a, b, dims, precision=precision, preferred_element_type=F32)


def _rope_matrix(nb):
    i = lax.broadcasted_iota(jnp.int32, (nb, A_HEAD, A_HEAD), 1)
    j = lax.broadcasted_iota(jnp.int32, (nb, A_HEAD, A_HEAD), 2)
    half = ROT_DIM // 2
    neg = (j < half) & (i == j + half)
    pos = (j >= half) & (j < ROT_DIM) & (i == j - half)
    return jnp.where(neg, -1.0, jnp.where(pos, 1.0, 0.0)).astype(F32)


def _norm_rope(xh, w, cos, sin):
    y = xh * lax.rsqrt(jnp.mean(xh * xh, axis=-1, keepdims=True) + EPS) * w
    return y * cos + _bdot(y, _rope_matrix(xh.shape[0]), 2, 1) * sin


def attn_block(cfg, first, q, za, kc, vc, cosc, sinc, kp, vp, cosp, sinp, qnw, knw, sinks_row):
    blk = ATTN_BLOCK
    nq, nk = cfg.AQH, cfg.AKH
    qi = lax.broadcasted_iota(jnp.int32, (blk, 2 * blk), 0)
    kj = lax.broadcasted_iota(jnp.int32, (blk, 2 * blk), 1)
    dist = qi + blk - kj
    valid = ((dist >= 0) & (dist < blk) & (jnp.logical_not(first) | (kj >= blk)))[None]
    cos2 = jnp.concatenate([cosp, cosc], axis=0)
    sin2 = jnp.concatenate([sinp, sinc], axis=0)
    head = lambda x, h: x[:, A_HEAD * h:A_HEAD * (h + 1)]
    k2 = jnp.stack([jnp.concatenate([head(kp, h), head(kc, h)], axis=0) for h in range(nk)], axis=0)
    v2 = jnp.stack([jnp.concatenate([head(vp, h), head(vc, h)], axis=0) for h in range(nk)], axis=0)
    k2 = _norm_rope(k2, knw[None], cos2[None], sin2[None]).astype(BF16)
    v2 = v2.astype(BF16)
    k2 = jnp.stack([k2[h // A_GROUP] for h in range(nq)], axis=0)
    v2 = jnp.stack([v2[h // A_GROUP] for h in range(nq)], axis=0)
    qh = jnp.stack([head(q, h) for h in range(nq)], axis=0)
    qh = _norm_rope(qh, qnw[None], cosc[None], sinc[None]).astype(BF16)
    s = _bdot(qh, k2, 2, 2, None) * (A_HEAD ** -0.5)
    s = jnp.where(valid, s, -1e30)
    sink = jnp.stack([sinks_row[:, A_HEAD * h:A_HEAD * h + 1] for h in range(nq)], axis=0)
    m = jnp.maximum(jnp.max(s, axis=-1, keepdims=True), sink)
    e = jnp.exp(s - m)
    den = jnp.sum(e, axis=-1, keepdims=True) + jnp.exp(sink - m)
    o = _bdot((e / den).astype(BF16), v2, 2, 1, None)
    return (jnp.concatenate([o[h] for h in range(nq)], axis=1) * _silu(za),)


def conv_block(cfg, first, u, zb, uh, cw, cb, lw, lb, pw, pb):
    BW = cfg.BW
    tb = u.shape[0]
    uu = jnp.concatenate([uh, u], axis=0)
    h = uu[:, :BW] * jax.nn.sigmoid(uu[:, BW:])
    row = lax.broadcasted_iota(jnp.int32, h.shape, 0)
    h = jnp.where(first & (row < B_HALO), 0.0, h)
    acc = jnp.zeros((tb, BW), F32) + cb
    base = B_HALO - (B_CONV - 1)
    for k in range(B_CONV):
        acc = acc + cw[k:k + 1, :] * h[base + k:base + k + tb, :]
    mu = jnp.mean(acc, axis=-1, keepdims=True)
    var = jnp.mean(jnp.square(acc - mu), axis=-1, keepdims=True)
    y = (acc - mu) * lax.rsqrt(var + EPS) * lw + lb
    s = _silu(y)
    o = jnp.dot(s.astype(BF16), pw.astype(BF16), preferred_element_type=F32) + pb
    return (o * _silu(zb),)


def gdn_prep_block(cfg, first, xq, xk, xv, braw, araw, hq, hk, hv, cw, alog, dtb):
    CW = cfg.CW
    tb = xq.shape[0]
    outs = []
    for idx, (x, xh) in enumerate(((xq, hq), (xk, hk), (xv, hv))):
        xx = jnp.concatenate([jnp.where(first, 0.0, xh), x], axis=0)
        w = cw[:, idx * CW:(idx + 1) * CW]
        acc = jnp.zeros((tb, CW), F32)
        base = C_HALO - (C_CONV - 1)
        for k in range(C_CONV):
            acc = acc + w[k:k + 1, :] * xx[base + k:base + k + tb, :]
        y = _silu(acc)
        if idx < 2:
            parts = []
            for h in range(cfg.CH):
                yh = y[:, C_HEAD * h:C_HEAD * (h + 1)]
                parts.append(yh * lax.rsqrt(jnp.sum(yh * yh, axis=-1, keepdims=True) + EPS))
            y = jnp.concatenate(parts, axis=1)
        outs.append(y)
    beta = jax.nn.sigmoid(braw)
    g = -jnp.exp(alog) * jax.nn.softplus(araw + dtb)
    return outs[0], outs[1], outs[2], g, beta


def _inverse_unit_lower(low, eye):
    pw = low
    inv = eye - low
    for _ in range(5):
        pwb = pw.astype(BF16)
        pw = _bdot(pwb, pwb, 2, 1, None)
        inv = inv + _bdot(inv.astype(BF16), pw.astype(BF16), 2, 1, None)
    ax = inv + _bdot(low, inv, 2, 1)
    return inv + _bdot(inv, eye - ax, 2, 1)


@jax.custom_vjp
def _saved_inverse(low, inv):
    return inv


def _saved_inverse_fwd(low, inv):
    return inv, inv


def _saved_inverse_bwd(inv, d):
    dlow = -_bdot(_bdot(inv, d, 1, 1), inv, 2, 2)
    return dlow, jnp.zeros_like(inv)


_saved_inverse.defvjp(_saved_inverse_fwd, _saved_inverse_bwd)


def gdn_intra_rows(cfg, first, qn, kn, v, g, beta, inv_saved=None):
    c = CHUNK
    CH = cfg.CH
    nchunk = qn.shape[0] // c
    i = lax.broadcasted_iota(jnp.int32, (c, c), 0)
    j = lax.broadcasted_iota(jnp.int32, (c, c), 1)
    incl = (i >= j)[None]
    strict = (i > j)[None]
    eye = (i == j).astype(F32)[None]
    tri = (i >= j).astype(F32)
    rows = [slice(c * ci, c * (ci + 1)) for ci in range(nchunk)]
    gcs = [jnp.dot(tri, g[r], precision=HI, preferred_element_type=F32) for r in rows]
    pairs = [(ci, h) for ci in range(nchunk) for h in range(CH)]
    heads = lambda x, wd: jnp.stack([x[rows[ci], wd * h:wd * (h + 1)] for ci, h in pairs], axis=0)
    gch = jnp.stack([gcs[ci][:, h:h + 1] for ci, h in pairs], axis=0)
    bh = jnp.stack([beta[rows[ci], h:h + 1] for ci, h in pairs], axis=0)
    q = heads(qn, C_HEAD) * (C_HEAD ** -0.5)
    k = heads(kn, C_HEAD)
    vv = heads(v, C_HEAD)
    a = jnp.broadcast_to(gch, (len(pairs), c, c))
    diff = jnp.where(incl, a - jnp.swapaxes(a, 1, 2), 0.0)
    decay = jnp.where(incl, jnp.exp(diff), 0.0)
    kb = k * bh
    low = jnp.where(strict, _bdot(kb, k, 2, 2) * decay, 0.0)
    if inv_saved is None:
        inv = _inverse_unit_lower(low, eye)
    else:
        inv = _saved_inverse(low, heads(inv_saved, c))
    eg = jnp.exp(gch)
    sol = _bdot(inv, jnp.concatenate([vv * bh, kb * eg], axis=2), 2, 1)
    intra = jnp.where(incl, _bdot(q, k, 2, 2) * decay, 0.0)
    qg = q * eg
    kd = k * jnp.exp(gch[:, c - 1:c, :] - gch)
    glast = jnp.concatenate([jnp.broadcast_to(gc[c - 1:c, :], gc.shape) for gc in gcs], axis=0)

    def unstack(x):
        return jnp.concatenate([jnp.concatenate([x[ci * CH + h] for h in range(CH)], axis=1) for ci in range(nchunk)],
                               axis=0)

    outs = (unstack(sol[:, :, :C_HEAD]), unstack(sol[:, :, C_HEAD:]), unstack(qg), unstack(kd), unstack(intra), glast)
    return outs + (unstack(inv),) if inv_saved is None else outs


def gdn_state_step(S, u, w, qg, kd, intra, glast):
    v_new = u - _bdot(w, S, 2, 1)
    o = _bdot(qg, S, 2, 1) + _bdot(intra, v_new, 2, 1)
    S_next = S * jnp.exp(glast) + _bdot(kd, v_new, 1, 1)
    return o, S_next


def gdn_out_block(cfg, first, o, zc, onw):
    parts = []
    for h in range(cfg.CH):
        sl = slice(C_HEAD * h, C_HEAD * (h + 1))
        oh = o[:, sl]
        y = oh * lax.rsqrt(jnp.mean(oh * oh, axis=-1, keepdims=True) + EPS) * onw
        parts.append(y * _silu(zc[:, sl]))
    return (jnp.concatenate(parts, axis=1),)


def rms_block(x, nw):
    return x * lax.rsqrt(jnp.mean(x * x, axis=-1, keepdims=True) + EPS) * nw


class Row:
    def __init__(self, arr, width, colblk=0, grad=None):
        self.arr, self.width, self.colblk, self.grad = arr, width, colblk, grad


class Halo:
    def __init__(self, arr, width, colblk, hr, tie=None):
        self.arr, self.width, self.colblk, self.hr, self.tie = arr, width, colblk, hr, tie


def _row_specs(tb, rows, halos, params, pos):
    specs = [pl.BlockSpec((tb, r.width), lambda i, cb=r.colblk: (pos(i), cb)) for r in rows]
    specs += [pl.BlockSpec((h.hr, h.width),
                           lambda i, cb=h.colblk, m=tb // h.hr: (jnp.maximum(pos(i) * m - 1, 0), cb))
              for h in halos]
    specs += [pl.BlockSpec(p.shape, lambda i: (0, 0)) for p in params]
    return specs


def rb_fwd(name, fn, n, tb, bps, rows, halos, params, outs):
    nr, nh, npar = len(rows), len(halos), len(params)

    def body(*refs):
        ins = refs[:nr + nh + npar]
        o_refs = refs[nr + nh + npar:]
        first = (pl.program_id(0) % bps) == 0
        res = fn(first, *[r[...] for r in ins])
        for ref, val in zip(o_refs, res):
            ref[...] = val.astype(ref.dtype)

    return pl.pallas_call(
        body, name=name, grid=(n // tb,),
        in_specs=_row_specs(tb, rows, halos, params, lambda i: i),
        out_specs=[pl.BlockSpec((tb, w), lambda i: (i, 0)) for w, _ in outs],
        out_shape=[jax.ShapeDtypeStruct((n, w), dt) for w, dt in outs],
        compiler_params=_cparams(("parallel",)),
    )(*[r.arr for r in rows], *[h.arr for h in halos], *params)


def rb_bwd(name, fn, n, tb, bps, rows, halos, params, douts, param_grads):
    nr, nh, npar, nd = len(rows), len(halos), len(params), len(douts)
    nblk = n // tb
    grow = [k for k, r in enumerate(rows) if r.grad is not None]
    ghalo = [k for k, h in enumerate(halos) if h.tie is not None]
    gpar = [k for k, f in enumerate(param_grads) if f]
    pos = lambda i: nblk - 1 - i

    def body(*refs):
        ins = refs[:nr + nh + npar]
        d_refs = refs[nr + nh + npar:nr + nh + npar + nd]
        rest = refs[nr + nh + npar + nd:]
        grow_refs = rest[:len(grow)]
        gpar_refs = rest[len(grow):len(grow) + len(gpar)]
        carry_refs = rest[len(grow) + len(gpar):]
        i = pl.program_id(0)
        first = (pos(i) % bps) == 0
        vals = [r[...] for r in ins]
        diff_idx = grow + [nr + k for k in ghalo] + [nr + nh + k for k in gpar]

        def f(*dargs):
            full = list(vals)
            for k, a in zip(diff_idx, dargs):
                full[k] = a
            return fn(first, *full)

        res, vjp = jax.vjp(f, *[vals[k] for k in diff_idx])
        grads = vjp(tuple(d[...].astype(r.dtype) for d, r in zip(d_refs, res)))
        g_rows = list(grads[:len(grow)])
        g_halos = grads[len(grow):len(grow) + len(ghalo)]
        g_pars = grads[len(grow) + len(ghalo):]

        @pl.when(i == 0)
        def _():
            for c in carry_refs:
                c[...] = jnp.zeros_like(c)
            for p in gpar_refs:
                p[...] = jnp.zeros_like(p)

        for k, ref in enumerate(grow_refs):
            ref[...] = g_rows[k].astype(ref.dtype)
        for ci, hk in enumerate(ghalo):
            h = halos[hk]
            k = grow.index(h.tie)
            tail = g_rows[k][tb - h.hr:, :] + carry_refs[ci][...]
            grow_refs[k][tb - h.hr:, :] = tail.astype(grow_refs[k].dtype)
            carry_refs[ci][...] = g_halos[ci]
        for ref, gp in zip(gpar_refs, g_pars):
            ref[...] += gp

    out_specs = [pl.BlockSpec((tb, rows[k].width), lambda i: (pos(i), 0)) for k in grow]
    out_specs += [pl.BlockSpec(params[k].shape, lambda i: (0, 0)) for k in gpar]
    out_shape = [jax.ShapeDtypeStruct((n, rows[k].width), rows[k].grad) for k in grow]
    out_shape += [jax.ShapeDtypeStruct(params[k].shape, F32) for k in gpar]
    in_specs = _row_specs(tb, rows, halos, params, pos)
    in_specs += [pl.BlockSpec((tb, d.shape[1]), lambda i: (pos(i), 0)) for d in douts]
    return pl.pallas_call(
        body, name=name, grid=(nblk,), in_specs=in_specs, out_specs=out_specs, out_shape=out_shape,
        scratch_shapes=[pltpu.VMEM((halos[k].hr, halos[k].width), F32) for k in ghalo],
        compiler_params=_cparams(("arbitrary",)),
    )(*[r.arr for r in rows], *[h.arr for h in halos], *params, *douts)


_DIMS = {'nn': (((1,), (0,)), ((), ())), 'nt': (((1,), (1,)), ((), ())), 'tn': (((0,), (0,)), ((), ()))}


def matmul(name, a, b, mode, tm, tn, tk, out_dtype=F32, add=None, comm=None):
    if mode == 'tn':
        K, M = a.shape
    else:
        M, K = a.shape
    N = b.shape[0] if mode == 'nt' else b.shape[1]
    tm, tn, tk = min(tm, M), min(tn, N), min(tk, K)
    assert M % tm == 0 and N % tn == 0 and K % tk == 0, (name, M, N, K, tm, tn, tk)
    nk = K // tk
    a_spec = pl.BlockSpec((tk, tm), lambda i, j, k: (k, i)) if mode == 'tn' else pl.BlockSpec((tm, tk), lambda i, j, k: (i, k))
    b_spec = pl.BlockSpec((tn, tk), lambda i, j, k: (j, k)) if mode == 'nt' else pl.BlockSpec((tk, tn), lambda i, j, k: (k, j))
    o_spec = pl.BlockSpec((tm, tn), lambda i, j, k: (i, j))
    has_add = add is not None

    def body(*refs):
        a_ref, b_ref = refs[0], refs[1]
        add_ref = refs[2] if has_add else None
        o_ref = refs[-1]
        k = pl.program_id(2)
        part = lax.dot_general(a_ref[...].astype(BF16), b_ref[...].astype(BF16), _DIMS[mode], preferred_element_type=F32)

        @pl.when(k == 0)
        def _():
            o_ref[...] = ((part + add_ref[...]) if has_add else part).astype(o_ref.dtype)

        if nk > 1:
            @pl.when(k > 0)
            def _():
                o_ref[...] += part

    assert nk == 1 or out_dtype == F32
    ins = [a, b] + ([add] if has_add else [])
    in_specs = [a_spec, b_spec] + ([o_spec] if has_add else [])
    outs, couts = call_with_comm(body, name, (M // tm, N // tn, nk), in_specs, [o_spec],
                                 [jax.ShapeDtypeStruct((M, N), out_dtype)], [], ("parallel", "parallel", "arbitrary"),
                                 ins, comm)
    return (outs[0], couts) if comm is not None else outs[0]


def norm_in_proj(name, x, nw, wp, tm, tn, comm=None):
    n, d = x.shape
    wpc = wp.shape[1]
    tm, tn = min(tm, n), min(tn, wpc)
    assert n % tm == 0 and wpc % tn == 0

    def body(x_ref, nw_ref, w_ref, p_ref, h_ref):
        @pl.when(pl.program_id(1) == 0)
        def _():
            h_ref[...] = rms_block(x_ref[...], nw_ref[...]).astype(BF16)

        p_ref[...] = jnp.dot(h_ref[...], w_ref[...], preferred_element_type=F32)

    outs, couts = call_with_comm(
        body, name, (n // tm, wpc // tn),
        [pl.BlockSpec((tm, d), lambda i, j: (i, 0)), pl.BlockSpec((1, d), lambda i, j: (0, 0)),
         pl.BlockSpec((d, tn), lambda i, j: (0, j))],
        [pl.BlockSpec((tm, tn), lambda i, j: (i, j)), pl.BlockSpec((tm, d), lambda i, j: (i, 0))],
        [jax.ShapeDtypeStruct((n, wpc), F32), jax.ShapeDtypeStruct((n, d), BF16)], [], ("parallel", "arbitrary"),
        [x, nw, wp], comm)
    return (outs[0], outs[1], couts) if comm is not None else (outs[0], outs[1])


def norm_bwd(name, x, nw, dh, dres, tb):
    n, d = x.shape
    tb = min(tb, n)

    def body(x_ref, nw_ref, dh_ref, dres_ref, dx_ref, dnw_ref):
        @pl.when(pl.program_id(0) == 0)
        def _():
            dnw_ref[...] = jnp.zeros_like(dnw_ref)

        _, vjp = jax.vjp(rms_block, x_ref[...], nw_ref[...])
        dx, dnw = vjp(dh_ref[...])
        dx_ref[...] = dx + dres_ref[...]
        dnw_ref[...] += dnw

    row = pl.BlockSpec((tb, d), lambda i: (i, 0))
    par = pl.BlockSpec((1, d), lambda i: (0, 0))
    return pl.pallas_call(
        body, name=name, grid=(n // tb,), in_specs=[row, par, row, row], out_specs=[row, par],
        out_shape=[jax.ShapeDtypeStruct((n, d), F32), jax.ShapeDtypeStruct((1, d), F32)],
        compiler_params=_cparams(("arbitrary",)),
    )(x, nw, dh, dres)


def loss_grad(name, y, target, tb):
    n, d = y.shape
    tb = min(tb, n)

    def body(y_ref, t_ref, dy_ref, loss_ref):
        @pl.when(pl.program_id(0) == 0)
        def _():
            loss_ref[...] = jnp.zeros_like(loss_ref)

        err = y_ref[...] - t_ref[...]
        dy_ref[...] = err * (1.0 / d)
        part = 0.5 * jnp.sum(jnp.mean(err * err, axis=-1, keepdims=True), axis=0, keepdims=True)
        loss_ref[...] += jnp.broadcast_to(part, loss_ref.shape)

    row = pl.BlockSpec((tb, d), lambda i: (i, 0))
    dy, loss = pl.pallas_call(
        body, name=name, grid=(n // tb,), in_specs=[row, row],
        out_specs=[row, pl.BlockSpec((8, LANE), lambda i: (0, 0))],
        out_shape=[jax.ShapeDtypeStruct((n, d), F32), jax.ShapeDtypeStruct((8, LANE), F32)],
        compiler_params=_cparams(("arbitrary",)),
    )(y, target)
    return dy, loss[0, 0]


def rope_tables(name, pos_col, inv_freq_row):
    n = pos_col.shape[0]

    def body(p_ref, f_ref, c_ref, s_ref):
        ang = p_ref[...].astype(F32) * f_ref[...]
        lane = lax.broadcasted_iota(jnp.int32, ang.shape, 1)
        c_ref[...] = jnp.where(lane < ROT_DIM, jnp.cos(ang), 1.0)
        s_ref[...] = jnp.where(lane < ROT_DIM, jnp.sin(ang), 0.0)

    return pl.pallas_call(
        body, name=name, out_shape=[jax.ShapeDtypeStruct((n, A_HEAD), F32)] * 2,
    )(pos_col, inv_freq_row)


def _scan_operands(cfg, nseq, u_ref, w_ref, qg_ref, kd_ref, a_ref, gl_ref):
    pairs = [(b, h) for b in range(nseq) for h in range(cfg.CH)]
    st = lambda r, wd: jnp.stack([r[b, :, wd * h:wd * (h + 1)] for b, h in pairs], axis=0)
    gl = jnp.stack([gl_ref[b, 0:1, h:h + 1] for b, h in pairs], axis=0)
    return st(u_ref, C_HEAD), st(w_ref, C_HEAD), st(qg_ref, C_HEAD), st(kd_ref, C_HEAD), st(a_ref, CHUNK), gl


def gdn_scan_fwd(name, cfg, nseq, u, w, qg, kd, intra, glast):
    CH, CW, T = cfg.CH, cfg.CW, cfg.T
    nc = T // CHUNK

    def body(u_ref, w_ref, qg_ref, kd_ref, a_ref, gl_ref, o_ref, sin_ref, s_ref):
        @pl.when(pl.program_id(0) == 0)
        def _():
            s_ref[...] = jnp.zeros_like(s_ref)

        S = s_ref[...]
        for b in range(nseq):
            sin_ref[b, 0] = S[b * CH:(b + 1) * CH]
        o, S_next = gdn_state_step(S, *_scan_operands(cfg, nseq, u_ref, w_ref, qg_ref, kd_ref, a_ref, gl_ref))
        s_ref[...] = S_next
        for b in range(nseq):
            o_ref[b] = jnp.concatenate([o[b * CH + h] for h in range(CH)], axis=1)

    row = lambda wd: pl.BlockSpec((nseq, CHUNK, wd), lambda c: (0, c, 0))
    widths = [CW, CW, CW, CW, CH * CHUNK, LANE]
    o, s_in = pl.pallas_call(
        body, name=name, grid=(nc,),
        in_specs=[row(x) for x in widths],
        out_specs=[row(CW), pl.BlockSpec((nseq, 1, CH, C_HEAD, C_HEAD), lambda c: (0, c, 0, 0, 0))],
        out_shape=[jax.ShapeDtypeStruct((nseq, T, CW), F32),
                   jax.ShapeDtypeStruct((nseq, nc, CH, C_HEAD, C_HEAD), F32)],
        scratch_shapes=[pltpu.VMEM((nseq * CH, C_HEAD, C_HEAD), F32)],
        compiler_params=_cparams(("arbitrary",)),
    )(*[a.reshape(nseq, T, a.shape[1]) for a in (u, w, qg, kd, intra, glast)])
    return o.reshape(nseq * T, CW), s_in


def gdn_scan_bwd(name, cfg, nseq, u, w, qg, kd, intra, glast, s_in, do, comm=None):
    CH, CW, T = cfg.CH, cfg.CW, cfg.T
    nc = T // CHUNK

    def body(u_ref, w_ref, qg_ref, kd_ref, a_ref, gl_ref, sin_ref, do_ref,
             du_ref, dw_ref, dqg_ref, dkd_ref, da_ref, dgl_ref, ds_ref):
        @pl.when(pl.program_id(0) == 0)
        def _():
            ds_ref[...] = jnp.zeros_like(ds_ref)

        S = jnp.concatenate([sin_ref[b, 0] for b in range(nseq)], axis=0)
        dout = jnp.stack([do_ref[b, :, C_HEAD * h:C_HEAD * (h + 1)] for b in range(nseq) for h in range(CH)], axis=0)
        _, vjp = jax.vjp(gdn_state_step, S, *_scan_operands(cfg, nseq, u_ref, w_ref, qg_ref, kd_ref, a_ref, gl_ref))
        dS, du, dw, dqg, dkd, da, dg = vjp((dout, ds_ref[...]))
        ds_ref[...] = dS
        lane = lax.broadcasted_iota(jnp.int32, (CHUNK, LANE), 1)
        rowi = lax.broadcasted_iota(jnp.int32, (CHUNK, LANE), 0)
        for b in range(nseq):
            cat = lambda x: jnp.concatenate([x[b * CH + h] for h in range(CH)], axis=1)
            du_ref[b] = cat(du)
            dw_ref[b] = cat(dw)
            dqg_ref[b] = cat(dqg)
            dkd_ref[b] = cat(dkd)
            da_ref[b] = cat(da)
            dgl = jnp.zeros((CHUNK, LANE), F32)
            for h in range(CH):
                dgl = dgl + jnp.where((lane == h) & (rowi == 0), dg[b * CH + h], 0.0)
            dgl_ref[b] = dgl

    row = lambda wd: pl.BlockSpec((nseq, CHUNK, wd), lambda c: (0, nc - 1 - c, 0))
    widths = [CW, CW, CW, CW, CH * CHUNK, LANE]
    outs, carried = call_with_comm(
        body, name, (nc,),
        [row(x) for x in widths]
        + [pl.BlockSpec((nseq, 1, CH, C_HEAD, C_HEAD), lambda c: (0, nc - 1 - c, 0, 0, 0)), row(CW)],
        [row(x) for x in widths], [jax.ShapeDtypeStruct((nseq, T, x), F32) for x in widths],
        [pltpu.VMEM((nseq * CH, C_HEAD, C_HEAD), F32)], ("arbitrary",),
        [a.reshape(nseq, T, a.shape[1]) for a in (u, w, qg, kd, intra, glast)] + [s_in, do.reshape(nseq, T, CW)], comm)
    return [o.reshape(nseq * T, o.shape[2]) for o in outs], carried


def _tile(total, cap, unit=LANE):
    best = None
    for t in range(unit, min(cap, total) + 1, unit):
        if total % t == 0:
            best = t
    assert best is not None, (total, cap, unit)
    return best


def _pad_lanes(v, width=LANE):
    return jnp.pad(v.reshape(1, -1), ((0, 0), (0, width - v.shape[-1])))


def permute_w_in(cfg, w):
    parts = []
    for n in cfg.order:
        off, wd = cfg.orig[n]
        blk = w[:, off:off + wd]
        if cfg.g[n][1] != wd:
            blk = jnp.pad(blk, ((0, 0), (0, cfg.g[n][1] - wd)))
        parts.append(blk)
    return jnp.concatenate(parts, axis=1)


def unpermute_w_in(cfg, wp):
    parts = []
    for n, (off, wd) in cfg.orig.items():
        parts.append(wp[:, cfg.g[n][0]:cfg.g[n][0] + wd])
    return jnp.concatenate(parts, axis=1)


def _layer_params(cfg, prm):
    return dict(
        nw=prm['norm_w'].reshape(1, -1),
        qnw=prm['q_norm_w'].reshape(1, -1), knw=prm['k_norm_w'].reshape(1, -1),
        sinks_row=jnp.repeat(prm['sinks'], A_HEAD).reshape(1, -1),
        cw=prm['b_conv_w'], cb=prm['b_conv_b'].reshape(1, -1),
        lw=prm['b_ln_w'].reshape(1, -1), lb=prm['b_ln_b'].reshape(1, -1),
        pw=prm['b_pw_w'], pb=prm['b_pw_b'].reshape(1, -1),
        ccw=prm['c_conv_w'], alog=_pad_lanes(prm['c_a_log']), dtb=_pad_lanes(prm['c_dt_bias']),
        onw=prm['c_onorm_w'].reshape(1, -1),
    )


def _attn_io(cfg, p, cos, sin, grads):
    gq = BF16 if grads else None
    rows = [Row(p, cfg.AW, cfg.blk('qa'), gq), Row(p, cfg.AW, cfg.blk('za'), gq),
            Row(p, cfg.AKW, cfg.blk('ka'), gq), Row(p, cfg.AKW, cfg.blk('va'), gq),
            Row(cos, A_HEAD), Row(sin, A_HEAD)]
    halos = [Halo(p, cfg.AKW, cfg.blk('ka'), ATTN_BLOCK, 2 if grads else None),
             Halo(p, cfg.AKW, cfg.blk('va'), ATTN_BLOCK, 3 if grads else None),
             Halo(cos, A_HEAD, 0, ATTN_BLOCK), Halo(sin, A_HEAD, 0, ATTN_BLOCK)]
    return rows, halos


def _conv_io(cfg, p, grads):
    gq = BF16 if grads else None
    rows = [Row(p, 2 * cfg.BW, cfg.blk('ub'), gq), Row(p, cfg.BW, cfg.blk('zb'), gq)]
    halos = [Halo(p, 2 * cfg.BW, cfg.blk('ub'), B_HALO, 0 if grads else None)]
    return rows, halos


def _prep_io(cfg, p, grads):
    gq = BF16 if grads else None
    rows = [Row(p, cfg.CW, cfg.blk(n), gq) for n in ('qc', 'kc', 'vc')]
    rows += [Row(p, LANE, cfg.blk('bc'), gq), Row(p, LANE, cfg.blk('ac'), gq)]
    halos = [Halo(p, cfg.CW, cfg.blk(n), C_HALO, k if grads else None) for k, n in enumerate(('qc', 'kc', 'vc'))]
    return rows, halos


TB_CONV = 128
TB_PREP = 256
TB_OUT = 256
TB_INTRA_FWD = 128
TB_INTRA_BWD = 128


def layer_forward(cfg, l, x, lp, wp, wo, cos, sin, comm=None):
    n = x.shape[0]
    nseq = n // cfg.T
    T = cfg.T
    p, h, *carried = norm_in_proj(f"in_proj_{l}", x, lp['nw'], wp, 1024, _tile(cfg.WP, 768), comm=comm)
    rows, halos = _attn_io(cfg, p, cos, sin, False)
    (oa,) = rb_fwd(f"attn_fwd_{l}", functools.partial(attn_block, cfg), n, ATTN_BLOCK, T // ATTN_BLOCK, rows, halos,
                   [lp['qnw'], lp['knw'], lp['sinks_row']], [(cfg.AW, BF16)])
    rows, halos = _conv_io(cfg, p, False)
    tbb = min(TB_CONV, T)
    (ob,) = rb_fwd(f"conv_fwd_{l}", functools.partial(conv_block, cfg), n, tbb, T // tbb, rows, halos,
                   [lp['cw'], lp['cb'], lp['lw'], lp['lb'], lp['pw'], lp['pb']], [(cfg.BW, BF16)])
    rows, halos = _prep_io(cfg, p, False)
    tbp = min(TB_PREP, T)
    qn, kn, v, g, beta = rb_fwd(f"gdn_prep_fwd_{l}", functools.partial(gdn_prep_block, cfg), n, tbp, T // tbp, rows,
                                halos, [lp['ccw'], lp['alog'], lp['dtb']],
                                [(cfg.CW, F32)] * 3 + [(LANE, F32)] * 2)
    intra_outs = rb_fwd(f"gdn_intra_fwd_{l}", functools.partial(gdn_intra_rows, cfg), n, TB_INTRA_FWD, T // TB_INTRA_FWD,
                        [Row(qn, cfg.CW), Row(kn, cfg.CW), Row(v, cfg.CW), Row(g, LANE), Row(beta, LANE)], [], [],
                        [(cfg.CW, F32)] * 4 + [(cfg.CH * CHUNK, F32), (LANE, F32), (cfg.CH * CHUNK, F32)])
    intra_outs, inv = intra_outs[:6], intra_outs[6]
    o, s_in = gdn_scan_fwd(f"gdn_scan_fwd_{l}", cfg, nseq, *intra_outs)
    tbo = min(TB_OUT, T)
    (oc,) = rb_fwd(f"gdn_out_fwd_{l}", functools.partial(gdn_out_block, cfg), n, tbo, T // tbo,
                   [Row(o, cfg.CW), Row(p, cfg.CW, cfg.blk('zc'))], [], [lp['onw']], [(cfg.CW, BF16)])
    y = jnp.concatenate([oa, ob, oc], axis=1)
    x_next = matmul(f"out_proj_{l}", y, wo, 'nn', 1024, 1024, cfg.D, add=x)
    saved = dict(x=x, p=p, h=h, y=y, qn=qn, kn=kn, v=v, g=g, beta=beta, intra_outs=intra_outs, inv=inv, s_in=s_in, o=o)
    return x_next, saved, (carried[0] if carried else None)


def layer_backward(cfg, l, dxn, sv, lp, wp, wo, cos, sin, rs=None):
    n = dxn.shape[0]
    nseq = n // cfg.T
    T = cfg.T
    p = sv['p']
    AW, BW, CW = cfg.AW, cfg.BW, cfg.CW
    if rs is None:
        dy = matmul(f"dy_{l}", dxn, wo, 'nt', 1024, 1024, cfg.D)
    else:
        dy, received = matmul(f"dy_{l}", dxn, wo, 'nt', 1024, 1024, cfg.D, comm=rs.swap())
    dwo = matmul(f"dwo_{l}", sv['y'], dxn, 'tn', 1024, 1024, 2048)
    doa, dob, doc = dy[:, :AW], dy[:, AW:AW + BW], dy[:, AW + BW:]
    tbo = min(TB_OUT, T)
    do, dzc, donw = rb_bwd(f"gdn_out_bwd_{l}", functools.partial(gdn_out_block, cfg), n, tbo, T // tbo,
                           [Row(sv['o'], CW, 0, F32), Row(p, CW, cfg.blk('zc'), BF16)], [], [lp['onw']], [doc], [True])
    dintra, got = gdn_scan_bwd(f"gdn_scan_bwd_{l}", cfg, nseq, *sv['intra_outs'], sv['s_in'], do,
                               comm=None if rs is None else rs.scatter(received))
    dqn, dkn, dv, dg, dbeta = rb_bwd(
        f"gdn_intra_bwd_{l}", functools.partial(gdn_intra_rows, cfg), n, TB_INTRA_BWD, T // TB_INTRA_BWD,
        [Row(sv['qn'], CW, 0, F32), Row(sv['kn'], CW, 0, F32), Row(sv['v'], CW, 0, F32), Row(sv['g'], LANE, 0, F32),
         Row(sv['beta'], LANE, 0, F32), Row(sv['inv'], cfg.CH * CHUNK)], [], [], list(dintra), [])
    rows, halos = _prep_io(cfg, p, True)
    tbp = min(TB_PREP, T)
    dqc, dkc, dvc, dbc, dac, dccw, dalog, ddtb = rb_bwd(
        f"gdn_prep_bwd_{l}", functools.partial(gdn_prep_block, cfg), n, tbp, T // tbp, rows, halos,
        [lp['ccw'], lp['alog'], lp['dtb']], [dqn, dkn, dv, dg, dbeta], [True] * 3)
    rows, halos = _conv_io(cfg, p, True)
    tbb = min(TB_CONV, T)
    dub, dzb, dcw, dcb, dlw, dlb, dpw, dpb = rb_bwd(
        f"conv_bwd_{l}", functools.partial(conv_block, cfg), n, tbb, T // tbb, rows, halos,
        [lp['cw'], lp['cb'], lp['lw'], lp['lb'], lp['pw'], lp['pb']], [dob], [True] * 6)
    rows, halos = _attn_io(cfg, p, cos, sin, True)
    dqa, dza, dka, dva, dqnw, dknw, dsinks_row = rb_bwd(
        f"attn_bwd_{l}", functools.partial(attn_block, cfg), n, ATTN_BLOCK, T // ATTN_BLOCK, rows, halos,
        [lp['qnw'], lp['knw'], lp['sinks_row']], [doa], [True] * 3)
    dgroups = dict(qa=dqa, za=dza, qc=dqc, kc=dkc, vc=dvc, zc=dzc, ka=dka, va=dva, ub=dub, zb=dzb, bc=dbc, ac=dac)
    dp = jnp.concatenate([dgroups[k] for k in cfg.order], axis=1)
    dh = matmul(f"dh_{l}", dp, wp, 'nt', 1024, 1024, _tile(cfg.WP, 2304))
    dwp = matmul(f"dwp_{l}", sv['h'], dp, 'tn', 1024, _tile(cfg.WP, 1152), 2048)
    dx, dnw = norm_bwd(f"norm_bwd_{l}", sv['x'], lp['nw'], dh, dxn, 256)
    grads = dict(
        norm_w=dnw[0], w_in_perm=dwp, q_norm_w=dqnw[0], k_norm_w=dknw[0],
        sinks=dsinks_row.reshape(cfg.AQH, A_HEAD)[:, 0],
        b_conv_w=dcw, b_conv_b=dcb[0], b_ln_w=dlw[0], b_ln_b=dlb[0], b_pw_w=dpw, b_pw_b=dpb[0],
        c_conv_w=dccw, c_a_log=dalog[0, :cfg.CH], c_dt_bias=ddtb[0, :cfg.CH], c_onorm_w=donw[0], w_out=dwo)
    return dx, grads, got


def rope_for(cfg, positions):
    n = positions.size
    inv_freq = ROPE_THETA ** (-np.arange(0, ROT_DIM, 2, dtype=np.float32) / ROT_DIM)
    freq_row = np.zeros((1, A_HEAD), np.float32)
    freq_row[0, :ROT_DIM] = np.concatenate([inv_freq, inv_freq])
    return rope_tables("rope_tables", positions.reshape(n, 1), jnp.asarray(freq_row))


def local_step(cfg, x, positions, prm, wps, wos, target):
    nseq = x.shape[0]
    n = nseq * cfg.T
    cos, sin = rope_for(cfg, positions)
    lps = [_layer_params(cfg, {k: v[l] for k, v in prm.items()}) for l in range(DEPTH)]
    saved = []
    xl = x.reshape(n, cfg.D)
    for l in range(DEPTH):
        xl, sv, _ = layer_forward(cfg, l, xl, lps[l], wps[l], wos[l], cos, sin)
        saved.append(sv)
    dx, loss = loss_grad("loss_grad", xl, target.reshape(n, cfg.D), 256)
    grads = [None] * DEPTH
    for l in reversed(range(DEPTH)):
        dx, grads[l], _ = layer_backward(cfg, l, dx, saved[l], lps[l], wps[l], wos[l], cos, sin)
    return loss, dx.reshape(x.shape), grads


N_CHIPS = 4
N_DEV = 8


def _place():
    return lax.axis_index("x"), lax.axis_index("y"), lax.axis_index("c")


def _other_chips(x, y):
    return [(1 - x, y), (x, 1 - y), (1 - x, 1 - y)]


def _remote(src, dst, send, recv, to):
    return pltpu.make_async_remote_copy(src_ref=src, dst_ref=dst, send_sem=send, recv_sem=recv, device_id=to,
                                        device_id_type=MESH)


def gather_comm(arrs):
    n = len(arrs)

    def first_copies(ins, outs, send, recv):
        x, y, c = _place()
        me = 2 * x + y
        return [_remote(ins[i].at[c], outs[i].at[me, c], send.at[i, j], recv.at[i, j], (cx, cy, c))
                for i in range(n) for j, (cx, cy) in enumerate(_other_chips(x, y))]

    def start(ins, outs, sems):
        for cp in first_copies(ins, outs, *sems):
            cp.start()

    def finish(ins, outs, sems):
        send, recv = sems
        x, y, c = _place()
        chips = _other_chips(x, y)
        sib = (x, y, 1 - c)
        passed = []
        for i in range(n):
            for j, (cx, cy) in enumerate(chips):
                blk = outs[i].at[2 * cx + cy, c]
                _remote(blk, blk, send.at[i, j], recv.at[i, j], (x, y, c)).wait_recv()
                cp = _remote(blk, blk, send.at[i, 3 + j], recv.at[i, 3 + j], sib)
                cp.start()
                passed.append(cp)
        for i in range(n):
            for j, (cx, cy) in enumerate(chips):
                blk = outs[i].at[2 * cx + cy, 1 - c]
                _remote(blk, blk, send.at[i, 3 + j], recv.at[i, 3 + j], sib).wait_recv()
        for cp in first_copies(ins, outs, send, recv) + passed:
            cp.wait_send()

    return Comm(arrs, [jax.ShapeDtypeStruct((N_CHIPS,) + a.shape, a.dtype) for a in arrs],
                [pltpu.SemaphoreType.DMA((n, 6)), pltpu.SemaphoreType.DMA((n, 6))], start, finish)


def fill_own(gathered, arrs):
    me = 2 * lax.axis_index("x") + lax.axis_index("y")
    return [lax.dynamic_update_index_in_dim(o, a, me, 0) for o, a in zip(gathered, arrs)]


def swap_comm(arrs):
    n = len(arrs)

    def copies(ins, outs, send, recv):
        x, y, c = _place()
        return [_remote(ins[i].at[:, 1 - c], outs[i], send.at[i], recv.at[i], (x, y, 1 - c)) for i in range(n)]

    def start(ins, outs, sems):
        for cp in copies(ins, outs, *sems):
            cp.start()

    def finish(ins, outs, sems):
        for cp in copies(ins, outs, *sems):
            cp.wait()

    return Comm(arrs, [jax.ShapeDtypeStruct((a.shape[0],) + a.shape[2:], a.dtype) for a in arrs],
                [pltpu.SemaphoreType.DMA((n,)), pltpu.SemaphoreType.DMA((n,))], start, finish)


def scatter_comm(arrs):
    n = len(arrs)

    def copies(ins, outs, send, recv):
        x, y, c = _place()
        return [_remote(ins[i].at[2 * cx + cy], outs[i].at[j], send.at[i, j], recv.at[i, j], (cx, cy, c))
                for i in range(n) for j, (cx, cy) in enumerate(_other_chips(x, y))]

    def start(ins, outs, sems):
        for cp in copies(ins, outs, *sems):
            cp.start()

    def finish(ins, outs, sems):
        send, recv = sems
        x, y, c = _place()
        for i in range(n):
            for j in range(3):
                blk = outs[i].at[j]
                _remote(blk, blk, send.at[i, j], recv.at[i, j], (x, y, c)).wait_recv()
        for cp in copies(ins, outs, send, recv):
            cp.wait_send()

    return Comm(arrs, [jax.ShapeDtypeStruct((3,) + a.shape[1:], a.dtype) for a in arrs],
                [pltpu.SemaphoreType.DMA((n, 3)), pltpu.SemaphoreType.DMA((n, 3))], start, finish)


def share_comm(arrs):
    n = len(arrs)

    def copies(outs, send, recv):
        x, y, c = _place()
        return [_remote(outs[i].at[c], outs[i].at[c], send.at[i], recv.at[i], (x, y, 1 - c)) for i in range(n)]

    def start(ins, outs, sems):
        for cp in copies(outs, *sems):
            cp.start()

    def finish(ins, outs, sems):
        send, recv = sems
        x, y, c = _place()
        for i in range(n):
            blk = outs[i].at[1 - c]
            _remote(blk, blk, send.at[i], recv.at[i], (x, y, c)).wait_recv()
        for cp in copies(outs, send, recv):
            cp.wait_send()

    return Comm(arrs, [jax.ShapeDtypeStruct(a.shape, a.dtype) for a in arrs],
                [pltpu.SemaphoreType.DMA((n,)), pltpu.SemaphoreType.DMA((n,))], start, finish,
                aliases={i: i for i in range(n)})


def all_reduce_small(name, packed):
    r = packed.shape[0]

    def body(in_ref, out_ref, buf, send, recv):
        x, y, c = _place()
        me = 4 * x + 2 * y + c
        buf[me] = in_ref[...]
        flips = [(fx, fy, fc) for fx in (0, 1) for fy in (0, 1) for fc in (0, 1) if (fx, fy, fc) != (0, 0, 0)]
        peers = [((x + fx) % 2, (y + fy) % 2, (c + fc) % 2) for fx, fy, fc in flips]
        cps = [_remote(in_ref, buf.at[me], send.at[k], recv.at[k], peer) for k, peer in enumerate(peers)]
        for cp in cps:
            cp.start()
        for k, (px, py, pc) in enumerate(peers):
            blk = buf.at[4 * px + 2 * py + pc]
            _remote(blk, blk, send.at[k], recv.at[k], (x, y, c)).wait_recv()
        for cp in cps:
            cp.wait_send()
        acc = buf[0]
        for d in range(1, N_DEV):
            acc = acc + buf[d]
        out_ref[...] = acc

    vm = pl.BlockSpec(memory_space=pltpu.VMEM)
    return pl.pallas_call(
        body, name=name, in_specs=[vm], out_specs=vm, out_shape=jax.ShapeDtypeStruct(packed.shape, F32),
        scratch_shapes=[pltpu.VMEM((N_DEV, r, LANE), F32), pltpu.SemaphoreType.DMA((N_DEV - 1,)),
                        pltpu.SemaphoreType.DMA((N_DEV - 1,))],
    )(packed)


def add_own_half(name, g, a, c_idx, tr):
    nch, _, r, cc = g.shape
    tr = min(tr, r)

    def body(c_ref, g_ref, a_ref, o_ref):
        o_ref[...] = (g_ref[0] + a_ref[...]).astype(o_ref.dtype)

    return pl.pallas_call(
        body, name=name,
        grid_spec=pltpu.PrefetchScalarGridSpec(
            num_scalar_prefetch=1, grid=(nch, r // tr),
            in_specs=[pl.BlockSpec((1, 1, tr, cc), lambda j, i, c_ref: (j, c_ref[0], i, 0)),
                      pl.BlockSpec((1, tr, cc), lambda j, i, c_ref: (j, i, 0))],
            out_specs=pl.BlockSpec((1, tr, cc), lambda j, i, c_ref: (j, i, 0))),
        out_shape=jax.ShapeDtypeStruct(a.shape, BF16),
        compiler_params=_cparams(("parallel", "parallel")),
    )(c_idx, g, a)


def sum_chips(name, p, b, idx, tr):
    _, r, cc = p.shape
    tr = min(tr, r)

    def body(idx_ref, p_ref, b_ref, o_ref):
        acc = p_ref[0].astype(F32)
        for k in range(3):
            acc = acc + b_ref[k].astype(F32)
        o_ref[0] = acc

    return pl.pallas_call(
        body, name=name,
        grid_spec=pltpu.PrefetchScalarGridSpec(
            num_scalar_prefetch=1, grid=(r // tr,),
            in_specs=[pl.BlockSpec((1, tr, cc), lambda i, s: (s[0], i, 0)),
                      pl.BlockSpec((3, tr, cc), lambda i, s: (0, i, 0))],
            out_specs=pl.BlockSpec((1, tr, cc), lambda i, s: (s[1], i, 0))),
        out_shape=jax.ShapeDtypeStruct((2, r, cc), F32),
        compiler_params=_cparams(("parallel",)),
    )(idx, p, b)


class GradReduce:
    def __init__(self, tag, parts, chip, c_idx):
        self.tag, self.c_idx = tag, c_idx
        self.parts = [p.reshape(p.shape[0], 2, p.shape[1] // 2, p.shape[2]) for p in parts]
        self.idx = jnp.concatenate([chip.astype(jnp.int32).reshape(1), c_idx])

    def swap(self):
        return swap_comm(self.parts)

    def scatter(self, received):
        self.part = [add_own_half(f"rs{self.tag}_add_sibling_{t}", g, a, self.c_idx, 128)
                     for t, (g, a) in enumerate(zip(self.parts, received))]
        return scatter_comm(self.part)

    def finish(self, got):
        red = [sum_chips(f"rs{self.tag}_sum_chips_{t}", p, b, self.idx, 128) for t, (p, b) in enumerate(zip(self.part, got))]
        out = run_comm(f"rs{self.tag}_share_halves", share_comm(red))
        return [o.reshape(-1, o.shape[-1]) for o in out]

    def run(self):
        received = run_comm(f"rs{self.tag}_swap_halves", self.swap())
        got = run_comm(f"rs{self.tag}_scatter_chips", self.scatter(received))
        return self.finish(got)


def adamw(name, w, g, m, v, tb):
    r, cc = w.shape
    tb = min(tb, r)
    assert r % tb == 0

    def body(w_ref, g_ref, m_ref, v_ref, d_ref, mo_ref, vo_ref):
        g = g_ref[...]
        m = ADAM_B1 * m_ref[...] + (1.0 - ADAM_B1) * g
        v = ADAM_B2 * v_ref[...] + (1.0 - ADAM_B2) * jnp.square(g)
        m_hat = m / (1.0 - ADAM_B1 ** ADAM_STEP)
        v_hat = v / (1.0 - ADAM_B2 ** ADAM_STEP)
        d_ref[...] = -ADAM_LR * (m_hat / (jnp.sqrt(v_hat) + ADAM_EPS) + ADAM_WD * w_ref[...])
        mo_ref[...] = m
        vo_ref[...] = v

    spec = pl.BlockSpec((tb, cc), lambda i: (i, 0))
    return pl.pallas_call(
        body, name=name, grid=(r // tb,), in_specs=[spec] * 4, out_specs=[spec] * 3,
        out_shape=[jax.ShapeDtypeStruct((r, cc), F32)] * 3,
        compiler_params=_cparams(("parallel",)),
    )(w, g, m, v)


def adamw_layers(name, w, g0, g1, m, v, tb):
    _, r, cc = w.shape
    tb = min(tb, r)
    nb = r // tb

    def body(w_ref, g0_ref, g1_ref, m_ref, v_ref, g_ref, d_ref, mo_ref, vo_ref):
        g = jnp.where(pl.program_id(0) == 0, g0_ref[...], g1_ref[...])
        m = ADAM_B1 * m_ref[0] + (1.0 - ADAM_B1) * g
        v = ADAM_B2 * v_ref[0] + (1.0 - ADAM_B2) * jnp.square(g)
        m_hat = m / (1.0 - ADAM_B1 ** ADAM_STEP)
        v_hat = v / (1.0 - ADAM_B2 ** ADAM_STEP)
        g_ref[0] = g
        d_ref[0] = -ADAM_LR * (m_hat / (jnp.sqrt(v_hat) + ADAM_EPS) + ADAM_WD * w_ref[0])
        mo_ref[0] = m
        vo_ref[0] = v

    spec = pl.BlockSpec((1, tb, cc), lambda l, i: (l, i, 0))
    g0_spec = pl.BlockSpec((tb, cc), lambda l, i: (jnp.where(l == 0, i, nb - 1), 0))
    g1_spec = pl.BlockSpec((tb, cc), lambda l, i: (jnp.where(l == 1, i, 0), 0))
    return pl.pallas_call(
        body, name=name, grid=(2, nb), in_specs=[spec, g0_spec, g1_spec, spec, spec], out_specs=[spec] * 4,
        out_shape=[jax.ShapeDtypeStruct(w.shape, F32)] * 4,
        compiler_params=_cparams(("arbitrary", "arbitrary")),
    )(w, g0, g1, m, v)


def adamw_cols_major(name, w, g0, g1, m, v, tb=LANE):
    wt, mt, vt = (jnp.transpose(a, (2, 0, 1)) for a in (w, m, v))
    cc, _, r = wt.shape

    def body(w_ref, g0_ref, g1_ref, m_ref, v_ref, g_ref, d_ref, mo_ref, vo_ref):
        for l, gl_ref in enumerate((g0_ref, g1_ref)):
            g = gl_ref[...].T
            m = ADAM_B1 * m_ref[:, l, :] + (1.0 - ADAM_B1) * g
            v = ADAM_B2 * v_ref[:, l, :] + (1.0 - ADAM_B2) * jnp.square(g)
            m_hat = m / (1.0 - ADAM_B1 ** ADAM_STEP)
            v_hat = v / (1.0 - ADAM_B2 ** ADAM_STEP)
            g_ref[:, l, :] = g
            d_ref[:, l, :] = -ADAM_LR * (m_hat / (jnp.sqrt(v_hat) + ADAM_EPS) + ADAM_WD * w_ref[:, l, :])
            mo_ref[:, l, :] = m
            vo_ref[:, l, :] = v

    spec = pl.BlockSpec((tb, 2, r), lambda i: (i, 0, 0))
    gspec = pl.BlockSpec((r, tb), lambda i: (0, i))
    outs = pl.pallas_call(
        body, name=name, grid=(pl.cdiv(cc, tb),), in_specs=[spec, gspec, gspec, spec, spec], out_specs=[spec] * 4,
        out_shape=[jax.ShapeDtypeStruct(wt.shape, F32)] * 4,
        compiler_params=_cparams(("parallel",)),
    )(wt, g0, g1, mt, vt)
    return [jnp.transpose(o, (1, 2, 0)) for o in outs]


def _pack(arrs):
    flat = jnp.concatenate([a.reshape(-1).astype(F32) for a in arrs])
    pad = (-flat.shape[0]) % (8 * LANE)
    return jnp.pad(flat, (0, pad)).reshape(-1, LANE)


def _unpack(packed, shapes):
    flat = packed.reshape(-1)
    out, off = [], 0
    for s in shapes:
        size = math.prod(s)
        out.append(flat[off:off + size].reshape(s))
        off += size
    return out


BIG = ('w_in', 'w_out', 'b_pw_w')
SMALL = tuple(k for k in WEIGHTS if k not in BIG)
CHIP_SHARDED_SMALL = {'b_conv_w': 2, 'c_conv_w': 2}


def kernel(x, positions, norm_w, w_in, q_norm_w, k_norm_w, sinks, b_conv_w, b_conv_b, b_ln_w, b_ln_b, b_pw_w, b_pw_b, c_conv_w, c_a_log, c_dt_bias, c_onorm_w, w_out, loss_target, m_norm_w, m_w_in, m_q_norm_w, m_k_norm_w, m_sinks, m_b_conv_w, m_b_conv_b, m_b_ln_w, m_b_ln_b, m_b_pw_w, m_b_pw_b, m_c_conv_w, m_c_a_log, m_c_dt_bias, m_c_onorm_w, m_w_out, v_norm_w, v_w_in, v_q_norm_w, v_k_norm_w, v_sinks, v_b_conv_w, v_b_conv_b, v_b_ln_w, v_b_ln_b, v_b_pw_w, v_b_pw_b, v_c_conv_w, v_c_a_log, v_c_dt_bias, v_c_onorm_w, v_w_out):
    cfg = Cfg(x.shape[-1], x.shape[-2])
    w = dict(norm_w=norm_w, w_in=w_in, q_norm_w=q_norm_w, k_norm_w=k_norm_w, sinks=sinks, b_conv_w=b_conv_w,
             b_conv_b=b_conv_b, b_ln_w=b_ln_w, b_ln_b=b_ln_b, b_pw_w=b_pw_w, b_pw_b=b_pw_b, c_conv_w=c_conv_w,
             c_a_log=c_a_log, c_dt_bias=c_dt_bias, c_onorm_w=c_onorm_w, w_out=w_out)
    m = dict(norm_w=m_norm_w, w_in=m_w_in, q_norm_w=m_q_norm_w, k_norm_w=m_k_norm_w, sinks=m_sinks,
             b_conv_w=m_b_conv_w, b_conv_b=m_b_conv_b, b_ln_w=m_b_ln_w, b_ln_b=m_b_ln_b, b_pw_w=m_b_pw_w,
             b_pw_b=m_b_pw_b, c_conv_w=m_c_conv_w, c_a_log=m_c_a_log, c_dt_bias=m_c_dt_bias, c_onorm_w=m_c_onorm_w,
             w_out=m_w_out)
    v = dict(norm_w=v_norm_w, w_in=v_w_in, q_norm_w=v_q_norm_w, k_norm_w=v_k_norm_w, sinks=v_sinks,
             b_conv_w=v_b_conv_w, b_conv_b=v_b_conv_b, b_ln_w=v_b_ln_w, b_ln_b=v_b_ln_b, b_pw_w=v_b_pw_w,
             b_pw_b=v_b_pw_b, c_conv_w=v_c_conv_w, c_a_log=v_c_a_log, c_dt_bias=v_c_dt_bias, c_onorm_w=v_c_onorm_w,
             w_out=v_w_out)
    chip = 2 * lax.axis_index("x") + lax.axis_index("y")
    c_idx = lax.axis_index("c").astype(jnp.int32).reshape(1)
    D, T = cfg.D, cfg.T
    nseq = x.shape[0]
    n = nseq * T
    halves = lambda a: a.reshape((2, a.shape[0] // 2) + a.shape[1:])
    w_in_b, w_out_b = w_in.astype(BF16), w_out.astype(BF16)
    per_layer = lambda l: [halves(w_in_b[l]), halves(w_out_b[l]), halves(b_pw_w[l])]

    def full_weights(g_in, g_out, g_pw):
        cols = jnp.concatenate(list(g_in.reshape(N_CHIPS, D, -1)), axis=1)
        return permute_w_in(cfg, cols), g_out.reshape(D, D), g_pw.reshape(cfg.BW, cfg.BW)

    def layer_prm(l, pw_full):
        prm = {k: w[k][l] for k in SMALL}
        prm['b_pw_w'] = pw_full
        prm['b_conv_w'] = jnp.concatenate(list(g_bcw[:, l]), axis=1)
        prm['c_conv_w'] = jnp.concatenate(list(g_ccw[:, l]), axis=1)
        return _layer_params(cfg, prm)

    first = per_layer(0) + [b_conv_w, c_conv_w]
    g_in0, g_out0, g_pw0, g_bcw, g_ccw = fill_own(run_comm("gather_weights_0", gather_comm(first)), first)
    wp0, wo0, pw0 = full_weights(g_in0, g_out0, g_pw0)
    cos, sin = rope_for(cfg, positions)
    lp0 = layer_prm(0, pw0)
    second = per_layer(1)
    x1, sv0, gathered = layer_forward(cfg, 0, x.reshape(n, D), lp0, wp0, wo0, cos, sin, comm=gather_comm(second))
    wp1, wo1, pw1 = full_weights(*fill_own(gathered, second))
    lp1 = layer_prm(1, pw1)
    x2, sv1, _ = layer_forward(cfg, 1, x1, lp1, wp1, wo1, cos, sin)
    dx2, loss_local = loss_grad("loss_grad", x2, loss_target.reshape(n, D), 256)
    loss = lax.psum(loss_local, ("x", "y", "c"))

    def partials(gr):
        shard_cols = cfg.IN_COLS // N_CHIPS
        return [unpermute_w_in(cfg, gr['w_in_perm']).reshape(D, N_CHIPS, shard_cols).transpose(1, 0, 2),
                gr['w_out'].reshape(N_CHIPS, D // N_CHIPS, D), gr['b_pw_w'].reshape(N_CHIPS, cfg.BW // N_CHIPS, cfg.BW)]

    dx1, gr1, _ = layer_backward(cfg, 1, dx2, sv1, lp1, wp1, wo1, cos, sin)
    rs1 = GradReduce(1, partials(gr1), chip, c_idx)
    dx0, gr0, got1 = layer_backward(cfg, 0, dx1, sv0, lp0, wp0, wo0, cos, sin, rs=rs1)
    red1 = rs1.finish(got1)
    red0 = GradReduce(0, partials(gr0), chip, c_idx).run()
    grad_x = dx0.reshape(x.shape)
    grads = [gr0, gr1]

    small_parts = [jnp.stack([grads[l][k] for l in range(DEPTH)]) for k in SMALL]
    small_red = _unpack(all_reduce_small("all_reduce_small", _pack(small_parts)), [a.shape for a in small_parts])
    g = {}
    for k, a in zip(SMALL, small_red):
        if k in CHIP_SHARDED_SMALL:
            ax = CHIP_SHARDED_SMALL[k]
            width = a.shape[ax] // N_CHIPS
            a = lax.dynamic_slice_in_dim(a, chip * width, width, axis=ax)
        g[k] = a

    delta, new_m, new_v = {}, {}, {}
    for k, g0, g1 in zip(BIG, red0, red1):
        update = adamw_layers if w[k].shape[-1] % LANE == 0 else adamw_cols_major
        g[k], delta[k], new_m[k], new_v[k] = update(f"adamw_{k}", w[k], g0, g1, m[k], v[k], 128)
    shapes = [w[k].shape for k in SMALL]
    outs = adamw("adamw_small", _pack([w[k] for k in SMALL]), _pack([g[k] for k in SMALL]),
                 _pack([m[k] for k in SMALL]), _pack([v[k] for k in SMALL]), 4096)
    for name_, packed in zip(('delta', 'm', 'v'), outs):
        for k, a in zip(SMALL, _unpack(packed, shapes)):
            {'delta': delta, 'm': new_m, 'v': new_v}[name_][k] = a
    return (loss, grad_x, *[g[k] for k in WEIGHTS], *[delta[k] for k in WEIGHTS], *[new_m[k] for k in WEIGHTS],
            *[new_v[k] for k in WEIGHTS])
```

```python
import functools
import math

import numpy as np
import jax
import jax.numpy as jnp
from jax import lax
from jax.experimental import pallas as pl
from jax.experimental.pallas import tpu as pltpu

F32 = jnp.float32
BF16 = jnp.bfloat16
HI = lax.Precision.HIGHEST
MESH = pl.DeviceIdType.MESH

DEPTH = 2
A_HEAD = 64
A_GROUP = 3
ATTN_BLOCK = 128
ROT_DIM = 16
ROPE_THETA = 500000.0
B_CONV = 31
B_HALO = 32
C_HEAD = 128
C_CONV = 4
C_HALO = 8
CHUNK = 64
EPS = 1e-6
LANE = 128

ADAM_LR = 0.001
ADAM_B1 = 0.9
ADAM_B2 = 0.999
ADAM_EPS = 1e-08
ADAM_WD = 0.01
ADAM_STEP = 10

VMEM_LIMIT = 56 * 1024 * 1024

WEIGHTS = ['norm_w', 'w_in', 'q_norm_w', 'k_norm_w', 'sinks', 'b_conv_w', 'b_conv_b', 'b_ln_w', 'b_ln_b',
           'b_pw_w', 'b_pw_b', 'c_conv_w', 'c_a_log', 'c_dt_bias', 'c_onorm_w', 'w_out']


class Cfg:
    def __init__(self, d_model=2048, seq=2048):
        self.D = d_model
        self.T = seq
        self.AW = 3 * d_model // 8
        self.AQH = self.AW // A_HEAD
        self.AKH = self.AQH // A_GROUP
        self.AKW = self.AKH * A_HEAD
        self.BW = d_model // 4
        self.CH = (d_model - self.AW - self.BW) // C_HEAD
        self.CW = self.CH * C_HEAD
        AW, AKW, BW, CW, CH = self.AW, self.AKW, self.BW, self.CW, self.CH
        orig = [('qa', AW), ('ka', AKW), ('va', AKW), ('za', AW), ('ub', 2 * BW), ('zb', BW),
                ('qc', CW), ('kc', CW), ('vc', CW), ('bc', CH), ('ac', CH), ('zc', CW)]
        self.orig = {}
        off = 0
        for n, w in orig:
            self.orig[n] = (off, w)
            off += w
        self.IN_COLS = off
        order = ['qa', 'za', 'qc', 'kc', 'vc', 'zc', 'ka', 'va', 'ub', 'zb', 'bc', 'ac']
        self.order = order
        self.g = {}
        off = 0
        for n in order:
            w = self.orig[n][1]
            wp = LANE if n in ('bc', 'ac') else w
            assert off % wp == 0, (n, off, wp)
            self.g[n] = (off, wp)
            off += wp
        self.WP = off

    def blk(self, name):
        off, w = self.g[name]
        return off // w


def _cparams(sem, vmem=VMEM_LIMIT):
    return pltpu.CompilerParams(dimension_semantics=sem, vmem_limit_bytes=vmem)


def _silu(x):
    return x * jax.nn.sigmoid(x)


ANY = pl.BlockSpec(memory_space=pl.ANY)


class Comm:
    def __init__(self, ins, out_shapes, sems, start, finish, aliases=None):
        self.ins, self.out_shapes, self.sems = list(ins), list(out_shapes), list(sems)
        self.start, self.finish, self.aliases = start, finish, dict(aliases or {})


def call_with_comm(body, name, grid, in_specs, out_specs, out_shape, scratch_shapes, semantics, args, comm=None):
    in_specs, out_specs, out_shape, scratch_shapes = list(in_specs), list(out_specs), list(out_shape), list(scratch_shapes)
    if comm is None:
        outs = pl.pallas_call(body, name=name, grid=grid, in_specs=in_specs, out_specs=out_specs, out_shape=out_shape,
                              scratch_shapes=scratch_shapes, compiler_params=_cparams(semantics))(*args)
        return list(outs), []
    ni, no, ns = len(in_specs), len(out_specs), len(scratch_shapes)
    nci, nco = len(comm.ins), len(comm.out_shapes)

    def wrapped(*refs):
        h_in, c_in = refs[:ni], refs[ni:ni + nci]
        h_out, c_out = refs[ni + nci:ni + nci + no], refs[ni + nci + no:ni + nci + no + nco]
        h_scr, c_sems = refs[ni + nci + no + nco:ni + nci + no + nco + ns], refs[ni + nci + no + nco + ns:]
        ids = [pl.program_id(d) for d in range(len(grid))]
        first = functools.reduce(jnp.logical_and, [i == 0 for i in ids])
        last = functools.reduce(jnp.logical_and, [i == g - 1 for i, g in zip(ids, grid)])

        @pl.when(first)
        def _():
            comm.start(c_in, c_out, c_sems)

        body(*h_in, *h_out, *h_scr)

        @pl.when(last)
        def _():
            comm.finish(c_in, c_out, c_sems)

    outs = pl.pallas_call(
        wrapped, name=name, grid=grid, in_specs=in_specs + [ANY] * nci, out_specs=out_specs + [ANY] * nco,
        out_shape=out_shape + comm.out_shapes, scratch_shapes=scratch_shapes + comm.sems,
        input_output_aliases={ni + k: no + v for k, v in comm.aliases.items()},
        compiler_params=_cparams(("arbitrary",) * len(grid)),
    )(*args, *comm.ins)
    return list(outs[:no]), list(outs[no:])


def run_comm(name, comm):
    nci, nco = len(comm.ins), len(comm.out_shapes)

    def body(*refs):
        c_in, c_out, c_sems = refs[:nci], refs[nci:nci + nco], refs[nci + nco:]
        comm.start(c_in, c_out, c_sems)
        comm.finish(c_in, c_out, c_sems)

    return pl.pallas_call(
        body, name=name, in_specs=[ANY] * nci, out_specs=[ANY] * nco, out_shape=comm.out_shapes,
        scratch_shapes=comm.sems, input_output_aliases=comm.aliases,
    )(*comm.ins)


def _bdot(a, b, ca, cb, precision=HI):
    dims = (((ca,), (cb,)), ((0,), (0,)))
    if precision is HI and a.dtype == F32:
        ah = a.astype(BF16)
        bh = b.astype(BF16)
        al = (a - ah.astype(F32)).astype(BF16)
        bl = (b - bh.astype(F32)).astype(BF16)
        dg = lambda p, q: lax.dot_general(p, q, dims, preferred_element_type=F32)
        return dg(ah, bh) + (dg(ah, bl) + dg(al, bh))
    return lax.dot_general(a, b, dims, precision=precision, preferred_element_type=F32)


def _rope_matrix(nb):
    i = lax.broadcasted_iota(jnp.int32, (nb, A_HEAD, A_HEAD), 1)
    j = lax.broadcasted_iota(jnp.int32, (nb, A_HEAD, A_HEAD), 2)
    half = ROT_DIM // 2
    neg = (j < half) & (i == j + half)
    pos = (j >= half) & (j < ROT_DIM) & (i == j - half)
    return jnp.where(neg, -1.0, jnp.where(pos, 1.0, 0.0)).astype(F32)


def _norm_rope(xh, w, cos, sin):
    y = xh * lax.rsqrt(jnp.mean(xh * xh, axis=-1, keepdims=True) + EPS) * w
    return y * cos + _bdot(y, _rope_matrix(xh.shape[0]), 2, 1) * sin


def attn_block(cfg, first, q, za, kc, vc, cosc, sinc, kp, vp, cosp, sinp, qnw, knw, sinks_row):
    blk = ATTN_BLOCK
    nq, nk = cfg.AQH, cfg.AKH
    qi = lax.broadcasted_iota(jnp.int32, (blk, 2 * blk), 0)
    kj = lax.broadcasted_iota(jnp.int32, (blk, 2 * blk), 1)
    dist = qi + blk - kj
    valid = ((dist >= 0) & (dist < blk) & (jnp.logical_not(first) | (kj >= blk)))[None]
    cos2 = jnp.concatenate([cosp, cosc], axis=0)
    sin2 = jnp.concatenate([sinp, sinc], axis=0)
    head = lambda x, h: x[:, A_HEAD * h:A_HEAD * (h + 1)]
    k2 = jnp.stack([jnp.concatenate([head(kp, h), head(kc, h)], axis=0) for h in range(nk)], axis=0)
    v2 = jnp.stack([jnp.concatenate([head(vp, h), head(vc, h)], axis=0) for h in range(nk)], axis=0)
    k2 = _norm_rope(k2, knw[None], cos2[None], sin2[None]).astype(BF16)
    v2 = v2.astype(BF16)
    k2 = jnp.stack([k2[h // A_GROUP] for h in range(nq)], axis=0)
    v2 = jnp.stack([v2[h // A_GROUP] for h in range(nq)], axis=0)
    qh = jnp.stack([head(q, h) for h in range(nq)], axis=0)
    qh = _norm_rope(qh, qnw[None], cosc[None], sinc[None]).astype(BF16)
    s = _bdot(qh, k2, 2, 2, None) * (A_HEAD ** -0.5)
    s = jnp.where(valid, s, -1e30)
    sink = jnp.stack([sinks_row[:, A_HEAD * h:A_HEAD * h + 1] for h in range(nq)], axis=0)
    m = jnp.maximum(jnp.max(s, axis=-1, keepdims=True), sink)
    e = jnp.exp(s - m)
    den = jnp.sum(e, axis=-1, keepdims=True) + jnp.exp(sink - m)
    o = _bdot((e / den).astype(BF16), v2, 2, 1, None)
    return (jnp.concatenate([o[h] for h in range(nq)], axis=1) * _silu(za),)


def conv_block(cfg, first, u, zb, uh, cw, cb, lw, lb, pw, pb):
    BW = cfg.BW
    tb = u.shape[0]
    uu = jnp.concatenate([uh, u], axis=0)
    h = uu[:, :BW] * jax.nn.sigmoid(uu[:, BW:])
    row = lax.broadcasted_iota(jnp.int32, h.shape, 0)
    h = jnp.where(first & (row < B_HALO), 0.0, h)
    acc = jnp.zeros((tb, BW), F32) + cb
    base = B_HALO - (B_CONV - 1)
    for k in range(B_CONV):
        acc = acc + cw[k:k + 1, :] * h[base + k:base + k + tb, :]
    mu = jnp.mean(acc, axis=-1, keepdims=True)
    var = jnp.mean(jnp.square(acc - mu), axis=-1, keepdims=True)
    y = (acc - mu) * lax.rsqrt(var + EPS) * lw + lb
    s = _silu(y)
    o = jnp.dot(s.astype(BF16), pw.astype(BF16), preferred_element_type=F32) + pb
    return (o * _silu(zb),)


def gdn_prep_block(cfg, first, xq, xk, xv, braw, araw, hq, hk, hv, cw, alog, dtb):
    CW = cfg.CW
    tb = xq.shape[0]
    outs = []
    for idx, (x, xh) in enumerate(((xq, hq), (xk, hk), (xv, hv))):
        xx = jnp.concatenate([jnp.where(first, 0.0, xh), x], axis=0)
        w = cw[:, idx * CW:(idx + 1) * CW]
        acc = jnp.zeros((tb, CW), F32)
        base = C_HALO - (C_CONV - 1)
        for k in range(C_CONV):
            acc = acc + w[k:k + 1, :] * xx[base + k:base + k + tb, :]
        y = _silu(acc)
        if idx < 2:
            parts = []
            for h in range(cfg.CH):
                yh = y[:, C_HEAD * h:C_HEAD * (h + 1)]
                parts.append(yh * lax.rsqrt(jnp.sum(yh * yh, axis=-1, keepdims=True) + EPS))
            y = jnp.concatenate(parts, axis=1)
        outs.append(y)
    beta = jax.nn.sigmoid(braw)
    g = -jnp.exp(alog) * jax.nn.softplus(araw + dtb)
    return outs[0], outs[1], outs[2], g, beta


def _inverse_unit_lower(low, eye):
    pw = low
    inv = eye - low
    for _ in range(5):
        pwb = pw.astype(BF16)
        pw = _bdot(pwb, pwb, 2, 1, None)
        inv = inv + _bdot(inv.astype(BF16), pw.astype(BF16), 2, 1, None)
    ax = inv + _bdot(low, inv, 2, 1)
    return inv + _bdot(inv, eye - ax, 2, 1)


@jax.custom_vjp
def _saved_inverse(low, inv):
    return inv


def _saved_inverse_fwd(low, inv):
    return inv, inv


def _saved_inverse_bwd(inv, d):
    dlow = -_bdot(_bdot(inv, d, 1, 1), inv, 2, 2)
    return dlow, jnp.zeros_like(inv)


_saved_inverse.defvjp(_saved_inverse_fwd, _saved_inverse_bwd)


def gdn_intra_rows(cfg, first, qn, kn, v, g, beta, inv_saved=None):
    c = CHUNK
    CH = cfg.CH
    nchunk = qn.shape[0] // c
    i = lax.broadcasted_iota(jnp.int32, (c, c), 0)
    j = lax.broadcasted_iota(jnp.int32, (c, c), 1)
    incl = (i >= j)[None]
    strict = (i > j)[None]
    eye = (i == j).astype(F32)[None]
    tri = (i >= j).astype(F32)
    rows = [slice(c * ci, c * (ci + 1)) for ci in range(nchunk)]
    gcs = [jnp.dot(tri, g[r], precision=HI, preferred_element_type=F32) for r in rows]
    pairs = [(ci, h) for ci in range(nchunk) for h in range(CH)]
    heads = lambda x, wd: jnp.stack([x[rows[ci], wd * h:wd * (h + 1)] for ci, h in pairs], axis=0)
    gch = jnp.stack([gcs[ci][:, h:h + 1] for ci, h in pairs], axis=0)
    bh = jnp.stack([beta[rows[ci], h:h + 1] for ci, h in pairs], axis=0)
    q = heads(qn, C_HEAD) * (C_HEAD ** -0.5)
    k = heads(kn, C_HEAD)
    vv = heads(v, C_HEAD)
    a = jnp.broadcast_to(gch, (len(pairs), c, c))
    diff = jnp.where(incl, a - jnp.swapaxes(a, 1, 2), 0.0)
    decay = jnp.where(incl, jnp.exp(diff), 0.0)
    kb = k * bh
    low = jnp.where(strict, _bdot(kb, k, 2, 2) * decay, 0.0)
    if inv_saved is None:
        inv = _inverse_unit_lower(low, eye)
    else:
        inv = _saved_inverse(low, heads(inv_saved, c))
    eg = jnp.exp(gch)
    sol = _bdot(inv, jnp.concatenate([vv * bh, kb * eg], axis=2), 2, 1)
    intra = jnp.where(incl, _bdot(q, k, 2, 2) * decay, 0.0)
    qg = q * eg
    kd = k * jnp.exp(gch[:, c - 1:c, :] - gch)
    glast = jnp.concatenate([jnp.broadcast_to(gc[c - 1:c, :], gc.shape) for gc in gcs], axis=0)

    def unstack(x):
        return jnp.concatenate([jnp.concatenate([x[ci * CH + h] for h in range(CH)], axis=1) for ci in range(nchunk)],
                               axis=0)

    outs = (unstack(sol[:, :, :C_HEAD]), unstack(sol[:, :, C_HEAD:]), unstack(qg), unstack(kd), unstack(intra), glast)
    return outs + (unstack(inv),) if inv_saved is None else outs


def gdn_state_step(S, u, w, qg, kd, intra, glast):
    v_new = u - _bdot(w, S, 2, 1)
    o = _bdot(qg, S, 2, 1) + _bdot(intra, v_new, 2, 1)
    S_next = S * jnp.exp(glast) + _bdot(kd, v_new, 1, 1)
    return o, S_next


def gdn_out_block(cfg, first, o, zc, onw):
    parts = []
    for h in range(cfg.CH):
        sl = slice(C_HEAD * h, C_HEAD * (h + 1))
        oh = o[:, sl]
        y = oh * lax.rsqrt(jnp.mean(oh * oh, axis=-1, keepdims=True) + EPS) * onw
        parts.append(y * _silu(zc[:, sl]))
    return (jnp.concatenate(parts, axis=1),)


def rms_block(x, nw):
    return x * lax.rsqrt(jnp.mean(x * x, axis=-1, keepdims=True) + EPS) * nw


class Row:
    def __init__(self, arr, width, colblk=0, grad=None):
        self.arr, self.width, self.colblk, self.grad = arr, width, colblk, grad


class Halo:
    def __init__(self, arr, width, colblk, hr, tie=None):
        self.arr, self.width, self.colblk, self.hr, self.tie = arr, width, colblk, hr, tie


def _row_specs(tb, rows, halos, params, pos):
    specs = [pl.BlockSpec((tb, r.width), lambda i, cb=r.colblk: (pos(i), cb)) for r in rows]
    specs += [pl.BlockSpec((h.hr, h.width),
                           lambda i, cb=h.colblk, m=tb // h.hr: (jnp.maximum(pos(i) * m - 1, 0), cb))
              for h in halos]
    specs += [pl.BlockSpec(p.shape, lambda i: (0, 0)) for p in params]
    return specs


def rb_fwd(name, fn, n, tb, bps, rows, halos, params, outs, comm=None):
    nr, nh, npar = len(rows), len(halos), len(params)

    def body(*refs):
        ins = refs[:nr + nh + npar]
        o_refs = refs[nr + nh + npar:]
        first = (pl.program_id(0) % bps) == 0
        res = fn(first, *[r[...] for r in ins])
        for ref, val in zip(o_refs, res):
            ref[...] = val.astype(ref.dtype)

    res, carried = call_with_comm(
        body, name, (n // tb,), _row_specs(tb, rows, halos, params, lambda i: i),
        [pl.BlockSpec((tb, w), lambda i: (i, 0)) for w, _ in outs],
        [jax.ShapeDtypeStruct((n, w), dt) for w, dt in outs], [], ("parallel",),
        [r.arr for r in rows] + [h.arr for h in halos] + list(params), comm)
    return (res, carried) if comm is not None else res


def rb_bwd(name, fn, n, tb, bps, rows, halos, params, douts, param_grads, comm=None):
    nr, nh, npar, nd = len(rows), len(halos), len(params), len(douts)
    nblk = n // tb
    grow = [k for k, r in enumerate(rows) if r.grad is not None]
    ghalo = [k for k, h in enumerate(halos) if h.tie is not None]
    gpar = [k for k, f in enumerate(param_grads) if f]
    pos = lambda i: nblk - 1 - i

    def body(*refs):
        ins = refs[:nr + nh + npar]
        d_refs = refs[nr + nh + npar:nr + nh + npar + nd]
        rest = refs[nr + nh + npar + nd:]
        grow_refs = rest[:len(grow)]
        gpar_refs = rest[len(grow):len(grow) + len(gpar)]
        carry_refs = rest[len(grow) + len(gpar):]
        i = pl.program_id(0)
        first = (pos(i) % bps) == 0
        vals = [r[...] for r in ins]
        diff_idx = grow + [nr + k for k in ghalo] + [nr + nh + k for k in gpar]

        def f(*dargs):
            full = list(vals)
            for k, a in zip(diff_idx, dargs):
                full[k] = a
            return fn(first, *full)

        res, vjp = jax.vjp(f, *[vals[k] for k in diff_idx])
        grads = vjp(tuple(d[...].astype(r.dtype) for d, r in zip(d_refs, res)))
        g_rows = list(grads[:len(grow)])
        g_halos = grads[len(grow):len(grow) + len(ghalo)]
        g_pars = grads[len(grow) + len(ghalo):]

        @pl.when(i == 0)
        def _():
            for c in carry_refs:
                c[...] = jnp.zeros_like(c)
            for p in gpar_refs:
                p[...] = jnp.zeros_like(p)

        for k, ref in enumerate(grow_refs):
            ref[...] = g_rows[k].astype(ref.dtype)
        for ci, hk in enumerate(ghalo):
            h = halos[hk]
            k = grow.index(h.tie)
            tail = g_rows[k][tb - h.hr:, :] + carry_refs[ci][...]
            grow_refs[k][tb - h.hr:, :] = tail.astype(grow_refs[k].dtype)
            carry_refs[ci][...] = g_halos[ci]
        for ref, gp in zip(gpar_refs, g_pars):
            ref[...] += gp

    out_specs = [pl.BlockSpec((tb, rows[k].width), lambda i: (pos(i), 0)) for k in grow]
    out_specs += [pl.BlockSpec(params[k].shape, lambda i: (0, 0)) for k in gpar]
    out_shape = [jax.ShapeDtypeStruct((n, rows[k].width), rows[k].grad) for k in grow]
    out_shape += [jax.ShapeDtypeStruct(params[k].shape, F32) for k in gpar]
    in_specs = _row_specs(tb, rows, halos, params, pos)
    in_specs += [pl.BlockSpec((tb, d.shape[1]), lambda i: (pos(i), 0)) for d in douts]
    res, carried = call_with_comm(
        body, name, (nblk,), in_specs, out_specs, out_shape,
        [pltpu.VMEM((halos[k].hr, halos[k].width), F32) for k in ghalo], ("arbitrary",),
        [r.arr for r in rows] + [h.arr for h in halos] + list(params) + list(douts), comm)
    return (res, carried) if comm is not None else res


_DIMS = {'nn': (((1,), (0,)), ((), ())), 'nt': (((1,), (1,)), ((), ())), 'tn': (((0,), (0,)), ((), ()))}


def matmul(name, a, b, mode, tm, tn, tk, out_dtype=F32, add=None, comm=None):
    if mode == 'tn':
        K, M = a.shape
    else:
        M, K = a.shape
    N = b.shape[0] if mode == 'nt' else b.shape[1]
    tm, tn, tk = min(tm, M), min(tn, N), min(tk, K)
    assert M % tm == 0 and N % tn == 0 and K % tk == 0, (name, M, N, K, tm, tn, tk)
    nk = K // tk
    a_spec = pl.BlockSpec((tk, tm), lambda i, j, k: (k, i)) if mode == 'tn' else pl.BlockSpec((tm, tk), lambda i, j, k: (i, k))
    b_spec = pl.BlockSpec((tn, tk), lambda i, j, k: (j, k)) if mode == 'nt' else pl.BlockSpec((tk, tn), lambda i, j, k: (k, j))
    o_spec = pl.BlockSpec((tm, tn), lambda i, j, k: (i, j))
    has_add = add is not None

    def body(*refs):
        a_ref, b_ref = refs[0], refs[1]
        add_ref = refs[2] if has_add else None
        o_ref = refs[-1]
        k = pl.program_id(2)
        part = lax.dot_general(a_ref[...].astype(BF16), b_ref[...].astype(BF16), _DIMS[mode], preferred_element_type=F32)

        @pl.when(k == 0)
        def _():
            o_ref[...] = ((part + add_ref[...]) if has_add else part).astype(o_ref.dtype)

        if nk > 1:
            @pl.when(k > 0)
            def _():
                o_ref[...] += part

    assert nk == 1 or out_dtype == F32
    ins = [a, b] + ([add] if has_add else [])
    in_specs = [a_spec, b_spec] + ([o_spec] if has_add else [])
    outs, couts = call_with_comm(body, name, (M // tm, N // tn, nk), in_specs, [o_spec],
                                 [jax.ShapeDtypeStruct((M, N), out_dtype)], [], ("parallel", "parallel", "arbitrary"),
                                 ins, comm)
    return (outs[0], couts) if comm is not None else outs[0]


def norm_in_proj(name, x, nw, wp, tm, tn, comm=None):
    n, d = x.shape
    wpc = wp.shape[1]
    tm, tn = min(tm, n), min(tn, wpc)
    assert n % tm == 0 and wpc % tn == 0

    def body(x_ref, nw_ref, w_ref, p_ref, h_ref):
        @pl.when(pl.program_id(1) == 0)
        def _():
            h_ref[...] = rms_block(x_ref[...], nw_ref[...]).astype(BF16)

        p_ref[...] = jnp.dot(h_ref[...], w_ref[...], preferred_element_type=F32)

    outs, couts = call_with_comm(
        body, name, (n // tm, wpc // tn),
        [pl.BlockSpec((tm, d), lambda i, j: (i, 0)), pl.BlockSpec((1, d), lambda i, j: (0, 0)),
         pl.BlockSpec((d, tn), lambda i, j: (0, j))],
        [pl.BlockSpec((tm, tn), lambda i, j: (i, j)), pl.BlockSpec((tm, d), lambda i, j: (i, 0))],
        [jax.ShapeDtypeStruct((n, wpc), F32), jax.ShapeDtypeStruct((n, d), BF16)], [], ("parallel", "arbitrary"),
        [x, nw, wp], comm)
    return (outs[0], outs[1], couts) if comm is not None else (outs[0], outs[1])


def norm_bwd(name, x, nw, dh, dres, tb):
    n, d = x.shape
    tb = min(tb, n)

    def body(x_ref, nw_ref, dh_ref, dres_ref, dx_ref, dnw_ref):
        @pl.when(pl.program_id(0) == 0)
        def _():
            dnw_ref[...] = jnp.zeros_like(dnw_ref)

        _, vjp = jax.vjp(rms_block, x_ref[...], nw_ref[...])
        dx, dnw = vjp(dh_ref[...])
        dx_ref[...] = dx + dres_ref[...]
        dnw_ref[...] += dnw

    row = pl.BlockSpec((tb, d), lambda i: (i, 0))
    par = pl.BlockSpec((1, d), lambda i: (0, 0))
    return pl.pallas_call(
        body, name=name, grid=(n // tb,), in_specs=[row, par, row, row], out_specs=[row, par],
        out_shape=[jax.ShapeDtypeStruct((n, d), F32), jax.ShapeDtypeStruct((1, d), F32)],
        compiler_params=_cparams(("arbitrary",)),
    )(x, nw, dh, dres)


def loss_grad(name, y, target, tb):
    n, d = y.shape
    tb = min(tb, n)

    def body(y_ref, t_ref, dy_ref, loss_ref):
        @pl.when(pl.program_id(0) == 0)
        def _():
            loss_ref[...] = jnp.zeros_like(loss_ref)

        err = y_ref[...] - t_ref[...]
        dy_ref[...] = err * (1.0 / d)
        part = 0.5 * jnp.sum(jnp.mean(err * err, axis=-1, keepdims=True), axis=0, keepdims=True)
        loss_ref[...] += jnp.broadcast_to(part, loss_ref.shape)

    row = pl.BlockSpec((tb, d), lambda i: (i, 0))
    dy, loss = pl.pallas_call(
        body, name=name, grid=(n // tb,), in_specs=[row, row],
        out_specs=[row, pl.BlockSpec((8, LANE), lambda i: (0, 0))],
        out_shape=[jax.ShapeDtypeStruct((n, d), F32), jax.ShapeDtypeStruct((8, LANE), F32)],
        compiler_params=_cparams(("arbitrary",)),
    )(y, target)
    return dy, loss[0, 0]


def rope_tables(name, pos_col, inv_freq_row):
    n = pos_col.shape[0]

    def body(p_ref, f_ref, c_ref, s_ref):
        ang = p_ref[...].astype(F32) * f_ref[...]
        lane = lax.broadcasted_iota(jnp.int32, ang.shape, 1)
        c_ref[...] = jnp.where(lane < ROT_DIM, jnp.cos(ang), 1.0)
        s_ref[...] = jnp.where(lane < ROT_DIM, jnp.sin(ang), 0.0)

    return pl.pallas_call(
        body, name=name, out_shape=[jax.ShapeDtypeStruct((n, A_HEAD), F32)] * 2,
    )(pos_col, inv_freq_row)


def _scan_operands(cfg, nseq, u_ref, w_ref, qg_ref, kd_ref, a_ref, gl_ref):
    pairs = [(b, h) for b in range(nseq) for h in range(cfg.CH)]
    st = lambda r, wd: jnp.stack([r[b, :, wd * h:wd * (h + 1)] for b, h in pairs], axis=0)
    gl = jnp.stack([gl_ref[b, 0:1, h:h + 1] for b, h in pairs], axis=0)
    return st(u_ref, C_HEAD), st(w_ref, C_HEAD), st(qg_ref, C_HEAD), st(kd_ref, C_HEAD), st(a_ref, CHUNK), gl


def gdn_scan_fwd(name, cfg, nseq, u, w, qg, kd, intra, glast):
    CH, CW, T = cfg.CH, cfg.CW, cfg.T
    nc = T // CHUNK

    def body(u_ref, w_ref, qg_ref, kd_ref, a_ref, gl_ref, o_ref, sin_ref, s_ref):
        @pl.when(pl.program_id(0) == 0)
        def _():
            s_ref[...] = jnp.zeros_like(s_ref)

        S = s_ref[...]
        for b in range(nseq):
            sin_ref[b, 0] = S[b * CH:(b + 1) * CH]
        o, S_next = gdn_state_step(S, *_scan_operands(cfg, nseq, u_ref, w_ref, qg_ref, kd_ref, a_ref, gl_ref))
        s_ref[...] = S_next
        for b in range(nseq):
            o_ref[b] = jnp.concatenate([o[b * CH + h] for h in range(CH)], axis=1)

    row = lambda wd: pl.BlockSpec((nseq, CHUNK, wd), lambda c: (0, c, 0))
    widths = [CW, CW, CW, CW, CH * CHUNK, LANE]
    o, s_in = pl.pallas_call(
        body, name=name, grid=(nc,),
        in_specs=[row(x) for x in widths],
        out_specs=[row(CW), pl.BlockSpec((nseq, 1, CH, C_HEAD, C_HEAD), lambda c: (0, c, 0, 0, 0))],
        out_shape=[jax.ShapeDtypeStruct((nseq, T, CW), F32),
                   jax.ShapeDtypeStruct((nseq, nc, CH, C_HEAD, C_HEAD), F32)],
        scratch_shapes=[pltpu.VMEM((nseq * CH, C_HEAD, C_HEAD), F32)],
        compiler_params=_cparams(("arbitrary",)),
    )(*[a.reshape(nseq, T, a.shape[1]) for a in (u, w, qg, kd, intra, glast)])
    return o.reshape(nseq * T, CW), s_in


def gdn_scan_bwd(name, cfg, nseq, u, w, qg, kd, intra, glast, s_in, do, comm=None):
    CH, CW, T = cfg.CH, cfg.CW, cfg.T
    nc = T // CHUNK

    def body(u_ref, w_ref, qg_ref, kd_ref, a_ref, gl_ref, sin_ref, do_ref,
             du_ref, dw_ref, dqg_ref, dkd_ref, da_ref, dgl_ref, ds_ref):
        @pl.when(pl.program_id(0) == 0)
        def _():
            ds_ref[...] = jnp.zeros_like(ds_ref)

        S = jnp.concatenate([sin_ref[b, 0] for b in range(nseq)], axis=0)
        dout = jnp.stack([do_ref[b, :, C_HEAD * h:C_HEAD * (h + 1)] for b in range(nseq) for h in range(CH)], axis=0)
        _, vjp = jax.vjp(gdn_state_step, S, *_scan_operands(cfg, nseq, u_ref, w_ref, qg_ref, kd_ref, a_ref, gl_ref))
        dS, du, dw, dqg, dkd, da, dg = vjp((dout, ds_ref[...]))
        ds_ref[...] = dS
        lane = lax.broadcasted_iota(jnp.int32, (CHUNK, LANE), 1)
        rowi = lax.broadcasted_iota(jnp.int32, (CHUNK, LANE), 0)
        for b in range(nseq):
            cat = lambda x: jnp.concatenate([x[b * CH + h] for h in range(CH)], axis=1)
            du_ref[b] = cat(du)
            dw_ref[b] = cat(dw)
            dqg_ref[b] = cat(dqg)
            dkd_ref[b] = cat(dkd)
            da_ref[b] = cat(da)
            dgl = jnp.zeros((CHUNK, LANE), F32)
            for h in range(CH):
                dgl = dgl + jnp.where((lane == h) & (rowi == 0), dg[b * CH + h], 0.0)
            dgl_ref[b] = dgl

    row = lambda wd: pl.BlockSpec((nseq, CHUNK, wd), lambda c: (0, nc - 1 - c, 0))
    widths = [CW, CW, CW, CW, CH * CHUNK, LANE]
    outs, carried = call_with_comm(
        body, name, (nc,),
        [row(x) for x in widths]
        + [pl.BlockSpec((nseq, 1, CH, C_HEAD, C_HEAD), lambda c: (0, nc - 1 - c, 0, 0, 0)), row(CW)],
        [row(x) for x in widths], [jax.ShapeDtypeStruct((nseq, T, x), F32) for x in widths],
        [pltpu.VMEM((nseq * CH, C_HEAD, C_HEAD), F32)], ("arbitrary",),
        [a.reshape(nseq, T, a.shape[1]) for a in (u, w, qg, kd, intra, glast)] + [s_in, do.reshape(nseq, T, CW)], comm)
    return [o.reshape(nseq * T, o.shape[2]) for o in outs], carried


def _tile(total, cap, unit=LANE):
    best = None
    for t in range(unit, min(cap, total) + 1, unit):
        if total % t == 0:
            best = t
    assert best is not None, (total, cap, unit)
    return best


def _pad_lanes(v, width=LANE):
    return jnp.pad(v.reshape(1, -1), ((0, 0), (0, width - v.shape[-1])))


def permute_w_in(cfg, w):
    parts = []
    for n in cfg.order:
        off, wd = cfg.orig[n]
        blk = w[:, off:off + wd]
        if cfg.g[n][1] != wd:
            blk = jnp.pad(blk, ((0, 0), (0, cfg.g[n][1] - wd)))
        parts.append(blk)
    return jnp.concatenate(parts, axis=1)


def unpermute_w_in(cfg, wp):
    parts = []
    for n, (off, wd) in cfg.orig.items():
        parts.append(wp[:, cfg.g[n][0]:cfg.g[n][0] + wd])
    return jnp.concatenate(parts, axis=1)


def _layer_params(cfg, prm):
    return dict(
        nw=prm['norm_w'].reshape(1, -1),
        qnw=prm['q_norm_w'].reshape(1, -1), knw=prm['k_norm_w'].reshape(1, -1),
        sinks_row=jnp.repeat(prm['sinks'], A_HEAD).reshape(1, -1),
        cw=prm['b_conv_w'], cb=prm['b_conv_b'].reshape(1, -1),
        lw=prm['b_ln_w'].reshape(1, -1), lb=prm['b_ln_b'].reshape(1, -1),
        pw=prm['b_pw_w'], pb=prm['b_pw_b'].reshape(1, -1),
        ccw=prm['c_conv_w'], alog=_pad_lanes(prm['c_a_log']), dtb=_pad_lanes(prm['c_dt_bias']),
        onw=prm['c_onorm_w'].reshape(1, -1),
    )


def _attn_io(cfg, p, cos, sin, grads):
    gq = BF16 if grads else None
    rows = [Row(p, cfg.AW, cfg.blk('qa'), gq), Row(p, cfg.AW, cfg.blk('za'), gq),
            Row(p, cfg.AKW, cfg.blk('ka'), gq), Row(p, cfg.AKW, cfg.blk('va'), gq),
            Row(cos, A_HEAD), Row(sin, A_HEAD)]
    halos = [Halo(p, cfg.AKW, cfg.blk('ka'), ATTN_BLOCK, 2 if grads else None),
             Halo(p, cfg.AKW, cfg.blk('va'), ATTN_BLOCK, 3 if grads else None),
             Halo(cos, A_HEAD, 0, ATTN_BLOCK), Halo(sin, A_HEAD, 0, ATTN_BLOCK)]
    return rows, halos


def _conv_io(cfg, p, grads):
    gq = BF16 if grads else None
    rows = [Row(p, 2 * cfg.BW, cfg.blk('ub'), gq), Row(p, cfg.BW, cfg.blk('zb'), gq)]
    halos = [Halo(p, 2 * cfg.BW, cfg.blk('ub'), B_HALO, 0 if grads else None)]
    return rows, halos


def _prep_io(cfg, p, grads):
    gq = BF16 if grads else None
    rows = [Row(p, cfg.CW, cfg.blk(n), gq) for n in ('qc', 'kc', 'vc')]
    rows += [Row(p, LANE, cfg.blk('bc'), gq), Row(p, LANE, cfg.blk('ac'), gq)]
    halos = [Halo(p, cfg.CW, cfg.blk(n), C_HALO, k if grads else None) for k, n in enumerate(('qc', 'kc', 'vc'))]
    return rows, halos


TB_CONV = 128
TB_PREP = 256
TB_OUT = 256
TB_INTRA_FWD = 128
TB_INTRA_BWD = 128


def layer_forward(cfg, l, x, lp, wp, wo, cos, sin, comms=(None, None)):
    n = x.shape[0]
    nseq = n // cfg.T
    T = cfg.T
    p, h, *carried = norm_in_proj(f"in_proj_{l}", x, lp['nw'], wp, 1024, _tile(cfg.WP, 768), comm=comms[0])
    rows, halos = _attn_io(cfg, p, cos, sin, False)
    oa = rb_fwd(f"attn_fwd_{l}", functools.partial(attn_block, cfg), n, ATTN_BLOCK, T // ATTN_BLOCK, rows, halos,
                [lp['qnw'], lp['knw'], lp['sinks_row']], [(cfg.AW, BF16)], comm=comms[1])
    if comms[1] is not None:
        oa, more = oa
        carried = [carried[0] + more] if carried else [more]
    (oa,) = oa
    rows, halos = _conv_io(cfg, p, False)
    tbb = min(TB_CONV, T)
    (ob,) = rb_fwd(f"conv_fwd_{l}", functools.partial(conv_block, cfg), n, tbb, T // tbb, rows, halos,
                   [lp['cw'], lp['cb'], lp['lw'], lp['lb'], lp['pw'], lp['pb']], [(cfg.BW, BF16)])
    rows, halos = _prep_io(cfg, p, False)
    tbp = min(TB_PREP, T)
    qn, kn, v, g, beta = rb_fwd(f"gdn_prep_fwd_{l}", functools.partial(gdn_prep_block, cfg), n, tbp, T // tbp, rows,
                                halos, [lp['ccw'], lp['alog'], lp['dtb']],
                                [(cfg.CW, F32)] * 3 + [(LANE, F32)] * 2)
    intra_outs = rb_fwd(f"gdn_intra_fwd_{l}", functools.partial(gdn_intra_rows, cfg), n, TB_INTRA_FWD, T // TB_INTRA_FWD,
                        [Row(qn, cfg.CW), Row(kn, cfg.CW), Row(v, cfg.CW), Row(g, LANE), Row(beta, LANE)], [], [],
                        [(cfg.CW, F32)] * 4 + [(cfg.CH * CHUNK, F32), (LANE, F32), (cfg.CH * CHUNK, F32)])
    intra_outs, inv = intra_outs[:6], intra_outs[6]
    o, s_in = gdn_scan_fwd(f"gdn_scan_fwd_{l}", cfg, nseq, *intra_outs)
    tbo = min(TB_OUT, T)
    (oc,) = rb_fwd(f"gdn_out_fwd_{l}", functools.partial(gdn_out_block, cfg), n, tbo, T // tbo,
                   [Row(o, cfg.CW), Row(p, cfg.CW, cfg.blk('zc'))], [], [lp['onw']], [(cfg.CW, BF16)])
    y = jnp.concatenate([oa, ob, oc], axis=1)
    x_next = matmul(f"out_proj_{l}", y, wo, 'nn', 1024, 1024, cfg.D, add=x)
    saved = dict(x=x, p=p, h=h, y=y, qn=qn, kn=kn, v=v, g=g, beta=beta, intra_outs=intra_outs, inv=inv, s_in=s_in, o=o)
    return x_next, saved, (carried[0] if carried else None)


def layer_backward(cfg, l, dxn, sv, lp, wp, wo, cos, sin, rs=None):
    n = dxn.shape[0]
    nseq = n // cfg.T
    T = cfg.T
    p = sv['p']
    AW, BW, CW = cfg.AW, cfg.BW, cfg.CW
    if rs is None:
        dy = matmul(f"dy_{l}", dxn, wo, 'nt', 1024, 1024, cfg.D)
    else:
        dy, received = matmul(f"dy_{l}", dxn, wo, 'nt', 1024, 1024, cfg.D, comm=rs.swap())
    dwo = matmul(f"dwo_{l}", sv['y'], dxn, 'tn', 1024, 1024, 2048)
    doa, dob, doc = dy[:, :AW], dy[:, AW:AW + BW], dy[:, AW + BW:]
    tbo = min(TB_OUT, T)
    do, dzc, donw = rb_bwd(f"gdn_out_bwd_{l}", functools.partial(gdn_out_block, cfg), n, tbo, T // tbo,
                           [Row(sv['o'], CW, 0, F32), Row(p, CW, cfg.blk('zc'), BF16)], [], [lp['onw']], [doc], [True])
    if rs is not None:
        rs.add(received)
    dintra, got_rest = gdn_scan_bwd(f"gdn_scan_bwd_{l}", cfg, nseq, *sv['intra_outs'], sv['s_in'], do,
                                    comm=None if rs is None else rs.scatter([1, 2]))
    dqn, dkn, dv, dg, dbeta = rb_bwd(
        f"gdn_intra_bwd_{l}", functools.partial(gdn_intra_rows, cfg), n, TB_INTRA_BWD, T // TB_INTRA_BWD,
        [Row(sv['qn'], CW, 0, F32), Row(sv['kn'], CW, 0, F32), Row(sv['v'], CW, 0, F32), Row(sv['g'], LANE, 0, F32),
         Row(sv['beta'], LANE, 0, F32), Row(sv['inv'], cfg.CH * CHUNK)], [], [], list(dintra), [])
    rows, halos = _prep_io(cfg, p, True)
    tbp = min(TB_PREP, T)
    dqc, dkc, dvc, dbc, dac, dccw, dalog, ddtb = rb_bwd(
        f"gdn_prep_bwd_{l}", functools.partial(gdn_prep_block, cfg), n, tbp, T // tbp, rows, halos,
        [lp['ccw'], lp['alog'], lp['dtb']], [dqn, dkn, dv, dg, dbeta], [True] * 3)
    rows, halos = _conv_io(cfg, p, True)
    tbb = min(TB_CONV, T)
    conv_grads = rb_bwd(
        f"conv_bwd_{l}", functools.partial(conv_block, cfg), n, tbb, T // tbb, rows, halos,
        [lp['cw'], lp['cb'], lp['lw'], lp['lb'], lp['pw'], lp['pb']], [dob], [True] * 6,
        comm=None if rs is None else rs.scatter([0]))
    got = None
    if rs is not None:
        conv_grads, got_w_in = conv_grads
        got = got_w_in + got_rest
    dub, dzb, dcw, dcb, dlw, dlb, dpw, dpb = conv_grads
    rows, halos = _attn_io(cfg, p, cos, sin, True)
    dqa, dza, dka, dva, dqnw, dknw, dsinks_row = rb_bwd(
        f"attn_bwd_{l}", functools.partial(attn_block, cfg), n, ATTN_BLOCK, T // ATTN_BLOCK, rows, halos,
        [lp['qnw'], lp['knw'], lp['sinks_row']], [doa], [True] * 3)
    dgroups = dict(qa=dqa, za=dza, qc=dqc, kc=dkc, vc=dvc, zc=dzc, ka=dka, va=dva, ub=dub, zb=dzb, bc=dbc, ac=dac)
    dp = jnp.concatenate([dgroups[k] for k in cfg.order], axis=1)
    dh = matmul(f"dh_{l}", dp, wp, 'nt', 1024, 1024, _tile(cfg.WP, 2304))
    dwp = matmul(f"dwp_{l}", sv['h'], dp, 'tn', 1024, _tile(cfg.WP, 1152), 2048)
    dx, dnw = norm_bwd(f"norm_bwd_{l}", sv['x'], lp['nw'], dh, dxn, 256)
    grads = dict(
        norm_w=dnw[0], w_in_perm=dwp, q_norm_w=dqnw[0], k_norm_w=dknw[0],
        sinks=dsinks_row.reshape(cfg.AQH, A_HEAD)[:, 0],
        b_conv_w=dcw, b_conv_b=dcb[0], b_ln_w=dlw[0], b_ln_b=dlb[0], b_pw_w=dpw, b_pw_b=dpb[0],
        c_conv_w=dccw, c_a_log=dalog[0, :cfg.CH], c_dt_bias=ddtb[0, :cfg.CH], c_onorm_w=donw[0], w_out=dwo)
    return dx, grads, got


def rope_for(cfg, positions):
    n = positions.size
    inv_freq = ROPE_THETA ** (-np.arange(0, ROT_DIM, 2, dtype=np.float32) / ROT_DIM)
    freq_row = np.zeros((1, A_HEAD), np.float32)
    freq_row[0, :ROT_DIM] = np.concatenate([inv_freq, inv_freq])
    return rope_tables("rope_tables", positions.reshape(n, 1), jnp.asarray(freq_row))


def local_step(cfg, x, positions, prm, wps, wos, target):
    nseq = x.shape[0]
    n = nseq * cfg.T
    cos, sin = rope_for(cfg, positions)
    lps = [_layer_params(cfg, {k: v[l] for k, v in prm.items()}) for l in range(DEPTH)]
    saved = []
    xl = x.reshape(n, cfg.D)
    for l in range(DEPTH):
        xl, sv, _ = layer_forward(cfg, l, xl, lps[l], wps[l], wos[l], cos, sin)
        saved.append(sv)
    dx, loss = loss_grad("loss_grad", xl, target.reshape(n, cfg.D), 256)
    grads = [None] * DEPTH
    for l in reversed(range(DEPTH)):
        dx, grads[l], _ = layer_backward(cfg, l, dx, saved[l], lps[l], wps[l], wos[l], cos, sin)
    return loss, dx.reshape(x.shape), grads


N_CHIPS = 4
N_DEV = 8


def _place():
    return lax.axis_index("x"), lax.axis_index("y"), lax.axis_index("c")


def _other_chips(x, y):
    return [(1 - x, y), (x, 1 - y), (1 - x, 1 - y)]


def _remote(src, dst, send, recv, to):
    return pltpu.make_async_remote_copy(src_ref=src, dst_ref=dst, send_sem=send, recv_sem=recv, device_id=to,
                                        device_id_type=MESH)


def gather_comm(arrs):
    n = len(arrs)

    def half(c):
        return [pl.ds(c * (a.shape[0] // 2), a.shape[0] // 2) for a in arrs]

    def first_copies(ins, outs, send, recv):
        x, y, c = _place()
        me = 2 * x + y
        mine = half(c)
        return [_remote(ins[i].at[mine[i]], outs[i].at[me, mine[i]], send.at[i, j], recv.at[i, j], (cx, cy, c))
                for i in range(n) for j, (cx, cy) in enumerate(_other_chips(x, y))]

    def start(ins, outs, sems):
        for cp in first_copies(ins, outs, *sems):
            cp.start()

    def finish(ins, outs, sems):
        send, recv = sems
        x, y, c = _place()
        chips = _other_chips(x, y)
        sib = (x, y, 1 - c)
        passed = []
        mine, other = half(c), half(1 - c)
        for i in range(n):
            for j, (cx, cy) in enumerate(chips):
                blk = outs[i].at[2 * cx + cy, mine[i]]
                _remote(blk, blk, send.at[i, j], recv.at[i, j], (x, y, c)).wait_recv()
                cp = _remote(blk, blk, send.at[i, 3 + j], recv.at[i, 3 + j], sib)
                cp.start()
                passed.append(cp)
        for i in range(n):
            for j, (cx, cy) in enumerate(chips):
                blk = outs[i].at[2 * cx + cy, other[i]]
                _remote(blk, blk, send.at[i, 3 + j], recv.at[i, 3 + j], sib).wait_recv()
        for cp in first_copies(ins, outs, send, recv) + passed:
            cp.wait_send()

    return Comm(arrs, [jax.ShapeDtypeStruct((N_CHIPS,) + a.shape, a.dtype) for a in arrs],
                [pltpu.SemaphoreType.DMA((n, 6)), pltpu.SemaphoreType.DMA((n, 6))], start, finish)


def fill_own(gathered, arrs):
    me = 2 * lax.axis_index("x") + lax.axis_index("y")
    return [lax.dynamic_update_index_in_dim(o, a, me, 0) for o, a in zip(gathered, arrs)]


def swap_comm(arrs):
    n = len(arrs)

    def copies(ins, outs, send, recv):
        x, y, c = _place()
        return [_remote(ins[i].at[:, 1 - c], outs[i], send.at[i], recv.at[i], (x, y, 1 - c)) for i in range(n)]

    def start(ins, outs, sems):
        for cp in copies(ins, outs, *sems):
            cp.start()

    def finish(ins, outs, sems):
        for cp in copies(ins, outs, *sems):
            cp.wait()

    return Comm(arrs, [jax.ShapeDtypeStruct((a.shape[0],) + a.shape[2:], a.dtype) for a in arrs],
                [pltpu.SemaphoreType.DMA((n,)), pltpu.SemaphoreType.DMA((n,))], start, finish)


def scatter_comm(arrs):
    n = len(arrs)

    def copies(ins, outs, send, recv):
        x, y, c = _place()
        return [_remote(ins[i].at[2 * cx + cy], outs[i].at[j], send.at[i, j], recv.at[i, j], (cx, cy, c))
                for i in range(n) for j, (cx, cy) in enumerate(_other_chips(x, y))]

    def start(ins, outs, sems):
        for cp in copies(ins, outs, *sems):
            cp.start()

    def finish(ins, outs, sems):
        send, recv = sems
        x, y, c = _place()
        for i in range(n):
            for j in range(3):
                blk = outs[i].at[j]
                _remote(blk, blk, send.at[i, j], recv.at[i, j], (x, y, c)).wait_recv()
        for cp in copies(ins, outs, send, recv):
            cp.wait_send()

    return Comm(arrs, [jax.ShapeDtypeStruct((3,) + a.shape[1:], a.dtype) for a in arrs],
                [pltpu.SemaphoreType.DMA((n, 3)), pltpu.SemaphoreType.DMA((n, 3))], start, finish)


def share_comm(arrs):
    n = len(arrs)

    def copies(outs, send, recv):
        x, y, c = _place()
        return [_remote(outs[i].at[c], outs[i].at[c], send.at[i], recv.at[i], (x, y, 1 - c)) for i in range(n)]

    def start(ins, outs, sems):
        for cp in copies(outs, *sems):
            cp.start()

    def finish(ins, outs, sems):
        send, recv = sems
        x, y, c = _place()
        for i in range(n):
            blk = outs[i].at[1 - c]
            _remote(blk, blk, send.at[i], recv.at[i], (x, y, c)).wait_recv()
        for cp in copies(outs, send, recv):
            cp.wait_send()

    return Comm(arrs, [jax.ShapeDtypeStruct(a.shape, a.dtype) for a in arrs],
                [pltpu.SemaphoreType.DMA((n,)), pltpu.SemaphoreType.DMA((n,))], start, finish,
                aliases={i: i for i in range(n)})


def all_reduce_small(name, packed):
    r = packed.shape[0]

    def body(in_ref, out_ref, buf, send, recv):
        x, y, c = _place()
        me = 4 * x + 2 * y + c
        buf[me] = in_ref[...]
        flips = [(fx, fy, fc) for fx in (0, 1) for fy in (0, 1) for fc in (0, 1) if (fx, fy, fc) != (0, 0, 0)]
        peers = [((x + fx) % 2, (y + fy) % 2, (c + fc) % 2) for fx, fy, fc in flips]
        cps = [_remote(in_ref, buf.at[me], send.at[k], recv.at[k], peer) for k, peer in enumerate(peers)]
        for cp in cps:
            cp.start()
        for k, (px, py, pc) in enumerate(peers):
            blk = buf.at[4 * px + 2 * py + pc]
            _remote(blk, blk, send.at[k], recv.at[k], (x, y, c)).wait_recv()
        for cp in cps:
            cp.wait_send()
        acc = buf[0]
        for d in range(1, N_DEV):
            acc = acc + buf[d]
        out_ref[...] = acc

    vm = pl.BlockSpec(memory_space=pltpu.VMEM)
    return pl.pallas_call(
        body, name=name, in_specs=[vm], out_specs=vm, out_shape=jax.ShapeDtypeStruct(packed.shape, F32),
        scratch_shapes=[pltpu.VMEM((N_DEV, r, LANE), F32), pltpu.SemaphoreType.DMA((N_DEV - 1,)),
                        pltpu.SemaphoreType.DMA((N_DEV - 1,))],
    )(packed)


def add_own_half(name, g, a, c_idx, tr):
    nch, _, r, cc = g.shape
    tr = min(tr, r)

    def body(c_ref, g_ref, a_ref, o_ref):
        o_ref[...] = (g_ref[0] + a_ref[...]).astype(o_ref.dtype)

    return pl.pallas_call(
        body, name=name,
        grid_spec=pltpu.PrefetchScalarGridSpec(
            num_scalar_prefetch=1, grid=(nch, r // tr),
            in_specs=[pl.BlockSpec((1, 1, tr, cc), lambda j, i, c_ref: (j, c_ref[0], i, 0)),
                      pl.BlockSpec((1, tr, cc), lambda j, i, c_ref: (j, i, 0))],
            out_specs=pl.BlockSpec((1, tr, cc), lambda j, i, c_ref: (j, i, 0))),
        out_shape=jax.ShapeDtypeStruct(a.shape, BF16),
        compiler_params=_cparams(("parallel", "parallel")),
    )(c_idx, g, a)


def sum_chips(name, p, b, idx, tr):
    _, r, cc = p.shape
    tr = min(tr, r)

    def body(idx_ref, p_ref, b_ref, o_ref):
        acc = p_ref[0].astype(F32)
        for k in range(3):
            acc = acc + b_ref[k].astype(F32)
        o_ref[0] = acc

    return pl.pallas_call(
        body, name=name,
        grid_spec=pltpu.PrefetchScalarGridSpec(
            num_scalar_prefetch=1, grid=(r // tr,),
            in_specs=[pl.BlockSpec((1, tr, cc), lambda i, s: (s[0], i, 0)),
                      pl.BlockSpec((3, tr, cc), lambda i, s: (0, i, 0))],
            out_specs=pl.BlockSpec((1, tr, cc), lambda i, s: (s[1], i, 0))),
        out_shape=jax.ShapeDtypeStruct((2, r, cc), F32),
        compiler_params=_cparams(("parallel",)),
    )(idx, p, b)


class GradReduce:
    def __init__(self, tag, parts, chip, c_idx):
        self.tag, self.c_idx = tag, c_idx
        self.parts = [p.reshape(p.shape[0], 2, p.shape[1] // 2, p.shape[2]) for p in parts]
        self.idx = jnp.concatenate([chip.astype(jnp.int32).reshape(1), c_idx])

    def swap(self):
        return swap_comm(self.parts)

    def add(self, received):
        self.part = [add_own_half(f"rs{self.tag}_add_sibling_{t}", g, a, self.c_idx, 128)
                     for t, (g, a) in enumerate(zip(self.parts, received))]

    def scatter(self, which=None):
        return scatter_comm(self.part if which is None else [self.part[t] for t in which])

    def finish(self, got):
        red = [sum_chips(f"rs{self.tag}_sum_chips_{t}", p, b, self.idx, 128) for t, (p, b) in enumerate(zip(self.part, got))]
        out = run_comm(f"rs{self.tag}_share_halves", share_comm(red))
        return [o.reshape(-1, o.shape[-1]) for o in out]

    def run(self):
        self.add(run_comm(f"rs{self.tag}_swap_halves", self.swap()))
        return self.finish(run_comm(f"rs{self.tag}_scatter_chips", self.scatter()))


def adamw_many(name, ws, gs, ms, vs):
    n = len(ws)

    def body(*refs):
        for i in range(n):
            w_ref, g_ref, m_ref, v_ref = (refs[k * n + i] for k in range(4))
            d_ref, mo_ref, vo_ref = (refs[(4 + k) * n + i] for k in range(3))
            g = g_ref[...]
            m = ADAM_B1 * m_ref[...] + (1.0 - ADAM_B1) * g
            v = ADAM_B2 * v_ref[...] + (1.0 - ADAM_B2) * jnp.square(g)
            m_hat = m / (1.0 - ADAM_B1 ** ADAM_STEP)
            v_hat = v / (1.0 - ADAM_B2 ** ADAM_STEP)
            d_ref[...] = -ADAM_LR * (m_hat / (jnp.sqrt(v_hat) + ADAM_EPS) + ADAM_WD * w_ref[...])
            mo_ref[...] = m
            vo_ref[...] = v

    vm = pl.BlockSpec(memory_space=pltpu.VMEM)
    return pl.pallas_call(
        body, name=name, in_specs=[vm] * (4 * n), out_specs=[vm] * (3 * n),
        out_shape=[jax.ShapeDtypeStruct(a.shape, F32) for a in ws] * 3,
    )(*ws, *gs, *ms, *vs)


def adamw_layers(name, w, g0, g1, m, v, tb):
    _, r, cc = w.shape
    tb = min(tb, r)
    nb = r // tb

    def body(w_ref, g0_ref, g1_ref, m_ref, v_ref, g_ref, d_ref, mo_ref, vo_ref):
        g = jnp.where(pl.program_id(0) == 0, g0_ref[...], g1_ref[...])
        m = ADAM_B1 * m_ref[0] + (1.0 - ADAM_B1) * g
        v = ADAM_B2 * v_ref[0] + (1.0 - ADAM_B2) * jnp.square(g)
        m_hat = m / (1.0 - ADAM_B1 ** ADAM_STEP)
        v_hat = v / (1.0 - ADAM_B2 ** ADAM_STEP)
        g_ref[0] = g
        d_ref[0] = -ADAM_LR * (m_hat / (jnp.sqrt(v_hat) + ADAM_EPS) + ADAM_WD * w_ref[0])
        mo_ref[0] = m
        vo_ref[0] = v

    spec = pl.BlockSpec((1, tb, cc), lambda l, i: (l, i, 0))
    g0_spec = pl.BlockSpec((tb, cc), lambda l, i: (jnp.where(l == 0, i, nb - 1), 0))
    g1_spec = pl.BlockSpec((tb, cc), lambda l, i: (jnp.where(l == 1, i, 0), 0))
    return pl.pallas_call(
        body, name=name, grid=(2, nb), in_specs=[spec, g0_spec, g1_spec, spec, spec], out_specs=[spec] * 4,
        out_shape=[jax.ShapeDtypeStruct(w.shape, F32)] * 4,
        compiler_params=_cparams(("arbitrary", "arbitrary")),
    )(w, g0, g1, m, v)


def adamw_cols_major(name, w, g0, g1, m, v, tb=LANE):
    wt, mt, vt = (jnp.transpose(a, (2, 0, 1)) for a in (w, m, v))
    cc, _, r = wt.shape

    def body(w_ref, g0_ref, g1_ref, m_ref, v_ref, g_ref, d_ref, mo_ref, vo_ref):
        for l, gl_ref in enumerate((g0_ref, g1_ref)):
            g = gl_ref[...].T
            m = ADAM_B1 * m_ref[:, l, :] + (1.0 - ADAM_B1) * g
            v = ADAM_B2 * v_ref[:, l, :] + (1.0 - ADAM_B2) * jnp.square(g)
            m_hat = m / (1.0 - ADAM_B1 ** ADAM_STEP)
            v_hat = v / (1.0 - ADAM_B2 ** ADAM_STEP)
            g_ref[:, l, :] = g
            d_ref[:, l, :] = -ADAM_LR * (m_hat / (jnp.sqrt(v_hat) + ADAM_EPS) + ADAM_WD * w_ref[:, l, :])
            mo_ref[:, l, :] = m
            vo_ref[:, l, :] = v

    spec = pl.BlockSpec((tb, 2, r), lambda i: (i, 0, 0))
    gspec = pl.BlockSpec((r, tb), lambda i: (0, i))
    outs = pl.pallas_call(
        body, name=name, grid=(pl.cdiv(cc, tb),), in_specs=[spec, gspec, gspec, spec, spec], out_specs=[spec] * 4,
        out_shape=[jax.ShapeDtypeStruct(wt.shape, F32)] * 4,
        compiler_params=_cparams(("parallel",)),
    )(wt, g0, g1, mt, vt)
    return [jnp.transpose(o, (1, 2, 0)) for o in outs]


def _pack(arrs):
    flat = jnp.concatenate([a.reshape(-1).astype(F32) for a in arrs])
    pad = (-flat.shape[0]) % (8 * LANE)
    return jnp.pad(flat, (0, pad)).reshape(-1, LANE)


def _unpack(packed, shapes):
    flat = packed.reshape(-1)
    out, off = [], 0
    for s in shapes:
        size = math.prod(s)
        out.append(flat[off:off + size].reshape(s))
        off += size
    return out


BIG = ('w_in', 'w_out', 'b_pw_w')
SMALL = tuple(k for k in WEIGHTS if k not in BIG)
CHIP_SHARDED_SMALL = {'b_conv_w': 2, 'c_conv_w': 2}


def kernel(x, positions, norm_w, w_in, q_norm_w, k_norm_w, sinks, b_conv_w, b_conv_b, b_ln_w, b_ln_b, b_pw_w, b_pw_b, c_conv_w, c_a_log, c_dt_bias, c_onorm_w, w_out, loss_target, m_norm_w, m_w_in, m_q_norm_w, m_k_norm_w, m_sinks, m_b_conv_w, m_b_conv_b, m_b_ln_w, m_b_ln_b, m_b_pw_w, m_b_pw_b, m_c_conv_w, m_c_a_log, m_c_dt_bias, m_c_onorm_w, m_w_out, v_norm_w, v_w_in, v_q_norm_w, v_k_norm_w, v_sinks, v_b_conv_w, v_b_conv_b, v_b_ln_w, v_b_ln_b, v_b_pw_w, v_b_pw_b, v_c_conv_w, v_c_a_log, v_c_dt_bias, v_c_onorm_w, v_w_out):
    cfg = Cfg(x.shape[-1], x.shape[-2])
    w = dict(norm_w=norm_w, w_in=w_in, q_norm_w=q_norm_w, k_norm_w=k_norm_w, sinks=sinks, b_conv_w=b_conv_w,
             b_conv_b=b_conv_b, b_ln_w=b_ln_w, b_ln_b=b_ln_b, b_pw_w=b_pw_w, b_pw_b=b_pw_b, c_conv_w=c_conv_w,
             c_a_log=c_a_log, c_dt_bias=c_dt_bias, c_onorm_w=c_onorm_w, w_out=w_out)
    m = dict(norm_w=m_norm_w, w_in=m_w_in, q_norm_w=m_q_norm_w, k_norm_w=m_k_norm_w, sinks=m_sinks,
             b_conv_w=m_b_conv_w, b_conv_b=m_b_conv_b, b_ln_w=m_b_ln_w, b_ln_b=m_b_ln_b, b_pw_w=m_b_pw_w,
             b_pw_b=m_b_pw_b, c_conv_w=m_c_conv_w, c_a_log=m_c_a_log, c_dt_bias=m_c_dt_bias, c_onorm_w=m_c_onorm_w,
             w_out=m_w_out)
    v = dict(norm_w=v_norm_w, w_in=v_w_in, q_norm_w=v_q_norm_w, k_norm_w=v_k_norm_w, sinks=v_sinks,
             b_conv_w=v_b_conv_w, b_conv_b=v_b_conv_b, b_ln_w=v_b_ln_w, b_ln_b=v_b_ln_b, b_pw_w=v_b_pw_w,
             b_pw_b=v_b_pw_b, c_conv_w=v_c_conv_w, c_a_log=v_c_a_log, c_dt_bias=v_c_dt_bias, c_onorm_w=v_c_onorm_w,
             w_out=v_w_out)
    chip = 2 * lax.axis_index("x") + lax.axis_index("y")
    c_idx = lax.axis_index("c").astype(jnp.int32).reshape(1)
    D, T = cfg.D, cfg.T
    nseq = x.shape[0]
    n = nseq * T
    w_in_b, w_out_b = w_in.astype(BF16), w_out.astype(BF16)
    per_layer = lambda l: [w_in_b[l], w_out_b[l], b_pw_w[l]]

    def full_weights(g_in, g_out, g_pw):
        return permute_w_in(cfg, jnp.concatenate(list(g_in), axis=1)), g_out.reshape(D, D), g_pw.reshape(cfg.BW, cfg.BW)

    def layer_prm(l, pw_full):
        prm = {k: w[k][l] for k in SMALL}
        prm['b_pw_w'] = pw_full
        prm['b_conv_w'] = jnp.concatenate(list(g_bcw[:, l]), axis=1)
        prm['c_conv_w'] = jnp.concatenate(list(g_ccw[:, l]), axis=1)
        return _layer_params(cfg, prm)

    first = per_layer(0) + [b_conv_w, c_conv_w]
    g_in0, g_out0, g_pw0, g_bcw, g_ccw = fill_own(run_comm("gather_weights_0", gather_comm(first)), first)
    wp0, wo0, pw0 = full_weights(g_in0, g_out0, g_pw0)
    cos, sin = rope_for(cfg, positions)
    lp0 = layer_prm(0, pw0)
    second = per_layer(1)
    x1, sv0, gathered = layer_forward(cfg, 0, x.reshape(n, D), lp0, wp0, wo0, cos, sin,
                                      comms=(gather_comm(second[:1]), gather_comm(second[1:])))
    wp1, wo1, pw1 = full_weights(*fill_own(gathered, second))
    lp1 = layer_prm(1, pw1)
    x2, sv1, _ = layer_forward(cfg, 1, x1, lp1, wp1, wo1, cos, sin)
    dx2, loss_local = loss_grad("loss_grad", x2, loss_target.reshape(n, D), 256)
    loss = lax.psum(loss_local, ("x", "y", "c"))

    def partials(gr):
        shard_cols = cfg.IN_COLS // N_CHIPS
        return [unpermute_w_in(cfg, gr['w_in_perm']).reshape(D, N_CHIPS, shard_cols).transpose(1, 0, 2),
                gr['w_out'].reshape(N_CHIPS, D // N_CHIPS, D), gr['b_pw_w'].reshape(N_CHIPS, cfg.BW // N_CHIPS, cfg.BW)]

    dx1, gr1, _ = layer_backward(cfg, 1, dx2, sv1, lp1, wp1, wo1, cos, sin)
    rs1 = GradReduce(1, partials(gr1), chip, c_idx)
    dx0, gr0, got1 = layer_backward(cfg, 0, dx1, sv0, lp0, wp0, wo0, cos, sin, rs=rs1)
    red1 = rs1.finish(got1)
    red0 = GradReduce(0, partials(gr0), chip, c_idx).run()
    grad_x = dx0.reshape(x.shape)
    grads = [gr0, gr1]

    small_parts = [jnp.stack([grads[l][k] for l in range(DEPTH)]) for k in SMALL]
    small_red = _unpack(all_reduce_small("all_reduce_small", _pack(small_parts)), [a.shape for a in small_parts])
    g = {}
    for k, a in zip(SMALL, small_red):
        if k in CHIP_SHARDED_SMALL:
            ax = CHIP_SHARDED_SMALL[k]
            width = a.shape[ax] // N_CHIPS
            a = lax.dynamic_slice_in_dim(a, chip * width, width, axis=ax)
        g[k] = a

    delta, new_m, new_v = {}, {}, {}
    for k, g0, g1 in zip(BIG, red0, red1):
        update = adamw_layers if w[k].shape[-1] % LANE == 0 else adamw_cols_major
        g[k], delta[k], new_m[k], new_v[k] = update(f"adamw_{k}", w[k], g0, g1, m[k], v[k], 128)
    outs = adamw_many("adamw_small", *[[d[k] for k in SMALL] for d in (w, g, m, v)])
    for i, k in enumerate(SMALL):
        delta[k], new_m[k], new_v[k] = outs[i], outs[len(SMALL) + i], outs[2 * len(SMALL) + i]
    return (loss, grad_x, *[g[k] for k in WEIGHTS], *[delta[k] for k in WEIGHTS], *[new_m[k] for k in WEIGHTS],
            *[new_v[k] for k in WEIGHTS])
```

```python
import functools
import math

import numpy as np
import jax
import jax.numpy as jnp
from jax import lax
from jax.experimental import pallas as pl
from jax.experimental.pallas import tpu as pltpu

F32 = jnp.float32
BF16 = jnp.bfloat16
HI = lax.Precision.HIGHEST
MESH = pl.DeviceIdType.MESH

DEPTH = 2
A_HEAD = 64
A_GROUP = 3
ATTN_BLOCK = 128
ROT_DIM = 16
ROPE_THETA = 500000.0
B_CONV = 31
B_HALO = 32
C_HEAD = 128
C_CONV = 4
C_HALO = 8
CHUNK = 64
EPS = 1e-6
LANE = 128

ADAM_LR = 0.001
ADAM_B1 = 0.9
ADAM_B2 = 0.999
ADAM_EPS = 1e-08
ADAM_WD = 0.01
ADAM_STEP = 10

VMEM_LIMIT = 56 * 1024 * 1024

WEIGHTS = ['norm_w', 'w_in', 'q_norm_w', 'k_norm_w', 'sinks', 'b_conv_w', 'b_conv_b', 'b_ln_w', 'b_ln_b',
           'b_pw_w', 'b_pw_b', 'c_conv_w', 'c_a_log', 'c_dt_bias', 'c_onorm_w', 'w_out']


class Cfg:
    def __init__(self, d_model=2048, seq=2048):
        self.D = d_model
        self.T = seq
        self.AW = 3 * d_model // 8
        self.AQH = self.AW // A_HEAD
        self.AKH = self.AQH // A_GROUP
        self.AKW = self.AKH * A_HEAD
        self.BW = d_model // 4
        self.CH = (d_model - self.AW - self.BW) // C_HEAD
        self.CW = self.CH * C_HEAD
        AW, AKW, BW, CW, CH = self.AW, self.AKW, self.BW, self.CW, self.CH
        orig = [('qa', AW), ('ka', AKW), ('va', AKW), ('za', AW), ('ub', 2 * BW), ('zb', BW),
                ('qc', CW), ('kc', CW), ('vc', CW), ('bc', CH), ('ac', CH), ('zc', CW)]
        self.orig = {}
        off = 0
        for n, w in orig:
            self.orig[n] = (off, w)
            off += w
        self.IN_COLS = off
        order = ['qa', 'za', 'qc', 'kc', 'vc', 'zc', 'ka', 'va', 'ub', 'zb', 'bc', 'ac']
        self.order = order
        self.g = {}
        off = 0
        for n in order:
            w = self.orig[n][1]
            wp = LANE if n in ('bc', 'ac') else w
            assert off % wp == 0, (n, off, wp)
            self.g[n] = (off, wp)
            off += wp
        self.WP = off

    def blk(self, name):
        off, w = self.g[name]
        return off // w


def _cparams(sem, vmem=VMEM_LIMIT):
    return pltpu.CompilerParams(dimension_semantics=sem, vmem_limit_bytes=vmem)


def _silu(x):
    return x * jax.nn.sigmoid(x)


ANY = pl.BlockSpec(memory_space=pl.ANY)


class Comm:
    def __init__(self, ins, out_shapes, sems, start, finish, aliases=None):
        self.ins, self.out_shapes, self.sems = list(ins), list(out_shapes), list(sems)
        self.start, self.finish, self.aliases = start, finish, dict(aliases or {})


def call_with_comm(body, name, grid, in_specs, out_specs, out_shape, scratch_shapes, semantics, args, comm=None):
    in_specs, out_specs, out_shape, scratch_shapes = list(in_specs), list(out_specs), list(out_shape), list(scratch_shapes)
    if comm is None:
        outs = pl.pallas_call(body, name=name, grid=grid, in_specs=in_specs, out_specs=out_specs, out_shape=out_shape,
                              scratch_shapes=scratch_shapes, compiler_params=_cparams(semantics))(*args)
        return list(outs), []
    ni, no, ns = len(in_specs), len(out_specs), len(scratch_shapes)
    nci, nco = len(comm.ins), len(comm.out_shapes)

    def wrapped(*refs):
        h_in, c_in = refs[:ni], refs[ni:ni + nci]
        h_out, c_out = refs[ni + nci:ni + nci + no], refs[ni + nci + no:ni + nci + no + nco]
        h_scr, c_sems = refs[ni + nci + no + nco:ni + nci + no + nco + ns], refs[ni + nci + no + nco + ns:]
        ids = [pl.program_id(d) for d in range(len(grid))]
        first = functools.reduce(jnp.logical_and, [i == 0 for i in ids])
        last = functools.reduce(jnp.logical_and, [i == g - 1 for i, g in zip(ids, grid)])

        @pl.when(first)
        def _():
            comm.start(c_in, c_out, c_sems)

        body(*h_in, *h_out, *h_scr)

        @pl.when(last)
        def _():
            comm.finish(c_in, c_out, c_sems)

    outs = pl.pallas_call(
        wrapped, name=name, grid=grid, in_specs=in_specs + [ANY] * nci, out_specs=out_specs + [ANY] * nco,
        out_shape=out_shape + comm.out_shapes, scratch_shapes=scratch_shapes + comm.sems,
        input_output_aliases={ni + k: no + v for k, v in comm.aliases.items()},
        compiler_params=_cparams(("arbitrary",) * len(grid)),
    )(*args, *comm.ins)
    return list(outs[:no]), list(outs[no:])


def run_comm(name, comm):
    nci, nco = len(comm.ins), len(comm.out_shapes)

    def body(*refs):
        c_in, c_out, c_sems = refs[:nci], refs[nci:nci + nco], refs[nci + nco:]
        comm.start(c_in, c_out, c_sems)
        comm.finish(c_in, c_out, c_sems)

    return pl.pallas_call(
        body, name=name, in_specs=[ANY] * nci, out_specs=[ANY] * nco, out_shape=comm.out_shapes,
        scratch_shapes=comm.sems, input_output_aliases=comm.aliases,
    )(*comm.ins)


def _bdot(a, b, ca, cb, precision=HI):
    dims = (((ca,), (cb,)), ((0,), (0,)))
    if precision is HI and a.dtype == F32:
        ah = a.astype(BF16)
        bh = b.astype(BF16)
        al = (a - ah.astype(F32)).astype(BF16)
        bl = (b - bh.astype(F32)).astype(BF16)
        dg = lambda p, q: lax.dot_general(p, q, dims, preferred_element_type=F32)
        return dg(ah, bh) + (dg(ah, bl) + dg(al, bh))
    return lax.dot_general(a, b, dims, precision=precision, preferred_element_type=F32)


def _rope_matrix(nb):
    i = lax.broadcasted_iota(jnp.int32, (nb, A_HEAD, A_HEAD), 1)
    j = lax.broadcasted_iota(jnp.int32, (nb, A_HEAD, A_HEAD), 2)
    half = ROT_DIM // 2
    neg = (j < half) & (i == j + half)
    pos = (j >= half) & (j < ROT_DIM) & (i == j - half)
    return jnp.where(neg, -1.0, jnp.where(pos, 1.0, 0.0)).astype(F32)


def _norm_rope(xh, w, cos, sin):
    y = xh * lax.rsqrt(jnp.mean(xh * xh, axis=-1, keepdims=True) + EPS) * w
    return y * cos + _bdot(y, _rope_matrix(xh.shape[0]), 2, 1) * sin


def attn_block(cfg, first, q, za, kc, vc, cosc, sinc, kp, vp, cosp, sinp, qnw, knw, sinks_row):
    blk = ATTN_BLOCK
    nq, nk = cfg.AQH, cfg.AKH
    qi = lax.broadcasted_iota(jnp.int32, (blk, 2 * blk), 0)
    kj = lax.broadcasted_iota(jnp.int32, (blk, 2 * blk), 1)
    dist = qi + blk - kj
    valid = ((dist >= 0) & (dist < blk) & (jnp.logical_not(first) | (kj >= blk)))[None]
    cos2 = jnp.concatenate([cosp, cosc], axis=0)
    sin2 = jnp.concatenate([sinp, sinc], axis=0)
    head = lambda x, h: x[:, A_HEAD * h:A_HEAD * (h + 1)]
    k2 = jnp.stack([jnp.concatenate([head(kp, h), head(kc, h)], axis=0) for h in range(nk)], axis=0)
    v2 = jnp.stack([jnp.concatenate([head(vp, h), head(vc, h)], axis=0) for h in range(nk)], axis=0)
    k2 = _norm_rope(k2, knw[None], cos2[None], sin2[None]).astype(BF16)
    v2 = v2.astype(BF16)
    k2 = jnp.stack([k2[h // A_GROUP] for h in range(nq)], axis=0)
    v2 = jnp.stack([v2[h // A_GROUP] for h in range(nq)], axis=0)
    qh = jnp.stack([head(q, h) for h in range(nq)], axis=0)
    qh = _norm_rope(qh, qnw[None], cosc[None], sinc[None]).astype(BF16)
    s = _bdot(qh, k2, 2, 2, None) * (A_HEAD ** -0.5)
    s = jnp.where(valid, s, -1e30)
    sink = jnp.stack([sinks_row[:, A_HEAD * h:A_HEAD * h + 1] for h in range(nq)], axis=0)
    m = jnp.maximum(jnp.max(s, axis=-1, keepdims=True), sink)
    e = jnp.exp(s - m)
    den = jnp.sum(e, axis=-1, keepdims=True) + jnp.exp(sink - m)
    o = _bdot((e / den).astype(BF16), v2, 2, 1, None)
    return (jnp.concatenate([o[h] for h in range(nq)], axis=1) * _silu(za),)


def conv_block(cfg, first, u, zb, uh, cw, cb, lw, lb, pw, pb):
    BW = cfg.BW
    tb = u.shape[0]
    uu = jnp.concatenate([uh, u], axis=0)
    h = uu[:, :BW] * jax.nn.sigmoid(uu[:, BW:])
    row = lax.broadcasted_iota(jnp.int32, h.shape, 0)
    h = jnp.where(first & (row < B_HALO), 0.0, h)
    acc = jnp.zeros((tb, BW), F32) + cb
    base = B_HALO - (B_CONV - 1)
    for k in range(B_CONV):
        acc = acc + cw[k:k + 1, :] * h[base + k:base + k + tb, :]
    mu = jnp.mean(acc, axis=-1, keepdims=True)
    var = jnp.mean(jnp.square(acc - mu), axis=-1, keepdims=True)
    y = (acc - mu) * lax.rsqrt(var + EPS) * lw + lb
    s = _silu(y)
    o = jnp.dot(s.astype(BF16), pw.astype(BF16), preferred_element_type=F32) + pb
    return (o * _silu(zb),)


def gdn_prep_block(cfg, first, xq, xk, xv, braw, araw, hq, hk, hv, cw, alog, dtb):
    CW = cfg.CW
    tb = xq.shape[0]
    outs = []
    for idx, (x, xh) in enumerate(((xq, hq), (xk, hk), (xv, hv))):
        xx = jnp.concatenate([jnp.where(first, 0.0, xh), x], axis=0)
        w = cw[:, idx * CW:(idx + 1) * CW]
        acc = jnp.zeros((tb, CW), F32)
        base = C_HALO - (C_CONV - 1)
        for k in range(C_CONV):
            acc = acc + w[k:k + 1, :] * xx[base + k:base + k + tb, :]
        y = _silu(acc)
        if idx < 2:
            parts = []
            for h in range(cfg.CH):
                yh = y[:, C_HEAD * h:C_HEAD * (h + 1)]
                parts.append(yh * lax.rsqrt(jnp.sum(yh * yh, axis=-1, keepdims=True) + EPS))
            y = jnp.concatenate(parts, axis=1)
        outs.append(y)
    beta = jax.nn.sigmoid(braw)
    g = -jnp.exp(alog) * jax.nn.softplus(araw + dtb)
    return outs[0], outs[1], outs[2], g, beta


def _inverse_unit_lower(low, eye):
    pw = low
    inv = eye - low
    for _ in range(5):
        pwb = pw.astype(BF16)
        pw = _bdot(pwb, pwb, 2, 1, None)
        inv = inv + _bdot(inv.astype(BF16), pw.astype(BF16), 2, 1, None)
    ax = inv + _bdot(low, inv, 2, 1)
    return inv + _bdot(inv, eye - ax, 2, 1)


@jax.custom_vjp
def _saved_inverse(low, inv):
    return inv


def _saved_inverse_fwd(low, inv):
    return inv, inv


def _saved_inverse_bwd(inv, d):
    dlow = -_bdot(_bdot(inv, d, 1, 1), inv, 2, 2)
    return dlow, jnp.zeros_like(inv)


_saved_inverse.defvjp(_saved_inverse_fwd, _saved_inverse_bwd)


def gdn_intra_rows(cfg, first, qn, kn, v, g, beta, inv_saved=None):
    c = CHUNK
    CH = cfg.CH
    nchunk = qn.shape[0] // c
    i = lax.broadcasted_iota(jnp.int32, (c, c), 0)
    j = lax.broadcasted_iota(jnp.int32, (c, c), 1)
    incl = (i >= j)[None]
    strict = (i > j)[None]
    eye = (i == j).astype(F32)[None]
    tri = (i >= j).astype(F32)
    rows = [slice(c * ci, c * (ci + 1)) for ci in range(nchunk)]
    gcs = [jnp.dot(tri, g[r], precision=HI, preferred_element_type=F32) for r in rows]
    pairs = [(ci, h) for ci in range(nchunk) for h in range(CH)]
    heads = lambda x, wd: jnp.stack([x[rows[ci], wd * h:wd * (h + 1)] for ci, h in pairs], axis=0)
    gch = jnp.stack([gcs[ci][:, h:h + 1] for ci, h in pairs], axis=0)
    bh = jnp.stack([beta[rows[ci], h:h + 1] for ci, h in pairs], axis=0)
    q = heads(qn, C_HEAD) * (C_HEAD ** -0.5)
    k = heads(kn, C_HEAD)
    vv = heads(v, C_HEAD)
    a = jnp.broadcast_to(gch, (len(pairs), c, c))
    diff = jnp.where(incl, a - jnp.swapaxes(a, 1, 2), 0.0)
    decay = jnp.where(incl, jnp.exp(diff), 0.0)
    kb = k * bh
    low = jnp.where(strict, _bdot(kb, k, 2, 2) * decay, 0.0)
    if inv_saved is None:
        inv = _inverse_unit_lower(low, eye)
    else:
        inv = _saved_inverse(low, heads(inv_saved, c))
    eg = jnp.exp(gch)
    sol = _bdot(inv, jnp.concatenate([vv * bh, kb * eg], axis=2), 2, 1)
    intra = jnp.where(incl, _bdot(q, k, 2, 2) * decay, 0.0)
    qg = q * eg
    kd = k * jnp.exp(gch[:, c - 1:c, :] - gch)
    glast = jnp.concatenate([jnp.broadcast_to(gc[c - 1:c, :], gc.shape) for gc in gcs], axis=0)

    def unstack(x):
        return jnp.concatenate([jnp.concatenate([x[ci * CH + h] for h in range(CH)], axis=1) for ci in range(nchunk)],
                               axis=0)

    outs = (unstack(sol[:, :, :C_HEAD]), unstack(sol[:, :, C_HEAD:]), unstack(qg), unstack(kd), unstack(intra), glast)
    return outs + (unstack(inv),) if inv_saved is None else outs


def gdn_state_step(S, u, w, qg, kd, intra, glast):
    v_new = u - _bdot(w, S, 2, 1)
    o = _bdot(qg, S, 2, 1) + _bdot(intra, v_new, 2, 1)
    S_next = S * jnp.exp(glast) + _bdot(kd, v_new, 1, 1)
    return o, S_next


def gdn_out_block(cfg, first, o, zc, onw):
    parts = []
    for h in range(cfg.CH):
        sl = slice(C_HEAD * h, C_HEAD * (h + 1))
        oh = o[:, sl]
        y = oh * lax.rsqrt(jnp.mean(oh * oh, axis=-1, keepdims=True) + EPS) * onw
        parts.append(y * _silu(zc[:, sl]))
    return (jnp.concatenate(parts, axis=1),)


def rms_block(x, nw):
    return x * lax.rsqrt(jnp.mean(x * x, axis=-1, keepdims=True) + EPS) * nw


class Row:
    def __init__(self, arr, width, colblk=0, grad=None):
        self.arr, self.width, self.colblk, self.grad = arr, width, colblk, grad


class Halo:
    def __init__(self, arr, width, colblk, hr, tie=None):
        self.arr, self.width, self.colblk, self.hr, self.tie = arr, width, colblk, hr, tie


def _row_specs(tb, rows, halos, params, pos):
    specs = [pl.BlockSpec((tb, r.width), lambda i, cb=r.colblk: (pos(i), cb)) for r in rows]
    specs += [pl.BlockSpec((h.hr, h.width),
                           lambda i, cb=h.colblk, m=tb // h.hr: (jnp.maximum(pos(i) * m - 1, 0), cb))
              for h in halos]
    specs += [pl.BlockSpec(p.shape, lambda i: (0, 0)) for p in params]
    return specs


def rb_fwd(name, fn, n, tb, bps, rows, halos, params, outs, comm=None):
    nr, nh, npar = len(rows), len(halos), len(params)

    def body(*refs):
        ins = refs[:nr + nh + npar]
        o_refs = refs[nr + nh + npar:]
        first = (pl.program_id(0) % bps) == 0
        res = fn(first, *[r[...] for r in ins])
        for ref, val in zip(o_refs, res):
            ref[...] = val.astype(ref.dtype)

    res, carried = call_with_comm(
        body, name, (n // tb,), _row_specs(tb, rows, halos, params, lambda i: i),
        [pl.BlockSpec((tb, w), lambda i: (i, 0)) for w, _ in outs],
        [jax.ShapeDtypeStruct((n, w), dt) for w, dt in outs], [], ("parallel",),
        [r.arr for r in rows] + [h.arr for h in halos] + list(params), comm)
    return (res, carried) if comm is not None else res


def rb_bwd(name, fn, n, tb, bps, rows, halos, params, douts, param_grads, comm=None):
    nr, nh, npar, nd = len(rows), len(halos), len(params), len(douts)
    nblk = n // tb
    grow = [k for k, r in enumerate(rows) if r.grad is not None]
    ghalo = [k for k, h in enumerate(halos) if h.tie is not None]
    gpar = [k for k, f in enumerate(param_grads) if f]
    pos = lambda i: nblk - 1 - i

    def body(*refs):
        ins = refs[:nr + nh + npar]
        d_refs = refs[nr + nh + npar:nr + nh + npar + nd]
        rest = refs[nr + nh + npar + nd:]
        grow_refs = rest[:len(grow)]
        gpar_refs = rest[len(grow):len(grow) + len(gpar)]
        carry_refs = rest[len(grow) + len(gpar):]
        i = pl.program_id(0)
        first = (pos(i) % bps) == 0
        vals = [r[...] for r in ins]
        diff_idx = grow + [nr + k for k in ghalo] + [nr + nh + k for k in gpar]

        def f(*dargs):
            full = list(vals)
            for k, a in zip(diff_idx, dargs):
                full[k] = a
            return fn(first, *full)

        res, vjp = jax.vjp(f, *[vals[k] for k in diff_idx])
        grads = vjp(tuple(d[...].astype(r.dtype) for d, r in zip(d_refs, res)))
        g_rows = list(grads[:len(grow)])
        g_halos = grads[len(grow):len(grow) + len(ghalo)]
        g_pars = grads[len(grow) + len(ghalo):]

        @pl.when(i == 0)
        def _():
            for c in carry_refs:
                c[...] = jnp.zeros_like(c)
            for p in gpar_refs:
                p[...] = jnp.zeros_like(p)

        for k, ref in enumerate(grow_refs):
            ref[...] = g_rows[k].astype(ref.dtype)
        for ci, hk in enumerate(ghalo):
            h = halos[hk]
            k = grow.index(h.tie)
            tail = g_rows[k][tb - h.hr:, :] + carry_refs[ci][...]
            grow_refs[k][tb - h.hr:, :] = tail.astype(grow_refs[k].dtype)
            carry_refs[ci][...] = g_halos[ci]
        for ref, gp in zip(gpar_refs, g_pars):
            ref[...] += gp

    out_specs = [pl.BlockSpec((tb, rows[k].width), lambda i: (pos(i), 0)) for k in grow]
    out_specs += [pl.BlockSpec(params[k].shape, lambda i: (0, 0)) for k in gpar]
    out_shape = [jax.ShapeDtypeStruct((n, rows[k].width), rows[k].grad) for k in grow]
    out_shape += [jax.ShapeDtypeStruct(params[k].shape, F32) for k in gpar]
    in_specs = _row_specs(tb, rows, halos, params, pos)
    in_specs += [pl.BlockSpec((tb, d.shape[1]), lambda i: (pos(i), 0)) for d in douts]
    res, carried = call_with_comm(
        body, name, (nblk,), in_specs, out_specs, out_shape,
        [pltpu.VMEM((halos[k].hr, halos[k].width), F32) for k in ghalo], ("arbitrary",),
        [r.arr for r in rows] + [h.arr for h in halos] + list(params) + list(douts), comm)
    return (res, carried) if comm is not None else res


_DIMS = {'nn': (((1,), (0,)), ((), ())), 'nt': (((1,), (1,)), ((), ())), 'tn': (((0,), (0,)), ((), ()))}


def matmul(name, a, b, mode, tm, tn, tk, out_dtype=F32, add=None, comm=None):
    if mode == 'tn':
        K, M = a.shape
    else:
        M, K = a.shape
    N = b.shape[0] if mode == 'nt' else b.shape[1]
    tm, tn, tk = min(tm, M), min(tn, N), min(tk, K)
    assert M % tm == 0 and N % tn == 0 and K % tk == 0, (name, M, N, K, tm, tn, tk)
    nk = K // tk
    a_spec = pl.BlockSpec((tk, tm), lambda i, j, k: (k, i)) if mode == 'tn' else pl.BlockSpec((tm, tk), lambda i, j, k: (i, k))
    b_spec = pl.BlockSpec((tn, tk), lambda i, j, k: (j, k)) if mode == 'nt' else pl.BlockSpec((tk, tn), lambda i, j, k: (k, j))
    o_spec = pl.BlockSpec((tm, tn), lambda i, j, k: (i, j))
    has_add = add is not None

    def body(*refs):
        a_ref, b_ref = refs[0], refs[1]
        add_ref = refs[2] if has_add else None
        o_ref = refs[-1]
        k = pl.program_id(2)
        part = lax.dot_general(a_ref[...].astype(BF16), b_ref[...].astype(BF16), _DIMS[mode], preferred_element_type=F32)

        @pl.when(k == 0)
        def _():
            o_ref[...] = ((part + add_ref[...]) if has_add else part).astype(o_ref.dtype)

        if nk > 1:
            @pl.when(k > 0)
            def _():
                o_ref[...] += part

    assert nk == 1 or out_dtype == F32
    ins = [a, b] + ([add] if has_add else [])
    in_specs = [a_spec, b_spec] + ([o_spec] if has_add else [])
    outs, couts = call_with_comm(body, name, (M // tm, N // tn, nk), in_specs, [o_spec],
                                 [jax.ShapeDtypeStruct((M, N), out_dtype)], [], ("parallel", "parallel", "arbitrary"),
                                 ins, comm)
    return (outs[0], couts) if comm is not None else outs[0]


def grad_w_blocks(name, h, dpb, tm, tk):
    n, d = h.shape
    nb, _, s = dpb.shape
    tm, tk = min(tm, d), min(tk, n)
    assert d % tm == 0 and n % tk == 0
    nk = n // tk

    def body(h_ref, b_ref, o_ref):
        k = pl.program_id(2)
        part = lax.dot_general(h_ref[...], b_ref[0], _DIMS['tn'], preferred_element_type=F32)

        @pl.when(k == 0)
        def _():
            o_ref[0] = part

        if nk > 1:
            @pl.when(k > 0)
            def _():
                o_ref[0] += part

    return pl.pallas_call(
        body, name=name, grid=(nb, d // tm, nk),
        in_specs=[pl.BlockSpec((tk, tm), lambda j, i, k: (k, i)), pl.BlockSpec((1, tk, s), lambda j, i, k: (j, k, 0))],
        out_specs=pl.BlockSpec((1, tm, s), lambda j, i, k: (j, i, 0)),
        out_shape=jax.ShapeDtypeStruct((nb, d, s), F32),
        compiler_params=_cparams(("parallel", "parallel", "arbitrary")),
    )(h, dpb)


def norm_in_proj(name, x, nw, wp, tm, tn, comm=None):
    n, d = x.shape
    wpc = wp.shape[1]
    tm, tn = min(tm, n), min(tn, wpc)
    assert n % tm == 0 and wpc % tn == 0

    def body(x_ref, nw_ref, w_ref, p_ref, h_ref):
        @pl.when(pl.program_id(1) == 0)
        def _():
            h_ref[...] = rms_block(x_ref[...], nw_ref[...]).astype(BF16)

        p_ref[...] = jnp.dot(h_ref[...], w_ref[...], preferred_element_type=F32)

    outs, couts = call_with_comm(
        body, name, (n // tm, wpc // tn),
        [pl.BlockSpec((tm, d), lambda i, j: (i, 0)), pl.BlockSpec((1, d), lambda i, j: (0, 0)),
         pl.BlockSpec((d, tn), lambda i, j: (0, j))],
        [pl.BlockSpec((tm, tn), lambda i, j: (i, j)), pl.BlockSpec((tm, d), lambda i, j: (i, 0))],
        [jax.ShapeDtypeStruct((n, wpc), F32), jax.ShapeDtypeStruct((n, d), BF16)], [], ("parallel", "arbitrary"),
        [x, nw, wp], comm)
    return (outs[0], outs[1], couts) if comm is not None else (outs[0], outs[1])


def norm_bwd(name, x, nw, dh, dres, tb):
    n, d = x.shape
    tb = min(tb, n)

    def body(x_ref, nw_ref, dh_ref, dres_ref, dx_ref, dnw_ref):
        @pl.when(pl.program_id(0) == 0)
        def _():
            dnw_ref[...] = jnp.zeros_like(dnw_ref)

        _, vjp = jax.vjp(rms_block, x_ref[...], nw_ref[...])
        dx, dnw = vjp(dh_ref[...])
        dx_ref[...] = dx + dres_ref[...]
        dnw_ref[...] += dnw

    row = pl.BlockSpec((tb, d), lambda i: (i, 0))
    par = pl.BlockSpec((1, d), lambda i: (0, 0))
    return pl.pallas_call(
        body, name=name, grid=(n // tb,), in_specs=[row, par, row, row], out_specs=[row, par],
        out_shape=[jax.ShapeDtypeStruct((n, d), F32), jax.ShapeDtypeStruct((1, d), F32)],
        compiler_params=_cparams(("arbitrary",)),
    )(x, nw, dh, dres)


def loss_grad(name, y, target, tb):
    n, d = y.shape
    tb = min(tb, n)

    def body(y_ref, t_ref, dy_ref, loss_ref):
        @pl.when(pl.program_id(0) == 0)
        def _():
            loss_ref[...] = jnp.zeros_like(loss_ref)

        err = y_ref[...] - t_ref[...]
        dy_ref[...] = err * (1.0 / d)
        part = 0.5 * jnp.sum(jnp.mean(err * err, axis=-1, keepdims=True), axis=0, keepdims=True)
        loss_ref[...] += jnp.broadcast_to(part, loss_ref.shape)

    row = pl.BlockSpec((tb, d), lambda i: (i, 0))
    dy, loss = pl.pallas_call(
        body, name=name, grid=(n // tb,), in_specs=[row, row],
        out_specs=[row, pl.BlockSpec((8, LANE), lambda i: (0, 0))],
        out_shape=[jax.ShapeDtypeStruct((n, d), F32), jax.ShapeDtypeStruct((8, LANE), F32)],
        compiler_params=_cparams(("arbitrary",)),
    )(y, target)
    return dy, loss[0, 0]


def rope_tables(name, pos_col, inv_freq_row):
    n = pos_col.shape[0]

    def body(p_ref, f_ref, c_ref, s_ref):
        ang = p_ref[...].astype(F32) * f_ref[...]
        lane = lax.broadcasted_iota(jnp.int32, ang.shape, 1)
        c_ref[...] = jnp.where(lane < ROT_DIM, jnp.cos(ang), 1.0)
        s_ref[...] = jnp.where(lane < ROT_DIM, jnp.sin(ang), 0.0)

    return pl.pallas_call(
        body, name=name, out_shape=[jax.ShapeDtypeStruct((n, A_HEAD), F32)] * 2,
    )(pos_col, inv_freq_row)


def _scan_operands(cfg, nseq, u_ref, w_ref, qg_ref, kd_ref, a_ref, gl_ref):
    pairs = [(b, h) for b in range(nseq) for h in range(cfg.CH)]
    st = lambda r, wd: jnp.stack([r[b, :, wd * h:wd * (h + 1)] for b, h in pairs], axis=0)
    gl = jnp.stack([gl_ref[b, 0:1, h:h + 1] for b, h in pairs], axis=0)
    return st(u_ref, C_HEAD), st(w_ref, C_HEAD), st(qg_ref, C_HEAD), st(kd_ref, C_HEAD), st(a_ref, CHUNK), gl


def gdn_scan_fwd(name, cfg, nseq, u, w, qg, kd, intra, glast):
    CH, CW, T = cfg.CH, cfg.CW, cfg.T
    nc = T // CHUNK

    def body(u_ref, w_ref, qg_ref, kd_ref, a_ref, gl_ref, o_ref, sin_ref, s_ref):
        @pl.when(pl.program_id(0) == 0)
        def _():
            s_ref[...] = jnp.zeros_like(s_ref)

        S = s_ref[...]
        for b in range(nseq):
            sin_ref[b, 0] = S[b * CH:(b + 1) * CH]
        o, S_next = gdn_state_step(S, *_scan_operands(cfg, nseq, u_ref, w_ref, qg_ref, kd_ref, a_ref, gl_ref))
        s_ref[...] = S_next
        for b in range(nseq):
            o_ref[b] = jnp.concatenate([o[b * CH + h] for h in range(CH)], axis=1)

    row = lambda wd: pl.BlockSpec((nseq, CHUNK, wd), lambda c: (0, c, 0))
    widths = [CW, CW, CW, CW, CH * CHUNK, LANE]
    o, s_in = pl.pallas_call(
        body, name=name, grid=(nc,),
        in_specs=[row(x) for x in widths],
        out_specs=[row(CW), pl.BlockSpec((nseq, 1, CH, C_HEAD, C_HEAD), lambda c: (0, c, 0, 0, 0))],
        out_shape=[jax.ShapeDtypeStruct((nseq, T, CW), F32),
                   jax.ShapeDtypeStruct((nseq, nc, CH, C_HEAD, C_HEAD), F32)],
        scratch_shapes=[pltpu.VMEM((nseq * CH, C_HEAD, C_HEAD), F32)],
        compiler_params=_cparams(("arbitrary",)),
    )(*[a.reshape(nseq, T, a.shape[1]) for a in (u, w, qg, kd, intra, glast)])
    return o.reshape(nseq * T, CW), s_in


def gdn_scan_bwd(name, cfg, nseq, u, w, qg, kd, intra, glast, s_in, do, comm=None):
    CH, CW, T = cfg.CH, cfg.CW, cfg.T
    nc = T // CHUNK

    def body(u_ref, w_ref, qg_ref, kd_ref, a_ref, gl_ref, sin_ref, do_ref,
             du_ref, dw_ref, dqg_ref, dkd_ref, da_ref, dgl_ref, ds_ref):
        @pl.when(pl.program_id(0) == 0)
        def _():
            ds_ref[...] = jnp.zeros_like(ds_ref)

        S = jnp.concatenate([sin_ref[b, 0] for b in range(nseq)], axis=0)
        dout = jnp.stack([do_ref[b, :, C_HEAD * h:C_HEAD * (h + 1)] for b in range(nseq) for h in range(CH)], axis=0)
        _, vjp = jax.vjp(gdn_state_step, S, *_scan_operands(cfg, nseq, u_ref, w_ref, qg_ref, kd_ref, a_ref, gl_ref))
        dS, du, dw, dqg, dkd, da, dg = vjp((dout, ds_ref[...]))
        ds_ref[...] = dS
        lane = lax.broadcasted_iota(jnp.int32, (CHUNK, LANE), 1)
        rowi = lax.broadcasted_iota(jnp.int32, (CHUNK, LANE), 0)
        for b in range(nseq):
            cat = lambda x: jnp.concatenate([x[b * CH + h] for h in range(CH)], axis=1)
            du_ref[b] = cat(du)
            dw_ref[b] = cat(dw)
            dqg_ref[b] = cat(dqg)
            dkd_ref[b] = cat(dkd)
            da_ref[b] = cat(da)
            dgl = jnp.zeros((CHUNK, LANE), F32)
            for h in range(CH):
                dgl = dgl + jnp.where((lane == h) & (rowi == 0), dg[b * CH + h], 0.0)
            dgl_ref[b] = dgl

    row = lambda wd: pl.BlockSpec((nseq, CHUNK, wd), lambda c: (0, nc - 1 - c, 0))
    widths = [CW, CW, CW, CW, CH * CHUNK, LANE]
    outs, carried = call_with_comm(
        body, name, (nc,),
        [row(x) for x in widths]
        + [pl.BlockSpec((nseq, 1, CH, C_HEAD, C_HEAD), lambda c: (0, nc - 1 - c, 0, 0, 0)), row(CW)],
        [row(x) for x in widths], [jax.ShapeDtypeStruct((nseq, T, x), F32) for x in widths],
        [pltpu.VMEM((nseq * CH, C_HEAD, C_HEAD), F32)], ("arbitrary",),
        [a.reshape(nseq, T, a.shape[1]) for a in (u, w, qg, kd, intra, glast)] + [s_in, do.reshape(nseq, T, CW)], comm)
    return [o.reshape(nseq * T, o.shape[2]) for o in outs], carried


def _tile(total, cap, unit=LANE):
    best = None
    for t in range(unit, min(cap, total) + 1, unit):
        if total % t == 0:
            best = t
    assert best is not None, (total, cap, unit)
    return best


def _pad_lanes(v, width=LANE):
    return jnp.pad(v.reshape(1, -1), ((0, 0), (0, width - v.shape[-1])))


def permute_w_in(cfg, w):
    parts = []
    for n in cfg.order:
        off, wd = cfg.orig[n]
        blk = w[:, off:off + wd]
        if cfg.g[n][1] != wd:
            blk = jnp.pad(blk, ((0, 0), (0, cfg.g[n][1] - wd)))
        parts.append(blk)
    return jnp.concatenate(parts, axis=1)


def chip_blocks(cfg, groups, n_chips):
    s = cfg.IN_COLS // n_chips
    blocks = []
    for j in range(n_chips):
        lo, hi = j * s, (j + 1) * s
        pieces = []
        for name, (off, wd) in cfg.orig.items():
            a, b = max(lo, off), min(hi, off + wd)
            if a < b:
                pieces.append(groups[name][:, a - off:b - off])
        blocks.append(jnp.concatenate(pieces, axis=1))
    return jnp.stack(blocks)


def _layer_params(cfg, prm):
    return dict(
        nw=prm['norm_w'].reshape(1, -1),
        qnw=prm['q_norm_w'].reshape(1, -1), knw=prm['k_norm_w'].reshape(1, -1),
        sinks_row=jnp.repeat(prm['sinks'], A_HEAD).reshape(1, -1),
        cw=prm['b_conv_w'], cb=prm['b_conv_b'].reshape(1, -1),
        lw=prm['b_ln_w'].reshape(1, -1), lb=prm['b_ln_b'].reshape(1, -1),
        pw=prm['b_pw_w'], pb=prm['b_pw_b'].reshape(1, -1),
        ccw=prm['c_conv_w'], alog=_pad_lanes(prm['c_a_log']), dtb=_pad_lanes(prm['c_dt_bias']),
        onw=prm['c_onorm_w'].reshape(1, -1),
    )


def _attn_io(cfg, p, cos, sin, grads):
    gq = BF16 if grads else None
    rows = [Row(p, cfg.AW, cfg.blk('qa'), gq), Row(p, cfg.AW, cfg.blk('za'), gq),
            Row(p, cfg.AKW, cfg.blk('ka'), gq), Row(p, cfg.AKW, cfg.blk('va'), gq),
            Row(cos, A_HEAD), Row(sin, A_HEAD)]
    halos = [Halo(p, cfg.AKW, cfg.blk('ka'), ATTN_BLOCK, 2 if grads else None),
             Halo(p, cfg.AKW, cfg.blk('va'), ATTN_BLOCK, 3 if grads else None),
             Halo(cos, A_HEAD, 0, ATTN_BLOCK), Halo(sin, A_HEAD, 0, ATTN_BLOCK)]
    return rows, halos


def _conv_io(cfg, p, grads):
    gq = BF16 if grads else None
    rows = [Row(p, 2 * cfg.BW, cfg.blk('ub'), gq), Row(p, cfg.BW, cfg.blk('zb'), gq)]
    halos = [Halo(p, 2 * cfg.BW, cfg.blk('ub'), B_HALO, 0 if grads else None)]
    return rows, halos


def _prep_io(cfg, p, grads):
    gq = BF16 if grads else None
    rows = [Row(p, cfg.CW, cfg.blk(n), gq) for n in ('qc', 'kc', 'vc')]
    rows += [Row(p, LANE, cfg.blk('bc'), gq), Row(p, LANE, cfg.blk('ac'), gq)]
    halos = [Halo(p, cfg.CW, cfg.blk(n), C_HALO, k if grads else None) for k, n in enumerate(('qc', 'kc', 'vc'))]
    return rows, halos


TB_CONV = 128
TB_PREP = 256
TB_OUT = 256
TB_INTRA_FWD = 128
TB_INTRA_BWD = 128


def layer_forward(cfg, l, x, lp, wp, wo, cos, sin, comms=(None, None)):
    n = x.shape[0]
    nseq = n // cfg.T
    T = cfg.T
    p, h, *carried = norm_in_proj(f"in_proj_{l}", x, lp['nw'], wp, 1024, _tile(cfg.WP, 768), comm=comms[0])
    rows, halos = _attn_io(cfg, p, cos, sin, False)
    oa = rb_fwd(f"attn_fwd_{l}", functools.partial(attn_block, cfg), n, ATTN_BLOCK, T // ATTN_BLOCK, rows, halos,
                [lp['qnw'], lp['knw'], lp['sinks_row']], [(cfg.AW, BF16)], comm=comms[1])
    if comms[1] is not None:
        oa, more = oa
        carried = [carried[0] + more] if carried else [more]
    (oa,) = oa
    rows, halos = _conv_io(cfg, p, False)
    tbb = min(TB_CONV, T)
    (ob,) = rb_fwd(f"conv_fwd_{l}", functools.partial(conv_block, cfg), n, tbb, T // tbb, rows, halos,
                   [lp['cw'], lp['cb'], lp['lw'], lp['lb'], lp['pw'], lp['pb']], [(cfg.BW, BF16)])
    rows, halos = _prep_io(cfg, p, False)
    tbp = min(TB_PREP, T)
    qn, kn, v, g, beta = rb_fwd(f"gdn_prep_fwd_{l}", functools.partial(gdn_prep_block, cfg), n, tbp, T // tbp, rows,
                                halos, [lp['ccw'], lp['alog'], lp['dtb']],
                                [(cfg.CW, F32)] * 3 + [(LANE, F32)] * 2)
    intra_outs = rb_fwd(f"gdn_intra_fwd_{l}", functools.partial(gdn_intra_rows, cfg), n, TB_INTRA_FWD, T // TB_INTRA_FWD,
                        [Row(qn, cfg.CW), Row(kn, cfg.CW), Row(v, cfg.CW), Row(g, LANE), Row(beta, LANE)], [], [],
                        [(cfg.CW, F32)] * 4 + [(cfg.CH * CHUNK, F32), (LANE, F32), (cfg.CH * CHUNK, F32)])
    intra_outs, inv = intra_outs[:6], intra_outs[6]
    o, s_in = gdn_scan_fwd(f"gdn_scan_fwd_{l}", cfg, nseq, *intra_outs)
    tbo = min(TB_OUT, T)
    (oc,) = rb_fwd(f"gdn_out_fwd_{l}", functools.partial(gdn_out_block, cfg), n, tbo, T // tbo,
                   [Row(o, cfg.CW), Row(p, cfg.CW, cfg.blk('zc'))], [], [lp['onw']], [(cfg.CW, BF16)])
    y = jnp.concatenate([oa, ob, oc], axis=1)
    x_next = matmul(f"out_proj_{l}", y, wo, 'nn', 1024, 1024, cfg.D, add=x)
    saved = dict(x=x, p=p, h=h, y=y, qn=qn, kn=kn, v=v, g=g, beta=beta, intra_outs=intra_outs, inv=inv, s_in=s_in, o=o)
    return x_next, saved, (carried[0] if carried else None)


def layer_backward(cfg, l, dxn, sv, lp, wp, wo, cos, sin, rs=None, own_rs=None):
    n = dxn.shape[0]
    nseq = n // cfg.T
    T = cfg.T
    p = sv['p']
    AW, BW, CW = cfg.AW, cfg.BW, cfg.CW
    if rs is None:
        dy = matmul(f"dy_{l}", dxn, wo, 'nt', 1024, 1024, cfg.D)
    else:
        dy, received = matmul(f"dy_{l}", dxn, wo, 'nt', 1024, 1024, cfg.D, comm=rs.swap())
    dwo = matmul(f"dwo_{l}", sv['y'], dxn, 'tn', 1024, 1024, 2048)
    doa, dob, doc = dy[:, :AW], dy[:, AW:AW + BW], dy[:, AW + BW:]
    tbo = min(TB_OUT, T)
    do, dzc, donw = rb_bwd(f"gdn_out_bwd_{l}", functools.partial(gdn_out_block, cfg), n, tbo, T // tbo,
                           [Row(sv['o'], CW, 0, F32), Row(p, CW, cfg.blk('zc'), BF16)], [], [lp['onw']], [doc], [True])
    if rs is not None:
        rs.add(received)
    dintra, got_rest = gdn_scan_bwd(f"gdn_scan_bwd_{l}", cfg, nseq, *sv['intra_outs'], sv['s_in'], do,
                                    comm=None if rs is None else rs.scatter([1, 2]))
    dqn, dkn, dv, dg, dbeta = rb_bwd(
        f"gdn_intra_bwd_{l}", functools.partial(gdn_intra_rows, cfg), n, TB_INTRA_BWD, T // TB_INTRA_BWD,
        [Row(sv['qn'], CW, 0, F32), Row(sv['kn'], CW, 0, F32), Row(sv['v'], CW, 0, F32), Row(sv['g'], LANE, 0, F32),
         Row(sv['beta'], LANE, 0, F32), Row(sv['inv'], cfg.CH * CHUNK)], [], [], list(dintra), [])
    rows, halos = _prep_io(cfg, p, True)
    tbp = min(TB_PREP, T)
    dqc, dkc, dvc, dbc, dac, dccw, dalog, ddtb = rb_bwd(
        f"gdn_prep_bwd_{l}", functools.partial(gdn_prep_block, cfg), n, tbp, T // tbp, rows, halos,
        [lp['ccw'], lp['alog'], lp['dtb']], [dqn, dkn, dv, dg, dbeta], [True] * 3)
    rows, halos = _conv_io(cfg, p, True)
    tbb = min(TB_CONV, T)
    conv_grads = rb_bwd(
        f"conv_bwd_{l}", functools.partial(conv_block, cfg), n, tbb, T // tbb, rows, halos,
        [lp['cw'], lp['cb'], lp['lw'], lp['lb'], lp['pw'], lp['pb']], [dob], [True] * 6,
        comm=None if rs is None else rs.scatter([0]))
    got = None
    if rs is not None:
        conv_grads, got_w_in = conv_grads
        got = got_w_in + got_rest
    dub, dzb, dcw, dcb, dlw, dlb, dpw, dpb = conv_grads
    rows, halos = _attn_io(cfg, p, cos, sin, True)
    dqa, dza, dka, dva, dqnw, dknw, dsinks_row = rb_bwd(
        f"attn_bwd_{l}", functools.partial(attn_block, cfg), n, ATTN_BLOCK, T // ATTN_BLOCK, rows, halos,
        [lp['qnw'], lp['knw'], lp['sinks_row']], [doa], [True] * 3)
    dgroups = dict(qa=dqa, za=dza, qc=dqc, kc=dkc, vc=dvc, zc=dzc, ka=dka, va=dva, ub=dub, zb=dzb, bc=dbc, ac=dac)
    dp = jnp.concatenate([dgroups[k] for k in cfg.order], axis=1)
    dw_in = grad_w_blocks(f"dwp_{l}", sv['h'], chip_blocks(cfg, dgroups, N_CHIPS), 1024, 2048)
    if own_rs is None:
        dh = matmul(f"dh_{l}", dp, wp, 'nt', 1024, 1024, _tile(cfg.WP, 2304))
    else:
        mine = own_rs(dict(w_in_blocks=dw_in, w_out=dwo, b_pw_w=dpw))
        dh, mine_received = matmul(f"dh_{l}", dp, wp, 'nt', 1024, 1024, _tile(cfg.WP, 2304), comm=mine.swap())
        mine.add(mine_received)
    dx, dnw = norm_bwd(f"norm_bwd_{l}", sv['x'], lp['nw'], dh, dxn, 256)
    grads = dict(
        norm_w=dnw[0], w_in_blocks=dw_in, q_norm_w=dqnw[0], k_norm_w=dknw[0],
        sinks=dsinks_row.reshape(cfg.AQH, A_HEAD)[:, 0],
        b_conv_w=dcw, b_conv_b=dcb[0], b_ln_w=dlw[0], b_ln_b=dlb[0], b_pw_w=dpw, b_pw_b=dpb[0],
        c_conv_w=dccw, c_a_log=dalog[0, :cfg.CH], c_dt_bias=ddtb[0, :cfg.CH], c_onorm_w=donw[0], w_out=dwo)
    return dx, grads, (got if own_rs is None else (got, mine))


def rope_for(cfg, positions):
    n = positions.size
    inv_freq = ROPE_THETA ** (-np.arange(0, ROT_DIM, 2, dtype=np.float32) / ROT_DIM)
    freq_row = np.zeros((1, A_HEAD), np.float32)
    freq_row[0, :ROT_DIM] = np.concatenate([inv_freq, inv_freq])
    return rope_tables("rope_tables", positions.reshape(n, 1), jnp.asarray(freq_row))


def local_step(cfg, x, positions, prm, wps, wos, target):
    nseq = x.shape[0]
    n = nseq * cfg.T
    cos, sin = rope_for(cfg, positions)
    lps = [_layer_params(cfg, {k: v[l] for k, v in prm.items()}) for l in range(DEPTH)]
    saved = []
    xl = x.reshape(n, cfg.D)
    for l in range(DEPTH):
        xl, sv, _ = layer_forward(cfg, l, xl, lps[l], wps[l], wos[l], cos, sin)
        saved.append(sv)
    dx, loss = loss_grad("loss_grad", xl, target.reshape(n, cfg.D), 256)
    grads = [None] * DEPTH
    for l in reversed(range(DEPTH)):
        dx, grads[l], _ = layer_backward(cfg, l, dx, saved[l], lps[l], wps[l], wos[l], cos, sin)
    return loss, dx.reshape(x.shape), grads


N_CHIPS = 4
N_DEV = 8


def _place():
    return lax.axis_index("x"), lax.axis_index("y"), lax.axis_index("c")


def _other_chips(x, y):
    return [(1 - x, y), (x, 1 - y), (1 - x, 1 - y)]


def _remote(src, dst, send, recv, to):
    return pltpu.make_async_remote_copy(src_ref=src, dst_ref=dst, send_sem=send, recv_sem=recv, device_id=to,
                                        device_id_type=MESH)


def gather_comm(arrs):
    n = len(arrs)

    def half(c):
        return [pl.ds(c * (a.shape[0] // 2), a.shape[0] // 2) for a in arrs]

    def first_copies(ins, outs, send, recv):
        x, y, c = _place()
        me = 2 * x + y
        mine = half(c)
        return [_remote(ins[i].at[mine[i]], outs[i].at[me, mine[i]], send.at[i, j], recv.at[i, j], (cx, cy, c))
                for i in range(n) for j, (cx, cy) in enumerate(_other_chips(x, y))]

    def start(ins, outs, sems):
        for cp in first_copies(ins, outs, *sems):
            cp.start()

    def finish(ins, outs, sems):
        send, recv = sems
        x, y, c = _place()
        chips = _other_chips(x, y)
        sib = (x, y, 1 - c)
        passed = []
        mine, other = half(c), half(1 - c)
        for i in range(n):
            for j, (cx, cy) in enumerate(chips):
                blk = outs[i].at[2 * cx + cy, mine[i]]
                _remote(blk, blk, send.at[i, j], recv.at[i, j], (x, y, c)).wait_recv()
                cp = _remote(blk, blk, send.at[i, 3 + j], recv.at[i, 3 + j], sib)
                cp.start()
                passed.append(cp)
        for i in range(n):
            for j, (cx, cy) in enumerate(chips):
                blk = outs[i].at[2 * cx + cy, other[i]]
                _remote(blk, blk, send.at[i, 3 + j], recv.at[i, 3 + j], sib).wait_recv()
        for cp in first_copies(ins, outs, send, recv) + passed:
            cp.wait_send()

    return Comm(arrs, [jax.ShapeDtypeStruct((N_CHIPS,) + a.shape, a.dtype) for a in arrs],
                [pltpu.SemaphoreType.DMA((n, 6)), pltpu.SemaphoreType.DMA((n, 6))], start, finish)


def fill_own(gathered, arrs):
    me = 2 * lax.axis_index("x") + lax.axis_index("y")
    return [lax.dynamic_update_index_in_dim(o, a, me, 0) for o, a in zip(gathered, arrs)]


def swap_comm(arrs):
    n = len(arrs)

    def copies(ins, outs, send, recv):
        x, y, c = _place()
        return [_remote(ins[i].at[:, 1 - c], outs[i], send.at[i], recv.at[i], (x, y, 1 - c)) for i in range(n)]

    def start(ins, outs, sems):
        for cp in copies(ins, outs, *sems):
            cp.start()

    def finish(ins, outs, sems):
        for cp in copies(ins, outs, *sems):
            cp.wait()

    return Comm(arrs, [jax.ShapeDtypeStruct((a.shape[0],) + a.shape[2:], a.dtype) for a in arrs],
                [pltpu.SemaphoreType.DMA((n,)), pltpu.SemaphoreType.DMA((n,))], start, finish)


def scatter_comm(arrs):
    n = len(arrs)

    def copies(ins, outs, send, recv):
        x, y, c = _place()
        return [_remote(ins[i].at[2 * cx + cy], outs[i].at[j], send.at[i, j], recv.at[i, j], (cx, cy, c))
                for i in range(n) for j, (cx, cy) in enumerate(_other_chips(x, y))]

    def start(ins, outs, sems):
        for cp in copies(ins, outs, *sems):
            cp.start()

    def finish(ins, outs, sems):
        send, recv = sems
        x, y, c = _place()
        for i in range(n):
            for j in range(3):
                blk = outs[i].at[j]
                _remote(blk, blk, send.at[i, j], recv.at[i, j], (x, y, c)).wait_recv()
        for cp in copies(ins, outs, send, recv):
            cp.wait_send()

    return Comm(arrs, [jax.ShapeDtypeStruct((3,) + a.shape[1:], a.dtype) for a in arrs],
                [pltpu.SemaphoreType.DMA((n, 3)), pltpu.SemaphoreType.DMA((n, 3))], start, finish)


def share_comm(arrs):
    n = len(arrs)

    def copies(outs, send, recv):
        x, y, c = _place()
        return [_remote(outs[i].at[c], outs[i].at[c], send.at[i], recv.at[i], (x, y, 1 - c)) for i in range(n)]

    def start(ins, outs, sems):
        for cp in copies(outs, *sems):
            cp.start()

    def finish(ins, outs, sems):
        send, recv = sems
        x, y, c = _place()
        for i in range(n):
            blk = outs[i].at[1 - c]
            _remote(blk, blk, send.at[i], recv.at[i], (x, y, c)).wait_recv()
        for cp in copies(outs, send, recv):
            cp.wait_send()

    return Comm(arrs, [jax.ShapeDtypeStruct(a.shape, a.dtype) for a in arrs],
                [pltpu.SemaphoreType.DMA((n,)), pltpu.SemaphoreType.DMA((n,))], start, finish,
                aliases={i: i for i in range(n)})


def all_reduce_small(name, packed):
    r = packed.shape[0]

    def body(in_ref, out_ref, buf, send, recv):
        x, y, c = _place()
        me = 4 * x + 2 * y + c
        buf[me] = in_ref[...]
        flips = [(fx, fy, fc) for fx in (0, 1) for fy in (0, 1) for fc in (0, 1) if (fx, fy, fc) != (0, 0, 0)]
        peers = [((x + fx) % 2, (y + fy) % 2, (c + fc) % 2) for fx, fy, fc in flips]
        cps = [_remote(in_ref, buf.at[me], send.at[k], recv.at[k], peer) for k, peer in enumerate(peers)]
        for cp in cps:
            cp.start()
        for k, (px, py, pc) in enumerate(peers):
            blk = buf.at[4 * px + 2 * py + pc]
            _remote(blk, blk, send.at[k], recv.at[k], (x, y, c)).wait_recv()
        for cp in cps:
            cp.wait_send()
        acc = buf[0]
        for d in range(1, N_DEV):
            acc = acc + buf[d]
        out_ref[...] = acc

    vm = pl.BlockSpec(memory_space=pltpu.VMEM)
    return pl.pallas_call(
        body, name=name, in_specs=[vm], out_specs=vm, out_shape=jax.ShapeDtypeStruct(packed.shape, F32),
        scratch_shapes=[pltpu.VMEM((N_DEV, r, LANE), F32), pltpu.SemaphoreType.DMA((N_DEV - 1,)),
                        pltpu.SemaphoreType.DMA((N_DEV - 1,))],
    )(packed)


def add_own_half(name, g, a, c_idx, tr):
    nch, _, r, cc = g.shape
    tr = min(tr, r)

    def body(c_ref, g_ref, a_ref, o_ref):
        o_ref[...] = (g_ref[0] + a_ref[...]).astype(o_ref.dtype)

    return pl.pallas_call(
        body, name=name,
        grid_spec=pltpu.PrefetchScalarGridSpec(
            num_scalar_prefetch=1, grid=(nch, r // tr),
            in_specs=[pl.BlockSpec((1, 1, tr, cc), lambda j, i, c_ref: (j, c_ref[0], i, 0)),
                      pl.BlockSpec((1, tr, cc), lambda j, i, c_ref: (j, i, 0))],
            out_specs=pl.BlockSpec((1, tr, cc), lambda j, i, c_ref: (j, i, 0))),
        out_shape=jax.ShapeDtypeStruct(a.shape, BF16),
        compiler_params=_cparams(("parallel", "parallel")),
    )(c_idx, g, a)


def sum_chips(name, p, b, idx, tr):
    _, r, cc = p.shape
    tr = min(tr, r)

    def body(idx_ref, p_ref, b_ref, o_ref):
        acc = p_ref[0].astype(F32)
        for k in range(3):
            acc = acc + b_ref[k].astype(F32)
        o_ref[0] = acc

    return pl.pallas_call(
        body, name=name,
        grid_spec=pltpu.PrefetchScalarGridSpec(
            num_scalar_prefetch=1, grid=(r // tr,),
            in_specs=[pl.BlockSpec((1, tr, cc), lambda i, s: (s[0], i, 0)),
                      pl.BlockSpec((3, tr, cc), lambda i, s: (0, i, 0))],
            out_specs=pl.BlockSpec((1, tr, cc), lambda i, s: (s[1], i, 0))),
        out_shape=jax.ShapeDtypeStruct((2, r, cc), F32),
        compiler_params=_cparams(("parallel",)),
    )(idx, p, b)


class GradReduce:
    def __init__(self, tag, parts, chip, c_idx):
        self.tag, self.c_idx = tag, c_idx
        self.parts = [p.reshape(p.shape[0], 2, p.shape[1] // 2, p.shape[2]) for p in parts]
        self.idx = jnp.concatenate([chip.astype(jnp.int32).reshape(1), c_idx])

    def swap(self):
        return swap_comm(self.parts)

    def add(self, received):
        self.part = [add_own_half(f"rs{self.tag}_add_sibling_{t}", g, a, self.c_idx, 128)
                     for t, (g, a) in enumerate(zip(self.parts, received))]

    def scatter(self, which=None):
        return scatter_comm(self.part if which is None else [self.part[t] for t in which])

    def finish(self, got):
        red = [sum_chips(f"rs{self.tag}_sum_chips_{t}", p, b, self.idx, 128) for t, (p, b) in enumerate(zip(self.part, got))]
        out = run_comm(f"rs{self.tag}_share_halves", share_comm(red))
        return [o.reshape(-1, o.shape[-1]) for o in out]


def adamw_many(name, ws, gs, ms, vs):
    n = len(ws)

    def body(*refs):
        for i in range(n):
            w_ref, g_ref, m_ref, v_ref = (refs[k * n + i] for k in range(4))
            d_ref, mo_ref, vo_ref = (refs[(4 + k) * n + i] for k in range(3))
            g = g_ref[...]
            m = ADAM_B1 * m_ref[...] + (1.0 - ADAM_B1) * g
            v = ADAM_B2 * v_ref[...] + (1.0 - ADAM_B2) * jnp.square(g)
            m_hat = m / (1.0 - ADAM_B1 ** ADAM_STEP)
            v_hat = v / (1.0 - ADAM_B2 ** ADAM_STEP)
            d_ref[...] = -ADAM_LR * (m_hat / (jnp.sqrt(v_hat) + ADAM_EPS) + ADAM_WD * w_ref[...])
            mo_ref[...] = m
            vo_ref[...] = v

    vm = pl.BlockSpec(memory_space=pltpu.VMEM)
    return pl.pallas_call(
        body, name=name, in_specs=[vm] * (4 * n), out_specs=[vm] * (3 * n),
        out_shape=[jax.ShapeDtypeStruct(a.shape, F32) for a in ws] * 3,
    )(*ws, *gs, *ms, *vs)


def adamw_layers(name, w, g0, g1, m, v, tb):
    _, r, cc = w.shape
    tb = min(tb, r)
    nb = r // tb

    def body(w_ref, g0_ref, g1_ref, m_ref, v_ref, g_ref, d_ref, mo_ref, vo_ref):
        g = jnp.where(pl.program_id(0) == 0, g0_ref[...], g1_ref[...])
        m = ADAM_B1 * m_ref[0] + (1.0 - ADAM_B1) * g
        v = ADAM_B2 * v_ref[0] + (1.0 - ADAM_B2) * jnp.square(g)
        m_hat = m / (1.0 - ADAM_B1 ** ADAM_STEP)
        v_hat = v / (1.0 - ADAM_B2 ** ADAM_STEP)
        g_ref[0] = g
        d_ref[0] = -ADAM_LR * (m_hat / (jnp.sqrt(v_hat) + ADAM_EPS) + ADAM_WD * w_ref[0])
        mo_ref[0] = m
        vo_ref[0] = v

    spec = pl.BlockSpec((1, tb, cc), lambda l, i: (l, i, 0))
    g0_spec = pl.BlockSpec((tb, cc), lambda l, i: (jnp.where(l == 0, i, nb - 1), 0))
    g1_spec = pl.BlockSpec((tb, cc), lambda l, i: (jnp.where(l == 1, i, 0), 0))
    return pl.pallas_call(
        body, name=name, grid=(2, nb), in_specs=[spec, g0_spec, g1_spec, spec, spec], out_specs=[spec] * 4,
        out_shape=[jax.ShapeDtypeStruct(w.shape, F32)] * 4,
        compiler_params=_cparams(("arbitrary", "arbitrary")),
    )(w, g0, g1, m, v)


def adamw_cols_major(name, w, g0, g1, m, v, tb=LANE):
    wt, mt, vt = (jnp.transpose(a, (2, 0, 1)) for a in (w, m, v))
    cc, _, r = wt.shape

    def body(w_ref, g0_ref, g1_ref, m_ref, v_ref, g_ref, d_ref, mo_ref, vo_ref):
        for l, gl_ref in enumerate((g0_ref, g1_ref)):
            g = gl_ref[...].T
            m = ADAM_B1 * m_ref[:, l, :] + (1.0 - ADAM_B1) * g
            v = ADAM_B2 * v_ref[:, l, :] + (1.0 - ADAM_B2) * jnp.square(g)
            m_hat = m / (1.0 - ADAM_B1 ** ADAM_STEP)
            v_hat = v / (1.0 - ADAM_B2 ** ADAM_STEP)
            g_ref[:, l, :] = g
            d_ref[:, l, :] = -ADAM_LR * (m_hat / (jnp.sqrt(v_hat) + ADAM_EPS) + ADAM_WD * w_ref[:, l, :])
            mo_ref[:, l, :] = m
            vo_ref[:, l, :] = v

    spec = pl.BlockSpec((tb, 2, r), lambda i: (i, 0, 0))
    gspec = pl.BlockSpec((r, tb), lambda i: (0, i))
    outs = pl.pallas_call(
        body, name=name, grid=(pl.cdiv(cc, tb),), in_specs=[spec, gspec, gspec, spec, spec], out_specs=[spec] * 4,
        out_shape=[jax.ShapeDtypeStruct(wt.shape, F32)] * 4,
        compiler_params=_cparams(("parallel",)),
    )(wt, g0, g1, mt, vt)
    return [jnp.transpose(o, (1, 2, 0)) for o in outs]


def _pack(arrs):
    flat = jnp.concatenate([a.reshape(-1).astype(F32) for a in arrs])
    pad = (-flat.shape[0]) % (8 * LANE)
    return jnp.pad(flat, (0, pad)).reshape(-1, LANE)


def _unpack(packed, shapes):
    flat = packed.reshape(-1)
    out, off = [], 0
    for s in shapes:
        size = math.prod(s)
        out.append(flat[off:off + size].reshape(s))
        off += size
    return out


BIG = ('w_in', 'w_out', 'b_pw_w')
SMALL = tuple(k for k in WEIGHTS if k not in BIG)
CHIP_SHARDED_SMALL = {'b_conv_w': 2, 'c_conv_w': 2}


def kernel(x, positions, norm_w, w_in, q_norm_w, k_norm_w, sinks, b_conv_w, b_conv_b, b_ln_w, b_ln_b, b_pw_w, b_pw_b, c_conv_w, c_a_log, c_dt_bias, c_onorm_w, w_out, loss_target, m_norm_w, m_w_in, m_q_norm_w, m_k_norm_w, m_sinks, m_b_conv_w, m_b_conv_b, m_b_ln_w, m_b_ln_b, m_b_pw_w, m_b_pw_b, m_c_conv_w, m_c_a_log, m_c_dt_bias, m_c_onorm_w, m_w_out, v_norm_w, v_w_in, v_q_norm_w, v_k_norm_w, v_sinks, v_b_conv_w, v_b_conv_b, v_b_ln_w, v_b_ln_b, v_b_pw_w, v_b_pw_b, v_c_conv_w, v_c_a_log, v_c_dt_bias, v_c_onorm_w, v_w_out):
    cfg = Cfg(x.shape[-1], x.shape[-2])
    w = dict(norm_w=norm_w, w_in=w_in, q_norm_w=q_norm_w, k_norm_w=k_norm_w, sinks=sinks, b_conv_w=b_conv_w,
             b_conv_b=b_conv_b, b_ln_w=b_ln_w, b_ln_b=b_ln_b, b_pw_w=b_pw_w, b_pw_b=b_pw_b, c_conv_w=c_conv_w,
             c_a_log=c_a_log, c_dt_bias=c_dt_bias, c_onorm_w=c_onorm_w, w_out=w_out)
    m = dict(norm_w=m_norm_w, w_in=m_w_in, q_norm_w=m_q_norm_w, k_norm_w=m_k_norm_w, sinks=m_sinks,
             b_conv_w=m_b_conv_w, b_conv_b=m_b_conv_b, b_ln_w=m_b_ln_w, b_ln_b=m_b_ln_b, b_pw_w=m_b_pw_w,
             b_pw_b=m_b_pw_b, c_conv_w=m_c_conv_w, c_a_log=m_c_a_log, c_dt_bias=m_c_dt_bias, c_onorm_w=m_c_onorm_w,
             w_out=m_w_out)
    v = dict(norm_w=v_norm_w, w_in=v_w_in, q_norm_w=v_q_norm_w, k_norm_w=v_k_norm_w, sinks=v_sinks,
             b_conv_w=v_b_conv_w, b_conv_b=v_b_conv_b, b_ln_w=v_b_ln_w, b_ln_b=v_b_ln_b, b_pw_w=v_b_pw_w,
             b_pw_b=v_b_pw_b, c_conv_w=v_c_conv_w, c_a_log=v_c_a_log, c_dt_bias=v_c_dt_bias, c_onorm_w=v_c_onorm_w,
             w_out=v_w_out)
    chip = 2 * lax.axis_index("x") + lax.axis_index("y")
    c_idx = lax.axis_index("c").astype(jnp.int32).reshape(1)
    D, T = cfg.D, cfg.T
    nseq = x.shape[0]
    n = nseq * T
    w_in_b, w_out_b = w_in.astype(BF16), w_out.astype(BF16)
    per_layer = lambda l: [w_in_b[l], w_out_b[l], b_pw_w[l]]

    def full_weights(g_in, g_out, g_pw):
        return permute_w_in(cfg, jnp.concatenate(list(g_in), axis=1)), g_out.reshape(D, D), g_pw.reshape(cfg.BW, cfg.BW)

    def layer_prm(l, pw_full):
        prm = {k: w[k][l] for k in SMALL}
        prm['b_pw_w'] = pw_full
        prm['b_conv_w'] = jnp.concatenate(list(g_bcw[:, l]), axis=1)
        prm['c_conv_w'] = jnp.concatenate(list(g_ccw[:, l]), axis=1)
        return _layer_params(cfg, prm)

    first = per_layer(0) + [b_conv_w, c_conv_w]
    g_in0, g_out0, g_pw0, g_bcw, g_ccw = fill_own(run_comm("gather_weights_0", gather_comm(first)), first)
    wp0, wo0, pw0 = full_weights(g_in0, g_out0, g_pw0)
    cos, sin = rope_for(cfg, positions)
    lp0 = layer_prm(0, pw0)
    second = per_layer(1)
    x1, sv0, gathered = layer_forward(cfg, 0, x.reshape(n, D), lp0, wp0, wo0, cos, sin,
                                      comms=(gather_comm(second[:1]), gather_comm(second[1:])))
    wp1, wo1, pw1 = full_weights(*fill_own(gathered, second))
    lp1 = layer_prm(1, pw1)
    x2, sv1, _ = layer_forward(cfg, 1, x1, lp1, wp1, wo1, cos, sin)
    dx2, loss_local = loss_grad("loss_grad", x2, loss_target.reshape(n, D), 256)
    loss = lax.psum(loss_local, ("x", "y", "c"))

    def partials(gr):
        return [gr['w_in_blocks'], gr['w_out'].reshape(N_CHIPS, D // N_CHIPS, D),
                gr['b_pw_w'].reshape(N_CHIPS, cfg.BW // N_CHIPS, cfg.BW)]

    dx1, gr1, _ = layer_backward(cfg, 1, dx2, sv1, lp1, wp1, wo1, cos, sin)
    rs1 = GradReduce(1, partials(gr1), chip, c_idx)
    dx0, gr0, (got1, rs0) = layer_backward(cfg, 0, dx1, sv0, lp0, wp0, wo0, cos, sin, rs=rs1,
                                           own_rs=lambda gr: GradReduce(0, partials(gr), chip, c_idx))
    red1 = rs1.finish(got1)
    red0 = rs0.finish(run_comm("rs0_scatter_chips", rs0.scatter()))
    grad_x = dx0.reshape(x.shape)
    grads = [gr0, gr1]

    small_parts = [jnp.stack([grads[l][k] for l in range(DEPTH)]) for k in SMALL]
    small_red = _unpack(all_reduce_small("all_reduce_small", _pack(small_parts)), [a.shape for a in small_parts])
    g = {}
    for k, a in zip(SMALL, small_red):
        if k in CHIP_SHARDED_SMALL:
            ax = CHIP_SHARDED_SMALL[k]
            width = a.shape[ax] // N_CHIPS
            a = lax.dynamic_slice_in_dim(a, chip * width, width, axis=ax)
        g[k] = a

    delta, new_m, new_v = {}, {}, {}
    for k, g0, g1 in zip(BIG, red0, red1):
        update = adamw_layers if w[k].shape[-1] % LANE == 0 else adamw_cols_major
        g[k], delta[k], new_m[k], new_v[k] = update(f"adamw_{k}", w[k], g0, g1, m[k], v[k], 128)
    outs = adamw_many("adamw_small", *[[d[k] for k in SMALL] for d in (w, g, m, v)])
    for i, k in enumerate(SMALL):
        delta[k], new_m[k], new_v[k] = outs[i], outs[len(SMALL) + i], outs[2 * len(SMALL) + i]
    return (loss, grad_x, *[g[k] for k in WEIGHTS], *[delta[k] for k in WEIGHTS], *[new_m[k] for k in WEIGHTS],
            *[new_v[k] for k in WEIGHTS])
```

```python
import functools
import math

import numpy as np
import jax
import jax.numpy as jnp
from jax import lax
from jax.experimental import pallas as pl
from jax.experimental.pallas import tpu as pltpu

F32 = jnp.float32
BF16 = jnp.bfloat16
HI = lax.Precision.HIGHEST
MESH = pl.DeviceIdType.MESH

DEPTH = 2
A_HEAD = 64
A_GROUP = 3
ATTN_BLOCK = 128
ROT_DIM = 16
ROPE_THETA = 500000.0
B_CONV = 31
B_HALO = 32
C_HEAD = 128
C_CONV = 4
C_HALO = 8
CHUNK = 64
EPS = 1e-6
LANE = 128

ADAM_LR = 0.001
ADAM_B1 = 0.9
ADAM_B2 = 0.999
ADAM_EPS = 1e-08
ADAM_WD = 0.01
ADAM_STEP = 10

VMEM_LIMIT = 56 * 1024 * 1024

WEIGHTS = ['norm_w', 'w_in', 'q_norm_w', 'k_norm_w', 'sinks', 'b_conv_w', 'b_conv_b', 'b_ln_w', 'b_ln_b',
           'b_pw_w', 'b_pw_b', 'c_conv_w', 'c_a_log', 'c_dt_bias', 'c_onorm_w', 'w_out']


class Cfg:
    def __init__(self, d_model=2048, seq=2048):
        self.D = d_model
        self.T = seq
        self.AW = 3 * d_model // 8
        self.AQH = self.AW // A_HEAD
        self.AKH = self.AQH // A_GROUP
        self.AKW = self.AKH * A_HEAD
        self.BW = d_model // 4
        self.CH = (d_model - self.AW - self.BW) // C_HEAD
        self.CW = self.CH * C_HEAD
        AW, AKW, BW, CW, CH = self.AW, self.AKW, self.BW, self.CW, self.CH
        orig = [('qa', AW), ('ka', AKW), ('va', AKW), ('za', AW), ('ub', 2 * BW), ('zb', BW),
                ('qc', CW), ('kc', CW), ('vc', CW), ('bc', CH), ('ac', CH), ('zc', CW)]
        self.orig = {}
        off = 0
        for n, w in orig:
            self.orig[n] = (off, w)
            off += w
        self.IN_COLS = off
        order = ['qa', 'za', 'qc', 'kc', 'vc', 'zc', 'ka', 'va', 'ub', 'zb', 'bc', 'ac']
        self.order = order
        self.g = {}
        off = 0
        for n in order:
            w = self.orig[n][1]
            wp = LANE if n in ('bc', 'ac') else w
            assert off % wp == 0, (n, off, wp)
            self.g[n] = (off, wp)
            off += wp
        self.WP = off

    def blk(self, name):
        off, w = self.g[name]
        return off // w


def _cparams(sem, vmem=VMEM_LIMIT):
    return pltpu.CompilerParams(dimension_semantics=sem, vmem_limit_bytes=vmem)


def _silu(x):
    return x * jax.nn.sigmoid(x)


ANY = pl.BlockSpec(memory_space=pl.ANY)


class Comm:
    def __init__(self, ins, out_shapes, sems, start, finish, aliases=None):
        self.ins, self.out_shapes, self.sems = list(ins), list(out_shapes), list(sems)
        self.start, self.finish, self.aliases = start, finish, dict(aliases or {})


def call_with_comm(body, name, grid, in_specs, out_specs, out_shape, scratch_shapes, semantics, args, comm=None):
    in_specs, out_specs, out_shape, scratch_shapes = list(in_specs), list(out_specs), list(out_shape), list(scratch_shapes)
    if comm is None:
        outs = pl.pallas_call(body, name=name, grid=grid, in_specs=in_specs, out_specs=out_specs, out_shape=out_shape,
                              scratch_shapes=scratch_shapes, compiler_params=_cparams(semantics))(*args)
        return list(outs), []
    ni, no, ns = len(in_specs), len(out_specs), len(scratch_shapes)
    nci, nco = len(comm.ins), len(comm.out_shapes)

    def wrapped(*refs):
        h_in, c_in = refs[:ni], refs[ni:ni + nci]
        h_out, c_out = refs[ni + nci:ni + nci + no], refs[ni + nci + no:ni + nci + no + nco]
        h_scr, c_sems = refs[ni + nci + no + nco:ni + nci + no + nco + ns], refs[ni + nci + no + nco + ns:]
        ids = [pl.program_id(d) for d in range(len(grid))]
        first = functools.reduce(jnp.logical_and, [i == 0 for i in ids])
        last = functools.reduce(jnp.logical_and, [i == g - 1 for i, g in zip(ids, grid)])

        @pl.when(first)
        def _():
            comm.start(c_in, c_out, c_sems)

        body(*h_in, *h_out, *h_scr)

        @pl.when(last)
        def _():
            comm.finish(c_in, c_out, c_sems)

    outs = pl.pallas_call(
        wrapped, name=name, grid=grid, in_specs=in_specs + [ANY] * nci, out_specs=out_specs + [ANY] * nco,
        out_shape=out_shape + comm.out_shapes, scratch_shapes=scratch_shapes + comm.sems,
        input_output_aliases={ni + k: no + v for k, v in comm.aliases.items()},
        compiler_params=_cparams(("arbitrary",) * len(grid)),
    )(*args, *comm.ins)
    return list(outs[:no]), list(outs[no:])


def run_comm(name, comm):
    nci, nco = len(comm.ins), len(comm.out_shapes)

    def body(*refs):
        c_in, c_out, c_sems = refs[:nci], refs[nci:nci + nco], refs[nci + nco:]
        comm.start(c_in, c_out, c_sems)
        comm.finish(c_in, c_out, c_sems)

    return pl.pallas_call(
        body, name=name, in_specs=[ANY] * nci, out_specs=[ANY] * nco, out_shape=comm.out_shapes,
        scratch_shapes=comm.sems, input_output_aliases=comm.aliases,
    )(*comm.ins)


def _bdot(a, b, ca, cb, precision=HI):
    dims = (((ca,), (cb,)), ((0,), (0,)))
    if precision is HI and a.dtype == F32:
        ah = a.astype(BF16)
        bh = b.astype(BF16)
        al = (a - ah.astype(F32)).astype(BF16)
        bl = (b - bh.astype(F32)).astype(BF16)
        dg = lambda p, q: lax.dot_general(p, q, dims, preferred_element_type=F32)
        return dg(ah, bh) + (dg(ah, bl) + dg(al, bh))
    return lax.dot_general(a, b, dims, precision=precision, preferred_element_type=F32)


def _rope_matrix(nb):
    i = lax.broadcasted_iota(jnp.int32, (nb, A_HEAD, A_HEAD), 1)
    j = lax.broadcasted_iota(jnp.int32, (nb, A_HEAD, A_HEAD), 2)
    half = ROT_DIM // 2
    neg = (j < half) & (i == j + half)
    pos = (j >= half) & (j < ROT_DIM) & (i == j - half)
    return jnp.where(neg, -1.0, jnp.where(pos, 1.0, 0.0)).astype(F32)


def _norm_rope(xh, w, cos, sin):
    y = xh * lax.rsqrt(jnp.mean(xh * xh, axis=-1, keepdims=True) + EPS) * w
    return y * cos + _bdot(y, _rope_matrix(xh.shape[0]), 2, 1) * sin


def attn_block(cfg, first, q, za, kc, vc, cosc, sinc, kp, vp, cosp, sinp, qnw, knw, sinks_row):
    blk = ATTN_BLOCK
    nq, nk = cfg.AQH, cfg.AKH
    qi = lax.broadcasted_iota(jnp.int32, (blk, 2 * blk), 0)
    kj = lax.broadcasted_iota(jnp.int32, (blk, 2 * blk), 1)
    dist = qi + blk - kj
    valid = ((dist >= 0) & (dist < blk) & (jnp.logical_not(first) | (kj >= blk)))[None]
    cos2 = jnp.concatenate([cosp, cosc], axis=0)
    sin2 = jnp.concatenate([sinp, sinc], axis=0)
    head = lambda x, h: x[:, A_HEAD * h:A_HEAD * (h + 1)]
    k2 = jnp.stack([jnp.concatenate([head(kp, h), head(kc, h)], axis=0) for h in range(nk)], axis=0)
    v2 = jnp.stack([jnp.concatenate([head(vp, h), head(vc, h)], axis=0) for h in range(nk)], axis=0)
    k2 = _norm_rope(k2, knw[None], cos2[None], sin2[None]).astype(BF16)
    v2 = v2.astype(BF16)
    k2 = jnp.stack([k2[h // A_GROUP] for h in range(nq)], axis=0)
    v2 = jnp.stack([v2[h // A_GROUP] for h in range(nq)], axis=0)
    qh = jnp.stack([head(q, h) for h in range(nq)], axis=0)
    qh = _norm_rope(qh, qnw[None], cosc[None], sinc[None]).astype(BF16)
    s = _bdot(qh, k2, 2, 2, None) * (A_HEAD ** -0.5)
    s = jnp.where(valid, s, -1e30)
    sink = jnp.stack([sinks_row[:, A_HEAD * h:A_HEAD * h + 1] for h in range(nq)], axis=0)
    m = jnp.maximum(jnp.max(s, axis=-1, keepdims=True), sink)
    e = jnp.exp(s - m)
    den = jnp.sum(e, axis=-1, keepdims=True) + jnp.exp(sink - m)
    o = _bdot((e / den).astype(BF16), v2, 2, 1, None)
    return (jnp.concatenate([o[h] for h in range(nq)], axis=1) * _silu(za),)


def conv_block(cfg, first, u, zb, uh, cw, cb, lw, lb, pw, pb):
    BW = cfg.BW
    tb = u.shape[0]
    uu = jnp.concatenate([uh, u], axis=0)
    h = uu[:, :BW] * jax.nn.sigmoid(uu[:, BW:])
    row = lax.broadcasted_iota(jnp.int32, h.shape, 0)
    h = jnp.where(first & (row < B_HALO), 0.0, h)
    acc = jnp.zeros((tb, BW), F32) + cb
    base = B_HALO - (B_CONV - 1)
    for k in range(B_CONV):
        acc = acc + cw[k:k + 1, :] * h[base + k:base + k + tb, :]
    mu = jnp.mean(acc, axis=-1, keepdims=True)
    var = jnp.mean(jnp.square(acc - mu), axis=-1, keepdims=True)
    y = (acc - mu) * lax.rsqrt(var + EPS) * lw + lb
    s = _silu(y)
    o = jnp.dot(s.astype(BF16), pw.astype(BF16), preferred_element_type=F32) + pb
    return (o * _silu(zb),)


def gdn_prep_block(cfg, first, xq, xk, xv, braw, araw, hq, hk, hv, cw, alog, dtb):
    CW = cfg.CW
    tb = xq.shape[0]
    outs = []
    for idx, (x, xh) in enumerate(((xq, hq), (xk, hk), (xv, hv))):
        xx = jnp.concatenate([jnp.where(first, 0.0, xh), x], axis=0)
        w = cw[:, idx * CW:(idx + 1) * CW]
        acc = jnp.zeros((tb, CW), F32)
        base = C_HALO - (C_CONV - 1)
        for k in range(C_CONV):
            acc = acc + w[k:k + 1, :] * xx[base + k:base + k + tb, :]
        y = _silu(acc)
        if idx < 2:
            parts = []
            for h in range(cfg.CH):
                yh = y[:, C_HEAD * h:C_HEAD * (h + 1)]
                parts.append(yh * lax.rsqrt(jnp.sum(yh * yh, axis=-1, keepdims=True) + EPS))
            y = jnp.concatenate(parts, axis=1)
        outs.append(y)
    beta = jax.nn.sigmoid(braw)
    g = -jnp.exp(alog) * jax.nn.softplus(araw + dtb)
    return outs[0], outs[1], outs[2], g, beta


def _inverse_unit_lower(low, eye):
    pw = low
    inv = eye - low
    for _ in range(5):
        pwb = pw.astype(BF16)
        pw = _bdot(pwb, pwb, 2, 1, None)
        inv = inv + _bdot(inv.astype(BF16), pw.astype(BF16), 2, 1, None)
    ax = inv + _bdot(low, inv, 2, 1)
    return inv + _bdot(inv, eye - ax, 2, 1)


@jax.custom_vjp
def _saved_inverse(low, inv):
    return inv


def _saved_inverse_fwd(low, inv):
    return inv, inv


def _saved_inverse_bwd(inv, d):
    dlow = -_bdot(_bdot(inv, d, 1, 1), inv, 2, 2)
    return dlow, jnp.zeros_like(inv)


_saved_inverse.defvjp(_saved_inverse_fwd, _saved_inverse_bwd)


def gdn_intra_rows(cfg, first, qn, kn, v, g, beta, inv_saved=None):
    c = CHUNK
    CH = cfg.CH
    nchunk = qn.shape[0] // c
    i = lax.broadcasted_iota(jnp.int32, (c, c), 0)
    j = lax.broadcasted_iota(jnp.int32, (c, c), 1)
    incl = (i >= j)[None]
    strict = (i > j)[None]
    eye = (i == j).astype(F32)[None]
    tri = (i >= j).astype(F32)
    rows = [slice(c * ci, c * (ci + 1)) for ci in range(nchunk)]
    gcs = [jnp.dot(tri, g[r], precision=HI, preferred_element_type=F32) for r in rows]
    pairs = [(ci, h) for ci in range(nchunk) for h in range(CH)]
    heads = lambda x, wd: jnp.stack([x[rows[ci], wd * h:wd * (h + 1)] for ci, h in pairs], axis=0)
    gch = jnp.stack([gcs[ci][:, h:h + 1] for ci, h in pairs], axis=0)
    bh = jnp.stack([beta[rows[ci], h:h + 1] for ci, h in pairs], axis=0)
    q = heads(qn, C_HEAD) * (C_HEAD ** -0.5)
    k = heads(kn, C_HEAD)
    vv = heads(v, C_HEAD)
    a = jnp.broadcast_to(gch, (len(pairs), c, c))
    diff = jnp.where(incl, a - jnp.swapaxes(a, 1, 2), 0.0)
    decay = jnp.where(incl, jnp.exp(diff), 0.0)
    kb = k * bh
    low = jnp.where(strict, _bdot(kb, k, 2, 2) * decay, 0.0)
    if inv_saved is None:
        inv = _inverse_unit_lower(low, eye)
    else:
        inv = _saved_inverse(low, heads(inv_saved, c))
    eg = jnp.exp(gch)
    sol = _bdot(inv, jnp.concatenate([vv * bh, kb * eg], axis=2), 2, 1)
    intra = jnp.where(incl, _bdot(q, k, 2, 2) * decay, 0.0)
    qg = q * eg
    kd = k * jnp.exp(gch[:, c - 1:c, :] - gch)
    glast = jnp.concatenate([jnp.broadcast_to(gc[c - 1:c, :], gc.shape) for gc in gcs], axis=0)

    def unstack(x):
        return jnp.concatenate([jnp.concatenate([x[ci * CH + h] for h in range(CH)], axis=1) for ci in range(nchunk)],
                               axis=0)

    outs = (unstack(sol[:, :, :C_HEAD]), unstack(sol[:, :, C_HEAD:]), unstack(qg), unstack(kd), unstack(intra), glast)
    return outs + (unstack(inv),) if inv_saved is None else outs


def gdn_state_step(S, u, w, qg, kd, intra, glast):
    v_new = u - _bdot(w, S, 2, 1)
    o = _bdot(qg, S, 2, 1) + _bdot(intra, v_new, 2, 1)
    S_next = S * jnp.exp(glast) + _bdot(kd, v_new, 1, 1)
    return o, S_next


def gdn_out_block(cfg, first, o, zc, onw):
    parts = []
    for h in range(cfg.CH):
        sl = slice(C_HEAD * h, C_HEAD * (h + 1))
        oh = o[:, sl]
        y = oh * lax.rsqrt(jnp.mean(oh * oh, axis=-1, keepdims=True) + EPS) * onw
        parts.append(y * _silu(zc[:, sl]))
    return (jnp.concatenate(parts, axis=1),)


def rms_block(x, nw):
    return x * lax.rsqrt(jnp.mean(x * x, axis=-1, keepdims=True) + EPS) * nw


class Row:
    def __init__(self, arr, width, colblk=0, grad=None):
        self.arr, self.width, self.colblk, self.grad = arr, width, colblk, grad


class Halo:
    def __init__(self, arr, width, colblk, hr, tie=None):
        self.arr, self.width, self.colblk, self.hr, self.tie = arr, width, colblk, hr, tie


def _row_specs(tb, rows, halos, params, pos):
    specs = [pl.BlockSpec((tb, r.width), lambda i, cb=r.colblk: (pos(i), cb)) for r in rows]
    specs += [pl.BlockSpec((h.hr, h.width),
                           lambda i, cb=h.colblk, m=tb // h.hr: (jnp.maximum(pos(i) * m - 1, 0), cb))
              for h in halos]
    specs += [pl.BlockSpec(p.shape, lambda i: (0, 0)) for p in params]
    return specs


def rb_fwd(name, fn, n, tb, bps, rows, halos, params, outs, comm=None):
    nr, nh, npar = len(rows), len(halos), len(params)

    def body(*refs):
        ins = refs[:nr + nh + npar]
        o_refs = refs[nr + nh + npar:]
        first = (pl.program_id(0) % bps) == 0
        res = fn(first, *[r[...] for r in ins])
        for ref, val in zip(o_refs, res):
            ref[...] = val.astype(ref.dtype)

    res, carried = call_with_comm(
        body, name, (n // tb,), _row_specs(tb, rows, halos, params, lambda i: i),
        [pl.BlockSpec((tb, w), lambda i: (i, 0)) for w, _ in outs],
        [jax.ShapeDtypeStruct((n, w), dt) for w, dt in outs], [], ("parallel",),
        [r.arr for r in rows] + [h.arr for h in halos] + list(params), comm)
    return (res, carried) if comm is not None else res


def rb_bwd(name, fn, n, tb, bps, rows, halos, params, douts, param_grads, comm=None):
    nr, nh, npar, nd = len(rows), len(halos), len(params), len(douts)
    nblk = n // tb
    grow = [k for k, r in enumerate(rows) if r.grad is not None]
    ghalo = [k for k, h in enumerate(halos) if h.tie is not None]
    gpar = [k for k, f in enumerate(param_grads) if f]
    pos = lambda i: nblk - 1 - i

    def body(*refs):
        ins = refs[:nr + nh + npar]
        d_refs = refs[nr + nh + npar:nr + nh + npar + nd]
        rest = refs[nr + nh + npar + nd:]
        grow_refs = rest[:len(grow)]
        gpar_refs = rest[len(grow):len(grow) + len(gpar)]
        carry_refs = rest[len(grow) + len(gpar):]
        i = pl.program_id(0)
        first = (pos(i) % bps) == 0
        vals = [r[...] for r in ins]
        diff_idx = grow + [nr + k for k in ghalo] + [nr + nh + k for k in gpar]

        def f(*dargs):
            full = list(vals)
            for k, a in zip(diff_idx, dargs):
                full[k] = a
            return fn(first, *full)

        res, vjp = jax.vjp(f, *[vals[k] for k in diff_idx])
        grads = vjp(tuple(d[...].astype(r.dtype) for d, r in zip(d_refs, res)))
        g_rows = list(grads[:len(grow)])
        g_halos = grads[len(grow):len(grow) + len(ghalo)]
        g_pars = grads[len(grow) + len(ghalo):]

        @pl.when(i == 0)
        def _():
            for c in carry_refs:
                c[...] = jnp.zeros_like(c)
            for p in gpar_refs:
                p[...] = jnp.zeros_like(p)

        for k, ref in enumerate(grow_refs):
            ref[...] = g_rows[k].astype(ref.dtype)
        for ci, hk in enumerate(ghalo):
            h = halos[hk]
            k = grow.index(h.tie)
            tail = g_rows[k][tb - h.hr:, :] + carry_refs[ci][...]
            grow_refs[k][tb - h.hr:, :] = tail.astype(grow_refs[k].dtype)
            carry_refs[ci][...] = g_halos[ci]
        for ref, gp in zip(gpar_refs, g_pars):
            ref[...] += gp

    out_specs = [pl.BlockSpec((tb, rows[k].width), lambda i: (pos(i), 0)) for k in grow]
    out_specs += [pl.BlockSpec(params[k].shape, lambda i: (0, 0)) for k in gpar]
    out_shape = [jax.ShapeDtypeStruct((n, rows[k].width), rows[k].grad) for k in grow]
    out_shape += [jax.ShapeDtypeStruct(params[k].shape, F32) for k in gpar]
    in_specs = _row_specs(tb, rows, halos, params, pos)
    in_specs += [pl.BlockSpec((tb, d.shape[1]), lambda i: (pos(i), 0)) for d in douts]
    res, carried = call_with_comm(
        body, name, (nblk,), in_specs, out_specs, out_shape,
        [pltpu.VMEM((halos[k].hr, halos[k].width), F32) for k in ghalo], ("arbitrary",),
        [r.arr for r in rows] + [h.arr for h in halos] + list(params) + list(douts), comm)
    return (res, carried) if comm is not None else res


_DIMS = {'nn': (((1,), (0,)), ((), ())), 'nt': (((1,), (1,)), ((), ())), 'tn': (((0,), (0,)), ((), ()))}


def matmul(name, a, b, mode, tm, tn, tk, out_dtype=F32, add=None, comm=None):
    if mode == 'tn':
        K, M = a.shape
    else:
        M, K = a.shape
    N = b.shape[0] if mode == 'nt' else b.shape[1]
    tm, tn, tk = min(tm, M), min(tn, N), min(tk, K)
    assert M % tm == 0 and N % tn == 0 and K % tk == 0, (name, M, N, K, tm, tn, tk)
    nk = K // tk
    a_spec = pl.BlockSpec((tk, tm), lambda i, j, k: (k, i)) if mode == 'tn' else pl.BlockSpec((tm, tk), lambda i, j, k: (i, k))
    b_spec = pl.BlockSpec((tn, tk), lambda i, j, k: (j, k)) if mode == 'nt' else pl.BlockSpec((tk, tn), lambda i, j, k: (k, j))
    o_spec = pl.BlockSpec((tm, tn), lambda i, j, k: (i, j))
    has_add = add is not None

    def body(*refs):
        a_ref, b_ref = refs[0], refs[1]
        add_ref = refs[2] if has_add else None
        o_ref = refs[-1]
        k = pl.program_id(2)
        part = lax.dot_general(a_ref[...].astype(BF16), b_ref[...].astype(BF16), _DIMS[mode], preferred_element_type=F32)

        @pl.when(k == 0)
        def _():
            o_ref[...] = ((part + add_ref[...]) if has_add else part).astype(o_ref.dtype)

        if nk > 1:
            @pl.when(k > 0)
            def _():
                o_ref[...] += part

    assert nk == 1 or out_dtype == F32
    ins = [a, b] + ([add] if has_add else [])
    in_specs = [a_spec, b_spec] + ([o_spec] if has_add else [])
    outs, couts = call_with_comm(body, name, (M // tm, N // tn, nk), in_specs, [o_spec],
                                 [jax.ShapeDtypeStruct((M, N), out_dtype)], [], ("parallel", "parallel", "arbitrary"),
                                 ins, comm)
    return (outs[0], couts) if comm is not None else outs[0]


def grad_w_blocks(name, h, dpb, tm, tk):
    n, d = h.shape
    nb, _, s = dpb.shape
    tm, tk = min(tm, d), min(tk, n)
    assert d % tm == 0 and n % tk == 0
    nk = n // tk

    def body(h_ref, b_ref, o_ref):
        k = pl.program_id(2)
        part = lax.dot_general(h_ref[...], b_ref[0], _DIMS['tn'], preferred_element_type=F32)

        @pl.when(k == 0)
        def _():
            o_ref[0] = part

        if nk > 1:
            @pl.when(k > 0)
            def _():
                o_ref[0] += part

    return pl.pallas_call(
        body, name=name, grid=(nb, d // tm, nk),
        in_specs=[pl.BlockSpec((tk, tm), lambda j, i, k: (k, i)), pl.BlockSpec((1, tk, s), lambda j, i, k: (j, k, 0))],
        out_specs=pl.BlockSpec((1, tm, s), lambda j, i, k: (j, i, 0)),
        out_shape=jax.ShapeDtypeStruct((nb, d, s), F32),
        compiler_params=_cparams(("parallel", "parallel", "arbitrary")),
    )(h, dpb)


def norm_in_proj(name, x, nw, wp, tm, tn, comm=None):
    n, d = x.shape
    wpc = wp.shape[1]
    tm, tn = min(tm, n), min(tn, wpc)
    assert n % tm == 0 and wpc % tn == 0

    def body(x_ref, nw_ref, w_ref, p_ref, h_ref):
        @pl.when(pl.program_id(1) == 0)
        def _():
            h_ref[...] = rms_block(x_ref[...], nw_ref[...]).astype(BF16)

        p_ref[...] = jnp.dot(h_ref[...], w_ref[...], preferred_element_type=F32)

    outs, couts = call_with_comm(
        body, name, (n // tm, wpc // tn),
        [pl.BlockSpec((tm, d), lambda i, j: (i, 0)), pl.BlockSpec((1, d), lambda i, j: (0, 0)),
         pl.BlockSpec((d, tn), lambda i, j: (0, j))],
        [pl.BlockSpec((tm, tn), lambda i, j: (i, j)), pl.BlockSpec((tm, d), lambda i, j: (i, 0))],
        [jax.ShapeDtypeStruct((n, wpc), F32), jax.ShapeDtypeStruct((n, d), BF16)], [], ("parallel", "arbitrary"),
        [x, nw, wp], comm)
    return (outs[0], outs[1], couts) if comm is not None else (outs[0], outs[1])


def grad_x_norm(name, dp, wp, x, nw, dres, tm, tk, comm=None):
    n, d = x.shape
    wpc = dp.shape[1]
    tm, tk = min(tm, n), min(tk, wpc)
    assert n % tm == 0 and wpc % tk == 0
    nk = wpc // tk

    def body(dp_ref, wp_ref, x_ref, nw_ref, dres_ref, dx_ref, dnw_ref, acc):
        i, k = pl.program_id(0), pl.program_id(1)

        @pl.when((i == 0) & (k == 0))
        def _():
            dnw_ref[...] = jnp.zeros_like(dnw_ref)

        part = lax.dot_general(dp_ref[...], wp_ref[...], _DIMS['nt'], preferred_element_type=F32)

        @pl.when(k == 0)
        def _():
            acc[...] = part

        @pl.when(k > 0)
        def _():
            acc[...] += part

        @pl.when(k == nk - 1)
        def _():
            _, vjp = jax.vjp(rms_block, x_ref[...], nw_ref[...])
            dx, dnw = vjp(acc[...])
            dx_ref[...] = dx + dres_ref[...]
            dnw_ref[...] += dnw

    row = pl.BlockSpec((tm, d), lambda i, k: (i, 0))
    par = pl.BlockSpec((1, d), lambda i, k: (0, 0))
    outs, carried = call_with_comm(
        body, name, (n // tm, nk),
        [pl.BlockSpec((tm, tk), lambda i, k: (i, k)), pl.BlockSpec((d, tk), lambda i, k: (0, k)), row, par, row],
        [row, par], [jax.ShapeDtypeStruct((n, d), F32), jax.ShapeDtypeStruct((1, d), F32)],
        [pltpu.VMEM((tm, d), F32)], ("arbitrary", "arbitrary"), [dp, wp, x, nw, dres], comm)
    return outs[0], outs[1], carried


def out_proj_loss(name, y, wo, x, target, tm):
    n, d = x.shape
    tm = min(tm, n)
    assert n % tm == 0

    def body(y_ref, w_ref, x_ref, t_ref, dz_ref, loss_ref):
        @pl.when(pl.program_id(0) == 0)
        def _():
            loss_ref[...] = jnp.zeros_like(loss_ref)

        z = x_ref[...] + jnp.dot(y_ref[...], w_ref[...], preferred_element_type=F32)
        err = z - t_ref[...]
        dz_ref[...] = err * (1.0 / d)
        part = 0.5 * jnp.sum(jnp.mean(err * err, axis=-1, keepdims=True), axis=0, keepdims=True)
        loss_ref[...] += jnp.broadcast_to(part, loss_ref.shape)

    row = pl.BlockSpec((tm, d), lambda i: (i, 0))
    dz, loss = pl.pallas_call(
        body, name=name, grid=(n // tm,),
        in_specs=[pl.BlockSpec((tm, y.shape[1]), lambda i: (i, 0)), pl.BlockSpec(wo.shape, lambda i: (0, 0)), row, row],
        out_specs=[row, pl.BlockSpec((8, LANE), lambda i: (0, 0))],
        out_shape=[jax.ShapeDtypeStruct((n, d), F32), jax.ShapeDtypeStruct((8, LANE), F32)],
        compiler_params=_cparams(("arbitrary",)),
    )(y, wo, x, target)
    return dz, loss[0, 0]


def rope_tables(name, pos_col, inv_freq_row):
    n = pos_col.shape[0]

    def body(p_ref, f_ref, c_ref, s_ref):
        ang = p_ref[...].astype(F32) * f_ref[...]
        lane = lax.broadcasted_iota(jnp.int32, ang.shape, 1)
        c_ref[...] = jnp.where(lane < ROT_DIM, jnp.cos(ang), 1.0)
        s_ref[...] = jnp.where(lane < ROT_DIM, jnp.sin(ang), 0.0)

    return pl.pallas_call(
        body, name=name, out_shape=[jax.ShapeDtypeStruct((n, A_HEAD), F32)] * 2,
    )(pos_col, inv_freq_row)


def _scan_operands(cfg, nseq, u_ref, w_ref, qg_ref, kd_ref, a_ref, gl_ref):
    pairs = [(b, h) for b in range(nseq) for h in range(cfg.CH)]
    st = lambda r, wd: jnp.stack([r[b, :, wd * h:wd * (h + 1)] for b, h in pairs], axis=0)
    gl = jnp.stack([gl_ref[b, 0:1, h:h + 1] for b, h in pairs], axis=0)
    return st(u_ref, C_HEAD), st(w_ref, C_HEAD), st(qg_ref, C_HEAD), st(kd_ref, C_HEAD), st(a_ref, CHUNK), gl


def gdn_scan_fwd(name, cfg, nseq, u, w, qg, kd, intra, glast):
    CH, CW, T = cfg.CH, cfg.CW, cfg.T
    nc = T // CHUNK

    def body(u_ref, w_ref, qg_ref, kd_ref, a_ref, gl_ref, o_ref, sin_ref, s_ref):
        @pl.when(pl.program_id(0) == 0)
        def _():
            s_ref[...] = jnp.zeros_like(s_ref)

        S = s_ref[...]
        for b in range(nseq):
            sin_ref[b, 0] = S[b * CH:(b + 1) * CH]
        o, S_next = gdn_state_step(S, *_scan_operands(cfg, nseq, u_ref, w_ref, qg_ref, kd_ref, a_ref, gl_ref))
        s_ref[...] = S_next
        for b in range(nseq):
            o_ref[b] = jnp.concatenate([o[b * CH + h] for h in range(CH)], axis=1)

    row = lambda wd: pl.BlockSpec((nseq, CHUNK, wd), lambda c: (0, c, 0))
    widths = [CW, CW, CW, CW, CH * CHUNK, LANE]
    o, s_in = pl.pallas_call(
        body, name=name, grid=(nc,),
        in_specs=[row(x) for x in widths],
        out_specs=[row(CW), pl.BlockSpec((nseq, 1, CH, C_HEAD, C_HEAD), lambda c: (0, c, 0, 0, 0))],
        out_shape=[jax.ShapeDtypeStruct((nseq, T, CW), F32),
                   jax.ShapeDtypeStruct((nseq, nc, CH, C_HEAD, C_HEAD), F32)],
        scratch_shapes=[pltpu.VMEM((nseq * CH, C_HEAD, C_HEAD), F32)],
        compiler_params=_cparams(("arbitrary",)),
    )(*[a.reshape(nseq, T, a.shape[1]) for a in (u, w, qg, kd, intra, glast)])
    return o.reshape(nseq * T, CW), s_in


def gdn_scan_bwd(name, cfg, nseq, u, w, qg, kd, intra, glast, s_in, do, comm=None):
    CH, CW, T = cfg.CH, cfg.CW, cfg.T
    nc = T // CHUNK

    def body(u_ref, w_ref, qg_ref, kd_ref, a_ref, gl_ref, sin_ref, do_ref,
             du_ref, dw_ref, dqg_ref, dkd_ref, da_ref, dgl_ref, ds_ref):
        @pl.when(pl.program_id(0) == 0)
        def _():
            ds_ref[...] = jnp.zeros_like(ds_ref)

        S = jnp.concatenate([sin_ref[b, 0] for b in range(nseq)], axis=0)
        dout = jnp.stack([do_ref[b, :, C_HEAD * h:C_HEAD * (h + 1)] for b in range(nseq) for h in range(CH)], axis=0)
        _, vjp = jax.vjp(gdn_state_step, S, *_scan_operands(cfg, nseq, u_ref, w_ref, qg_ref, kd_ref, a_ref, gl_ref))
        dS, du, dw, dqg, dkd, da, dg = vjp((dout, ds_ref[...]))
        ds_ref[...] = dS
        lane = lax.broadcasted_iota(jnp.int32, (CHUNK, LANE), 1)
        rowi = lax.broadcasted_iota(jnp.int32, (CHUNK, LANE), 0)
        for b in range(nseq):
            cat = lambda x: jnp.concatenate([x[b * CH + h] for h in range(CH)], axis=1)
            du_ref[b] = cat(du)
            dw_ref[b] = cat(dw)
            dqg_ref[b] = cat(dqg)
            dkd_ref[b] = cat(dkd)
            da_ref[b] = cat(da)
            dgl = jnp.zeros((CHUNK, LANE), F32)
            for h in range(CH):
                dgl = dgl + jnp.where((lane == h) & (rowi == 0), dg[b * CH + h], 0.0)
            dgl_ref[b] = dgl

    row = lambda wd: pl.BlockSpec((nseq, CHUNK, wd), lambda c: (0, nc - 1 - c, 0))
    widths = [CW, CW, CW, CW, CH * CHUNK, LANE]
    outs, carried = call_with_comm(
        body, name, (nc,),
        [row(x) for x in widths]
        + [pl.BlockSpec((nseq, 1, CH, C_HEAD, C_HEAD), lambda c: (0, nc - 1 - c, 0, 0, 0)), row(CW)],
        [row(x) for x in widths], [jax.ShapeDtypeStruct((nseq, T, x), F32) for x in widths],
        [pltpu.VMEM((nseq * CH, C_HEAD, C_HEAD), F32)], ("arbitrary",),
        [a.reshape(nseq, T, a.shape[1]) for a in (u, w, qg, kd, intra, glast)] + [s_in, do.reshape(nseq, T, CW)], comm)
    return [o.reshape(nseq * T, o.shape[2]) for o in outs], carried


def _tile(total, cap, unit=LANE):
    best = None
    for t in range(unit, min(cap, total) + 1, unit):
        if total % t == 0:
            best = t
    assert best is not None, (total, cap, unit)
    return best


def _pad_lanes(v, width=LANE):
    return jnp.pad(v.reshape(1, -1), ((0, 0), (0, width - v.shape[-1])))


def permute_w_in(cfg, w):
    parts = []
    for n in cfg.order:
        off, wd = cfg.orig[n]
        blk = w[:, off:off + wd]
        if cfg.g[n][1] != wd:
            blk = jnp.pad(blk, ((0, 0), (0, cfg.g[n][1] - wd)))
        parts.append(blk)
    return jnp.concatenate(parts, axis=1)


def chip_blocks(cfg, groups, n_chips):
    s = cfg.IN_COLS // n_chips
    blocks = []
    for j in range(n_chips):
        lo, hi = j * s, (j + 1) * s
        pieces = []
        for name, (off, wd) in cfg.orig.items():
            a, b = max(lo, off), min(hi, off + wd)
            if a < b:
                pieces.append(groups[name][:, a - off:b - off])
        blocks.append(jnp.concatenate(pieces, axis=1))
    return jnp.stack(blocks)


def _layer_params(cfg, prm):
    return dict(
        nw=prm['norm_w'].reshape(1, -1),
        qnw=prm['q_norm_w'].reshape(1, -1), knw=prm['k_norm_w'].reshape(1, -1),
        sinks_row=jnp.repeat(prm['sinks'], A_HEAD).reshape(1, -1),
        cw=prm['b_conv_w'], cb=prm['b_conv_b'].reshape(1, -1),
        lw=prm['b_ln_w'].reshape(1, -1), lb=prm['b_ln_b'].reshape(1, -1),
        pw=prm['b_pw_w'], pb=prm['b_pw_b'].reshape(1, -1),
        ccw=prm['c_conv_w'], alog=_pad_lanes(prm['c_a_log']), dtb=_pad_lanes(prm['c_dt_bias']),
        onw=prm['c_onorm_w'].reshape(1, -1),
    )


def _attn_io(cfg, p, cos, sin, grads):
    gq = BF16 if grads else None
    rows = [Row(p, cfg.AW, cfg.blk('qa'), gq), Row(p, cfg.AW, cfg.blk('za'), gq),
            Row(p, cfg.AKW, cfg.blk('ka'), gq), Row(p, cfg.AKW, cfg.blk('va'), gq),
            Row(cos, A_HEAD), Row(sin, A_HEAD)]
    halos = [Halo(p, cfg.AKW, cfg.blk('ka'), ATTN_BLOCK, 2 if grads else None),
             Halo(p, cfg.AKW, cfg.blk('va'), ATTN_BLOCK, 3 if grads else None),
             Halo(cos, A_HEAD, 0, ATTN_BLOCK), Halo(sin, A_HEAD, 0, ATTN_BLOCK)]
    return rows, halos


def _conv_io(cfg, p, grads):
    gq = BF16 if grads else None
    rows = [Row(p, 2 * cfg.BW, cfg.blk('ub'), gq), Row(p, cfg.BW, cfg.blk('zb'), gq)]
    halos = [Halo(p, 2 * cfg.BW, cfg.blk('ub'), B_HALO, 0 if grads else None)]
    return rows, halos


def _prep_io(cfg, p, grads):
    gq = BF16 if grads else None
    rows = [Row(p, cfg.CW, cfg.blk(n), gq) for n in ('qc', 'kc', 'vc')]
    rows += [Row(p, LANE, cfg.blk('bc'), gq), Row(p, LANE, cfg.blk('ac'), gq)]
    halos = [Halo(p, cfg.CW, cfg.blk(n), C_HALO, k if grads else None) for k, n in enumerate(('qc', 'kc', 'vc'))]
    return rows, halos


TB_CONV = 128
TB_PREP = 256
TB_OUT = 256
TB_INTRA_FWD = 128
TB_INTRA_BWD = 128


def layer_forward(cfg, l, x, lp, wp, wo, cos, sin, comms=(None, None), target=None):
    n = x.shape[0]
    nseq = n // cfg.T
    T = cfg.T
    p, h, *carried = norm_in_proj(f"in_proj_{l}", x, lp['nw'], wp, 1024, _tile(cfg.WP, 768), comm=comms[0])
    rows, halos = _attn_io(cfg, p, cos, sin, False)
    oa = rb_fwd(f"attn_fwd_{l}", functools.partial(attn_block, cfg), n, ATTN_BLOCK, T // ATTN_BLOCK, rows, halos,
                [lp['qnw'], lp['knw'], lp['sinks_row']], [(cfg.AW, BF16)], comm=comms[1])
    if comms[1] is not None:
        oa, more = oa
        carried = [carried[0] + more] if carried else [more]
    (oa,) = oa
    rows, halos = _conv_io(cfg, p, False)
    tbb = min(TB_CONV, T)
    (ob,) = rb_fwd(f"conv_fwd_{l}", functools.partial(conv_block, cfg), n, tbb, T // tbb, rows, halos,
                   [lp['cw'], lp['cb'], lp['lw'], lp['lb'], lp['pw'], lp['pb']], [(cfg.BW, BF16)])
    rows, halos = _prep_io(cfg, p, False)
    tbp = min(TB_PREP, T)
    qn, kn, v, g, beta = rb_fwd(f"gdn_prep_fwd_{l}", functools.partial(gdn_prep_block, cfg), n, tbp, T // tbp, rows,
                                halos, [lp['ccw'], lp['alog'], lp['dtb']],
                                [(cfg.CW, F32)] * 3 + [(LANE, F32)] * 2)
    intra_outs = rb_fwd(f"gdn_intra_fwd_{l}", functools.partial(gdn_intra_rows, cfg), n, TB_INTRA_FWD, T // TB_INTRA_FWD,
                        [Row(qn, cfg.CW), Row(kn, cfg.CW), Row(v, cfg.CW), Row(g, LANE), Row(beta, LANE)], [], [],
                        [(cfg.CW, F32)] * 4 + [(cfg.CH * CHUNK, F32), (LANE, F32), (cfg.CH * CHUNK, F32)])
    intra_outs, inv = intra_outs[:6], intra_outs[6]
    o, s_in = gdn_scan_fwd(f"gdn_scan_fwd_{l}", cfg, nseq, *intra_outs)
    tbo = min(TB_OUT, T)
    (oc,) = rb_fwd(f"gdn_out_fwd_{l}", functools.partial(gdn_out_block, cfg), n, tbo, T // tbo,
                   [Row(o, cfg.CW), Row(p, cfg.CW, cfg.blk('zc'))], [], [lp['onw']], [(cfg.CW, BF16)])
    y = jnp.concatenate([oa, ob, oc], axis=1)
    if target is None:
        x_next = matmul(f"out_proj_{l}", y, wo, 'nn', 1024, 1024, cfg.D, add=x)
    else:
        x_next = out_proj_loss(f"out_proj_{l}", y, wo, x, target, 512)
    saved = dict(x=x, p=p, h=h, y=y, qn=qn, kn=kn, v=v, g=g, beta=beta, intra_outs=intra_outs, inv=inv, s_in=s_in, o=o)
    return x_next, saved, (carried[0] if carried else None)


def layer_backward(cfg, l, dxn, sv, lp, wp, wo, cos, sin, rs=None, own_rs=None):
    n = dxn.shape[0]
    nseq = n // cfg.T
    T = cfg.T
    p = sv['p']
    AW, BW, CW = cfg.AW, cfg.BW, cfg.CW
    dy = matmul(f"dy_{l}", dxn, wo, 'nt', 1024, 1024, cfg.D)
    dwo = matmul(f"dwo_{l}", sv['y'], dxn, 'tn', 1024, 1024, 2048)
    doa, dob, doc = dy[:, :AW], dy[:, AW:AW + BW], dy[:, AW + BW:]
    tbo = min(TB_OUT, T)
    do, dzc, donw = rb_bwd(f"gdn_out_bwd_{l}", functools.partial(gdn_out_block, cfg), n, tbo, T // tbo,
                           [Row(sv['o'], CW, 0, F32), Row(p, CW, cfg.blk('zc'), BF16)], [], [lp['onw']], [doc], [True])
    dintra, got_rest = gdn_scan_bwd(f"gdn_scan_bwd_{l}", cfg, nseq, *sv['intra_outs'], sv['s_in'], do,
                                    comm=None if rs is None else rs.scatter([1, 2]))
    dqn, dkn, dv, dg, dbeta = rb_bwd(
        f"gdn_intra_bwd_{l}", functools.partial(gdn_intra_rows, cfg), n, TB_INTRA_BWD, T // TB_INTRA_BWD,
        [Row(sv['qn'], CW, 0, F32), Row(sv['kn'], CW, 0, F32), Row(sv['v'], CW, 0, F32), Row(sv['g'], LANE, 0, F32),
         Row(sv['beta'], LANE, 0, F32), Row(sv['inv'], cfg.CH * CHUNK)], [], [], list(dintra), [])
    rows, halos = _prep_io(cfg, p, True)
    tbp = min(TB_PREP, T)
    dqc, dkc, dvc, dbc, dac, dccw, dalog, ddtb = rb_bwd(
        f"gdn_prep_bwd_{l}", functools.partial(gdn_prep_block, cfg), n, tbp, T // tbp, rows, halos,
        [lp['ccw'], lp['alog'], lp['dtb']], [dqn, dkn, dv, dg, dbeta], [True] * 3)
    rows, halos = _conv_io(cfg, p, True)
    tbb = min(TB_CONV, T)
    conv_grads = rb_bwd(
        f"conv_bwd_{l}", functools.partial(conv_block, cfg), n, tbb, T // tbb, rows, halos,
        [lp['cw'], lp['cb'], lp['lw'], lp['lb'], lp['pw'], lp['pb']], [dob], [True] * 6,
        comm=None if rs is None else rs.scatter([0]))
    got = None
    if rs is not None:
        conv_grads, got_w_in = conv_grads
        got = got_w_in + got_rest
    dub, dzb, dcw, dcb, dlw, dlb, dpw, dpb = conv_grads
    rows, halos = _attn_io(cfg, p, cos, sin, True)
    dqa, dza, dka, dva, dqnw, dknw, dsinks_row = rb_bwd(
        f"attn_bwd_{l}", functools.partial(attn_block, cfg), n, ATTN_BLOCK, T // ATTN_BLOCK, rows, halos,
        [lp['qnw'], lp['knw'], lp['sinks_row']], [doa], [True] * 3)
    dgroups = dict(qa=dqa, za=dza, qc=dqc, kc=dkc, vc=dvc, zc=dzc, ka=dka, va=dva, ub=dub, zb=dzb, bc=dbc, ac=dac)
    dp = jnp.concatenate([dgroups[k] for k in cfg.order], axis=1)
    dw_in = grad_w_blocks(f"dwp_{l}", sv['h'], chip_blocks(cfg, dgroups, N_CHIPS), 1024, 2048)
    mine = None if own_rs is None else own_rs(dict(w_in_blocks=dw_in, w_out=dwo, b_pw_w=dpw))
    dx, dnw, mine_received = grad_x_norm(f"dh_{l}", dp, wp, sv['x'], lp['nw'], dxn, 512, _tile(cfg.WP, 1152),
                                         comm=None if mine is None else mine.swap())
    if mine is not None:
        mine.add(mine_received)
    grads = dict(
        norm_w=dnw[0], w_in_blocks=dw_in, q_norm_w=dqnw[0], k_norm_w=dknw[0],
        sinks=dsinks_row.reshape(cfg.AQH, A_HEAD)[:, 0],
        b_conv_w=dcw, b_conv_b=dcb[0], b_ln_w=dlw[0], b_ln_b=dlb[0], b_pw_w=dpw, b_pw_b=dpb[0],
        c_conv_w=dccw, c_a_log=dalog[0, :cfg.CH], c_dt_bias=ddtb[0, :cfg.CH], c_onorm_w=donw[0], w_out=dwo)
    return dx, grads, (got, mine)


def rope_for(cfg, positions):
    n = positions.size
    inv_freq = ROPE_THETA ** (-np.arange(0, ROT_DIM, 2, dtype=np.float32) / ROT_DIM)
    freq_row = np.zeros((1, A_HEAD), np.float32)
    freq_row[0, :ROT_DIM] = np.concatenate([inv_freq, inv_freq])
    return rope_tables("rope_tables", positions.reshape(n, 1), jnp.asarray(freq_row))


def local_step(cfg, x, positions, prm, wps, wos, target):
    nseq = x.shape[0]
    n = nseq * cfg.T
    cos, sin = rope_for(cfg, positions)
    lps = [_layer_params(cfg, {k: v[l] for k, v in prm.items()}) for l in range(DEPTH)]
    saved = []
    xl = x.reshape(n, cfg.D)
    for l in range(DEPTH):
        xl, sv, _ = layer_forward(cfg, l, xl, lps[l], wps[l], wos[l], cos, sin,
                                  target=target.reshape(n, cfg.D) if l == DEPTH - 1 else None)
        saved.append(sv)
    dx, loss = xl
    grads = [None] * DEPTH
    for l in reversed(range(DEPTH)):
        dx, grads[l], _ = layer_backward(cfg, l, dx, saved[l], lps[l], wps[l], wos[l], cos, sin)
    return loss, dx.reshape(x.shape), grads


N_CHIPS = 4
N_DEV = 8


def _place():
    return lax.axis_index("x"), lax.axis_index("y"), lax.axis_index("c")


def _other_chips(x, y):
    return [(1 - x, y), (x, 1 - y), (1 - x, 1 - y)]


def _remote(src, dst, send, recv, to):
    return pltpu.make_async_remote_copy(src_ref=src, dst_ref=dst, send_sem=send, recv_sem=recv, device_id=to,
                                        device_id_type=MESH)


def gather_comm(arrs):
    n = len(arrs)

    def half(c):
        return [pl.ds(c * (a.shape[0] // 2), a.shape[0] // 2) for a in arrs]

    def first_copies(ins, outs, send, recv):
        x, y, c = _place()
        me = 2 * x + y
        mine = half(c)
        return [_remote(ins[i].at[mine[i]], outs[i].at[me, mine[i]], send.at[i, j], recv.at[i, j], (cx, cy, c))
                for i in range(n) for j, (cx, cy) in enumerate(_other_chips(x, y))]

    def start(ins, outs, sems):
        for cp in first_copies(ins, outs, *sems):
            cp.start()

    def finish(ins, outs, sems):
        send, recv = sems
        x, y, c = _place()
        chips = _other_chips(x, y)
        sib = (x, y, 1 - c)
        passed = []
        mine, other = half(c), half(1 - c)
        for i in range(n):
            for j, (cx, cy) in enumerate(chips):
                blk = outs[i].at[2 * cx + cy, mine[i]]
                _remote(blk, blk, send.at[i, j], recv.at[i, j], (x, y, c)).wait_recv()
                cp = _remote(blk, blk, send.at[i, 3 + j], recv.at[i, 3 + j], sib)
                cp.start()
                passed.append(cp)
        for i in range(n):
            for j, (cx, cy) in enumerate(chips):
                blk = outs[i].at[2 * cx + cy, other[i]]
                _remote(blk, blk, send.at[i, 3 + j], recv.at[i, 3 + j], sib).wait_recv()
        for cp in first_copies(ins, outs, send, recv) + passed:
            cp.wait_send()

    return Comm(arrs, [jax.ShapeDtypeStruct((N_CHIPS,) + a.shape, a.dtype) for a in arrs],
                [pltpu.SemaphoreType.DMA((n, 6)), pltpu.SemaphoreType.DMA((n, 6))], start, finish)


def fill_own(gathered, arrs):
    me = 2 * lax.axis_index("x") + lax.axis_index("y")
    return [lax.dynamic_update_index_in_dim(o, a, me, 0) for o, a in zip(gathered, arrs)]


def swap_comm(arrs):
    n = len(arrs)

    def copies(ins, outs, send, recv):
        x, y, c = _place()
        return [_remote(ins[i].at[:, 1 - c], outs[i], send.at[i], recv.at[i], (x, y, 1 - c)) for i in range(n)]

    def start(ins, outs, sems):
        for cp in copies(ins, outs, *sems):
            cp.start()

    def finish(ins, outs, sems):
        for cp in copies(ins, outs, *sems):
            cp.wait()

    return Comm(arrs, [jax.ShapeDtypeStruct((a.shape[0],) + a.shape[2:], a.dtype) for a in arrs],
                [pltpu.SemaphoreType.DMA((n,)), pltpu.SemaphoreType.DMA((n,))], start, finish)


def scatter_comm(arrs):
    n = len(arrs)

    def copies(ins, outs, send, recv):
        x, y, c = _place()
        return [_remote(ins[i].at[2 * cx + cy], outs[i].at[j], send.at[i, j], recv.at[i, j], (cx, cy, c))
                for i in range(n) for j, (cx, cy) in enumerate(_other_chips(x, y))]

    def start(ins, outs, sems):
        for cp in copies(ins, outs, *sems):
            cp.start()

    def finish(ins, outs, sems):
        send, recv = sems
        x, y, c = _place()
        for i in range(n):
            for j in range(3):
                blk = outs[i].at[j]
                _remote(blk, blk, send.at[i, j], recv.at[i, j], (x, y, c)).wait_recv()
        for cp in copies(ins, outs, send, recv):
            cp.wait_send()

    return Comm(arrs, [jax.ShapeDtypeStruct((3,) + a.shape[1:], a.dtype) for a in arrs],
                [pltpu.SemaphoreType.DMA((n, 3)), pltpu.SemaphoreType.DMA((n, 3))], start, finish)


def share_comm(arrs):
    n = len(arrs)

    def copies(outs, send, recv):
        x, y, c = _place()
        return [_remote(outs[i].at[c], outs[i].at[c], send.at[i], recv.at[i], (x, y, 1 - c)) for i in range(n)]

    def start(ins, outs, sems):
        for cp in copies(outs, *sems):
            cp.start()

    def finish(ins, outs, sems):
        send, recv = sems
        x, y, c = _place()
        for i in range(n):
            blk = outs[i].at[1 - c]
            _remote(blk, blk, send.at[i], recv.at[i], (x, y, c)).wait_recv()
        for cp in copies(outs, send, recv):
            cp.wait_send()

    return Comm(arrs, [jax.ShapeDtypeStruct(a.shape, a.dtype) for a in arrs],
                [pltpu.SemaphoreType.DMA((n,)), pltpu.SemaphoreType.DMA((n,))], start, finish,
                aliases={i: i for i in range(n)})


def all_reduce_small(name, packed):
    r = packed.shape[0]

    def body(in_ref, out_ref, buf, send, recv):
        x, y, c = _place()
        me = 4 * x + 2 * y + c
        buf[me] = in_ref[...]
        flips = [(fx, fy, fc) for fx in (0, 1) for fy in (0, 1) for fc in (0, 1) if (fx, fy, fc) != (0, 0, 0)]
        peers = [((x + fx) % 2, (y + fy) % 2, (c + fc) % 2) for fx, fy, fc in flips]
        cps = [_remote(in_ref, buf.at[me], send.at[k], recv.at[k], peer) for k, peer in enumerate(peers)]
        for cp in cps:
            cp.start()
        for k, (px, py, pc) in enumerate(peers):
            blk = buf.at[4 * px + 2 * py + pc]
            _remote(blk, blk, send.at[k], recv.at[k], (x, y, c)).wait_recv()
        for cp in cps:
            cp.wait_send()
        acc = buf[0]
        for d in range(1, N_DEV):
            acc = acc + buf[d]
        out_ref[...] = acc

    vm = pl.BlockSpec(memory_space=pltpu.VMEM)
    return pl.pallas_call(
        body, name=name, in_specs=[vm], out_specs=vm, out_shape=jax.ShapeDtypeStruct(packed.shape, F32),
        scratch_shapes=[pltpu.VMEM((N_DEV, r, LANE), F32), pltpu.SemaphoreType.DMA((N_DEV - 1,)),
                        pltpu.SemaphoreType.DMA((N_DEV - 1,))],
    )(packed)


def add_own_half(name, g, a, c_idx, tr):
    nch, _, r, cc = g.shape
    tr = min(tr, r)

    def body(c_ref, g_ref, a_ref, o_ref):
        o_ref[...] = (g_ref[0] + a_ref[...]).astype(o_ref.dtype)

    return pl.pallas_call(
        body, name=name,
        grid_spec=pltpu.PrefetchScalarGridSpec(
            num_scalar_prefetch=1, grid=(nch, r // tr),
            in_specs=[pl.BlockSpec((1, 1, tr, cc), lambda j, i, c_ref: (j, c_ref[0], i, 0)),
                      pl.BlockSpec((1, tr, cc), lambda j, i, c_ref: (j, i, 0))],
            out_specs=pl.BlockSpec((1, tr, cc), lambda j, i, c_ref: (j, i, 0))),
        out_shape=jax.ShapeDtypeStruct(a.shape, BF16),
        compiler_params=_cparams(("parallel", "parallel")),
    )(c_idx, g, a)


def sum_chips(name, p, b, idx, tr):
    _, r, cc = p.shape
    tr = min(tr, r)

    def body(idx_ref, p_ref, b_ref, o_ref):
        acc = p_ref[0].astype(F32)
        for k in range(3):
            acc = acc + b_ref[k].astype(F32)
        o_ref[0] = acc

    return pl.pallas_call(
        body, name=name,
        grid_spec=pltpu.PrefetchScalarGridSpec(
            num_scalar_prefetch=1, grid=(r // tr,),
            in_specs=[pl.BlockSpec((1, tr, cc), lambda i, s: (s[0], i, 0)),
                      pl.BlockSpec((3, tr, cc), lambda i, s: (0, i, 0))],
            out_specs=pl.BlockSpec((1, tr, cc), lambda i, s: (s[1], i, 0))),
        out_shape=jax.ShapeDtypeStruct((2, r, cc), F32),
        compiler_params=_cparams(("parallel",)),
    )(idx, p, b)


class GradReduce:
    def __init__(self, tag, parts, chip, c_idx):
        self.tag, self.c_idx = tag, c_idx
        self.parts = [p.reshape(p.shape[0], 2, p.shape[1] // 2, p.shape[2]) for p in parts]
        self.idx = jnp.concatenate([chip.astype(jnp.int32).reshape(1), c_idx])

    def swap(self):
        return swap_comm(self.parts)

    def add(self, received):
        self.part = [add_own_half(f"rs{self.tag}_add_sibling_{t}", g, a, self.c_idx, 128)
                     for t, (g, a) in enumerate(zip(self.parts, received))]

    def scatter(self, which=None):
        return scatter_comm(self.part if which is None else [self.part[t] for t in which])

    def finish(self, got):
        red = [sum_chips(f"rs{self.tag}_sum_chips_{t}", p, b, self.idx, 128) for t, (p, b) in enumerate(zip(self.part, got))]
        out = run_comm(f"rs{self.tag}_share_halves", share_comm(red))
        return [o.reshape(-1, o.shape[-1]) for o in out]


def adamw_many(name, ws, gs, ms, vs):
    n = len(ws)

    def body(*refs):
        for i in range(n):
            w_ref, g_ref, m_ref, v_ref = (refs[k * n + i] for k in range(4))
            d_ref, mo_ref, vo_ref = (refs[(4 + k) * n + i] for k in range(3))
            g = g_ref[...]
            m = ADAM_B1 * m_ref[...] + (1.0 - ADAM_B1) * g
            v = ADAM_B2 * v_ref[...] + (1.0 - ADAM_B2) * jnp.square(g)
            m_hat = m / (1.0 - ADAM_B1 ** ADAM_STEP)
            v_hat = v / (1.0 - ADAM_B2 ** ADAM_STEP)
            d_ref[...] = -ADAM_LR * (m_hat / (jnp.sqrt(v_hat) + ADAM_EPS) + ADAM_WD * w_ref[...])
            mo_ref[...] = m
            vo_ref[...] = v

    vm = pl.BlockSpec(memory_space=pltpu.VMEM)
    return pl.pallas_call(
        body, name=name, in_specs=[vm] * (4 * n), out_specs=[vm] * (3 * n),
        out_shape=[jax.ShapeDtypeStruct(a.shape, F32) for a in ws] * 3,
    )(*ws, *gs, *ms, *vs)


def adamw_layers(name, w, g0, g1, m, v, tb):
    _, r, cc = w.shape
    tb = min(tb, r)
    nb = r // tb

    def body(w_ref, g0_ref, g1_ref, m_ref, v_ref, g_ref, d_ref, mo_ref, vo_ref):
        g = jnp.where(pl.program_id(0) == 0, g0_ref[...], g1_ref[...])
        m = ADAM_B1 * m_ref[0] + (1.0 - ADAM_B1) * g
        v = ADAM_B2 * v_ref[0] + (1.0 - ADAM_B2) * jnp.square(g)
        m_hat = m / (1.0 - ADAM_B1 ** ADAM_STEP)
        v_hat = v / (1.0 - ADAM_B2 ** ADAM_STEP)
        g_ref[0] = g
        d_ref[0] = -ADAM_LR * (m_hat / (jnp.sqrt(v_hat) + ADAM_EPS) + ADAM_WD * w_ref[0])
        mo_ref[0] = m
        vo_ref[0] = v

    spec = pl.BlockSpec((1, tb, cc), lambda l, i: (l, i, 0))
    g0_spec = pl.BlockSpec((tb, cc), lambda l, i: (jnp.where(l == 0, i, nb - 1), 0))
    g1_spec = pl.BlockSpec((tb, cc), lambda l, i: (jnp.where(l == 1, i, 0), 0))
    return pl.pallas_call(
        body, name=name, grid=(2, nb), in_specs=[spec, g0_spec, g1_spec, spec, spec], out_specs=[spec] * 4,
        out_shape=[jax.ShapeDtypeStruct(w.shape, F32)] * 4,
        compiler_params=_cparams(("arbitrary", "arbitrary")),
    )(w, g0, g1, m, v)


def adamw_cols_major(name, w, g0, g1, m, v, tb=LANE):
    wt, mt, vt = (jnp.transpose(a, (2, 0, 1)) for a in (w, m, v))
    cc, _, r = wt.shape

    def body(w_ref, g0_ref, g1_ref, m_ref, v_ref, g_ref, d_ref, mo_ref, vo_ref):
        for l, gl_ref in enumerate((g0_ref, g1_ref)):
            g = gl_ref[...].T
            m = ADAM_B1 * m_ref[:, l, :] + (1.0 - ADAM_B1) * g
            v = ADAM_B2 * v_ref[:, l, :] + (1.0 - ADAM_B2) * jnp.square(g)
            m_hat = m / (1.0 - ADAM_B1 ** ADAM_STEP)
            v_hat = v / (1.0 - ADAM_B2 ** ADAM_STEP)
            g_ref[:, l, :] = g
            d_ref[:, l, :] = -ADAM_LR * (m_hat / (jnp.sqrt(v_hat) + ADAM_EPS) + ADAM_WD * w_ref[:, l, :])
            mo_ref[:, l, :] = m
            vo_ref[:, l, :] = v

    spec = pl.BlockSpec((tb, 2, r), lambda i: (i, 0, 0))
    gspec = pl.BlockSpec((r, tb), lambda i: (0, i))
    outs = pl.pallas_call(
        body, name=name, grid=(pl.cdiv(cc, tb),), in_specs=[spec, gspec, gspec, spec, spec], out_specs=[spec] * 4,
        out_shape=[jax.ShapeDtypeStruct(wt.shape, F32)] * 4,
        compiler_params=_cparams(("parallel",)),
    )(wt, g0, g1, mt, vt)
    return [jnp.transpose(o, (1, 2, 0)) for o in outs]


def _pack(arrs):
    flat = jnp.concatenate([a.reshape(-1).astype(F32) for a in arrs])
    pad = (-flat.shape[0]) % (8 * LANE)
    return jnp.pad(flat, (0, pad)).reshape(-1, LANE)


def _unpack(packed, shapes):
    flat = packed.reshape(-1)
    out, off = [], 0
    for s in shapes:
        size = math.prod(s)
        out.append(flat[off:off + size].reshape(s))
        off += size
    return out


BIG = ('w_in', 'w_out', 'b_pw_w')
SMALL = tuple(k for k in WEIGHTS if k not in BIG)
CHIP_SHARDED_SMALL = {'b_conv_w': 2, 'c_conv_w': 2}


def kernel(x, positions, norm_w, w_in, q_norm_w, k_norm_w, sinks, b_conv_w, b_conv_b, b_ln_w, b_ln_b, b_pw_w, b_pw_b, c_conv_w, c_a_log, c_dt_bias, c_onorm_w, w_out, loss_target, m_norm_w, m_w_in, m_q_norm_w, m_k_norm_w, m_sinks, m_b_conv_w, m_b_conv_b, m_b_ln_w, m_b_ln_b, m_b_pw_w, m_b_pw_b, m_c_conv_w, m_c_a_log, m_c_dt_bias, m_c_onorm_w, m_w_out, v_norm_w, v_w_in, v_q_norm_w, v_k_norm_w, v_sinks, v_b_conv_w, v_b_conv_b, v_b_ln_w, v_b_ln_b, v_b_pw_w, v_b_pw_b, v_c_conv_w, v_c_a_log, v_c_dt_bias, v_c_onorm_w, v_w_out):
    cfg = Cfg(x.shape[-1], x.shape[-2])
    w = dict(norm_w=norm_w, w_in=w_in, q_norm_w=q_norm_w, k_norm_w=k_norm_w, sinks=sinks, b_conv_w=b_conv_w,
             b_conv_b=b_conv_b, b_ln_w=b_ln_w, b_ln_b=b_ln_b, b_pw_w=b_pw_w, b_pw_b=b_pw_b, c_conv_w=c_conv_w,
             c_a_log=c_a_log, c_dt_bias=c_dt_bias, c_onorm_w=c_onorm_w, w_out=w_out)
    m = dict(norm_w=m_norm_w, w_in=m_w_in, q_norm_w=m_q_norm_w, k_norm_w=m_k_norm_w, sinks=m_sinks,
             b_conv_w=m_b_conv_w, b_conv_b=m_b_conv_b, b_ln_w=m_b_ln_w, b_ln_b=m_b_ln_b, b_pw_w=m_b_pw_w,
             b_pw_b=m_b_pw_b, c_conv_w=m_c_conv_w, c_a_log=m_c_a_log, c_dt_bias=m_c_dt_bias, c_onorm_w=m_c_onorm_w,
             w_out=m_w_out)
    v = dict(norm_w=v_norm_w, w_in=v_w_in, q_norm_w=v_q_norm_w, k_norm_w=v_k_norm_w, sinks=v_sinks,
             b_conv_w=v_b_conv_w, b_conv_b=v_b_conv_b, b_ln_w=v_b_ln_w, b_ln_b=v_b_ln_b, b_pw_w=v_b_pw_w,
             b_pw_b=v_b_pw_b, c_conv_w=v_c_conv_w, c_a_log=v_c_a_log, c_dt_bias=v_c_dt_bias, c_onorm_w=v_c_onorm_w,
             w_out=v_w_out)
    chip = 2 * lax.axis_index("x") + lax.axis_index("y")
    c_idx = lax.axis_index("c").astype(jnp.int32).reshape(1)
    D, T = cfg.D, cfg.T
    nseq = x.shape[0]
    n = nseq * T
    w_in_b, w_out_b = w_in.astype(BF16), w_out.astype(BF16)
    per_layer = lambda l: [w_in_b[l], w_out_b[l], b_pw_w[l]]

    def full_weights(g_in, g_out, g_pw):
        return permute_w_in(cfg, jnp.concatenate(list(g_in), axis=1)), g_out.reshape(D, D), g_pw.reshape(cfg.BW, cfg.BW)

    def layer_prm(l, pw_full):
        prm = {k: w[k][l] for k in SMALL}
        prm['b_pw_w'] = pw_full
        prm['b_conv_w'] = jnp.concatenate(list(g_bcw[:, l]), axis=1)
        prm['c_conv_w'] = jnp.concatenate(list(g_ccw[:, l]), axis=1)
        return _layer_params(cfg, prm)

    first = per_layer(0) + [b_conv_w, c_conv_w]
    g_in0, g_out0, g_pw0, g_bcw, g_ccw = fill_own(run_comm("gather_weights_0", gather_comm(first)), first)
    wp0, wo0, pw0 = full_weights(g_in0, g_out0, g_pw0)
    cos, sin = rope_for(cfg, positions)
    lp0 = layer_prm(0, pw0)
    second = per_layer(1)
    x1, sv0, gathered = layer_forward(cfg, 0, x.reshape(n, D), lp0, wp0, wo0, cos, sin,
                                      comms=(gather_comm(second[:1]), gather_comm(second[1:])))
    wp1, wo1, pw1 = full_weights(*fill_own(gathered, second))
    lp1 = layer_prm(1, pw1)
    (dx2, loss_local), sv1, _ = layer_forward(cfg, 1, x1, lp1, wp1, wo1, cos, sin, target=loss_target.reshape(n, D))
    loss = lax.psum(loss_local, ("x", "y", "c"))

    def partials(gr):
        return [gr['w_in_blocks'], gr['w_out'].reshape(N_CHIPS, D // N_CHIPS, D),
                gr['b_pw_w'].reshape(N_CHIPS, cfg.BW // N_CHIPS, cfg.BW)]

    dx1, gr1, (_, rs1) = layer_backward(cfg, 1, dx2, sv1, lp1, wp1, wo1, cos, sin,
                                        own_rs=lambda gr: GradReduce(1, partials(gr), chip, c_idx))
    dx0, gr0, (got1, rs0) = layer_backward(cfg, 0, dx1, sv0, lp0, wp0, wo0, cos, sin, rs=rs1,
                                           own_rs=lambda gr: GradReduce(0, partials(gr), chip, c_idx))
    red1 = rs1.finish(got1)
    red0 = rs0.finish(run_comm("rs0_scatter_chips", rs0.scatter()))
    grad_x = dx0.reshape(x.shape)
    grads = [gr0, gr1]

    small_parts = [jnp.stack([grads[l][k] for l in range(DEPTH)]) for k in SMALL]
    small_red = _unpack(all_reduce_small("all_reduce_small", _pack(small_parts)), [a.shape for a in small_parts])
    g = {}
    for k, a in zip(SMALL, small_red):
        if k in CHIP_SHARDED_SMALL:
            ax = CHIP_SHARDED_SMALL[k]
            width = a.shape[ax] // N_CHIPS
            a = lax.dynamic_slice_in_dim(a, chip * width, width, axis=ax)
        g[k] = a

    delta, new_m, new_v = {}, {}, {}
    for k, g0, g1 in zip(BIG, red0, red1):
        update = adamw_layers if w[k].shape[-1] % LANE == 0 else adamw_cols_major
        g[k], delta[k], new_m[k], new_v[k] = update(f"adamw_{k}", w[k], g0, g1, m[k], v[k], 128)
    outs = adamw_many("adamw_small", *[[d[k] for k in SMALL] for d in (w, g, m, v)])
    for i, k in enumerate(SMALL):
        delta[k], new_m[k], new_v[k] = outs[i], outs[len(SMALL) + i], outs[2 * len(SMALL) + i]
    return (loss, grad_x, *[g[k] for k in WEIGHTS], *[delta[k] for k in WEIGHTS], *[new_m[k] for k in WEIGHTS],
            *[new_v[k] for k in WEIGHTS])
```

```python
import functools
import math

import numpy as np
import jax
import jax.numpy as jnp
from jax import lax
from jax.experimental import pallas as pl
from jax.experimental.pallas import tpu as pltpu

F32 = jnp.float32
BF16 = jnp.bfloat16
HI = lax.Precision.HIGHEST
MESH = pl.DeviceIdType.MESH

DEPTH = 2
A_HEAD = 64
A_GROUP = 3
ATTN_BLOCK = 128
ROT_DIM = 16
ROPE_THETA = 500000.0
B_CONV = 31
B_HALO = 32
C_HEAD = 128
C_CONV = 4
C_HALO = 8
CHUNK = 64
EPS = 1e-6
LANE = 128

ADAM_LR = 0.001
ADAM_B1 = 0.9
ADAM_B2 = 0.999
ADAM_EPS = 1e-08
ADAM_WD = 0.01
ADAM_STEP = 10

VMEM_LIMIT = 56 * 1024 * 1024

WEIGHTS = ['norm_w', 'w_in', 'q_norm_w', 'k_norm_w', 'sinks', 'b_conv_w', 'b_conv_b', 'b_ln_w', 'b_ln_b',
           'b_pw_w', 'b_pw_b', 'c_conv_w', 'c_a_log', 'c_dt_bias', 'c_onorm_w', 'w_out']


class Cfg:
    def __init__(self, d_model=2048, seq=2048):
        self.D = d_model
        self.T = seq
        self.AW = 3 * d_model // 8
        self.AQH = self.AW // A_HEAD
        self.AKH = self.AQH // A_GROUP
        self.AKW = self.AKH * A_HEAD
        self.BW = d_model // 4
        self.CH = (d_model - self.AW - self.BW) // C_HEAD
        self.CW = self.CH * C_HEAD
        AW, AKW, BW, CW, CH = self.AW, self.AKW, self.BW, self.CW, self.CH
        orig = [('qa', AW), ('ka', AKW), ('va', AKW), ('za', AW), ('ub', 2 * BW), ('zb', BW),
                ('qc', CW), ('kc', CW), ('vc', CW), ('bc', CH), ('ac', CH), ('zc', CW)]
        self.orig = {}
        off = 0
        for n, w in orig:
            self.orig[n] = (off, w)
            off += w
        self.IN_COLS = off
        order = ['qa', 'za', 'qc', 'kc', 'vc', 'zc', 'ka', 'va', 'ub', 'zb', 'bc', 'ac']
        self.order = order
        self.g = {}
        off = 0
        for n in order:
            w = self.orig[n][1]
            wp = LANE if n in ('bc', 'ac') else w
            assert off % wp == 0, (n, off, wp)
            self.g[n] = (off, wp)
            off += wp
        self.WP = off

    def blk(self, name):
        off, w = self.g[name]
        return off // w


def _cparams(sem, vmem=VMEM_LIMIT):
    return pltpu.CompilerParams(dimension_semantics=sem, vmem_limit_bytes=vmem)


def _silu(x):
    return x * jax.nn.sigmoid(x)


ANY = pl.BlockSpec(memory_space=pl.ANY)


class Comm:
    def __init__(self, ins, out_shapes, sems, start, finish, aliases=None):
        self.ins, self.out_shapes, self.sems = list(ins), list(out_shapes), list(sems)
        self.start, self.finish, self.aliases = start, finish, dict(aliases or {})


def call_with_comm(body, name, grid, in_specs, out_specs, out_shape, scratch_shapes, semantics, args, comm=None):
    in_specs, out_specs, out_shape, scratch_shapes = list(in_specs), list(out_specs), list(out_shape), list(scratch_shapes)
    if comm is None:
        outs = pl.pallas_call(body, name=name, grid=grid, in_specs=in_specs, out_specs=out_specs, out_shape=out_shape,
                              scratch_shapes=scratch_shapes, compiler_params=_cparams(semantics))(*args)
        return list(outs), []
    ni, no, ns = len(in_specs), len(out_specs), len(scratch_shapes)
    nci, nco = len(comm.ins), len(comm.out_shapes)

    def wrapped(*refs):
        h_in, c_in = refs[:ni], refs[ni:ni + nci]
        h_out, c_out = refs[ni + nci:ni + nci + no], refs[ni + nci + no:ni + nci + no + nco]
        h_scr, c_sems = refs[ni + nci + no + nco:ni + nci + no + nco + ns], refs[ni + nci + no + nco + ns:]
        ids = [pl.program_id(d) for d in range(len(grid))]
        first = functools.reduce(jnp.logical_and, [i == 0 for i in ids])
        last = functools.reduce(jnp.logical_and, [i == g - 1 for i, g in zip(ids, grid)])

        @pl.when(first)
        def _():
            comm.start(c_in, c_out, c_sems)

        body(*h_in, *h_out, *h_scr)

        @pl.when(last)
        def _():
            comm.finish(c_in, c_out, c_sems)

    outs = pl.pallas_call(
        wrapped, name=name, grid=grid, in_specs=in_specs + [ANY] * nci, out_specs=out_specs + [ANY] * nco,
        out_shape=out_shape + comm.out_shapes, scratch_shapes=scratch_shapes + comm.sems,
        input_output_aliases={ni + k: no + v for k, v in comm.aliases.items()},
        compiler_params=_cparams(("arbitrary",) * len(grid)),
    )(*args, *comm.ins)
    return list(outs[:no]), list(outs[no:])


def run_comm(name, comm):
    nci, nco = len(comm.ins), len(comm.out_shapes)

    def body(*refs):
        c_in, c_out, c_sems = refs[:nci], refs[nci:nci + nco], refs[nci + nco:]
        comm.start(c_in, c_out, c_sems)
        comm.finish(c_in, c_out, c_sems)

    return pl.pallas_call(
        body, name=name, in_specs=[ANY] * nci, out_specs=[ANY] * nco, out_shape=comm.out_shapes,
        scratch_shapes=comm.sems, input_output_aliases=comm.aliases,
    )(*comm.ins)


def _bdot(a, b, ca, cb, precision=HI):
    dims = (((ca,), (cb,)), ((0,), (0,)))
    if precision is HI and a.dtype == F32:
        ah = a.astype(BF16)
        bh = b.astype(BF16)
        al = (a - ah.astype(F32)).astype(BF16)
        bl = (b - bh.astype(F32)).astype(BF16)
        dg = lambda p, q: lax.dot_general(p, q, dims, preferred_element_type=F32)
        return dg(ah, bh) + (dg(ah, bl) + dg(al, bh))
    return lax.dot_general(a, b, dims, precision=precision, preferred_element_type=F32)


def _rope_matrix(nb):
    i = lax.broadcasted_iota(jnp.int32, (nb, A_HEAD, A_HEAD), 1)
    j = lax.broadcasted_iota(jnp.int32, (nb, A_HEAD, A_HEAD), 2)
    half = ROT_DIM // 2
    neg = (j < half) & (i == j + half)
    pos = (j >= half) & (j < ROT_DIM) & (i == j - half)
    return jnp.where(neg, -1.0, jnp.where(pos, 1.0, 0.0)).astype(F32)


def _norm_rope(xh, w, cos, sin):
    y = xh * lax.rsqrt(jnp.mean(xh * xh, axis=-1, keepdims=True) + EPS) * w
    return y * cos + _bdot(y, _rope_matrix(xh.shape[0]), 2, 1) * sin


def attn_block(cfg, first, q, za, kc, vc, cosc, sinc, kp, vp, cosp, sinp, qnw, knw, sinks_row):
    blk = ATTN_BLOCK
    nq, nk = cfg.AQH, cfg.AKH
    qi = lax.broadcasted_iota(jnp.int32, (blk, 2 * blk), 0)
    kj = lax.broadcasted_iota(jnp.int32, (blk, 2 * blk), 1)
    dist = qi + blk - kj
    valid = ((dist >= 0) & (dist < blk) & (jnp.logical_not(first) | (kj >= blk)))[None]
    cos2 = jnp.concatenate([cosp, cosc], axis=0)
    sin2 = jnp.concatenate([sinp, sinc], axis=0)
    head = lambda x, h: x[:, A_HEAD * h:A_HEAD * (h + 1)]
    k2 = jnp.stack([jnp.concatenate([head(kp, h), head(kc, h)], axis=0) for h in range(nk)], axis=0)
    v2 = jnp.stack([jnp.concatenate([head(vp, h), head(vc, h)], axis=0) for h in range(nk)], axis=0)
    k2 = _norm_rope(k2, knw[None], cos2[None], sin2[None]).astype(BF16)
    v2 = v2.astype(BF16)
    k2 = jnp.stack([k2[h // A_GROUP] for h in range(nq)], axis=0)
    v2 = jnp.stack([v2[h // A_GROUP] for h in range(nq)], axis=0)
    qh = jnp.stack([head(q, h) for h in range(nq)], axis=0)
    qh = _norm_rope(qh, qnw[None], cosc[None], sinc[None]).astype(BF16)
    s = _bdot(qh, k2, 2, 2, None) * (A_HEAD ** -0.5)
    s = jnp.where(valid, s, -1e30)
    sink = jnp.stack([sinks_row[:, A_HEAD * h:A_HEAD * h + 1] for h in range(nq)], axis=0)
    m = jnp.maximum(jnp.max(s, axis=-1, keepdims=True), sink)
    e = jnp.exp(s - m)
    den = jnp.sum(e, axis=-1, keepdims=True) + jnp.exp(sink - m)
    o = _bdot((e / den).astype(BF16), v2, 2, 1, None)
    return (jnp.concatenate([o[h] for h in range(nq)], axis=1) * _silu(za),)


def conv_block(cfg, first, u, zb, uh, cw, cb, lw, lb, pw, pb):
    BW = cfg.BW
    tb = u.shape[0]
    uu = jnp.concatenate([uh, u], axis=0)
    h = uu[:, :BW] * jax.nn.sigmoid(uu[:, BW:])
    row = lax.broadcasted_iota(jnp.int32, h.shape, 0)
    h = jnp.where(first & (row < B_HALO), 0.0, h)
    acc = jnp.zeros((tb, BW), F32) + cb
    base = B_HALO - (B_CONV - 1)
    for k in range(B_CONV):
        acc = acc + cw[k:k + 1, :] * h[base + k:base + k + tb, :]
    mu = jnp.mean(acc, axis=-1, keepdims=True)
    var = jnp.mean(jnp.square(acc - mu), axis=-1, keepdims=True)
    y = (acc - mu) * lax.rsqrt(var + EPS) * lw + lb
    s = _silu(y)
    o = jnp.dot(s.astype(BF16), pw.astype(BF16), preferred_element_type=F32) + pb
    return (o * _silu(zb),)


def gdn_prep_block(cfg, first, xq, xk, xv, braw, araw, hq, hk, hv, cw, alog, dtb):
    CW = cfg.CW
    tb = xq.shape[0]
    outs = []
    for idx, (x, xh) in enumerate(((xq, hq), (xk, hk), (xv, hv))):
        xx = jnp.concatenate([jnp.where(first, 0.0, xh), x], axis=0)
        w = cw[:, idx * CW:(idx + 1) * CW]
        acc = jnp.zeros((tb, CW), F32)
        base = C_HALO - (C_CONV - 1)
        for k in range(C_CONV):
            acc = acc + w[k:k + 1, :] * xx[base + k:base + k + tb, :]
        y = _silu(acc)
        if idx < 2:
            parts = []
            for h in range(cfg.CH):
                yh = y[:, C_HEAD * h:C_HEAD * (h + 1)]
                parts.append(yh * lax.rsqrt(jnp.sum(yh * yh, axis=-1, keepdims=True) + EPS))
            y = jnp.concatenate(parts, axis=1)
        outs.append(y)
    beta = jax.nn.sigmoid(braw)
    g = -jnp.exp(alog) * jax.nn.softplus(araw + dtb)
    return outs[0], outs[1], outs[2], g, beta


def _inverse_unit_lower(low, eye):
    pw = low
    inv = eye - low
    for _ in range(5):
        pwb = pw.astype(BF16)
        pw = _bdot(pwb, pwb, 2, 1, None)
        inv = inv + _bdot(inv.astype(BF16), pw.astype(BF16), 2, 1, None)
    ax = inv + _bdot(low, inv, 2, 1)
    return inv + _bdot(inv, eye - ax, 2, 1)


@jax.custom_vjp
def _saved_inverse(low, inv):
    return inv


def _saved_inverse_fwd(low, inv):
    return inv, inv


def _saved_inverse_bwd(inv, d):
    dlow = -_bdot(_bdot(inv, d, 1, 1), inv, 2, 2)
    return dlow, jnp.zeros_like(inv)


_saved_inverse.defvjp(_saved_inverse_fwd, _saved_inverse_bwd)


def gdn_intra_rows(cfg, first, qn, kn, v, g, beta, inv_saved=None):
    c = CHUNK
    CH = cfg.CH
    nchunk = qn.shape[0] // c
    i = lax.broadcasted_iota(jnp.int32, (c, c), 0)
    j = lax.broadcasted_iota(jnp.int32, (c, c), 1)
    incl = (i >= j)[None]
    strict = (i > j)[None]
    eye = (i == j).astype(F32)[None]
    tri = (i >= j).astype(F32)
    rows = [slice(c * ci, c * (ci + 1)) for ci in range(nchunk)]
    gcs = [jnp.dot(tri, g[r], precision=HI, preferred_element_type=F32) for r in rows]
    pairs = [(ci, h) for ci in range(nchunk) for h in range(CH)]
    heads = lambda x, wd: jnp.stack([x[rows[ci], wd * h:wd * (h + 1)] for ci, h in pairs], axis=0)
    gch = jnp.stack([gcs[ci][:, h:h + 1] for ci, h in pairs], axis=0)
    bh = jnp.stack([beta[rows[ci], h:h + 1] for ci, h in pairs], axis=0)
    q = heads(qn, C_HEAD) * (C_HEAD ** -0.5)
    k = heads(kn, C_HEAD)
    vv = heads(v, C_HEAD)
    a = jnp.broadcast_to(gch, (len(pairs), c, c))
    diff = jnp.where(incl, a - jnp.swapaxes(a, 1, 2), 0.0)
    decay = jnp.where(incl, jnp.exp(diff), 0.0)
    kb = k * bh
    low = jnp.where(strict, _bdot(kb, k, 2, 2) * decay, 0.0)
    if inv_saved is None:
        inv = _inverse_unit_lower(low, eye)
    else:
        inv = _saved_inverse(low, heads(inv_saved, c))
    eg = jnp.exp(gch)
    sol = _bdot(inv, jnp.concatenate([vv * bh, kb * eg], axis=2), 2, 1)
    intra = jnp.where(incl, _bdot(q, k, 2, 2) * decay, 0.0)
    qg = q * eg
    kd = k * jnp.exp(gch[:, c - 1:c, :] - gch)
    glast = jnp.concatenate([jnp.broadcast_to(gc[c - 1:c, :], gc.shape) for gc in gcs], axis=0)

    def unstack(x):
        return jnp.concatenate([jnp.concatenate([x[ci * CH + h] for h in range(CH)], axis=1) for ci in range(nchunk)],
                               axis=0)

    outs = (unstack(sol[:, :, :C_HEAD]), unstack(sol[:, :, C_HEAD:]), unstack(qg), unstack(kd), unstack(intra), glast)
    return outs + (unstack(inv),) if inv_saved is None else outs


def gdn_state_step(S, u, w, qg, kd, intra, glast):
    v_new = u - _bdot(w, S, 2, 1)
    o = _bdot(qg, S, 2, 1) + _bdot(intra, v_new, 2, 1)
    S_next = S * jnp.exp(glast) + _bdot(kd, v_new, 1, 1)
    return o, S_next


def gdn_out_block(cfg, first, o, zc, onw):
    parts = []
    for h in range(cfg.CH):
        sl = slice(C_HEAD * h, C_HEAD * (h + 1))
        oh = o[:, sl]
        y = oh * lax.rsqrt(jnp.mean(oh * oh, axis=-1, keepdims=True) + EPS) * onw
        parts.append(y * _silu(zc[:, sl]))
    return (jnp.concatenate(parts, axis=1),)


def rms_block(x, nw):
    return x * lax.rsqrt(jnp.mean(x * x, axis=-1, keepdims=True) + EPS) * nw


class Row:
    def __init__(self, arr, width, colblk=0, grad=None):
        self.arr, self.width, self.colblk, self.grad = arr, width, colblk, grad


class Halo:
    def __init__(self, arr, width, colblk, hr, tie=None):
        self.arr, self.width, self.colblk, self.hr, self.tie = arr, width, colblk, hr, tie


def _row_specs(tb, rows, halos, params, pos):
    specs = [pl.BlockSpec((tb, r.width), lambda i, cb=r.colblk: (pos(i), cb)) for r in rows]
    specs += [pl.BlockSpec((h.hr, h.width),
                           lambda i, cb=h.colblk, m=tb // h.hr: (jnp.maximum(pos(i) * m - 1, 0), cb))
              for h in halos]
    specs += [pl.BlockSpec(p.shape, lambda i: (0, 0)) for p in params]
    return specs


def rb_fwd(name, fn, n, tb, bps, rows, halos, params, outs, comm=None):
    nr, nh, npar = len(rows), len(halos), len(params)

    def body(*refs):
        ins = refs[:nr + nh + npar]
        o_refs = refs[nr + nh + npar:]
        first = (pl.program_id(0) % bps) == 0
        res = fn(first, *[r[...] for r in ins])
        for ref, val in zip(o_refs, res):
            ref[...] = val.astype(ref.dtype)

    res, carried = call_with_comm(
        body, name, (n // tb,), _row_specs(tb, rows, halos, params, lambda i: i),
        [pl.BlockSpec((tb, w), lambda i: (i, 0)) for w, _ in outs],
        [jax.ShapeDtypeStruct((n, w), dt) for w, dt in outs], [], ("parallel",),
        [r.arr for r in rows] + [h.arr for h in halos] + list(params), comm)
    return (res, carried) if comm is not None else res


def rb_bwd(name, fn, n, tb, bps, rows, halos, params, douts, param_grads, comm=None):
    nr, nh, npar, nd = len(rows), len(halos), len(params), len(douts)
    nblk = n // tb
    grow = [k for k, r in enumerate(rows) if r.grad is not None]
    ghalo = [k for k, h in enumerate(halos) if h.tie is not None]
    gpar = [k for k, f in enumerate(param_grads) if f]
    pos = lambda i: nblk - 1 - i

    def body(*refs):
        ins = refs[:nr + nh + npar]
        d_refs = refs[nr + nh + npar:nr + nh + npar + nd]
        rest = refs[nr + nh + npar + nd:]
        grow_refs = rest[:len(grow)]
        gpar_refs = rest[len(grow):len(grow) + len(gpar)]
        carry_refs = rest[len(grow) + len(gpar):]
        i = pl.program_id(0)
        first = (pos(i) % bps) == 0
        vals = [r[...] for r in ins]
        diff_idx = grow + [nr + k for k in ghalo] + [nr + nh + k for k in gpar]

        def f(*dargs):
            full = list(vals)
            for k, a in zip(diff_idx, dargs):
                full[k] = a
            return fn(first, *full)

        res, vjp = jax.vjp(f, *[vals[k] for k in diff_idx])
        grads = vjp(tuple(d[...].astype(r.dtype) for d, r in zip(d_refs, res)))
        g_rows = list(grads[:len(grow)])
        g_halos = grads[len(grow):len(grow) + len(ghalo)]
        g_pars = grads[len(grow) + len(ghalo):]

        @pl.when(i == 0)
        def _():
            for c in carry_refs:
                c[...] = jnp.zeros_like(c)
            for p in gpar_refs:
                p[...] = jnp.zeros_like(p)

        for k, ref in enumerate(grow_refs):
            ref[...] = g_rows[k].astype(ref.dtype)
        for ci, hk in enumerate(ghalo):
            h = halos[hk]
            k = grow.index(h.tie)
            tail = g_rows[k][tb - h.hr:, :] + carry_refs[ci][...]
            grow_refs[k][tb - h.hr:, :] = tail.astype(grow_refs[k].dtype)
            carry_refs[ci][...] = g_halos[ci]
        for ref, gp in zip(gpar_refs, g_pars):
            ref[...] += gp

    out_specs = [pl.BlockSpec((tb, rows[k].width), lambda i: (pos(i), 0)) for k in grow]
    out_specs += [pl.BlockSpec(params[k].shape, lambda i: (0, 0)) for k in gpar]
    out_shape = [jax.ShapeDtypeStruct((n, rows[k].width), rows[k].grad) for k in grow]
    out_shape += [jax.ShapeDtypeStruct(params[k].shape, F32) for k in gpar]
    in_specs = _row_specs(tb, rows, halos, params, pos)
    in_specs += [pl.BlockSpec((tb, d.shape[1]), lambda i: (pos(i), 0)) for d in douts]
    res, carried = call_with_comm(
        body, name, (nblk,), in_specs, out_specs, out_shape,
        [pltpu.VMEM((halos[k].hr, halos[k].width), F32) for k in ghalo], ("arbitrary",),
        [r.arr for r in rows] + [h.arr for h in halos] + list(params) + list(douts), comm)
    return (res, carried) if comm is not None else res


_DIMS = {'nn': (((1,), (0,)), ((), ())), 'nt': (((1,), (1,)), ((), ())), 'tn': (((0,), (0,)), ((), ()))}


def matmul(name, a, b, mode, tm, tn, tk, out_dtype=F32, add=None, comm=None):
    if mode == 'tn':
        K, M = a.shape
    else:
        M, K = a.shape
    N = b.shape[0] if mode == 'nt' else b.shape[1]
    tm, tn, tk = min(tm, M), min(tn, N), min(tk, K)
    assert M % tm == 0 and N % tn == 0 and K % tk == 0, (name, M, N, K, tm, tn, tk)
    nk = K // tk
    a_spec = pl.BlockSpec((tk, tm), lambda i, j, k: (k, i)) if mode == 'tn' else pl.BlockSpec((tm, tk), lambda i, j, k: (i, k))
    b_spec = pl.BlockSpec((tn, tk), lambda i, j, k: (j, k)) if mode == 'nt' else pl.BlockSpec((tk, tn), lambda i, j, k: (k, j))
    o_spec = pl.BlockSpec((tm, tn), lambda i, j, k: (i, j))
    has_add = add is not None

    def body(*refs):
        a_ref, b_ref = refs[0], refs[1]
        add_ref = refs[2] if has_add else None
        o_ref = refs[-1]
        k = pl.program_id(2)
        part = lax.dot_general(a_ref[...].astype(BF16), b_ref[...].astype(BF16), _DIMS[mode], preferred_element_type=F32)

        @pl.when(k == 0)
        def _():
            o_ref[...] = ((part + add_ref[...]) if has_add else part).astype(o_ref.dtype)

        if nk > 1:
            @pl.when(k > 0)
            def _():
                o_ref[...] += part

    assert nk == 1 or out_dtype == F32
    ins = [a, b] + ([add] if has_add else [])
    in_specs = [a_spec, b_spec] + ([o_spec] if has_add else [])
    outs, couts = call_with_comm(body, name, (M // tm, N // tn, nk), in_specs, [o_spec],
                                 [jax.ShapeDtypeStruct((M, N), out_dtype)], [], ("parallel", "parallel", "arbitrary"),
                                 ins, comm)
    return (outs[0], couts) if comm is not None else outs[0]


def grad_w_blocks(name, h, dpb, tm, tk):
    n, d = h.shape
    nb, _, s = dpb.shape
    tm, tk = min(tm, d), min(tk, n)
    assert d % tm == 0 and n % tk == 0
    nk = n // tk

    def body(h_ref, b_ref, o_ref):
        k = pl.program_id(2)
        part = lax.dot_general(h_ref[...], b_ref[0], _DIMS['tn'], preferred_element_type=F32)

        @pl.when(k == 0)
        def _():
            o_ref[0] = part

        if nk > 1:
            @pl.when(k > 0)
            def _():
                o_ref[0] += part

    return pl.pallas_call(
        body, name=name, grid=(nb, d // tm, nk),
        in_specs=[pl.BlockSpec((tk, tm), lambda j, i, k: (k, i)), pl.BlockSpec((1, tk, s), lambda j, i, k: (j, k, 0))],
        out_specs=pl.BlockSpec((1, tm, s), lambda j, i, k: (j, i, 0)),
        out_shape=jax.ShapeDtypeStruct((nb, d, s), F32),
        compiler_params=_cparams(("parallel", "parallel", "arbitrary")),
    )(h, dpb)


def norm_in_proj(name, x, nw, wp, tm, tn, comm=None):
    n, d = x.shape
    wpc = wp.shape[1]
    tm, tn = min(tm, n), min(tn, wpc)
    assert n % tm == 0 and wpc % tn == 0

    def body(x_ref, nw_ref, w_ref, p_ref, h_ref):
        @pl.when(pl.program_id(1) == 0)
        def _():
            h_ref[...] = rms_block(x_ref[...], nw_ref[...]).astype(BF16)

        p_ref[...] = jnp.dot(h_ref[...], w_ref[...], preferred_element_type=F32)

    outs, couts = call_with_comm(
        body, name, (n // tm, wpc // tn),
        [pl.BlockSpec((tm, d), lambda i, j: (i, 0)), pl.BlockSpec((1, d), lambda i, j: (0, 0)),
         pl.BlockSpec((d, tn), lambda i, j: (0, j))],
        [pl.BlockSpec((tm, tn), lambda i, j: (i, j)), pl.BlockSpec((tm, d), lambda i, j: (i, 0))],
        [jax.ShapeDtypeStruct((n, wpc), F32), jax.ShapeDtypeStruct((n, d), BF16)], [], ("parallel", "arbitrary"),
        [x, nw, wp], comm)
    return (outs[0], outs[1], couts) if comm is not None else (outs[0], outs[1])


def norm_bwd(name, x, nw, dh, dres, tb, comm=None):
    n, d = x.shape
    tb = min(tb, n)

    def body(x_ref, nw_ref, dh_ref, dres_ref, dx_ref, dnw_ref):
        @pl.when(pl.program_id(0) == 0)
        def _():
            dnw_ref[...] = jnp.zeros_like(dnw_ref)

        _, vjp = jax.vjp(rms_block, x_ref[...], nw_ref[...])
        dx, dnw = vjp(dh_ref[...])
        dx_ref[...] = dx + dres_ref[...]
        dnw_ref[...] += dnw

    row = pl.BlockSpec((tb, d), lambda i: (i, 0))
    par = pl.BlockSpec((1, d), lambda i: (0, 0))
    outs, carried = call_with_comm(
        body, name, (n // tb,), [row, par, row, row], [row, par],
        [jax.ShapeDtypeStruct((n, d), F32), jax.ShapeDtypeStruct((1, d), F32)], [], ("arbitrary",),
        [x, nw, dh, dres], comm)
    return (outs[0], outs[1], carried) if comm is not None else (outs[0], outs[1])


def out_proj_loss(name, y, wo, x, target, tm):
    n, d = x.shape
    tm = min(tm, n)
    assert n % tm == 0

    def body(y_ref, w_ref, x_ref, t_ref, dz_ref, loss_ref):
        @pl.when(pl.program_id(0) == 0)
        def _():
            loss_ref[...] = jnp.zeros_like(loss_ref)

        z = x_ref[...] + jnp.dot(y_ref[...], w_ref[...], preferred_element_type=F32)
        err = z - t_ref[...]
        dz_ref[...] = err * (1.0 / d)
        part = 0.5 * jnp.sum(jnp.mean(err * err, axis=-1, keepdims=True), axis=0, keepdims=True)
        loss_ref[...] += jnp.broadcast_to(part, loss_ref.shape)

    row = pl.BlockSpec((tm, d), lambda i: (i, 0))
    dz, loss = pl.pallas_call(
        body, name=name, grid=(n // tm,),
        in_specs=[pl.BlockSpec((tm, y.shape[1]), lambda i: (i, 0)), pl.BlockSpec(wo.shape, lambda i: (0, 0)), row, row],
        out_specs=[row, pl.BlockSpec((8, LANE), lambda i: (0, 0))],
        out_shape=[jax.ShapeDtypeStruct((n, d), F32), jax.ShapeDtypeStruct((8, LANE), F32)],
        compiler_params=_cparams(("arbitrary",)),
    )(y, wo, x, target)
    return dz, loss[0, 0]


def rope_tables(name, pos_col, inv_freq_row):
    n = pos_col.shape[0]

    def body(p_ref, f_ref, c_ref, s_ref):
        ang = p_ref[...].astype(F32) * f_ref[...]
        lane = lax.broadcasted_iota(jnp.int32, ang.shape, 1)
        c_ref[...] = jnp.where(lane < ROT_DIM, jnp.cos(ang), 1.0)
        s_ref[...] = jnp.where(lane < ROT_DIM, jnp.sin(ang), 0.0)

    return pl.pallas_call(
        body, name=name, out_shape=[jax.ShapeDtypeStruct((n, A_HEAD), F32)] * 2,
    )(pos_col, inv_freq_row)


def _scan_operands(cfg, nseq, u_ref, w_ref, qg_ref, kd_ref, a_ref, gl_ref):
    pairs = [(b, h) for b in range(nseq) for h in range(cfg.CH)]
    st = lambda r, wd: jnp.stack([r[b, :, wd * h:wd * (h + 1)] for b, h in pairs], axis=0)
    gl = jnp.stack([gl_ref[b, 0:1, h:h + 1] for b, h in pairs], axis=0)
    return st(u_ref, C_HEAD), st(w_ref, C_HEAD), st(qg_ref, C_HEAD), st(kd_ref, C_HEAD), st(a_ref, CHUNK), gl


def gdn_scan_fwd(name, cfg, nseq, u, w, qg, kd, intra, glast):
    CH, CW, T = cfg.CH, cfg.CW, cfg.T
    nc = T // CHUNK

    def body(u_ref, w_ref, qg_ref, kd_ref, a_ref, gl_ref, o_ref, sin_ref, s_ref):
        @pl.when(pl.program_id(0) == 0)
        def _():
            s_ref[...] = jnp.zeros_like(s_ref)

        S = s_ref[...]
        for b in range(nseq):
            sin_ref[b, 0] = S[b * CH:(b + 1) * CH]
        o, S_next = gdn_state_step(S, *_scan_operands(cfg, nseq, u_ref, w_ref, qg_ref, kd_ref, a_ref, gl_ref))
        s_ref[...] = S_next
        for b in range(nseq):
            o_ref[b] = jnp.concatenate([o[b * CH + h] for h in range(CH)], axis=1)

    row = lambda wd: pl.BlockSpec((nseq, CHUNK, wd), lambda c: (0, c, 0))
    widths = [CW, CW, CW, CW, CH * CHUNK, LANE]
    o, s_in = pl.pallas_call(
        body, name=name, grid=(nc,),
        in_specs=[row(x) for x in widths],
        out_specs=[row(CW), pl.BlockSpec((nseq, 1, CH, C_HEAD, C_HEAD), lambda c: (0, c, 0, 0, 0))],
        out_shape=[jax.ShapeDtypeStruct((nseq, T, CW), F32),
                   jax.ShapeDtypeStruct((nseq, nc, CH, C_HEAD, C_HEAD), F32)],
        scratch_shapes=[pltpu.VMEM((nseq * CH, C_HEAD, C_HEAD), F32)],
        compiler_params=_cparams(("arbitrary",)),
    )(*[a.reshape(nseq, T, a.shape[1]) for a in (u, w, qg, kd, intra, glast)])
    return o.reshape(nseq * T, CW), s_in


def gdn_scan_bwd(name, cfg, nseq, u, w, qg, kd, intra, glast, s_in, do, comm=None):
    CH, CW, T = cfg.CH, cfg.CW, cfg.T
    nc = T // CHUNK

    def body(u_ref, w_ref, qg_ref, kd_ref, a_ref, gl_ref, sin_ref, do_ref,
             du_ref, dw_ref, dqg_ref, dkd_ref, da_ref, dgl_ref, ds_ref):
        @pl.when(pl.program_id(0) == 0)
        def _():
            ds_ref[...] = jnp.zeros_like(ds_ref)

        S = jnp.concatenate([sin_ref[b, 0] for b in range(nseq)], axis=0)
        dout = jnp.stack([do_ref[b, :, C_HEAD * h:C_HEAD * (h + 1)] for b in range(nseq) for h in range(CH)], axis=0)
        _, vjp = jax.vjp(gdn_state_step, S, *_scan_operands(cfg, nseq, u_ref, w_ref, qg_ref, kd_ref, a_ref, gl_ref))
        dS, du, dw, dqg, dkd, da, dg = vjp((dout, ds_ref[...]))
        ds_ref[...] = dS
        lane = lax.broadcasted_iota(jnp.int32, (CHUNK, LANE), 1)
        rowi = lax.broadcasted_iota(jnp.int32, (CHUNK, LANE), 0)
        for b in range(nseq):
            cat = lambda x: jnp.concatenate([x[b * CH + h] for h in range(CH)], axis=1)
            du_ref[b] = cat(du)
            dw_ref[b] = cat(dw)
            dqg_ref[b] = cat(dqg)
            dkd_ref[b] = cat(dkd)
            da_ref[b] = cat(da)
            dgl = jnp.zeros((CHUNK, LANE), F32)
            for h in range(CH):
                dgl = dgl + jnp.where((lane == h) & (rowi == 0), dg[b * CH + h], 0.0)
            dgl_ref[b] = dgl

    row = lambda wd: pl.BlockSpec((nseq, CHUNK, wd), lambda c: (0, nc - 1 - c, 0))
    widths = [CW, CW, CW, CW, CH * CHUNK, LANE]
    outs, carried = call_with_comm(
        body, name, (nc,),
        [row(x) for x in widths]
        + [pl.BlockSpec((nseq, 1, CH, C_HEAD, C_HEAD), lambda c: (0, nc - 1 - c, 0, 0, 0)), row(CW)],
        [row(x) for x in widths], [jax.ShapeDtypeStruct((nseq, T, x), F32) for x in widths],
        [pltpu.VMEM((nseq * CH, C_HEAD, C_HEAD), F32)], ("arbitrary",),
        [a.reshape(nseq, T, a.shape[1]) for a in (u, w, qg, kd, intra, glast)] + [s_in, do.reshape(nseq, T, CW)], comm)
    return [o.reshape(nseq * T, o.shape[2]) for o in outs], carried


def _tile(total, cap, unit=LANE):
    best = None
    for t in range(unit, min(cap, total) + 1, unit):
        if total % t == 0:
            best = t
    assert best is not None, (total, cap, unit)
    return best


def _pad_lanes(v, width=LANE):
    return jnp.pad(v.reshape(1, -1), ((0, 0), (0, width - v.shape[-1])))


def permute_w_in(cfg, w):
    parts = []
    for n in cfg.order:
        off, wd = cfg.orig[n]
        blk = w[:, off:off + wd]
        if cfg.g[n][1] != wd:
            blk = jnp.pad(blk, ((0, 0), (0, cfg.g[n][1] - wd)))
        parts.append(blk)
    return jnp.concatenate(parts, axis=1)


def chip_blocks(cfg, groups, n_chips):
    s = cfg.IN_COLS // n_chips
    blocks = []
    for j in range(n_chips):
        lo, hi = j * s, (j + 1) * s
        pieces = []
        for name, (off, wd) in cfg.orig.items():
            a, b = max(lo, off), min(hi, off + wd)
            if a < b:
                pieces.append(groups[name][:, a - off:b - off])
        blocks.append(jnp.concatenate(pieces, axis=1))
    return jnp.stack(blocks)


def _layer_params(cfg, prm):
    return dict(
        nw=prm['norm_w'].reshape(1, -1),
        qnw=prm['q_norm_w'].reshape(1, -1), knw=prm['k_norm_w'].reshape(1, -1),
        sinks_row=jnp.repeat(prm['sinks'], A_HEAD).reshape(1, -1),
        cw=prm['b_conv_w'], cb=prm['b_conv_b'].reshape(1, -1),
        lw=prm['b_ln_w'].reshape(1, -1), lb=prm['b_ln_b'].reshape(1, -1),
        pw=prm['b_pw_w'], pb=prm['b_pw_b'].reshape(1, -1),
        ccw=prm['c_conv_w'], alog=_pad_lanes(prm['c_a_log']), dtb=_pad_lanes(prm['c_dt_bias']),
        onw=prm['c_onorm_w'].reshape(1, -1),
    )


def _attn_io(cfg, p, cos, sin, grads):
    gq = BF16 if grads else None
    rows = [Row(p, cfg.AW, cfg.blk('qa'), gq), Row(p, cfg.AW, cfg.blk('za'), gq),
            Row(p, cfg.AKW, cfg.blk('ka'), gq), Row(p, cfg.AKW, cfg.blk('va'), gq),
            Row(cos, A_HEAD), Row(sin, A_HEAD)]
    halos = [Halo(p, cfg.AKW, cfg.blk('ka'), ATTN_BLOCK, 2 if grads else None),
             Halo(p, cfg.AKW, cfg.blk('va'), ATTN_BLOCK, 3 if grads else None),
             Halo(cos, A_HEAD, 0, ATTN_BLOCK), Halo(sin, A_HEAD, 0, ATTN_BLOCK)]
    return rows, halos


def _conv_io(cfg, p, grads):
    gq = BF16 if grads else None
    rows = [Row(p, 2 * cfg.BW, cfg.blk('ub'), gq), Row(p, cfg.BW, cfg.blk('zb'), gq)]
    halos = [Halo(p, 2 * cfg.BW, cfg.blk('ub'), B_HALO, 0 if grads else None)]
    return rows, halos


def _prep_io(cfg, p, grads):
    gq = BF16 if grads else None
    rows = [Row(p, cfg.CW, cfg.blk(n), gq) for n in ('qc', 'kc', 'vc')]
    rows += [Row(p, LANE, cfg.blk('bc'), gq), Row(p, LANE, cfg.blk('ac'), gq)]
    halos = [Halo(p, cfg.CW, cfg.blk(n), C_HALO, k if grads else None) for k, n in enumerate(('qc', 'kc', 'vc'))]
    return rows, halos


TB_CONV = 128
TB_PREP = 256
TB_OUT = 256
TB_INTRA_FWD = 128
TB_INTRA_BWD = 128


def layer_forward(cfg, l, x, lp, wp, wo, cos, sin, comms=(None, None), target=None):
    n = x.shape[0]
    nseq = n // cfg.T
    T = cfg.T
    p, h, *carried = norm_in_proj(f"in_proj_{l}", x, lp['nw'], wp, 1024, _tile(cfg.WP, 768), comm=comms[0])
    rows, halos = _attn_io(cfg, p, cos, sin, False)
    oa = rb_fwd(f"attn_fwd_{l}", functools.partial(attn_block, cfg), n, ATTN_BLOCK, T // ATTN_BLOCK, rows, halos,
                [lp['qnw'], lp['knw'], lp['sinks_row']], [(cfg.AW, BF16)], comm=comms[1])
    if comms[1] is not None:
        oa, more = oa
        carried = [carried[0] + more] if carried else [more]
    (oa,) = oa
    rows, halos = _conv_io(cfg, p, False)
    tbb = min(TB_CONV, T)
    (ob,) = rb_fwd(f"conv_fwd_{l}", functools.partial(conv_block, cfg), n, tbb, T // tbb, rows, halos,
                   [lp['cw'], lp['cb'], lp['lw'], lp['lb'], lp['pw'], lp['pb']], [(cfg.BW, BF16)])
    rows, halos = _prep_io(cfg, p, False)
    tbp = min(TB_PREP, T)
    qn, kn, v, g, beta = rb_fwd(f"gdn_prep_fwd_{l}", functools.partial(gdn_prep_block, cfg), n, tbp, T // tbp, rows,
                                halos, [lp['ccw'], lp['alog'], lp['dtb']],
                                [(cfg.CW, F32)] * 3 + [(LANE, F32)] * 2)
    intra_outs = rb_fwd(f"gdn_intra_fwd_{l}", functools.partial(gdn_intra_rows, cfg), n, TB_INTRA_FWD, T // TB_INTRA_FWD,
                        [Row(qn, cfg.CW), Row(kn, cfg.CW), Row(v, cfg.CW), Row(g, LANE), Row(beta, LANE)], [], [],
                        [(cfg.CW, F32)] * 4 + [(cfg.CH * CHUNK, F32), (LANE, F32), (cfg.CH * CHUNK, F32)])
    intra_outs, inv = intra_outs[:6], intra_outs[6]
    o, s_in = gdn_scan_fwd(f"gdn_scan_fwd_{l}", cfg, nseq, *intra_outs)
    tbo = min(TB_OUT, T)
    (oc,) = rb_fwd(f"gdn_out_fwd_{l}", functools.partial(gdn_out_block, cfg), n, tbo, T // tbo,
                   [Row(o, cfg.CW), Row(p, cfg.CW, cfg.blk('zc'))], [], [lp['onw']], [(cfg.CW, BF16)])
    y = jnp.concatenate([oa, ob, oc], axis=1)
    if target is None:
        x_next = matmul(f"out_proj_{l}", y, wo, 'nn', 1024, 1024, cfg.D, add=x)
    else:
        x_next = out_proj_loss(f"out_proj_{l}", y, wo, x, target, 512)
    saved = dict(x=x, p=p, h=h, y=y, qn=qn, kn=kn, v=v, g=g, beta=beta, intra_outs=intra_outs, inv=inv, s_in=s_in, o=o)
    return x_next, saved, (carried[0] if carried else None)


def layer_backward(cfg, l, dxn, sv, lp, wp, wo, cos, sin, rs=None, own_rs=None):
    n = dxn.shape[0]
    nseq = n // cfg.T
    T = cfg.T
    p = sv['p']
    AW, BW, CW = cfg.AW, cfg.BW, cfg.CW
    dy = matmul(f"dy_{l}", dxn, wo, 'nt', 1024, 1024, cfg.D)
    dwo = matmul(f"dwo_{l}", sv['y'], dxn, 'tn', 1024, 1024, 2048)
    doa, dob, doc = dy[:, :AW], dy[:, AW:AW + BW], dy[:, AW + BW:]
    tbo = min(TB_OUT, T)
    do, dzc, donw = rb_bwd(f"gdn_out_bwd_{l}", functools.partial(gdn_out_block, cfg), n, tbo, T // tbo,
                           [Row(sv['o'], CW, 0, F32), Row(p, CW, cfg.blk('zc'), BF16)], [], [lp['onw']], [doc], [True])
    dintra, got_rest = gdn_scan_bwd(f"gdn_scan_bwd_{l}", cfg, nseq, *sv['intra_outs'], sv['s_in'], do,
                                    comm=None if rs is None else rs.scatter([1, 2]))
    dqn, dkn, dv, dg, dbeta = rb_bwd(
        f"gdn_intra_bwd_{l}", functools.partial(gdn_intra_rows, cfg), n, TB_INTRA_BWD, T // TB_INTRA_BWD,
        [Row(sv['qn'], CW, 0, F32), Row(sv['kn'], CW, 0, F32), Row(sv['v'], CW, 0, F32), Row(sv['g'], LANE, 0, F32),
         Row(sv['beta'], LANE, 0, F32), Row(sv['inv'], cfg.CH * CHUNK)], [], [], list(dintra), [])
    rows, halos = _prep_io(cfg, p, True)
    tbp = min(TB_PREP, T)
    dqc, dkc, dvc, dbc, dac, dccw, dalog, ddtb = rb_bwd(
        f"gdn_prep_bwd_{l}", functools.partial(gdn_prep_block, cfg), n, tbp, T // tbp, rows, halos,
        [lp['ccw'], lp['alog'], lp['dtb']], [dqn, dkn, dv, dg, dbeta], [True] * 3)
    rows, halos = _conv_io(cfg, p, True)
    tbb = min(TB_CONV, T)
    conv_grads = rb_bwd(
        f"conv_bwd_{l}", functools.partial(conv_block, cfg), n, tbb, T // tbb, rows, halos,
        [lp['cw'], lp['cb'], lp['lw'], lp['lb'], lp['pw'], lp['pb']], [dob], [True] * 6,
        comm=None if rs is None else rs.scatter([0]))
    got = None
    if rs is not None:
        conv_grads, got_w_in = conv_grads
        got = got_w_in + got_rest
    dub, dzb, dcw, dcb, dlw, dlb, dpw, dpb = conv_grads
    rows, halos = _attn_io(cfg, p, cos, sin, True)
    dqa, dza, dka, dva, dqnw, dknw, dsinks_row = rb_bwd(
        f"attn_bwd_{l}", functools.partial(attn_block, cfg), n, ATTN_BLOCK, T // ATTN_BLOCK, rows, halos,
        [lp['qnw'], lp['knw'], lp['sinks_row']], [doa], [True] * 3)
    dgroups = dict(qa=dqa, za=dza, qc=dqc, kc=dkc, vc=dvc, zc=dzc, ka=dka, va=dva, ub=dub, zb=dzb, bc=dbc, ac=dac)
    dp = jnp.concatenate([dgroups[k] for k in cfg.order], axis=1)
    dw_in = grad_w_blocks(f"dwp_{l}", sv['h'], chip_blocks(cfg, dgroups, N_CHIPS), 1024, 2048)
    mine = None if own_rs is None else own_rs(dict(w_in_blocks=dw_in, w_out=dwo, b_pw_w=dpw))
    got_mine = None
    if mine is None:
        dh = matmul(f"dh_{l}", dp, wp, 'nt', 1024, 1024, _tile(cfg.WP, 2304))
        dx, dnw = norm_bwd(f"norm_bwd_{l}", sv['x'], lp['nw'], dh, dxn, 256)
    elif l > 0:
        dh, mine_received = matmul(f"dh_{l}", dp, wp, 'nt', 1024, 1024, _tile(cfg.WP, 2304), comm=mine.swap())
        mine.add(mine_received)
        dx, dnw = norm_bwd(f"norm_bwd_{l}", sv['x'], lp['nw'], dh, dxn, 256)
    else:
        mine.add(run_comm(f"rs{l}_swap_halves", mine.swap()))
        dh, got_w_in = matmul(f"dh_{l}", dp, wp, 'nt', 1024, 1024, _tile(cfg.WP, 2304), comm=mine.scatter([0]))
        dx, dnw, got_others = norm_bwd(f"norm_bwd_{l}", sv['x'], lp['nw'], dh, dxn, 256, comm=mine.scatter([1, 2]))
        got_mine = got_w_in + got_others
    grads = dict(
        norm_w=dnw[0], w_in_blocks=dw_in, q_norm_w=dqnw[0], k_norm_w=dknw[0],
        sinks=dsinks_row.reshape(cfg.AQH, A_HEAD)[:, 0],
        b_conv_w=dcw, b_conv_b=dcb[0], b_ln_w=dlw[0], b_ln_b=dlb[0], b_pw_w=dpw, b_pw_b=dpb[0],
        c_conv_w=dccw, c_a_log=dalog[0, :cfg.CH], c_dt_bias=ddtb[0, :cfg.CH], c_onorm_w=donw[0], w_out=dwo)
    return dx, grads, (got, mine, got_mine)


def rope_for(cfg, positions):
    n = positions.size
    inv_freq = ROPE_THETA ** (-np.arange(0, ROT_DIM, 2, dtype=np.float32) / ROT_DIM)
    freq_row = np.zeros((1, A_HEAD), np.float32)
    freq_row[0, :ROT_DIM] = np.concatenate([inv_freq, inv_freq])
    return rope_tables("rope_tables", positions.reshape(n, 1), jnp.asarray(freq_row))


def local_step(cfg, x, positions, prm, wps, wos, target):
    nseq = x.shape[0]
    n = nseq * cfg.T
    cos, sin = rope_for(cfg, positions)
    lps = [_layer_params(cfg, {k: v[l] for k, v in prm.items()}) for l in range(DEPTH)]
    saved = []
    xl = x.reshape(n, cfg.D)
    for l in range(DEPTH):
        xl, sv, _ = layer_forward(cfg, l, xl, lps[l], wps[l], wos[l], cos, sin,
                                  target=target.reshape(n, cfg.D) if l == DEPTH - 1 else None)
        saved.append(sv)
    dx, loss = xl
    grads = [None] * DEPTH
    for l in reversed(range(DEPTH)):
        dx, grads[l], _ = layer_backward(cfg, l, dx, saved[l], lps[l], wps[l], wos[l], cos, sin)
    return loss, dx.reshape(x.shape), grads


N_CHIPS = 4
N_DEV = 8


def _place():
    return lax.axis_index("x"), lax.axis_index("y"), lax.axis_index("c")


def _other_chips(x, y):
    return [(1 - x, y), (x, 1 - y), (1 - x, 1 - y)]


def _remote(src, dst, send, recv, to):
    return pltpu.make_async_remote_copy(src_ref=src, dst_ref=dst, send_sem=send, recv_sem=recv, device_id=to,
                                        device_id_type=MESH)


def gather_comm(arrs):
    n = len(arrs)

    def half(c):
        return [pl.ds(c * (a.shape[0] // 2), a.shape[0] // 2) for a in arrs]

    def first_copies(ins, outs, send, recv):
        x, y, c = _place()
        me = 2 * x + y
        mine = half(c)
        return [_remote(ins[i].at[mine[i]], outs[i].at[me, mine[i]], send.at[i, j], recv.at[i, j], (cx, cy, c))
                for i in range(n) for j, (cx, cy) in enumerate(_other_chips(x, y))]

    def start(ins, outs, sems):
        for cp in first_copies(ins, outs, *sems):
            cp.start()

    def finish(ins, outs, sems):
        send, recv = sems
        x, y, c = _place()
        chips = _other_chips(x, y)
        sib = (x, y, 1 - c)
        passed = []
        mine, other = half(c), half(1 - c)
        for i in range(n):
            for j, (cx, cy) in enumerate(chips):
                blk = outs[i].at[2 * cx + cy, mine[i]]
                _remote(blk, blk, send.at[i, j], recv.at[i, j], (x, y, c)).wait_recv()
                cp = _remote(blk, blk, send.at[i, 3 + j], recv.at[i, 3 + j], sib)
                cp.start()
                passed.append(cp)
        for i in range(n):
            for j, (cx, cy) in enumerate(chips):
                blk = outs[i].at[2 * cx + cy, other[i]]
                _remote(blk, blk, send.at[i, 3 + j], recv.at[i, 3 + j], sib).wait_recv()
        for cp in first_copies(ins, outs, send, recv) + passed:
            cp.wait_send()

    return Comm(arrs, [jax.ShapeDtypeStruct((N_CHIPS,) + a.shape, a.dtype) for a in arrs],
                [pltpu.SemaphoreType.DMA((n, 6)), pltpu.SemaphoreType.DMA((n, 6))], start, finish)


def fill_own(gathered, arrs):
    me = 2 * lax.axis_index("x") + lax.axis_index("y")
    return [lax.dynamic_update_index_in_dim(o, a, me, 0) for o, a in zip(gathered, arrs)]


def swap_comm(arrs):
    n = len(arrs)

    def copies(ins, outs, send, recv):
        x, y, c = _place()
        return [_remote(ins[i].at[:, 1 - c], outs[i], send.at[i], recv.at[i], (x, y, 1 - c)) for i in range(n)]

    def start(ins, outs, sems):
        for cp in copies(ins, outs, *sems):
            cp.start()

    def finish(ins, outs, sems):
        for cp in copies(ins, outs, *sems):
            cp.wait()

    return Comm(arrs, [jax.ShapeDtypeStruct((a.shape[0],) + a.shape[2:], a.dtype) for a in arrs],
                [pltpu.SemaphoreType.DMA((n,)), pltpu.SemaphoreType.DMA((n,))], start, finish)


def scatter_comm(arrs):
    n = len(arrs)

    def copies(ins, outs, send, recv):
        x, y, c = _place()
        return [_remote(ins[i].at[2 * cx + cy], outs[i].at[j], send.at[i, j], recv.at[i, j], (cx, cy, c))
                for i in range(n) for j, (cx, cy) in enumerate(_other_chips(x, y))]

    def start(ins, outs, sems):
        for cp in copies(ins, outs, *sems):
            cp.start()

    def finish(ins, outs, sems):
        send, recv = sems
        x, y, c = _place()
        for i in range(n):
            for j in range(3):
                blk = outs[i].at[j]
                _remote(blk, blk, send.at[i, j], recv.at[i, j], (x, y, c)).wait_recv()
        for cp in copies(ins, outs, send, recv):
            cp.wait_send()

    return Comm(arrs, [jax.ShapeDtypeStruct((3,) + a.shape[1:], a.dtype) for a in arrs],
                [pltpu.SemaphoreType.DMA((n, 3)), pltpu.SemaphoreType.DMA((n, 3))], start, finish)


def share_comm(arrs):
    n = len(arrs)

    def copies(outs, send, recv):
        x, y, c = _place()
        return [_remote(outs[i].at[c], outs[i].at[c], send.at[i], recv.at[i], (x, y, 1 - c)) for i in range(n)]

    def start(ins, outs, sems):
        for cp in copies(outs, *sems):
            cp.start()

    def finish(ins, outs, sems):
        send, recv = sems
        x, y, c = _place()
        for i in range(n):
            blk = outs[i].at[1 - c]
            _remote(blk, blk, send.at[i], recv.at[i], (x, y, c)).wait_recv()
        for cp in copies(outs, send, recv):
            cp.wait_send()

    return Comm(arrs, [jax.ShapeDtypeStruct(a.shape, a.dtype) for a in arrs],
                [pltpu.SemaphoreType.DMA((n,)), pltpu.SemaphoreType.DMA((n,))], start, finish,
                aliases={i: i for i in range(n)})


def all_reduce_small(name, packed):
    r = packed.shape[0]

    def body(in_ref, out_ref, buf, send, recv):
        x, y, c = _place()
        me = 4 * x + 2 * y + c
        buf[me] = in_ref[...]
        flips = [(fx, fy, fc) for fx in (0, 1) for fy in (0, 1) for fc in (0, 1) if (fx, fy, fc) != (0, 0, 0)]
        peers = [((x + fx) % 2, (y + fy) % 2, (c + fc) % 2) for fx, fy, fc in flips]
        cps = [_remote(in_ref, buf.at[me], send.at[k], recv.at[k], peer) for k, peer in enumerate(peers)]
        for cp in cps:
            cp.start()
        for k, (px, py, pc) in enumerate(peers):
            blk = buf.at[4 * px + 2 * py + pc]
            _remote(blk, blk, send.at[k], recv.at[k], (x, y, c)).wait_recv()
        for cp in cps:
            cp.wait_send()
        acc = buf[0]
        for d in range(1, N_DEV):
            acc = acc + buf[d]
        out_ref[...] = acc

    vm = pl.BlockSpec(memory_space=pltpu.VMEM)
    return pl.pallas_call(
        body, name=name, in_specs=[vm], out_specs=vm, out_shape=jax.ShapeDtypeStruct(packed.shape, F32),
        scratch_shapes=[pltpu.VMEM((N_DEV, r, LANE), F32), pltpu.SemaphoreType.DMA((N_DEV - 1,)),
                        pltpu.SemaphoreType.DMA((N_DEV - 1,))],
    )(packed)


def add_own_half(name, g, a, c_idx, tr):
    nch, _, r, cc = g.shape
    tr = min(tr, r)

    def body(c_ref, g_ref, a_ref, o_ref):
        o_ref[...] = (g_ref[0] + a_ref[...]).astype(o_ref.dtype)

    return pl.pallas_call(
        body, name=name,
        grid_spec=pltpu.PrefetchScalarGridSpec(
            num_scalar_prefetch=1, grid=(nch, r // tr),
            in_specs=[pl.BlockSpec((1, 1, tr, cc), lambda j, i, c_ref: (j, c_ref[0], i, 0)),
                      pl.BlockSpec((1, tr, cc), lambda j, i, c_ref: (j, i, 0))],
            out_specs=pl.BlockSpec((1, tr, cc), lambda j, i, c_ref: (j, i, 0))),
        out_shape=jax.ShapeDtypeStruct(a.shape, BF16),
        compiler_params=_cparams(("parallel", "parallel")),
    )(c_idx, g, a)


def sum_chips(name, p, b, idx, tr):
    _, r, cc = p.shape
    tr = min(tr, r)

    def body(idx_ref, p_ref, b_ref, o_ref):
        acc = p_ref[0].astype(F32)
        for k in range(3):
            acc = acc + b_ref[k].astype(F32)
        o_ref[0] = acc

    return pl.pallas_call(
        body, name=name,
        grid_spec=pltpu.PrefetchScalarGridSpec(
            num_scalar_prefetch=1, grid=(r // tr,),
            in_specs=[pl.BlockSpec((1, tr, cc), lambda i, s: (s[0], i, 0)),
                      pl.BlockSpec((3, tr, cc), lambda i, s: (0, i, 0))],
            out_specs=pl.BlockSpec((1, tr, cc), lambda i, s: (s[1], i, 0))),
        out_shape=jax.ShapeDtypeStruct((2, r, cc), F32),
        compiler_params=_cparams(("parallel",)),
    )(idx, p, b)


class GradReduce:
    def __init__(self, tag, parts, chip, c_idx):
        self.tag, self.c_idx = tag, c_idx
        self.parts = [p.reshape(p.shape[0], 2, p.shape[1] // 2, p.shape[2]) for p in parts]
        self.idx = jnp.concatenate([chip.astype(jnp.int32).reshape(1), c_idx])

    def swap(self):
        return swap_comm(self.parts)

    def add(self, received):
        self.part = [add_own_half(f"rs{self.tag}_add_sibling_{t}", g, a, self.c_idx, 128)
                     for t, (g, a) in enumerate(zip(self.parts, received))]

    def scatter(self, which=None):
        return scatter_comm(self.part if which is None else [self.part[t] for t in which])

    def finish(self, got):
        red = [sum_chips(f"rs{self.tag}_sum_chips_{t}", p, b, self.idx, 128) for t, (p, b) in enumerate(zip(self.part, got))]
        out = run_comm(f"rs{self.tag}_share_halves", share_comm(red))
        return [o.reshape(-1, o.shape[-1]) for o in out]


def adamw_many(name, ws, gs, ms, vs):
    n = len(ws)

    def body(*refs):
        for i in range(n):
            w_ref, g_ref, m_ref, v_ref = (refs[k * n + i] for k in range(4))
            d_ref, mo_ref, vo_ref = (refs[(4 + k) * n + i] for k in range(3))
            g = g_ref[...]
            m = ADAM_B1 * m_ref[...] + (1.0 - ADAM_B1) * g
            v = ADAM_B2 * v_ref[...] + (1.0 - ADAM_B2) * jnp.square(g)
            m_hat = m / (1.0 - ADAM_B1 ** ADAM_STEP)
            v_hat = v / (1.0 - ADAM_B2 ** ADAM_STEP)
            d_ref[...] = -ADAM_LR * (m_hat / (jnp.sqrt(v_hat) + ADAM_EPS) + ADAM_WD * w_ref[...])
            mo_ref[...] = m
            vo_ref[...] = v

    vm = pl.BlockSpec(memory_space=pltpu.VMEM)
    return pl.pallas_call(
        body, name=name, in_specs=[vm] * (4 * n), out_specs=[vm] * (3 * n),
        out_shape=[jax.ShapeDtypeStruct(a.shape, F32) for a in ws] * 3,
    )(*ws, *gs, *ms, *vs)


def adamw_layers(name, w, g0, g1, m, v, tb):
    _, r, cc = w.shape
    tb = min(tb, r)
    nb = r // tb

    def body(w_ref, g0_ref, g1_ref, m_ref, v_ref, g_ref, d_ref, mo_ref, vo_ref):
        g = jnp.where(pl.program_id(0) == 0, g0_ref[...], g1_ref[...])
        m = ADAM_B1 * m_ref[0] + (1.0 - ADAM_B1) * g
        v = ADAM_B2 * v_ref[0] + (1.0 - ADAM_B2) * jnp.square(g)
        m_hat = m / (1.0 - ADAM_B1 ** ADAM_STEP)
        v_hat = v / (1.0 - ADAM_B2 ** ADAM_STEP)
        g_ref[0] = g
        d_ref[0] = -ADAM_LR * (m_hat / (jnp.sqrt(v_hat) + ADAM_EPS) + ADAM_WD * w_ref[0])
        mo_ref[0] = m
        vo_ref[0] = v

    spec = pl.BlockSpec((1, tb, cc), lambda l, i: (l, i, 0))
    g0_spec = pl.BlockSpec((tb, cc), lambda l, i: (jnp.where(l == 0, i, nb - 1), 0))
    g1_spec = pl.BlockSpec((tb, cc), lambda l, i: (jnp.where(l == 1, i, 0), 0))
    return pl.pallas_call(
        body, name=name, grid=(2, nb), in_specs=[spec, g0_spec, g1_spec, spec, spec], out_specs=[spec] * 4,
        out_shape=[jax.ShapeDtypeStruct(w.shape, F32)] * 4,
        compiler_params=_cparams(("arbitrary", "arbitrary")),
    )(w, g0, g1, m, v)


def adamw_cols_major(name, w, g0, g1, m, v, tb=LANE):
    wt, mt, vt = (jnp.transpose(a, (2, 0, 1)) for a in (w, m, v))
    cc, _, r = wt.shape

    def body(w_ref, g0_ref, g1_ref, m_ref, v_ref, g_ref, d_ref, mo_ref, vo_ref):
        for l, gl_ref in enumerate((g0_ref, g1_ref)):
            g = gl_ref[...].T
            m = ADAM_B1 * m_ref[:, l, :] + (1.0 - ADAM_B1) * g
            v = ADAM_B2 * v_ref[:, l, :] + (1.0 - ADAM_B2) * jnp.square(g)
            m_hat = m / (1.0 - ADAM_B1 ** ADAM_STEP)
            v_hat = v / (1.0 - ADAM_B2 ** ADAM_STEP)
            g_ref[:, l, :] = g
            d_ref[:, l, :] = -ADAM_LR * (m_hat / (jnp.sqrt(v_hat) + ADAM_EPS) + ADAM_WD * w_ref[:, l, :])
            mo_ref[:, l, :] = m
            vo_ref[:, l, :] = v

    spec = pl.BlockSpec((tb, 2, r), lambda i: (i, 0, 0))
    gspec = pl.BlockSpec((r, tb), lambda i: (0, i))
    outs = pl.pallas_call(
        body, name=name, grid=(pl.cdiv(cc, tb),), in_specs=[spec, gspec, gspec, spec, spec], out_specs=[spec] * 4,
        out_shape=[jax.ShapeDtypeStruct(wt.shape, F32)] * 4,
        compiler_params=_cparams(("parallel",)),
    )(wt, g0, g1, mt, vt)
    return [jnp.transpose(o, (1, 2, 0)) for o in outs]


def _pack(arrs):
    flat = jnp.concatenate([a.reshape(-1).astype(F32) for a in arrs])
    pad = (-flat.shape[0]) % (8 * LANE)
    return jnp.pad(flat, (0, pad)).reshape(-1, LANE)


def _unpack(packed, shapes):
    flat = packed.reshape(-1)
    out, off = [], 0
    for s in shapes:
        size = math.prod(s)
        out.append(flat[off:off + size].reshape(s))
        off += size
    return out


BIG = ('w_in', 'w_out', 'b_pw_w')
SMALL = tuple(k for k in WEIGHTS if k not in BIG)
CHIP_SHARDED_SMALL = {'b_conv_w': 2, 'c_conv_w': 2}


def kernel(x, positions, norm_w, w_in, q_norm_w, k_norm_w, sinks, b_conv_w, b_conv_b, b_ln_w, b_ln_b, b_pw_w, b_pw_b, c_conv_w, c_a_log, c_dt_bias, c_onorm_w, w_out, loss_target, m_norm_w, m_w_in, m_q_norm_w, m_k_norm_w, m_sinks, m_b_conv_w, m_b_conv_b, m_b_ln_w, m_b_ln_b, m_b_pw_w, m_b_pw_b, m_c_conv_w, m_c_a_log, m_c_dt_bias, m_c_onorm_w, m_w_out, v_norm_w, v_w_in, v_q_norm_w, v_k_norm_w, v_sinks, v_b_conv_w, v_b_conv_b, v_b_ln_w, v_b_ln_b, v_b_pw_w, v_b_pw_b, v_c_conv_w, v_c_a_log, v_c_dt_bias, v_c_onorm_w, v_w_out):
    cfg = Cfg(x.shape[-1], x.shape[-2])
    w = dict(norm_w=norm_w, w_in=w_in, q_norm_w=q_norm_w, k_norm_w=k_norm_w, sinks=sinks, b_conv_w=b_conv_w,
             b_conv_b=b_conv_b, b_ln_w=b_ln_w, b_ln_b=b_ln_b, b_pw_w=b_pw_w, b_pw_b=b_pw_b, c_conv_w=c_conv_w,
             c_a_log=c_a_log, c_dt_bias=c_dt_bias, c_onorm_w=c_onorm_w, w_out=w_out)
    m = dict(norm_w=m_norm_w, w_in=m_w_in, q_norm_w=m_q_norm_w, k_norm_w=m_k_norm_w, sinks=m_sinks,
             b_conv_w=m_b_conv_w, b_conv_b=m_b_conv_b, b_ln_w=m_b_ln_w, b_ln_b=m_b_ln_b, b_pw_w=m_b_pw_w,
             b_pw_b=m_b_pw_b, c_conv_w=m_c_conv_w, c_a_log=m_c_a_log, c_dt_bias=m_c_dt_bias, c_onorm_w=m_c_onorm_w,
             w_out=m_w_out)
    v = dict(norm_w=v_norm_w, w_in=v_w_in, q_norm_w=v_q_norm_w, k_norm_w=v_k_norm_w, sinks=v_sinks,
             b_conv_w=v_b_conv_w, b_conv_b=v_b_conv_b, b_ln_w=v_b_ln_w, b_ln_b=v_b_ln_b, b_pw_w=v_b_pw_w,
             b_pw_b=v_b_pw_b, c_conv_w=v_c_conv_w, c_a_log=v_c_a_log, c_dt_bias=v_c_dt_bias, c_onorm_w=v_c_onorm_w,
             w_out=v_w_out)
    chip = 2 * lax.axis_index("x") + lax.axis_index("y")
    c_idx = lax.axis_index("c").astype(jnp.int32).reshape(1)
    D, T = cfg.D, cfg.T
    nseq = x.shape[0]
    n = nseq * T
    w_in_b, w_out_b = w_in.astype(BF16), w_out.astype(BF16)
    per_layer = lambda l: [w_in_b[l], w_out_b[l], b_pw_w[l]]

    def full_weights(g_in, g_out, g_pw):
        return permute_w_in(cfg, jnp.concatenate(list(g_in), axis=1)), g_out.reshape(D, D), g_pw.reshape(cfg.BW, cfg.BW)

    def layer_prm(l, pw_full):
        prm = {k: w[k][l] for k in SMALL}
        prm['b_pw_w'] = pw_full
        prm['b_conv_w'] = jnp.concatenate(list(g_bcw[:, l]), axis=1)
        prm['c_conv_w'] = jnp.concatenate(list(g_ccw[:, l]), axis=1)
        return _layer_params(cfg, prm)

    first = per_layer(0) + [b_conv_w, c_conv_w]
    g_in0, g_out0, g_pw0, g_bcw, g_ccw = fill_own(run_comm("gather_weights_0", gather_comm(first)), first)
    wp0, wo0, pw0 = full_weights(g_in0, g_out0, g_pw0)
    cos, sin = rope_for(cfg, positions)
    lp0 = layer_prm(0, pw0)
    second = per_layer(1)
    x1, sv0, gathered = layer_forward(cfg, 0, x.reshape(n, D), lp0, wp0, wo0, cos, sin,
                                      comms=(gather_comm(second[:1]), gather_comm(second[1:])))
    wp1, wo1, pw1 = full_weights(*fill_own(gathered, second))
    lp1 = layer_prm(1, pw1)
    (dx2, loss_local), sv1, _ = layer_forward(cfg, 1, x1, lp1, wp1, wo1, cos, sin, target=loss_target.reshape(n, D))
    loss = lax.psum(loss_local, ("x", "y", "c"))

    def partials(gr):
        return [gr['w_in_blocks'], gr['w_out'].reshape(N_CHIPS, D // N_CHIPS, D),
                gr['b_pw_w'].reshape(N_CHIPS, cfg.BW // N_CHIPS, cfg.BW)]

    dx1, gr1, (_, rs1, _) = layer_backward(cfg, 1, dx2, sv1, lp1, wp1, wo1, cos, sin,
                                        own_rs=lambda gr: GradReduce(1, partials(gr), chip, c_idx))
    dx0, gr0, (got1, rs0, got0) = layer_backward(cfg, 0, dx1, sv0, lp0, wp0, wo0, cos, sin, rs=rs1,
                                           own_rs=lambda gr: GradReduce(0, partials(gr), chip, c_idx))
    red1 = rs1.finish(got1)
    red0 = rs0.finish(got0)
    grad_x = dx0.reshape(x.shape)
    grads = [gr0, gr1]

    small_parts = [jnp.stack([grads[l][k] for l in range(DEPTH)]) for k in SMALL]
    small_red = _unpack(all_reduce_small("all_reduce_small", _pack(small_parts)), [a.shape for a in small_parts])
    g = {}
    for k, a in zip(SMALL, small_red):
        if k in CHIP_SHARDED_SMALL:
            ax = CHIP_SHARDED_SMALL[k]
            width = a.shape[ax] // N_CHIPS
            a = lax.dynamic_slice_in_dim(a, chip * width, width, axis=ax)
        g[k] = a

    delta, new_m, new_v = {}, {}, {}
    for k, g0, g1 in zip(BIG, red0, red1):
        update = adamw_layers if w[k].shape[-1] % LANE == 0 else adamw_cols_major
        g[k], delta[k], new_m[k], new_v[k] = update(f"adamw_{k}", w[k], g0, g1, m[k], v[k], 128)
    outs = adamw_many("adamw_small", *[[d[k] for k in SMALL] for d in (w, g, m, v)])
    for i, k in enumerate(SMALL):
        delta[k], new_m[k], new_v[k] = outs[i], outs[len(SMALL) + i], outs[2 * len(SMALL) + i]
    return (loss, grad_x, *[g[k] for k in WEIGHTS], *[delta[k] for k in WEIGHTS], *[new_m[k] for k in WEIGHTS],
            *[new_v[k] for k in WEIGHTS])
```

```python
import functools
import math

import numpy as np
import jax
import jax.numpy as jnp
from jax import lax
from jax.experimental import pallas as pl
from jax.experimental.pallas import tpu as pltpu

F32 = jnp.float32
BF16 = jnp.bfloat16
HI = lax.Precision.HIGHEST
MESH = pl.DeviceIdType.MESH

DEPTH = 2
A_HEAD = 64
A_GROUP = 3
ATTN_BLOCK = 128
ROT_DIM = 16
ROPE_THETA = 500000.0
B_CONV = 31
B_HALO = 32
C_HEAD = 128
C_CONV = 4
C_HALO = 8
CHUNK = 64
EPS = 1e-6
LANE = 128

ADAM_LR = 0.001
ADAM_B1 = 0.9
ADAM_B2 = 0.999
ADAM_EPS = 1e-08
ADAM_WD = 0.01
ADAM_STEP = 10

VMEM_LIMIT = 56 * 1024 * 1024

WEIGHTS = ['norm_w', 'w_in', 'q_norm_w', 'k_norm_w', 'sinks', 'b_conv_w', 'b_conv_b', 'b_ln_w', 'b_ln_b',
           'b_pw_w', 'b_pw_b', 'c_conv_w', 'c_a_log', 'c_dt_bias', 'c_onorm_w', 'w_out']


class Cfg:
    def __init__(self, d_model=2048, seq=2048):
        self.D = d_model
        self.T = seq
        self.AW = 3 * d_model // 8
        self.AQH = self.AW // A_HEAD
        self.AKH = self.AQH // A_GROUP
        self.AKW = self.AKH * A_HEAD
        self.BW = d_model // 4
        self.CH = (d_model - self.AW - self.BW) // C_HEAD
        self.CW = self.CH * C_HEAD
        AW, AKW, BW, CW, CH = self.AW, self.AKW, self.BW, self.CW, self.CH
        orig = [('qa', AW), ('ka', AKW), ('va', AKW), ('za', AW), ('ub', 2 * BW), ('zb', BW),
                ('qc', CW), ('kc', CW), ('vc', CW), ('bc', CH), ('ac', CH), ('zc', CW)]
        self.orig = {}
        off = 0
        for n, w in orig:
            self.orig[n] = (off, w)
            off += w
        self.IN_COLS = off
        order = ['qa', 'za', 'qc', 'kc', 'vc', 'zc', 'ka', 'va', 'ub', 'zb', 'bc', 'ac']
        self.order = order
        self.g = {}
        off = 0
        for n in order:
            w = self.orig[n][1]
            wp = LANE if n in ('bc', 'ac') else w
            assert off % wp == 0, (n, off, wp)
            self.g[n] = (off, wp)
            off += wp
        self.WP = off

    def blk(self, name):
        off, w = self.g[name]
        return off // w


def _cparams(sem, vmem=VMEM_LIMIT):
    return pltpu.CompilerParams(dimension_semantics=sem, vmem_limit_bytes=vmem)


def _silu(x):
    return x * jax.nn.sigmoid(x)


ANY = pl.BlockSpec(memory_space=pl.ANY)


class Comm:
    def __init__(self, ins, out_shapes, sems, start, finish, aliases=None):
        self.ins, self.out_shapes, self.sems = list(ins), list(out_shapes), list(sems)
        self.start, self.finish, self.aliases = start, finish, dict(aliases or {})


def call_with_comm(body, name, grid, in_specs, out_specs, out_shape, scratch_shapes, semantics, args, comm=None):
    in_specs, out_specs, out_shape, scratch_shapes = list(in_specs), list(out_specs), list(out_shape), list(scratch_shapes)
    if comm is None:
        outs = pl.pallas_call(body, name=name, grid=grid, in_specs=in_specs, out_specs=out_specs, out_shape=out_shape,
                              scratch_shapes=scratch_shapes, compiler_params=_cparams(semantics))(*args)
        return list(outs), []
    ni, no, ns = len(in_specs), len(out_specs), len(scratch_shapes)
    nci, nco = len(comm.ins), len(comm.out_shapes)

    def wrapped(*refs):
        h_in, c_in = refs[:ni], refs[ni:ni + nci]
        h_out, c_out = refs[ni + nci:ni + nci + no], refs[ni + nci + no:ni + nci + no + nco]
        h_scr, c_sems = refs[ni + nci + no + nco:ni + nci + no + nco + ns], refs[ni + nci + no + nco + ns:]
        ids = [pl.program_id(d) for d in range(len(grid))]
        first = functools.reduce(jnp.logical_and, [i == 0 for i in ids])
        last = functools.reduce(jnp.logical_and, [i == g - 1 for i, g in zip(ids, grid)])

        @pl.when(first)
        def _():
            comm.start(c_in, c_out, c_sems)

        body(*h_in, *h_out, *h_scr)

        @pl.when(last)
        def _():
            comm.finish(c_in, c_out, c_sems)

    outs = pl.pallas_call(
        wrapped, name=name, grid=grid, in_specs=in_specs + [ANY] * nci, out_specs=out_specs + [ANY] * nco,
        out_shape=out_shape + comm.out_shapes, scratch_shapes=scratch_shapes + comm.sems,
        input_output_aliases={ni + k: no + v for k, v in comm.aliases.items()},
        compiler_params=_cparams(("arbitrary",) * len(grid)),
    )(*args, *comm.ins)
    return list(outs[:no]), list(outs[no:])


def run_comm(name, comm):
    nci, nco = len(comm.ins), len(comm.out_shapes)

    def body(*refs):
        c_in, c_out, c_sems = refs[:nci], refs[nci:nci + nco], refs[nci + nco:]
        comm.start(c_in, c_out, c_sems)
        comm.finish(c_in, c_out, c_sems)

    return pl.pallas_call(
        body, name=name, in_specs=[ANY] * nci, out_specs=[ANY] * nco, out_shape=comm.out_shapes,
        scratch_shapes=comm.sems, input_output_aliases=comm.aliases,
    )(*comm.ins)


def _bdot(a, b, ca, cb, precision=HI):
    dims = (((ca,), (cb,)), ((0,), (0,)))
    if precision is HI and a.dtype == F32:
        ah = a.astype(BF16)
        bh = b.astype(BF16)
        al = (a - ah.astype(F32)).astype(BF16)
        bl = (b - bh.astype(F32)).astype(BF16)
        dg = lambda p, q: lax.dot_general(p, q, dims, preferred_element_type=F32)
        return dg(ah, bh) + (dg(ah, bl) + dg(al, bh))
    return lax.dot_general(a, b, dims, precision=precision, preferred_element_type=F32)


def _rope_matrix(nb):
    i = lax.broadcasted_iota(jnp.int32, (nb, A_HEAD, A_HEAD), 1)
    j = lax.broadcasted_iota(jnp.int32, (nb, A_HEAD, A_HEAD), 2)
    half = ROT_DIM // 2
    neg = (j < half) & (i == j + half)
    pos = (j >= half) & (j < ROT_DIM) & (i == j - half)
    return jnp.where(neg, -1.0, jnp.where(pos, 1.0, 0.0)).astype(F32)


def _norm_rope(xh, w, cos, sin):
    y = xh * lax.rsqrt(jnp.mean(xh * xh, axis=-1, keepdims=True) + EPS) * w
    return y * cos + _bdot(y, _rope_matrix(xh.shape[0]), 2, 1) * sin


def attn_block(cfg, first, q, za, kc, vc, cosc, sinc, kp, vp, cosp, sinp, qnw, knw, sinks_row):
    blk = ATTN_BLOCK
    nq, nk = cfg.AQH, cfg.AKH
    qi = lax.broadcasted_iota(jnp.int32, (blk, 2 * blk), 0)
    kj = lax.broadcasted_iota(jnp.int32, (blk, 2 * blk), 1)
    dist = qi + blk - kj
    valid = ((dist >= 0) & (dist < blk) & (jnp.logical_not(first) | (kj >= blk)))[None]
    cos2 = jnp.concatenate([cosp, cosc], axis=0)
    sin2 = jnp.concatenate([sinp, sinc], axis=0)
    head = lambda x, h: x[:, A_HEAD * h:A_HEAD * (h + 1)]
    k2 = jnp.stack([jnp.concatenate([head(kp, h), head(kc, h)], axis=0) for h in range(nk)], axis=0)
    v2 = jnp.stack([jnp.concatenate([head(vp, h), head(vc, h)], axis=0) for h in range(nk)], axis=0)
    k2 = _norm_rope(k2, knw[None], cos2[None], sin2[None]).astype(BF16)
    v2 = v2.astype(BF16)
    k2 = jnp.stack([k2[h // A_GROUP] for h in range(nq)], axis=0)
    v2 = jnp.stack([v2[h // A_GROUP] for h in range(nq)], axis=0)
    qh = jnp.stack([head(q, h) for h in range(nq)], axis=0)
    qh = _norm_rope(qh, qnw[None], cosc[None], sinc[None]).astype(BF16)
    s = _bdot(qh, k2, 2, 2, None) * (A_HEAD ** -0.5)
    s = jnp.where(valid, s, -1e30)
    sink = jnp.stack([sinks_row[:, A_HEAD * h:A_HEAD * h + 1] for h in range(nq)], axis=0)
    m = jnp.maximum(jnp.max(s, axis=-1, keepdims=True), sink)
    e = jnp.exp(s - m)
    den = jnp.sum(e, axis=-1, keepdims=True) + jnp.exp(sink - m)
    o = _bdot((e / den).astype(BF16), v2, 2, 1, None)
    return (jnp.concatenate([o[h] for h in range(nq)], axis=1) * _silu(za),)


def conv_block(cfg, first, u, zb, uh, cw, cb, lw, lb, pw, pb):
    BW = cfg.BW
    tb = u.shape[0]
    uu = jnp.concatenate([uh, u], axis=0)
    h = uu[:, :BW] * jax.nn.sigmoid(uu[:, BW:])
    row = lax.broadcasted_iota(jnp.int32, h.shape, 0)
    h = jnp.where(first & (row < B_HALO), 0.0, h)
    acc = jnp.zeros((tb, BW), F32) + cb
    base = B_HALO - (B_CONV - 1)
    for k in range(B_CONV):
        acc = acc + cw[k:k + 1, :] * h[base + k:base + k + tb, :]
    mu = jnp.mean(acc, axis=-1, keepdims=True)
    var = jnp.mean(jnp.square(acc - mu), axis=-1, keepdims=True)
    y = (acc - mu) * lax.rsqrt(var + EPS) * lw + lb
    s = _silu(y)
    o = jnp.dot(s.astype(BF16), pw.astype(BF16), preferred_element_type=F32) + pb
    return (o * _silu(zb),)


def gdn_prep_block(cfg, first, xq, xk, xv, braw, araw, hq, hk, hv, cw, alog, dtb):
    CW = cfg.CW
    tb = xq.shape[0]
    outs = []
    for idx, (x, xh) in enumerate(((xq, hq), (xk, hk), (xv, hv))):
        xx = jnp.concatenate([jnp.where(first, 0.0, xh), x], axis=0)
        w = cw[:, idx * CW:(idx + 1) * CW]
        acc = jnp.zeros((tb, CW), F32)
        base = C_HALO - (C_CONV - 1)
        for k in range(C_CONV):
            acc = acc + w[k:k + 1, :] * xx[base + k:base + k + tb, :]
        y = _silu(acc)
        if idx < 2:
            parts = []
            for h in range(cfg.CH):
                yh = y[:, C_HEAD * h:C_HEAD * (h + 1)]
                parts.append(yh * lax.rsqrt(jnp.sum(yh * yh, axis=-1, keepdims=True) + EPS))
            y = jnp.concatenate(parts, axis=1)
        outs.append(y)
    beta = jax.nn.sigmoid(braw)
    g = -jnp.exp(alog) * jax.nn.softplus(araw + dtb)
    return outs[0], outs[1], outs[2], g, beta


def _inverse_unit_lower(low, eye):
    pw = low
    inv = eye - low
    for _ in range(5):
        pwb = pw.astype(BF16)
        pw = _bdot(pwb, pwb, 2, 1, None)
        inv = inv + _bdot(inv.astype(BF16), pw.astype(BF16), 2, 1, None)
    ax = inv + _bdot(low, inv, 2, 1)
    return inv + _bdot(inv, eye - ax, 2, 1)


@jax.custom_vjp
def _saved_inverse(low, inv):
    return inv


def _saved_inverse_fwd(low, inv):
    return inv, inv


def _saved_inverse_bwd(inv, d):
    dlow = -_bdot(_bdot(inv, d, 1, 1), inv, 2, 2)
    return dlow, jnp.zeros_like(inv)


_saved_inverse.defvjp(_saved_inverse_fwd, _saved_inverse_bwd)


def gdn_intra_rows(cfg, first, qn, kn, v, g, beta, inv_saved=None):
    c = CHUNK
    CH = cfg.CH
    nchunk = qn.shape[0] // c
    i = lax.broadcasted_iota(jnp.int32, (c, c), 0)
    j = lax.broadcasted_iota(jnp.int32, (c, c), 1)
    incl = (i >= j)[None]
    strict = (i > j)[None]
    eye = (i == j).astype(F32)[None]
    tri = (i >= j).astype(F32)
    rows = [slice(c * ci, c * (ci + 1)) for ci in range(nchunk)]
    gcs = [jnp.dot(tri, g[r], precision=HI, preferred_element_type=F32) for r in rows]
    pairs = [(ci, h) for ci in range(nchunk) for h in range(CH)]
    heads = lambda x, wd: jnp.stack([x[rows[ci], wd * h:wd * (h + 1)] for ci, h in pairs], axis=0)
    gch = jnp.stack([gcs[ci][:, h:h + 1] for ci, h in pairs], axis=0)
    bh = jnp.stack([beta[rows[ci], h:h + 1] for ci, h in pairs], axis=0)
    q = heads(qn, C_HEAD) * (C_HEAD ** -0.5)
    k = heads(kn, C_HEAD)
    vv = heads(v, C_HEAD)
    a = jnp.broadcast_to(gch, (len(pairs), c, c))
    diff = jnp.where(incl, a - jnp.swapaxes(a, 1, 2), 0.0)
    decay = jnp.where(incl, jnp.exp(diff), 0.0)
    kb = k * bh
    low = jnp.where(strict, _bdot(kb, k, 2, 2) * decay, 0.0)
    if inv_saved is None:
        inv = _inverse_unit_lower(low, eye)
    else:
        inv = _saved_inverse(low, heads(inv_saved, c))
    eg = jnp.exp(gch)
    sol = _bdot(inv, jnp.concatenate([vv * bh, kb * eg], axis=2), 2, 1)
    intra = jnp.where(incl, _bdot(q, k, 2, 2) * decay, 0.0)
    qg = q * eg
    kd = k * jnp.exp(gch[:, c - 1:c, :] - gch)
    glast = jnp.concatenate([jnp.broadcast_to(gc[c - 1:c, :], gc.shape) for gc in gcs], axis=0)

    def unstack(x):
        return jnp.concatenate([jnp.concatenate([x[ci * CH + h] for h in range(CH)], axis=1) for ci in range(nchunk)],
                               axis=0)

    outs = (unstack(sol[:, :, :C_HEAD]), unstack(sol[:, :, C_HEAD:]), unstack(qg), unstack(kd), unstack(intra), glast)
    return outs + (unstack(inv),) if inv_saved is None else outs


def gdn_state_step(S, u, w, qg, kd, intra, glast):
    v_new = u - _bdot(w, S, 2, 1)
    o = _bdot(qg, S, 2, 1) + _bdot(intra, v_new, 2, 1)
    S_next = S * jnp.exp(glast) + _bdot(kd, v_new, 1, 1)
    return o, S_next


def gdn_out_block(cfg, first, o, zc, onw):
    parts = []
    for h in range(cfg.CH):
        sl = slice(C_HEAD * h, C_HEAD * (h + 1))
        oh = o[:, sl]
        y = oh * lax.rsqrt(jnp.mean(oh * oh, axis=-1, keepdims=True) + EPS) * onw
        parts.append(y * _silu(zc[:, sl]))
    return (jnp.concatenate(parts, axis=1),)


def rms_block(x, nw):
    return x * lax.rsqrt(jnp.mean(x * x, axis=-1, keepdims=True) + EPS) * nw


class Row:
    def __init__(self, arr, width, colblk=0, grad=None):
        self.arr, self.width, self.colblk, self.grad = arr, width, colblk, grad


class Halo:
    def __init__(self, arr, width, colblk, hr, tie=None):
        self.arr, self.width, self.colblk, self.hr, self.tie = arr, width, colblk, hr, tie


def _row_specs(tb, rows, halos, params, pos):
    specs = [pl.BlockSpec((tb, r.width), lambda i, cb=r.colblk: (pos(i), cb)) for r in rows]
    specs += [pl.BlockSpec((h.hr, h.width),
                           lambda i, cb=h.colblk, m=tb // h.hr: (jnp.maximum(pos(i) * m - 1, 0), cb))
              for h in halos]
    specs += [pl.BlockSpec(p.shape, lambda i: (0, 0)) for p in params]
    return specs


def rb_fwd(name, fn, n, tb, bps, rows, halos, params, outs, comm=None):
    nr, nh, npar = len(rows), len(halos), len(params)

    def body(*refs):
        ins = refs[:nr + nh + npar]
        o_refs = refs[nr + nh + npar:]
        first = (pl.program_id(0) % bps) == 0
        res = fn(first, *[r[...] for r in ins])
        for ref, val in zip(o_refs, res):
            ref[...] = val.astype(ref.dtype)

    res, carried = call_with_comm(
        body, name, (n // tb,), _row_specs(tb, rows, halos, params, lambda i: i),
        [pl.BlockSpec((tb, w), lambda i: (i, 0)) for w, _ in outs],
        [jax.ShapeDtypeStruct((n, w), dt) for w, dt in outs], [], ("parallel",),
        [r.arr for r in rows] + [h.arr for h in halos] + list(params), comm)
    return (res, carried) if comm is not None else res


def rb_bwd(name, fn, n, tb, bps, rows, halos, params, douts, param_grads, comm=None):
    nr, nh, npar, nd = len(rows), len(halos), len(params), len(douts)
    nblk = n // tb
    grow = [k for k, r in enumerate(rows) if r.grad is not None]
    ghalo = [k for k, h in enumerate(halos) if h.tie is not None]
    gpar = [k for k, f in enumerate(param_grads) if f]
    pos = lambda i: nblk - 1 - i

    def body(*refs):
        ins = refs[:nr + nh + npar]
        d_refs = refs[nr + nh + npar:nr + nh + npar + nd]
        rest = refs[nr + nh + npar + nd:]
        grow_refs = rest[:len(grow)]
        gpar_refs = rest[len(grow):len(grow) + len(gpar)]
        carry_refs = rest[len(grow) + len(gpar):]
        i = pl.program_id(0)
        first = (pos(i) % bps) == 0
        vals = [r[...] for r in ins]
        diff_idx = grow + [nr + k for k in ghalo] + [nr + nh + k for k in gpar]

        def f(*dargs):
            full = list(vals)
            for k, a in zip(diff_idx, dargs):
                full[k] = a
            return fn(first, *full)

        res, vjp = jax.vjp(f, *[vals[k] for k in diff_idx])
        grads = vjp(tuple(d[...].astype(r.dtype) for d, r in zip(d_refs, res)))
        g_rows = list(grads[:len(grow)])
        g_halos = grads[len(grow):len(grow) + len(ghalo)]
        g_pars = grads[len(grow) + len(ghalo):]

        @pl.when(i == 0)
        def _():
            for c in carry_refs:
                c[...] = jnp.zeros_like(c)
            for p in gpar_refs:
                p[...] = jnp.zeros_like(p)

        for k, ref in enumerate(grow_refs):
            ref[...] = g_rows[k].astype(ref.dtype)
        for ci, hk in enumerate(ghalo):
            h = halos[hk]
            k = grow.index(h.tie)
            tail = g_rows[k][tb - h.hr:, :] + carry_refs[ci][...]
            grow_refs[k][tb - h.hr:, :] = tail.astype(grow_refs[k].dtype)
            carry_refs[ci][...] = g_halos[ci]
        for ref, gp in zip(gpar_refs, g_pars):
            ref[...] += gp

    out_specs = [pl.BlockSpec((tb, rows[k].width), lambda i: (pos(i), 0)) for k in grow]
    out_specs += [pl.BlockSpec(params[k].shape, lambda i: (0, 0)) for k in gpar]
    out_shape = [jax.ShapeDtypeStruct((n, rows[k].width), rows[k].grad) for k in grow]
    out_shape += [jax.ShapeDtypeStruct(params[k].shape, F32) for k in gpar]
    in_specs = _row_specs(tb, rows, halos, params, pos)
    in_specs += [pl.BlockSpec((tb, d.shape[1]), lambda i: (pos(i), 0)) for d in douts]
    res, carried = call_with_comm(
        body, name, (nblk,), in_specs, out_specs, out_shape,
        [pltpu.VMEM((halos[k].hr, halos[k].width), F32) for k in ghalo], ("arbitrary",),
        [r.arr for r in rows] + [h.arr for h in halos] + list(params) + list(douts), comm)
    return (res, carried) if comm is not None else res


_DIMS = {'nn': (((1,), (0,)), ((), ())), 'nt': (((1,), (1,)), ((), ())), 'tn': (((0,), (0,)), ((), ()))}


def matmul(name, a, b, mode, tm, tn, tk, out_dtype=F32, add=None, comm=None):
    if mode == 'tn':
        K, M = a.shape
    else:
        M, K = a.shape
    N = b.shape[0] if mode == 'nt' else b.shape[1]
    tm, tn, tk = min(tm, M), min(tn, N), min(tk, K)
    assert M % tm == 0 and N % tn == 0 and K % tk == 0, (name, M, N, K, tm, tn, tk)
    nk = K // tk
    a_spec = pl.BlockSpec((tk, tm), lambda i, j, k: (k, i)) if mode == 'tn' else pl.BlockSpec((tm, tk), lambda i, j, k: (i, k))
    b_spec = pl.BlockSpec((tn, tk), lambda i, j, k: (j, k)) if mode == 'nt' else pl.BlockSpec((tk, tn), lambda i, j, k: (k, j))
    o_spec = pl.BlockSpec((tm, tn), lambda i, j, k: (i, j))
    has_add = add is not None

    def body(*refs):
        a_ref, b_ref = refs[0], refs[1]
        add_ref = refs[2] if has_add else None
        o_ref = refs[-1]
        k = pl.program_id(2)
        part = lax.dot_general(a_ref[...].astype(BF16), b_ref[...].astype(BF16), _DIMS[mode], preferred_element_type=F32)

        @pl.when(k == 0)
        def _():
            o_ref[...] = ((part + add_ref[...]) if has_add else part).astype(o_ref.dtype)

        if nk > 1:
            @pl.when(k > 0)
            def _():
                o_ref[...] += part

    assert nk == 1 or out_dtype == F32
    ins = [a, b] + ([add] if has_add else [])
    in_specs = [a_spec, b_spec] + ([o_spec] if has_add else [])
    outs, couts = call_with_comm(body, name, (M // tm, N // tn, nk), in_specs, [o_spec],
                                 [jax.ShapeDtypeStruct((M, N), out_dtype)], [], ("parallel", "parallel", "arbitrary"),
                                 ins, comm)
    return (outs[0], couts) if comm is not None else outs[0]


def grad_w_blocks(name, h, dpb, tm, tk):
    n, d = h.shape
    nb, _, s = dpb.shape
    tm, tk = min(tm, d), min(tk, n)
    assert d % tm == 0 and n % tk == 0
    nk = n // tk

    def body(h_ref, b_ref, o_ref):
        k = pl.program_id(2)
        part = lax.dot_general(h_ref[...], b_ref[0], _DIMS['tn'], preferred_element_type=F32)

        @pl.when(k == 0)
        def _():
            o_ref[0] = part

        if nk > 1:
            @pl.when(k > 0)
            def _():
                o_ref[0] += part

    return pl.pallas_call(
        body, name=name, grid=(nb, d // tm, nk),
        in_specs=[pl.BlockSpec((tk, tm), lambda j, i, k: (k, i)), pl.BlockSpec((1, tk, s), lambda j, i, k: (j, k, 0))],
        out_specs=pl.BlockSpec((1, tm, s), lambda j, i, k: (j, i, 0)),
        out_shape=jax.ShapeDtypeStruct((nb, d, s), F32),
        compiler_params=_cparams(("parallel", "parallel", "arbitrary")),
    )(h, dpb)


def norm_in_proj(name, x, nw, wp, tm, tn, comm=None):
    n, d = x.shape
    wpc = wp.shape[1]
    tm, tn = min(tm, n), min(tn, wpc)
    assert n % tm == 0 and wpc % tn == 0

    def body(x_ref, nw_ref, w_ref, p_ref, h_ref):
        @pl.when(pl.program_id(1) == 0)
        def _():
            h_ref[...] = rms_block(x_ref[...], nw_ref[...]).astype(BF16)

        p_ref[...] = jnp.dot(h_ref[...], w_ref[...], preferred_element_type=F32)

    outs, couts = call_with_comm(
        body, name, (n // tm, wpc // tn),
        [pl.BlockSpec((tm, d), lambda i, j: (i, 0)), pl.BlockSpec((1, d), lambda i, j: (0, 0)),
         pl.BlockSpec((d, tn), lambda i, j: (0, j))],
        [pl.BlockSpec((tm, tn), lambda i, j: (i, j)), pl.BlockSpec((tm, d), lambda i, j: (i, 0))],
        [jax.ShapeDtypeStruct((n, wpc), F32), jax.ShapeDtypeStruct((n, d), BF16)], [], ("parallel", "arbitrary"),
        [x, nw, wp], comm)
    return (outs[0], outs[1], couts) if comm is not None else (outs[0], outs[1])


def norm_bwd(name, x, nw, dh, dres, tb, comm=None):
    n, d = x.shape
    tb = min(tb, n)

    def body(x_ref, nw_ref, dh_ref, dres_ref, dx_ref, dnw_ref):
        @pl.when(pl.program_id(0) == 0)
        def _():
            dnw_ref[...] = jnp.zeros_like(dnw_ref)

        _, vjp = jax.vjp(rms_block, x_ref[...], nw_ref[...])
        dx, dnw = vjp(dh_ref[...])
        dx_ref[...] = dx + dres_ref[...]
        dnw_ref[...] += dnw

    row = pl.BlockSpec((tb, d), lambda i: (i, 0))
    par = pl.BlockSpec((1, d), lambda i: (0, 0))
    outs, carried = call_with_comm(
        body, name, (n // tb,), [row, par, row, row], [row, par],
        [jax.ShapeDtypeStruct((n, d), F32), jax.ShapeDtypeStruct((1, d), F32)], [], ("arbitrary",),
        [x, nw, dh, dres], comm)
    return (outs[0], outs[1], carried) if comm is not None else (outs[0], outs[1])


def out_proj_loss(name, y, wo, x, target, tm):
    n, d = x.shape
    tm = min(tm, n)
    assert n % tm == 0

    def body(y_ref, w_ref, x_ref, t_ref, dz_ref, loss_ref):
        @pl.when(pl.program_id(0) == 0)
        def _():
            loss_ref[...] = jnp.zeros_like(loss_ref)

        z = x_ref[...] + jnp.dot(y_ref[...], w_ref[...], preferred_element_type=F32)
        err = z - t_ref[...]
        dz_ref[...] = err * (1.0 / d)
        part = 0.5 * jnp.sum(jnp.mean(err * err, axis=-1, keepdims=True), axis=0, keepdims=True)
        loss_ref[...] += jnp.broadcast_to(part, loss_ref.shape)

    row = pl.BlockSpec((tm, d), lambda i: (i, 0))
    dz, loss = pl.pallas_call(
        body, name=name, grid=(n // tm,),
        in_specs=[pl.BlockSpec((tm, y.shape[1]), lambda i: (i, 0)), pl.BlockSpec(wo.shape, lambda i: (0, 0)), row, row],
        out_specs=[row, pl.BlockSpec((8, LANE), lambda i: (0, 0))],
        out_shape=[jax.ShapeDtypeStruct((n, d), F32), jax.ShapeDtypeStruct((8, LANE), F32)],
        compiler_params=_cparams(("arbitrary",)),
    )(y, wo, x, target)
    return dz, loss[0, 0]


def rope_tables(name, pos_col, inv_freq_row):
    n = pos_col.shape[0]

    def body(p_ref, f_ref, c_ref, s_ref):
        ang = p_ref[...].astype(F32) * f_ref[...]
        lane = lax.broadcasted_iota(jnp.int32, ang.shape, 1)
        c_ref[...] = jnp.where(lane < ROT_DIM, jnp.cos(ang), 1.0)
        s_ref[...] = jnp.where(lane < ROT_DIM, jnp.sin(ang), 0.0)

    return pl.pallas_call(
        body, name=name, out_shape=[jax.ShapeDtypeStruct((n, A_HEAD), F32)] * 2,
    )(pos_col, inv_freq_row)


def _scan_operands(cfg, nseq, u_ref, w_ref, qg_ref, kd_ref, a_ref, gl_ref):
    pairs = [(b, h) for b in range(nseq) for h in range(cfg.CH)]
    st = lambda r, wd: jnp.stack([r[b, :, wd * h:wd * (h + 1)] for b, h in pairs], axis=0)
    gl = jnp.stack([gl_ref[b, 0:1, h:h + 1] for b, h in pairs], axis=0)
    return st(u_ref, C_HEAD), st(w_ref, C_HEAD), st(qg_ref, C_HEAD), st(kd_ref, C_HEAD), st(a_ref, CHUNK), gl


def gdn_scan_fwd(name, cfg, nseq, u, w, qg, kd, intra, glast):
    CH, CW, T = cfg.CH, cfg.CW, cfg.T
    nc = T // CHUNK

    def body(u_ref, w_ref, qg_ref, kd_ref, a_ref, gl_ref, o_ref, sin_ref, s_ref):
        @pl.when(pl.program_id(0) == 0)
        def _():
            s_ref[...] = jnp.zeros_like(s_ref)

        S = s_ref[...]
        for b in range(nseq):
            sin_ref[b, 0] = S[b * CH:(b + 1) * CH]
        o, S_next = gdn_state_step(S, *_scan_operands(cfg, nseq, u_ref, w_ref, qg_ref, kd_ref, a_ref, gl_ref))
        s_ref[...] = S_next
        for b in range(nseq):
            o_ref[b] = jnp.concatenate([o[b * CH + h] for h in range(CH)], axis=1)

    row = lambda wd: pl.BlockSpec((nseq, CHUNK, wd), lambda c: (0, c, 0))
    widths = [CW, CW, CW, CW, CH * CHUNK, LANE]
    o, s_in = pl.pallas_call(
        body, name=name, grid=(nc,),
        in_specs=[row(x) for x in widths],
        out_specs=[row(CW), pl.BlockSpec((nseq, 1, CH, C_HEAD, C_HEAD), lambda c: (0, c, 0, 0, 0))],
        out_shape=[jax.ShapeDtypeStruct((nseq, T, CW), F32),
                   jax.ShapeDtypeStruct((nseq, nc, CH, C_HEAD, C_HEAD), F32)],
        scratch_shapes=[pltpu.VMEM((nseq * CH, C_HEAD, C_HEAD), F32)],
        compiler_params=_cparams(("arbitrary",)),
    )(*[a.reshape(nseq, T, a.shape[1]) for a in (u, w, qg, kd, intra, glast)])
    return o.reshape(nseq * T, CW), s_in


def gdn_scan_bwd(name, cfg, nseq, u, w, qg, kd, intra, glast, s_in, do, comm=None):
    CH, CW, T = cfg.CH, cfg.CW, cfg.T
    nc = T // CHUNK

    def body(u_ref, w_ref, qg_ref, kd_ref, a_ref, gl_ref, sin_ref, do_ref,
             du_ref, dw_ref, dqg_ref, dkd_ref, da_ref, dgl_ref, ds_ref):
        @pl.when(pl.program_id(0) == 0)
        def _():
            ds_ref[...] = jnp.zeros_like(ds_ref)

        S = jnp.concatenate([sin_ref[b, 0] for b in range(nseq)], axis=0)
        dout = jnp.stack([do_ref[b, :, C_HEAD * h:C_HEAD * (h + 1)] for b in range(nseq) for h in range(CH)], axis=0)
        _, vjp = jax.vjp(gdn_state_step, S, *_scan_operands(cfg, nseq, u_ref, w_ref, qg_ref, kd_ref, a_ref, gl_ref))
        dS, du, dw, dqg, dkd, da, dg = vjp((dout, ds_ref[...]))
        ds_ref[...] = dS
        lane = lax.broadcasted_iota(jnp.int32, (CHUNK, LANE), 1)
        rowi = lax.broadcasted_iota(jnp.int32, (CHUNK, LANE), 0)
        for b in range(nseq):
            cat = lambda x: jnp.concatenate([x[b * CH + h] for h in range(CH)], axis=1)
            du_ref[b] = cat(du)
            dw_ref[b] = cat(dw)
            dqg_ref[b] = cat(dqg)
            dkd_ref[b] = cat(dkd)
            da_ref[b] = cat(da)
            dgl = jnp.zeros((CHUNK, LANE), F32)
            for h in range(CH):
                dgl = dgl + jnp.where((lane == h) & (rowi == 0), dg[b * CH + h], 0.0)
            dgl_ref[b] = dgl

    row = lambda wd: pl.BlockSpec((nseq, CHUNK, wd), lambda c: (0, nc - 1 - c, 0))
    widths = [CW, CW, CW, CW, CH * CHUNK, LANE]
    outs, carried = call_with_comm(
        body, name, (nc,),
        [row(x) for x in widths]
        + [pl.BlockSpec((nseq, 1, CH, C_HEAD, C_HEAD), lambda c: (0, nc - 1 - c, 0, 0, 0)), row(CW)],
        [row(x) for x in widths], [jax.ShapeDtypeStruct((nseq, T, x), F32) for x in widths],
        [pltpu.VMEM((nseq * CH, C_HEAD, C_HEAD), F32)], ("arbitrary",),
        [a.reshape(nseq, T, a.shape[1]) for a in (u, w, qg, kd, intra, glast)] + [s_in, do.reshape(nseq, T, CW)], comm)
    return [o.reshape(nseq * T, o.shape[2]) for o in outs], carried


def _tile(total, cap, unit=LANE):
    best = None
    for t in range(unit, min(cap, total) + 1, unit):
        if total % t == 0:
            best = t
    assert best is not None, (total, cap, unit)
    return best


def _pad_lanes(v, width=LANE):
    return jnp.pad(v.reshape(1, -1), ((0, 0), (0, width - v.shape[-1])))


def permute_w_in(cfg, w):
    parts = []
    for n in cfg.order:
        off, wd = cfg.orig[n]
        blk = w[:, off:off + wd]
        if cfg.g[n][1] != wd:
            blk = jnp.pad(blk, ((0, 0), (0, cfg.g[n][1] - wd)))
        parts.append(blk)
    return jnp.concatenate(parts, axis=1)


def chip_blocks(cfg, groups, n_chips):
    s = cfg.IN_COLS // n_chips
    blocks = []
    for j in range(n_chips):
        lo, hi = j * s, (j + 1) * s
        pieces = []
        for name, (off, wd) in cfg.orig.items():
            a, b = max(lo, off), min(hi, off + wd)
            if a < b:
                pieces.append(groups[name][:, a - off:b - off])
        blocks.append(jnp.concatenate(pieces, axis=1))
    return jnp.stack(blocks)


def _layer_params(cfg, prm):
    return dict(
        nw=prm['norm_w'].reshape(1, -1),
        qnw=prm['q_norm_w'].reshape(1, -1), knw=prm['k_norm_w'].reshape(1, -1),
        sinks_row=jnp.repeat(prm['sinks'], A_HEAD).reshape(1, -1),
        cw=prm['b_conv_w'], cb=prm['b_conv_b'].reshape(1, -1),
        lw=prm['b_ln_w'].reshape(1, -1), lb=prm['b_ln_b'].reshape(1, -1),
        pw=prm['b_pw_w'], pb=prm['b_pw_b'].reshape(1, -1),
        ccw=prm['c_conv_w'], alog=_pad_lanes(prm['c_a_log']), dtb=_pad_lanes(prm['c_dt_bias']),
        onw=prm['c_onorm_w'].reshape(1, -1),
    )


def _attn_io(cfg, p, cos, sin, grads):
    gq = BF16 if grads else None
    rows = [Row(p, cfg.AW, cfg.blk('qa'), gq), Row(p, cfg.AW, cfg.blk('za'), gq),
            Row(p, cfg.AKW, cfg.blk('ka'), gq), Row(p, cfg.AKW, cfg.blk('va'), gq),
            Row(cos, A_HEAD), Row(sin, A_HEAD)]
    halos = [Halo(p, cfg.AKW, cfg.blk('ka'), ATTN_BLOCK, 2 if grads else None),
             Halo(p, cfg.AKW, cfg.blk('va'), ATTN_BLOCK, 3 if grads else None),
             Halo(cos, A_HEAD, 0, ATTN_BLOCK), Halo(sin, A_HEAD, 0, ATTN_BLOCK)]
    return rows, halos


def _conv_io(cfg, p, grads):
    gq = BF16 if grads else None
    rows = [Row(p, 2 * cfg.BW, cfg.blk('ub'), gq), Row(p, cfg.BW, cfg.blk('zb'), gq)]
    halos = [Halo(p, 2 * cfg.BW, cfg.blk('ub'), B_HALO, 0 if grads else None)]
    return rows, halos


def _prep_io(cfg, p, grads):
    gq = BF16 if grads else None
    rows = [Row(p, cfg.CW, cfg.blk(n), gq) for n in ('qc', 'kc', 'vc')]
    rows += [Row(p, LANE, cfg.blk('bc'), gq), Row(p, LANE, cfg.blk('ac'), gq)]
    halos = [Halo(p, cfg.CW, cfg.blk(n), C_HALO, k if grads else None) for k, n in enumerate(('qc', 'kc', 'vc'))]
    return rows, halos


TB_CONV = 128
TB_PREP = 256
TB_OUT = 256
TB_INTRA_FWD = 128
TB_INTRA_BWD = 128


def layer_forward(cfg, l, x, lp, wp, wo, cos, sin, comms=None, target=None):
    n = x.shape[0]
    nseq = n // cfg.T
    T = cfg.T
    comms = comms or {}
    carried = {}

    def hosted(key, res):
        if comms.get(key) is None:
            return res
        res, carried[key] = res
        return res

    nw = lp['nw'] if isinstance(lp, dict) else lp[0]
    p, h, *rode = norm_in_proj(f"in_proj_{l}", x, nw, wp, 1024, _tile(cfg.WP, 768), comm=comms.get('in_proj'))
    if rode:
        carried['in_proj'] = rode[0]
    if not isinstance(lp, dict):
        lp, wo = lp[1](carried['in_proj'])
    rows, halos = _attn_io(cfg, p, cos, sin, False)
    (oa,) = hosted('attn', rb_fwd(f"attn_fwd_{l}", functools.partial(attn_block, cfg), n, ATTN_BLOCK, T // ATTN_BLOCK,
                                  rows, halos, [lp['qnw'], lp['knw'], lp['sinks_row']], [(cfg.AW, BF16)],
                                  comm=comms.get('attn')))
    rows, halos = _conv_io(cfg, p, False)
    tbb = min(TB_CONV, T)
    (ob,) = hosted('conv', rb_fwd(f"conv_fwd_{l}", functools.partial(conv_block, cfg), n, tbb, T // tbb, rows, halos,
                                  [lp['cw'], lp['cb'], lp['lw'], lp['lb'], lp['pw'], lp['pb']], [(cfg.BW, BF16)],
                                  comm=comms.get('conv')))
    rows, halos = _prep_io(cfg, p, False)
    tbp = min(TB_PREP, T)
    qn, kn, v, g, beta = rb_fwd(f"gdn_prep_fwd_{l}", functools.partial(gdn_prep_block, cfg), n, tbp, T // tbp, rows,
                                halos, [lp['ccw'], lp['alog'], lp['dtb']],
                                [(cfg.CW, F32)] * 3 + [(LANE, F32)] * 2)
    intra_outs = hosted('intra', rb_fwd(
        f"gdn_intra_fwd_{l}", functools.partial(gdn_intra_rows, cfg), n, TB_INTRA_FWD, T // TB_INTRA_FWD,
        [Row(qn, cfg.CW), Row(kn, cfg.CW), Row(v, cfg.CW), Row(g, LANE), Row(beta, LANE)], [], [],
        [(cfg.CW, F32)] * 4 + [(cfg.CH * CHUNK, F32), (LANE, F32), (cfg.CH * CHUNK, F32)], comm=comms.get('intra')))
    intra_outs, inv = intra_outs[:6], intra_outs[6]
    o, s_in = gdn_scan_fwd(f"gdn_scan_fwd_{l}", cfg, nseq, *intra_outs)
    tbo = min(TB_OUT, T)
    (oc,) = rb_fwd(f"gdn_out_fwd_{l}", functools.partial(gdn_out_block, cfg), n, tbo, T // tbo,
                   [Row(o, cfg.CW), Row(p, cfg.CW, cfg.blk('zc'))], [], [lp['onw']], [(cfg.CW, BF16)])
    y = jnp.concatenate([oa, ob, oc], axis=1)
    if target is None:
        x_next = matmul(f"out_proj_{l}", y, wo, 'nn', 1024, 1024, cfg.D, add=x)
    else:
        x_next = out_proj_loss(f"out_proj_{l}", y, wo, x, target, 512)
    saved = dict(x=x, p=p, h=h, y=y, qn=qn, kn=kn, v=v, g=g, beta=beta, intra_outs=intra_outs, inv=inv, s_in=s_in, o=o,
                 lp=lp, wo=wo)
    return x_next, saved, carried


def layer_backward(cfg, l, dxn, sv, lp, wp, wo, cos, sin, rs=None, own_rs=None):
    n = dxn.shape[0]
    nseq = n // cfg.T
    T = cfg.T
    p = sv['p']
    AW, BW, CW = cfg.AW, cfg.BW, cfg.CW
    dy = matmul(f"dy_{l}", dxn, wo, 'nt', 1024, 1024, cfg.D)
    dwo = matmul(f"dwo_{l}", sv['y'], dxn, 'tn', 1024, 1024, 2048)
    doa, dob, doc = dy[:, :AW], dy[:, AW:AW + BW], dy[:, AW + BW:]
    tbo = min(TB_OUT, T)
    do, dzc, donw = rb_bwd(f"gdn_out_bwd_{l}", functools.partial(gdn_out_block, cfg), n, tbo, T // tbo,
                           [Row(sv['o'], CW, 0, F32), Row(p, CW, cfg.blk('zc'), BF16)], [], [lp['onw']], [doc], [True])
    dintra, got_rest = gdn_scan_bwd(f"gdn_scan_bwd_{l}", cfg, nseq, *sv['intra_outs'], sv['s_in'], do,
                                    comm=None if rs is None else rs.scatter([1, 2]))
    dqn, dkn, dv, dg, dbeta = rb_bwd(
        f"gdn_intra_bwd_{l}", functools.partial(gdn_intra_rows, cfg), n, TB_INTRA_BWD, T // TB_INTRA_BWD,
        [Row(sv['qn'], CW, 0, F32), Row(sv['kn'], CW, 0, F32), Row(sv['v'], CW, 0, F32), Row(sv['g'], LANE, 0, F32),
         Row(sv['beta'], LANE, 0, F32), Row(sv['inv'], cfg.CH * CHUNK)], [], [], list(dintra), [])
    rows, halos = _prep_io(cfg, p, True)
    tbp = min(TB_PREP, T)
    dqc, dkc, dvc, dbc, dac, dccw, dalog, ddtb = rb_bwd(
        f"gdn_prep_bwd_{l}", functools.partial(gdn_prep_block, cfg), n, tbp, T // tbp, rows, halos,
        [lp['ccw'], lp['alog'], lp['dtb']], [dqn, dkn, dv, dg, dbeta], [True] * 3)
    rows, halos = _conv_io(cfg, p, True)
    tbb = min(TB_CONV, T)
    conv_grads = rb_bwd(
        f"conv_bwd_{l}", functools.partial(conv_block, cfg), n, tbb, T // tbb, rows, halos,
        [lp['cw'], lp['cb'], lp['lw'], lp['lb'], lp['pw'], lp['pb']], [dob], [True] * 6,
        comm=None if rs is None else rs.scatter([0]))
    got = None
    if rs is not None:
        conv_grads, got_w_in = conv_grads
        got = got_w_in + got_rest
    dub, dzb, dcw, dcb, dlw, dlb, dpw, dpb = conv_grads
    rows, halos = _attn_io(cfg, p, cos, sin, True)
    dqa, dza, dka, dva, dqnw, dknw, dsinks_row = rb_bwd(
        f"attn_bwd_{l}", functools.partial(attn_block, cfg), n, ATTN_BLOCK, T // ATTN_BLOCK, rows, halos,
        [lp['qnw'], lp['knw'], lp['sinks_row']], [doa], [True] * 3)
    dgroups = dict(qa=dqa, za=dza, qc=dqc, kc=dkc, vc=dvc, zc=dzc, ka=dka, va=dva, ub=dub, zb=dzb, bc=dbc, ac=dac)
    dp = jnp.concatenate([dgroups[k] for k in cfg.order], axis=1)
    dw_in = grad_w_blocks(f"dwp_{l}", sv['h'], chip_blocks(cfg, dgroups, N_CHIPS), 1024, 2048)
    mine = None if own_rs is None else own_rs(dict(w_in_blocks=dw_in, w_out=dwo, b_pw_w=dpw))
    got_mine = None
    if mine is None:
        dh = matmul(f"dh_{l}", dp, wp, 'nt', 1024, 1024, _tile(cfg.WP, 2304))
        dx, dnw = norm_bwd(f"norm_bwd_{l}", sv['x'], lp['nw'], dh, dxn, 256)
    elif l > 0:
        dh, mine_received = matmul(f"dh_{l}", dp, wp, 'nt', 1024, 1024, _tile(cfg.WP, 2304), comm=mine.swap())
        mine.add(mine_received)
        dx, dnw = norm_bwd(f"norm_bwd_{l}", sv['x'], lp['nw'], dh, dxn, 256)
    else:
        mine.add(run_comm(f"rs{l}_swap_halves", mine.swap()))
        dh, got_w_in = matmul(f"dh_{l}", dp, wp, 'nt', 1024, 1024, _tile(cfg.WP, 2304), comm=mine.scatter([0]))
        dx, dnw, got_others = norm_bwd(f"norm_bwd_{l}", sv['x'], lp['nw'], dh, dxn, 256, comm=mine.scatter([1, 2]))
        got_mine = got_w_in + got_others
    grads = dict(
        norm_w=dnw[0], w_in_blocks=dw_in, q_norm_w=dqnw[0], k_norm_w=dknw[0],
        sinks=dsinks_row.reshape(cfg.AQH, A_HEAD)[:, 0],
        b_conv_w=dcw, b_conv_b=dcb[0], b_ln_w=dlw[0], b_ln_b=dlb[0], b_pw_w=dpw, b_pw_b=dpb[0],
        c_conv_w=dccw, c_a_log=dalog[0, :cfg.CH], c_dt_bias=ddtb[0, :cfg.CH], c_onorm_w=donw[0], w_out=dwo)
    return dx, grads, (got, mine, got_mine)


def rope_for(cfg, positions):
    n = positions.size
    inv_freq = ROPE_THETA ** (-np.arange(0, ROT_DIM, 2, dtype=np.float32) / ROT_DIM)
    freq_row = np.zeros((1, A_HEAD), np.float32)
    freq_row[0, :ROT_DIM] = np.concatenate([inv_freq, inv_freq])
    return rope_tables("rope_tables", positions.reshape(n, 1), jnp.asarray(freq_row))


def local_step(cfg, x, positions, prm, wps, wos, target):
    nseq = x.shape[0]
    n = nseq * cfg.T
    cos, sin = rope_for(cfg, positions)
    lps = [_layer_params(cfg, {k: v[l] for k, v in prm.items()}) for l in range(DEPTH)]
    saved = []
    xl = x.reshape(n, cfg.D)
    for l in range(DEPTH):
        xl, sv, _ = layer_forward(cfg, l, xl, lps[l], wps[l], wos[l], cos, sin,
                                  target=target.reshape(n, cfg.D) if l == DEPTH - 1 else None)
        saved.append(sv)
    dx, loss = xl
    grads = [None] * DEPTH
    for l in reversed(range(DEPTH)):
        dx, grads[l], _ = layer_backward(cfg, l, dx, saved[l], lps[l], wps[l], wos[l], cos, sin)
    return loss, dx.reshape(x.shape), grads


N_CHIPS = 4
N_DEV = 8


def _place():
    return lax.axis_index("x"), lax.axis_index("y"), lax.axis_index("c")


def _other_chips(x, y):
    return [(1 - x, y), (x, 1 - y), (1 - x, 1 - y)]


def _remote(src, dst, send, recv, to):
    return pltpu.make_async_remote_copy(src_ref=src, dst_ref=dst, send_sem=send, recv_sem=recv, device_id=to,
                                        device_id_type=MESH)


def gather_comm(arrs):
    n = len(arrs)

    def half(c):
        return [pl.ds(c * (a.shape[0] // 2), a.shape[0] // 2) for a in arrs]

    def first_copies(ins, outs, send, recv):
        x, y, c = _place()
        me = 2 * x + y
        mine = half(c)
        return [_remote(ins[i].at[mine[i]], outs[i].at[me, mine[i]], send.at[i, j], recv.at[i, j], (cx, cy, c))
                for i in range(n) for j, (cx, cy) in enumerate(_other_chips(x, y))]

    def start(ins, outs, sems):
        for cp in first_copies(ins, outs, *sems):
            cp.start()

    def finish(ins, outs, sems):
        send, recv = sems
        x, y, c = _place()
        chips = _other_chips(x, y)
        sib = (x, y, 1 - c)
        passed = []
        mine, other = half(c), half(1 - c)
        for i in range(n):
            for j, (cx, cy) in enumerate(chips):
                blk = outs[i].at[2 * cx + cy, mine[i]]
                _remote(blk, blk, send.at[i, j], recv.at[i, j], (x, y, c)).wait_recv()
                cp = _remote(blk, blk, send.at[i, 3 + j], recv.at[i, 3 + j], sib)
                cp.start()
                passed.append(cp)
        for i in range(n):
            for j, (cx, cy) in enumerate(chips):
                blk = outs[i].at[2 * cx + cy, other[i]]
                _remote(blk, blk, send.at[i, 3 + j], recv.at[i, 3 + j], sib).wait_recv()
        for cp in first_copies(ins, outs, send, recv) + passed:
            cp.wait_send()

    return Comm(arrs, [jax.ShapeDtypeStruct((N_CHIPS,) + a.shape, a.dtype) for a in arrs],
                [pltpu.SemaphoreType.DMA((n, 6)), pltpu.SemaphoreType.DMA((n, 6))], start, finish)


def fill_own(gathered, arrs):
    me = 2 * lax.axis_index("x") + lax.axis_index("y")
    return [lax.dynamic_update_index_in_dim(o, a, me, 0) for o, a in zip(gathered, arrs)]


def swap_comm(arrs):
    n = len(arrs)

    def copies(ins, outs, send, recv):
        x, y, c = _place()
        return [_remote(ins[i].at[:, 1 - c], outs[i], send.at[i], recv.at[i], (x, y, 1 - c)) for i in range(n)]

    def start(ins, outs, sems):
        for cp in copies(ins, outs, *sems):
            cp.start()

    def finish(ins, outs, sems):
        for cp in copies(ins, outs, *sems):
            cp.wait()

    return Comm(arrs, [jax.ShapeDtypeStruct((a.shape[0],) + a.shape[2:], a.dtype) for a in arrs],
                [pltpu.SemaphoreType.DMA((n,)), pltpu.SemaphoreType.DMA((n,))], start, finish)


def scatter_comm(arrs):
    n = len(arrs)

    def copies(ins, outs, send, recv):
        x, y, c = _place()
        return [_remote(ins[i].at[2 * cx + cy], outs[i].at[j], send.at[i, j], recv.at[i, j], (cx, cy, c))
                for i in range(n) for j, (cx, cy) in enumerate(_other_chips(x, y))]

    def start(ins, outs, sems):
        for cp in copies(ins, outs, *sems):
            cp.start()

    def finish(ins, outs, sems):
        send, recv = sems
        x, y, c = _place()
        for i in range(n):
            for j in range(3):
                blk = outs[i].at[j]
                _remote(blk, blk, send.at[i, j], recv.at[i, j], (x, y, c)).wait_recv()
        for cp in copies(ins, outs, send, recv):
            cp.wait_send()

    return Comm(arrs, [jax.ShapeDtypeStruct((3,) + a.shape[1:], a.dtype) for a in arrs],
                [pltpu.SemaphoreType.DMA((n, 3)), pltpu.SemaphoreType.DMA((n, 3))], start, finish)


def share_comm(arrs):
    n = len(arrs)

    def copies(outs, send, recv):
        x, y, c = _place()
        return [_remote(outs[i].at[c], outs[i].at[c], send.at[i], recv.at[i], (x, y, 1 - c)) for i in range(n)]

    def start(ins, outs, sems):
        for cp in copies(outs, *sems):
            cp.start()

    def finish(ins, outs, sems):
        send, recv = sems
        x, y, c = _place()
        for i in range(n):
            blk = outs[i].at[1 - c]
            _remote(blk, blk, send.at[i], recv.at[i], (x, y, c)).wait_recv()
        for cp in copies(outs, send, recv):
            cp.wait_send()

    return Comm(arrs, [jax.ShapeDtypeStruct(a.shape, a.dtype) for a in arrs],
                [pltpu.SemaphoreType.DMA((n,)), pltpu.SemaphoreType.DMA((n,))], start, finish,
                aliases={i: i for i in range(n)})


def all_reduce_small(name, packed):
    r = packed.shape[0]

    def body(in_ref, out_ref, buf, send, recv):
        x, y, c = _place()
        me = 4 * x + 2 * y + c
        buf[me] = in_ref[...]
        flips = [(fx, fy, fc) for fx in (0, 1) for fy in (0, 1) for fc in (0, 1) if (fx, fy, fc) != (0, 0, 0)]
        peers = [((x + fx) % 2, (y + fy) % 2, (c + fc) % 2) for fx, fy, fc in flips]
        cps = [_remote(in_ref, buf.at[me], send.at[k], recv.at[k], peer) for k, peer in enumerate(peers)]
        for cp in cps:
            cp.start()
        for k, (px, py, pc) in enumerate(peers):
            blk = buf.at[4 * px + 2 * py + pc]
            _remote(blk, blk, send.at[k], recv.at[k], (x, y, c)).wait_recv()
        for cp in cps:
            cp.wait_send()
        acc = buf[0]
        for d in range(1, N_DEV):
            acc = acc + buf[d]
        out_ref[...] = acc

    vm = pl.BlockSpec(memory_space=pltpu.VMEM)
    return pl.pallas_call(
        body, name=name, in_specs=[vm], out_specs=vm, out_shape=jax.ShapeDtypeStruct(packed.shape, F32),
        scratch_shapes=[pltpu.VMEM((N_DEV, r, LANE), F32), pltpu.SemaphoreType.DMA((N_DEV - 1,)),
                        pltpu.SemaphoreType.DMA((N_DEV - 1,))],
    )(packed)


def add_own_half(name, g, a, c_idx, tr):
    nch, _, r, cc = g.shape
    tr = min(tr, r)

    def body(c_ref, g_ref, a_ref, o_ref):
        o_ref[...] = (g_ref[0] + a_ref[...]).astype(o_ref.dtype)

    return pl.pallas_call(
        body, name=name,
        grid_spec=pltpu.PrefetchScalarGridSpec(
            num_scalar_prefetch=1, grid=(nch, r // tr),
            in_specs=[pl.BlockSpec((1, 1, tr, cc), lambda j, i, c_ref: (j, c_ref[0], i, 0)),
                      pl.BlockSpec((1, tr, cc), lambda j, i, c_ref: (j, i, 0))],
            out_specs=pl.BlockSpec((1, tr, cc), lambda j, i, c_ref: (j, i, 0))),
        out_shape=jax.ShapeDtypeStruct(a.shape, BF16),
        compiler_params=_cparams(("parallel", "parallel")),
    )(c_idx, g, a)


def sum_chips(name, p, b, idx, tr):
    _, r, cc = p.shape
    tr = min(tr, r)

    def body(idx_ref, p_ref, b_ref, o_ref):
        acc = p_ref[0].astype(F32)
        for k in range(3):
            acc = acc + b_ref[k].astype(F32)
        o_ref[0] = acc

    return pl.pallas_call(
        body, name=name,
        grid_spec=pltpu.PrefetchScalarGridSpec(
            num_scalar_prefetch=1, grid=(r // tr,),
            in_specs=[pl.BlockSpec((1, tr, cc), lambda i, s: (s[0], i, 0)),
                      pl.BlockSpec((3, tr, cc), lambda i, s: (0, i, 0))],
            out_specs=pl.BlockSpec((1, tr, cc), lambda i, s: (s[1], i, 0))),
        out_shape=jax.ShapeDtypeStruct((2, r, cc), F32),
        compiler_params=_cparams(("parallel",)),
    )(idx, p, b)


class GradReduce:
    def __init__(self, tag, parts, chip, c_idx):
        self.tag, self.c_idx = tag, c_idx
        self.parts = [p.reshape(p.shape[0], 2, p.shape[1] // 2, p.shape[2]) for p in parts]
        self.idx = jnp.concatenate([chip.astype(jnp.int32).reshape(1), c_idx])

    def swap(self):
        return swap_comm(self.parts)

    def add(self, received):
        self.part = [add_own_half(f"rs{self.tag}_add_sibling_{t}", g, a, self.c_idx, 128)
                     for t, (g, a) in enumerate(zip(self.parts, received))]

    def scatter(self, which=None):
        return scatter_comm(self.part if which is None else [self.part[t] for t in which])

    def finish(self, got):
        red = [sum_chips(f"rs{self.tag}_sum_chips_{t}", p, b, self.idx, 128) for t, (p, b) in enumerate(zip(self.part, got))]
        out = run_comm(f"rs{self.tag}_share_halves", share_comm(red))
        return [o.reshape(-1, o.shape[-1]) for o in out]


def adamw_many(name, ws, gs, ms, vs):
    n = len(ws)

    def body(*refs):
        for i in range(n):
            w_ref, g_ref, m_ref, v_ref = (refs[k * n + i] for k in range(4))
            d_ref, mo_ref, vo_ref = (refs[(4 + k) * n + i] for k in range(3))
            g = g_ref[...]
            m = ADAM_B1 * m_ref[...] + (1.0 - ADAM_B1) * g
            v = ADAM_B2 * v_ref[...] + (1.0 - ADAM_B2) * jnp.square(g)
            m_hat = m / (1.0 - ADAM_B1 ** ADAM_STEP)
            v_hat = v / (1.0 - ADAM_B2 ** ADAM_STEP)
            d_ref[...] = -ADAM_LR * (m_hat / (jnp.sqrt(v_hat) + ADAM_EPS) + ADAM_WD * w_ref[...])
            mo_ref[...] = m
            vo_ref[...] = v

    vm = pl.BlockSpec(memory_space=pltpu.VMEM)
    return pl.pallas_call(
        body, name=name, in_specs=[vm] * (4 * n), out_specs=[vm] * (3 * n),
        out_shape=[jax.ShapeDtypeStruct(a.shape, F32) for a in ws] * 3,
    )(*ws, *gs, *ms, *vs)


def adamw_layers(name, w, g0, g1, m, v, tb):
    _, r, cc = w.shape
    tb = min(tb, r)
    nb = r // tb

    def body(w_ref, g0_ref, g1_ref, m_ref, v_ref, g_ref, d_ref, mo_ref, vo_ref):
        g = jnp.where(pl.program_id(0) == 0, g0_ref[...], g1_ref[...])
        m = ADAM_B1 * m_ref[0] + (1.0 - ADAM_B1) * g
        v = ADAM_B2 * v_ref[0] + (1.0 - ADAM_B2) * jnp.square(g)
        m_hat = m / (1.0 - ADAM_B1 ** ADAM_STEP)
        v_hat = v / (1.0 - ADAM_B2 ** ADAM_STEP)
        g_ref[0] = g
        d_ref[0] = -ADAM_LR * (m_hat / (jnp.sqrt(v_hat) + ADAM_EPS) + ADAM_WD * w_ref[0])
        mo_ref[0] = m
        vo_ref[0] = v

    spec = pl.BlockSpec((1, tb, cc), lambda l, i: (l, i, 0))
    g0_spec = pl.BlockSpec((tb, cc), lambda l, i: (jnp.where(l == 0, i, nb - 1), 0))
    g1_spec = pl.BlockSpec((tb, cc), lambda l, i: (jnp.where(l == 1, i, 0), 0))
    return pl.pallas_call(
        body, name=name, grid=(2, nb), in_specs=[spec, g0_spec, g1_spec, spec, spec], out_specs=[spec] * 4,
        out_shape=[jax.ShapeDtypeStruct(w.shape, F32)] * 4,
        compiler_params=_cparams(("arbitrary", "arbitrary")),
    )(w, g0, g1, m, v)


def adamw_cols_major(name, w, g0, g1, m, v, tb=LANE):
    wt, mt, vt = (jnp.transpose(a, (2, 0, 1)) for a in (w, m, v))
    cc, _, r = wt.shape

    def body(w_ref, g0_ref, g1_ref, m_ref, v_ref, g_ref, d_ref, mo_ref, vo_ref):
        for l, gl_ref in enumerate((g0_ref, g1_ref)):
            g = gl_ref[...].T
            m = ADAM_B1 * m_ref[:, l, :] + (1.0 - ADAM_B1) * g
            v = ADAM_B2 * v_ref[:, l, :] + (1.0 - ADAM_B2) * jnp.square(g)
            m_hat = m / (1.0 - ADAM_B1 ** ADAM_STEP)
            v_hat = v / (1.0 - ADAM_B2 ** ADAM_STEP)
            g_ref[:, l, :] = g
            d_ref[:, l, :] = -ADAM_LR * (m_hat / (jnp.sqrt(v_hat) + ADAM_EPS) + ADAM_WD * w_ref[:, l, :])
            mo_ref[:, l, :] = m
            vo_ref[:, l, :] = v

    spec = pl.BlockSpec((tb, 2, r), lambda i: (i, 0, 0))
    gspec = pl.BlockSpec((r, tb), lambda i: (0, i))
    outs = pl.pallas_call(
        body, name=name, grid=(pl.cdiv(cc, tb),), in_specs=[spec, gspec, gspec, spec, spec], out_specs=[spec] * 4,
        out_shape=[jax.ShapeDtypeStruct(wt.shape, F32)] * 4,
        compiler_params=_cparams(("parallel",)),
    )(wt, g0, g1, mt, vt)
    return [jnp.transpose(o, (1, 2, 0)) for o in outs]


def _pack(arrs):
    flat = jnp.concatenate([a.reshape(-1).astype(F32) for a in arrs])
    pad = (-flat.shape[0]) % (8 * LANE)
    return jnp.pad(flat, (0, pad)).reshape(-1, LANE)


def _unpack(packed, shapes):
    flat = packed.reshape(-1)
    out, off = [], 0
    for s in shapes:
        size = math.prod(s)
        out.append(flat[off:off + size].reshape(s))
        off += size
    return out


BIG = ('w_in', 'w_out', 'b_pw_w')
SMALL = tuple(k for k in WEIGHTS if k not in BIG)
CHIP_SHARDED_SMALL = {'b_conv_w': 2, 'c_conv_w': 2}


def kernel(x, positions, norm_w, w_in, q_norm_w, k_norm_w, sinks, b_conv_w, b_conv_b, b_ln_w, b_ln_b, b_pw_w, b_pw_b, c_conv_w, c_a_log, c_dt_bias, c_onorm_w, w_out, loss_target, m_norm_w, m_w_in, m_q_norm_w, m_k_norm_w, m_sinks, m_b_conv_w, m_b_conv_b, m_b_ln_w, m_b_ln_b, m_b_pw_w, m_b_pw_b, m_c_conv_w, m_c_a_log, m_c_dt_bias, m_c_onorm_w, m_w_out, v_norm_w, v_w_in, v_q_norm_w, v_k_norm_w, v_sinks, v_b_conv_w, v_b_conv_b, v_b_ln_w, v_b_ln_b, v_b_pw_w, v_b_pw_b, v_c_conv_w, v_c_a_log, v_c_dt_bias, v_c_onorm_w, v_w_out):
    cfg = Cfg(x.shape[-1], x.shape[-2])
    w = dict(norm_w=norm_w, w_in=w_in, q_norm_w=q_norm_w, k_norm_w=k_norm_w, sinks=sinks, b_conv_w=b_conv_w,
             b_conv_b=b_conv_b, b_ln_w=b_ln_w, b_ln_b=b_ln_b, b_pw_w=b_pw_w, b_pw_b=b_pw_b, c_conv_w=c_conv_w,
             c_a_log=c_a_log, c_dt_bias=c_dt_bias, c_onorm_w=c_onorm_w, w_out=w_out)
    m = dict(norm_w=m_norm_w, w_in=m_w_in, q_norm_w=m_q_norm_w, k_norm_w=m_k_norm_w, sinks=m_sinks,
             b_conv_w=m_b_conv_w, b_conv_b=m_b_conv_b, b_ln_w=m_b_ln_w, b_ln_b=m_b_ln_b, b_pw_w=m_b_pw_w,
             b_pw_b=m_b_pw_b, c_conv_w=m_c_conv_w, c_a_log=m_c_a_log, c_dt_bias=m_c_dt_bias, c_onorm_w=m_c_onorm_w,
             w_out=m_w_out)
    v = dict(norm_w=v_norm_w, w_in=v_w_in, q_norm_w=v_q_norm_w, k_norm_w=v_k_norm_w, sinks=v_sinks,
             b_conv_w=v_b_conv_w, b_conv_b=v_b_conv_b, b_ln_w=v_b_ln_w, b_ln_b=v_b_ln_b, b_pw_w=v_b_pw_w,
             b_pw_b=v_b_pw_b, c_conv_w=v_c_conv_w, c_a_log=v_c_a_log, c_dt_bias=v_c_dt_bias, c_onorm_w=v_c_onorm_w,
             w_out=v_w_out)
    chip = 2 * lax.axis_index("x") + lax.axis_index("y")
    c_idx = lax.axis_index("c").astype(jnp.int32).reshape(1)
    D, T = cfg.D, cfg.T
    nseq = x.shape[0]
    n = nseq * T
    w_in_b, w_out_b = w_in.astype(BF16), w_out.astype(BF16)

    def permuted(g_in):
        return permute_w_in(cfg, jnp.concatenate(list(g_in), axis=1))

    def layer_prm(l, g_pw, g_bcw, g_ccw):
        prm = {k: w[k][l] for k in SMALL}
        prm['b_pw_w'] = g_pw.reshape(cfg.BW, cfg.BW)
        prm['b_conv_w'] = jnp.concatenate(list(g_bcw[:, l]), axis=1)
        prm['c_conv_w'] = jnp.concatenate(list(g_ccw[:, l]), axis=1)
        return _layer_params(cfg, prm)

    (g_in0,) = fill_own(run_comm("gather_weights_0", gather_comm([w_in_b[0]])), [w_in_b[0]])
    cos, sin = rope_for(cfg, positions)
    early = [w_out_b[0], b_pw_w[0], b_conv_w, c_conv_w]
    top, bottom = w_in_b[1][:D // 2], w_in_b[1][D // 2:]
    late = [w_out_b[1], b_pw_w[1]]
    conv_ws = {}

    def layer0_rest(rode):
        g_out0, g_pw0, conv_ws['b'], conv_ws['c'] = fill_own(rode, early)
        return layer_prm(0, g_pw0, conv_ws['b'], conv_ws['c']), g_out0.reshape(D, D)

    x1, sv0, rode = layer_forward(
        cfg, 0, x.reshape(n, D), (norm_w[0].reshape(1, -1), layer0_rest), permuted(g_in0), None, cos, sin,
        comms=dict(in_proj=gather_comm(early), attn=gather_comm([top]), conv=gather_comm(late),
                   intra=gather_comm([bottom])))
    lp0, wp0, wo0 = sv0['lp'], permuted(g_in0), sv0['wo']
    (g_top,), (g_bottom,) = fill_own(rode['attn'], [top]), fill_own(rode['intra'], [bottom])
    g_out1, g_pw1 = fill_own(rode['conv'], late)
    wp1, wo1 = permuted(jnp.concatenate([g_top, g_bottom], axis=1)), g_out1.reshape(D, D)
    lp1 = layer_prm(1, g_pw1, conv_ws['b'], conv_ws['c'])
    (dx2, loss_local), sv1, _ = layer_forward(cfg, 1, x1, lp1, wp1, wo1, cos, sin, target=loss_target.reshape(n, D))
    loss = lax.psum(loss_local, ("x", "y", "c"))

    def partials(gr):
        return [gr['w_in_blocks'], gr['w_out'].reshape(N_CHIPS, D // N_CHIPS, D),
                gr['b_pw_w'].reshape(N_CHIPS, cfg.BW // N_CHIPS, cfg.BW)]

    dx1, gr1, (_, rs1, _) = layer_backward(cfg, 1, dx2, sv1, lp1, wp1, wo1, cos, sin,
                                        own_rs=lambda gr: GradReduce(1, partials(gr), chip, c_idx))
    dx0, gr0, (got1, rs0, got0) = layer_backward(cfg, 0, dx1, sv0, lp0, wp0, wo0, cos, sin, rs=rs1,
                                           own_rs=lambda gr: GradReduce(0, partials(gr), chip, c_idx))
    red1 = rs1.finish(got1)
    red0 = rs0.finish(got0)
    grad_x = dx0.reshape(x.shape)
    grads = [gr0, gr1]

    small_parts = [jnp.stack([grads[l][k] for l in range(DEPTH)]) for k in SMALL]
    small_red = _unpack(all_reduce_small("all_reduce_small", _pack(small_parts)), [a.shape for a in small_parts])
    g = {}
    for k, a in zip(SMALL, small_red):
        if k in CHIP_SHARDED_SMALL:
            ax = CHIP_SHARDED_SMALL[k]
            width = a.shape[ax] // N_CHIPS
            a = lax.dynamic_slice_in_dim(a, chip * width, width, axis=ax)
        g[k] = a

    delta, new_m, new_v = {}, {}, {}
    for k, g0, g1 in zip(BIG, red0, red1):
        update = adamw_layers if w[k].shape[-1] % LANE == 0 else adamw_cols_major
        g[k], delta[k], new_m[k], new_v[k] = update(f"adamw_{k}", w[k], g0, g1, m[k], v[k], 128)
    outs = adamw_many("adamw_small", *[[d[k] for k in SMALL] for d in (w, g, m, v)])
    for i, k in enumerate(SMALL):
        delta[k], new_m[k], new_v[k] = outs[i], outs[len(SMALL) + i], outs[2 * len(SMALL) + i]
    return (loss, grad_x, *[g[k] for k in WEIGHTS], *[delta[k] for k in WEIGHTS], *[new_m[k] for k in WEIGHTS],
            *[new_v[k] for k in WEIGHTS])
```

```python
import functools
import math

import numpy as np
import jax
import jax.numpy as jnp
from jax import lax
from jax.experimental import pallas as pl
from jax.experimental.pallas import tpu as pltpu

F32 = jnp.float32
BF16 = jnp.bfloat16
HI = lax.Precision.HIGHEST
MESH = pl.DeviceIdType.MESH

DEPTH = 2
A_HEAD = 64
A_GROUP = 3
ATTN_BLOCK = 128
ROT_DIM = 16
ROPE_THETA = 500000.0
B_CONV = 31
B_HALO = 32
C_HEAD = 128
C_CONV = 4
C_HALO = 8
CHUNK = 64
EPS = 1e-6
LANE = 128

ADAM_LR = 0.001
ADAM_B1 = 0.9
ADAM_B2 = 0.999
ADAM_EPS = 1e-08
ADAM_WD = 0.01
ADAM_STEP = 10

VMEM_LIMIT = 56 * 1024 * 1024

WEIGHTS = ['norm_w', 'w_in', 'q_norm_w', 'k_norm_w', 'sinks', 'b_conv_w', 'b_conv_b', 'b_ln_w', 'b_ln_b',
           'b_pw_w', 'b_pw_b', 'c_conv_w', 'c_a_log', 'c_dt_bias', 'c_onorm_w', 'w_out']


class Cfg:
    def __init__(self, d_model=2048, seq=2048):
        self.D = d_model
        self.T = seq
        self.AW = 3 * d_model // 8
        self.AQH = self.AW // A_HEAD
        self.AKH = self.AQH // A_GROUP
        self.AKW = self.AKH * A_HEAD
        self.BW = d_model // 4
        self.CH = (d_model - self.AW - self.BW) // C_HEAD
        self.CW = self.CH * C_HEAD
        AW, AKW, BW, CW, CH = self.AW, self.AKW, self.BW, self.CW, self.CH
        orig = [('qa', AW), ('ka', AKW), ('va', AKW), ('za', AW), ('ub', 2 * BW), ('zb', BW),
                ('qc', CW), ('kc', CW), ('vc', CW), ('bc', CH), ('ac', CH), ('zc', CW)]
        self.orig = {}
        off = 0
        for n, w in orig:
            self.orig[n] = (off, w)
            off += w
        self.IN_COLS = off
        order = ['qa', 'za', 'qc', 'kc', 'vc', 'zc', 'ka', 'va', 'ub', 'zb', 'bc', 'ac']
        self.order = order
        self.g = {}
        off = 0
        for n in order:
            w = self.orig[n][1]
            wp = LANE if n in ('bc', 'ac') else w
            assert off % wp == 0, (n, off, wp)
            self.g[n] = (off, wp)
            off += wp
        self.WP = off

    def blk(self, name):
        off, w = self.g[name]
        return off // w


def _cparams(sem, vmem=VMEM_LIMIT):
    return pltpu.CompilerParams(dimension_semantics=sem, vmem_limit_bytes=vmem)


def _silu(x):
    return x * jax.nn.sigmoid(x)


ANY = pl.BlockSpec(memory_space=pl.ANY)


class Comm:
    def __init__(self, ins, out_shapes, sems, start, finish, aliases=None):
        self.ins, self.out_shapes, self.sems = list(ins), list(out_shapes), list(sems)
        self.start, self.finish, self.aliases = start, finish, dict(aliases or {})


def call_with_comm(body, name, grid, in_specs, out_specs, out_shape, scratch_shapes, semantics, args, comm=None):
    in_specs, out_specs, out_shape, scratch_shapes = list(in_specs), list(out_specs), list(out_shape), list(scratch_shapes)
    if comm is None:
        outs = pl.pallas_call(body, name=name, grid=grid, in_specs=in_specs, out_specs=out_specs, out_shape=out_shape,
                              scratch_shapes=scratch_shapes, compiler_params=_cparams(semantics))(*args)
        return list(outs), []
    ni, no, ns = len(in_specs), len(out_specs), len(scratch_shapes)
    nci, nco = len(comm.ins), len(comm.out_shapes)

    def wrapped(*refs):
        h_in, c_in = refs[:ni], refs[ni:ni + nci]
        h_out, c_out = refs[ni + nci:ni + nci + no], refs[ni + nci + no:ni + nci + no + nco]
        h_scr, c_sems = refs[ni + nci + no + nco:ni + nci + no + nco + ns], refs[ni + nci + no + nco + ns:]
        ids = [pl.program_id(d) for d in range(len(grid))]
        first = functools.reduce(jnp.logical_and, [i == 0 for i in ids])
        last = functools.reduce(jnp.logical_and, [i == g - 1 for i, g in zip(ids, grid)])

        @pl.when(first)
        def _():
            comm.start(c_in, c_out, c_sems)

        body(*h_in, *h_out, *h_scr)

        @pl.when(last)
        def _():
            comm.finish(c_in, c_out, c_sems)

    outs = pl.pallas_call(
        wrapped, name=name, grid=grid, in_specs=in_specs + [ANY] * nci, out_specs=out_specs + [ANY] * nco,
        out_shape=out_shape + comm.out_shapes, scratch_shapes=scratch_shapes + comm.sems,
        input_output_aliases={ni + k: no + v for k, v in comm.aliases.items()},
        compiler_params=_cparams(("arbitrary",) * len(grid)),
    )(*args, *comm.ins)
    return list(outs[:no]), list(outs[no:])


def run_comm(name, comm):
    nci, nco = len(comm.ins), len(comm.out_shapes)

    def body(*refs):
        c_in, c_out, c_sems = refs[:nci], refs[nci:nci + nco], refs[nci + nco:]
        comm.start(c_in, c_out, c_sems)
        comm.finish(c_in, c_out, c_sems)

    return pl.pallas_call(
        body, name=name, in_specs=[ANY] * nci, out_specs=[ANY] * nco, out_shape=comm.out_shapes,
        scratch_shapes=comm.sems, input_output_aliases=comm.aliases,
    )(*comm.ins)


def _bdot(a, b, ca, cb, precision=HI):
    dims = (((ca,), (cb,)), ((0,), (0,)))
    if precision is HI and a.dtype == F32:
        ah = a.astype(BF16)
        bh = b.astype(BF16)
        al = (a - ah.astype(F32)).astype(BF16)
        bl = (b - bh.astype(F32)).astype(BF16)
        dg = lambda p, q: lax.dot_general(p, q, dims, preferred_element_type=F32)
        return dg(ah, bh) + (dg(ah, bl) + dg(al, bh))
    return lax.dot_general(a, b, dims, precision=precision, preferred_element_type=F32)


def _rope_matrix(nb):
    i = lax.broadcasted_iota(jnp.int32, (nb, A_HEAD, A_HEAD), 1)
    j = lax.broadcasted_iota(jnp.int32, (nb, A_HEAD, A_HEAD), 2)
    half = ROT_DIM // 2
    neg = (j < half) & (i == j + half)
    pos = (j >= half) & (j < ROT_DIM) & (i == j - half)
    return jnp.where(neg, -1.0, jnp.where(pos, 1.0, 0.0)).astype(F32)


def _norm_rope(xh, w, cos, sin):
    y = xh * lax.rsqrt(jnp.mean(xh * xh, axis=-1, keepdims=True) + EPS) * w
    return y * cos + _bdot(y, _rope_matrix(xh.shape[0]), 2, 1) * sin


def attn_block(cfg, first, q, za, kc, vc, cosc, sinc, kp, vp, cosp, sinp, qnw, knw, sinks_row):
    blk = ATTN_BLOCK
    nq, nk = cfg.AQH, cfg.AKH
    qi = lax.broadcasted_iota(jnp.int32, (blk, 2 * blk), 0)
    kj = lax.broadcasted_iota(jnp.int32, (blk, 2 * blk), 1)
    dist = qi + blk - kj
    valid = ((dist >= 0) & (dist < blk) & (jnp.logical_not(first) | (kj >= blk)))[None]
    cos2 = jnp.concatenate([cosp, cosc], axis=0)
    sin2 = jnp.concatenate([sinp, sinc], axis=0)
    head = lambda x, h: x[:, A_HEAD * h:A_HEAD * (h + 1)]
    k2 = jnp.stack([jnp.concatenate([head(kp, h), head(kc, h)], axis=0) for h in range(nk)], axis=0)
    v2 = jnp.stack([jnp.concatenate([head(vp, h), head(vc, h)], axis=0) for h in range(nk)], axis=0)
    k2 = _norm_rope(k2, knw[None], cos2[None], sin2[None]).astype(BF16)
    v2 = v2.astype(BF16)
    k2 = jnp.stack([k2[h // A_GROUP] for h in range(nq)], axis=0)
    v2 = jnp.stack([v2[h // A_GROUP] for h in range(nq)], axis=0)
    qh = jnp.stack([head(q, h) for h in range(nq)], axis=0)
    qh = _norm_rope(qh, qnw[None], cosc[None], sinc[None]).astype(BF16)
    s = _bdot(qh, k2, 2, 2, None) * (A_HEAD ** -0.5)
    s = jnp.where(valid, s, -1e30)
    sink = jnp.stack([sinks_row[:, A_HEAD * h:A_HEAD * h + 1] for h in range(nq)], axis=0)
    m = jnp.maximum(jnp.max(s, axis=-1, keepdims=True), sink)
    e = jnp.exp(s - m)
    den = jnp.sum(e, axis=-1, keepdims=True) + jnp.exp(sink - m)
    o = _bdot((e / den).astype(BF16), v2, 2, 1, None)
    return (jnp.concatenate([o[h] for h in range(nq)], axis=1) * _silu(za),)


def conv_block(cfg, first, u, zb, uh, cw, cb, lw, lb, pw, pb):
    BW = cfg.BW
    tb = u.shape[0]
    uu = jnp.concatenate([uh, u], axis=0)
    h = uu[:, :BW] * jax.nn.sigmoid(uu[:, BW:])
    row = lax.broadcasted_iota(jnp.int32, h.shape, 0)
    h = jnp.where(first & (row < B_HALO), 0.0, h)
    acc = jnp.zeros((tb, BW), F32) + cb
    base = B_HALO - (B_CONV - 1)
    for k in range(B_CONV):
        acc = acc + cw[k:k + 1, :] * h[base + k:base + k + tb, :]
    mu = jnp.mean(acc, axis=-1, keepdims=True)
    var = jnp.mean(jnp.square(acc - mu), axis=-1, keepdims=True)
    y = (acc - mu) * lax.rsqrt(var + EPS) * lw + lb
    s = _silu(y)
    o = jnp.dot(s.astype(BF16), pw.astype(BF16), preferred_element_type=F32) + pb
    return (o * _silu(zb),)


def gdn_prep_block(cfg, first, xq, xk, xv, braw, araw, hq, hk, hv, cw, alog, dtb):
    CW = cfg.CW
    tb = xq.shape[0]
    outs = []
    for idx, (x, xh) in enumerate(((xq, hq), (xk, hk), (xv, hv))):
        xx = jnp.concatenate([jnp.where(first, 0.0, xh), x], axis=0)
        w = cw[:, idx * CW:(idx + 1) * CW]
        acc = jnp.zeros((tb, CW), F32)
        base = C_HALO - (C_CONV - 1)
        for k in range(C_CONV):
            acc = acc + w[k:k + 1, :] * xx[base + k:base + k + tb, :]
        y = _silu(acc)
        if idx < 2:
            parts = []
            for h in range(cfg.CH):
                yh = y[:, C_HEAD * h:C_HEAD * (h + 1)]
                parts.append(yh * lax.rsqrt(jnp.sum(yh * yh, axis=-1, keepdims=True) + EPS))
            y = jnp.concatenate(parts, axis=1)
        outs.append(y)
    beta = jax.nn.sigmoid(braw)
    g = -jnp.exp(alog) * jax.nn.softplus(araw + dtb)
    return outs[0], outs[1], outs[2], g, beta


def _inverse_unit_lower(low, eye):
    pw = low
    inv = eye - low
    for _ in range(5):
        pwb = pw.astype(BF16)
        pw = _bdot(pwb, pwb, 2, 1, None)
        inv = inv + _bdot(inv.astype(BF16), pw.astype(BF16), 2, 1, None)
    ax = inv + _bdot(low, inv, 2, 1)
    return inv + _bdot(inv, eye - ax, 2, 1)


@jax.custom_vjp
def _saved_inverse(low, inv):
    return inv


def _saved_inverse_fwd(low, inv):
    return inv, inv


def _saved_inverse_bwd(inv, d):
    dlow = -_bdot(_bdot(inv, d, 1, 1), inv, 2, 2)
    return dlow, jnp.zeros_like(inv)


_saved_inverse.defvjp(_saved_inverse_fwd, _saved_inverse_bwd)


def gdn_intra_rows(cfg, first, qn, kn, v, g, beta, inv_saved=None):
    c = CHUNK
    CH = cfg.CH
    nchunk = qn.shape[0] // c
    i = lax.broadcasted_iota(jnp.int32, (c, c), 0)
    j = lax.broadcasted_iota(jnp.int32, (c, c), 1)
    incl = (i >= j)[None]
    strict = (i > j)[None]
    eye = (i == j).astype(F32)[None]
    tri = (i >= j).astype(F32)
    rows = [slice(c * ci, c * (ci + 1)) for ci in range(nchunk)]
    gcs = [jnp.dot(tri, g[r], precision=HI, preferred_element_type=F32) for r in rows]
    pairs = [(ci, h) for ci in range(nchunk) for h in range(CH)]
    heads = lambda x, wd: jnp.stack([x[rows[ci], wd * h:wd * (h + 1)] for ci, h in pairs], axis=0)
    gch = jnp.stack([gcs[ci][:, h:h + 1] for ci, h in pairs], axis=0)
    bh = jnp.stack([beta[rows[ci], h:h + 1] for ci, h in pairs], axis=0)
    q = heads(qn, C_HEAD) * (C_HEAD ** -0.5)
    k = heads(kn, C_HEAD)
    vv = heads(v, C_HEAD)
    a = jnp.broadcast_to(gch, (len(pairs), c, c))
    diff = jnp.where(incl, a - jnp.swapaxes(a, 1, 2), 0.0)
    decay = jnp.where(incl, jnp.exp(diff), 0.0)
    kb = k * bh
    low = jnp.where(strict, _bdot(kb, k, 2, 2) * decay, 0.0)
    if inv_saved is None:
        inv = _inverse_unit_lower(low, eye)
    else:
        inv = _saved_inverse(low, heads(inv_saved, c))
    eg = jnp.exp(gch)
    sol = _bdot(inv, jnp.concatenate([vv * bh, kb * eg], axis=2), 2, 1)
    intra = jnp.where(incl, _bdot(q, k, 2, 2) * decay, 0.0)
    qg = q * eg
    kd = k * jnp.exp(gch[:, c - 1:c, :] - gch)
    glast = jnp.concatenate([jnp.broadcast_to(gc[c - 1:c, :], gc.shape) for gc in gcs], axis=0)

    def unstack(x):
        return jnp.concatenate([jnp.concatenate([x[ci * CH + h] for h in range(CH)], axis=1) for ci in range(nchunk)],
                               axis=0)

    outs = (unstack(sol[:, :, :C_HEAD]), unstack(sol[:, :, C_HEAD:]), unstack(qg), unstack(kd), unstack(intra), glast)
    return outs + (unstack(inv),) if inv_saved is None else outs


def gdn_state_step(S, u, w, qg, kd, intra, glast):
    v_new = u - _bdot(w, S, 2, 1)
    o = _bdot(qg, S, 2, 1) + _bdot(intra, v_new, 2, 1)
    S_next = S * jnp.exp(glast) + _bdot(kd, v_new, 1, 1)
    return o, S_next


def gdn_out_block(cfg, first, o, zc, onw):
    parts = []
    for h in range(cfg.CH):
        sl = slice(C_HEAD * h, C_HEAD * (h + 1))
        oh = o[:, sl]
        y = oh * lax.rsqrt(jnp.mean(oh * oh, axis=-1, keepdims=True) + EPS) * onw
        parts.append(y * _silu(zc[:, sl]))
    return (jnp.concatenate(parts, axis=1),)


def rms_block(x, nw):
    return x * lax.rsqrt(jnp.mean(x * x, axis=-1, keepdims=True) + EPS) * nw


class Row:
    def __init__(self, arr, width, colblk=0, grad=None):
        self.arr, self.width, self.colblk, self.grad = arr, width, colblk, grad


class Halo:
    def __init__(self, arr, width, colblk, hr, tie=None):
        self.arr, self.width, self.colblk, self.hr, self.tie = arr, width, colblk, hr, tie


def _row_specs(tb, rows, halos, params, pos):
    specs = [pl.BlockSpec((tb, r.width), lambda i, cb=r.colblk: (pos(i), cb)) for r in rows]
    specs += [pl.BlockSpec((h.hr, h.width),
                           lambda i, cb=h.colblk, m=tb // h.hr: (jnp.maximum(pos(i) * m - 1, 0), cb))
              for h in halos]
    specs += [pl.BlockSpec(p.shape, lambda i: (0, 0)) for p in params]
    return specs


def rb_fwd(name, fn, n, tb, bps, rows, halos, params, outs, comm=None):
    nr, nh, npar = len(rows), len(halos), len(params)

    def body(*refs):
        ins = refs[:nr + nh + npar]
        o_refs = refs[nr + nh + npar:]
        first = (pl.program_id(0) % bps) == 0
        res = fn(first, *[r[...] for r in ins])
        for ref, val in zip(o_refs, res):
            ref[...] = val.astype(ref.dtype)

    res, carried = call_with_comm(
        body, name, (n // tb,), _row_specs(tb, rows, halos, params, lambda i: i),
        [pl.BlockSpec((tb, w), lambda i: (i, 0)) for w, _ in outs],
        [jax.ShapeDtypeStruct((n, w), dt) for w, dt in outs], [], ("parallel",),
        [r.arr for r in rows] + [h.arr for h in halos] + list(params), comm)
    return (res, carried) if comm is not None else res


def rb_bwd(name, fn, n, tb, bps, rows, halos, params, douts, param_grads, comm=None):
    nr, nh, npar, nd = len(rows), len(halos), len(params), len(douts)
    nblk = n // tb
    grow = [k for k, r in enumerate(rows) if r.grad is not None]
    ghalo = [k for k, h in enumerate(halos) if h.tie is not None]
    gpar = [k for k, f in enumerate(param_grads) if f]
    pos = lambda i: nblk - 1 - i

    def body(*refs):
        ins = refs[:nr + nh + npar]
        d_refs = refs[nr + nh + npar:nr + nh + npar + nd]
        rest = refs[nr + nh + npar + nd:]
        grow_refs = rest[:len(grow)]
        gpar_refs = rest[len(grow):len(grow) + len(gpar)]
        carry_refs = rest[len(grow) + len(gpar):]
        i = pl.program_id(0)
        first = (pos(i) % bps) == 0
        vals = [r[...] for r in ins]
        diff_idx = grow + [nr + k for k in ghalo] + [nr + nh + k for k in gpar]

        def f(*dargs):
            full = list(vals)
            for k, a in zip(diff_idx, dargs):
                full[k] = a
            return fn(first, *full)

        res, vjp = jax.vjp(f, *[vals[k] for k in diff_idx])
        grads = vjp(tuple(d[...].astype(r.dtype) for d, r in zip(d_refs, res)))
        g_rows = list(grads[:len(grow)])
        g_halos = grads[len(grow):len(grow) + len(ghalo)]
        g_pars = grads[len(grow) + len(ghalo):]

        @pl.when(i == 0)
        def _():
            for c in carry_refs:
                c[...] = jnp.zeros_like(c)
            for p in gpar_refs:
                p[...] = jnp.zeros_like(p)

        for k, ref in enumerate(grow_refs):
            ref[...] = g_rows[k].astype(ref.dtype)
        for ci, hk in enumerate(ghalo):
            h = halos[hk]
            k = grow.index(h.tie)
            tail = g_rows[k][tb - h.hr:, :] + carry_refs[ci][...]
            grow_refs[k][tb - h.hr:, :] = tail.astype(grow_refs[k].dtype)
            carry_refs[ci][...] = g_halos[ci]
        for ref, gp in zip(gpar_refs, g_pars):
            ref[...] += gp

    out_specs = [pl.BlockSpec((tb, rows[k].width), lambda i: (pos(i), 0)) for k in grow]
    out_specs += [pl.BlockSpec(params[k].shape, lambda i: (0, 0)) for k in gpar]
    out_shape = [jax.ShapeDtypeStruct((n, rows[k].width), rows[k].grad) for k in grow]
    out_shape += [jax.ShapeDtypeStruct(params[k].shape, F32) for k in gpar]
    in_specs = _row_specs(tb, rows, halos, params, pos)
    in_specs += [pl.BlockSpec((tb, d.shape[1]), lambda i: (pos(i), 0)) for d in douts]
    res, carried = call_with_comm(
        body, name, (nblk,), in_specs, out_specs, out_shape,
        [pltpu.VMEM((halos[k].hr, halos[k].width), F32) for k in ghalo], ("arbitrary",),
        [r.arr for r in rows] + [h.arr for h in halos] + list(params) + list(douts), comm)
    return (res, carried) if comm is not None else res


_DIMS = {'nn': (((1,), (0,)), ((), ())), 'nt': (((1,), (1,)), ((), ())), 'tn': (((0,), (0,)), ((), ()))}


def matmul(name, a, b, mode, tm, tn, tk, out_dtype=F32, add=None, comm=None):
    if mode == 'tn':
        K, M = a.shape
    else:
        M, K = a.shape
    N = b.shape[0] if mode == 'nt' else b.shape[1]
    tm, tn, tk = min(tm, M), min(tn, N), min(tk, K)
    assert M % tm == 0 and N % tn == 0 and K % tk == 0, (name, M, N, K, tm, tn, tk)
    nk = K // tk
    a_spec = pl.BlockSpec((tk, tm), lambda i, j, k: (k, i)) if mode == 'tn' else pl.BlockSpec((tm, tk), lambda i, j, k: (i, k))
    b_spec = pl.BlockSpec((tn, tk), lambda i, j, k: (j, k)) if mode == 'nt' else pl.BlockSpec((tk, tn), lambda i, j, k: (k, j))
    o_spec = pl.BlockSpec((tm, tn), lambda i, j, k: (i, j))
    has_add = add is not None

    def body(*refs):
        a_ref, b_ref = refs[0], refs[1]
        add_ref = refs[2] if has_add else None
        o_ref = refs[-1]
        k = pl.program_id(2)
        part = lax.dot_general(a_ref[...].astype(BF16), b_ref[...].astype(BF16), _DIMS[mode], preferred_element_type=F32)

        @pl.when(k == 0)
        def _():
            o_ref[...] = ((part + add_ref[...]) if has_add else part).astype(o_ref.dtype)

        if nk > 1:
            @pl.when(k > 0)
            def _():
                o_ref[...] += part

    assert nk == 1 or out_dtype == F32
    ins = [a, b] + ([add] if has_add else [])
    in_specs = [a_spec, b_spec] + ([o_spec] if has_add else [])
    outs, couts = call_with_comm(body, name, (M // tm, N // tn, nk), in_specs, [o_spec],
                                 [jax.ShapeDtypeStruct((M, N), out_dtype)], [], ("parallel", "parallel", "arbitrary"),
                                 ins, comm)
    return (outs[0], couts) if comm is not None else outs[0]


def grad_w_blocks(name, h, dpb, tm, tk):
    n, d = h.shape
    nb, _, s = dpb.shape
    tm, tk = min(tm, d), min(tk, n)
    assert d % tm == 0 and n % tk == 0
    nk = n // tk

    def body(h_ref, b_ref, o_ref):
        k = pl.program_id(2)
        part = lax.dot_general(h_ref[...], b_ref[0], _DIMS['tn'], preferred_element_type=F32)

        @pl.when(k == 0)
        def _():
            o_ref[0] = part

        if nk > 1:
            @pl.when(k > 0)
            def _():
                o_ref[0] += part

    return pl.pallas_call(
        body, name=name, grid=(nb, d // tm, nk),
        in_specs=[pl.BlockSpec((tk, tm), lambda j, i, k: (k, i)), pl.BlockSpec((1, tk, s), lambda j, i, k: (j, k, 0))],
        out_specs=pl.BlockSpec((1, tm, s), lambda j, i, k: (j, i, 0)),
        out_shape=jax.ShapeDtypeStruct((nb, d, s), F32),
        compiler_params=_cparams(("parallel", "parallel", "arbitrary")),
    )(h, dpb)


def grad_h_blocks(name, dpb, wb, tm, tn, comm=None):
    nb, n, s = dpb.shape
    d = wb.shape[1]
    tm, tn = min(tm, n), min(tn, d)
    assert n % tm == 0 and d % tn == 0

    def body(a_ref, b_ref, o_ref):
        k = pl.program_id(2)
        part = lax.dot_general(a_ref[0], b_ref[0], _DIMS['nt'], preferred_element_type=F32)

        @pl.when(k == 0)
        def _():
            o_ref[...] = part

        @pl.when(k > 0)
        def _():
            o_ref[...] += part

    outs, carried = call_with_comm(
        body, name, (n // tm, d // tn, nb),
        [pl.BlockSpec((1, tm, s), lambda i, j, k: (k, i, 0)), pl.BlockSpec((1, tn, s), lambda i, j, k: (k, j, 0))],
        [pl.BlockSpec((tm, tn), lambda i, j, k: (i, j))], [jax.ShapeDtypeStruct((n, d), F32)], [],
        ("parallel", "parallel", "arbitrary"), [dpb, wb], comm)
    return (outs[0], carried) if comm is not None else outs[0]


def norm_in_proj(name, x, nw, wp, tm, tn, comm=None):
    n, d = x.shape
    wpc = wp.shape[1]
    tm, tn = min(tm, n), min(tn, wpc)
    assert n % tm == 0 and wpc % tn == 0

    def body(x_ref, nw_ref, w_ref, p_ref, h_ref):
        @pl.when(pl.program_id(1) == 0)
        def _():
            h_ref[...] = rms_block(x_ref[...], nw_ref[...]).astype(BF16)

        p_ref[...] = jnp.dot(h_ref[...], w_ref[...], preferred_element_type=F32)

    outs, couts = call_with_comm(
        body, name, (n // tm, wpc // tn),
        [pl.BlockSpec((tm, d), lambda i, j: (i, 0)), pl.BlockSpec((1, d), lambda i, j: (0, 0)),
         pl.BlockSpec((d, tn), lambda i, j: (0, j))],
        [pl.BlockSpec((tm, tn), lambda i, j: (i, j)), pl.BlockSpec((tm, d), lambda i, j: (i, 0))],
        [jax.ShapeDtypeStruct((n, wpc), F32), jax.ShapeDtypeStruct((n, d), BF16)], [], ("parallel", "arbitrary"),
        [x, nw, wp], comm)
    return (outs[0], outs[1], couts) if comm is not None else (outs[0], outs[1])


def norm_bwd(name, x, nw, dh, dres, tb, comm=None):
    n, d = x.shape
    tb = min(tb, n)

    def body(x_ref, nw_ref, dh_ref, dres_ref, dx_ref, dnw_ref):
        @pl.when(pl.program_id(0) == 0)
        def _():
            dnw_ref[...] = jnp.zeros_like(dnw_ref)

        _, vjp = jax.vjp(rms_block, x_ref[...], nw_ref[...])
        dx, dnw = vjp(dh_ref[...])
        dx_ref[...] = dx + dres_ref[...]
        dnw_ref[...] += dnw

    row = pl.BlockSpec((tb, d), lambda i: (i, 0))
    par = pl.BlockSpec((1, d), lambda i: (0, 0))
    outs, carried = call_with_comm(
        body, name, (n // tb,), [row, par, row, row], [row, par],
        [jax.ShapeDtypeStruct((n, d), F32), jax.ShapeDtypeStruct((1, d), F32)], [], ("arbitrary",),
        [x, nw, dh, dres], comm)
    return (outs[0], outs[1], carried) if comm is not None else (outs[0], outs[1])


def out_proj_loss(name, y, wo, x, target, tm):
    n, d = x.shape
    tm = min(tm, n)
    assert n % tm == 0

    def body(y_ref, w_ref, x_ref, t_ref, dz_ref, loss_ref):
        @pl.when(pl.program_id(0) == 0)
        def _():
            loss_ref[...] = jnp.zeros_like(loss_ref)

        z = x_ref[...] + jnp.dot(y_ref[...], w_ref[...], preferred_element_type=F32)
        err = z - t_ref[...]
        dz_ref[...] = err * (1.0 / d)
        part = 0.5 * jnp.sum(jnp.mean(err * err, axis=-1, keepdims=True), axis=0, keepdims=True)
        loss_ref[...] += jnp.broadcast_to(part, loss_ref.shape)

    row = pl.BlockSpec((tm, d), lambda i: (i, 0))
    dz, loss = pl.pallas_call(
        body, name=name, grid=(n // tm,),
        in_specs=[pl.BlockSpec((tm, y.shape[1]), lambda i: (i, 0)), pl.BlockSpec(wo.shape, lambda i: (0, 0)), row, row],
        out_specs=[row, pl.BlockSpec((8, LANE), lambda i: (0, 0))],
        out_shape=[jax.ShapeDtypeStruct((n, d), F32), jax.ShapeDtypeStruct((8, LANE), F32)],
        compiler_params=_cparams(("arbitrary",)),
    )(y, wo, x, target)
    return dz, loss[0, 0]


def rope_tables(name, pos_col, inv_freq_row):
    n = pos_col.shape[0]

    def body(p_ref, f_ref, c_ref, s_ref):
        ang = p_ref[...].astype(F32) * f_ref[...]
        lane = lax.broadcasted_iota(jnp.int32, ang.shape, 1)
        c_ref[...] = jnp.where(lane < ROT_DIM, jnp.cos(ang), 1.0)
        s_ref[...] = jnp.where(lane < ROT_DIM, jnp.sin(ang), 0.0)

    return pl.pallas_call(
        body, name=name, out_shape=[jax.ShapeDtypeStruct((n, A_HEAD), F32)] * 2,
    )(pos_col, inv_freq_row)


def _scan_operands(cfg, nseq, u_ref, w_ref, qg_ref, kd_ref, a_ref, gl_ref):
    pairs = [(b, h) for b in range(nseq) for h in range(cfg.CH)]
    st = lambda r, wd: jnp.stack([r[b, :, wd * h:wd * (h + 1)] for b, h in pairs], axis=0)
    gl = jnp.stack([gl_ref[b, 0:1, h:h + 1] for b, h in pairs], axis=0)
    return st(u_ref, C_HEAD), st(w_ref, C_HEAD), st(qg_ref, C_HEAD), st(kd_ref, C_HEAD), st(a_ref, CHUNK), gl


def gdn_scan_fwd(name, cfg, nseq, u, w, qg, kd, intra, glast):
    CH, CW, T = cfg.CH, cfg.CW, cfg.T
    nc = T // CHUNK

    def body(u_ref, w_ref, qg_ref, kd_ref, a_ref, gl_ref, o_ref, sin_ref, s_ref):
        @pl.when(pl.program_id(0) == 0)
        def _():
            s_ref[...] = jnp.zeros_like(s_ref)

        S = s_ref[...]
        for b in range(nseq):
            sin_ref[b, 0] = S[b * CH:(b + 1) * CH]
        o, S_next = gdn_state_step(S, *_scan_operands(cfg, nseq, u_ref, w_ref, qg_ref, kd_ref, a_ref, gl_ref))
        s_ref[...] = S_next
        for b in range(nseq):
            o_ref[b] = jnp.concatenate([o[b * CH + h] for h in range(CH)], axis=1)

    row = lambda wd: pl.BlockSpec((nseq, CHUNK, wd), lambda c: (0, c, 0))
    widths = [CW, CW, CW, CW, CH * CHUNK, LANE]
    o, s_in = pl.pallas_call(
        body, name=name, grid=(nc,),
        in_specs=[row(x) for x in widths],
        out_specs=[row(CW), pl.BlockSpec((nseq, 1, CH, C_HEAD, C_HEAD), lambda c: (0, c, 0, 0, 0))],
        out_shape=[jax.ShapeDtypeStruct((nseq, T, CW), F32),
                   jax.ShapeDtypeStruct((nseq, nc, CH, C_HEAD, C_HEAD), F32)],
        scratch_shapes=[pltpu.VMEM((nseq * CH, C_HEAD, C_HEAD), F32)],
        compiler_params=_cparams(("arbitrary",)),
    )(*[a.reshape(nseq, T, a.shape[1]) for a in (u, w, qg, kd, intra, glast)])
    return o.reshape(nseq * T, CW), s_in


def gdn_scan_bwd(name, cfg, nseq, u, w, qg, kd, intra, glast, s_in, do, comm=None):
    CH, CW, T = cfg.CH, cfg.CW, cfg.T
    nc = T // CHUNK

    def body(u_ref, w_ref, qg_ref, kd_ref, a_ref, gl_ref, sin_ref, do_ref,
             du_ref, dw_ref, dqg_ref, dkd_ref, da_ref, dgl_ref, ds_ref):
        @pl.when(pl.program_id(0) == 0)
        def _():
            ds_ref[...] = jnp.zeros_like(ds_ref)

        S = jnp.concatenate([sin_ref[b, 0] for b in range(nseq)], axis=0)
        dout = jnp.stack([do_ref[b, :, C_HEAD * h:C_HEAD * (h + 1)] for b in range(nseq) for h in range(CH)], axis=0)
        _, vjp = jax.vjp(gdn_state_step, S, *_scan_operands(cfg, nseq, u_ref, w_ref, qg_ref, kd_ref, a_ref, gl_ref))
        dS, du, dw, dqg, dkd, da, dg = vjp((dout, ds_ref[...]))
        ds_ref[...] = dS
        lane = lax.broadcasted_iota(jnp.int32, (CHUNK, LANE), 1)
        rowi = lax.broadcasted_iota(jnp.int32, (CHUNK, LANE), 0)
        for b in range(nseq):
            cat = lambda x: jnp.concatenate([x[b * CH + h] for h in range(CH)], axis=1)
            du_ref[b] = cat(du)
            dw_ref[b] = cat(dw)
            dqg_ref[b] = cat(dqg)
            dkd_ref[b] = cat(dkd)
            da_ref[b] = cat(da)
            dgl = jnp.zeros((CHUNK, LANE), F32)
            for h in range(CH):
                dgl = dgl + jnp.where((lane == h) & (rowi == 0), dg[b * CH + h], 0.0)
            dgl_ref[b] = dgl

    row = lambda wd: pl.BlockSpec((nseq, CHUNK, wd), lambda c: (0, nc - 1 - c, 0))
    widths = [CW, CW, CW, CW, CH * CHUNK, LANE]
    outs, carried = call_with_comm(
        body, name, (nc,),
        [row(x) for x in widths]
        + [pl.BlockSpec((nseq, 1, CH, C_HEAD, C_HEAD), lambda c: (0, nc - 1 - c, 0, 0, 0)), row(CW)],
        [row(x) for x in widths], [jax.ShapeDtypeStruct((nseq, T, x), F32) for x in widths],
        [pltpu.VMEM((nseq * CH, C_HEAD, C_HEAD), F32)], ("arbitrary",),
        [a.reshape(nseq, T, a.shape[1]) for a in (u, w, qg, kd, intra, glast)] + [s_in, do.reshape(nseq, T, CW)], comm)
    return [o.reshape(nseq * T, o.shape[2]) for o in outs], carried


def _tile(total, cap, unit=LANE):
    best = None
    for t in range(unit, min(cap, total) + 1, unit):
        if total % t == 0:
            best = t
    assert best is not None, (total, cap, unit)
    return best


def _pad_lanes(v, width=LANE):
    return jnp.pad(v.reshape(1, -1), ((0, 0), (0, width - v.shape[-1])))


def permute_w_in(cfg, w):
    parts = []
    for n in cfg.order:
        off, wd = cfg.orig[n]
        blk = w[:, off:off + wd]
        if cfg.g[n][1] != wd:
            blk = jnp.pad(blk, ((0, 0), (0, cfg.g[n][1] - wd)))
        parts.append(blk)
    return jnp.concatenate(parts, axis=1)


def chip_blocks(cfg, groups, n_chips):
    s = cfg.IN_COLS // n_chips
    blocks = []
    for j in range(n_chips):
        lo, hi = j * s, (j + 1) * s
        pieces = []
        for name, (off, wd) in cfg.orig.items():
            a, b = max(lo, off), min(hi, off + wd)
            if a < b:
                pieces.append(groups[name][:, a - off:b - off])
        blocks.append(jnp.concatenate(pieces, axis=1))
    return jnp.stack(blocks)


def _layer_params(cfg, prm):
    return dict(
        nw=prm['norm_w'].reshape(1, -1),
        qnw=prm['q_norm_w'].reshape(1, -1), knw=prm['k_norm_w'].reshape(1, -1),
        sinks_row=jnp.repeat(prm['sinks'], A_HEAD).reshape(1, -1),
        cw=prm['b_conv_w'], cb=prm['b_conv_b'].reshape(1, -1),
        lw=prm['b_ln_w'].reshape(1, -1), lb=prm['b_ln_b'].reshape(1, -1),
        pw=prm['b_pw_w'], pb=prm['b_pw_b'].reshape(1, -1),
        ccw=prm['c_conv_w'], alog=_pad_lanes(prm['c_a_log']), dtb=_pad_lanes(prm['c_dt_bias']),
        onw=prm['c_onorm_w'].reshape(1, -1),
    )


def _attn_io(cfg, p, cos, sin, grads):
    gq = BF16 if grads else None
    rows = [Row(p, cfg.AW, cfg.blk('qa'), gq), Row(p, cfg.AW, cfg.blk('za'), gq),
            Row(p, cfg.AKW, cfg.blk('ka'), gq), Row(p, cfg.AKW, cfg.blk('va'), gq),
            Row(cos, A_HEAD), Row(sin, A_HEAD)]
    halos = [Halo(p, cfg.AKW, cfg.blk('ka'), ATTN_BLOCK, 2 if grads else None),
             Halo(p, cfg.AKW, cfg.blk('va'), ATTN_BLOCK, 3 if grads else None),
             Halo(cos, A_HEAD, 0, ATTN_BLOCK), Halo(sin, A_HEAD, 0, ATTN_BLOCK)]
    return rows, halos


def _conv_io(cfg, p, grads):
    gq = BF16 if grads else None
    rows = [Row(p, 2 * cfg.BW, cfg.blk('ub'), gq), Row(p, cfg.BW, cfg.blk('zb'), gq)]
    halos = [Halo(p, 2 * cfg.BW, cfg.blk('ub'), B_HALO, 0 if grads else None)]
    return rows, halos


def _prep_io(cfg, p, grads):
    gq = BF16 if grads else None
    rows = [Row(p, cfg.CW, cfg.blk(n), gq) for n in ('qc', 'kc', 'vc')]
    rows += [Row(p, LANE, cfg.blk('bc'), gq), Row(p, LANE, cfg.blk('ac'), gq)]
    halos = [Halo(p, cfg.CW, cfg.blk(n), C_HALO, k if grads else None) for k, n in enumerate(('qc', 'kc', 'vc'))]
    return rows, halos


TB_CONV = 128
TB_PREP = 256
TB_OUT = 256
TB_INTRA_FWD = 128
TB_INTRA_BWD = 128


def layer_forward(cfg, l, x, lp, wp, wo, cos, sin, comms=None, target=None):
    n = x.shape[0]
    nseq = n // cfg.T
    T = cfg.T
    comms = comms or {}
    carried = {}

    def hosted(key, res):
        if comms.get(key) is None:
            return res
        res, carried[key] = res
        return res

    nw = lp['nw'] if isinstance(lp, dict) else lp[0]
    p, h, *rode = norm_in_proj(f"in_proj_{l}", x, nw, wp, 1024, _tile(cfg.WP, 768), comm=comms.get('in_proj'))
    if rode:
        carried['in_proj'] = rode[0]
    if not isinstance(lp, dict):
        lp, wo = lp[1](carried['in_proj'])
    rows, halos = _attn_io(cfg, p, cos, sin, False)
    (oa,) = hosted('attn', rb_fwd(f"attn_fwd_{l}", functools.partial(attn_block, cfg), n, ATTN_BLOCK, T // ATTN_BLOCK,
                                  rows, halos, [lp['qnw'], lp['knw'], lp['sinks_row']], [(cfg.AW, BF16)],
                                  comm=comms.get('attn')))
    rows, halos = _conv_io(cfg, p, False)
    tbb = min(TB_CONV, T)
    (ob,) = hosted('conv', rb_fwd(f"conv_fwd_{l}", functools.partial(conv_block, cfg), n, tbb, T // tbb, rows, halos,
                                  [lp['cw'], lp['cb'], lp['lw'], lp['lb'], lp['pw'], lp['pb']], [(cfg.BW, BF16)],
                                  comm=comms.get('conv')))
    rows, halos = _prep_io(cfg, p, False)
    tbp = min(TB_PREP, T)
    qn, kn, v, g, beta = rb_fwd(f"gdn_prep_fwd_{l}", functools.partial(gdn_prep_block, cfg), n, tbp, T // tbp, rows,
                                halos, [lp['ccw'], lp['alog'], lp['dtb']],
                                [(cfg.CW, F32)] * 3 + [(LANE, F32)] * 2)
    intra_outs = hosted('intra', rb_fwd(
        f"gdn_intra_fwd_{l}", functools.partial(gdn_intra_rows, cfg), n, TB_INTRA_FWD, T // TB_INTRA_FWD,
        [Row(qn, cfg.CW), Row(kn, cfg.CW), Row(v, cfg.CW), Row(g, LANE), Row(beta, LANE)], [], [],
        [(cfg.CW, F32)] * 4 + [(cfg.CH * CHUNK, F32), (LANE, F32), (cfg.CH * CHUNK, F32)], comm=comms.get('intra')))
    intra_outs, inv = intra_outs[:6], intra_outs[6]
    o, s_in = gdn_scan_fwd(f"gdn_scan_fwd_{l}", cfg, nseq, *intra_outs)
    tbo = min(TB_OUT, T)
    (oc,) = rb_fwd(f"gdn_out_fwd_{l}", functools.partial(gdn_out_block, cfg), n, tbo, T // tbo,
                   [Row(o, cfg.CW), Row(p, cfg.CW, cfg.blk('zc'))], [], [lp['onw']], [(cfg.CW, BF16)])
    y = jnp.concatenate([oa, ob, oc], axis=1)
    if target is None:
        x_next = matmul(f"out_proj_{l}", y, wo, 'nn', 1024, 1024, cfg.D, add=x)
    else:
        x_next = out_proj_loss(f"out_proj_{l}", y, wo, x, target, 512)
    saved = dict(x=x, p=p, h=h, y=y, qn=qn, kn=kn, v=v, g=g, beta=beta, intra_outs=intra_outs, inv=inv, s_in=s_in, o=o,
                 lp=lp, wo=wo)
    return x_next, saved, carried


def layer_backward(cfg, l, dxn, sv, lp, wb, wo, cos, sin, rs=None, own_rs=None):
    n = dxn.shape[0]
    nseq = n // cfg.T
    T = cfg.T
    p = sv['p']
    AW, BW, CW = cfg.AW, cfg.BW, cfg.CW
    dy = matmul(f"dy_{l}", dxn, wo, 'nt', 1024, 1024, cfg.D)
    dwo = matmul(f"dwo_{l}", sv['y'], dxn, 'tn', 1024, 1024, 2048)
    doa, dob, doc = dy[:, :AW], dy[:, AW:AW + BW], dy[:, AW + BW:]
    tbo = min(TB_OUT, T)
    do, dzc, donw = rb_bwd(f"gdn_out_bwd_{l}", functools.partial(gdn_out_block, cfg), n, tbo, T // tbo,
                           [Row(sv['o'], CW, 0, F32), Row(p, CW, cfg.blk('zc'), BF16)], [], [lp['onw']], [doc], [True])
    dintra, got_rest = gdn_scan_bwd(f"gdn_scan_bwd_{l}", cfg, nseq, *sv['intra_outs'], sv['s_in'], do,
                                    comm=None if rs is None else rs.scatter([1, 2]))
    dqn, dkn, dv, dg, dbeta = rb_bwd(
        f"gdn_intra_bwd_{l}", functools.partial(gdn_intra_rows, cfg), n, TB_INTRA_BWD, T // TB_INTRA_BWD,
        [Row(sv['qn'], CW, 0, F32), Row(sv['kn'], CW, 0, F32), Row(sv['v'], CW, 0, F32), Row(sv['g'], LANE, 0, F32),
         Row(sv['beta'], LANE, 0, F32), Row(sv['inv'], cfg.CH * CHUNK)], [], [], list(dintra), [])
    rows, halos = _prep_io(cfg, p, True)
    tbp = min(TB_PREP, T)
    dqc, dkc, dvc, dbc, dac, dccw, dalog, ddtb = rb_bwd(
        f"gdn_prep_bwd_{l}", functools.partial(gdn_prep_block, cfg), n, tbp, T // tbp, rows, halos,
        [lp['ccw'], lp['alog'], lp['dtb']], [dqn, dkn, dv, dg, dbeta], [True] * 3)
    rows, halos = _conv_io(cfg, p, True)
    tbb = min(TB_CONV, T)
    conv_grads = rb_bwd(
        f"conv_bwd_{l}", functools.partial(conv_block, cfg), n, tbb, T // tbb, rows, halos,
        [lp['cw'], lp['cb'], lp['lw'], lp['lb'], lp['pw'], lp['pb']], [dob], [True] * 6,
        comm=None if rs is None else rs.scatter([0]))
    got = None
    if rs is not None:
        conv_grads, got_w_in = conv_grads
        got = got_w_in + got_rest
    dub, dzb, dcw, dcb, dlw, dlb, dpw, dpb = conv_grads
    rows, halos = _attn_io(cfg, p, cos, sin, True)
    dqa, dza, dka, dva, dqnw, dknw, dsinks_row = rb_bwd(
        f"attn_bwd_{l}", functools.partial(attn_block, cfg), n, ATTN_BLOCK, T // ATTN_BLOCK, rows, halos,
        [lp['qnw'], lp['knw'], lp['sinks_row']], [doa], [True] * 3)
    dgroups = dict(qa=dqa, za=dza, qc=dqc, kc=dkc, vc=dvc, zc=dzc, ka=dka, va=dva, ub=dub, zb=dzb, bc=dbc, ac=dac)
    dp_blocks = chip_blocks(cfg, dgroups, N_CHIPS)
    dw_in = grad_w_blocks(f"dwp_{l}", sv['h'], dp_blocks, 1024, 2048)
    mine = None if own_rs is None else own_rs(dict(w_in_blocks=dw_in, w_out=dwo, b_pw_w=dpw))
    got_mine = None
    if mine is None:
        dh = grad_h_blocks(f"dh_{l}", dp_blocks, wb, 1024, 1024)
        dx, dnw = norm_bwd(f"norm_bwd_{l}", sv['x'], lp['nw'], dh, dxn, 256)
    elif l > 0:
        dh, mine_received = grad_h_blocks(f"dh_{l}", dp_blocks, wb, 1024, 1024, comm=mine.swap())
        mine.add(mine_received)
        dx, dnw = norm_bwd(f"norm_bwd_{l}", sv['x'], lp['nw'], dh, dxn, 256)
    else:
        mine.add(run_comm(f"rs{l}_swap_halves", mine.swap()))
        dh, got_w_in = grad_h_blocks(f"dh_{l}", dp_blocks, wb, 1024, 1024, comm=mine.scatter([0]))
        dx, dnw, got_others = norm_bwd(f"norm_bwd_{l}", sv['x'], lp['nw'], dh, dxn, 256, comm=mine.scatter([1, 2]))
        got_mine = got_w_in + got_others
    grads = dict(
        norm_w=dnw[0], w_in_blocks=dw_in, q_norm_w=dqnw[0], k_norm_w=dknw[0],
        sinks=dsinks_row.reshape(cfg.AQH, A_HEAD)[:, 0],
        b_conv_w=dcw, b_conv_b=dcb[0], b_ln_w=dlw[0], b_ln_b=dlb[0], b_pw_w=dpw, b_pw_b=dpb[0],
        c_conv_w=dccw, c_a_log=dalog[0, :cfg.CH], c_dt_bias=ddtb[0, :cfg.CH], c_onorm_w=donw[0], w_out=dwo)
    return dx, grads, (got, mine, got_mine)


def rope_for(cfg, positions):
    n = positions.size
    inv_freq = ROPE_THETA ** (-np.arange(0, ROT_DIM, 2, dtype=np.float32) / ROT_DIM)
    freq_row = np.zeros((1, A_HEAD), np.float32)
    freq_row[0, :ROT_DIM] = np.concatenate([inv_freq, inv_freq])
    return rope_tables("rope_tables", positions.reshape(n, 1), jnp.asarray(freq_row))


def local_step(cfg, x, positions, prm, wps, wos, target):
    nseq = x.shape[0]
    n = nseq * cfg.T
    cos, sin = rope_for(cfg, positions)
    lps = [_layer_params(cfg, {k: v[l] for k, v in prm.items()}) for l in range(DEPTH)]
    saved = []
    xl = x.reshape(n, cfg.D)
    for l in range(DEPTH):
        xl, sv, _ = layer_forward(cfg, l, xl, lps[l], wps[l], wos[l], cos, sin,
                                  target=target.reshape(n, cfg.D) if l == DEPTH - 1 else None)
        saved.append(sv)
    dx, loss = xl
    grads = [None] * DEPTH
    for l in reversed(range(DEPTH)):
        groups = {k: wps[l][:, off:off + wd] for k, (off, wd) in cfg.g.items()}
        wb = chip_blocks(cfg, groups, N_CHIPS)
        dx, grads[l], _ = layer_backward(cfg, l, dx, saved[l], lps[l], wb, wos[l], cos, sin)
    return loss, dx.reshape(x.shape), grads


N_CHIPS = 4
N_DEV = 8


def _place():
    return lax.axis_index("x"), lax.axis_index("y"), lax.axis_index("c")


def _other_chips(x, y):
    return [(1 - x, y), (x, 1 - y), (1 - x, 1 - y)]


def _remote(src, dst, send, recv, to):
    return pltpu.make_async_remote_copy(src_ref=src, dst_ref=dst, send_sem=send, recv_sem=recv, device_id=to,
                                        device_id_type=MESH)


def gather_comm(arrs):
    n = len(arrs)

    def half(c):
        return [pl.ds(c * (a.shape[0] // 2), a.shape[0] // 2) for a in arrs]

    def first_copies(ins, outs, send, recv):
        x, y, c = _place()
        me = 2 * x + y
        mine = half(c)
        return [_remote(ins[i].at[mine[i]], outs[i].at[me, mine[i]], send.at[i, j], recv.at[i, j], (cx, cy, c))
                for i in range(n) for j, (cx, cy) in enumerate(_other_chips(x, y))]

    def start(ins, outs, sems):
        for cp in first_copies(ins, outs, *sems):
            cp.start()

    def finish(ins, outs, sems):
        send, recv = sems
        x, y, c = _place()
        chips = _other_chips(x, y)
        sib = (x, y, 1 - c)
        passed = []
        mine, other = half(c), half(1 - c)
        for i in range(n):
            for j, (cx, cy) in enumerate(chips):
                blk = outs[i].at[2 * cx + cy, mine[i]]
                _remote(blk, blk, send.at[i, j], recv.at[i, j], (x, y, c)).wait_recv()
                cp = _remote(blk, blk, send.at[i, 3 + j], recv.at[i, 3 + j], sib)
                cp.start()
                passed.append(cp)
        for i in range(n):
            for j, (cx, cy) in enumerate(chips):
                blk = outs[i].at[2 * cx + cy, other[i]]
                _remote(blk, blk, send.at[i, 3 + j], recv.at[i, 3 + j], sib).wait_recv()
        for cp in first_copies(ins, outs, send, recv) + passed:
            cp.wait_send()

    return Comm(arrs, [jax.ShapeDtypeStruct((N_CHIPS,) + a.shape, a.dtype) for a in arrs],
                [pltpu.SemaphoreType.DMA((n, 6)), pltpu.SemaphoreType.DMA((n, 6))], start, finish)


def fill_own(gathered, arrs):
    me = 2 * lax.axis_index("x") + lax.axis_index("y")
    return [lax.dynamic_update_index_in_dim(o, a, me, 0) for o, a in zip(gathered, arrs)]


def swap_comm(arrs):
    n = len(arrs)

    def copies(ins, outs, send, recv):
        x, y, c = _place()
        return [_remote(ins[i].at[:, 1 - c], outs[i], send.at[i], recv.at[i], (x, y, 1 - c)) for i in range(n)]

    def start(ins, outs, sems):
        for cp in copies(ins, outs, *sems):
            cp.start()

    def finish(ins, outs, sems):
        for cp in copies(ins, outs, *sems):
            cp.wait()

    return Comm(arrs, [jax.ShapeDtypeStruct((a.shape[0],) + a.shape[2:], a.dtype) for a in arrs],
                [pltpu.SemaphoreType.DMA((n,)), pltpu.SemaphoreType.DMA((n,))], start, finish)


def scatter_comm(arrs):
    n = len(arrs)

    def copies(ins, outs, send, recv):
        x, y, c = _place()
        return [_remote(ins[i].at[2 * cx + cy], outs[i].at[j], send.at[i, j], recv.at[i, j], (cx, cy, c))
                for i in range(n) for j, (cx, cy) in enumerate(_other_chips(x, y))]

    def start(ins, outs, sems):
        for cp in copies(ins, outs, *sems):
            cp.start()

    def finish(ins, outs, sems):
        send, recv = sems
        x, y, c = _place()
        for i in range(n):
            for j in range(3):
                blk = outs[i].at[j]
                _remote(blk, blk, send.at[i, j], recv.at[i, j], (x, y, c)).wait_recv()
        for cp in copies(ins, outs, send, recv):
            cp.wait_send()

    return Comm(arrs, [jax.ShapeDtypeStruct((3,) + a.shape[1:], a.dtype) for a in arrs],
                [pltpu.SemaphoreType.DMA((n, 3)), pltpu.SemaphoreType.DMA((n, 3))], start, finish)


def share_comm(arrs):
    n = len(arrs)

    def copies(outs, send, recv):
        x, y, c = _place()
        return [_remote(outs[i].at[c], outs[i].at[c], send.at[i], recv.at[i], (x, y, 1 - c)) for i in range(n)]

    def start(ins, outs, sems):
        for cp in copies(outs, *sems):
            cp.start()

    def finish(ins, outs, sems):
        send, recv = sems
        x, y, c = _place()
        for i in range(n):
            blk = outs[i].at[1 - c]
            _remote(blk, blk, send.at[i], recv.at[i], (x, y, c)).wait_recv()
        for cp in copies(outs, send, recv):
            cp.wait_send()

    return Comm(arrs, [jax.ShapeDtypeStruct(a.shape, a.dtype) for a in arrs],
                [pltpu.SemaphoreType.DMA((n,)), pltpu.SemaphoreType.DMA((n,))], start, finish,
                aliases={i: i for i in range(n)})


def all_reduce_small(name, packed):
    r = packed.shape[0]

    def body(in_ref, out_ref, buf, send, recv):
        x, y, c = _place()
        me = 4 * x + 2 * y + c
        buf[me] = in_ref[...]
        flips = [(fx, fy, fc) for fx in (0, 1) for fy in (0, 1) for fc in (0, 1) if (fx, fy, fc) != (0, 0, 0)]
        peers = [((x + fx) % 2, (y + fy) % 2, (c + fc) % 2) for fx, fy, fc in flips]
        cps = [_remote(in_ref, buf.at[me], send.at[k], recv.at[k], peer) for k, peer in enumerate(peers)]
        for cp in cps:
            cp.start()
        for k, (px, py, pc) in enumerate(peers):
            blk = buf.at[4 * px + 2 * py + pc]
            _remote(blk, blk, send.at[k], recv.at[k], (x, y, c)).wait_recv()
        for cp in cps:
            cp.wait_send()
        acc = buf[0]
        for d in range(1, N_DEV):
            acc = acc + buf[d]
        out_ref[...] = acc

    vm = pl.BlockSpec(memory_space=pltpu.VMEM)
    return pl.pallas_call(
        body, name=name, in_specs=[vm], out_specs=vm, out_shape=jax.ShapeDtypeStruct(packed.shape, F32),
        scratch_shapes=[pltpu.VMEM((N_DEV, r, LANE), F32), pltpu.SemaphoreType.DMA((N_DEV - 1,)),
                        pltpu.SemaphoreType.DMA((N_DEV - 1,))],
    )(packed)


def add_own_half(name, g, a, c_idx, tr):
    nch, _, r, cc = g.shape
    tr = min(tr, r)

    def body(c_ref, g_ref, a_ref, o_ref):
        o_ref[...] = (g_ref[0] + a_ref[...]).astype(o_ref.dtype)

    return pl.pallas_call(
        body, name=name,
        grid_spec=pltpu.PrefetchScalarGridSpec(
            num_scalar_prefetch=1, grid=(nch, r // tr),
            in_specs=[pl.BlockSpec((1, 1, tr, cc), lambda j, i, c_ref: (j, c_ref[0], i, 0)),
                      pl.BlockSpec((1, tr, cc), lambda j, i, c_ref: (j, i, 0))],
            out_specs=pl.BlockSpec((1, tr, cc), lambda j, i, c_ref: (j, i, 0))),
        out_shape=jax.ShapeDtypeStruct(a.shape, BF16),
        compiler_params=_cparams(("parallel", "parallel")),
    )(c_idx, g, a)


def sum_chips(name, p, b, idx, tr):
    _, r, cc = p.shape
    tr = min(tr, r)

    def body(idx_ref, p_ref, b_ref, o_ref):
        acc = p_ref[0].astype(F32)
        for k in range(3):
            acc = acc + b_ref[k].astype(F32)
        o_ref[0] = acc

    return pl.pallas_call(
        body, name=name,
        grid_spec=pltpu.PrefetchScalarGridSpec(
            num_scalar_prefetch=1, grid=(r // tr,),
            in_specs=[pl.BlockSpec((1, tr, cc), lambda i, s: (s[0], i, 0)),
                      pl.BlockSpec((3, tr, cc), lambda i, s: (0, i, 0))],
            out_specs=pl.BlockSpec((1, tr, cc), lambda i, s: (s[1], i, 0))),
        out_shape=jax.ShapeDtypeStruct((2, r, cc), F32),
        compiler_params=_cparams(("parallel",)),
    )(idx, p, b)


class GradReduce:
    def __init__(self, tag, parts, chip, c_idx):
        self.tag, self.c_idx = tag, c_idx
        self.parts = [p.reshape(p.shape[0], 2, p.shape[1] // 2, p.shape[2]) for p in parts]
        self.idx = jnp.concatenate([chip.astype(jnp.int32).reshape(1), c_idx])

    def swap(self):
        return swap_comm(self.parts)

    def add(self, received):
        self.part = [add_own_half(f"rs{self.tag}_add_sibling_{t}", g, a, self.c_idx, 128)
                     for t, (g, a) in enumerate(zip(self.parts, received))]

    def scatter(self, which=None):
        return scatter_comm(self.part if which is None else [self.part[t] for t in which])

    def finish(self, got):
        red = [sum_chips(f"rs{self.tag}_sum_chips_{t}", p, b, self.idx, 128) for t, (p, b) in enumerate(zip(self.part, got))]
        out = run_comm(f"rs{self.tag}_share_halves", share_comm(red))
        return [o.reshape(-1, o.shape[-1]) for o in out]


def adamw_many(name, ws, gs, ms, vs):
    n = len(ws)

    def body(*refs):
        for i in range(n):
            w_ref, g_ref, m_ref, v_ref = (refs[k * n + i] for k in range(4))
            d_ref, mo_ref, vo_ref = (refs[(4 + k) * n + i] for k in range(3))
            g = g_ref[...]
            m = ADAM_B1 * m_ref[...] + (1.0 - ADAM_B1) * g
            v = ADAM_B2 * v_ref[...] + (1.0 - ADAM_B2) * jnp.square(g)
            m_hat = m / (1.0 - ADAM_B1 ** ADAM_STEP)
            v_hat = v / (1.0 - ADAM_B2 ** ADAM_STEP)
            d_ref[...] = -ADAM_LR * (m_hat / (jnp.sqrt(v_hat) + ADAM_EPS) + ADAM_WD * w_ref[...])
            mo_ref[...] = m
            vo_ref[...] = v

    vm = pl.BlockSpec(memory_space=pltpu.VMEM)
    return pl.pallas_call(
        body, name=name, in_specs=[vm] * (4 * n), out_specs=[vm] * (3 * n),
        out_shape=[jax.ShapeDtypeStruct(a.shape, F32) for a in ws] * 3,
    )(*ws, *gs, *ms, *vs)


def adamw_layers(name, w, g0, g1, m, v, tb):
    _, r, cc = w.shape
    tb = min(tb, r)
    nb = r // tb

    def body(w_ref, g0_ref, g1_ref, m_ref, v_ref, g_ref, d_ref, mo_ref, vo_ref):
        g = jnp.where(pl.program_id(0) == 0, g0_ref[...], g1_ref[...])
        m = ADAM_B1 * m_ref[0] + (1.0 - ADAM_B1) * g
        v = ADAM_B2 * v_ref[0] + (1.0 - ADAM_B2) * jnp.square(g)
        m_hat = m / (1.0 - ADAM_B1 ** ADAM_STEP)
        v_hat = v / (1.0 - ADAM_B2 ** ADAM_STEP)
        g_ref[0] = g
        d_ref[0] = -ADAM_LR * (m_hat / (jnp.sqrt(v_hat) + ADAM_EPS) + ADAM_WD * w_ref[0])
        mo_ref[0] = m
        vo_ref[0] = v

    spec = pl.BlockSpec((1, tb, cc), lambda l, i: (l, i, 0))
    g0_spec = pl.BlockSpec((tb, cc), lambda l, i: (jnp.where(l == 0, i, nb - 1), 0))
    g1_spec = pl.BlockSpec((tb, cc), lambda l, i: (jnp.where(l == 1, i, 0), 0))
    return pl.pallas_call(
        body, name=name, grid=(2, nb), in_specs=[spec, g0_spec, g1_spec, spec, spec], out_specs=[spec] * 4,
        out_shape=[jax.ShapeDtypeStruct(w.shape, F32)] * 4,
        compiler_params=_cparams(("arbitrary", "arbitrary")),
    )(w, g0, g1, m, v)


def adamw_cols_major(name, w, g0, g1, m, v, tb=LANE):
    wt, mt, vt = (jnp.transpose(a, (2, 0, 1)) for a in (w, m, v))
    cc, _, r = wt.shape

    def body(w_ref, g0_ref, g1_ref, m_ref, v_ref, g_ref, d_ref, mo_ref, vo_ref):
        for l, gl_ref in enumerate((g0_ref, g1_ref)):
            g = gl_ref[...].T
            m = ADAM_B1 * m_ref[:, l, :] + (1.0 - ADAM_B1) * g
            v = ADAM_B2 * v_ref[:, l, :] + (1.0 - ADAM_B2) * jnp.square(g)
            m_hat = m / (1.0 - ADAM_B1 ** ADAM_STEP)
            v_hat = v / (1.0 - ADAM_B2 ** ADAM_STEP)
            g_ref[:, l, :] = g
            d_ref[:, l, :] = -ADAM_LR * (m_hat / (jnp.sqrt(v_hat) + ADAM_EPS) + ADAM_WD * w_ref[:, l, :])
            mo_ref[:, l, :] = m
            vo_ref[:, l, :] = v

    spec = pl.BlockSpec((tb, 2, r), lambda i: (i, 0, 0))
    gspec = pl.BlockSpec((r, tb), lambda i: (0, i))
    outs = pl.pallas_call(
        body, name=name, grid=(pl.cdiv(cc, tb),), in_specs=[spec, gspec, gspec, spec, spec], out_specs=[spec] * 4,
        out_shape=[jax.ShapeDtypeStruct(wt.shape, F32)] * 4,
        compiler_params=_cparams(("parallel",)),
    )(wt, g0, g1, mt, vt)
    return [jnp.transpose(o, (1, 2, 0)) for o in outs]


def _pack(arrs):
    flat = jnp.concatenate([a.reshape(-1).astype(F32) for a in arrs])
    pad = (-flat.shape[0]) % (8 * LANE)
    return jnp.pad(flat, (0, pad)).reshape(-1, LANE)


def _unpack(packed, shapes):
    flat = packed.reshape(-1)
    out, off = [], 0
    for s in shapes:
        size = math.prod(s)
        out.append(flat[off:off + size].reshape(s))
        off += size
    return out


BIG = ('w_in', 'w_out', 'b_pw_w')
SMALL = tuple(k for k in WEIGHTS if k not in BIG)
CHIP_SHARDED_SMALL = {'b_conv_w': 2, 'c_conv_w': 2}


def kernel(x, positions, norm_w, w_in, q_norm_w, k_norm_w, sinks, b_conv_w, b_conv_b, b_ln_w, b_ln_b, b_pw_w, b_pw_b, c_conv_w, c_a_log, c_dt_bias, c_onorm_w, w_out, loss_target, m_norm_w, m_w_in, m_q_norm_w, m_k_norm_w, m_sinks, m_b_conv_w, m_b_conv_b, m_b_ln_w, m_b_ln_b, m_b_pw_w, m_b_pw_b, m_c_conv_w, m_c_a_log, m_c_dt_bias, m_c_onorm_w, m_w_out, v_norm_w, v_w_in, v_q_norm_w, v_k_norm_w, v_sinks, v_b_conv_w, v_b_conv_b, v_b_ln_w, v_b_ln_b, v_b_pw_w, v_b_pw_b, v_c_conv_w, v_c_a_log, v_c_dt_bias, v_c_onorm_w, v_w_out):
    cfg = Cfg(x.shape[-1], x.shape[-2])
    w = dict(norm_w=norm_w, w_in=w_in, q_norm_w=q_norm_w, k_norm_w=k_norm_w, sinks=sinks, b_conv_w=b_conv_w,
             b_conv_b=b_conv_b, b_ln_w=b_ln_w, b_ln_b=b_ln_b, b_pw_w=b_pw_w, b_pw_b=b_pw_b, c_conv_w=c_conv_w,
             c_a_log=c_a_log, c_dt_bias=c_dt_bias, c_onorm_w=c_onorm_w, w_out=w_out)
    m = dict(norm_w=m_norm_w, w_in=m_w_in, q_norm_w=m_q_norm_w, k_norm_w=m_k_norm_w, sinks=m_sinks,
             b_conv_w=m_b_conv_w, b_conv_b=m_b_conv_b, b_ln_w=m_b_ln_w, b_ln_b=m_b_ln_b, b_pw_w=m_b_pw_w,
             b_pw_b=m_b_pw_b, c_conv_w=m_c_conv_w, c_a_log=m_c_a_log, c_dt_bias=m_c_dt_bias, c_onorm_w=m_c_onorm_w,
             w_out=m_w_out)
    v = dict(norm_w=v_norm_w, w_in=v_w_in, q_norm_w=v_q_norm_w, k_norm_w=v_k_norm_w, sinks=v_sinks,
             b_conv_w=v_b_conv_w, b_conv_b=v_b_conv_b, b_ln_w=v_b_ln_w, b_ln_b=v_b_ln_b, b_pw_w=v_b_pw_w,
             b_pw_b=v_b_pw_b, c_conv_w=v_c_conv_w, c_a_log=v_c_a_log, c_dt_bias=v_c_dt_bias, c_onorm_w=v_c_onorm_w,
             w_out=v_w_out)
    chip = 2 * lax.axis_index("x") + lax.axis_index("y")
    c_idx = lax.axis_index("c").astype(jnp.int32).reshape(1)
    D, T = cfg.D, cfg.T
    nseq = x.shape[0]
    n = nseq * T
    w_in_b, w_out_b = w_in.astype(BF16), w_out.astype(BF16)

    def permuted(g_in):
        return permute_w_in(cfg, jnp.concatenate(list(g_in), axis=1))

    def layer_prm(l, g_pw, g_bcw, g_ccw):
        prm = {k: w[k][l] for k in SMALL}
        prm['b_pw_w'] = g_pw.reshape(cfg.BW, cfg.BW)
        prm['b_conv_w'] = jnp.concatenate(list(g_bcw[:, l]), axis=1)
        prm['c_conv_w'] = jnp.concatenate(list(g_ccw[:, l]), axis=1)
        return _layer_params(cfg, prm)

    (g_in0,) = fill_own(run_comm("gather_weights_0", gather_comm([w_in_b[0]])), [w_in_b[0]])
    cos, sin = rope_for(cfg, positions)
    early = [w_out_b[0], b_pw_w[0], b_conv_w, c_conv_w]
    top, bottom = w_in_b[1][:D // 2], w_in_b[1][D // 2:]
    late = [w_out_b[1], b_pw_w[1]]
    conv_ws = {}

    def layer0_rest(rode):
        g_out0, g_pw0, conv_ws['b'], conv_ws['c'] = fill_own(rode, early)
        return layer_prm(0, g_pw0, conv_ws['b'], conv_ws['c']), g_out0.reshape(D, D)

    x1, sv0, rode = layer_forward(
        cfg, 0, x.reshape(n, D), (norm_w[0].reshape(1, -1), layer0_rest), permuted(g_in0), None, cos, sin,
        comms=dict(in_proj=gather_comm(early), attn=gather_comm([top]), conv=gather_comm(late),
                   intra=gather_comm([bottom])))
    lp0, wo0 = sv0['lp'], sv0['wo']
    (g_top,), (g_bottom,) = fill_own(rode['attn'], [top]), fill_own(rode['intra'], [bottom])
    g_out1, g_pw1 = fill_own(rode['conv'], late)
    g_in1 = jnp.concatenate([g_top, g_bottom], axis=1)
    wp1, wo1 = permuted(g_in1), g_out1.reshape(D, D)
    lp1 = layer_prm(1, g_pw1, conv_ws['b'], conv_ws['c'])
    (dx2, loss_local), sv1, _ = layer_forward(cfg, 1, x1, lp1, wp1, wo1, cos, sin, target=loss_target.reshape(n, D))

    def partials(gr):
        return [gr['w_in_blocks'], gr['w_out'].reshape(N_CHIPS, D // N_CHIPS, D),
                gr['b_pw_w'].reshape(N_CHIPS, cfg.BW // N_CHIPS, cfg.BW)]

    dx1, gr1, (_, rs1, _) = layer_backward(cfg, 1, dx2, sv1, lp1, g_in1, wo1, cos, sin,
                                        own_rs=lambda gr: GradReduce(1, partials(gr), chip, c_idx))
    dx0, gr0, (got1, rs0, got0) = layer_backward(cfg, 0, dx1, sv0, lp0, g_in0, wo0, cos, sin, rs=rs1,
                                           own_rs=lambda gr: GradReduce(0, partials(gr), chip, c_idx))
    red1 = rs1.finish(got1)
    red0 = rs0.finish(got0)
    grad_x = dx0.reshape(x.shape)
    grads = [gr0, gr1]

    small_parts = [jnp.stack([grads[l][k] for l in range(DEPTH)]) for k in SMALL] + [loss_local.reshape(1)]
    *small_red, loss = _unpack(all_reduce_small("all_reduce_small", _pack(small_parts)), [a.shape for a in small_parts])
    loss = loss.reshape(())
    g = {}
    for k, a in zip(SMALL, small_red):
        if k in CHIP_SHARDED_SMALL:
            ax = CHIP_SHARDED_SMALL[k]
            width = a.shape[ax] // N_CHIPS
            a = lax.dynamic_slice_in_dim(a, chip * width, width, axis=ax)
        g[k] = a

    delta, new_m, new_v = {}, {}, {}
    for k, g0, g1 in zip(BIG, red0, red1):
        update = adamw_layers if w[k].shape[-1] % LANE == 0 else adamw_cols_major
        g[k], delta[k], new_m[k], new_v[k] = update(f"adamw_{k}", w[k], g0, g1, m[k], v[k], 128)
    outs = adamw_many("adamw_small", *[[d[k] for k in SMALL] for d in (w, g, m, v)])
    for i, k in enumerate(SMALL):
        delta[k], new_m[k], new_v[k] = outs[i], outs[len(SMALL) + i], outs[2 * len(SMALL) + i]
    return (loss, grad_x, *[g[k] for k in WEIGHTS], *[delta[k] for k in WEIGHTS], *[new_m[k] for k in WEIGHTS],
            *[new_v[k] for k in WEIGHTS])
```

```python
import functools
import math

import numpy as np
import jax
import jax.numpy as jnp
from jax import lax
from jax.experimental import pallas as pl
from jax.experimental.pallas import tpu as pltpu

F32 = jnp.float32
BF16 = jnp.bfloat16
HI = lax.Precision.HIGHEST
MESH = pl.DeviceIdType.MESH

DEPTH = 2
A_HEAD = 64
A_GROUP = 3
ATTN_BLOCK = 128
ROT_DIM = 16
ROPE_THETA = 500000.0
B_CONV = 31
B_HALO = 32
C_HEAD = 128
C_CONV = 4
C_HALO = 8
CHUNK = 64
EPS = 1e-6
LANE = 128

ADAM_LR = 0.001
ADAM_B1 = 0.9
ADAM_B2 = 0.999
ADAM_EPS = 1e-08
ADAM_WD = 0.01
ADAM_STEP = 10

VMEM_LIMIT = 56 * 1024 * 1024

WEIGHTS = ['norm_w', 'w_in', 'q_norm_w', 'k_norm_w', 'sinks', 'b_conv_w', 'b_conv_b', 'b_ln_w', 'b_ln_b',
           'b_pw_w', 'b_pw_b', 'c_conv_w', 'c_a_log', 'c_dt_bias', 'c_onorm_w', 'w_out']


class Cfg:
    def __init__(self, d_model=2048, seq=2048):
        self.D = d_model
        self.T = seq
        self.AW = 3 * d_model // 8
        self.AQH = self.AW // A_HEAD
        self.AKH = self.AQH // A_GROUP
        self.AKW = self.AKH * A_HEAD
        self.BW = d_model // 4
        self.CH = (d_model - self.AW - self.BW) // C_HEAD
        self.CW = self.CH * C_HEAD
        AW, AKW, BW, CW, CH = self.AW, self.AKW, self.BW, self.CW, self.CH
        orig = [('qa', AW), ('ka', AKW), ('va', AKW), ('za', AW), ('ub', 2 * BW), ('zb', BW),
                ('qc', CW), ('kc', CW), ('vc', CW), ('bc', CH), ('ac', CH), ('zc', CW)]
        self.orig = {}
        off = 0
        for n, w in orig:
            self.orig[n] = (off, w)
            off += w
        self.IN_COLS = off
        order = ['qa', 'za', 'qc', 'kc', 'vc', 'zc', 'ka', 'va', 'ub', 'zb', 'bc', 'ac']
        self.order = order
        self.g = {}
        off = 0
        for n in order:
            w = self.orig[n][1]
            wp = LANE if n in ('bc', 'ac') else w
            assert off % wp == 0, (n, off, wp)
            self.g[n] = (off, wp)
            off += wp
        self.WP = off

    def blk(self, name):
        off, w = self.g[name]
        return off // w


def _cparams(sem, vmem=VMEM_LIMIT):
    return pltpu.CompilerParams(dimension_semantics=sem, vmem_limit_bytes=vmem)


def _silu(x):
    return x * jax.nn.sigmoid(x)


ANY = pl.BlockSpec(memory_space=pl.ANY)


class Comm:
    def __init__(self, ins, out_shapes, sems, start, finish, aliases=None):
        self.ins, self.out_shapes, self.sems = list(ins), list(out_shapes), list(sems)
        self.start, self.finish, self.aliases = start, finish, dict(aliases or {})


def call_with_comm(body, name, grid, in_specs, out_specs, out_shape, scratch_shapes, semantics, args, comm=None):
    in_specs, out_specs, out_shape, scratch_shapes = list(in_specs), list(out_specs), list(out_shape), list(scratch_shapes)
    if comm is None:
        outs = pl.pallas_call(body, name=name, grid=grid, in_specs=in_specs, out_specs=out_specs, out_shape=out_shape,
                              scratch_shapes=scratch_shapes, compiler_params=_cparams(semantics))(*args)
        return list(outs), []
    ni, no, ns = len(in_specs), len(out_specs), len(scratch_shapes)
    nci, nco = len(comm.ins), len(comm.out_shapes)

    def wrapped(*refs):
        h_in, c_in = refs[:ni], refs[ni:ni + nci]
        h_out, c_out = refs[ni + nci:ni + nci + no], refs[ni + nci + no:ni + nci + no + nco]
        h_scr, c_sems = refs[ni + nci + no + nco:ni + nci + no + nco + ns], refs[ni + nci + no + nco + ns:]
        ids = [pl.program_id(d) for d in range(len(grid))]
        first = functools.reduce(jnp.logical_and, [i == 0 for i in ids])
        last = functools.reduce(jnp.logical_and, [i == g - 1 for i, g in zip(ids, grid)])

        @pl.when(first)
        def _():
            comm.start(c_in, c_out, c_sems)

        body(*h_in, *h_out, *h_scr)

        @pl.when(last)
        def _():
            comm.finish(c_in, c_out, c_sems)

    outs = pl.pallas_call(
        wrapped, name=name, grid=grid, in_specs=in_specs + [ANY] * nci, out_specs=out_specs + [ANY] * nco,
        out_shape=out_shape + comm.out_shapes, scratch_shapes=scratch_shapes + comm.sems,
        input_output_aliases={ni + k: no + v for k, v in comm.aliases.items()},
        compiler_params=_cparams(("arbitrary",) * len(grid)),
    )(*args, *comm.ins)
    return list(outs[:no]), list(outs[no:])


def run_comm(name, comm):
    nci, nco = len(comm.ins), len(comm.out_shapes)

    def body(*refs):
        c_in, c_out, c_sems = refs[:nci], refs[nci:nci + nco], refs[nci + nco:]
        comm.start(c_in, c_out, c_sems)
        comm.finish(c_in, c_out, c_sems)

    return pl.pallas_call(
        body, name=name, in_specs=[ANY] * nci, out_specs=[ANY] * nco, out_shape=comm.out_shapes,
        scratch_shapes=comm.sems, input_output_aliases=comm.aliases,
    )(*comm.ins)


def _bdot(a, b, ca, cb, precision=HI):
    dims = (((ca,), (cb,)), ((0,), (0,)))
    if precision is HI and a.dtype == F32:
        ah = a.astype(BF16)
        bh = b.astype(BF16)
        al = (a - ah.astype(F32)).astype(BF16)
        bl = (b - bh.astype(F32)).astype(BF16)
        dg = lambda p, q: lax.dot_general(p, q, dims, preferred_element_type=F32)
        return dg(ah, bh) + (dg(ah, bl) + dg(al, bh))
    return lax.dot_general(a, b, dims, precision=precision, preferred_element_type=F32)


def _rope_matrix(nb):
    i = lax.broadcasted_iota(jnp.int32, (nb, A_HEAD, A_HEAD), 1)
    j = lax.broadcasted_iota(jnp.int32, (nb, A_HEAD, A_HEAD), 2)
    half = ROT_DIM // 2
    neg = (j < half) & (i == j + half)
    pos = (j >= half) & (j < ROT_DIM) & (i == j - half)
    return jnp.where(neg, -1.0, jnp.where(pos, 1.0, 0.0)).astype(F32)


def _norm_rope(xh, w, cos, sin):
    y = xh * lax.rsqrt(jnp.mean(xh * xh, axis=-1, keepdims=True) + EPS) * w
    return y * cos + _bdot(y, _rope_matrix(xh.shape[0]), 2, 1) * sin


def attn_block(cfg, first, q, za, kc, vc, cosc, sinc, kp, vp, cosp, sinp, qnw, knw, sinks_row):
    blk = ATTN_BLOCK
    nq, nk = cfg.AQH, cfg.AKH
    qi = lax.broadcasted_iota(jnp.int32, (blk, 2 * blk), 0)
    kj = lax.broadcasted_iota(jnp.int32, (blk, 2 * blk), 1)
    dist = qi + blk - kj
    valid = ((dist >= 0) & (dist < blk) & (jnp.logical_not(first) | (kj >= blk)))[None]
    cos2 = jnp.concatenate([cosp, cosc], axis=0)
    sin2 = jnp.concatenate([sinp, sinc], axis=0)
    head = lambda x, h: x[:, A_HEAD * h:A_HEAD * (h + 1)]
    k2 = jnp.stack([jnp.concatenate([head(kp, h), head(kc, h)], axis=0) for h in range(nk)], axis=0)
    v2 = jnp.stack([jnp.concatenate([head(vp, h), head(vc, h)], axis=0) for h in range(nk)], axis=0)
    k2 = _norm_rope(k2, knw[None], cos2[None], sin2[None]).astype(BF16)
    v2 = v2.astype(BF16)
    k2 = jnp.stack([k2[h // A_GROUP] for h in range(nq)], axis=0)
    v2 = jnp.stack([v2[h // A_GROUP] for h in range(nq)], axis=0)
    qh = jnp.stack([head(q, h) for h in range(nq)], axis=0)
    qh = _norm_rope(qh, qnw[None], cosc[None], sinc[None]).astype(BF16)
    s = _bdot(qh, k2, 2, 2, None) * (A_HEAD ** -0.5)
    s = jnp.where(valid, s, -1e30)
    sink = jnp.stack([sinks_row[:, A_HEAD * h:A_HEAD * h + 1] for h in range(nq)], axis=0)
    m = jnp.maximum(jnp.max(s, axis=-1, keepdims=True), sink)
    e = jnp.exp(s - m)
    den = jnp.sum(e, axis=-1, keepdims=True) + jnp.exp(sink - m)
    o = _bdot((e / den).astype(BF16), v2, 2, 1, None)
    return (jnp.concatenate([o[h] for h in range(nq)], axis=1) * _silu(za),)


def conv_block(cfg, first, u, zb, uh, cw, cb, lw, lb, pw, pb):
    BW = cfg.BW
    tb = u.shape[0]
    uu = jnp.concatenate([uh, u], axis=0)
    h = uu[:, :BW] * jax.nn.sigmoid(uu[:, BW:])
    row = lax.broadcasted_iota(jnp.int32, h.shape, 0)
    h = jnp.where(first & (row < B_HALO), 0.0, h)
    acc = jnp.zeros((tb, BW), F32) + cb
    base = B_HALO - (B_CONV - 1)
    for k in range(B_CONV):
        acc = acc + cw[k:k + 1, :] * h[base + k:base + k + tb, :]
    mu = jnp.mean(acc, axis=-1, keepdims=True)
    var = jnp.mean(jnp.square(acc - mu), axis=-1, keepdims=True)
    y = (acc - mu) * lax.rsqrt(var + EPS) * lw + lb
    s = _silu(y)
    o = jnp.dot(s.astype(BF16), pw.astype(BF16), preferred_element_type=F32) + pb
    return (o * _silu(zb),)


def gdn_prep_block(cfg, first, xq, xk, xv, braw, araw, hq, hk, hv, cw, alog, dtb):
    CW = cfg.CW
    tb = xq.shape[0]
    outs = []
    for idx, (x, xh) in enumerate(((xq, hq), (xk, hk), (xv, hv))):
        xx = jnp.concatenate([jnp.where(first, 0.0, xh), x], axis=0)
        w = cw[:, idx * CW:(idx + 1) * CW]
        acc = jnp.zeros((tb, CW), F32)
        base = C_HALO - (C_CONV - 1)
        for k in range(C_CONV):
            acc = acc + w[k:k + 1, :] * xx[base + k:base + k + tb, :]
        y = _silu(acc)
        if idx < 2:
            parts = []
            for h in range(cfg.CH):
                yh = y[:, C_HEAD * h:C_HEAD * (h + 1)]
                parts.append(yh * lax.rsqrt(jnp.sum(yh * yh, axis=-1, keepdims=True) + EPS))
            y = jnp.concatenate(parts, axis=1)
        outs.append(y)
    beta = jax.nn.sigmoid(braw)
    g = -jnp.exp(alog) * jax.nn.softplus(araw + dtb)
    return outs[0], outs[1], outs[2], g, beta


def _inverse_unit_lower(low, eye):
    pw = low
    inv = eye - low
    for _ in range(5):
        pwb = pw.astype(BF16)
        pw = _bdot(pwb, pwb, 2, 1, None)
        inv = inv + _bdot(inv.astype(BF16), pw.astype(BF16), 2, 1, None)
    ax = inv + _bdot(low, inv, 2, 1)
    return inv + _bdot(inv, eye - ax, 2, 1)


@jax.custom_vjp
def _saved_inverse(low, inv):
    return inv


def _saved_inverse_fwd(low, inv):
    return inv, inv


def _saved_inverse_bwd(inv, d):
    dlow = -_bdot(_bdot(inv, d, 1, 1), inv, 2, 2)
    return dlow, jnp.zeros_like(inv)


_saved_inverse.defvjp(_saved_inverse_fwd, _saved_inverse_bwd)


def gdn_intra_rows(cfg, first, qn, kn, v, g, beta, inv_saved=None):
    c = CHUNK
    CH = cfg.CH
    nchunk = qn.shape[0] // c
    i = lax.broadcasted_iota(jnp.int32, (c, c), 0)
    j = lax.broadcasted_iota(jnp.int32, (c, c), 1)
    incl = (i >= j)[None]
    strict = (i > j)[None]
    eye = (i == j).astype(F32)[None]
    tri = (i >= j).astype(F32)
    rows = [slice(c * ci, c * (ci + 1)) for ci in range(nchunk)]
    gcs = [jnp.dot(tri, g[r], precision=HI, preferred_element_type=F32) for r in rows]
    pairs = [(ci, h) for ci in range(nchunk) for h in range(CH)]
    heads = lambda x, wd: jnp.stack([x[rows[ci], wd * h:wd * (h + 1)] for ci, h in pairs], axis=0)
    gch = jnp.stack([gcs[ci][:, h:h + 1] for ci, h in pairs], axis=0)
    bh = jnp.stack([beta[rows[ci], h:h + 1] for ci, h in pairs], axis=0)
    q = heads(qn, C_HEAD) * (C_HEAD ** -0.5)
    k = heads(kn, C_HEAD)
    vv = heads(v, C_HEAD)
    a = jnp.broadcast_to(gch, (len(pairs), c, c))
    diff = jnp.where(incl, a - jnp.swapaxes(a, 1, 2), 0.0)
    decay = jnp.where(incl, jnp.exp(diff), 0.0)
    kb = k * bh
    low = jnp.where(strict, _bdot(kb, k, 2, 2) * decay, 0.0)
    if inv_saved is None:
        inv = _inverse_unit_lower(low, eye)
    else:
        inv = _saved_inverse(low, heads(inv_saved, c))
    eg = jnp.exp(gch)
    sol = _bdot(inv, jnp.concatenate([vv * bh, kb * eg], axis=2), 2, 1)
    intra = jnp.where(incl, _bdot(q, k, 2, 2) * decay, 0.0)
    qg = q * eg
    kd = k * jnp.exp(gch[:, c - 1:c, :] - gch)
    glast = jnp.concatenate([jnp.broadcast_to(gc[c - 1:c, :], gc.shape) for gc in gcs], axis=0)

    def unstack(x):
        return jnp.concatenate([jnp.concatenate([x[ci * CH + h] for h in range(CH)], axis=1) for ci in range(nchunk)],
                               axis=0)

    outs = (unstack(sol[:, :, :C_HEAD]), unstack(sol[:, :, C_HEAD:]), unstack(qg), unstack(kd), unstack(intra), glast)
    return outs + (unstack(inv),) if inv_saved is None else outs


def gdn_state_step(S, u, w, qg, kd, intra, glast):
    v_new = u - _bdot(w, S, 2, 1)
    o = _bdot(qg, S, 2, 1) + _bdot(intra, v_new, 2, 1)
    S_next = S * jnp.exp(glast) + _bdot(kd, v_new, 1, 1)
    return o, S_next


def gdn_out_block(cfg, first, o, zc, onw):
    parts = []
    for h in range(cfg.CH):
        sl = slice(C_HEAD * h, C_HEAD * (h + 1))
        oh = o[:, sl]
        y = oh * lax.rsqrt(jnp.mean(oh * oh, axis=-1, keepdims=True) + EPS) * onw
        parts.append(y * _silu(zc[:, sl]))
    return (jnp.concatenate(parts, axis=1),)


def rms_block(x, nw):
    return x * lax.rsqrt(jnp.mean(x * x, axis=-1, keepdims=True) + EPS) * nw


class Row:
    def __init__(self, arr, width, colblk=0, grad=None):
        self.arr, self.width, self.colblk, self.grad = arr, width, colblk, grad


class Halo:
    def __init__(self, arr, width, colblk, hr, tie=None):
        self.arr, self.width, self.colblk, self.hr, self.tie = arr, width, colblk, hr, tie


def _row_specs(tb, rows, halos, params, pos):
    specs = [pl.BlockSpec((tb, r.width), lambda i, cb=r.colblk: (pos(i), cb)) for r in rows]
    specs += [pl.BlockSpec((h.hr, h.width),
                           lambda i, cb=h.colblk, m=tb // h.hr: (jnp.maximum(pos(i) * m - 1, 0), cb))
              for h in halos]
    specs += [pl.BlockSpec(p.shape, lambda i: (0, 0)) for p in params]
    return specs


def rb_fwd(name, fn, n, tb, bps, rows, halos, params, outs, comm=None):
    nr, nh, npar = len(rows), len(halos), len(params)

    def body(*refs):
        ins = refs[:nr + nh + npar]
        o_refs = refs[nr + nh + npar:]
        first = (pl.program_id(0) % bps) == 0
        res = fn(first, *[r[...] for r in ins])
        for ref, val in zip(o_refs, res):
            ref[...] = val.astype(ref.dtype)

    res, carried = call_with_comm(
        body, name, (n // tb,), _row_specs(tb, rows, halos, params, lambda i: i),
        [pl.BlockSpec((tb, w), lambda i: (i, 0)) for w, _ in outs],
        [jax.ShapeDtypeStruct((n, w), dt) for w, dt in outs], [], ("parallel",),
        [r.arr for r in rows] + [h.arr for h in halos] + list(params), comm)
    return (res, carried) if comm is not None else res


def rb_bwd(name, fn, n, tb, bps, rows, halos, params, douts, param_grads, comm=None):
    nr, nh, npar, nd = len(rows), len(halos), len(params), len(douts)
    nblk = n // tb
    grow = [k for k, r in enumerate(rows) if r.grad is not None]
    ghalo = [k for k, h in enumerate(halos) if h.tie is not None]
    gpar = [k for k, f in enumerate(param_grads) if f]
    pos = lambda i: nblk - 1 - i

    def body(*refs):
        ins = refs[:nr + nh + npar]
        d_refs = refs[nr + nh + npar:nr + nh + npar + nd]
        rest = refs[nr + nh + npar + nd:]
        grow_refs = rest[:len(grow)]
        gpar_refs = rest[len(grow):len(grow) + len(gpar)]
        carry_refs = rest[len(grow) + len(gpar):]
        i = pl.program_id(0)
        first = (pos(i) % bps) == 0
        vals = [r[...] for r in ins]
        diff_idx = grow + [nr + k for k in ghalo] + [nr + nh + k for k in gpar]

        def f(*dargs):
            full = list(vals)
            for k, a in zip(diff_idx, dargs):
                full[k] = a
            return fn(first, *full)

        res, vjp = jax.vjp(f, *[vals[k] for k in diff_idx])
        grads = vjp(tuple(d[...].astype(r.dtype) for d, r in zip(d_refs, res)))
        g_rows = list(grads[:len(grow)])
        g_halos = grads[len(grow):len(grow) + len(ghalo)]
        g_pars = grads[len(grow) + len(ghalo):]

        @pl.when(i == 0)
        def _():
            for c in carry_refs:
                c[...] = jnp.zeros_like(c)
            for p in gpar_refs:
                p[...] = jnp.zeros_like(p)

        for k, ref in enumerate(grow_refs):
            ref[...] = g_rows[k].astype(ref.dtype)
        for ci, hk in enumerate(ghalo):
            h = halos[hk]
            k = grow.index(h.tie)
            tail = g_rows[k][tb - h.hr:, :] + carry_refs[ci][...]
            grow_refs[k][tb - h.hr:, :] = tail.astype(grow_refs[k].dtype)
            carry_refs[ci][...] = g_halos[ci]
        for ref, gp in zip(gpar_refs, g_pars):
            ref[...] += gp

    out_specs = [pl.BlockSpec((tb, rows[k].width), lambda i: (pos(i), 0)) for k in grow]
    out_specs += [pl.BlockSpec(params[k].shape, lambda i: (0, 0)) for k in gpar]
    out_shape = [jax.ShapeDtypeStruct((n, rows[k].width), rows[k].grad) for k in grow]
    out_shape += [jax.ShapeDtypeStruct(params[k].shape, F32) for k in gpar]
    in_specs = _row_specs(tb, rows, halos, params, pos)
    in_specs += [pl.BlockSpec((tb, d.shape[1]), lambda i: (pos(i), 0)) for d in douts]
    res, carried = call_with_comm(
        body, name, (nblk,), in_specs, out_specs, out_shape,
        [pltpu.VMEM((halos[k].hr, halos[k].width), F32) for k in ghalo], ("arbitrary",),
        [r.arr for r in rows] + [h.arr for h in halos] + list(params) + list(douts), comm)
    return (res, carried) if comm is not None else res


_DIMS = {'nn': (((1,), (0,)), ((), ())), 'nt': (((1,), (1,)), ((), ())), 'tn': (((0,), (0,)), ((), ()))}


def matmul(name, a, b, mode, tm, tn, tk, out_dtype=F32, add=None, comm=None):
    if mode == 'tn':
        K, M = a.shape
    else:
        M, K = a.shape
    N = b.shape[0] if mode == 'nt' else b.shape[1]
    tm, tn, tk = min(tm, M), min(tn, N), min(tk, K)
    assert M % tm == 0 and N % tn == 0 and K % tk == 0, (name, M, N, K, tm, tn, tk)
    nk = K // tk
    a_spec = pl.BlockSpec((tk, tm), lambda i, j, k: (k, i)) if mode == 'tn' else pl.BlockSpec((tm, tk), lambda i, j, k: (i, k))
    b_spec = pl.BlockSpec((tn, tk), lambda i, j, k: (j, k)) if mode == 'nt' else pl.BlockSpec((tk, tn), lambda i, j, k: (k, j))
    o_spec = pl.BlockSpec((tm, tn), lambda i, j, k: (i, j))
    has_add = add is not None

    def body(*refs):
        a_ref, b_ref = refs[0], refs[1]
        add_ref = refs[2] if has_add else None
        o_ref = refs[-1]
        k = pl.program_id(2)
        part = lax.dot_general(a_ref[...].astype(BF16), b_ref[...].astype(BF16), _DIMS[mode], preferred_element_type=F32)

        @pl.when(k == 0)
        def _():
            o_ref[...] = ((part + add_ref[...]) if has_add else part).astype(o_ref.dtype)

        if nk > 1:
            @pl.when(k > 0)
            def _():
                o_ref[...] += part

    assert nk == 1 or out_dtype == F32
    ins = [a, b] + ([add] if has_add else [])
    in_specs = [a_spec, b_spec] + ([o_spec] if has_add else [])
    outs, couts = call_with_comm(body, name, (M // tm, N // tn, nk), in_specs, [o_spec],
                                 [jax.ShapeDtypeStruct((M, N), out_dtype)], [], ("parallel", "parallel", "arbitrary"),
                                 ins, comm)
    return (outs[0], couts) if comm is not None else outs[0]


def grad_w_blocks(name, h, dpb, tm, tk):
    n, d = h.shape
    nb, _, s = dpb.shape
    tm, tk = min(tm, d), min(tk, n)
    assert d % tm == 0 and n % tk == 0
    nk = n // tk

    def body(h_ref, b_ref, o_ref):
        k = pl.program_id(2)
        part = lax.dot_general(h_ref[...], b_ref[0], _DIMS['tn'], preferred_element_type=F32)

        @pl.when(k == 0)
        def _():
            o_ref[0] = part

        if nk > 1:
            @pl.when(k > 0)
            def _():
                o_ref[0] += part

    return pl.pallas_call(
        body, name=name, grid=(nb, d // tm, nk),
        in_specs=[pl.BlockSpec((tk, tm), lambda j, i, k: (k, i)), pl.BlockSpec((1, tk, s), lambda j, i, k: (j, k, 0))],
        out_specs=pl.BlockSpec((1, tm, s), lambda j, i, k: (j, i, 0)),
        out_shape=jax.ShapeDtypeStruct((nb, d, s), F32),
        compiler_params=_cparams(("parallel", "parallel", "arbitrary")),
    )(h, dpb)


def grad_h_blocks(name, dpb, wb, tm, tn, comm=None):
    nb, n, s = dpb.shape
    d = wb.shape[1]
    tm, tn = min(tm, n), min(tn, d)
    assert n % tm == 0 and d % tn == 0

    def body(a_ref, b_ref, o_ref):
        k = pl.program_id(2)
        part = lax.dot_general(a_ref[0], b_ref[0], _DIMS['nt'], preferred_element_type=F32)

        @pl.when(k == 0)
        def _():
            o_ref[...] = part

        @pl.when(k > 0)
        def _():
            o_ref[...] += part

    outs, carried = call_with_comm(
        body, name, (n // tm, d // tn, nb),
        [pl.BlockSpec((1, tm, s), lambda i, j, k: (k, i, 0)), pl.BlockSpec((1, tn, s), lambda i, j, k: (k, j, 0))],
        [pl.BlockSpec((tm, tn), lambda i, j, k: (i, j))], [jax.ShapeDtypeStruct((n, d), F32)], [],
        ("parallel", "parallel", "arbitrary"), [dpb, wb], comm)
    return (outs[0], carried) if comm is not None else outs[0]


def norm_in_proj(name, x, nw, wp, tm, tn, comm=None):
    n, d = x.shape
    wpc = wp.shape[1]
    tm, tn = min(tm, n), min(tn, wpc)
    assert n % tm == 0 and wpc % tn == 0

    def body(x_ref, nw_ref, w_ref, p_ref, h_ref):
        @pl.when(pl.program_id(1) == 0)
        def _():
            h_ref[...] = rms_block(x_ref[...], nw_ref[...]).astype(BF16)

        p_ref[...] = jnp.dot(h_ref[...], w_ref[...], preferred_element_type=F32)

    outs, couts = call_with_comm(
        body, name, (n // tm, wpc // tn),
        [pl.BlockSpec((tm, d), lambda i, j: (i, 0)), pl.BlockSpec((1, d), lambda i, j: (0, 0)),
         pl.BlockSpec((d, tn), lambda i, j: (0, j))],
        [pl.BlockSpec((tm, tn), lambda i, j: (i, j)), pl.BlockSpec((tm, d), lambda i, j: (i, 0))],
        [jax.ShapeDtypeStruct((n, wpc), F32), jax.ShapeDtypeStruct((n, d), BF16)], [], ("parallel", "arbitrary"),
        [x, nw, wp], comm)
    return (outs[0], outs[1], couts) if comm is not None else (outs[0], outs[1])


def norm_bwd(name, x, nw, dh, dres, tb, comm=None):
    n, d = x.shape
    tb = min(tb, n)

    def body(x_ref, nw_ref, dh_ref, dres_ref, dx_ref, dnw_ref):
        @pl.when(pl.program_id(0) == 0)
        def _():
            dnw_ref[...] = jnp.zeros_like(dnw_ref)

        _, vjp = jax.vjp(rms_block, x_ref[...], nw_ref[...])
        dx, dnw = vjp(dh_ref[...])
        dx_ref[...] = dx + dres_ref[...]
        dnw_ref[...] += dnw

    row = pl.BlockSpec((tb, d), lambda i: (i, 0))
    par = pl.BlockSpec((1, d), lambda i: (0, 0))
    outs, carried = call_with_comm(
        body, name, (n // tb,), [row, par, row, row], [row, par],
        [jax.ShapeDtypeStruct((n, d), F32), jax.ShapeDtypeStruct((1, d), F32)], [], ("arbitrary",),
        [x, nw, dh, dres], comm)
    return (outs[0], outs[1], carried) if comm is not None else (outs[0], outs[1])


def out_proj_loss(name, y, wo, x, target, tm):
    n, d = x.shape
    tm = min(tm, n)
    assert n % tm == 0

    def body(y_ref, w_ref, x_ref, t_ref, dz_ref, loss_ref):
        @pl.when(pl.program_id(0) == 0)
        def _():
            loss_ref[...] = jnp.zeros_like(loss_ref)

        z = x_ref[...] + jnp.dot(y_ref[...], w_ref[...], preferred_element_type=F32)
        err = z - t_ref[...]
        dz_ref[...] = err * (1.0 / d)
        part = 0.5 * jnp.sum(jnp.mean(err * err, axis=-1, keepdims=True), axis=0, keepdims=True)
        loss_ref[...] += jnp.broadcast_to(part, loss_ref.shape)

    row = pl.BlockSpec((tm, d), lambda i: (i, 0))
    dz, loss = pl.pallas_call(
        body, name=name, grid=(n // tm,),
        in_specs=[pl.BlockSpec((tm, y.shape[1]), lambda i: (i, 0)), pl.BlockSpec(wo.shape, lambda i: (0, 0)), row, row],
        out_specs=[row, pl.BlockSpec((8, LANE), lambda i: (0, 0))],
        out_shape=[jax.ShapeDtypeStruct((n, d), F32), jax.ShapeDtypeStruct((8, LANE), F32)],
        compiler_params=_cparams(("arbitrary",)),
    )(y, wo, x, target)
    return dz, loss[0, 0]


def rope_tables(name, pos_col, inv_freq_row):
    n = pos_col.shape[0]

    def body(p_ref, f_ref, c_ref, s_ref):
        ang = p_ref[...].astype(F32) * f_ref[...]
        lane = lax.broadcasted_iota(jnp.int32, ang.shape, 1)
        c_ref[...] = jnp.where(lane < ROT_DIM, jnp.cos(ang), 1.0)
        s_ref[...] = jnp.where(lane < ROT_DIM, jnp.sin(ang), 0.0)

    return pl.pallas_call(
        body, name=name, out_shape=[jax.ShapeDtypeStruct((n, A_HEAD), F32)] * 2,
    )(pos_col, inv_freq_row)


def _scan_operands(cfg, nseq, u_ref, w_ref, qg_ref, kd_ref, a_ref, gl_ref):
    pairs = [(b, h) for b in range(nseq) for h in range(cfg.CH)]
    st = lambda r, wd: jnp.stack([r[b, :, wd * h:wd * (h + 1)] for b, h in pairs], axis=0)
    gl = jnp.stack([gl_ref[b, 0:1, h:h + 1] for b, h in pairs], axis=0)
    return st(u_ref, C_HEAD), st(w_ref, C_HEAD), st(qg_ref, C_HEAD), st(kd_ref, C_HEAD), st(a_ref, CHUNK), gl


def gdn_scan_fwd(name, cfg, nseq, u, w, qg, kd, intra, glast):
    CH, CW, T = cfg.CH, cfg.CW, cfg.T
    nc = T // CHUNK

    def body(u_ref, w_ref, qg_ref, kd_ref, a_ref, gl_ref, o_ref, sin_ref, s_ref):
        @pl.when(pl.program_id(0) == 0)
        def _():
            s_ref[...] = jnp.zeros_like(s_ref)

        S = s_ref[...]
        for b in range(nseq):
            sin_ref[b, 0] = S[b * CH:(b + 1) * CH]
        o, S_next = gdn_state_step(S, *_scan_operands(cfg, nseq, u_ref, w_ref, qg_ref, kd_ref, a_ref, gl_ref))
        s_ref[...] = S_next
        for b in range(nseq):
            o_ref[b] = jnp.concatenate([o[b * CH + h] for h in range(CH)], axis=1)

    row = lambda wd: pl.BlockSpec((nseq, CHUNK, wd), lambda c: (0, c, 0))
    widths = [CW, CW, CW, CW, CH * CHUNK, LANE]
    o, s_in = pl.pallas_call(
        body, name=name, grid=(nc,),
        in_specs=[row(x) for x in widths],
        out_specs=[row(CW), pl.BlockSpec((nseq, 1, CH, C_HEAD, C_HEAD), lambda c: (0, c, 0, 0, 0))],
        out_shape=[jax.ShapeDtypeStruct((nseq, T, CW), F32),
                   jax.ShapeDtypeStruct((nseq, nc, CH, C_HEAD, C_HEAD), F32)],
        scratch_shapes=[pltpu.VMEM((nseq * CH, C_HEAD, C_HEAD), F32)],
        compiler_params=_cparams(("arbitrary",)),
    )(*[a.reshape(nseq, T, a.shape[1]) for a in (u, w, qg, kd, intra, glast)])
    return o.reshape(nseq * T, CW), s_in


def gdn_scan_bwd(name, cfg, nseq, u, w, qg, kd, intra, glast, s_in, do, comm=None):
    CH, CW, T = cfg.CH, cfg.CW, cfg.T
    nc = T // CHUNK

    def body(u_ref, w_ref, qg_ref, kd_ref, a_ref, gl_ref, sin_ref, do_ref,
             du_ref, dw_ref, dqg_ref, dkd_ref, da_ref, dgl_ref, ds_ref):
        @pl.when(pl.program_id(0) == 0)
        def _():
            ds_ref[...] = jnp.zeros_like(ds_ref)

        S = jnp.concatenate([sin_ref[b, 0] for b in range(nseq)], axis=0)
        dout = jnp.stack([do_ref[b, :, C_HEAD * h:C_HEAD * (h + 1)] for b in range(nseq) for h in range(CH)], axis=0)
        _, vjp = jax.vjp(gdn_state_step, S, *_scan_operands(cfg, nseq, u_ref, w_ref, qg_ref, kd_ref, a_ref, gl_ref))
        dS, du, dw, dqg, dkd, da, dg = vjp((dout, ds_ref[...]))
        ds_ref[...] = dS
        lane = lax.broadcasted_iota(jnp.int32, (CHUNK, LANE), 1)
        rowi = lax.broadcasted_iota(jnp.int32, (CHUNK, LANE), 0)
        for b in range(nseq):
            cat = lambda x: jnp.concatenate([x[b * CH + h] for h in range(CH)], axis=1)
            du_ref[b] = cat(du)
            dw_ref[b] = cat(dw)
            dqg_ref[b] = cat(dqg)
            dkd_ref[b] = cat(dkd)
            da_ref[b] = cat(da)
            dgl = jnp.zeros((CHUNK, LANE), F32)
            for h in range(CH):
                dgl = dgl + jnp.where((lane == h) & (rowi == 0), dg[b * CH + h], 0.0)
            dgl_ref[b] = dgl

    row = lambda wd: pl.BlockSpec((nseq, CHUNK, wd), lambda c: (0, nc - 1 - c, 0))
    widths = [CW, CW, CW, CW, CH * CHUNK, LANE]
    outs, carried = call_with_comm(
        body, name, (nc,),
        [row(x) for x in widths]
        + [pl.BlockSpec((nseq, 1, CH, C_HEAD, C_HEAD), lambda c: (0, nc - 1 - c, 0, 0, 0)), row(CW)],
        [row(x) for x in widths], [jax.ShapeDtypeStruct((nseq, T, x), F32) for x in widths],
        [pltpu.VMEM((nseq * CH, C_HEAD, C_HEAD), F32)], ("arbitrary",),
        [a.reshape(nseq, T, a.shape[1]) for a in (u, w, qg, kd, intra, glast)] + [s_in, do.reshape(nseq, T, CW)], comm)
    return [o.reshape(nseq * T, o.shape[2]) for o in outs], carried


def _tile(total, cap, unit=LANE):
    best = None
    for t in range(unit, min(cap, total) + 1, unit):
        if total % t == 0:
            best = t
    assert best is not None, (total, cap, unit)
    return best


def _pad_lanes(v, width=LANE):
    return jnp.pad(v.reshape(1, -1), ((0, 0), (0, width - v.shape[-1])))


def permute_w_in(cfg, w):
    parts = []
    for n in cfg.order:
        off, wd = cfg.orig[n]
        blk = w[:, off:off + wd]
        if cfg.g[n][1] != wd:
            blk = jnp.pad(blk, ((0, 0), (0, cfg.g[n][1] - wd)))
        parts.append(blk)
    return jnp.concatenate(parts, axis=1)


def chip_blocks(cfg, groups, n_chips):
    s = cfg.IN_COLS // n_chips
    blocks = []
    for j in range(n_chips):
        lo, hi = j * s, (j + 1) * s
        pieces = []
        for name, (off, wd) in cfg.orig.items():
            a, b = max(lo, off), min(hi, off + wd)
            if a < b:
                pieces.append(groups[name][:, a - off:b - off])
        blocks.append(jnp.concatenate(pieces, axis=1))
    return jnp.stack(blocks)


def _layer_params(cfg, prm):
    return dict(
        nw=prm['norm_w'].reshape(1, -1),
        qnw=prm['q_norm_w'].reshape(1, -1), knw=prm['k_norm_w'].reshape(1, -1),
        sinks_row=jnp.repeat(prm['sinks'], A_HEAD).reshape(1, -1),
        cw=prm['b_conv_w'], cb=prm['b_conv_b'].reshape(1, -1),
        lw=prm['b_ln_w'].reshape(1, -1), lb=prm['b_ln_b'].reshape(1, -1),
        pw=prm['b_pw_w'], pb=prm['b_pw_b'].reshape(1, -1),
        ccw=prm['c_conv_w'], alog=_pad_lanes(prm['c_a_log']), dtb=_pad_lanes(prm['c_dt_bias']),
        onw=prm['c_onorm_w'].reshape(1, -1),
    )


def _attn_io(cfg, p, cos, sin, grads):
    gq = BF16 if grads else None
    rows = [Row(p, cfg.AW, cfg.blk('qa'), gq), Row(p, cfg.AW, cfg.blk('za'), gq),
            Row(p, cfg.AKW, cfg.blk('ka'), gq), Row(p, cfg.AKW, cfg.blk('va'), gq),
            Row(cos, A_HEAD), Row(sin, A_HEAD)]
    halos = [Halo(p, cfg.AKW, cfg.blk('ka'), ATTN_BLOCK, 2 if grads else None),
             Halo(p, cfg.AKW, cfg.blk('va'), ATTN_BLOCK, 3 if grads else None),
             Halo(cos, A_HEAD, 0, ATTN_BLOCK), Halo(sin, A_HEAD, 0, ATTN_BLOCK)]
    return rows, halos


def _conv_io(cfg, p, grads):
    gq = BF16 if grads else None
    rows = [Row(p, 2 * cfg.BW, cfg.blk('ub'), gq), Row(p, cfg.BW, cfg.blk('zb'), gq)]
    halos = [Halo(p, 2 * cfg.BW, cfg.blk('ub'), B_HALO, 0 if grads else None)]
    return rows, halos


def _prep_io(cfg, p, grads):
    gq = BF16 if grads else None
    rows = [Row(p, cfg.CW, cfg.blk(n), gq) for n in ('qc', 'kc', 'vc')]
    rows += [Row(p, LANE, cfg.blk('bc'), gq), Row(p, LANE, cfg.blk('ac'), gq)]
    halos = [Halo(p, cfg.CW, cfg.blk(n), C_HALO, k if grads else None) for k, n in enumerate(('qc', 'kc', 'vc'))]
    return rows, halos


TB_CONV = 128
TB_PREP = 256
TB_OUT = 512
TB_INTRA_FWD = 256
TB_INTRA_BWD = 256


def layer_forward(cfg, l, x, lp, wp, wo, cos, sin, comms=None, target=None):
    n = x.shape[0]
    nseq = n // cfg.T
    T = cfg.T
    comms = comms or {}
    carried = {}

    def hosted(key, res):
        if comms.get(key) is None:
            return res
        res, carried[key] = res
        return res

    nw = lp['nw'] if isinstance(lp, dict) else lp[0]
    p, h, *rode = norm_in_proj(f"in_proj_{l}", x, nw, wp, 1024, _tile(cfg.WP, 768), comm=comms.get('in_proj'))
    if rode:
        carried['in_proj'] = rode[0]
    if not isinstance(lp, dict):
        lp, wo = lp[1](carried['in_proj'])
    rows, halos = _attn_io(cfg, p, cos, sin, False)
    (oa,) = hosted('attn', rb_fwd(f"attn_fwd_{l}", functools.partial(attn_block, cfg), n, ATTN_BLOCK, T // ATTN_BLOCK,
                                  rows, halos, [lp['qnw'], lp['knw'], lp['sinks_row']], [(cfg.AW, BF16)],
                                  comm=comms.get('attn')))
    rows, halos = _conv_io(cfg, p, False)
    tbb = min(TB_CONV, T)
    (ob,) = hosted('conv', rb_fwd(f"conv_fwd_{l}", functools.partial(conv_block, cfg), n, tbb, T // tbb, rows, halos,
                                  [lp['cw'], lp['cb'], lp['lw'], lp['lb'], lp['pw'], lp['pb']], [(cfg.BW, BF16)],
                                  comm=comms.get('conv')))
    rows, halos = _prep_io(cfg, p, False)
    tbp = min(TB_PREP, T)
    qn, kn, v, g, beta = rb_fwd(f"gdn_prep_fwd_{l}", functools.partial(gdn_prep_block, cfg), n, tbp, T // tbp, rows,
                                halos, [lp['ccw'], lp['alog'], lp['dtb']],
                                [(cfg.CW, F32)] * 3 + [(LANE, F32)] * 2)
    intra_outs = hosted('intra', rb_fwd(
        f"gdn_intra_fwd_{l}", functools.partial(gdn_intra_rows, cfg), n, min(TB_INTRA_FWD, T), T // min(TB_INTRA_FWD, T),
        [Row(qn, cfg.CW), Row(kn, cfg.CW), Row(v, cfg.CW), Row(g, LANE), Row(beta, LANE)], [], [],
        [(cfg.CW, F32)] * 4 + [(cfg.CH * CHUNK, F32), (LANE, F32), (cfg.CH * CHUNK, F32)], comm=comms.get('intra')))
    intra_outs, inv = intra_outs[:6], intra_outs[6]
    o, s_in = gdn_scan_fwd(f"gdn_scan_fwd_{l}", cfg, nseq, *intra_outs)
    tbo = min(TB_OUT, T)
    (oc,) = rb_fwd(f"gdn_out_fwd_{l}", functools.partial(gdn_out_block, cfg), n, tbo, T // tbo,
                   [Row(o, cfg.CW), Row(p, cfg.CW, cfg.blk('zc'))], [], [lp['onw']], [(cfg.CW, BF16)])
    y = jnp.concatenate([oa, ob, oc], axis=1)
    if target is None:
        x_next = matmul(f"out_proj_{l}", y, wo, 'nn', 1024, 1024, cfg.D, add=x)
    else:
        x_next = out_proj_loss(f"out_proj_{l}", y, wo, x, target, 512)
    saved = dict(x=x, p=p, h=h, y=y, qn=qn, kn=kn, v=v, g=g, beta=beta, intra_outs=intra_outs, inv=inv, s_in=s_in, o=o,
                 lp=lp, wo=wo)
    return x_next, saved, carried


def layer_backward(cfg, l, dxn, sv, lp, wb, wo, cos, sin, rs=None, own_rs=None):
    n = dxn.shape[0]
    nseq = n // cfg.T
    T = cfg.T
    p = sv['p']
    AW, BW, CW = cfg.AW, cfg.BW, cfg.CW
    dy = matmul(f"dy_{l}", dxn, wo, 'nt', 1024, 1024, cfg.D)
    dwo = matmul(f"dwo_{l}", sv['y'], dxn, 'tn', 1024, 1024, 2048)
    doa, dob, doc = dy[:, :AW], dy[:, AW:AW + BW], dy[:, AW + BW:]
    tbo = min(TB_OUT, T)
    do, dzc, donw = rb_bwd(f"gdn_out_bwd_{l}", functools.partial(gdn_out_block, cfg), n, tbo, T // tbo,
                           [Row(sv['o'], CW, 0, F32), Row(p, CW, cfg.blk('zc'), BF16)], [], [lp['onw']], [doc], [True])
    dintra, got_rest = gdn_scan_bwd(f"gdn_scan_bwd_{l}", cfg, nseq, *sv['intra_outs'], sv['s_in'], do,
                                    comm=None if rs is None else rs.scatter([1, 2]))
    dqn, dkn, dv, dg, dbeta = rb_bwd(
        f"gdn_intra_bwd_{l}", functools.partial(gdn_intra_rows, cfg), n, min(TB_INTRA_BWD, T), T // min(TB_INTRA_BWD, T),
        [Row(sv['qn'], CW, 0, F32), Row(sv['kn'], CW, 0, F32), Row(sv['v'], CW, 0, F32), Row(sv['g'], LANE, 0, F32),
         Row(sv['beta'], LANE, 0, F32), Row(sv['inv'], cfg.CH * CHUNK)], [], [], list(dintra), [])
    rows, halos = _prep_io(cfg, p, True)
    tbp = min(TB_PREP, T)
    dqc, dkc, dvc, dbc, dac, dccw, dalog, ddtb = rb_bwd(
        f"gdn_prep_bwd_{l}", functools.partial(gdn_prep_block, cfg), n, tbp, T // tbp, rows, halos,
        [lp['ccw'], lp['alog'], lp['dtb']], [dqn, dkn, dv, dg, dbeta], [True] * 3)
    rows, halos = _conv_io(cfg, p, True)
    tbb = min(TB_CONV, T)
    conv_grads = rb_bwd(
        f"conv_bwd_{l}", functools.partial(conv_block, cfg), n, tbb, T // tbb, rows, halos,
        [lp['cw'], lp['cb'], lp['lw'], lp['lb'], lp['pw'], lp['pb']], [dob], [True] * 6,
        comm=None if rs is None else rs.scatter([0]))
    got = None
    if rs is not None:
        conv_grads, got_w_in = conv_grads
        got = got_w_in + got_rest
    dub, dzb, dcw, dcb, dlw, dlb, dpw, dpb = conv_grads
    rows, halos = _attn_io(cfg, p, cos, sin, True)
    dqa, dza, dka, dva, dqnw, dknw, dsinks_row = rb_bwd(
        f"attn_bwd_{l}", functools.partial(attn_block, cfg), n, ATTN_BLOCK, T // ATTN_BLOCK, rows, halos,
        [lp['qnw'], lp['knw'], lp['sinks_row']], [doa], [True] * 3)
    dgroups = dict(qa=dqa, za=dza, qc=dqc, kc=dkc, vc=dvc, zc=dzc, ka=dka, va=dva, ub=dub, zb=dzb, bc=dbc, ac=dac)
    dp_blocks = chip_blocks(cfg, dgroups, N_CHIPS)
    dw_in = grad_w_blocks(f"dwp_{l}", sv['h'], dp_blocks, 1024, 2048)
    mine = None if own_rs is None else own_rs(dict(w_in_blocks=dw_in, w_out=dwo, b_pw_w=dpw))
    got_mine = None
    if mine is None:
        dh = grad_h_blocks(f"dh_{l}", dp_blocks, wb, 1024, 1024)
        dx, dnw = norm_bwd(f"norm_bwd_{l}", sv['x'], lp['nw'], dh, dxn, 256)
    elif l > 0:
        dh, mine_received = grad_h_blocks(f"dh_{l}", dp_blocks, wb, 1024, 1024, comm=mine.swap())
        mine.add(mine_received)
        dx, dnw = norm_bwd(f"norm_bwd_{l}", sv['x'], lp['nw'], dh, dxn, 256)
    else:
        mine.add(run_comm(f"rs{l}_swap_halves", mine.swap()))
        dh, got_w_in = grad_h_blocks(f"dh_{l}", dp_blocks, wb, 1024, 1024, comm=mine.scatter([0]))
        dx, dnw, got_others = norm_bwd(f"norm_bwd_{l}", sv['x'], lp['nw'], dh, dxn, 256, comm=mine.scatter([1, 2]))
        got_mine = got_w_in + got_others
    grads = dict(
        norm_w=dnw[0], w_in_blocks=dw_in, q_norm_w=dqnw[0], k_norm_w=dknw[0],
        sinks=dsinks_row.reshape(cfg.AQH, A_HEAD)[:, 0],
        b_conv_w=dcw, b_conv_b=dcb[0], b_ln_w=dlw[0], b_ln_b=dlb[0], b_pw_w=dpw, b_pw_b=dpb[0],
        c_conv_w=dccw, c_a_log=dalog[0, :cfg.CH], c_dt_bias=ddtb[0, :cfg.CH], c_onorm_w=donw[0], w_out=dwo)
    return dx, grads, (got, mine, got_mine)


def rope_for(cfg, positions):
    n = positions.size
    inv_freq = ROPE_THETA ** (-np.arange(0, ROT_DIM, 2, dtype=np.float32) / ROT_DIM)
    freq_row = np.zeros((1, A_HEAD), np.float32)
    freq_row[0, :ROT_DIM] = np.concatenate([inv_freq, inv_freq])
    return rope_tables("rope_tables", positions.reshape(n, 1), jnp.asarray(freq_row))


def local_step(cfg, x, positions, prm, wps, wos, target):
    nseq = x.shape[0]
    n = nseq * cfg.T
    cos, sin = rope_for(cfg, positions)
    lps = [_layer_params(cfg, {k: v[l] for k, v in prm.items()}) for l in range(DEPTH)]
    saved = []
    xl = x.reshape(n, cfg.D)
    for l in range(DEPTH):
        xl, sv, _ = layer_forward(cfg, l, xl, lps[l], wps[l], wos[l], cos, sin,
                                  target=target.reshape(n, cfg.D) if l == DEPTH - 1 else None)
        saved.append(sv)
    dx, loss = xl
    grads = [None] * DEPTH
    for l in reversed(range(DEPTH)):
        groups = {k: wps[l][:, off:off + wd] for k, (off, wd) in cfg.g.items()}
        wb = chip_blocks(cfg, groups, N_CHIPS)
        dx, grads[l], _ = layer_backward(cfg, l, dx, saved[l], lps[l], wb, wos[l], cos, sin)
    return loss, dx.reshape(x.shape), grads


N_CHIPS = 4
N_DEV = 8


def _place():
    return lax.axis_index("x"), lax.axis_index("y"), lax.axis_index("c")


def _other_chips(x, y):
    return [(1 - x, y), (x, 1 - y), (1 - x, 1 - y)]


def _remote(src, dst, send, recv, to):
    return pltpu.make_async_remote_copy(src_ref=src, dst_ref=dst, send_sem=send, recv_sem=recv, device_id=to,
                                        device_id_type=MESH)


def gather_comm(arrs):
    n = len(arrs)

    def half(c):
        return [pl.ds(c * (a.shape[0] // 2), a.shape[0] // 2) for a in arrs]

    def first_copies(ins, outs, send, recv):
        x, y, c = _place()
        me = 2 * x + y
        mine = half(c)
        return [_remote(ins[i].at[mine[i]], outs[i].at[me, mine[i]], send.at[i, j], recv.at[i, j], (cx, cy, c))
                for i in range(n) for j, (cx, cy) in enumerate(_other_chips(x, y))]

    def start(ins, outs, sems):
        for cp in first_copies(ins, outs, *sems):
            cp.start()

    def finish(ins, outs, sems):
        send, recv = sems
        x, y, c = _place()
        chips = _other_chips(x, y)
        sib = (x, y, 1 - c)
        passed = []
        mine, other = half(c), half(1 - c)
        for i in range(n):
            for j, (cx, cy) in enumerate(chips):
                blk = outs[i].at[2 * cx + cy, mine[i]]
                _remote(blk, blk, send.at[i, j], recv.at[i, j], (x, y, c)).wait_recv()
                cp = _remote(blk, blk, send.at[i, 3 + j], recv.at[i, 3 + j], sib)
                cp.start()
                passed.append(cp)
        for i in range(n):
            for j, (cx, cy) in enumerate(chips):
                blk = outs[i].at[2 * cx + cy, other[i]]
                _remote(blk, blk, send.at[i, 3 + j], recv.at[i, 3 + j], sib).wait_recv()
        for cp in first_copies(ins, outs, send, recv) + passed:
            cp.wait_send()

    return Comm(arrs, [jax.ShapeDtypeStruct((N_CHIPS,) + a.shape, a.dtype) for a in arrs],
                [pltpu.SemaphoreType.DMA((n, 6)), pltpu.SemaphoreType.DMA((n, 6))], start, finish)


def fill_own(gathered, arrs):
    me = 2 * lax.axis_index("x") + lax.axis_index("y")
    return [lax.dynamic_update_index_in_dim(o, a, me, 0) for o, a in zip(gathered, arrs)]


def swap_comm(arrs):
    n = len(arrs)

    def copies(ins, outs, send, recv):
        x, y, c = _place()
        return [_remote(ins[i].at[:, 1 - c], outs[i], send.at[i], recv.at[i], (x, y, 1 - c)) for i in range(n)]

    def start(ins, outs, sems):
        for cp in copies(ins, outs, *sems):
            cp.start()

    def finish(ins, outs, sems):
        for cp in copies(ins, outs, *sems):
            cp.wait()

    return Comm(arrs, [jax.ShapeDtypeStruct((a.shape[0],) + a.shape[2:], a.dtype) for a in arrs],
                [pltpu.SemaphoreType.DMA((n,)), pltpu.SemaphoreType.DMA((n,))], start, finish)


def scatter_comm(arrs):
    n = len(arrs)

    def copies(ins, outs, send, recv):
        x, y, c = _place()
        return [_remote(ins[i].at[2 * cx + cy], outs[i].at[j], send.at[i, j], recv.at[i, j], (cx, cy, c))
                for i in range(n) for j, (cx, cy) in enumerate(_other_chips(x, y))]

    def start(ins, outs, sems):
        for cp in copies(ins, outs, *sems):
            cp.start()

    def finish(ins, outs, sems):
        send, recv = sems
        x, y, c = _place()
        for i in range(n):
            for j in range(3):
                blk = outs[i].at[j]
                _remote(blk, blk, send.at[i, j], recv.at[i, j], (x, y, c)).wait_recv()
        for cp in copies(ins, outs, send, recv):
            cp.wait_send()

    return Comm(arrs, [jax.ShapeDtypeStruct((3,) + a.shape[1:], a.dtype) for a in arrs],
                [pltpu.SemaphoreType.DMA((n, 3)), pltpu.SemaphoreType.DMA((n, 3))], start, finish)


def share_comm(arrs):
    n = len(arrs)

    def copies(outs, send, recv):
        x, y, c = _place()
        return [_remote(outs[i].at[c], outs[i].at[c], send.at[i], recv.at[i], (x, y, 1 - c)) for i in range(n)]

    def start(ins, outs, sems):
        for cp in copies(outs, *sems):
            cp.start()

    def finish(ins, outs, sems):
        send, recv = sems
        x, y, c = _place()
        for i in range(n):
            blk = outs[i].at[1 - c]
            _remote(blk, blk, send.at[i], recv.at[i], (x, y, c)).wait_recv()
        for cp in copies(outs, send, recv):
            cp.wait_send()

    return Comm(arrs, [jax.ShapeDtypeStruct(a.shape, a.dtype) for a in arrs],
                [pltpu.SemaphoreType.DMA((n,)), pltpu.SemaphoreType.DMA((n,))], start, finish,
                aliases={i: i for i in range(n)})


def all_reduce_small(name, packed):
    r = packed.shape[0]

    def body(in_ref, out_ref, buf, send, recv):
        x, y, c = _place()
        me = 4 * x + 2 * y + c
        buf[me] = in_ref[...]
        flips = [(fx, fy, fc) for fx in (0, 1) for fy in (0, 1) for fc in (0, 1) if (fx, fy, fc) != (0, 0, 0)]
        peers = [((x + fx) % 2, (y + fy) % 2, (c + fc) % 2) for fx, fy, fc in flips]
        cps = [_remote(in_ref, buf.at[me], send.at[k], recv.at[k], peer) for k, peer in enumerate(peers)]
        for cp in cps:
            cp.start()
        for k, (px, py, pc) in enumerate(peers):
            blk = buf.at[4 * px + 2 * py + pc]
            _remote(blk, blk, send.at[k], recv.at[k], (x, y, c)).wait_recv()
        for cp in cps:
            cp.wait_send()
        acc = buf[0]
        for d in range(1, N_DEV):
            acc = acc + buf[d]
        out_ref[...] = acc

    vm = pl.BlockSpec(memory_space=pltpu.VMEM)
    return pl.pallas_call(
        body, name=name, in_specs=[vm], out_specs=vm, out_shape=jax.ShapeDtypeStruct(packed.shape, F32),
        scratch_shapes=[pltpu.VMEM((N_DEV, r, LANE), F32), pltpu.SemaphoreType.DMA((N_DEV - 1,)),
                        pltpu.SemaphoreType.DMA((N_DEV - 1,))],
    )(packed)


def add_own_half(name, g, a, c_idx, tr):
    nch, _, r, cc = g.shape
    tr = min(tr, r)

    def body(c_ref, g_ref, a_ref, o_ref):
        o_ref[...] = (g_ref[0] + a_ref[...]).astype(o_ref.dtype)

    return pl.pallas_call(
        body, name=name,
        grid_spec=pltpu.PrefetchScalarGridSpec(
            num_scalar_prefetch=1, grid=(nch, r // tr),
            in_specs=[pl.BlockSpec((1, 1, tr, cc), lambda j, i, c_ref: (j, c_ref[0], i, 0)),
                      pl.BlockSpec((1, tr, cc), lambda j, i, c_ref: (j, i, 0))],
            out_specs=pl.BlockSpec((1, tr, cc), lambda j, i, c_ref: (j, i, 0))),
        out_shape=jax.ShapeDtypeStruct(a.shape, BF16),
        compiler_params=_cparams(("parallel", "parallel")),
    )(c_idx, g, a)


def sum_chips(name, p, b, idx, tr):
    _, r, cc = p.shape
    tr = min(tr, r)

    def body(idx_ref, p_ref, b_ref, o_ref):
        acc = p_ref[0].astype(F32)
        for k in range(3):
            acc = acc + b_ref[k].astype(F32)
        o_ref[0] = acc

    return pl.pallas_call(
        body, name=name,
        grid_spec=pltpu.PrefetchScalarGridSpec(
            num_scalar_prefetch=1, grid=(r // tr,),
            in_specs=[pl.BlockSpec((1, tr, cc), lambda i, s: (s[0], i, 0)),
                      pl.BlockSpec((3, tr, cc), lambda i, s: (0, i, 0))],
            out_specs=pl.BlockSpec((1, tr, cc), lambda i, s: (s[1], i, 0))),
        out_shape=jax.ShapeDtypeStruct((2, r, cc), F32),
        compiler_params=_cparams(("parallel",)),
    )(idx, p, b)


class GradReduce:
    def __init__(self, tag, parts, chip, c_idx):
        self.tag, self.c_idx = tag, c_idx
        self.parts = [p.reshape(p.shape[0], 2, p.shape[1] // 2, p.shape[2]) for p in parts]
        self.idx = jnp.concatenate([chip.astype(jnp.int32).reshape(1), c_idx])

    def swap(self):
        return swap_comm(self.parts)

    def add(self, received):
        self.part = [add_own_half(f"rs{self.tag}_add_sibling_{t}", g, a, self.c_idx, 128)
                     for t, (g, a) in enumerate(zip(self.parts, received))]

    def scatter(self, which=None):
        return scatter_comm(self.part if which is None else [self.part[t] for t in which])

    def finish(self, got):
        red = [sum_chips(f"rs{self.tag}_sum_chips_{t}", p, b, self.idx, 128) for t, (p, b) in enumerate(zip(self.part, got))]
        out = run_comm(f"rs{self.tag}_share_halves", share_comm(red))
        return [o.reshape(-1, o.shape[-1]) for o in out]


def adamw_many(name, ws, gs, ms, vs):
    n = len(ws)

    def body(*refs):
        for i in range(n):
            w_ref, g_ref, m_ref, v_ref = (refs[k * n + i] for k in range(4))
            d_ref, mo_ref, vo_ref = (refs[(4 + k) * n + i] for k in range(3))
            g = g_ref[...]
            m = ADAM_B1 * m_ref[...] + (1.0 - ADAM_B1) * g
            v = ADAM_B2 * v_ref[...] + (1.0 - ADAM_B2) * jnp.square(g)
            m_hat = m / (1.0 - ADAM_B1 ** ADAM_STEP)
            v_hat = v / (1.0 - ADAM_B2 ** ADAM_STEP)
            d_ref[...] = -ADAM_LR * (m_hat / (jnp.sqrt(v_hat) + ADAM_EPS) + ADAM_WD * w_ref[...])
            mo_ref[...] = m
            vo_ref[...] = v

    vm = pl.BlockSpec(memory_space=pltpu.VMEM)
    return pl.pallas_call(
        body, name=name, in_specs=[vm] * (4 * n), out_specs=[vm] * (3 * n),
        out_shape=[jax.ShapeDtypeStruct(a.shape, F32) for a in ws] * 3,
    )(*ws, *gs, *ms, *vs)


def adamw_layers(name, w, g0, g1, m, v, tb):
    _, r, cc = w.shape
    tb = min(tb, r)
    nb = r // tb

    def body(w_ref, g0_ref, g1_ref, m_ref, v_ref, g_ref, d_ref, mo_ref, vo_ref):
        g = jnp.where(pl.program_id(0) == 0, g0_ref[...], g1_ref[...])
        m = ADAM_B1 * m_ref[0] + (1.0 - ADAM_B1) * g
        v = ADAM_B2 * v_ref[0] + (1.0 - ADAM_B2) * jnp.square(g)
        m_hat = m / (1.0 - ADAM_B1 ** ADAM_STEP)
        v_hat = v / (1.0 - ADAM_B2 ** ADAM_STEP)
        g_ref[0] = g
        d_ref[0] = -ADAM_LR * (m_hat / (jnp.sqrt(v_hat) + ADAM_EPS) + ADAM_WD * w_ref[0])
        mo_ref[0] = m
        vo_ref[0] = v

    spec = pl.BlockSpec((1, tb, cc), lambda l, i: (l, i, 0))
    g0_spec = pl.BlockSpec((tb, cc), lambda l, i: (jnp.where(l == 0, i, nb - 1), 0))
    g1_spec = pl.BlockSpec((tb, cc), lambda l, i: (jnp.where(l == 1, i, 0), 0))
    return pl.pallas_call(
        body, name=name, grid=(2, nb), in_specs=[spec, g0_spec, g1_spec, spec, spec], out_specs=[spec] * 4,
        out_shape=[jax.ShapeDtypeStruct(w.shape, F32)] * 4,
        compiler_params=_cparams(("arbitrary", "arbitrary")),
    )(w, g0, g1, m, v)


def adamw_cols_major(name, w, g0, g1, m, v, tb=LANE):
    wt, mt, vt = (jnp.transpose(a, (2, 0, 1)) for a in (w, m, v))
    cc, _, r = wt.shape

    def body(w_ref, g0_ref, g1_ref, m_ref, v_ref, g_ref, d_ref, mo_ref, vo_ref):
        for l, gl_ref in enumerate((g0_ref, g1_ref)):
            g = gl_ref[...].T
            m = ADAM_B1 * m_ref[:, l, :] + (1.0 - ADAM_B1) * g
            v = ADAM_B2 * v_ref[:, l, :] + (1.0 - ADAM_B2) * jnp.square(g)
            m_hat = m / (1.0 - ADAM_B1 ** ADAM_STEP)
            v_hat = v / (1.0 - ADAM_B2 ** ADAM_STEP)
            g_ref[:, l, :] = g
            d_ref[:, l, :] = -ADAM_LR * (m_hat / (jnp.sqrt(v_hat) + ADAM_EPS) + ADAM_WD * w_ref[:, l, :])
            mo_ref[:, l, :] = m
            vo_ref[:, l, :] = v

    spec = pl.BlockSpec((tb, 2, r), lambda i: (i, 0, 0))
    gspec = pl.BlockSpec((r, tb), lambda i: (0, i))
    outs = pl.pallas_call(
        body, name=name, grid=(pl.cdiv(cc, tb),), in_specs=[spec, gspec, gspec, spec, spec], out_specs=[spec] * 4,
        out_shape=[jax.ShapeDtypeStruct(wt.shape, F32)] * 4,
        compiler_params=_cparams(("parallel",)),
    )(wt, g0, g1, mt, vt)
    return [jnp.transpose(o, (1, 2, 0)) for o in outs]


def _pack(arrs):
    flat = jnp.concatenate([a.reshape(-1).astype(F32) for a in arrs])
    pad = (-flat.shape[0]) % (8 * LANE)
    return jnp.pad(flat, (0, pad)).reshape(-1, LANE)


def _unpack(packed, shapes):
    flat = packed.reshape(-1)
    out, off = [], 0
    for s in shapes:
        size = math.prod(s)
        out.append(flat[off:off + size].reshape(s))
        off += size
    return out


BIG = ('w_in', 'w_out', 'b_pw_w')
SMALL = tuple(k for k in WEIGHTS if k not in BIG)
CHIP_SHARDED_SMALL = {'b_conv_w': 2, 'c_conv_w': 2}


def kernel(x, positions, norm_w, w_in, q_norm_w, k_norm_w, sinks, b_conv_w, b_conv_b, b_ln_w, b_ln_b, b_pw_w, b_pw_b, c_conv_w, c_a_log, c_dt_bias, c_onorm_w, w_out, loss_target, m_norm_w, m_w_in, m_q_norm_w, m_k_norm_w, m_sinks, m_b_conv_w, m_b_conv_b, m_b_ln_w, m_b_ln_b, m_b_pw_w, m_b_pw_b, m_c_conv_w, m_c_a_log, m_c_dt_bias, m_c_onorm_w, m_w_out, v_norm_w, v_w_in, v_q_norm_w, v_k_norm_w, v_sinks, v_b_conv_w, v_b_conv_b, v_b_ln_w, v_b_ln_b, v_b_pw_w, v_b_pw_b, v_c_conv_w, v_c_a_log, v_c_dt_bias, v_c_onorm_w, v_w_out):
    cfg = Cfg(x.shape[-1], x.shape[-2])
    w = dict(norm_w=norm_w, w_in=w_in, q_norm_w=q_norm_w, k_norm_w=k_norm_w, sinks=sinks, b_conv_w=b_conv_w,
             b_conv_b=b_conv_b, b_ln_w=b_ln_w, b_ln_b=b_ln_b, b_pw_w=b_pw_w, b_pw_b=b_pw_b, c_conv_w=c_conv_w,
             c_a_log=c_a_log, c_dt_bias=c_dt_bias, c_onorm_w=c_onorm_w, w_out=w_out)
    m = dict(norm_w=m_norm_w, w_in=m_w_in, q_norm_w=m_q_norm_w, k_norm_w=m_k_norm_w, sinks=m_sinks,
             b_conv_w=m_b_conv_w, b_conv_b=m_b_conv_b, b_ln_w=m_b_ln_w, b_ln_b=m_b_ln_b, b_pw_w=m_b_pw_w,
             b_pw_b=m_b_pw_b, c_conv_w=m_c_conv_w, c_a_log=m_c_a_log, c_dt_bias=m_c_dt_bias, c_onorm_w=m_c_onorm_w,
             w_out=m_w_out)
    v = dict(norm_w=v_norm_w, w_in=v_w_in, q_norm_w=v_q_norm_w, k_norm_w=v_k_norm_w, sinks=v_sinks,
             b_conv_w=v_b_conv_w, b_conv_b=v_b_conv_b, b_ln_w=v_b_ln_w, b_ln_b=v_b_ln_b, b_pw_w=v_b_pw_w,
             b_pw_b=v_b_pw_b, c_conv_w=v_c_conv_w, c_a_log=v_c_a_log, c_dt_bias=v_c_dt_bias, c_onorm_w=v_c_onorm_w,
             w_out=v_w_out)
    chip = 2 * lax.axis_index("x") + lax.axis_index("y")
    c_idx = lax.axis_index("c").astype(jnp.int32).reshape(1)
    D, T = cfg.D, cfg.T
    nseq = x.shape[0]
    n = nseq * T
    w_in_b, w_out_b = w_in.astype(BF16), w_out.astype(BF16)

    def permuted(g_in):
        return permute_w_in(cfg, jnp.concatenate(list(g_in), axis=1))

    def layer_prm(l, g_pw, g_bcw, g_ccw):
        prm = {k: w[k][l] for k in SMALL}
        prm['b_pw_w'] = g_pw.reshape(cfg.BW, cfg.BW)
        prm['b_conv_w'] = jnp.concatenate(list(g_bcw[:, l]), axis=1)
        prm['c_conv_w'] = jnp.concatenate(list(g_ccw[:, l]), axis=1)
        return _layer_params(cfg, prm)

    (g_in0,) = fill_own(run_comm("gather_weights_0", gather_comm([w_in_b[0]])), [w_in_b[0]])
    cos, sin = rope_for(cfg, positions)
    early = [w_out_b[0], b_pw_w[0], b_conv_w, c_conv_w]
    top, bottom = w_in_b[1][:D // 2], w_in_b[1][D // 2:]
    late = [w_out_b[1], b_pw_w[1]]
    conv_ws = {}

    def layer0_rest(rode):
        g_out0, g_pw0, conv_ws['b'], conv_ws['c'] = fill_own(rode, early)
        return layer_prm(0, g_pw0, conv_ws['b'], conv_ws['c']), g_out0.reshape(D, D)

    x1, sv0, rode = layer_forward(
        cfg, 0, x.reshape(n, D), (norm_w[0].reshape(1, -1), layer0_rest), permuted(g_in0), None, cos, sin,
        comms=dict(in_proj=gather_comm(early), attn=gather_comm([top]), conv=gather_comm(late),
                   intra=gather_comm([bottom])))
    lp0, wo0 = sv0['lp'], sv0['wo']
    (g_top,), (g_bottom,) = fill_own(rode['attn'], [top]), fill_own(rode['intra'], [bottom])
    g_out1, g_pw1 = fill_own(rode['conv'], late)
    g_in1 = jnp.concatenate([g_top, g_bottom], axis=1)
    wp1, wo1 = permuted(g_in1), g_out1.reshape(D, D)
    lp1 = layer_prm(1, g_pw1, conv_ws['b'], conv_ws['c'])
    (dx2, loss_local), sv1, _ = layer_forward(cfg, 1, x1, lp1, wp1, wo1, cos, sin, target=loss_target.reshape(n, D))

    def partials(gr):
        return [gr['w_in_blocks'], gr['w_out'].reshape(N_CHIPS, D // N_CHIPS, D),
                gr['b_pw_w'].reshape(N_CHIPS, cfg.BW // N_CHIPS, cfg.BW)]

    dx1, gr1, (_, rs1, _) = layer_backward(cfg, 1, dx2, sv1, lp1, g_in1, wo1, cos, sin,
                                        own_rs=lambda gr: GradReduce(1, partials(gr), chip, c_idx))
    dx0, gr0, (got1, rs0, got0) = layer_backward(cfg, 0, dx1, sv0, lp0, g_in0, wo0, cos, sin, rs=rs1,
                                           own_rs=lambda gr: GradReduce(0, partials(gr), chip, c_idx))
    red1 = rs1.finish(got1)
    red0 = rs0.finish(got0)
    grad_x = dx0.reshape(x.shape)
    grads = [gr0, gr1]

    small_parts = [jnp.stack([grads[l][k] for l in range(DEPTH)]) for k in SMALL] + [loss_local.reshape(1)]
    *small_red, loss = _unpack(all_reduce_small("all_reduce_small", _pack(small_parts)), [a.shape for a in small_parts])
    loss = loss.reshape(())
    g = {}
    for k, a in zip(SMALL, small_red):
        if k in CHIP_SHARDED_SMALL:
            ax = CHIP_SHARDED_SMALL[k]
            width = a.shape[ax] // N_CHIPS
            a = lax.dynamic_slice_in_dim(a, chip * width, width, axis=ax)
        g[k] = a

    delta, new_m, new_v = {}, {}, {}
    for k, g0, g1 in zip(BIG, red0, red1):
        update = adamw_layers if w[k].shape[-1] % LANE == 0 else adamw_cols_major
        g[k], delta[k], new_m[k], new_v[k] = update(f"adamw_{k}", w[k], g0, g1, m[k], v[k], 128)
    outs = adamw_many("adamw_small", *[[d[k] for k in SMALL] for d in (w, g, m, v)])
    for i, k in enumerate(SMALL):
        delta[k], new_m[k], new_v[k] = outs[i], outs[len(SMALL) + i], outs[2 * len(SMALL) + i]
    return (loss, grad_x, *[g[k] for k in WEIGHTS], *[delta[k] for k in WEIGHTS], *[new_m[k] for k in WEIGHTS],
            *[new_v[k] for k in WEIGHTS])
```

```python
import functools
import math

import numpy as np
import jax
import jax.numpy as jnp
from jax import lax
from jax.experimental import pallas as pl
from jax.experimental.pallas import tpu as pltpu

F32 = jnp.float32
BF16 = jnp.bfloat16
HI = lax.Precision.HIGHEST
MESH = pl.DeviceIdType.MESH

DEPTH = 2
A_HEAD = 64
A_GROUP = 3
ATTN_BLOCK = 128
ROT_DIM = 16
ROPE_THETA = 500000.0
B_CONV = 31
B_HALO = 32
C_HEAD = 128
C_CONV = 4
C_HALO = 8
CHUNK = 64
EPS = 1e-6
LANE = 128

ADAM_LR = 0.001
ADAM_B1 = 0.9
ADAM_B2 = 0.999
ADAM_EPS = 1e-08
ADAM_WD = 0.01
ADAM_STEP = 10

VMEM_LIMIT = 56 * 1024 * 1024

WEIGHTS = ['norm_w', 'w_in', 'q_norm_w', 'k_norm_w', 'sinks', 'b_conv_w', 'b_conv_b', 'b_ln_w', 'b_ln_b',
           'b_pw_w', 'b_pw_b', 'c_conv_w', 'c_a_log', 'c_dt_bias', 'c_onorm_w', 'w_out']


class Cfg:
    def __init__(self, d_model=2048, seq=2048):
        self.D = d_model
        self.T = seq
        self.AW = 3 * d_model // 8
        self.AQH = self.AW // A_HEAD
        self.AKH = self.AQH // A_GROUP
        self.AKW = self.AKH * A_HEAD
        self.BW = d_model // 4
        self.CH = (d_model - self.AW - self.BW) // C_HEAD
        self.CW = self.CH * C_HEAD
        AW, AKW, BW, CW, CH = self.AW, self.AKW, self.BW, self.CW, self.CH
        orig = [('qa', AW), ('ka', AKW), ('va', AKW), ('za', AW), ('ub', 2 * BW), ('zb', BW),
                ('qc', CW), ('kc', CW), ('vc', CW), ('bc', CH), ('ac', CH), ('zc', CW)]
        self.orig = {}
        off = 0
        for n, w in orig:
            self.orig[n] = (off, w)
            off += w
        self.IN_COLS = off
        order = ['qa', 'za', 'qc', 'kc', 'vc', 'zc', 'ka', 'va', 'ub', 'zb', 'bc', 'ac']
        self.order = order
        self.g = {}
        off = 0
        for n in order:
            w = self.orig[n][1]
            wp = LANE if n in ('bc', 'ac') else w
            assert off % wp == 0, (n, off, wp)
            self.g[n] = (off, wp)
            off += wp
        self.WP = off

    def blk(self, name):
        off, w = self.g[name]
        return off // w


def _cparams(sem, vmem=VMEM_LIMIT):
    return pltpu.CompilerParams(dimension_semantics=sem, vmem_limit_bytes=vmem)


def _silu(x):
    return x * jax.nn.sigmoid(x)


ANY = pl.BlockSpec(memory_space=pl.ANY)


class Comm:
    def __init__(self, ins, out_shapes, sems, start, finish, aliases=None):
        self.ins, self.out_shapes, self.sems = list(ins), list(out_shapes), list(sems)
        self.start, self.finish, self.aliases = start, finish, dict(aliases or {})


def call_with_comm(body, name, grid, in_specs, out_specs, out_shape, scratch_shapes, semantics, args, comm=None):
    in_specs, out_specs, out_shape, scratch_shapes = list(in_specs), list(out_specs), list(out_shape), list(scratch_shapes)
    if comm is None:
        outs = pl.pallas_call(body, name=name, grid=grid, in_specs=in_specs, out_specs=out_specs, out_shape=out_shape,
                              scratch_shapes=scratch_shapes, compiler_params=_cparams(semantics))(*args)
        return list(outs), []
    ni, no, ns = len(in_specs), len(out_specs), len(scratch_shapes)
    nci, nco = len(comm.ins), len(comm.out_shapes)

    def wrapped(*refs):
        h_in, c_in = refs[:ni], refs[ni:ni + nci]
        h_out, c_out = refs[ni + nci:ni + nci + no], refs[ni + nci + no:ni + nci + no + nco]
        h_scr, c_sems = refs[ni + nci + no + nco:ni + nci + no + nco + ns], refs[ni + nci + no + nco + ns:]
        ids = [pl.program_id(d) for d in range(len(grid))]
        first = functools.reduce(jnp.logical_and, [i == 0 for i in ids])
        last = functools.reduce(jnp.logical_and, [i == g - 1 for i, g in zip(ids, grid)])

        @pl.when(first)
        def _():
            comm.start(c_in, c_out, c_sems)

        body(*h_in, *h_out, *h_scr)

        @pl.when(last)
        def _():
            comm.finish(c_in, c_out, c_sems)

    outs = pl.pallas_call(
        wrapped, name=name, grid=grid, in_specs=in_specs + [ANY] * nci, out_specs=out_specs + [ANY] * nco,
        out_shape=out_shape + comm.out_shapes, scratch_shapes=scratch_shapes + comm.sems,
        input_output_aliases={ni + k: no + v for k, v in comm.aliases.items()},
        compiler_params=_cparams(("arbitrary",) * len(grid)),
    )(*args, *comm.ins)
    return list(outs[:no]), list(outs[no:])


def run_comm(name, comm):
    nci, nco = len(comm.ins), len(comm.out_shapes)

    def body(*refs):
        c_in, c_out, c_sems = refs[:nci], refs[nci:nci + nco], refs[nci + nco:]
        comm.start(c_in, c_out, c_sems)
        comm.finish(c_in, c_out, c_sems)

    return pl.pallas_call(
        body, name=name, in_specs=[ANY] * nci, out_specs=[ANY] * nco, out_shape=comm.out_shapes,
        scratch_shapes=comm.sems, input_output_aliases=comm.aliases,
    )(*comm.ins)


def _stack(xs, axis=0):
    assert axis == 0
    return jnp.concatenate([x[None] for x in xs], axis=0)


def _bdot(a, b, ca, cb, precision=HI):
    dims = (((ca,), (cb,)), ((0,), (0,)))
    if precision is HI and a.dtype == F32:
        ah = a.astype(BF16)
        bh = b.astype(BF16)
        al = (a - ah.astype(F32)).astype(BF16)
        bl = (b - bh.astype(F32)).astype(BF16)
        dg = lambda p, q: lax.dot_general(p, q, dims, preferred_element_type=F32)
        return dg(ah, bh) + (dg(ah, bl) + dg(al, bh))
    return lax.dot_general(a, b, dims, precision=precision, preferred_element_type=F32)


def _rope_matrix(nb):
    i = lax.broadcasted_iota(jnp.int32, (nb, A_HEAD, A_HEAD), 1)
    j = lax.broadcasted_iota(jnp.int32, (nb, A_HEAD, A_HEAD), 2)
    half = ROT_DIM // 2
    neg = (j < half) & (i == j + half)
    pos = (j >= half) & (j < ROT_DIM) & (i == j - half)
    return jnp.where(neg, -1.0, jnp.where(pos, 1.0, 0.0)).astype(F32)


def _norm_rope(xh, w, cos, sin):
    y = xh * lax.rsqrt(jnp.mean(xh * xh, axis=-1, keepdims=True) + EPS) * w
    return y * cos + _bdot(y, _rope_matrix(xh.shape[0]), 2, 1) * sin


def attn_block(cfg, first, q, za, kc, vc, cosc, sinc, kp, vp, cosp, sinp, qnw, knw, sinks_row):
    blk = ATTN_BLOCK
    nq, nk = cfg.AQH, cfg.AKH
    qi = lax.broadcasted_iota(jnp.int32, (blk, 2 * blk), 0)
    kj = lax.broadcasted_iota(jnp.int32, (blk, 2 * blk), 1)
    dist = qi + blk - kj
    valid = ((dist >= 0) & (dist < blk) & (jnp.logical_not(first) | (kj >= blk)))[None]
    cos2 = jnp.concatenate([cosp, cosc], axis=0)
    sin2 = jnp.concatenate([sinp, sinc], axis=0)
    head = lambda x, h: x[:, A_HEAD * h:A_HEAD * (h + 1)]
    k2 = _stack([jnp.concatenate([head(kp, h), head(kc, h)], axis=0) for h in range(nk)], axis=0)
    v2 = _stack([jnp.concatenate([head(vp, h), head(vc, h)], axis=0) for h in range(nk)], axis=0)
    k2 = _norm_rope(k2, knw[None], cos2[None], sin2[None]).astype(BF16)
    v2 = v2.astype(BF16)
    k2 = _stack([k2[h // A_GROUP] for h in range(nq)], axis=0)
    v2 = _stack([v2[h // A_GROUP] for h in range(nq)], axis=0)
    qh = _stack([head(q, h) for h in range(nq)], axis=0)
    qh = _norm_rope(qh, qnw[None], cosc[None], sinc[None]).astype(BF16)
    s = _bdot(qh, k2, 2, 2, None) * (A_HEAD ** -0.5)
    s = jnp.where(valid, s, -1e30)
    sink = _stack([sinks_row[:, A_HEAD * h:A_HEAD * h + 1] for h in range(nq)], axis=0)
    m = jnp.maximum(jnp.max(s, axis=-1, keepdims=True), sink)
    e = jnp.exp(s - m)
    den = jnp.sum(e, axis=-1, keepdims=True) + jnp.exp(sink - m)
    o = _bdot((e / den).astype(BF16), v2, 2, 1, None)
    return (jnp.concatenate([o[h] for h in range(nq)], axis=1) * _silu(za),)


def conv_block(cfg, first, u, zb, uh, cw, cb, lw, lb, pw, pb):
    BW = cfg.BW
    tb = u.shape[0]
    uu = jnp.concatenate([uh, u], axis=0)
    h = uu[:, :BW] * jax.nn.sigmoid(uu[:, BW:])
    row = lax.broadcasted_iota(jnp.int32, h.shape, 0)
    h = jnp.where(first & (row < B_HALO), 0.0, h)
    acc = jnp.zeros((tb, BW), F32) + cb
    base = B_HALO - (B_CONV - 1)
    for k in range(B_CONV):
        acc = acc + cw[k:k + 1, :] * h[base + k:base + k + tb, :]
    mu = jnp.mean(acc, axis=-1, keepdims=True)
    var = jnp.mean(jnp.square(acc - mu), axis=-1, keepdims=True)
    y = (acc - mu) * lax.rsqrt(var + EPS) * lw + lb
    s = _silu(y)
    o = jnp.dot(s.astype(BF16), pw.astype(BF16), preferred_element_type=F32) + pb
    return (o * _silu(zb),)


def gdn_prep_block(cfg, first, xq, xk, xv, braw, araw, hq, hk, hv, cw, alog, dtb):
    CW = cfg.CW
    tb = xq.shape[0]
    outs = []
    for idx, (x, xh) in enumerate(((xq, hq), (xk, hk), (xv, hv))):
        xx = jnp.concatenate([jnp.where(first, 0.0, xh), x], axis=0)
        w = cw[:, idx * CW:(idx + 1) * CW]
        acc = jnp.zeros((tb, CW), F32)
        base = C_HALO - (C_CONV - 1)
        for k in range(C_CONV):
            acc = acc + w[k:k + 1, :] * xx[base + k:base + k + tb, :]
        y = _silu(acc)
        if idx < 2:
            parts = []
            for h in range(cfg.CH):
                yh = y[:, C_HEAD * h:C_HEAD * (h + 1)]
                parts.append(yh * lax.rsqrt(jnp.sum(yh * yh, axis=-1, keepdims=True) + EPS))
            y = jnp.concatenate(parts, axis=1)
        outs.append(y)
    beta = jax.nn.sigmoid(braw)
    g = -jnp.exp(alog) * jax.nn.softplus(araw + dtb)
    return outs[0], outs[1], outs[2], g, beta


def _inverse_unit_lower(low, eye):
    pw = low
    inv = eye - low
    for _ in range(5):
        pwb = pw.astype(BF16)
        pw = _bdot(pwb, pwb, 2, 1, None)
        inv = inv + _bdot(inv.astype(BF16), pw.astype(BF16), 2, 1, None)
    ax = inv + _bdot(low, inv, 2, 1)
    return inv + _bdot(inv, eye - ax, 2, 1)


@jax.custom_vjp
def _saved_inverse(low, inv):
    return inv


def _saved_inverse_fwd(low, inv):
    return inv, inv


def _saved_inverse_bwd(inv, d):
    dlow = -_bdot(_bdot(inv, d, 1, 1), inv, 2, 2)
    return dlow, jnp.zeros_like(inv)


_saved_inverse.defvjp(_saved_inverse_fwd, _saved_inverse_bwd)


def gdn_intra_rows(cfg, first, qn, kn, v, g, beta, inv_saved=None):
    c = CHUNK
    CH = cfg.CH
    nchunk = qn.shape[0] // c
    i = lax.broadcasted_iota(jnp.int32, (c, c), 0)
    j = lax.broadcasted_iota(jnp.int32, (c, c), 1)
    incl = (i >= j)[None]
    strict = (i > j)[None]
    eye = (i == j).astype(F32)[None]
    tri = (i >= j).astype(F32)
    rows = [slice(c * ci, c * (ci + 1)) for ci in range(nchunk)]
    gcs = [jnp.dot(tri, g[r], precision=HI, preferred_element_type=F32) for r in rows]
    pairs = [(ci, h) for ci in range(nchunk) for h in range(CH)]
    heads = lambda x, wd: _stack([x[rows[ci], wd * h:wd * (h + 1)] for ci, h in pairs], axis=0)
    gch = _stack([gcs[ci][:, h:h + 1] for ci, h in pairs], axis=0)
    bh = _stack([beta[rows[ci], h:h + 1] for ci, h in pairs], axis=0)
    q = heads(qn, C_HEAD) * (C_HEAD ** -0.5)
    k = heads(kn, C_HEAD)
    vv = heads(v, C_HEAD)
    a = jnp.broadcast_to(gch, (len(pairs), c, c))
    diff = jnp.where(incl, a - jnp.swapaxes(a, 1, 2), 0.0)
    decay = jnp.where(incl, jnp.exp(diff), 0.0)
    kb = k * bh
    low = jnp.where(strict, _bdot(kb, k, 2, 2) * decay, 0.0)
    if inv_saved is None:
        inv = _inverse_unit_lower(low, eye)
    else:
        inv = _saved_inverse(low, heads(inv_saved, c))
    eg = jnp.exp(gch)
    sol = _bdot(inv, jnp.concatenate([vv * bh, kb * eg], axis=2), 2, 1)
    intra = jnp.where(incl, _bdot(q, k, 2, 2) * decay, 0.0)
    qg = q * eg
    kd = k * jnp.exp(gch[:, c - 1:c, :] - gch)
    glast = jnp.concatenate([jnp.broadcast_to(gc[c - 1:c, :], gc.shape) for gc in gcs], axis=0)

    def unstack(x):
        return jnp.concatenate([jnp.concatenate([x[ci * CH + h] for h in range(CH)], axis=1) for ci in range(nchunk)],
                               axis=0)

    outs = (unstack(sol[:, :, :C_HEAD]), unstack(sol[:, :, C_HEAD:]), unstack(qg), unstack(kd), unstack(intra), glast)
    return outs + (unstack(inv),) if inv_saved is None else outs


def gdn_state_step(S, u, w, qg, kd, intra, glast):
    v_new = u - _bdot(w, S, 2, 1)
    o = _bdot(qg, S, 2, 1) + _bdot(intra, v_new, 2, 1)
    S_next = S * jnp.exp(glast) + _bdot(kd, v_new, 1, 1)
    return o, S_next


def gdn_out_block(cfg, first, o, zc, onw):
    parts = []
    for h in range(cfg.CH):
        sl = slice(C_HEAD * h, C_HEAD * (h + 1))
        oh = o[:, sl]
        y = oh * lax.rsqrt(jnp.mean(oh * oh, axis=-1, keepdims=True) + EPS) * onw
        parts.append(y * _silu(zc[:, sl]))
    return (jnp.concatenate(parts, axis=1),)


def rms_block(x, nw):
    return x * lax.rsqrt(jnp.mean(x * x, axis=-1, keepdims=True) + EPS) * nw


class Row:
    def __init__(self, arr, width, colblk=0, grad=None):
        self.arr, self.width, self.colblk, self.grad = arr, width, colblk, grad


class Halo:
    def __init__(self, arr, width, colblk, hr, tie=None):
        self.arr, self.width, self.colblk, self.hr, self.tie = arr, width, colblk, hr, tie


def _row_specs(tb, rows, halos, params, pos):
    specs = [pl.BlockSpec((tb, r.width), lambda i, cb=r.colblk: (pos(i), cb)) for r in rows]
    specs += [pl.BlockSpec((h.hr, h.width),
                           lambda i, cb=h.colblk, m=tb // h.hr: (jnp.maximum(pos(i) * m - 1, 0), cb))
              for h in halos]
    specs += [pl.BlockSpec(p.shape, lambda i: (0, 0)) for p in params]
    return specs


def rb_fwd(name, fn, n, tb, bps, rows, halos, params, outs, comm=None):
    nr, nh, npar = len(rows), len(halos), len(params)

    def body(*refs):
        ins = refs[:nr + nh + npar]
        o_refs = refs[nr + nh + npar:]
        first = (pl.program_id(0) % bps) == 0
        res = fn(first, *[r[...] for r in ins])
        for ref, val in zip(o_refs, res):
            ref[...] = val.astype(ref.dtype)

    res, carried = call_with_comm(
        body, name, (n // tb,), _row_specs(tb, rows, halos, params, lambda i: i),
        [pl.BlockSpec((tb, w), lambda i: (i, 0)) for w, _ in outs],
        [jax.ShapeDtypeStruct((n, w), dt) for w, dt in outs], [], ("parallel",),
        [r.arr for r in rows] + [h.arr for h in halos] + list(params), comm)
    return (res, carried) if comm is not None else res


def rb_bwd(name, fn, n, tb, bps, rows, halos, params, douts, param_grads, comm=None):
    nr, nh, npar, nd = len(rows), len(halos), len(params), len(douts)
    nblk = n // tb
    grow = [k for k, r in enumerate(rows) if r.grad is not None]
    ghalo = [k for k, h in enumerate(halos) if h.tie is not None]
    gpar = [k for k, f in enumerate(param_grads) if f]
    pos = lambda i: nblk - 1 - i

    def body(*refs):
        ins = refs[:nr + nh + npar]
        d_refs = refs[nr + nh + npar:nr + nh + npar + nd]
        rest = refs[nr + nh + npar + nd:]
        grow_refs = rest[:len(grow)]
        gpar_refs = rest[len(grow):len(grow) + len(gpar)]
        carry_refs = rest[len(grow) + len(gpar):]
        i = pl.program_id(0)
        first = (pos(i) % bps) == 0
        vals = [r[...] for r in ins]
        diff_idx = grow + [nr + k for k in ghalo] + [nr + nh + k for k in gpar]

        def f(*dargs):
            full = list(vals)
            for k, a in zip(diff_idx, dargs):
                full[k] = a
            return fn(first, *full)

        res, vjp = jax.vjp(f, *[vals[k] for k in diff_idx])
        grads = vjp(tuple(d[...].astype(r.dtype) for d, r in zip(d_refs, res)))
        g_rows = list(grads[:len(grow)])
        g_halos = grads[len(grow):len(grow) + len(ghalo)]
        g_pars = grads[len(grow) + len(ghalo):]

        @pl.when(i == 0)
        def _():
            for c in carry_refs:
                c[...] = jnp.zeros_like(c)
            for p in gpar_refs:
                p[...] = jnp.zeros_like(p)

        for k, ref in enumerate(grow_refs):
            ref[...] = g_rows[k].astype(ref.dtype)
        for ci, hk in enumerate(ghalo):
            h = halos[hk]
            k = grow.index(h.tie)
            tail = g_rows[k][tb - h.hr:, :] + carry_refs[ci][...]
            grow_refs[k][tb - h.hr:, :] = tail.astype(grow_refs[k].dtype)
            carry_refs[ci][...] = g_halos[ci]
        for ref, gp in zip(gpar_refs, g_pars):
            ref[...] += gp

    out_specs = [pl.BlockSpec((tb, rows[k].width), lambda i: (pos(i), 0)) for k in grow]
    out_specs += [pl.BlockSpec(params[k].shape, lambda i: (0, 0)) for k in gpar]
    out_shape = [jax.ShapeDtypeStruct((n, rows[k].width), rows[k].grad) for k in grow]
    out_shape += [jax.ShapeDtypeStruct(params[k].shape, F32) for k in gpar]
    in_specs = _row_specs(tb, rows, halos, params, pos)
    in_specs += [pl.BlockSpec((tb, d.shape[1]), lambda i: (pos(i), 0)) for d in douts]
    res, carried = call_with_comm(
        body, name, (nblk,), in_specs, out_specs, out_shape,
        [pltpu.VMEM((halos[k].hr, halos[k].width), F32) for k in ghalo], ("arbitrary",),
        [r.arr for r in rows] + [h.arr for h in halos] + list(params) + list(douts), comm)
    return (res, carried) if comm is not None else res


_DIMS = {'nn': (((1,), (0,)), ((), ())), 'nt': (((1,), (1,)), ((), ())), 'tn': (((0,), (0,)), ((), ()))}


def matmul(name, a, b, mode, tm, tn, tk, out_dtype=F32, add=None, comm=None):
    if mode == 'tn':
        K, M = a.shape
    else:
        M, K = a.shape
    N = b.shape[0] if mode == 'nt' else b.shape[1]
    tm, tn, tk = min(tm, M), min(tn, N), min(tk, K)
    assert M % tm == 0 and N % tn == 0 and K % tk == 0, (name, M, N, K, tm, tn, tk)
    nk = K // tk
    a_spec = pl.BlockSpec((tk, tm), lambda i, j, k: (k, i)) if mode == 'tn' else pl.BlockSpec((tm, tk), lambda i, j, k: (i, k))
    b_spec = pl.BlockSpec((tn, tk), lambda i, j, k: (j, k)) if mode == 'nt' else pl.BlockSpec((tk, tn), lambda i, j, k: (k, j))
    o_spec = pl.BlockSpec((tm, tn), lambda i, j, k: (i, j))
    has_add = add is not None

    def body(*refs):
        a_ref, b_ref = refs[0], refs[1]
        add_ref = refs[2] if has_add else None
        o_ref = refs[-1]
        k = pl.program_id(2)
        part = lax.dot_general(a_ref[...].astype(BF16), b_ref[...].astype(BF16), _DIMS[mode], preferred_element_type=F32)

        @pl.when(k == 0)
        def _():
            o_ref[...] = ((part + add_ref[...]) if has_add else part).astype(o_ref.dtype)

        if nk > 1:
            @pl.when(k > 0)
            def _():
                o_ref[...] += part

    assert nk == 1 or out_dtype == F32
    ins = [a, b] + ([add] if has_add else [])
    in_specs = [a_spec, b_spec] + ([o_spec] if has_add else [])
    outs, couts = call_with_comm(body, name, (M // tm, N // tn, nk), in_specs, [o_spec],
                                 [jax.ShapeDtypeStruct((M, N), out_dtype)], [], ("parallel", "parallel", "arbitrary"),
                                 ins, comm)
    return (outs[0], couts) if comm is not None else outs[0]


def grad_w_blocks(name, h, dpb, tm, tk):
    n, d = h.shape
    nb, _, s = dpb.shape
    tm, tk = min(tm, d), min(tk, n)
    assert d % tm == 0 and n % tk == 0
    nk = n // tk

    def body(h_ref, b_ref, o_ref):
        k = pl.program_id(2)
        part = lax.dot_general(h_ref[...], b_ref[0], _DIMS['tn'], preferred_element_type=F32)

        @pl.when(k == 0)
        def _():
            o_ref[0] = part

        if nk > 1:
            @pl.when(k > 0)
            def _():
                o_ref[0] += part

    return pl.pallas_call(
        body, name=name, grid=(nb, d // tm, nk),
        in_specs=[pl.BlockSpec((tk, tm), lambda j, i, k: (k, i)), pl.BlockSpec((1, tk, s), lambda j, i, k: (j, k, 0))],
        out_specs=pl.BlockSpec((1, tm, s), lambda j, i, k: (j, i, 0)),
        out_shape=jax.ShapeDtypeStruct((nb, d, s), F32),
        compiler_params=_cparams(("parallel", "parallel", "arbitrary")),
    )(h, dpb)


def grad_h_blocks(name, dpb, wb, tm, tn, comm=None):
    nb, n, s = dpb.shape
    d = wb.shape[1]
    tm, tn = min(tm, n), min(tn, d)
    assert n % tm == 0 and d % tn == 0

    def body(a_ref, b_ref, o_ref):
        k = pl.program_id(2)
        part = lax.dot_general(a_ref[0], b_ref[0], _DIMS['nt'], preferred_element_type=F32)

        @pl.when(k == 0)
        def _():
            o_ref[...] = part

        @pl.when(k > 0)
        def _():
            o_ref[...] += part

    outs, carried = call_with_comm(
        body, name, (n // tm, d // tn, nb),
        [pl.BlockSpec((1, tm, s), lambda i, j, k: (k, i, 0)), pl.BlockSpec((1, tn, s), lambda i, j, k: (k, j, 0))],
        [pl.BlockSpec((tm, tn), lambda i, j, k: (i, j))], [jax.ShapeDtypeStruct((n, d), F32)], [],
        ("parallel", "parallel", "arbitrary"), [dpb, wb], comm)
    return (outs[0], carried) if comm is not None else outs[0]


def norm_in_proj(name, x, nw, wp, tm, tn, comm=None):
    n, d = x.shape
    wpc = wp.shape[1]
    tm, tn = min(tm, n), min(tn, wpc)
    assert n % tm == 0 and wpc % tn == 0

    def body(x_ref, nw_ref, w_ref, p_ref, h_ref):
        @pl.when(pl.program_id(1) == 0)
        def _():
            h_ref[...] = rms_block(x_ref[...], nw_ref[...]).astype(BF16)

        p_ref[...] = jnp.dot(h_ref[...], w_ref[...], preferred_element_type=F32)

    outs, couts = call_with_comm(
        body, name, (n // tm, wpc // tn),
        [pl.BlockSpec((tm, d), lambda i, j: (i, 0)), pl.BlockSpec((1, d), lambda i, j: (0, 0)),
         pl.BlockSpec((d, tn), lambda i, j: (0, j))],
        [pl.BlockSpec((tm, tn), lambda i, j: (i, j)), pl.BlockSpec((tm, d), lambda i, j: (i, 0))],
        [jax.ShapeDtypeStruct((n, wpc), F32), jax.ShapeDtypeStruct((n, d), BF16)], [], ("parallel", "arbitrary"),
        [x, nw, wp], comm)
    return (outs[0], outs[1], couts) if comm is not None else (outs[0], outs[1])


def norm_bwd(name, x, nw, dh, dres, tb, comm=None):
    n, d = x.shape
    tb = min(tb, n)

    def body(x_ref, nw_ref, dh_ref, dres_ref, dx_ref, dnw_ref):
        @pl.when(pl.program_id(0) == 0)
        def _():
            dnw_ref[...] = jnp.zeros_like(dnw_ref)

        _, vjp = jax.vjp(rms_block, x_ref[...], nw_ref[...])
        dx, dnw = vjp(dh_ref[...])
        dx_ref[...] = dx + dres_ref[...]
        dnw_ref[...] += dnw

    row = pl.BlockSpec((tb, d), lambda i: (i, 0))
    par = pl.BlockSpec((1, d), lambda i: (0, 0))
    outs, carried = call_with_comm(
        body, name, (n // tb,), [row, par, row, row], [row, par],
        [jax.ShapeDtypeStruct((n, d), F32), jax.ShapeDtypeStruct((1, d), F32)], [], ("arbitrary",),
        [x, nw, dh, dres], comm)
    return (outs[0], outs[1], carried) if comm is not None else (outs[0], outs[1])


def out_proj_loss(name, y, wo, x, target, tm):
    n, d = x.shape
    tm = min(tm, n)
    assert n % tm == 0

    def body(y_ref, w_ref, x_ref, t_ref, dz_ref, loss_ref):
        @pl.when(pl.program_id(0) == 0)
        def _():
            loss_ref[...] = jnp.zeros_like(loss_ref)

        z = x_ref[...] + jnp.dot(y_ref[...], w_ref[...], preferred_element_type=F32)
        err = z - t_ref[...]
        dz_ref[...] = err * (1.0 / d)
        part = 0.5 * jnp.sum(jnp.mean(err * err, axis=-1, keepdims=True), axis=0, keepdims=True)
        loss_ref[...] += jnp.broadcast_to(part, loss_ref.shape)

    row = pl.BlockSpec((tm, d), lambda i: (i, 0))
    dz, loss = pl.pallas_call(
        body, name=name, grid=(n // tm,),
        in_specs=[pl.BlockSpec((tm, y.shape[1]), lambda i: (i, 0)), pl.BlockSpec(wo.shape, lambda i: (0, 0)), row, row],
        out_specs=[row, pl.BlockSpec((8, LANE), lambda i: (0, 0))],
        out_shape=[jax.ShapeDtypeStruct((n, d), F32), jax.ShapeDtypeStruct((8, LANE), F32)],
        compiler_params=_cparams(("arbitrary",)),
    )(y, wo, x, target)
    return dz, loss[0, 0]


def rope_tables(name, pos_col, inv_freq_row):
    n = pos_col.shape[0]

    def body(p_ref, f_ref, c_ref, s_ref):
        ang = p_ref[...].astype(F32) * f_ref[...]
        lane = lax.broadcasted_iota(jnp.int32, ang.shape, 1)
        c_ref[...] = jnp.where(lane < ROT_DIM, jnp.cos(ang), 1.0)
        s_ref[...] = jnp.where(lane < ROT_DIM, jnp.sin(ang), 0.0)

    return pl.pallas_call(
        body, name=name, out_shape=[jax.ShapeDtypeStruct((n, A_HEAD), F32)] * 2,
    )(pos_col, inv_freq_row)


def _scan_operands(cfg, nseq, u_ref, w_ref, qg_ref, kd_ref, a_ref, gl_ref):
    pairs = [(b, h) for b in range(nseq) for h in range(cfg.CH)]
    st = lambda r, wd: _stack([r[b, :, wd * h:wd * (h + 1)] for b, h in pairs], axis=0)
    gl = _stack([gl_ref[b, 0:1, h:h + 1] for b, h in pairs], axis=0)
    return st(u_ref, C_HEAD), st(w_ref, C_HEAD), st(qg_ref, C_HEAD), st(kd_ref, C_HEAD), st(a_ref, CHUNK), gl


def gdn_scan_fwd(name, cfg, nseq, u, w, qg, kd, intra, glast):
    CH, CW, T = cfg.CH, cfg.CW, cfg.T
    nc = T // CHUNK

    def body(u_ref, w_ref, qg_ref, kd_ref, a_ref, gl_ref, o_ref, sin_ref, s_ref):
        @pl.when(pl.program_id(0) == 0)
        def _():
            s_ref[...] = jnp.zeros_like(s_ref)

        S = s_ref[...]
        for b in range(nseq):
            sin_ref[b, 0] = S[b * CH:(b + 1) * CH]
        o, S_next = gdn_state_step(S, *_scan_operands(cfg, nseq, u_ref, w_ref, qg_ref, kd_ref, a_ref, gl_ref))
        s_ref[...] = S_next
        for b in range(nseq):
            o_ref[b] = jnp.concatenate([o[b * CH + h] for h in range(CH)], axis=1)

    row = lambda wd: pl.BlockSpec((nseq, CHUNK, wd), lambda c: (0, c, 0))
    widths = [CW, CW, CW, CW, CH * CHUNK, LANE]
    o, s_in = pl.pallas_call(
        body, name=name, grid=(nc,),
        in_specs=[row(x) for x in widths],
        out_specs=[row(CW), pl.BlockSpec((nseq, 1, CH, C_HEAD, C_HEAD), lambda c: (0, c, 0, 0, 0))],
        out_shape=[jax.ShapeDtypeStruct((nseq, T, CW), F32),
                   jax.ShapeDtypeStruct((nseq, nc, CH, C_HEAD, C_HEAD), F32)],
        scratch_shapes=[pltpu.VMEM((nseq * CH, C_HEAD, C_HEAD), F32)],
        compiler_params=_cparams(("arbitrary",)),
    )(*[a.reshape(nseq, T, a.shape[1]) for a in (u, w, qg, kd, intra, glast)])
    return o.reshape(nseq * T, CW), s_in


def gdn_scan_bwd(name, cfg, nseq, u, w, qg, kd, intra, glast, s_in, do, comm=None):
    CH, CW, T = cfg.CH, cfg.CW, cfg.T
    nc = T // CHUNK

    def body(u_ref, w_ref, qg_ref, kd_ref, a_ref, gl_ref, sin_ref, do_ref,
             du_ref, dw_ref, dqg_ref, dkd_ref, da_ref, dgl_ref, ds_ref):
        @pl.when(pl.program_id(0) == 0)
        def _():
            ds_ref[...] = jnp.zeros_like(ds_ref)

        S = jnp.concatenate([sin_ref[b, 0] for b in range(nseq)], axis=0)
        dout = _stack([do_ref[b, :, C_HEAD * h:C_HEAD * (h + 1)] for b in range(nseq) for h in range(CH)], axis=0)
        _, vjp = jax.vjp(gdn_state_step, S, *_scan_operands(cfg, nseq, u_ref, w_ref, qg_ref, kd_ref, a_ref, gl_ref))
        dS, du, dw, dqg, dkd, da, dg = vjp((dout, ds_ref[...]))
        ds_ref[...] = dS
        lane = lax.broadcasted_iota(jnp.int32, (CHUNK, LANE), 1)
        rowi = lax.broadcasted_iota(jnp.int32, (CHUNK, LANE), 0)
        for b in range(nseq):
            cat = lambda x: jnp.concatenate([x[b * CH + h] for h in range(CH)], axis=1)
            du_ref[b] = cat(du)
            dw_ref[b] = cat(dw)
            dqg_ref[b] = cat(dqg)
            dkd_ref[b] = cat(dkd)
            da_ref[b] = cat(da)
            dgl = jnp.zeros((CHUNK, LANE), F32)
            for h in range(CH):
                dgl = dgl + jnp.where((lane == h) & (rowi == 0), dg[b * CH + h], 0.0)
            dgl_ref[b] = dgl

    row = lambda wd: pl.BlockSpec((nseq, CHUNK, wd), lambda c: (0, nc - 1 - c, 0))
    widths = [CW, CW, CW, CW, CH * CHUNK, LANE]
    outs, carried = call_with_comm(
        body, name, (nc,),
        [row(x) for x in widths]
        + [pl.BlockSpec((nseq, 1, CH, C_HEAD, C_HEAD), lambda c: (0, nc - 1 - c, 0, 0, 0)), row(CW)],
        [row(x) for x in widths], [jax.ShapeDtypeStruct((nseq, T, x), F32) for x in widths],
        [pltpu.VMEM((nseq * CH, C_HEAD, C_HEAD), F32)], ("arbitrary",),
        [a.reshape(nseq, T, a.shape[1]) for a in (u, w, qg, kd, intra, glast)] + [s_in, do.reshape(nseq, T, CW)], comm)
    return [o.reshape(nseq * T, o.shape[2]) for o in outs], carried


def _tile(total, cap, unit=LANE):
    best = None
    for t in range(unit, min(cap, total) + 1, unit):
        if total % t == 0:
            best = t
    assert best is not None, (total, cap, unit)
    return best


def _pad_lanes(v, width=LANE):
    return jnp.pad(v.reshape(1, -1), ((0, 0), (0, width - v.shape[-1])))


def permute_w_in(cfg, w):
    parts = []
    for n in cfg.order:
        off, wd = cfg.orig[n]
        blk = w[:, off:off + wd]
        if cfg.g[n][1] != wd:
            blk = jnp.pad(blk, ((0, 0), (0, cfg.g[n][1] - wd)))
        parts.append(blk)
    return jnp.concatenate(parts, axis=1)


def chip_blocks(cfg, groups, n_chips):
    s = cfg.IN_COLS // n_chips
    blocks = []
    for j in range(n_chips):
        lo, hi = j * s, (j + 1) * s
        pieces = []
        for name, (off, wd) in cfg.orig.items():
            a, b = max(lo, off), min(hi, off + wd)
            if a < b:
                pieces.append(groups[name][:, a - off:b - off])
        blocks.append(jnp.concatenate(pieces, axis=1))
    return jnp.stack(blocks)


def _layer_params(cfg, prm):
    return dict(
        nw=prm['norm_w'].reshape(1, -1),
        qnw=prm['q_norm_w'].reshape(1, -1), knw=prm['k_norm_w'].reshape(1, -1),
        sinks_row=jnp.repeat(prm['sinks'], A_HEAD).reshape(1, -1),
        cw=prm['b_conv_w'], cb=prm['b_conv_b'].reshape(1, -1),
        lw=prm['b_ln_w'].reshape(1, -1), lb=prm['b_ln_b'].reshape(1, -1),
        pw=prm['b_pw_w'], pb=prm['b_pw_b'].reshape(1, -1),
        ccw=prm['c_conv_w'], alog=_pad_lanes(prm['c_a_log']), dtb=_pad_lanes(prm['c_dt_bias']),
        onw=prm['c_onorm_w'].reshape(1, -1),
    )


def _attn_io(cfg, p, cos, sin, grads):
    gq = BF16 if grads else None
    rows = [Row(p, cfg.AW, cfg.blk('qa'), gq), Row(p, cfg.AW, cfg.blk('za'), gq),
            Row(p, cfg.AKW, cfg.blk('ka'), gq), Row(p, cfg.AKW, cfg.blk('va'), gq),
            Row(cos, A_HEAD), Row(sin, A_HEAD)]
    halos = [Halo(p, cfg.AKW, cfg.blk('ka'), ATTN_BLOCK, 2 if grads else None),
             Halo(p, cfg.AKW, cfg.blk('va'), ATTN_BLOCK, 3 if grads else None),
             Halo(cos, A_HEAD, 0, ATTN_BLOCK), Halo(sin, A_HEAD, 0, ATTN_BLOCK)]
    return rows, halos


def _conv_io(cfg, p, grads):
    gq = BF16 if grads else None
    rows = [Row(p, 2 * cfg.BW, cfg.blk('ub'), gq), Row(p, cfg.BW, cfg.blk('zb'), gq)]
    halos = [Halo(p, 2 * cfg.BW, cfg.blk('ub'), B_HALO, 0 if grads else None)]
    return rows, halos


def _prep_io(cfg, p, grads):
    gq = BF16 if grads else None
    rows = [Row(p, cfg.CW, cfg.blk(n), gq) for n in ('qc', 'kc', 'vc')]
    rows += [Row(p, LANE, cfg.blk('bc'), gq), Row(p, LANE, cfg.blk('ac'), gq)]
    halos = [Halo(p, cfg.CW, cfg.blk(n), C_HALO, k if grads else None) for k, n in enumerate(('qc', 'kc', 'vc'))]
    return rows, halos


TB_CONV = 128
TB_PREP = 256
TB_OUT = 512
TB_INTRA_FWD = 256
TB_INTRA_BWD = 256


def layer_forward(cfg, l, x, lp, wp, wo, cos, sin, comms=None, target=None):
    n = x.shape[0]
    nseq = n // cfg.T
    T = cfg.T
    comms = comms or {}
    carried = {}

    def hosted(key, res):
        if comms.get(key) is None:
            return res
        res, carried[key] = res
        return res

    nw = lp['nw'] if isinstance(lp, dict) else lp[0]
    p, h, *rode = norm_in_proj(f"in_proj_{l}", x, nw, wp, 1024, _tile(cfg.WP, 768), comm=comms.get('in_proj'))
    if rode:
        carried['in_proj'] = rode[0]
    if not isinstance(lp, dict):
        lp, wo = lp[1](carried['in_proj'])
    rows, halos = _attn_io(cfg, p, cos, sin, False)
    (oa,) = hosted('attn', rb_fwd(f"attn_fwd_{l}", functools.partial(attn_block, cfg), n, ATTN_BLOCK, T // ATTN_BLOCK,
                                  rows, halos, [lp['qnw'], lp['knw'], lp['sinks_row']], [(cfg.AW, BF16)],
                                  comm=comms.get('attn')))
    rows, halos = _conv_io(cfg, p, False)
    tbb = min(TB_CONV, T)
    (ob,) = hosted('conv', rb_fwd(f"conv_fwd_{l}", functools.partial(conv_block, cfg), n, tbb, T // tbb, rows, halos,
                                  [lp['cw'], lp['cb'], lp['lw'], lp['lb'], lp['pw'], lp['pb']], [(cfg.BW, BF16)],
                                  comm=comms.get('conv')))
    rows, halos = _prep_io(cfg, p, False)
    tbp = min(TB_PREP, T)
    qn, kn, v, g, beta = rb_fwd(f"gdn_prep_fwd_{l}", functools.partial(gdn_prep_block, cfg), n, tbp, T // tbp, rows,
                                halos, [lp['ccw'], lp['alog'], lp['dtb']],
                                [(cfg.CW, F32)] * 3 + [(LANE, F32)] * 2)
    intra_outs = hosted('intra', rb_fwd(
        f"gdn_intra_fwd_{l}", functools.partial(gdn_intra_rows, cfg), n, min(TB_INTRA_FWD, T), T // min(TB_INTRA_FWD, T),
        [Row(qn, cfg.CW), Row(kn, cfg.CW), Row(v, cfg.CW), Row(g, LANE), Row(beta, LANE)], [], [],
        [(cfg.CW, F32)] * 4 + [(cfg.CH * CHUNK, F32), (LANE, F32), (cfg.CH * CHUNK, F32)], comm=comms.get('intra')))
    intra_outs, inv = intra_outs[:6], intra_outs[6]
    o, s_in = gdn_scan_fwd(f"gdn_scan_fwd_{l}", cfg, nseq, *intra_outs)
    tbo = min(TB_OUT, T)
    (oc,) = rb_fwd(f"gdn_out_fwd_{l}", functools.partial(gdn_out_block, cfg), n, tbo, T // tbo,
                   [Row(o, cfg.CW), Row(p, cfg.CW, cfg.blk('zc'))], [], [lp['onw']], [(cfg.CW, BF16)])
    y = jnp.concatenate([oa, ob, oc], axis=1)
    if target is None:
        x_next = matmul(f"out_proj_{l}", y, wo, 'nn', 1024, 1024, cfg.D, add=x)
    else:
        x_next = out_proj_loss(f"out_proj_{l}", y, wo, x, target, 512)
    saved = dict(x=x, p=p, h=h, y=y, qn=qn, kn=kn, v=v, g=g, beta=beta, intra_outs=intra_outs, inv=inv, s_in=s_in, o=o,
                 lp=lp, wo=wo)
    return x_next, saved, carried


def layer_backward(cfg, l, dxn, sv, lp, wb, wo, cos, sin, rs=None, own_rs=None):
    n = dxn.shape[0]
    nseq = n // cfg.T
    T = cfg.T
    p = sv['p']
    AW, BW, CW = cfg.AW, cfg.BW, cfg.CW
    dy = matmul(f"dy_{l}", dxn, wo, 'nt', 1024, 1024, cfg.D)
    dwo = matmul(f"dwo_{l}", sv['y'], dxn, 'tn', 1024, 1024, 2048)
    doa, dob, doc = dy[:, :AW], dy[:, AW:AW + BW], dy[:, AW + BW:]
    tbo = min(TB_OUT, T)
    do, dzc, donw = rb_bwd(f"gdn_out_bwd_{l}", functools.partial(gdn_out_block, cfg), n, tbo, T // tbo,
                           [Row(sv['o'], CW, 0, F32), Row(p, CW, cfg.blk('zc'), BF16)], [], [lp['onw']], [doc], [True])
    dintra, got_rest = gdn_scan_bwd(f"gdn_scan_bwd_{l}", cfg, nseq, *sv['intra_outs'], sv['s_in'], do,
                                    comm=None if rs is None else rs.scatter([1, 2]))
    dqn, dkn, dv, dg, dbeta = rb_bwd(
        f"gdn_intra_bwd_{l}", functools.partial(gdn_intra_rows, cfg), n, min(TB_INTRA_BWD, T), T // min(TB_INTRA_BWD, T),
        [Row(sv['qn'], CW, 0, F32), Row(sv['kn'], CW, 0, F32), Row(sv['v'], CW, 0, F32), Row(sv['g'], LANE, 0, F32),
         Row(sv['beta'], LANE, 0, F32), Row(sv['inv'], cfg.CH * CHUNK)], [], [], list(dintra), [])
    rows, halos = _prep_io(cfg, p, True)
    tbp = min(TB_PREP, T)
    dqc, dkc, dvc, dbc, dac, dccw, dalog, ddtb = rb_bwd(
        f"gdn_prep_bwd_{l}", functools.partial(gdn_prep_block, cfg), n, tbp, T // tbp, rows, halos,
        [lp['ccw'], lp['alog'], lp['dtb']], [dqn, dkn, dv, dg, dbeta], [True] * 3)
    rows, halos = _conv_io(cfg, p, True)
    tbb = min(TB_CONV, T)
    conv_grads = rb_bwd(
        f"conv_bwd_{l}", functools.partial(conv_block, cfg), n, tbb, T // tbb, rows, halos,
        [lp['cw'], lp['cb'], lp['lw'], lp['lb'], lp['pw'], lp['pb']], [dob], [True] * 6,
        comm=None if rs is None else rs.scatter([0]))
    got = None
    if rs is not None:
        conv_grads, got_w_in = conv_grads
        got = got_w_in + got_rest
    dub, dzb, dcw, dcb, dlw, dlb, dpw, dpb = conv_grads
    rows, halos = _attn_io(cfg, p, cos, sin, True)
    dqa, dza, dka, dva, dqnw, dknw, dsinks_row = rb_bwd(
        f"attn_bwd_{l}", functools.partial(attn_block, cfg), n, ATTN_BLOCK, T // ATTN_BLOCK, rows, halos,
        [lp['qnw'], lp['knw'], lp['sinks_row']], [doa], [True] * 3)
    dgroups = dict(qa=dqa, za=dza, qc=dqc, kc=dkc, vc=dvc, zc=dzc, ka=dka, va=dva, ub=dub, zb=dzb, bc=dbc, ac=dac)
    dp_blocks = chip_blocks(cfg, dgroups, N_CHIPS)
    dw_in = grad_w_blocks(f"dwp_{l}", sv['h'], dp_blocks, 1024, 2048)
    mine = None if own_rs is None else own_rs(dict(w_in_blocks=dw_in, w_out=dwo, b_pw_w=dpw))
    got_mine = None
    if mine is None:
        dh = grad_h_blocks(f"dh_{l}", dp_blocks, wb, 1024, 1024)
        dx, dnw = norm_bwd(f"norm_bwd_{l}", sv['x'], lp['nw'], dh, dxn, 256)
    elif l > 0:
        dh, mine_received = grad_h_blocks(f"dh_{l}", dp_blocks, wb, 1024, 1024, comm=mine.swap())
        mine.add(mine_received)
        dx, dnw = norm_bwd(f"norm_bwd_{l}", sv['x'], lp['nw'], dh, dxn, 256)
    else:
        mine.add(run_comm(f"rs{l}_swap_halves", mine.swap()))
        dh, got_w_in = grad_h_blocks(f"dh_{l}", dp_blocks, wb, 1024, 1024, comm=mine.scatter([0]))
        dx, dnw, got_others = norm_bwd(f"norm_bwd_{l}", sv['x'], lp['nw'], dh, dxn, 256, comm=mine.scatter([1, 2]))
        got_mine = got_w_in + got_others
    grads = dict(
        norm_w=dnw[0], w_in_blocks=dw_in, q_norm_w=dqnw[0], k_norm_w=dknw[0],
        sinks=dsinks_row.reshape(cfg.AQH, A_HEAD)[:, 0],
        b_conv_w=dcw, b_conv_b=dcb[0], b_ln_w=dlw[0], b_ln_b=dlb[0], b_pw_w=dpw, b_pw_b=dpb[0],
        c_conv_w=dccw, c_a_log=dalog[0, :cfg.CH], c_dt_bias=ddtb[0, :cfg.CH], c_onorm_w=donw[0], w_out=dwo)
    return dx, grads, (got, mine, got_mine)


def rope_for(cfg, positions):
    n = positions.size
    inv_freq = ROPE_THETA ** (-np.arange(0, ROT_DIM, 2, dtype=np.float32) / ROT_DIM)
    freq_row = np.zeros((1, A_HEAD), np.float32)
    freq_row[0, :ROT_DIM] = np.concatenate([inv_freq, inv_freq])
    return rope_tables("rope_tables", positions.reshape(n, 1), jnp.asarray(freq_row))


def local_step(cfg, x, positions, prm, wps, wos, target):
    nseq = x.shape[0]
    n = nseq * cfg.T
    cos, sin = rope_for(cfg, positions)
    lps = [_layer_params(cfg, {k: v[l] for k, v in prm.items()}) for l in range(DEPTH)]
    saved = []
    xl = x.reshape(n, cfg.D)
    for l in range(DEPTH):
        xl, sv, _ = layer_forward(cfg, l, xl, lps[l], wps[l], wos[l], cos, sin,
                                  target=target.reshape(n, cfg.D) if l == DEPTH - 1 else None)
        saved.append(sv)
    dx, loss = xl
    grads = [None] * DEPTH
    for l in reversed(range(DEPTH)):
        groups = {k: wps[l][:, off:off + wd] for k, (off, wd) in cfg.g.items()}
        wb = chip_blocks(cfg, groups, N_CHIPS)
        dx, grads[l], _ = layer_backward(cfg, l, dx, saved[l], lps[l], wb, wos[l], cos, sin)
    return loss, dx.reshape(x.shape), grads


N_CHIPS = 4
N_DEV = 8


def _place():
    return lax.axis_index("x"), lax.axis_index("y"), lax.axis_index("c")


def _other_chips(x, y):
    return [(1 - x, y), (x, 1 - y), (1 - x, 1 - y)]


def _remote(src, dst, send, recv, to):
    return pltpu.make_async_remote_copy(src_ref=src, dst_ref=dst, send_sem=send, recv_sem=recv, device_id=to,
                                        device_id_type=MESH)


def gather_comm(arrs):
    n = len(arrs)

    def half(c):
        return [pl.ds(c * (a.shape[0] // 2), a.shape[0] // 2) for a in arrs]

    def first_copies(ins, outs, send, recv):
        x, y, c = _place()
        me = 2 * x + y
        mine = half(c)
        return [_remote(ins[i].at[mine[i]], outs[i].at[me, mine[i]], send.at[i, j], recv.at[i, j], (cx, cy, c))
                for i in range(n) for j, (cx, cy) in enumerate(_other_chips(x, y))]

    def start(ins, outs, sems):
        for cp in first_copies(ins, outs, *sems):
            cp.start()

    def finish(ins, outs, sems):
        send, recv = sems
        x, y, c = _place()
        chips = _other_chips(x, y)
        sib = (x, y, 1 - c)
        passed = []
        mine, other = half(c), half(1 - c)
        for i in range(n):
            for j, (cx, cy) in enumerate(chips):
                blk = outs[i].at[2 * cx + cy, mine[i]]
                _remote(blk, blk, send.at[i, j], recv.at[i, j], (x, y, c)).wait_recv()
                cp = _remote(blk, blk, send.at[i, 3 + j], recv.at[i, 3 + j], sib)
                cp.start()
                passed.append(cp)
        for i in range(n):
            for j, (cx, cy) in enumerate(chips):
                blk = outs[i].at[2 * cx + cy, other[i]]
                _remote(blk, blk, send.at[i, 3 + j], recv.at[i, 3 + j], sib).wait_recv()
        for cp in first_copies(ins, outs, send, recv) + passed:
            cp.wait_send()

    return Comm(arrs, [jax.ShapeDtypeStruct((N_CHIPS,) + a.shape, a.dtype) for a in arrs],
                [pltpu.SemaphoreType.DMA((n, 6)), pltpu.SemaphoreType.DMA((n, 6))], start, finish)


def fill_own(gathered, arrs):
    me = 2 * lax.axis_index("x") + lax.axis_index("y")
    return [lax.dynamic_update_index_in_dim(o, a, me, 0) for o, a in zip(gathered, arrs)]


def swap_comm(arrs):
    n = len(arrs)

    def copies(ins, outs, send, recv):
        x, y, c = _place()
        return [_remote(ins[i].at[:, 1 - c], outs[i], send.at[i], recv.at[i], (x, y, 1 - c)) for i in range(n)]

    def start(ins, outs, sems):
        for cp in copies(ins, outs, *sems):
            cp.start()

    def finish(ins, outs, sems):
        for cp in copies(ins, outs, *sems):
            cp.wait()

    return Comm(arrs, [jax.ShapeDtypeStruct((a.shape[0],) + a.shape[2:], a.dtype) for a in arrs],
                [pltpu.SemaphoreType.DMA((n,)), pltpu.SemaphoreType.DMA((n,))], start, finish)


def scatter_comm(arrs):
    n = len(arrs)

    def copies(ins, outs, send, recv):
        x, y, c = _place()
        return [_remote(ins[i].at[2 * cx + cy], outs[i].at[j], send.at[i, j], recv.at[i, j], (cx, cy, c))
                for i in range(n) for j, (cx, cy) in enumerate(_other_chips(x, y))]

    def start(ins, outs, sems):
        for cp in copies(ins, outs, *sems):
            cp.start()

    def finish(ins, outs, sems):
        send, recv = sems
        x, y, c = _place()
        for i in range(n):
            for j in range(3):
                blk = outs[i].at[j]
                _remote(blk, blk, send.at[i, j], recv.at[i, j], (x, y, c)).wait_recv()
        for cp in copies(ins, outs, send, recv):
            cp.wait_send()

    return Comm(arrs, [jax.ShapeDtypeStruct((3,) + a.shape[1:], a.dtype) for a in arrs],
                [pltpu.SemaphoreType.DMA((n, 3)), pltpu.SemaphoreType.DMA((n, 3))], start, finish)


def share_comm(arrs):
    n = len(arrs)

    def copies(outs, send, recv):
        x, y, c = _place()
        return [_remote(outs[i].at[c], outs[i].at[c], send.at[i], recv.at[i], (x, y, 1 - c)) for i in range(n)]

    def start(ins, outs, sems):
        for cp in copies(outs, *sems):
            cp.start()

    def finish(ins, outs, sems):
        send, recv = sems
        x, y, c = _place()
        for i in range(n):
            blk = outs[i].at[1 - c]
            _remote(blk, blk, send.at[i], recv.at[i], (x, y, c)).wait_recv()
        for cp in copies(outs, send, recv):
            cp.wait_send()

    return Comm(arrs, [jax.ShapeDtypeStruct(a.shape, a.dtype) for a in arrs],
                [pltpu.SemaphoreType.DMA((n,)), pltpu.SemaphoreType.DMA((n,))], start, finish,
                aliases={i: i for i in range(n)})


def all_reduce_small(name, packed):
    r = packed.shape[0]

    def body(in_ref, out_ref, buf, send, recv):
        x, y, c = _place()
        me = 4 * x + 2 * y + c
        buf[me] = in_ref[...]
        flips = [(fx, fy, fc) for fx in (0, 1) for fy in (0, 1) for fc in (0, 1) if (fx, fy, fc) != (0, 0, 0)]
        peers = [((x + fx) % 2, (y + fy) % 2, (c + fc) % 2) for fx, fy, fc in flips]
        cps = [_remote(in_ref, buf.at[me], send.at[k], recv.at[k], peer) for k, peer in enumerate(peers)]
        for cp in cps:
            cp.start()
        for k, (px, py, pc) in enumerate(peers):
            blk = buf.at[4 * px + 2 * py + pc]
            _remote(blk, blk, send.at[k], recv.at[k], (x, y, c)).wait_recv()
        for cp in cps:
            cp.wait_send()
        acc = buf[0]
        for d in range(1, N_DEV):
            acc = acc + buf[d]
        out_ref[...] = acc

    vm = pl.BlockSpec(memory_space=pltpu.VMEM)
    return pl.pallas_call(
        body, name=name, in_specs=[vm], out_specs=vm, out_shape=jax.ShapeDtypeStruct(packed.shape, F32),
        scratch_shapes=[pltpu.VMEM((N_DEV, r, LANE), F32), pltpu.SemaphoreType.DMA((N_DEV - 1,)),
                        pltpu.SemaphoreType.DMA((N_DEV - 1,))],
    )(packed)


def add_own_half(name, g, a, c_idx, tr):
    nch, _, r, cc = g.shape
    tr = min(tr, r)

    def body(c_ref, g_ref, a_ref, o_ref):
        o_ref[...] = (g_ref[0] + a_ref[...]).astype(o_ref.dtype)

    return pl.pallas_call(
        body, name=name,
        grid_spec=pltpu.PrefetchScalarGridSpec(
            num_scalar_prefetch=1, grid=(nch, r // tr),
            in_specs=[pl.BlockSpec((1, 1, tr, cc), lambda j, i, c_ref: (j, c_ref[0], i, 0)),
                      pl.BlockSpec((1, tr, cc), lambda j, i, c_ref: (j, i, 0))],
            out_specs=pl.BlockSpec((1, tr, cc), lambda j, i, c_ref: (j, i, 0))),
        out_shape=jax.ShapeDtypeStruct(a.shape, BF16),
        compiler_params=_cparams(("parallel", "parallel")),
    )(c_idx, g, a)


def sum_chips(name, p, b, idx, tr):
    _, r, cc = p.shape
    tr = min(tr, r)

    def body(idx_ref, p_ref, b_ref, o_ref):
        acc = p_ref[0].astype(F32)
        for k in range(3):
            acc = acc + b_ref[k].astype(F32)
        o_ref[0] = acc

    return pl.pallas_call(
        body, name=name,
        grid_spec=pltpu.PrefetchScalarGridSpec(
            num_scalar_prefetch=1, grid=(r // tr,),
            in_specs=[pl.BlockSpec((1, tr, cc), lambda i, s: (s[0], i, 0)),
                      pl.BlockSpec((3, tr, cc), lambda i, s: (0, i, 0))],
            out_specs=pl.BlockSpec((1, tr, cc), lambda i, s: (s[1], i, 0))),
        out_shape=jax.ShapeDtypeStruct((2, r, cc), F32),
        compiler_params=_cparams(("parallel",)),
    )(idx, p, b)


class GradReduce:
    def __init__(self, tag, parts, chip, c_idx):
        self.tag, self.c_idx = tag, c_idx
        self.parts = [p.reshape(p.shape[0], 2, p.shape[1] // 2, p.shape[2]) for p in parts]
        self.idx = jnp.concatenate([chip.astype(jnp.int32).reshape(1), c_idx])

    def swap(self):
        return swap_comm(self.parts)

    def add(self, received):
        self.part = [add_own_half(f"rs{self.tag}_add_sibling_{t}", g, a, self.c_idx, 128)
                     for t, (g, a) in enumerate(zip(self.parts, received))]

    def scatter(self, which=None):
        return scatter_comm(self.part if which is None else [self.part[t] for t in which])

    def finish(self, got):
        red = [sum_chips(f"rs{self.tag}_sum_chips_{t}", p, b, self.idx, 128) for t, (p, b) in enumerate(zip(self.part, got))]
        out = run_comm(f"rs{self.tag}_share_halves", share_comm(red))
        return [o.reshape(-1, o.shape[-1]) for o in out]


def adamw_many(name, ws, gs, ms, vs):
    n = len(ws)

    def body(*refs):
        for i in range(n):
            w_ref, g_ref, m_ref, v_ref = (refs[k * n + i] for k in range(4))
            d_ref, mo_ref, vo_ref = (refs[(4 + k) * n + i] for k in range(3))
            g = g_ref[...]
            m = ADAM_B1 * m_ref[...] + (1.0 - ADAM_B1) * g
            v = ADAM_B2 * v_ref[...] + (1.0 - ADAM_B2) * jnp.square(g)
            m_hat = m / (1.0 - ADAM_B1 ** ADAM_STEP)
            v_hat = v / (1.0 - ADAM_B2 ** ADAM_STEP)
            d_ref[...] = -ADAM_LR * (m_hat / (jnp.sqrt(v_hat) + ADAM_EPS) + ADAM_WD * w_ref[...])
            mo_ref[...] = m
            vo_ref[...] = v

    vm = pl.BlockSpec(memory_space=pltpu.VMEM)
    return pl.pallas_call(
        body, name=name, in_specs=[vm] * (4 * n), out_specs=[vm] * (3 * n),
        out_shape=[jax.ShapeDtypeStruct(a.shape, F32) for a in ws] * 3,
    )(*ws, *gs, *ms, *vs)


def adamw_layers(name, w, g0, g1, m, v, tb):
    _, r, cc = w.shape
    tb = min(tb, r)
    nb = r // tb

    def body(w_ref, g0_ref, g1_ref, m_ref, v_ref, g_ref, d_ref, mo_ref, vo_ref):
        g = jnp.where(pl.program_id(0) == 0, g0_ref[...], g1_ref[...])
        m = ADAM_B1 * m_ref[0] + (1.0 - ADAM_B1) * g
        v = ADAM_B2 * v_ref[0] + (1.0 - ADAM_B2) * jnp.square(g)
        m_hat = m / (1.0 - ADAM_B1 ** ADAM_STEP)
        v_hat = v / (1.0 - ADAM_B2 ** ADAM_STEP)
        g_ref[0] = g
        d_ref[0] = -ADAM_LR * (m_hat / (jnp.sqrt(v_hat) + ADAM_EPS) + ADAM_WD * w_ref[0])
        mo_ref[0] = m
        vo_ref[0] = v

    spec = pl.BlockSpec((1, tb, cc), lambda l, i: (l, i, 0))
    g0_spec = pl.BlockSpec((tb, cc), lambda l, i: (jnp.where(l == 0, i, nb - 1), 0))
    g1_spec = pl.BlockSpec((tb, cc), lambda l, i: (jnp.where(l == 1, i, 0), 0))
    return pl.pallas_call(
        body, name=name, grid=(2, nb), in_specs=[spec, g0_spec, g1_spec, spec, spec], out_specs=[spec] * 4,
        out_shape=[jax.ShapeDtypeStruct(w.shape, F32)] * 4,
        compiler_params=_cparams(("arbitrary", "arbitrary")),
    )(w, g0, g1, m, v)


def adamw_cols_major(name, w, g0, g1, m, v, tb=LANE):
    wt, mt, vt = (jnp.transpose(a, (2, 0, 1)) for a in (w, m, v))
    cc, _, r = wt.shape

    def body(w_ref, g0_ref, g1_ref, m_ref, v_ref, g_ref, d_ref, mo_ref, vo_ref):
        for l, gl_ref in enumerate((g0_ref, g1_ref)):
            g = gl_ref[...].T
            m = ADAM_B1 * m_ref[:, l, :] + (1.0 - ADAM_B1) * g
            v = ADAM_B2 * v_ref[:, l, :] + (1.0 - ADAM_B2) * jnp.square(g)
            m_hat = m / (1.0 - ADAM_B1 ** ADAM_STEP)
            v_hat = v / (1.0 - ADAM_B2 ** ADAM_STEP)
            g_ref[:, l, :] = g
            d_ref[:, l, :] = -ADAM_LR * (m_hat / (jnp.sqrt(v_hat) + ADAM_EPS) + ADAM_WD * w_ref[:, l, :])
            mo_ref[:, l, :] = m
            vo_ref[:, l, :] = v

    spec = pl.BlockSpec((tb, 2, r), lambda i: (i, 0, 0))
    gspec = pl.BlockSpec((r, tb), lambda i: (0, i))
    outs = pl.pallas_call(
        body, name=name, grid=(pl.cdiv(cc, tb),), in_specs=[spec, gspec, gspec, spec, spec], out_specs=[spec] * 4,
        out_shape=[jax.ShapeDtypeStruct(wt.shape, F32)] * 4,
        compiler_params=_cparams(("parallel",)),
    )(wt, g0, g1, mt, vt)
    return [jnp.transpose(o, (1, 2, 0)) for o in outs]


def _pack(arrs):
    flat = jnp.concatenate([a.reshape(-1).astype(F32) for a in arrs])
    pad = (-flat.shape[0]) % (8 * LANE)
    return jnp.pad(flat, (0, pad)).reshape(-1, LANE)


def _unpack(packed, shapes):
    flat = packed.reshape(-1)
    out, off = [], 0
    for s in shapes:
        size = math.prod(s)
        out.append(flat[off:off + size].reshape(s))
        off += size
    return out


BIG = ('w_in', 'w_out', 'b_pw_w')
SMALL = tuple(k for k in WEIGHTS if k not in BIG)
CHIP_SHARDED_SMALL = {'b_conv_w': 2, 'c_conv_w': 2}


def kernel(x, positions, norm_w, w_in, q_norm_w, k_norm_w, sinks, b_conv_w, b_conv_b, b_ln_w, b_ln_b, b_pw_w, b_pw_b, c_conv_w, c_a_log, c_dt_bias, c_onorm_w, w_out, loss_target, m_norm_w, m_w_in, m_q_norm_w, m_k_norm_w, m_sinks, m_b_conv_w, m_b_conv_b, m_b_ln_w, m_b_ln_b, m_b_pw_w, m_b_pw_b, m_c_conv_w, m_c_a_log, m_c_dt_bias, m_c_onorm_w, m_w_out, v_norm_w, v_w_in, v_q_norm_w, v_k_norm_w, v_sinks, v_b_conv_w, v_b_conv_b, v_b_ln_w, v_b_ln_b, v_b_pw_w, v_b_pw_b, v_c_conv_w, v_c_a_log, v_c_dt_bias, v_c_onorm_w, v_w_out):
    cfg = Cfg(x.shape[-1], x.shape[-2])
    w = dict(norm_w=norm_w, w_in=w_in, q_norm_w=q_norm_w, k_norm_w=k_norm_w, sinks=sinks, b_conv_w=b_conv_w,
             b_conv_b=b_conv_b, b_ln_w=b_ln_w, b_ln_b=b_ln_b, b_pw_w=b_pw_w, b_pw_b=b_pw_b, c_conv_w=c_conv_w,
             c_a_log=c_a_log, c_dt_bias=c_dt_bias, c_onorm_w=c_onorm_w, w_out=w_out)
    m = dict(norm_w=m_norm_w, w_in=m_w_in, q_norm_w=m_q_norm_w, k_norm_w=m_k_norm_w, sinks=m_sinks,
             b_conv_w=m_b_conv_w, b_conv_b=m_b_conv_b, b_ln_w=m_b_ln_w, b_ln_b=m_b_ln_b, b_pw_w=m_b_pw_w,
             b_pw_b=m_b_pw_b, c_conv_w=m_c_conv_w, c_a_log=m_c_a_log, c_dt_bias=m_c_dt_bias, c_onorm_w=m_c_onorm_w,
             w_out=m_w_out)
    v = dict(norm_w=v_norm_w, w_in=v_w_in, q_norm_w=v_q_norm_w, k_norm_w=v_k_norm_w, sinks=v_sinks,
             b_conv_w=v_b_conv_w, b_conv_b=v_b_conv_b, b_ln_w=v_b_ln_w, b_ln_b=v_b_ln_b, b_pw_w=v_b_pw_w,
             b_pw_b=v_b_pw_b, c_conv_w=v_c_conv_w, c_a_log=v_c_a_log, c_dt_bias=v_c_dt_bias, c_onorm_w=v_c_onorm_w,
             w_out=v_w_out)
    chip = 2 * lax.axis_index("x") + lax.axis_index("y")
    c_idx = lax.axis_index("c").astype(jnp.int32).reshape(1)
    D, T = cfg.D, cfg.T
    nseq = x.shape[0]
    n = nseq * T
    w_in_b, w_out_b = w_in.astype(BF16), w_out.astype(BF16)

    def permuted(g_in):
        return permute_w_in(cfg, jnp.concatenate(list(g_in), axis=1))

    def layer_prm(l, g_pw, g_bcw, g_ccw):
        prm = {k: w[k][l] for k in SMALL}
        prm['b_pw_w'] = g_pw.reshape(cfg.BW, cfg.BW)
        prm['b_conv_w'] = jnp.concatenate(list(g_bcw[:, l]), axis=1)
        prm['c_conv_w'] = jnp.concatenate(list(g_ccw[:, l]), axis=1)
        return _layer_params(cfg, prm)

    (g_in0,) = fill_own(run_comm("gather_weights_0", gather_comm([w_in_b[0]])), [w_in_b[0]])
    cos, sin = rope_for(cfg, positions)
    early = [w_out_b[0], b_pw_w[0], b_conv_w, c_conv_w]
    top, bottom = w_in_b[1][:D // 2], w_in_b[1][D // 2:]
    late = [w_out_b[1], b_pw_w[1]]
    conv_ws = {}

    def layer0_rest(rode):
        g_out0, g_pw0, conv_ws['b'], conv_ws['c'] = fill_own(rode, early)
        return layer_prm(0, g_pw0, conv_ws['b'], conv_ws['c']), g_out0.reshape(D, D)

    x1, sv0, rode = layer_forward(
        cfg, 0, x.reshape(n, D), (norm_w[0].reshape(1, -1), layer0_rest), permuted(g_in0), None, cos, sin,
        comms=dict(in_proj=gather_comm(early), attn=gather_comm([top]), conv=gather_comm(late),
                   intra=gather_comm([bottom])))
    lp0, wo0 = sv0['lp'], sv0['wo']
    (g_top,), (g_bottom,) = fill_own(rode['attn'], [top]), fill_own(rode['intra'], [bottom])
    g_out1, g_pw1 = fill_own(rode['conv'], late)
    g_in1 = jnp.concatenate([g_top, g_bottom], axis=1)
    wp1, wo1 = permuted(g_in1), g_out1.reshape(D, D)
    lp1 = layer_prm(1, g_pw1, conv_ws['b'], conv_ws['c'])
    (dx2, loss_local), sv1, _ = layer_forward(cfg, 1, x1, lp1, wp1, wo1, cos, sin, target=loss_target.reshape(n, D))

    def partials(gr):
        return [gr['w_in_blocks'], gr['w_out'].reshape(N_CHIPS, D // N_CHIPS, D),
                gr['b_pw_w'].reshape(N_CHIPS, cfg.BW // N_CHIPS, cfg.BW)]

    dx1, gr1, (_, rs1, _) = layer_backward(cfg, 1, dx2, sv1, lp1, g_in1, wo1, cos, sin,
                                        own_rs=lambda gr: GradReduce(1, partials(gr), chip, c_idx))
    dx0, gr0, (got1, rs0, got0) = layer_backward(cfg, 0, dx1, sv0, lp0, g_in0, wo0, cos, sin, rs=rs1,
                                           own_rs=lambda gr: GradReduce(0, partials(gr), chip, c_idx))
    red1 = rs1.finish(got1)
    red0 = rs0.finish(got0)
    grad_x = dx0.reshape(x.shape)
    grads = [gr0, gr1]

    small_parts = [_stack([grads[l][k] for l in range(DEPTH)]) for k in SMALL] + [loss_local.reshape(1)]
    *small_red, loss = _unpack(all_reduce_small("all_reduce_small", _pack(small_parts)), [a.shape for a in small_parts])
    loss = loss.reshape(())
    g = {}
    for k, a in zip(SMALL, small_red):
        if k in CHIP_SHARDED_SMALL:
            ax = CHIP_SHARDED_SMALL[k]
            width = a.shape[ax] // N_CHIPS
            a = lax.dynamic_slice_in_dim(a, chip * width, width, axis=ax)
        g[k] = a

    delta, new_m, new_v = {}, {}, {}
    for k, g0, g1 in zip(BIG, red0, red1):
        update = adamw_layers if w[k].shape[-1] % LANE == 0 else adamw_cols_major
        g[k], delta[k], new_m[k], new_v[k] = update(f"adamw_{k}", w[k], g0, g1, m[k], v[k], 128)
    outs = adamw_many("adamw_small", *[[d[k] for k in SMALL] for d in (w, g, m, v)])
    for i, k in enumerate(SMALL):
        delta[k], new_m[k], new_v[k] = outs[i], outs[len(SMALL) + i], outs[2 * len(SMALL) + i]
    return (loss, grad_x, *[g[k] for k in WEIGHTS], *[delta[k] for k in WEIGHTS], *[new_m[k] for k in WEIGHTS],
            *[new_v[k] for k in WEIGHTS])
```

```python
import functools
import math

import numpy as np
import jax
import jax.numpy as jnp
from jax import lax
from jax.experimental import pallas as pl
from jax.experimental.pallas import tpu as pltpu

F32 = jnp.float32
BF16 = jnp.bfloat16
HI = lax.Precision.HIGHEST
MESH = pl.DeviceIdType.MESH

DEPTH = 2
A_HEAD = 64
A_GROUP = 3
ATTN_BLOCK = 128
ROT_DIM = 16
ROPE_THETA = 500000.0
B_CONV = 31
B_HALO = 32
C_HEAD = 128
C_CONV = 4
C_HALO = 8
CHUNK = 64
EPS = 1e-6
LANE = 128

ADAM_LR = 0.001
ADAM_B1 = 0.9
ADAM_B2 = 0.999
ADAM_EPS = 1e-08
ADAM_WD = 0.01
ADAM_STEP = 10

VMEM_LIMIT = 56 * 1024 * 1024

WEIGHTS = ['norm_w', 'w_in', 'q_norm_w', 'k_norm_w', 'sinks', 'b_conv_w', 'b_conv_b', 'b_ln_w', 'b_ln_b',
           'b_pw_w', 'b_pw_b', 'c_conv_w', 'c_a_log', 'c_dt_bias', 'c_onorm_w', 'w_out']


class Cfg:
    def __init__(self, d_model=2048, seq=2048):
        self.D = d_model
        self.T = seq
        self.AW = 3 * d_model // 8
        self.AQH = self.AW // A_HEAD
        self.AKH = self.AQH // A_GROUP
        self.AKW = self.AKH * A_HEAD
        self.BW = d_model // 4
        self.CH = (d_model - self.AW - self.BW) // C_HEAD
        self.CW = self.CH * C_HEAD
        AW, AKW, BW, CW, CH = self.AW, self.AKW, self.BW, self.CW, self.CH
        orig = [('qa', AW), ('ka', AKW), ('va', AKW), ('za', AW), ('ub', 2 * BW), ('zb', BW),
                ('qc', CW), ('kc', CW), ('vc', CW), ('bc', CH), ('ac', CH), ('zc', CW)]
        self.orig = {}
        off = 0
        for n, w in orig:
            self.orig[n] = (off, w)
            off += w
        self.IN_COLS = off
        order = ['qa', 'za', 'qc', 'kc', 'vc', 'zc', 'ka', 'va', 'ub', 'zb', 'bc', 'ac']
        self.order = order
        self.g = {}
        off = 0
        for n in order:
            w = self.orig[n][1]
            wp = LANE if n in ('bc', 'ac') else w
            assert off % wp == 0, (n, off, wp)
            self.g[n] = (off, wp)
            off += wp
        self.WP = off

    def blk(self, name):
        off, w = self.g[name]
        return off // w


def _cparams(sem, vmem=VMEM_LIMIT):
    return pltpu.CompilerParams(dimension_semantics=sem, vmem_limit_bytes=vmem)


def _silu(x):
    return x * jax.nn.sigmoid(x)


ANY = pl.BlockSpec(memory_space=pl.ANY)


class Comm:
    def __init__(self, ins, out_shapes, sems, start, finish, aliases=None):
        self.ins, self.out_shapes, self.sems = list(ins), list(out_shapes), list(sems)
        self.start, self.finish, self.aliases = start, finish, dict(aliases or {})


def call_with_comm(body, name, grid, in_specs, out_specs, out_shape, scratch_shapes, semantics, args, comm=None):
    in_specs, out_specs, out_shape, scratch_shapes = list(in_specs), list(out_specs), list(out_shape), list(scratch_shapes)
    if comm is None:
        outs = pl.pallas_call(body, name=name, grid=grid, in_specs=in_specs, out_specs=out_specs, out_shape=out_shape,
                              scratch_shapes=scratch_shapes, compiler_params=_cparams(semantics))(*args)
        return list(outs), []
    ni, no, ns = len(in_specs), len(out_specs), len(scratch_shapes)
    nci, nco = len(comm.ins), len(comm.out_shapes)

    def wrapped(*refs):
        h_in, c_in = refs[:ni], refs[ni:ni + nci]
        h_out, c_out = refs[ni + nci:ni + nci + no], refs[ni + nci + no:ni + nci + no + nco]
        h_scr, c_sems = refs[ni + nci + no + nco:ni + nci + no + nco + ns], refs[ni + nci + no + nco + ns:]
        ids = [pl.program_id(d) for d in range(len(grid))]
        first = functools.reduce(jnp.logical_and, [i == 0 for i in ids])
        last = functools.reduce(jnp.logical_and, [i == g - 1 for i, g in zip(ids, grid)])

        @pl.when(first)
        def _():
            comm.start(c_in, c_out, c_sems)

        body(*h_in, *h_out, *h_scr)

        @pl.when(last)
        def _():
            comm.finish(c_in, c_out, c_sems)

    outs = pl.pallas_call(
        wrapped, name=name, grid=grid, in_specs=in_specs + [ANY] * nci, out_specs=out_specs + [ANY] * nco,
        out_shape=out_shape + comm.out_shapes, scratch_shapes=scratch_shapes + comm.sems,
        input_output_aliases={ni + k: no + v for k, v in comm.aliases.items()},
        compiler_params=_cparams(("arbitrary",) * len(grid)),
    )(*args, *comm.ins)
    return list(outs[:no]), list(outs[no:])


def run_comm(name, comm):
    nci, nco = len(comm.ins), len(comm.out_shapes)

    def body(*refs):
        c_in, c_out, c_sems = refs[:nci], refs[nci:nci + nco], refs[nci + nco:]
        comm.start(c_in, c_out, c_sems)
        comm.finish(c_in, c_out, c_sems)

    return pl.pallas_call(
        body, name=name, in_specs=[ANY] * nci, out_specs=[ANY] * nco, out_shape=comm.out_shapes,
        scratch_shapes=comm.sems, input_output_aliases=comm.aliases,
    )(*comm.ins)


def _stack(xs, axis=0):
    assert axis == 0
    return jnp.concatenate([x[None] for x in xs], axis=0)


def _bdot(a, b, ca, cb, precision=HI):
    dims = (((ca,), (cb,)), ((0,), (0,)))
    if precision is HI and a.dtype == F32:
        ah = a.astype(BF16)
        bh = b.astype(BF16)
        al = (a - ah.astype(F32)).astype(BF16)
        bl = (b - bh.astype(F32)).astype(BF16)
        dg = lambda p, q: lax.dot_general(p, q, dims, preferred_element_type=F32)
        return dg(ah, bh) + (dg(ah, bl) + dg(al, bh))
    return lax.dot_general(a, b, dims, precision=precision, preferred_element_type=F32)


def _rope_matrix(nb):
    i = lax.broadcasted_iota(jnp.int32, (nb, A_HEAD, A_HEAD), 1)
    j = lax.broadcasted_iota(jnp.int32, (nb, A_HEAD, A_HEAD), 2)
    half = ROT_DIM // 2
    neg = (j < half) & (i == j + half)
    pos = (j >= half) & (j < ROT_DIM) & (i == j - half)
    return jnp.where(neg, -1.0, jnp.where(pos, 1.0, 0.0)).astype(F32)


def _norm_rope(xh, w, cos, sin):
    y = xh * lax.rsqrt(jnp.mean(xh * xh, axis=-1, keepdims=True) + EPS) * w
    return y * cos + _bdot(y, _rope_matrix(xh.shape[0]), 2, 1) * sin


def _sink_probs(s, sink):
    m = jnp.maximum(jnp.max(s, axis=-1, keepdims=True), sink)
    e = jnp.exp(s - m)
    es = jnp.exp(sink - m)
    den = jnp.sum(e, axis=-1, keepdims=True) + es
    return e / den, es / den


@jax.custom_vjp
def _sink_softmax(s, sink):
    return _sink_probs(s, sink)[0]


def _sink_softmax_fwd(s, sink):
    p, ps = _sink_probs(s, sink)
    return p, (p, ps)


def _sink_softmax_bwd(res, dp):
    p, ps = res
    inner = jnp.sum(p * dp, axis=-1, keepdims=True)
    return p * (dp - inner), -jnp.sum(ps * inner, axis=1, keepdims=True)


_sink_softmax.defvjp(_sink_softmax_fwd, _sink_softmax_bwd)


def attn_block(cfg, first, q, za, kc, vc, cosc, sinc, kp, vp, cosp, sinp, qnw, knw, sinks_row):
    blk = ATTN_BLOCK
    nq, nk = cfg.AQH, cfg.AKH
    qi = lax.broadcasted_iota(jnp.int32, (blk, 2 * blk), 0)
    kj = lax.broadcasted_iota(jnp.int32, (blk, 2 * blk), 1)
    dist = qi + blk - kj
    valid = ((dist >= 0) & (dist < blk) & (jnp.logical_not(first) | (kj >= blk)))[None]
    cos2 = jnp.concatenate([cosp, cosc], axis=0)
    sin2 = jnp.concatenate([sinp, sinc], axis=0)
    head = lambda x, h: x[:, A_HEAD * h:A_HEAD * (h + 1)]
    k2 = _stack([jnp.concatenate([head(kp, h), head(kc, h)], axis=0) for h in range(nk)], axis=0)
    v2 = _stack([jnp.concatenate([head(vp, h), head(vc, h)], axis=0) for h in range(nk)], axis=0)
    k2 = _norm_rope(k2, knw[None], cos2[None], sin2[None]).astype(BF16)
    v2 = v2.astype(BF16)
    k2 = _stack([k2[h // A_GROUP] for h in range(nq)], axis=0)
    v2 = _stack([v2[h // A_GROUP] for h in range(nq)], axis=0)
    qh = _stack([head(q, h) for h in range(nq)], axis=0)
    qh = _norm_rope(qh, qnw[None], cosc[None], sinc[None]).astype(BF16)
    s = _bdot(qh, k2, 2, 2, None) * (A_HEAD ** -0.5)
    s = jnp.where(valid, s, -1e30)
    sink = _stack([sinks_row[:, A_HEAD * h:A_HEAD * h + 1] for h in range(nq)], axis=0)
    o = _bdot(_sink_softmax(s, sink).astype(BF16), v2, 2, 1, None)
    return (jnp.concatenate([o[h] for h in range(nq)], axis=1) * _silu(za),)


def conv_block(cfg, first, u, zb, uh, cw, cb, lw, lb, pw, pb):
    BW = cfg.BW
    tb = u.shape[0]
    uu = jnp.concatenate([uh, u], axis=0)
    h = uu[:, :BW] * jax.nn.sigmoid(uu[:, BW:])
    row = lax.broadcasted_iota(jnp.int32, h.shape, 0)
    h = jnp.where(first & (row < B_HALO), 0.0, h)
    acc = jnp.zeros((tb, BW), F32) + cb
    base = B_HALO - (B_CONV - 1)
    for k in range(B_CONV):
        acc = acc + cw[k:k + 1, :] * h[base + k:base + k + tb, :]
    mu = jnp.mean(acc, axis=-1, keepdims=True)
    var = jnp.mean(jnp.square(acc - mu), axis=-1, keepdims=True)
    y = (acc - mu) * lax.rsqrt(var + EPS) * lw + lb
    s = _silu(y)
    o = jnp.dot(s.astype(BF16), pw.astype(BF16), preferred_element_type=F32) + pb
    return (o * _silu(zb),)


def gdn_prep_block(cfg, first, xq, xk, xv, braw, araw, hq, hk, hv, cw, alog, dtb):
    CW = cfg.CW
    tb = xq.shape[0]
    outs = []
    for idx, (x, xh) in enumerate(((xq, hq), (xk, hk), (xv, hv))):
        xx = jnp.concatenate([jnp.where(first, 0.0, xh), x], axis=0)
        w = cw[:, idx * CW:(idx + 1) * CW]
        acc = jnp.zeros((tb, CW), F32)
        base = C_HALO - (C_CONV - 1)
        for k in range(C_CONV):
            acc = acc + w[k:k + 1, :] * xx[base + k:base + k + tb, :]
        y = _silu(acc)
        if idx < 2:
            parts = []
            for h in range(cfg.CH):
                yh = y[:, C_HEAD * h:C_HEAD * (h + 1)]
                parts.append(yh * lax.rsqrt(jnp.sum(yh * yh, axis=-1, keepdims=True) + EPS))
            y = jnp.concatenate(parts, axis=1)
        outs.append(y)
    beta = jax.nn.sigmoid(braw)
    g = -jnp.exp(alog) * jax.nn.softplus(araw + dtb)
    return outs[0], outs[1], outs[2], g, beta


def _inverse_unit_lower(low, eye):
    pw = low
    inv = eye - low
    for _ in range(5):
        pwb = pw.astype(BF16)
        pw = _bdot(pwb, pwb, 2, 1, None)
        inv = inv + _bdot(inv.astype(BF16), pw.astype(BF16), 2, 1, None)
    ax = inv + _bdot(low, inv, 2, 1)
    return inv + _bdot(inv, eye - ax, 2, 1)


@jax.custom_vjp
def _saved_inverse(low, inv):
    return inv


def _saved_inverse_fwd(low, inv):
    return inv, inv


def _saved_inverse_bwd(inv, d):
    dlow = -_bdot(_bdot(inv, d, 1, 1), inv, 2, 2)
    return dlow, jnp.zeros_like(inv)


_saved_inverse.defvjp(_saved_inverse_fwd, _saved_inverse_bwd)


def gdn_intra_rows(cfg, first, qn, kn, v, g, beta, inv_saved=None):
    c = CHUNK
    CH = cfg.CH
    nchunk = qn.shape[0] // c
    i = lax.broadcasted_iota(jnp.int32, (c, c), 0)
    j = lax.broadcasted_iota(jnp.int32, (c, c), 1)
    incl = (i >= j)[None]
    strict = (i > j)[None]
    eye = (i == j).astype(F32)[None]
    tri = (i >= j).astype(F32)
    rows = [slice(c * ci, c * (ci + 1)) for ci in range(nchunk)]
    gcs = [jnp.dot(tri, g[r], precision=HI, preferred_element_type=F32) for r in rows]
    pairs = [(ci, h) for ci in range(nchunk) for h in range(CH)]
    heads = lambda x, wd: _stack([x[rows[ci], wd * h:wd * (h + 1)] for ci, h in pairs], axis=0)
    gch = _stack([gcs[ci][:, h:h + 1] for ci, h in pairs], axis=0)
    bh = _stack([beta[rows[ci], h:h + 1] for ci, h in pairs], axis=0)
    q = heads(qn, C_HEAD) * (C_HEAD ** -0.5)
    k = heads(kn, C_HEAD)
    vv = heads(v, C_HEAD)
    a = jnp.broadcast_to(gch, (len(pairs), c, c))
    diff = jnp.where(incl, a - jnp.swapaxes(a, 1, 2), 0.0)
    decay = jnp.where(incl, jnp.exp(diff), 0.0)
    kb = k * bh
    low = jnp.where(strict, _bdot(kb, k, 2, 2) * decay, 0.0)
    if inv_saved is None:
        inv = _inverse_unit_lower(low, eye)
    else:
        inv = _saved_inverse(low, heads(inv_saved, c))
    eg = jnp.exp(gch)
    sol = _bdot(inv, jnp.concatenate([vv * bh, kb * eg], axis=2), 2, 1)
    intra = jnp.where(incl, _bdot(q, k, 2, 2) * decay, 0.0)
    qg = q * eg
    kd = k * jnp.exp(gch[:, c - 1:c, :] - gch)
    glast = jnp.concatenate([jnp.broadcast_to(gc[c - 1:c, :], gc.shape) for gc in gcs], axis=0)

    def unstack(x):
        return jnp.concatenate([jnp.concatenate([x[ci * CH + h] for h in range(CH)], axis=1) for ci in range(nchunk)],
                               axis=0)

    outs = (unstack(sol[:, :, :C_HEAD]), unstack(sol[:, :, C_HEAD:]), unstack(qg), unstack(kd), unstack(intra), glast)
    return outs + (unstack(inv),) if inv_saved is None else outs


def gdn_state_step(S, u, w, qg, kd, intra, glast):
    v_new = u - _bdot(w, S, 2, 1)
    o = _bdot(qg, S, 2, 1) + _bdot(intra, v_new, 2, 1)
    S_next = S * jnp.exp(glast) + _bdot(kd, v_new, 1, 1)
    return o, S_next


def gdn_out_block(cfg, first, o, zc, onw):
    parts = []
    for h in range(cfg.CH):
        sl = slice(C_HEAD * h, C_HEAD * (h + 1))
        oh = o[:, sl]
        y = oh * lax.rsqrt(jnp.mean(oh * oh, axis=-1, keepdims=True) + EPS) * onw
        parts.append(y * _silu(zc[:, sl]))
    return (jnp.concatenate(parts, axis=1),)


def rms_block(x, nw):
    return x * lax.rsqrt(jnp.mean(x * x, axis=-1, keepdims=True) + EPS) * nw


class Row:
    def __init__(self, arr, width, colblk=0, grad=None):
        self.arr, self.width, self.colblk, self.grad = arr, width, colblk, grad


class Halo:
    def __init__(self, arr, width, colblk, hr, tie=None):
        self.arr, self.width, self.colblk, self.hr, self.tie = arr, width, colblk, hr, tie


def _row_specs(tb, rows, halos, params, pos):
    specs = [pl.BlockSpec((tb, r.width), lambda i, cb=r.colblk: (pos(i), cb)) for r in rows]
    specs += [pl.BlockSpec((h.hr, h.width),
                           lambda i, cb=h.colblk, m=tb // h.hr: (jnp.maximum(pos(i) * m - 1, 0), cb))
              for h in halos]
    specs += [pl.BlockSpec(p.shape, lambda i: (0, 0)) for p in params]
    return specs


def rb_fwd(name, fn, n, tb, bps, rows, halos, params, outs, comm=None):
    nr, nh, npar = len(rows), len(halos), len(params)

    def body(*refs):
        ins = refs[:nr + nh + npar]
        o_refs = refs[nr + nh + npar:]
        first = (pl.program_id(0) % bps) == 0
        res = fn(first, *[r[...] for r in ins])
        for ref, val in zip(o_refs, res):
            ref[...] = val.astype(ref.dtype)

    res, carried = call_with_comm(
        body, name, (n // tb,), _row_specs(tb, rows, halos, params, lambda i: i),
        [pl.BlockSpec((tb, w), lambda i: (i, 0)) for w, _ in outs],
        [jax.ShapeDtypeStruct((n, w), dt) for w, dt in outs], [], ("parallel",),
        [r.arr for r in rows] + [h.arr for h in halos] + list(params), comm)
    return (res, carried) if comm is not None else res


def rb_bwd(name, fn, n, tb, bps, rows, halos, params, douts, param_grads, comm=None):
    nr, nh, npar, nd = len(rows), len(halos), len(params), len(douts)
    nblk = n // tb
    grow = [k for k, r in enumerate(rows) if r.grad is not None]
    ghalo = [k for k, h in enumerate(halos) if h.tie is not None]
    gpar = [k for k, f in enumerate(param_grads) if f]
    pos = lambda i: nblk - 1 - i

    def body(*refs):
        ins = refs[:nr + nh + npar]
        d_refs = refs[nr + nh + npar:nr + nh + npar + nd]
        rest = refs[nr + nh + npar + nd:]
        grow_refs = rest[:len(grow)]
        gpar_refs = rest[len(grow):len(grow) + len(gpar)]
        carry_refs = rest[len(grow) + len(gpar):]
        i = pl.program_id(0)
        first = (pos(i) % bps) == 0
        vals = [r[...] for r in ins]
        diff_idx = grow + [nr + k for k in ghalo] + [nr + nh + k for k in gpar]

        def f(*dargs):
            full = list(vals)
            for k, a in zip(diff_idx, dargs):
                full[k] = a
            return fn(first, *full)

        res, vjp = jax.vjp(f, *[vals[k] for k in diff_idx])
        grads = vjp(tuple(d[...].astype(r.dtype) for d, r in zip(d_refs, res)))
        g_rows = list(grads[:len(grow)])
        g_halos = grads[len(grow):len(grow) + len(ghalo)]
        g_pars = grads[len(grow) + len(ghalo):]

        @pl.when(i == 0)
        def _():
            for c in carry_refs:
                c[...] = jnp.zeros_like(c)
            for p in gpar_refs:
                p[...] = jnp.zeros_like(p)

        for k, ref in enumerate(grow_refs):
            ref[...] = g_rows[k].astype(ref.dtype)
        for ci, hk in enumerate(ghalo):
            h = halos[hk]
            k = grow.index(h.tie)
            tail = g_rows[k][tb - h.hr:, :] + carry_refs[ci][...]
            grow_refs[k][tb - h.hr:, :] = tail.astype(grow_refs[k].dtype)
            carry_refs[ci][...] = g_halos[ci]
        for ref, gp in zip(gpar_refs, g_pars):
            ref[...] += gp

    out_specs = [pl.BlockSpec((tb, rows[k].width), lambda i: (pos(i), 0)) for k in grow]
    out_specs += [pl.BlockSpec(params[k].shape, lambda i: (0, 0)) for k in gpar]
    out_shape = [jax.ShapeDtypeStruct((n, rows[k].width), rows[k].grad) for k in grow]
    out_shape += [jax.ShapeDtypeStruct(params[k].shape, F32) for k in gpar]
    in_specs = _row_specs(tb, rows, halos, params, pos)
    in_specs += [pl.BlockSpec((tb, d.shape[1]), lambda i: (pos(i), 0)) for d in douts]
    res, carried = call_with_comm(
        body, name, (nblk,), in_specs, out_specs, out_shape,
        [pltpu.VMEM((halos[k].hr, halos[k].width), F32) for k in ghalo], ("arbitrary",),
        [r.arr for r in rows] + [h.arr for h in halos] + list(params) + list(douts), comm)
    return (res, carried) if comm is not None else res


_DIMS = {'nn': (((1,), (0,)), ((), ())), 'nt': (((1,), (1,)), ((), ())), 'tn': (((0,), (0,)), ((), ()))}


def matmul(name, a, b, mode, tm, tn, tk, out_dtype=F32, add=None, comm=None):
    if mode == 'tn':
        K, M = a.shape
    else:
        M, K = a.shape
    N = b.shape[0] if mode == 'nt' else b.shape[1]
    tm, tn, tk = min(tm, M), min(tn, N), min(tk, K)
    assert M % tm == 0 and N % tn == 0 and K % tk == 0, (name, M, N, K, tm, tn, tk)
    nk = K // tk
    a_spec = pl.BlockSpec((tk, tm), lambda i, j, k: (k, i)) if mode == 'tn' else pl.BlockSpec((tm, tk), lambda i, j, k: (i, k))
    b_spec = pl.BlockSpec((tn, tk), lambda i, j, k: (j, k)) if mode == 'nt' else pl.BlockSpec((tk, tn), lambda i, j, k: (k, j))
    o_spec = pl.BlockSpec((tm, tn), lambda i, j, k: (i, j))
    has_add = add is not None

    def body(*refs):
        a_ref, b_ref = refs[0], refs[1]
        add_ref = refs[2] if has_add else None
        o_ref = refs[-1]
        k = pl.program_id(2)
        part = lax.dot_general(a_ref[...].astype(BF16), b_ref[...].astype(BF16), _DIMS[mode], preferred_element_type=F32)

        @pl.when(k == 0)
        def _():
            o_ref[...] = ((part + add_ref[...]) if has_add else part).astype(o_ref.dtype)

        if nk > 1:
            @pl.when(k > 0)
            def _():
                o_ref[...] += part

    assert nk == 1 or out_dtype == F32
    ins = [a, b] + ([add] if has_add else [])
    in_specs = [a_spec, b_spec] + ([o_spec] if has_add else [])
    outs, couts = call_with_comm(body, name, (M // tm, N // tn, nk), in_specs, [o_spec],
                                 [jax.ShapeDtypeStruct((M, N), out_dtype)], [], ("parallel", "parallel", "arbitrary"),
                                 ins, comm)
    return (outs[0], couts) if comm is not None else outs[0]


def grad_w_blocks(name, h, dpb, tm, tk):
    n, d = h.shape
    nb, _, s = dpb.shape
    tm, tk = min(tm, d), min(tk, n)
    assert d % tm == 0 and n % tk == 0
    nk = n // tk

    def body(h_ref, b_ref, o_ref):
        k = pl.program_id(2)
        part = lax.dot_general(h_ref[...], b_ref[0], _DIMS['tn'], preferred_element_type=F32)

        @pl.when(k == 0)
        def _():
            o_ref[0] = part

        if nk > 1:
            @pl.when(k > 0)
            def _():
                o_ref[0] += part

    return pl.pallas_call(
        body, name=name, grid=(nb, d // tm, nk),
        in_specs=[pl.BlockSpec((tk, tm), lambda j, i, k: (k, i)), pl.BlockSpec((1, tk, s), lambda j, i, k: (j, k, 0))],
        out_specs=pl.BlockSpec((1, tm, s), lambda j, i, k: (j, i, 0)),
        out_shape=jax.ShapeDtypeStruct((nb, d, s), F32),
        compiler_params=_cparams(("parallel", "parallel", "arbitrary")),
    )(h, dpb)


def grad_h_blocks(name, dpb, wb, tm, tn, comm=None):
    nb, n, s = dpb.shape
    d = wb.shape[1]
    tm, tn = min(tm, n), min(tn, d)
    assert n % tm == 0 and d % tn == 0

    def body(a_ref, b_ref, o_ref):
        k = pl.program_id(2)
        part = lax.dot_general(a_ref[0], b_ref[0], _DIMS['nt'], preferred_element_type=F32)

        @pl.when(k == 0)
        def _():
            o_ref[...] = part

        @pl.when(k > 0)
        def _():
            o_ref[...] += part

    outs, carried = call_with_comm(
        body, name, (n // tm, d // tn, nb),
        [pl.BlockSpec((1, tm, s), lambda i, j, k: (k, i, 0)), pl.BlockSpec((1, tn, s), lambda i, j, k: (k, j, 0))],
        [pl.BlockSpec((tm, tn), lambda i, j, k: (i, j))], [jax.ShapeDtypeStruct((n, d), F32)], [],
        ("parallel", "parallel", "arbitrary"), [dpb, wb], comm)
    return (outs[0], carried) if comm is not None else outs[0]


def norm_in_proj(name, x, nw, wp, tm, tn, comm=None):
    n, d = x.shape
    wpc = wp.shape[1]
    tm, tn = min(tm, n), min(tn, wpc)
    assert n % tm == 0 and wpc % tn == 0

    def body(x_ref, nw_ref, w_ref, p_ref, h_ref):
        @pl.when(pl.program_id(1) == 0)
        def _():
            h_ref[...] = rms_block(x_ref[...], nw_ref[...]).astype(BF16)

        p_ref[...] = jnp.dot(h_ref[...], w_ref[...], preferred_element_type=F32)

    outs, couts = call_with_comm(
        body, name, (n // tm, wpc // tn),
        [pl.BlockSpec((tm, d), lambda i, j: (i, 0)), pl.BlockSpec((1, d), lambda i, j: (0, 0)),
         pl.BlockSpec((d, tn), lambda i, j: (0, j))],
        [pl.BlockSpec((tm, tn), lambda i, j: (i, j)), pl.BlockSpec((tm, d), lambda i, j: (i, 0))],
        [jax.ShapeDtypeStruct((n, wpc), F32), jax.ShapeDtypeStruct((n, d), BF16)], [], ("parallel", "arbitrary"),
        [x, nw, wp], comm)
    return (outs[0], outs[1], couts) if comm is not None else (outs[0], outs[1])


def norm_bwd(name, x, nw, dh, dres, tb, comm=None):
    n, d = x.shape
    tb = min(tb, n)

    def body(x_ref, nw_ref, dh_ref, dres_ref, dx_ref, dnw_ref):
        @pl.when(pl.program_id(0) == 0)
        def _():
            dnw_ref[...] = jnp.zeros_like(dnw_ref)

        _, vjp = jax.vjp(rms_block, x_ref[...], nw_ref[...])
        dx, dnw = vjp(dh_ref[...])
        dx_ref[...] = dx + dres_ref[...]
        dnw_ref[...] += dnw

    row = pl.BlockSpec((tb, d), lambda i: (i, 0))
    par = pl.BlockSpec((1, d), lambda i: (0, 0))
    outs, carried = call_with_comm(
        body, name, (n // tb,), [row, par, row, row], [row, par],
        [jax.ShapeDtypeStruct((n, d), F32), jax.ShapeDtypeStruct((1, d), F32)], [], ("arbitrary",),
        [x, nw, dh, dres], comm)
    return (outs[0], outs[1], carried) if comm is not None else (outs[0], outs[1])


def out_proj_loss(name, y, wo, x, target, tm):
    n, d = x.shape
    tm = min(tm, n)
    assert n % tm == 0

    def body(y_ref, w_ref, x_ref, t_ref, dz_ref, loss_ref):
        @pl.when(pl.program_id(0) == 0)
        def _():
            loss_ref[...] = jnp.zeros_like(loss_ref)

        z = x_ref[...] + jnp.dot(y_ref[...], w_ref[...], preferred_element_type=F32)
        err = z - t_ref[...]
        dz_ref[...] = err * (1.0 / d)
        part = 0.5 * jnp.sum(jnp.mean(err * err, axis=-1, keepdims=True), axis=0, keepdims=True)
        loss_ref[...] += jnp.broadcast_to(part, loss_ref.shape)

    row = pl.BlockSpec((tm, d), lambda i: (i, 0))
    dz, loss = pl.pallas_call(
        body, name=name, grid=(n // tm,),
        in_specs=[pl.BlockSpec((tm, y.shape[1]), lambda i: (i, 0)), pl.BlockSpec(wo.shape, lambda i: (0, 0)), row, row],
        out_specs=[row, pl.BlockSpec((8, LANE), lambda i: (0, 0))],
        out_shape=[jax.ShapeDtypeStruct((n, d), F32), jax.ShapeDtypeStruct((8, LANE), F32)],
        compiler_params=_cparams(("arbitrary",)),
    )(y, wo, x, target)
    return dz, loss[0, 0]


def rope_tables(name, pos_col, inv_freq_row):
    n = pos_col.shape[0]

    def body(p_ref, f_ref, c_ref, s_ref):
        ang = p_ref[...].astype(F32) * f_ref[...]
        lane = lax.broadcasted_iota(jnp.int32, ang.shape, 1)
        c_ref[...] = jnp.where(lane < ROT_DIM, jnp.cos(ang), 1.0)
        s_ref[...] = jnp.where(lane < ROT_DIM, jnp.sin(ang), 0.0)

    return pl.pallas_call(
        body, name=name, out_shape=[jax.ShapeDtypeStruct((n, A_HEAD), F32)] * 2,
    )(pos_col, inv_freq_row)


def _scan_operands(cfg, nseq, u_ref, w_ref, qg_ref, kd_ref, a_ref, gl_ref):
    pairs = [(b, h) for b in range(nseq) for h in range(cfg.CH)]
    st = lambda r, wd: _stack([r[b, :, wd * h:wd * (h + 1)] for b, h in pairs], axis=0)
    gl = _stack([gl_ref[b, 0:1, h:h + 1] for b, h in pairs], axis=0)
    return st(u_ref, C_HEAD), st(w_ref, C_HEAD), st(qg_ref, C_HEAD), st(kd_ref, C_HEAD), st(a_ref, CHUNK), gl


def gdn_scan_fwd(name, cfg, nseq, u, w, qg, kd, intra, glast):
    CH, CW, T = cfg.CH, cfg.CW, cfg.T
    nc = T // CHUNK

    def body(u_ref, w_ref, qg_ref, kd_ref, a_ref, gl_ref, o_ref, sin_ref, s_ref):
        @pl.when(pl.program_id(0) == 0)
        def _():
            s_ref[...] = jnp.zeros_like(s_ref)

        S = s_ref[...]
        for b in range(nseq):
            sin_ref[b, 0] = S[b * CH:(b + 1) * CH]
        o, S_next = gdn_state_step(S, *_scan_operands(cfg, nseq, u_ref, w_ref, qg_ref, kd_ref, a_ref, gl_ref))
        s_ref[...] = S_next
        for b in range(nseq):
            o_ref[b] = jnp.concatenate([o[b * CH + h] for h in range(CH)], axis=1)

    row = lambda wd: pl.BlockSpec((nseq, CHUNK, wd), lambda c: (0, c, 0))
    widths = [CW, CW, CW, CW, CH * CHUNK, LANE]
    o, s_in = pl.pallas_call(
        body, name=name, grid=(nc,),
        in_specs=[row(x) for x in widths],
        out_specs=[row(CW), pl.BlockSpec((nseq, 1, CH, C_HEAD, C_HEAD), lambda c: (0, c, 0, 0, 0))],
        out_shape=[jax.ShapeDtypeStruct((nseq, T, CW), F32),
                   jax.ShapeDtypeStruct((nseq, nc, CH, C_HEAD, C_HEAD), F32)],
        scratch_shapes=[pltpu.VMEM((nseq * CH, C_HEAD, C_HEAD), F32)],
        compiler_params=_cparams(("arbitrary",)),
    )(*[a.reshape(nseq, T, a.shape[1]) for a in (u, w, qg, kd, intra, glast)])
    return o.reshape(nseq * T, CW), s_in


def gdn_scan_bwd(name, cfg, nseq, u, w, qg, kd, intra, glast, s_in, do, comm=None):
    CH, CW, T = cfg.CH, cfg.CW, cfg.T
    nc = T // CHUNK

    def body(u_ref, w_ref, qg_ref, kd_ref, a_ref, gl_ref, sin_ref, do_ref,
             du_ref, dw_ref, dqg_ref, dkd_ref, da_ref, dgl_ref, ds_ref):
        @pl.when(pl.program_id(0) == 0)
        def _():
            ds_ref[...] = jnp.zeros_like(ds_ref)

        S = jnp.concatenate([sin_ref[b, 0] for b in range(nseq)], axis=0)
        dout = _stack([do_ref[b, :, C_HEAD * h:C_HEAD * (h + 1)] for b in range(nseq) for h in range(CH)], axis=0)
        _, vjp = jax.vjp(gdn_state_step, S, *_scan_operands(cfg, nseq, u_ref, w_ref, qg_ref, kd_ref, a_ref, gl_ref))
        dS, du, dw, dqg, dkd, da, dg = vjp((dout, ds_ref[...]))
        ds_ref[...] = dS
        lane = lax.broadcasted_iota(jnp.int32, (CHUNK, LANE), 1)
        rowi = lax.broadcasted_iota(jnp.int32, (CHUNK, LANE), 0)
        for b in range(nseq):
            cat = lambda x: jnp.concatenate([x[b * CH + h] for h in range(CH)], axis=1)
            du_ref[b] = cat(du)
            dw_ref[b] = cat(dw)
            dqg_ref[b] = cat(dqg)
            dkd_ref[b] = cat(dkd)
            da_ref[b] = cat(da)
            dgl = jnp.zeros((CHUNK, LANE), F32)
            for h in range(CH):
                dgl = dgl + jnp.where((lane == h) & (rowi == 0), dg[b * CH + h], 0.0)
            dgl_ref[b] = dgl

    row = lambda wd: pl.BlockSpec((nseq, CHUNK, wd), lambda c: (0, nc - 1 - c, 0))
    widths = [CW, CW, CW, CW, CH * CHUNK, LANE]
    outs, carried = call_with_comm(
        body, name, (nc,),
        [row(x) for x in widths]
        + [pl.BlockSpec((nseq, 1, CH, C_HEAD, C_HEAD), lambda c: (0, nc - 1 - c, 0, 0, 0)), row(CW)],
        [row(x) for x in widths], [jax.ShapeDtypeStruct((nseq, T, x), F32) for x in widths],
        [pltpu.VMEM((nseq * CH, C_HEAD, C_HEAD), F32)], ("arbitrary",),
        [a.reshape(nseq, T, a.shape[1]) for a in (u, w, qg, kd, intra, glast)] + [s_in, do.reshape(nseq, T, CW)], comm)
    return [o.reshape(nseq * T, o.shape[2]) for o in outs], carried


def _tile(total, cap, unit=LANE):
    best = None
    for t in range(unit, min(cap, total) + 1, unit):
        if total % t == 0:
            best = t
    assert best is not None, (total, cap, unit)
    return best


def _pad_lanes(v, width=LANE):
    return jnp.pad(v.reshape(1, -1), ((0, 0), (0, width - v.shape[-1])))


def permute_w_in(cfg, w):
    parts = []
    for n in cfg.order:
        off, wd = cfg.orig[n]
        blk = w[:, off:off + wd]
        if cfg.g[n][1] != wd:
            blk = jnp.pad(blk, ((0, 0), (0, cfg.g[n][1] - wd)))
        parts.append(blk)
    return jnp.concatenate(parts, axis=1)


def chip_blocks(cfg, groups, n_chips):
    s = cfg.IN_COLS // n_chips
    blocks = []
    for j in range(n_chips):
        lo, hi = j * s, (j + 1) * s
        pieces = []
        for name, (off, wd) in cfg.orig.items():
            a, b = max(lo, off), min(hi, off + wd)
            if a < b:
                pieces.append(groups[name][:, a - off:b - off])
        blocks.append(jnp.concatenate(pieces, axis=1))
    return jnp.stack(blocks)


def _layer_params(cfg, prm):
    return dict(
        nw=prm['norm_w'].reshape(1, -1),
        qnw=prm['q_norm_w'].reshape(1, -1), knw=prm['k_norm_w'].reshape(1, -1),
        sinks_row=jnp.repeat(prm['sinks'], A_HEAD).reshape(1, -1),
        cw=prm['b_conv_w'], cb=prm['b_conv_b'].reshape(1, -1),
        lw=prm['b_ln_w'].reshape(1, -1), lb=prm['b_ln_b'].reshape(1, -1),
        pw=prm['b_pw_w'], pb=prm['b_pw_b'].reshape(1, -1),
        ccw=prm['c_conv_w'], alog=_pad_lanes(prm['c_a_log']), dtb=_pad_lanes(prm['c_dt_bias']),
        onw=prm['c_onorm_w'].reshape(1, -1),
    )


def _attn_io(cfg, p, cos, sin, grads):
    gq = BF16 if grads else None
    rows = [Row(p, cfg.AW, cfg.blk('qa'), gq), Row(p, cfg.AW, cfg.blk('za'), gq),
            Row(p, cfg.AKW, cfg.blk('ka'), gq), Row(p, cfg.AKW, cfg.blk('va'), gq),
            Row(cos, A_HEAD), Row(sin, A_HEAD)]
    halos = [Halo(p, cfg.AKW, cfg.blk('ka'), ATTN_BLOCK, 2 if grads else None),
             Halo(p, cfg.AKW, cfg.blk('va'), ATTN_BLOCK, 3 if grads else None),
             Halo(cos, A_HEAD, 0, ATTN_BLOCK), Halo(sin, A_HEAD, 0, ATTN_BLOCK)]
    return rows, halos


def _conv_io(cfg, p, grads):
    gq = BF16 if grads else None
    rows = [Row(p, 2 * cfg.BW, cfg.blk('ub'), gq), Row(p, cfg.BW, cfg.blk('zb'), gq)]
    halos = [Halo(p, 2 * cfg.BW, cfg.blk('ub'), B_HALO, 0 if grads else None)]
    return rows, halos


def _prep_io(cfg, p, grads):
    gq = BF16 if grads else None
    rows = [Row(p, cfg.CW, cfg.blk(n), gq) for n in ('qc', 'kc', 'vc')]
    rows += [Row(p, LANE, cfg.blk('bc'), gq), Row(p, LANE, cfg.blk('ac'), gq)]
    halos = [Halo(p, cfg.CW, cfg.blk(n), C_HALO, k if grads else None) for k, n in enumerate(('qc', 'kc', 'vc'))]
    return rows, halos


TB_CONV = 128
TB_PREP = 256
TB_OUT = 512
TB_INTRA_FWD = 256
TB_INTRA_BWD = 256


def layer_forward(cfg, l, x, lp, wp, wo, cos, sin, comms=None, target=None):
    n = x.shape[0]
    nseq = n // cfg.T
    T = cfg.T
    comms = comms or {}
    carried = {}

    def hosted(key, res):
        if comms.get(key) is None:
            return res
        res, carried[key] = res
        return res

    nw = lp['nw'] if isinstance(lp, dict) else lp[0]
    p, h, *rode = norm_in_proj(f"in_proj_{l}", x, nw, wp, 1024, _tile(cfg.WP, 768), comm=comms.get('in_proj'))
    if rode:
        carried['in_proj'] = rode[0]
    if not isinstance(lp, dict):
        lp, wo = lp[1](carried['in_proj'])
    rows, halos = _attn_io(cfg, p, cos, sin, False)
    (oa,) = hosted('attn', rb_fwd(f"attn_fwd_{l}", functools.partial(attn_block, cfg), n, ATTN_BLOCK, T // ATTN_BLOCK,
                                  rows, halos, [lp['qnw'], lp['knw'], lp['sinks_row']], [(cfg.AW, BF16)],
                                  comm=comms.get('attn')))
    rows, halos = _conv_io(cfg, p, False)
    tbb = min(TB_CONV, T)
    (ob,) = hosted('conv', rb_fwd(f"conv_fwd_{l}", functools.partial(conv_block, cfg), n, tbb, T // tbb, rows, halos,
                                  [lp['cw'], lp['cb'], lp['lw'], lp['lb'], lp['pw'], lp['pb']], [(cfg.BW, BF16)],
                                  comm=comms.get('conv')))
    rows, halos = _prep_io(cfg, p, False)
    tbp = min(TB_PREP, T)
    qn, kn, v, g, beta = rb_fwd(f"gdn_prep_fwd_{l}", functools.partial(gdn_prep_block, cfg), n, tbp, T // tbp, rows,
                                halos, [lp['ccw'], lp['alog'], lp['dtb']],
                                [(cfg.CW, F32)] * 3 + [(LANE, F32)] * 2)
    intra_outs = hosted('intra', rb_fwd(
        f"gdn_intra_fwd_{l}", functools.partial(gdn_intra_rows, cfg), n, min(TB_INTRA_FWD, T), T // min(TB_INTRA_FWD, T),
        [Row(qn, cfg.CW), Row(kn, cfg.CW), Row(v, cfg.CW), Row(g, LANE), Row(beta, LANE)], [], [],
        [(cfg.CW, F32)] * 4 + [(cfg.CH * CHUNK, F32), (LANE, F32), (cfg.CH * CHUNK, F32)], comm=comms.get('intra')))
    intra_outs, inv = intra_outs[:6], intra_outs[6]
    o, s_in = gdn_scan_fwd(f"gdn_scan_fwd_{l}", cfg, nseq, *intra_outs)
    tbo = min(TB_OUT, T)
    (oc,) = rb_fwd(f"gdn_out_fwd_{l}", functools.partial(gdn_out_block, cfg), n, tbo, T // tbo,
                   [Row(o, cfg.CW), Row(p, cfg.CW, cfg.blk('zc'))], [], [lp['onw']], [(cfg.CW, BF16)])
    y = jnp.concatenate([oa, ob, oc], axis=1)
    if target is None:
        x_next = matmul(f"out_proj_{l}", y, wo, 'nn', 1024, 1024, cfg.D, add=x)
    else:
        x_next = out_proj_loss(f"out_proj_{l}", y, wo, x, target, 512)
    saved = dict(x=x, p=p, h=h, y=y, qn=qn, kn=kn, v=v, g=g, beta=beta, intra_outs=intra_outs, inv=inv, s_in=s_in, o=o,
                 lp=lp, wo=wo)
    return x_next, saved, carried


def layer_backward(cfg, l, dxn, sv, lp, wb, wo, cos, sin, rs=None, own_rs=None):
    n = dxn.shape[0]
    nseq = n // cfg.T
    T = cfg.T
    p = sv['p']
    AW, BW, CW = cfg.AW, cfg.BW, cfg.CW
    dy = matmul(f"dy_{l}", dxn, wo, 'nt', 1024, 1024, cfg.D)
    dwo = matmul(f"dwo_{l}", sv['y'], dxn, 'tn', 1024, 1024, 2048)
    doa, dob, doc = dy[:, :AW], dy[:, AW:AW + BW], dy[:, AW + BW:]
    tbo = min(TB_OUT, T)
    do, dzc, donw = rb_bwd(f"gdn_out_bwd_{l}", functools.partial(gdn_out_block, cfg), n, tbo, T // tbo,
                           [Row(sv['o'], CW, 0, F32), Row(p, CW, cfg.blk('zc'), BF16)], [], [lp['onw']], [doc], [True])
    dintra, got_rest = gdn_scan_bwd(f"gdn_scan_bwd_{l}", cfg, nseq, *sv['intra_outs'], sv['s_in'], do,
                                    comm=None if rs is None else rs.scatter([1, 2]))
    dqn, dkn, dv, dg, dbeta = rb_bwd(
        f"gdn_intra_bwd_{l}", functools.partial(gdn_intra_rows, cfg), n, min(TB_INTRA_BWD, T), T // min(TB_INTRA_BWD, T),
        [Row(sv['qn'], CW, 0, F32), Row(sv['kn'], CW, 0, F32), Row(sv['v'], CW, 0, F32), Row(sv['g'], LANE, 0, F32),
         Row(sv['beta'], LANE, 0, F32), Row(sv['inv'], cfg.CH * CHUNK)], [], [], list(dintra), [])
    rows, halos = _prep_io(cfg, p, True)
    tbp = min(TB_PREP, T)
    dqc, dkc, dvc, dbc, dac, dccw, dalog, ddtb = rb_bwd(
        f"gdn_prep_bwd_{l}", functools.partial(gdn_prep_block, cfg), n, tbp, T // tbp, rows, halos,
        [lp['ccw'], lp['alog'], lp['dtb']], [dqn, dkn, dv, dg, dbeta], [True] * 3)
    rows, halos = _conv_io(cfg, p, True)
    tbb = min(TB_CONV, T)
    conv_grads = rb_bwd(
        f"conv_bwd_{l}", functools.partial(conv_block, cfg), n, tbb, T // tbb, rows, halos,
        [lp['cw'], lp['cb'], lp['lw'], lp['lb'], lp['pw'], lp['pb']], [dob], [True] * 6,
        comm=None if rs is None else rs.scatter([0]))
    got = None
    if rs is not None:
        conv_grads, got_w_in = conv_grads
        got = got_w_in + got_rest
    dub, dzb, dcw, dcb, dlw, dlb, dpw, dpb = conv_grads
    rows, halos = _attn_io(cfg, p, cos, sin, True)
    dqa, dza, dka, dva, dqnw, dknw, dsinks_row = rb_bwd(
        f"attn_bwd_{l}", functools.partial(attn_block, cfg), n, ATTN_BLOCK, T // ATTN_BLOCK, rows, halos,
        [lp['qnw'], lp['knw'], lp['sinks_row']], [doa], [True] * 3)
    dgroups = dict(qa=dqa, za=dza, qc=dqc, kc=dkc, vc=dvc, zc=dzc, ka=dka, va=dva, ub=dub, zb=dzb, bc=dbc, ac=dac)
    dp_blocks = chip_blocks(cfg, dgroups, N_CHIPS)
    dw_in = grad_w_blocks(f"dwp_{l}", sv['h'], dp_blocks, 1024, 2048)
    mine = None if own_rs is None else own_rs(dict(w_in_blocks=dw_in, w_out=dwo, b_pw_w=dpw))
    got_mine = None
    if mine is None:
        dh = grad_h_blocks(f"dh_{l}", dp_blocks, wb, 1024, 1024)
        dx, dnw = norm_bwd(f"norm_bwd_{l}", sv['x'], lp['nw'], dh, dxn, 256)
    elif l > 0:
        dh, mine_received = grad_h_blocks(f"dh_{l}", dp_blocks, wb, 1024, 1024, comm=mine.swap())
        mine.add(mine_received)
        dx, dnw = norm_bwd(f"norm_bwd_{l}", sv['x'], lp['nw'], dh, dxn, 256)
    else:
        mine.add(run_comm(f"rs{l}_swap_halves", mine.swap()))
        dh, got_w_in = grad_h_blocks(f"dh_{l}", dp_blocks, wb, 1024, 1024, comm=mine.scatter([0]))
        dx, dnw, got_others = norm_bwd(f"norm_bwd_{l}", sv['x'], lp['nw'], dh, dxn, 256, comm=mine.scatter([1, 2]))
        got_mine = got_w_in + got_others
    grads = dict(
        norm_w=dnw[0], w_in_blocks=dw_in, q_norm_w=dqnw[0], k_norm_w=dknw[0],
        sinks=dsinks_row.reshape(cfg.AQH, A_HEAD)[:, 0],
        b_conv_w=dcw, b_conv_b=dcb[0], b_ln_w=dlw[0], b_ln_b=dlb[0], b_pw_w=dpw, b_pw_b=dpb[0],
        c_conv_w=dccw, c_a_log=dalog[0, :cfg.CH], c_dt_bias=ddtb[0, :cfg.CH], c_onorm_w=donw[0], w_out=dwo)
    return dx, grads, (got, mine, got_mine)


def rope_for(cfg, positions):
    n = positions.size
    inv_freq = ROPE_THETA ** (-np.arange(0, ROT_DIM, 2, dtype=np.float32) / ROT_DIM)
    freq_row = np.zeros((1, A_HEAD), np.float32)
    freq_row[0, :ROT_DIM] = np.concatenate([inv_freq, inv_freq])
    return rope_tables("rope_tables", positions.reshape(n, 1), jnp.asarray(freq_row))


def local_step(cfg, x, positions, prm, wps, wos, target):
    nseq = x.shape[0]
    n = nseq * cfg.T
    cos, sin = rope_for(cfg, positions)
    lps = [_layer_params(cfg, {k: v[l] for k, v in prm.items()}) for l in range(DEPTH)]
    saved = []
    xl = x.reshape(n, cfg.D)
    for l in range(DEPTH):
        xl, sv, _ = layer_forward(cfg, l, xl, lps[l], wps[l], wos[l], cos, sin,
                                  target=target.reshape(n, cfg.D) if l == DEPTH - 1 else None)
        saved.append(sv)
    dx, loss = xl
    grads = [None] * DEPTH
    for l in reversed(range(DEPTH)):
        groups = {k: wps[l][:, off:off + wd] for k, (off, wd) in cfg.g.items()}
        wb = chip_blocks(cfg, groups, N_CHIPS)
        dx, grads[l], _ = layer_backward(cfg, l, dx, saved[l], lps[l], wb, wos[l], cos, sin)
    return loss, dx.reshape(x.shape), grads


N_CHIPS = 4
N_DEV = 8


def _place():
    return lax.axis_index("x"), lax.axis_index("y"), lax.axis_index("c")


def _other_chips(x, y):
    return [(1 - x, y), (x, 1 - y), (1 - x, 1 - y)]


def _remote(src, dst, send, recv, to):
    return pltpu.make_async_remote_copy(src_ref=src, dst_ref=dst, send_sem=send, recv_sem=recv, device_id=to,
                                        device_id_type=MESH)


def gather_comm(arrs):
    n = len(arrs)

    def half(c):
        return [pl.ds(c * (a.shape[0] // 2), a.shape[0] // 2) for a in arrs]

    def first_copies(ins, outs, send, recv):
        x, y, c = _place()
        me = 2 * x + y
        mine = half(c)
        return [_remote(ins[i].at[mine[i]], outs[i].at[me, mine[i]], send.at[i, j], recv.at[i, j], (cx, cy, c))
                for i in range(n) for j, (cx, cy) in enumerate(_other_chips(x, y))]

    def start(ins, outs, sems):
        for cp in first_copies(ins, outs, *sems):
            cp.start()

    def finish(ins, outs, sems):
        send, recv = sems
        x, y, c = _place()
        chips = _other_chips(x, y)
        sib = (x, y, 1 - c)
        passed = []
        mine, other = half(c), half(1 - c)
        for i in range(n):
            for j, (cx, cy) in enumerate(chips):
                blk = outs[i].at[2 * cx + cy, mine[i]]
                _remote(blk, blk, send.at[i, j], recv.at[i, j], (x, y, c)).wait_recv()
                cp = _remote(blk, blk, send.at[i, 3 + j], recv.at[i, 3 + j], sib)
                cp.start()
                passed.append(cp)
        for i in range(n):
            for j, (cx, cy) in enumerate(chips):
                blk = outs[i].at[2 * cx + cy, other[i]]
                _remote(blk, blk, send.at[i, 3 + j], recv.at[i, 3 + j], sib).wait_recv()
        for cp in first_copies(ins, outs, send, recv) + passed:
            cp.wait_send()

    return Comm(arrs, [jax.ShapeDtypeStruct((N_CHIPS,) + a.shape, a.dtype) for a in arrs],
                [pltpu.SemaphoreType.DMA((n, 6)), pltpu.SemaphoreType.DMA((n, 6))], start, finish)


def fill_own(gathered, arrs):
    me = 2 * lax.axis_index("x") + lax.axis_index("y")
    return [lax.dynamic_update_index_in_dim(o, a, me, 0) for o, a in zip(gathered, arrs)]


def swap_comm(arrs):
    n = len(arrs)

    def copies(ins, outs, send, recv):
        x, y, c = _place()
        return [_remote(ins[i].at[:, 1 - c], outs[i], send.at[i], recv.at[i], (x, y, 1 - c)) for i in range(n)]

    def start(ins, outs, sems):
        for cp in copies(ins, outs, *sems):
            cp.start()

    def finish(ins, outs, sems):
        for cp in copies(ins, outs, *sems):
            cp.wait()

    return Comm(arrs, [jax.ShapeDtypeStruct((a.shape[0],) + a.shape[2:], a.dtype) for a in arrs],
                [pltpu.SemaphoreType.DMA((n,)), pltpu.SemaphoreType.DMA((n,))], start, finish)


def scatter_comm(arrs):
    n = len(arrs)

    def copies(ins, outs, send, recv):
        x, y, c = _place()
        return [_remote(ins[i].at[2 * cx + cy], outs[i].at[j], send.at[i, j], recv.at[i, j], (cx, cy, c))
                for i in range(n) for j, (cx, cy) in enumerate(_other_chips(x, y))]

    def start(ins, outs, sems):
        for cp in copies(ins, outs, *sems):
            cp.start()

    def finish(ins, outs, sems):
        send, recv = sems
        x, y, c = _place()
        for i in range(n):
            for j in range(3):
                blk = outs[i].at[j]
                _remote(blk, blk, send.at[i, j], recv.at[i, j], (x, y, c)).wait_recv()
        for cp in copies(ins, outs, send, recv):
            cp.wait_send()

    return Comm(arrs, [jax.ShapeDtypeStruct((3,) + a.shape[1:], a.dtype) for a in arrs],
                [pltpu.SemaphoreType.DMA((n, 3)), pltpu.SemaphoreType.DMA((n, 3))], start, finish)


def share_comm(arrs):
    n = len(arrs)

    def copies(outs, send, recv):
        x, y, c = _place()
        return [_remote(outs[i].at[c], outs[i].at[c], send.at[i], recv.at[i], (x, y, 1 - c)) for i in range(n)]

    def start(ins, outs, sems):
        for cp in copies(outs, *sems):
            cp.start()

    def finish(ins, outs, sems):
        send, recv = sems
        x, y, c = _place()
        for i in range(n):
            blk = outs[i].at[1 - c]
            _remote(blk, blk, send.at[i], recv.at[i], (x, y, c)).wait_recv()
        for cp in copies(outs, send, recv):
            cp.wait_send()

    return Comm(arrs, [jax.ShapeDtypeStruct(a.shape, a.dtype) for a in arrs],
                [pltpu.SemaphoreType.DMA((n,)), pltpu.SemaphoreType.DMA((n,))], start, finish,
                aliases={i: i for i in range(n)})


def all_reduce_small(name, packed):
    r = packed.shape[0]

    def body(in_ref, out_ref, buf, send, recv):
        x, y, c = _place()
        me = 4 * x + 2 * y + c
        buf[me] = in_ref[...]
        flips = [(fx, fy, fc) for fx in (0, 1) for fy in (0, 1) for fc in (0, 1) if (fx, fy, fc) != (0, 0, 0)]
        peers = [((x + fx) % 2, (y + fy) % 2, (c + fc) % 2) for fx, fy, fc in flips]
        cps = [_remote(in_ref, buf.at[me], send.at[k], recv.at[k], peer) for k, peer in enumerate(peers)]
        for cp in cps:
            cp.start()
        for k, (px, py, pc) in enumerate(peers):
            blk = buf.at[4 * px + 2 * py + pc]
            _remote(blk, blk, send.at[k], recv.at[k], (x, y, c)).wait_recv()
        for cp in cps:
            cp.wait_send()
        acc = buf[0]
        for d in range(1, N_DEV):
            acc = acc + buf[d]
        out_ref[...] = acc

    vm = pl.BlockSpec(memory_space=pltpu.VMEM)
    return pl.pallas_call(
        body, name=name, in_specs=[vm], out_specs=vm, out_shape=jax.ShapeDtypeStruct(packed.shape, F32),
        scratch_shapes=[pltpu.VMEM((N_DEV, r, LANE), F32), pltpu.SemaphoreType.DMA((N_DEV - 1,)),
                        pltpu.SemaphoreType.DMA((N_DEV - 1,))],
    )(packed)


def add_own_half(name, g, a, c_idx, tr):
    nch, _, r, cc = g.shape
    tr = min(tr, r)

    def body(c_ref, g_ref, a_ref, o_ref):
        o_ref[...] = (g_ref[0] + a_ref[...]).astype(o_ref.dtype)

    return pl.pallas_call(
        body, name=name,
        grid_spec=pltpu.PrefetchScalarGridSpec(
            num_scalar_prefetch=1, grid=(nch, r // tr),
            in_specs=[pl.BlockSpec((1, 1, tr, cc), lambda j, i, c_ref: (j, c_ref[0], i, 0)),
                      pl.BlockSpec((1, tr, cc), lambda j, i, c_ref: (j, i, 0))],
            out_specs=pl.BlockSpec((1, tr, cc), lambda j, i, c_ref: (j, i, 0))),
        out_shape=jax.ShapeDtypeStruct(a.shape, BF16),
        compiler_params=_cparams(("parallel", "parallel")),
    )(c_idx, g, a)


def sum_chips(name, p, b, idx, tr):
    _, r, cc = p.shape
    tr = min(tr, r)

    def body(idx_ref, p_ref, b_ref, o_ref):
        acc = p_ref[0].astype(F32)
        for k in range(3):
            acc = acc + b_ref[k].astype(F32)
        o_ref[0] = acc

    return pl.pallas_call(
        body, name=name,
        grid_spec=pltpu.PrefetchScalarGridSpec(
            num_scalar_prefetch=1, grid=(r // tr,),
            in_specs=[pl.BlockSpec((1, tr, cc), lambda i, s: (s[0], i, 0)),
                      pl.BlockSpec((3, tr, cc), lambda i, s: (0, i, 0))],
            out_specs=pl.BlockSpec((1, tr, cc), lambda i, s: (s[1], i, 0))),
        out_shape=jax.ShapeDtypeStruct((2, r, cc), F32),
        compiler_params=_cparams(("parallel",)),
    )(idx, p, b)


class GradReduce:
    def __init__(self, tag, parts, chip, c_idx):
        self.tag, self.c_idx = tag, c_idx
        self.parts = [p.reshape(p.shape[0], 2, p.shape[1] // 2, p.shape[2]) for p in parts]
        self.idx = jnp.concatenate([chip.astype(jnp.int32).reshape(1), c_idx])

    def swap(self):
        return swap_comm(self.parts)

    def add(self, received):
        self.part = [add_own_half(f"rs{self.tag}_add_sibling_{t}", g, a, self.c_idx, 128)
                     for t, (g, a) in enumerate(zip(self.parts, received))]

    def scatter(self, which=None):
        return scatter_comm(self.part if which is None else [self.part[t] for t in which])

    def finish(self, got):
        red = [sum_chips(f"rs{self.tag}_sum_chips_{t}", p, b, self.idx, 128) for t, (p, b) in enumerate(zip(self.part, got))]
        out = run_comm(f"rs{self.tag}_share_halves", share_comm(red))
        return [o.reshape(-1, o.shape[-1]) for o in out]


def adamw_many(name, ws, gs, ms, vs):
    n = len(ws)

    def body(*refs):
        for i in range(n):
            w_ref, g_ref, m_ref, v_ref = (refs[k * n + i] for k in range(4))
            d_ref, mo_ref, vo_ref = (refs[(4 + k) * n + i] for k in range(3))
            g = g_ref[...]
            m = ADAM_B1 * m_ref[...] + (1.0 - ADAM_B1) * g
            v = ADAM_B2 * v_ref[...] + (1.0 - ADAM_B2) * jnp.square(g)
            m_hat = m / (1.0 - ADAM_B1 ** ADAM_STEP)
            v_hat = v / (1.0 - ADAM_B2 ** ADAM_STEP)
            d_ref[...] = -ADAM_LR * (m_hat / (jnp.sqrt(v_hat) + ADAM_EPS) + ADAM_WD * w_ref[...])
            mo_ref[...] = m
            vo_ref[...] = v

    vm = pl.BlockSpec(memory_space=pltpu.VMEM)
    return pl.pallas_call(
        body, name=name, in_specs=[vm] * (4 * n), out_specs=[vm] * (3 * n),
        out_shape=[jax.ShapeDtypeStruct(a.shape, F32) for a in ws] * 3,
    )(*ws, *gs, *ms, *vs)


def adamw_layers(name, w, g0, g1, m, v, tb):
    _, r, cc = w.shape
    tb = min(tb, r)
    nb = r // tb

    def body(w_ref, g0_ref, g1_ref, m_ref, v_ref, g_ref, d_ref, mo_ref, vo_ref):
        g = jnp.where(pl.program_id(0) == 0, g0_ref[...], g1_ref[...])
        m = ADAM_B1 * m_ref[0] + (1.0 - ADAM_B1) * g
        v = ADAM_B2 * v_ref[0] + (1.0 - ADAM_B2) * jnp.square(g)
        m_hat = m / (1.0 - ADAM_B1 ** ADAM_STEP)
        v_hat = v / (1.0 - ADAM_B2 ** ADAM_STEP)
        g_ref[0] = g
        d_ref[0] = -ADAM_LR * (m_hat / (jnp.sqrt(v_hat) + ADAM_EPS) + ADAM_WD * w_ref[0])
        mo_ref[0] = m
        vo_ref[0] = v

    spec = pl.BlockSpec((1, tb, cc), lambda l, i: (l, i, 0))
    g0_spec = pl.BlockSpec((tb, cc), lambda l, i: (jnp.where(l == 0, i, nb - 1), 0))
    g1_spec = pl.BlockSpec((tb, cc), lambda l, i: (jnp.where(l == 1, i, 0), 0))
    return pl.pallas_call(
        body, name=name, grid=(2, nb), in_specs=[spec, g0_spec, g1_spec, spec, spec], out_specs=[spec] * 4,
        out_shape=[jax.ShapeDtypeStruct(w.shape, F32)] * 4,
        compiler_params=_cparams(("arbitrary", "arbitrary")),
    )(w, g0, g1, m, v)


def adamw_cols_major(name, w, g0, g1, m, v, tb=LANE):
    wt, mt, vt = (jnp.transpose(a, (2, 0, 1)) for a in (w, m, v))
    cc, _, r = wt.shape

    def body(w_ref, g0_ref, g1_ref, m_ref, v_ref, g_ref, d_ref, mo_ref, vo_ref):
        for l, gl_ref in enumerate((g0_ref, g1_ref)):
            g = gl_ref[...].T
            m = ADAM_B1 * m_ref[:, l, :] + (1.0 - ADAM_B1) * g
            v = ADAM_B2 * v_ref[:, l, :] + (1.0 - ADAM_B2) * jnp.square(g)
            m_hat = m / (1.0 - ADAM_B1 ** ADAM_STEP)
            v_hat = v / (1.0 - ADAM_B2 ** ADAM_STEP)
            g_ref[:, l, :] = g
            d_ref[:, l, :] = -ADAM_LR * (m_hat / (jnp.sqrt(v_hat) + ADAM_EPS) + ADAM_WD * w_ref[:, l, :])
            mo_ref[:, l, :] = m
            vo_ref[:, l, :] = v

    spec = pl.BlockSpec((tb, 2, r), lambda i: (i, 0, 0))
    gspec = pl.BlockSpec((r, tb), lambda i: (0, i))
    outs = pl.pallas_call(
        body, name=name, grid=(pl.cdiv(cc, tb),), in_specs=[spec, gspec, gspec, spec, spec], out_specs=[spec] * 4,
        out_shape=[jax.ShapeDtypeStruct(wt.shape, F32)] * 4,
        compiler_params=_cparams(("parallel",)),
    )(wt, g0, g1, mt, vt)
    return [jnp.transpose(o, (1, 2, 0)) for o in outs]


def _pack(arrs):
    flat = jnp.concatenate([a.reshape(-1).astype(F32) for a in arrs])
    pad = (-flat.shape[0]) % (8 * LANE)
    return jnp.pad(flat, (0, pad)).reshape(-1, LANE)


def _unpack(packed, shapes):
    flat = packed.reshape(-1)
    out, off = [], 0
    for s in shapes:
        size = math.prod(s)
        out.append(flat[off:off + size].reshape(s))
        off += size
    return out


BIG = ('w_in', 'w_out', 'b_pw_w')
SMALL = tuple(k for k in WEIGHTS if k not in BIG)
CHIP_SHARDED_SMALL = {'b_conv_w': 2, 'c_conv_w': 2}


def kernel(x, positions, norm_w, w_in, q_norm_w, k_norm_w, sinks, b_conv_w, b_conv_b, b_ln_w, b_ln_b, b_pw_w, b_pw_b, c_conv_w, c_a_log, c_dt_bias, c_onorm_w, w_out, loss_target, m_norm_w, m_w_in, m_q_norm_w, m_k_norm_w, m_sinks, m_b_conv_w, m_b_conv_b, m_b_ln_w, m_b_ln_b, m_b_pw_w, m_b_pw_b, m_c_conv_w, m_c_a_log, m_c_dt_bias, m_c_onorm_w, m_w_out, v_norm_w, v_w_in, v_q_norm_w, v_k_norm_w, v_sinks, v_b_conv_w, v_b_conv_b, v_b_ln_w, v_b_ln_b, v_b_pw_w, v_b_pw_b, v_c_conv_w, v_c_a_log, v_c_dt_bias, v_c_onorm_w, v_w_out):
    cfg = Cfg(x.shape[-1], x.shape[-2])
    w = dict(norm_w=norm_w, w_in=w_in, q_norm_w=q_norm_w, k_norm_w=k_norm_w, sinks=sinks, b_conv_w=b_conv_w,
             b_conv_b=b_conv_b, b_ln_w=b_ln_w, b_ln_b=b_ln_b, b_pw_w=b_pw_w, b_pw_b=b_pw_b, c_conv_w=c_conv_w,
             c_a_log=c_a_log, c_dt_bias=c_dt_bias, c_onorm_w=c_onorm_w, w_out=w_out)
    m = dict(norm_w=m_norm_w, w_in=m_w_in, q_norm_w=m_q_norm_w, k_norm_w=m_k_norm_w, sinks=m_sinks,
             b_conv_w=m_b_conv_w, b_conv_b=m_b_conv_b, b_ln_w=m_b_ln_w, b_ln_b=m_b_ln_b, b_pw_w=m_b_pw_w,
             b_pw_b=m_b_pw_b, c_conv_w=m_c_conv_w, c_a_log=m_c_a_log, c_dt_bias=m_c_dt_bias, c_onorm_w=m_c_onorm_w,
             w_out=m_w_out)
    v = dict(norm_w=v_norm_w, w_in=v_w_in, q_norm_w=v_q_norm_w, k_norm_w=v_k_norm_w, sinks=v_sinks,
             b_conv_w=v_b_conv_w, b_conv_b=v_b_conv_b, b_ln_w=v_b_ln_w, b_ln_b=v_b_ln_b, b_pw_w=v_b_pw_w,
             b_pw_b=v_b_pw_b, c_conv_w=v_c_conv_w, c_a_log=v_c_a_log, c_dt_bias=v_c_dt_bias, c_onorm_w=v_c_onorm_w,
             w_out=v_w_out)
    chip = 2 * lax.axis_index("x") + lax.axis_index("y")
    c_idx = lax.axis_index("c").astype(jnp.int32).reshape(1)
    D, T = cfg.D, cfg.T
    nseq = x.shape[0]
    n = nseq * T
    w_in_b, w_out_b = w_in.astype(BF16), w_out.astype(BF16)

    def permuted(g_in):
        return permute_w_in(cfg, jnp.concatenate(list(g_in), axis=1))

    def layer_prm(l, g_pw, g_bcw, g_ccw):
        prm = {k: w[k][l] for k in SMALL}
        prm['b_pw_w'] = g_pw.reshape(cfg.BW, cfg.BW)
        prm['b_conv_w'] = jnp.concatenate(list(g_bcw[:, l]), axis=1)
        prm['c_conv_w'] = jnp.concatenate(list(g_ccw[:, l]), axis=1)
        return _layer_params(cfg, prm)

    (g_in0,) = fill_own(run_comm("gather_weights_0", gather_comm([w_in_b[0]])), [w_in_b[0]])
    cos, sin = rope_for(cfg, positions)
    early = [w_out_b[0], b_pw_w[0], b_conv_w, c_conv_w]
    top, bottom = w_in_b[1][:D // 2], w_in_b[1][D // 2:]
    late = [w_out_b[1], b_pw_w[1]]
    conv_ws = {}

    def layer0_rest(rode):
        g_out0, g_pw0, conv_ws['b'], conv_ws['c'] = fill_own(rode, early)
        return layer_prm(0, g_pw0, conv_ws['b'], conv_ws['c']), g_out0.reshape(D, D)

    x1, sv0, rode = layer_forward(
        cfg, 0, x.reshape(n, D), (norm_w[0].reshape(1, -1), layer0_rest), permuted(g_in0), None, cos, sin,
        comms=dict(in_proj=gather_comm(early), attn=gather_comm([top]), conv=gather_comm(late),
                   intra=gather_comm([bottom])))
    lp0, wo0 = sv0['lp'], sv0['wo']
    (g_top,), (g_bottom,) = fill_own(rode['attn'], [top]), fill_own(rode['intra'], [bottom])
    g_out1, g_pw1 = fill_own(rode['conv'], late)
    g_in1 = jnp.concatenate([g_top, g_bottom], axis=1)
    wp1, wo1 = permuted(g_in1), g_out1.reshape(D, D)
    lp1 = layer_prm(1, g_pw1, conv_ws['b'], conv_ws['c'])
    (dx2, loss_local), sv1, _ = layer_forward(cfg, 1, x1, lp1, wp1, wo1, cos, sin, target=loss_target.reshape(n, D))

    def partials(gr):
        return [gr['w_in_blocks'], gr['w_out'].reshape(N_CHIPS, D // N_CHIPS, D),
                gr['b_pw_w'].reshape(N_CHIPS, cfg.BW // N_CHIPS, cfg.BW)]

    dx1, gr1, (_, rs1, _) = layer_backward(cfg, 1, dx2, sv1, lp1, g_in1, wo1, cos, sin,
                                        own_rs=lambda gr: GradReduce(1, partials(gr), chip, c_idx))
    dx0, gr0, (got1, rs0, got0) = layer_backward(cfg, 0, dx1, sv0, lp0, g_in0, wo0, cos, sin, rs=rs1,
                                           own_rs=lambda gr: GradReduce(0, partials(gr), chip, c_idx))
    red1 = rs1.finish(got1)
    red0 = rs0.finish(got0)
    grad_x = dx0.reshape(x.shape)
    grads = [gr0, gr1]

    small_parts = [_stack([grads[l][k] for l in range(DEPTH)]) for k in SMALL] + [loss_local.reshape(1)]
    *small_red, loss = _unpack(all_reduce_small("all_reduce_small", _pack(small_parts)), [a.shape for a in small_parts])
    loss = loss.reshape(())
    g = {}
    for k, a in zip(SMALL, small_red):
        if k in CHIP_SHARDED_SMALL:
            ax = CHIP_SHARDED_SMALL[k]
            width = a.shape[ax] // N_CHIPS
            a = lax.dynamic_slice_in_dim(a, chip * width, width, axis=ax)
        g[k] = a

    delta, new_m, new_v = {}, {}, {}
    for k, g0, g1 in zip(BIG, red0, red1):
        update = adamw_layers if w[k].shape[-1] % LANE == 0 else adamw_cols_major
        g[k], delta[k], new_m[k], new_v[k] = update(f"adamw_{k}", w[k], g0, g1, m[k], v[k], 128)
    outs = adamw_many("adamw_small", *[[d[k] for k in SMALL] for d in (w, g, m, v)])
    for i, k in enumerate(SMALL):
        delta[k], new_m[k], new_v[k] = outs[i], outs[len(SMALL) + i], outs[2 * len(SMALL) + i]
    return (loss, grad_x, *[g[k] for k in WEIGHTS], *[delta[k] for k in WEIGHTS], *[new_m[k] for k in WEIGHTS],
            *[new_v[k] for k in WEIGHTS])
```

```python
import functools
import math

import numpy as np
import jax
import jax.numpy as jnp
from jax import lax
from jax.experimental import pallas as pl
from jax.experimental.pallas import tpu as pltpu

F32 = jnp.float32
BF16 = jnp.bfloat16
HI = lax.Precision.HIGHEST
MESH = pl.DeviceIdType.MESH

DEPTH = 2
A_HEAD = 64
A_GROUP = 3
ATTN_BLOCK = 128
ROT_DIM = 16
ROPE_THETA = 500000.0
B_CONV = 31
B_HALO = 32
C_HEAD = 128
C_CONV = 4
C_HALO = 8
CHUNK = 64
EPS = 1e-6
LANE = 128

ADAM_LR = 0.001
ADAM_B1 = 0.9
ADAM_B2 = 0.999
ADAM_EPS = 1e-08
ADAM_WD = 0.01
ADAM_STEP = 10

VMEM_LIMIT = 56 * 1024 * 1024

WEIGHTS = ['norm_w', 'w_in', 'q_norm_w', 'k_norm_w', 'sinks', 'b_conv_w', 'b_conv_b', 'b_ln_w', 'b_ln_b',
           'b_pw_w', 'b_pw_b', 'c_conv_w', 'c_a_log', 'c_dt_bias', 'c_onorm_w', 'w_out']


class Cfg:
    def __init__(self, d_model=2048, seq=2048):
        self.D = d_model
        self.T = seq
        self.AW = 3 * d_model // 8
        self.AQH = self.AW // A_HEAD
        self.AKH = self.AQH // A_GROUP
        self.AKW = self.AKH * A_HEAD
        self.BW = d_model // 4
        self.CH = (d_model - self.AW - self.BW) // C_HEAD
        self.CW = self.CH * C_HEAD
        AW, AKW, BW, CW, CH = self.AW, self.AKW, self.BW, self.CW, self.CH
        orig = [('qa', AW), ('ka', AKW), ('va', AKW), ('za', AW), ('ub', 2 * BW), ('zb', BW),
                ('qc', CW), ('kc', CW), ('vc', CW), ('bc', CH), ('ac', CH), ('zc', CW)]
        self.orig = {}
        off = 0
        for n, w in orig:
            self.orig[n] = (off, w)
            off += w
        self.IN_COLS = off
        order = ['qa', 'za', 'qc', 'kc', 'vc', 'zc', 'ka', 'va', 'ub', 'zb', 'bc', 'ac']
        self.order = order
        self.g = {}
        off = 0
        for n in order:
            w = self.orig[n][1]
            wp = LANE if n in ('bc', 'ac') else w
            assert off % wp == 0, (n, off, wp)
            self.g[n] = (off, wp)
            off += wp
        self.WP = off

    def blk(self, name):
        off, w = self.g[name]
        return off // w


def _cparams(sem, vmem=VMEM_LIMIT):
    return pltpu.CompilerParams(dimension_semantics=sem, vmem_limit_bytes=vmem)


def _silu(x):
    return x * jax.nn.sigmoid(x)


ANY = pl.BlockSpec(memory_space=pl.ANY)


class Comm:
    def __init__(self, ins, out_shapes, sems, start, finish, aliases=None):
        self.ins, self.out_shapes, self.sems = list(ins), list(out_shapes), list(sems)
        self.start, self.finish, self.aliases = start, finish, dict(aliases or {})


def call_with_comm(body, name, grid, in_specs, out_specs, out_shape, scratch_shapes, semantics, args, comm=None):
    in_specs, out_specs, out_shape, scratch_shapes = list(in_specs), list(out_specs), list(out_shape), list(scratch_shapes)
    if comm is None:
        outs = pl.pallas_call(body, name=name, grid=grid, in_specs=in_specs, out_specs=out_specs, out_shape=out_shape,
                              scratch_shapes=scratch_shapes, compiler_params=_cparams(semantics))(*args)
        return list(outs), []
    ni, no, ns = len(in_specs), len(out_specs), len(scratch_shapes)
    nci, nco = len(comm.ins), len(comm.out_shapes)

    def wrapped(*refs):
        h_in, c_in = refs[:ni], refs[ni:ni + nci]
        h_out, c_out = refs[ni + nci:ni + nci + no], refs[ni + nci + no:ni + nci + no + nco]
        h_scr, c_sems = refs[ni + nci + no + nco:ni + nci + no + nco + ns], refs[ni + nci + no + nco + ns:]
        ids = [pl.program_id(d) for d in range(len(grid))]
        first = functools.reduce(jnp.logical_and, [i == 0 for i in ids])
        last = functools.reduce(jnp.logical_and, [i == g - 1 for i, g in zip(ids, grid)])

        @pl.when(first)
        def _():
            comm.start(c_in, c_out, c_sems)

        body(*h_in, *h_out, *h_scr)

        @pl.when(last)
        def _():
            comm.finish(c_in, c_out, c_sems)

    outs = pl.pallas_call(
        wrapped, name=name, grid=grid, in_specs=in_specs + [ANY] * nci, out_specs=out_specs + [ANY] * nco,
        out_shape=out_shape + comm.out_shapes, scratch_shapes=scratch_shapes + comm.sems,
        input_output_aliases={ni + k: no + v for k, v in comm.aliases.items()},
        compiler_params=_cparams(("arbitrary",) * len(grid)),
    )(*args, *comm.ins)
    return list(outs[:no]), list(outs[no:])


def run_comm(name, comm):
    nci, nco = len(comm.ins), len(comm.out_shapes)

    def body(*refs):
        c_in, c_out, c_sems = refs[:nci], refs[nci:nci + nco], refs[nci + nco:]
        comm.start(c_in, c_out, c_sems)
        comm.finish(c_in, c_out, c_sems)

    return pl.pallas_call(
        body, name=name, in_specs=[ANY] * nci, out_specs=[ANY] * nco, out_shape=comm.out_shapes,
        scratch_shapes=comm.sems, input_output_aliases=comm.aliases,
    )(*comm.ins)


def _stack(xs, axis=0):
    assert axis == 0
    return jnp.concatenate([x[None] for x in xs], axis=0)


def _bdot(a, b, ca, cb, precision=HI):
    dims = (((ca,), (cb,)), ((0,), (0,)))
    if precision is HI and a.dtype == F32:
        ah = a.astype(BF16)
        bh = b.astype(BF16)
        al = (a - ah.astype(F32)).astype(BF16)
        bl = (b - bh.astype(F32)).astype(BF16)
        dg = lambda p, q: lax.dot_general(p, q, dims, preferred_element_type=F32)
        return dg(ah, bh) + (dg(ah, bl) + dg(al, bh))
    return lax.dot_general(a, b, dims, precision=precision, preferred_element_type=F32)


def _rope_matrix(nb):
    i = lax.broadcasted_iota(jnp.int32, (nb, A_HEAD, A_HEAD), 1)
    j = lax.broadcasted_iota(jnp.int32, (nb, A_HEAD, A_HEAD), 2)
    half = ROT_DIM // 2
    neg = (j < half) & (i == j + half)
    pos = (j >= half) & (j < ROT_DIM) & (i == j - half)
    return jnp.where(neg, -1.0, jnp.where(pos, 1.0, 0.0)).astype(F32)


def _norm_rope(xh, w, cos, sin):
    y = xh * lax.rsqrt(jnp.mean(xh * xh, axis=-1, keepdims=True) + EPS) * w
    return y * cos + _bdot(y, _rope_matrix(xh.shape[0]), 2, 1) * sin


def _sink_probs(s, sink):
    m = jnp.maximum(jnp.max(s, axis=-1, keepdims=True), sink)
    e = jnp.exp(s - m)
    es = jnp.exp(sink - m)
    den = jnp.sum(e, axis=-1, keepdims=True) + es
    return e / den, es / den


@jax.custom_vjp
def _sink_softmax(s, sink):
    return _sink_probs(s, sink)[0]


def _sink_softmax_fwd(s, sink):
    p, ps = _sink_probs(s, sink)
    return p, (p, ps)


def _sink_softmax_bwd(res, dp):
    p, ps = res
    inner = jnp.sum(p * dp, axis=-1, keepdims=True)
    return p * (dp - inner), -jnp.sum(ps * inner, axis=1, keepdims=True)


_sink_softmax.defvjp(_sink_softmax_fwd, _sink_softmax_bwd)


def attn_block(cfg, first, q, za, kc, vc, cosc, sinc, kp, vp, cosp, sinp, qnw, knw, sinks_row):
    blk = ATTN_BLOCK
    nq, nk = cfg.AQH, cfg.AKH
    qi = lax.broadcasted_iota(jnp.int32, (blk, 2 * blk), 0)
    kj = lax.broadcasted_iota(jnp.int32, (blk, 2 * blk), 1)
    dist = qi + blk - kj
    valid = ((dist >= 0) & (dist < blk) & (jnp.logical_not(first) | (kj >= blk)))[None]
    cos2 = jnp.concatenate([cosp, cosc], axis=0)
    sin2 = jnp.concatenate([sinp, sinc], axis=0)
    head = lambda x, h: x[:, A_HEAD * h:A_HEAD * (h + 1)]
    k2 = _stack([jnp.concatenate([head(kp, h), head(kc, h)], axis=0) for h in range(nk)], axis=0)
    v2 = _stack([jnp.concatenate([head(vp, h), head(vc, h)], axis=0) for h in range(nk)], axis=0)
    k2 = _norm_rope(k2, knw[None], cos2[None], sin2[None]).astype(BF16)
    v2 = v2.astype(BF16)
    k2 = _stack([k2[h // A_GROUP] for h in range(nq)], axis=0)
    v2 = _stack([v2[h // A_GROUP] for h in range(nq)], axis=0)
    qh = _stack([head(q, h) for h in range(nq)], axis=0)
    qh = _norm_rope(qh, qnw[None], cosc[None], sinc[None]).astype(BF16)
    s = _bdot(qh, k2, 2, 2, None) * (A_HEAD ** -0.5)
    s = jnp.where(valid, s, -1e30)
    sink = _stack([sinks_row[:, A_HEAD * h:A_HEAD * h + 1] for h in range(nq)], axis=0)
    o = _bdot(_sink_softmax(s, sink).astype(BF16), v2, 2, 1, None)
    return (jnp.concatenate([o[h] for h in range(nq)], axis=1) * _silu(za),)


def conv_block(cfg, first, u, zb, uh, cw, cb, lw, lb, pw, pb):
    BW = cfg.BW
    tb = u.shape[0]
    uu = jnp.concatenate([uh, u], axis=0)
    h = uu[:, :BW] * jax.nn.sigmoid(uu[:, BW:])
    row = lax.broadcasted_iota(jnp.int32, h.shape, 0)
    h = jnp.where(first & (row < B_HALO), 0.0, h)
    acc = jnp.zeros((tb, BW), F32) + cb
    base = B_HALO - (B_CONV - 1)
    for k in range(B_CONV):
        acc = acc + cw[k:k + 1, :] * h[base + k:base + k + tb, :]
    mu = jnp.mean(acc, axis=-1, keepdims=True)
    var = jnp.mean(jnp.square(acc - mu), axis=-1, keepdims=True)
    y = (acc - mu) * lax.rsqrt(var + EPS) * lw + lb
    s = _silu(y)
    o = jnp.dot(s.astype(BF16), pw.astype(BF16), preferred_element_type=F32) + pb
    return (o * _silu(zb),)


def gdn_prep_block(cfg, first, xq, xk, xv, braw, araw, hq, hk, hv, cw, alog, dtb):
    CW = cfg.CW
    tb = xq.shape[0]
    outs = []
    for idx, (x, xh) in enumerate(((xq, hq), (xk, hk), (xv, hv))):
        xx = jnp.concatenate([jnp.where(first, 0.0, xh), x], axis=0)
        w = cw[:, idx * CW:(idx + 1) * CW]
        acc = jnp.zeros((tb, CW), F32)
        base = C_HALO - (C_CONV - 1)
        for k in range(C_CONV):
            acc = acc + w[k:k + 1, :] * xx[base + k:base + k + tb, :]
        y = _silu(acc)
        if idx < 2:
            parts = []
            for h in range(cfg.CH):
                yh = y[:, C_HEAD * h:C_HEAD * (h + 1)]
                parts.append(yh * lax.rsqrt(jnp.sum(yh * yh, axis=-1, keepdims=True) + EPS))
            y = jnp.concatenate(parts, axis=1)
        outs.append(y)
    beta = jax.nn.sigmoid(braw)
    g = -jnp.exp(alog) * jax.nn.softplus(araw + dtb)
    return outs[0], outs[1], outs[2], g, beta


def _inverse_unit_lower(low, eye):
    pw = low
    inv = eye - low
    for _ in range(5):
        pwb = pw.astype(BF16)
        pw = _bdot(pwb, pwb, 2, 1, None)
        inv = inv + _bdot(inv.astype(BF16), pw.astype(BF16), 2, 1, None)
    ax = inv + _bdot(low, inv, 2, 1)
    return inv + _bdot(inv, eye - ax, 2, 1)


@jax.custom_vjp
def _saved_inverse(low, inv):
    return inv


def _saved_inverse_fwd(low, inv):
    return inv, inv


def _saved_inverse_bwd(inv, d):
    dlow = -_bdot(_bdot(inv, d, 1, 1), inv, 2, 2)
    return dlow, jnp.zeros_like(inv)


_saved_inverse.defvjp(_saved_inverse_fwd, _saved_inverse_bwd)


def gdn_intra_rows(cfg, first, qn, kn, v, g, beta, inv_saved=None):
    c = CHUNK
    CH = cfg.CH
    nchunk = qn.shape[0] // c
    i = lax.broadcasted_iota(jnp.int32, (c, c), 0)
    j = lax.broadcasted_iota(jnp.int32, (c, c), 1)
    incl = (i >= j)[None]
    strict = (i > j)[None]
    eye = (i == j).astype(F32)[None]
    tri = (i >= j).astype(F32)
    rows = [slice(c * ci, c * (ci + 1)) for ci in range(nchunk)]
    gcs = [jnp.dot(tri, g[r], precision=HI, preferred_element_type=F32) for r in rows]
    pairs = [(ci, h) for ci in range(nchunk) for h in range(CH)]
    heads = lambda x, wd: _stack([x[rows[ci], wd * h:wd * (h + 1)] for ci, h in pairs], axis=0)
    gch = _stack([gcs[ci][:, h:h + 1] for ci, h in pairs], axis=0)
    bh = _stack([beta[rows[ci], h:h + 1] for ci, h in pairs], axis=0)
    q = heads(qn, C_HEAD) * (C_HEAD ** -0.5)
    k = heads(kn, C_HEAD)
    vv = heads(v, C_HEAD)
    a = jnp.broadcast_to(gch, (len(pairs), c, c))
    diff = jnp.where(incl, a - jnp.swapaxes(a, 1, 2), 0.0)
    decay = jnp.where(incl, jnp.exp(diff), 0.0)
    kb = k * bh
    low = jnp.where(strict, _bdot(kb, k, 2, 2) * decay, 0.0)
    if inv_saved is None:
        inv = _inverse_unit_lower(low, eye)
    else:
        inv = _saved_inverse(low, heads(inv_saved, c))
    eg = jnp.exp(gch)
    sol = _bdot(inv, jnp.concatenate([vv * bh, kb * eg], axis=2), 2, 1)
    intra = jnp.where(incl, _bdot(q, k, 2, 2) * decay, 0.0)
    qg = q * eg
    kd = k * jnp.exp(gch[:, c - 1:c, :] - gch)
    glast = jnp.concatenate([jnp.broadcast_to(gc[c - 1:c, :], gc.shape) for gc in gcs], axis=0)

    def unstack(x):
        return jnp.concatenate([jnp.concatenate([x[ci * CH + h] for h in range(CH)], axis=1) for ci in range(nchunk)],
                               axis=0)

    outs = (unstack(sol[:, :, :C_HEAD]), unstack(sol[:, :, C_HEAD:]), unstack(qg), unstack(kd), unstack(intra), glast)
    return outs + (unstack(inv),) if inv_saved is None else outs


def gdn_state_step(S, u, w, qg, kd, intra, glast):
    v_new = u - _bdot(w, S, 2, 1)
    o = _bdot(qg, S, 2, 1) + _bdot(intra, v_new, 2, 1)
    S_next = S * jnp.exp(glast) + _bdot(kd, v_new, 1, 1)
    return o, S_next


def gdn_out_block(cfg, first, o, zc, onw):
    parts = []
    for h in range(cfg.CH):
        sl = slice(C_HEAD * h, C_HEAD * (h + 1))
        oh = o[:, sl]
        y = oh * lax.rsqrt(jnp.mean(oh * oh, axis=-1, keepdims=True) + EPS) * onw
        parts.append(y * _silu(zc[:, sl]))
    return (jnp.concatenate(parts, axis=1),)


def rms_block(x, nw):
    return x * lax.rsqrt(jnp.mean(x * x, axis=-1, keepdims=True) + EPS) * nw


class Row:
    def __init__(self, arr, width, colblk=0, grad=None):
        self.arr, self.width, self.colblk, self.grad = arr, width, colblk, grad


class Halo:
    def __init__(self, arr, width, colblk, hr, tie=None):
        self.arr, self.width, self.colblk, self.hr, self.tie = arr, width, colblk, hr, tie


def _row_specs(tb, rows, halos, params, pos):
    specs = [pl.BlockSpec((tb, r.width), lambda i, cb=r.colblk: (pos(i), cb)) for r in rows]
    specs += [pl.BlockSpec((h.hr, h.width),
                           lambda i, cb=h.colblk, m=tb // h.hr: (jnp.maximum(pos(i) * m - 1, 0), cb))
              for h in halos]
    specs += [pl.BlockSpec(p.shape, lambda i: (0, 0)) for p in params]
    return specs


def rb_fwd(name, fn, n, tb, bps, rows, halos, params, outs, comm=None):
    nr, nh, npar = len(rows), len(halos), len(params)

    def body(*refs):
        ins = refs[:nr + nh + npar]
        o_refs = refs[nr + nh + npar:]
        first = (pl.program_id(0) % bps) == 0
        res = fn(first, *[r[...] for r in ins])
        for ref, val in zip(o_refs, res):
            ref[...] = val.astype(ref.dtype)

    res, carried = call_with_comm(
        body, name, (n // tb,), _row_specs(tb, rows, halos, params, lambda i: i),
        [pl.BlockSpec((tb, w), lambda i: (i, 0)) for w, _ in outs],
        [jax.ShapeDtypeStruct((n, w), dt) for w, dt in outs], [], ("parallel",),
        [r.arr for r in rows] + [h.arr for h in halos] + list(params), comm)
    return (res, carried) if comm is not None else res


def rb_bwd(name, fn, n, tb, bps, rows, halos, params, douts, param_grads, comm=None):
    nr, nh, npar, nd = len(rows), len(halos), len(params), len(douts)
    nblk = n // tb
    grow = [k for k, r in enumerate(rows) if r.grad is not None]
    ghalo = [k for k, h in enumerate(halos) if h.tie is not None]
    gpar = [k for k, f in enumerate(param_grads) if f]
    pos = lambda i: nblk - 1 - i

    def body(*refs):
        ins = refs[:nr + nh + npar]
        d_refs = refs[nr + nh + npar:nr + nh + npar + nd]
        rest = refs[nr + nh + npar + nd:]
        grow_refs = rest[:len(grow)]
        gpar_refs = rest[len(grow):len(grow) + len(gpar)]
        carry_refs = rest[len(grow) + len(gpar):]
        i = pl.program_id(0)
        first = (pos(i) % bps) == 0
        vals = [r[...] for r in ins]
        diff_idx = grow + [nr + k for k in ghalo] + [nr + nh + k for k in gpar]

        def f(*dargs):
            full = list(vals)
            for k, a in zip(diff_idx, dargs):
                full[k] = a
            return fn(first, *full)

        res, vjp = jax.vjp(f, *[vals[k] for k in diff_idx])
        grads = vjp(tuple(d[...].astype(r.dtype) for d, r in zip(d_refs, res)))
        g_rows = list(grads[:len(grow)])
        g_halos = grads[len(grow):len(grow) + len(ghalo)]
        g_pars = grads[len(grow) + len(ghalo):]

        @pl.when(i == 0)
        def _():
            for c in carry_refs:
                c[...] = jnp.zeros_like(c)
            for p in gpar_refs:
                p[...] = jnp.zeros_like(p)

        for k, ref in enumerate(grow_refs):
            ref[...] = g_rows[k].astype(ref.dtype)
        for ci, hk in enumerate(ghalo):
            h = halos[hk]
            k = grow.index(h.tie)
            tail = g_rows[k][tb - h.hr:, :] + carry_refs[ci][...]
            grow_refs[k][tb - h.hr:, :] = tail.astype(grow_refs[k].dtype)
            carry_refs[ci][...] = g_halos[ci]
        for ref, gp in zip(gpar_refs, g_pars):
            ref[...] += gp

    out_specs = [pl.BlockSpec((tb, rows[k].width), lambda i: (pos(i), 0)) for k in grow]
    out_specs += [pl.BlockSpec(params[k].shape, lambda i: (0, 0)) for k in gpar]
    out_shape = [jax.ShapeDtypeStruct((n, rows[k].width), rows[k].grad) for k in grow]
    out_shape += [jax.ShapeDtypeStruct(params[k].shape, F32) for k in gpar]
    in_specs = _row_specs(tb, rows, halos, params, pos)
    in_specs += [pl.BlockSpec((tb, d.shape[1]), lambda i: (pos(i), 0)) for d in douts]
    res, carried = call_with_comm(
        body, name, (nblk,), in_specs, out_specs, out_shape,
        [pltpu.VMEM((halos[k].hr, halos[k].width), F32) for k in ghalo], ("arbitrary",),
        [r.arr for r in rows] + [h.arr for h in halos] + list(params) + list(douts), comm)
    return (res, carried) if comm is not None else res


_DIMS = {'nn': (((1,), (0,)), ((), ())), 'nt': (((1,), (1,)), ((), ())), 'tn': (((0,), (0,)), ((), ()))}


def matmul(name, a, b, mode, tm, tn, tk, out_dtype=F32, add=None, comm=None):
    if mode == 'tn':
        K, M = a.shape
    else:
        M, K = a.shape
    N = b.shape[0] if mode == 'nt' else b.shape[1]
    tm, tn, tk = min(tm, M), min(tn, N), min(tk, K)
    assert M % tm == 0 and N % tn == 0 and K % tk == 0, (name, M, N, K, tm, tn, tk)
    nk = K // tk
    a_spec = pl.BlockSpec((tk, tm), lambda i, j, k: (k, i)) if mode == 'tn' else pl.BlockSpec((tm, tk), lambda i, j, k: (i, k))
    b_spec = pl.BlockSpec((tn, tk), lambda i, j, k: (j, k)) if mode == 'nt' else pl.BlockSpec((tk, tn), lambda i, j, k: (k, j))
    o_spec = pl.BlockSpec((tm, tn), lambda i, j, k: (i, j))
    has_add = add is not None

    def body(*refs):
        a_ref, b_ref = refs[0], refs[1]
        add_ref = refs[2] if has_add else None
        o_ref = refs[-1]
        k = pl.program_id(2)
        part = lax.dot_general(a_ref[...].astype(BF16), b_ref[...].astype(BF16), _DIMS[mode], preferred_element_type=F32)

        @pl.when(k == 0)
        def _():
            o_ref[...] = ((part + add_ref[...]) if has_add else part).astype(o_ref.dtype)

        if nk > 1:
            @pl.when(k > 0)
            def _():
                o_ref[...] += part

    assert nk == 1 or out_dtype == F32
    ins = [a, b] + ([add] if has_add else [])
    in_specs = [a_spec, b_spec] + ([o_spec] if has_add else [])
    outs, couts = call_with_comm(body, name, (M // tm, N // tn, nk), in_specs, [o_spec],
                                 [jax.ShapeDtypeStruct((M, N), out_dtype)], [], ("parallel", "parallel", "arbitrary"),
                                 ins, comm)
    return (outs[0], couts) if comm is not None else outs[0]


def grad_w_blocks(name, h, dpb, tm, tk):
    n, d = h.shape
    nb, _, s = dpb.shape
    tm, tk = min(tm, d), min(tk, n)
    assert d % tm == 0 and n % tk == 0
    nk = n // tk

    def body(h_ref, b_ref, o_ref, lo_ref):
        k = pl.program_id(2)
        part = lax.dot_general(h_ref[...], b_ref[0], _DIMS['tn'], preferred_element_type=F32)

        @pl.when(k == 0)
        def _():
            o_ref[0] = part

        if nk > 1:
            @pl.when(k > 0)
            def _():
                o_ref[0] += part

        @pl.when(k == nk - 1)
        def _():
            lo_ref[0] = o_ref[0].astype(BF16)

    spec = pl.BlockSpec((1, tm, s), lambda j, i, k: (j, i, 0))
    return pl.pallas_call(
        body, name=name, grid=(nb, d // tm, nk),
        in_specs=[pl.BlockSpec((tk, tm), lambda j, i, k: (k, i)), pl.BlockSpec((1, tk, s), lambda j, i, k: (j, k, 0))],
        out_specs=[spec, spec],
        out_shape=[jax.ShapeDtypeStruct((nb, d, s), F32), jax.ShapeDtypeStruct((nb, d, s), BF16)],
        compiler_params=_cparams(("parallel", "parallel", "arbitrary")),
    )(h, dpb)


def grad_h_blocks(name, dpb, wb, tm, tn, comm=None):
    nb, n, s = dpb.shape
    d = wb.shape[1]
    tm, tn = min(tm, n), min(tn, d)
    assert n % tm == 0 and d % tn == 0

    def body(a_ref, b_ref, o_ref):
        k = pl.program_id(2)
        part = lax.dot_general(a_ref[0], b_ref[0], _DIMS['nt'], preferred_element_type=F32)

        @pl.when(k == 0)
        def _():
            o_ref[...] = part

        @pl.when(k > 0)
        def _():
            o_ref[...] += part

    outs, carried = call_with_comm(
        body, name, (n // tm, d // tn, nb),
        [pl.BlockSpec((1, tm, s), lambda i, j, k: (k, i, 0)), pl.BlockSpec((1, tn, s), lambda i, j, k: (k, j, 0))],
        [pl.BlockSpec((tm, tn), lambda i, j, k: (i, j))], [jax.ShapeDtypeStruct((n, d), F32)], [],
        ("parallel", "parallel", "arbitrary"), [dpb, wb], comm)
    return (outs[0], carried) if comm is not None else outs[0]


def norm_in_proj(name, x, nw, wp, tm, tn, comm=None):
    n, d = x.shape
    wpc = wp.shape[1]
    tm, tn = min(tm, n), min(tn, wpc)
    assert n % tm == 0 and wpc % tn == 0

    def body(x_ref, nw_ref, w_ref, p_ref, h_ref):
        @pl.when(pl.program_id(1) == 0)
        def _():
            h_ref[...] = rms_block(x_ref[...], nw_ref[...]).astype(BF16)

        p_ref[...] = jnp.dot(h_ref[...], w_ref[...], preferred_element_type=F32)

    outs, couts = call_with_comm(
        body, name, (n // tm, wpc // tn),
        [pl.BlockSpec((tm, d), lambda i, j: (i, 0)), pl.BlockSpec((1, d), lambda i, j: (0, 0)),
         pl.BlockSpec((d, tn), lambda i, j: (0, j))],
        [pl.BlockSpec((tm, tn), lambda i, j: (i, j)), pl.BlockSpec((tm, d), lambda i, j: (i, 0))],
        [jax.ShapeDtypeStruct((n, wpc), F32), jax.ShapeDtypeStruct((n, d), BF16)], [], ("parallel", "arbitrary"),
        [x, nw, wp], comm)
    return (outs[0], outs[1], couts) if comm is not None else (outs[0], outs[1])


def norm_bwd(name, x, nw, dh, dres, tb, comm=None):
    n, d = x.shape
    tb = min(tb, n)

    def body(x_ref, nw_ref, dh_ref, dres_ref, dx_ref, dnw_ref):
        @pl.when(pl.program_id(0) == 0)
        def _():
            dnw_ref[...] = jnp.zeros_like(dnw_ref)

        _, vjp = jax.vjp(rms_block, x_ref[...], nw_ref[...])
        dx, dnw = vjp(dh_ref[...])
        dx_ref[...] = dx + dres_ref[...]
        dnw_ref[...] += dnw

    row = pl.BlockSpec((tb, d), lambda i: (i, 0))
    par = pl.BlockSpec((1, d), lambda i: (0, 0))
    outs, carried = call_with_comm(
        body, name, (n // tb,), [row, par, row, row], [row, par],
        [jax.ShapeDtypeStruct((n, d), F32), jax.ShapeDtypeStruct((1, d), F32)], [], ("arbitrary",),
        [x, nw, dh, dres], comm)
    return (outs[0], outs[1], carried) if comm is not None else (outs[0], outs[1])


def out_proj_loss(name, y, wo, x, target, tm):
    n, d = x.shape
    tm = min(tm, n)
    assert n % tm == 0

    def body(y_ref, w_ref, x_ref, t_ref, dz_ref, loss_ref):
        @pl.when(pl.program_id(0) == 0)
        def _():
            loss_ref[...] = jnp.zeros_like(loss_ref)

        z = x_ref[...] + jnp.dot(y_ref[...], w_ref[...], preferred_element_type=F32)
        err = z - t_ref[...]
        dz_ref[...] = err * (1.0 / d)
        part = 0.5 * jnp.sum(jnp.mean(err * err, axis=-1, keepdims=True), axis=0, keepdims=True)
        loss_ref[...] += jnp.broadcast_to(part, loss_ref.shape)

    row = pl.BlockSpec((tm, d), lambda i: (i, 0))
    dz, loss = pl.pallas_call(
        body, name=name, grid=(n // tm,),
        in_specs=[pl.BlockSpec((tm, y.shape[1]), lambda i: (i, 0)), pl.BlockSpec(wo.shape, lambda i: (0, 0)), row, row],
        out_specs=[row, pl.BlockSpec((8, LANE), lambda i: (0, 0))],
        out_shape=[jax.ShapeDtypeStruct((n, d), F32), jax.ShapeDtypeStruct((8, LANE), F32)],
        compiler_params=_cparams(("arbitrary",)),
    )(y, wo, x, target)
    return dz, loss[0, 0]


def rope_tables(name, pos_col, inv_freq_row):
    n = pos_col.shape[0]

    def body(p_ref, f_ref, c_ref, s_ref):
        ang = p_ref[...].astype(F32) * f_ref[...]
        lane = lax.broadcasted_iota(jnp.int32, ang.shape, 1)
        c_ref[...] = jnp.where(lane < ROT_DIM, jnp.cos(ang), 1.0)
        s_ref[...] = jnp.where(lane < ROT_DIM, jnp.sin(ang), 0.0)

    return pl.pallas_call(
        body, name=name, out_shape=[jax.ShapeDtypeStruct((n, A_HEAD), F32)] * 2,
    )(pos_col, inv_freq_row)


def _scan_operands(cfg, nseq, u_ref, w_ref, qg_ref, kd_ref, a_ref, gl_ref):
    pairs = [(b, h) for b in range(nseq) for h in range(cfg.CH)]
    st = lambda r, wd: _stack([r[b, :, wd * h:wd * (h + 1)] for b, h in pairs], axis=0)
    gl = _stack([gl_ref[b, 0:1, h:h + 1] for b, h in pairs], axis=0)
    return st(u_ref, C_HEAD), st(w_ref, C_HEAD), st(qg_ref, C_HEAD), st(kd_ref, C_HEAD), st(a_ref, CHUNK), gl


def gdn_scan_fwd(name, cfg, nseq, u, w, qg, kd, intra, glast):
    CH, CW, T = cfg.CH, cfg.CW, cfg.T
    nc = T // CHUNK

    def body(u_ref, w_ref, qg_ref, kd_ref, a_ref, gl_ref, o_ref, sin_ref, s_ref):
        @pl.when(pl.program_id(0) == 0)
        def _():
            s_ref[...] = jnp.zeros_like(s_ref)

        S = s_ref[...]
        for b in range(nseq):
            sin_ref[b, 0] = S[b * CH:(b + 1) * CH]
        o, S_next = gdn_state_step(S, *_scan_operands(cfg, nseq, u_ref, w_ref, qg_ref, kd_ref, a_ref, gl_ref))
        s_ref[...] = S_next
        for b in range(nseq):
            o_ref[b] = jnp.concatenate([o[b * CH + h] for h in range(CH)], axis=1)

    row = lambda wd: pl.BlockSpec((nseq, CHUNK, wd), lambda c: (0, c, 0))
    widths = [CW, CW, CW, CW, CH * CHUNK, LANE]
    o, s_in = pl.pallas_call(
        body, name=name, grid=(nc,),
        in_specs=[row(x) for x in widths],
        out_specs=[row(CW), pl.BlockSpec((nseq, 1, CH, C_HEAD, C_HEAD), lambda c: (0, c, 0, 0, 0))],
        out_shape=[jax.ShapeDtypeStruct((nseq, T, CW), F32),
                   jax.ShapeDtypeStruct((nseq, nc, CH, C_HEAD, C_HEAD), F32)],
        scratch_shapes=[pltpu.VMEM((nseq * CH, C_HEAD, C_HEAD), F32)],
        compiler_params=_cparams(("arbitrary",)),
    )(*[a.reshape(nseq, T, a.shape[1]) for a in (u, w, qg, kd, intra, glast)])
    return o.reshape(nseq * T, CW), s_in


def gdn_scan_bwd(name, cfg, nseq, u, w, qg, kd, intra, glast, s_in, do, comm=None):
    CH, CW, T = cfg.CH, cfg.CW, cfg.T
    nc = T // CHUNK

    def body(u_ref, w_ref, qg_ref, kd_ref, a_ref, gl_ref, sin_ref, do_ref,
             du_ref, dw_ref, dqg_ref, dkd_ref, da_ref, dgl_ref, ds_ref):
        @pl.when(pl.program_id(0) == 0)
        def _():
            ds_ref[...] = jnp.zeros_like(ds_ref)

        S = jnp.concatenate([sin_ref[b, 0] for b in range(nseq)], axis=0)
        dout = _stack([do_ref[b, :, C_HEAD * h:C_HEAD * (h + 1)] for b in range(nseq) for h in range(CH)], axis=0)
        _, vjp = jax.vjp(gdn_state_step, S, *_scan_operands(cfg, nseq, u_ref, w_ref, qg_ref, kd_ref, a_ref, gl_ref))
        dS, du, dw, dqg, dkd, da, dg = vjp((dout, ds_ref[...]))
        ds_ref[...] = dS
        lane = lax.broadcasted_iota(jnp.int32, (CHUNK, LANE), 1)
        rowi = lax.broadcasted_iota(jnp.int32, (CHUNK, LANE), 0)
        for b in range(nseq):
            cat = lambda x: jnp.concatenate([x[b * CH + h] for h in range(CH)], axis=1)
            du_ref[b] = cat(du)
            dw_ref[b] = cat(dw)
            dqg_ref[b] = cat(dqg)
            dkd_ref[b] = cat(dkd)
            da_ref[b] = cat(da)
            dgl = jnp.zeros((CHUNK, LANE), F32)
            for h in range(CH):
                dgl = dgl + jnp.where((lane == h) & (rowi == 0), dg[b * CH + h], 0.0)
            dgl_ref[b] = dgl

    row = lambda wd: pl.BlockSpec((nseq, CHUNK, wd), lambda c: (0, nc - 1 - c, 0))
    widths = [CW, CW, CW, CW, CH * CHUNK, LANE]
    outs, carried = call_with_comm(
        body, name, (nc,),
        [row(x) for x in widths]
        + [pl.BlockSpec((nseq, 1, CH, C_HEAD, C_HEAD), lambda c: (0, nc - 1 - c, 0, 0, 0)), row(CW)],
        [row(x) for x in widths], [jax.ShapeDtypeStruct((nseq, T, x), F32) for x in widths],
        [pltpu.VMEM((nseq * CH, C_HEAD, C_HEAD), F32)], ("arbitrary",),
        [a.reshape(nseq, T, a.shape[1]) for a in (u, w, qg, kd, intra, glast)] + [s_in, do.reshape(nseq, T, CW)], comm)
    return [o.reshape(nseq * T, o.shape[2]) for o in outs], carried


def _tile(total, cap, unit=LANE):
    best = None
    for t in range(unit, min(cap, total) + 1, unit):
        if total % t == 0:
            best = t
    assert best is not None, (total, cap, unit)
    return best


def _pad_lanes(v, width=LANE):
    return jnp.pad(v.reshape(1, -1), ((0, 0), (0, width - v.shape[-1])))


def permute_w_in(cfg, w):
    parts = []
    for n in cfg.order:
        off, wd = cfg.orig[n]
        blk = w[:, off:off + wd]
        if cfg.g[n][1] != wd:
            blk = jnp.pad(blk, ((0, 0), (0, cfg.g[n][1] - wd)))
        parts.append(blk)
    return jnp.concatenate(parts, axis=1)


def chip_blocks(cfg, groups, n_chips):
    s = cfg.IN_COLS // n_chips
    blocks = []
    for j in range(n_chips):
        lo, hi = j * s, (j + 1) * s
        pieces = []
        for name, (off, wd) in cfg.orig.items():
            a, b = max(lo, off), min(hi, off + wd)
            if a < b:
                pieces.append(groups[name][:, a - off:b - off])
        blocks.append(jnp.concatenate(pieces, axis=1))
    return jnp.stack(blocks)


def _layer_params(cfg, prm):
    return dict(
        nw=prm['norm_w'].reshape(1, -1),
        qnw=prm['q_norm_w'].reshape(1, -1), knw=prm['k_norm_w'].reshape(1, -1),
        sinks_row=jnp.repeat(prm['sinks'], A_HEAD).reshape(1, -1),
        cw=prm['b_conv_w'], cb=prm['b_conv_b'].reshape(1, -1),
        lw=prm['b_ln_w'].reshape(1, -1), lb=prm['b_ln_b'].reshape(1, -1),
        pw=prm['b_pw_w'], pb=prm['b_pw_b'].reshape(1, -1),
        ccw=prm['c_conv_w'], alog=_pad_lanes(prm['c_a_log']), dtb=_pad_lanes(prm['c_dt_bias']),
        onw=prm['c_onorm_w'].reshape(1, -1),
    )


def _attn_io(cfg, p, cos, sin, grads):
    gq = BF16 if grads else None
    rows = [Row(p, cfg.AW, cfg.blk('qa'), gq), Row(p, cfg.AW, cfg.blk('za'), gq),
            Row(p, cfg.AKW, cfg.blk('ka'), gq), Row(p, cfg.AKW, cfg.blk('va'), gq),
            Row(cos, A_HEAD), Row(sin, A_HEAD)]
    halos = [Halo(p, cfg.AKW, cfg.blk('ka'), ATTN_BLOCK, 2 if grads else None),
             Halo(p, cfg.AKW, cfg.blk('va'), ATTN_BLOCK, 3 if grads else None),
             Halo(cos, A_HEAD, 0, ATTN_BLOCK), Halo(sin, A_HEAD, 0, ATTN_BLOCK)]
    return rows, halos


def _conv_io(cfg, p, grads):
    gq = BF16 if grads else None
    rows = [Row(p, 2 * cfg.BW, cfg.blk('ub'), gq), Row(p, cfg.BW, cfg.blk('zb'), gq)]
    halos = [Halo(p, 2 * cfg.BW, cfg.blk('ub'), B_HALO, 0 if grads else None)]
    return rows, halos


def _prep_io(cfg, p, grads):
    gq = BF16 if grads else None
    rows = [Row(p, cfg.CW, cfg.blk(n), gq) for n in ('qc', 'kc', 'vc')]
    rows += [Row(p, LANE, cfg.blk('bc'), gq), Row(p, LANE, cfg.blk('ac'), gq)]
    halos = [Halo(p, cfg.CW, cfg.blk(n), C_HALO, k if grads else None) for k, n in enumerate(('qc', 'kc', 'vc'))]
    return rows, halos


TB_CONV = 128
TB_PREP = 256
TB_OUT = 512
TB_INTRA_FWD = 256
TB_INTRA_BWD = 256


def layer_forward(cfg, l, x, lp, wp, wo, cos, sin, comms=None, target=None):
    n = x.shape[0]
    nseq = n // cfg.T
    T = cfg.T
    comms = comms or {}
    carried = {}

    def hosted(key, res):
        if comms.get(key) is None:
            return res
        res, carried[key] = res
        return res

    nw = lp['nw'] if isinstance(lp, dict) else lp[0]
    p, h, *rode = norm_in_proj(f"in_proj_{l}", x, nw, wp, 1024, _tile(cfg.WP, 768), comm=comms.get('in_proj'))
    if rode:
        carried['in_proj'] = rode[0]
    if not isinstance(lp, dict):
        lp, wo = lp[1](carried['in_proj'])
    rows, halos = _attn_io(cfg, p, cos, sin, False)
    (oa,) = hosted('attn', rb_fwd(f"attn_fwd_{l}", functools.partial(attn_block, cfg), n, ATTN_BLOCK, T // ATTN_BLOCK,
                                  rows, halos, [lp['qnw'], lp['knw'], lp['sinks_row']], [(cfg.AW, BF16)],
                                  comm=comms.get('attn')))
    rows, halos = _conv_io(cfg, p, False)
    tbb = min(TB_CONV, T)
    (ob,) = hosted('conv', rb_fwd(f"conv_fwd_{l}", functools.partial(conv_block, cfg), n, tbb, T // tbb, rows, halos,
                                  [lp['cw'], lp['cb'], lp['lw'], lp['lb'], lp['pw'], lp['pb']], [(cfg.BW, BF16)],
                                  comm=comms.get('conv')))
    rows, halos = _prep_io(cfg, p, False)
    tbp = min(TB_PREP, T)
    qn, kn, v, g, beta = rb_fwd(f"gdn_prep_fwd_{l}", functools.partial(gdn_prep_block, cfg), n, tbp, T // tbp, rows,
                                halos, [lp['ccw'], lp['alog'], lp['dtb']],
                                [(cfg.CW, F32)] * 3 + [(LANE, F32)] * 2)
    intra_outs = hosted('intra', rb_fwd(
        f"gdn_intra_fwd_{l}", functools.partial(gdn_intra_rows, cfg), n, min(TB_INTRA_FWD, T), T // min(TB_INTRA_FWD, T),
        [Row(qn, cfg.CW), Row(kn, cfg.CW), Row(v, cfg.CW), Row(g, LANE), Row(beta, LANE)], [], [],
        [(cfg.CW, F32)] * 4 + [(cfg.CH * CHUNK, F32), (LANE, F32), (cfg.CH * CHUNK, F32)], comm=comms.get('intra')))
    intra_outs, inv = intra_outs[:6], intra_outs[6]
    o, s_in = gdn_scan_fwd(f"gdn_scan_fwd_{l}", cfg, nseq, *intra_outs)
    tbo = min(TB_OUT, T)
    (oc,) = rb_fwd(f"gdn_out_fwd_{l}", functools.partial(gdn_out_block, cfg), n, tbo, T // tbo,
                   [Row(o, cfg.CW), Row(p, cfg.CW, cfg.blk('zc'))], [], [lp['onw']], [(cfg.CW, BF16)])
    y = jnp.concatenate([oa, ob, oc], axis=1)
    if target is None:
        x_next = matmul(f"out_proj_{l}", y, wo, 'nn', 1024, 1024, cfg.D, add=x)
    else:
        x_next = out_proj_loss(f"out_proj_{l}", y, wo, x, target, 512)
    saved = dict(x=x, p=p, h=h, y=y, qn=qn, kn=kn, v=v, g=g, beta=beta, intra_outs=intra_outs, inv=inv, s_in=s_in, o=o,
                 lp=lp, wo=wo)
    return x_next, saved, carried


def layer_backward(cfg, l, dxn, sv, lp, wb, wo, cos, sin, rs=None, own_rs=None):
    n = dxn.shape[0]
    nseq = n // cfg.T
    T = cfg.T
    p = sv['p']
    AW, BW, CW = cfg.AW, cfg.BW, cfg.CW
    dy = matmul(f"dy_{l}", dxn, wo, 'nt', 1024, 1024, cfg.D)
    dwo = matmul(f"dwo_{l}", sv['y'], dxn, 'tn', 1024, 1024, 2048)
    doa, dob, doc = dy[:, :AW], dy[:, AW:AW + BW], dy[:, AW + BW:]
    tbo = min(TB_OUT, T)
    do, dzc, donw = rb_bwd(f"gdn_out_bwd_{l}", functools.partial(gdn_out_block, cfg), n, tbo, T // tbo,
                           [Row(sv['o'], CW, 0, F32), Row(p, CW, cfg.blk('zc'), BF16)], [], [lp['onw']], [doc], [True])
    dintra, got_rest = gdn_scan_bwd(f"gdn_scan_bwd_{l}", cfg, nseq, *sv['intra_outs'], sv['s_in'], do,
                                    comm=None if rs is None else rs.scatter([1, 2]))
    dqn, dkn, dv, dg, dbeta = rb_bwd(
        f"gdn_intra_bwd_{l}", functools.partial(gdn_intra_rows, cfg), n, min(TB_INTRA_BWD, T), T // min(TB_INTRA_BWD, T),
        [Row(sv['qn'], CW, 0, F32), Row(sv['kn'], CW, 0, F32), Row(sv['v'], CW, 0, F32), Row(sv['g'], LANE, 0, F32),
         Row(sv['beta'], LANE, 0, F32), Row(sv['inv'], cfg.CH * CHUNK)], [], [], list(dintra), [])
    rows, halos = _prep_io(cfg, p, True)
    tbp = min(TB_PREP, T)
    dqc, dkc, dvc, dbc, dac, dccw, dalog, ddtb = rb_bwd(
        f"gdn_prep_bwd_{l}", functools.partial(gdn_prep_block, cfg), n, tbp, T // tbp, rows, halos,
        [lp['ccw'], lp['alog'], lp['dtb']], [dqn, dkn, dv, dg, dbeta], [True] * 3)
    rows, halos = _conv_io(cfg, p, True)
    tbb = min(TB_CONV, T)
    conv_grads = rb_bwd(
        f"conv_bwd_{l}", functools.partial(conv_block, cfg), n, tbb, T // tbb, rows, halos,
        [lp['cw'], lp['cb'], lp['lw'], lp['lb'], lp['pw'], lp['pb']], [dob], [True] * 6,
        comm=None if rs is None else rs.scatter([0]))
    got = None
    if rs is not None:
        conv_grads, got_w_in = conv_grads
        got = got_w_in + got_rest
    dub, dzb, dcw, dcb, dlw, dlb, dpw, dpb = conv_grads
    rows, halos = _attn_io(cfg, p, cos, sin, True)
    dqa, dza, dka, dva, dqnw, dknw, dsinks_row = rb_bwd(
        f"attn_bwd_{l}", functools.partial(attn_block, cfg), n, ATTN_BLOCK, T // ATTN_BLOCK, rows, halos,
        [lp['qnw'], lp['knw'], lp['sinks_row']], [doa], [True] * 3)
    dgroups = dict(qa=dqa, za=dza, qc=dqc, kc=dkc, vc=dvc, zc=dzc, ka=dka, va=dva, ub=dub, zb=dzb, bc=dbc, ac=dac)
    dp_blocks = chip_blocks(cfg, dgroups, N_CHIPS)
    dw_in, dw_in_lo = grad_w_blocks(f"dwp_{l}", sv['h'], dp_blocks, 1024, 2048)
    mine = None if own_rs is None else own_rs(dict(w_in_blocks=dw_in, w_in_lo=dw_in_lo, w_out=dwo, b_pw_w=dpw))
    got_mine = None
    if mine is None:
        dh = grad_h_blocks(f"dh_{l}", dp_blocks, wb, 1024, 1024)
        dx, dnw = norm_bwd(f"norm_bwd_{l}", sv['x'], lp['nw'], dh, dxn, 256)
    elif l > 0:
        dh, mine_received = grad_h_blocks(f"dh_{l}", dp_blocks, wb, 1024, 1024, comm=mine.swap())
        mine.add(mine_received)
        dx, dnw = norm_bwd(f"norm_bwd_{l}", sv['x'], lp['nw'], dh, dxn, 256)
    else:
        mine.add(run_comm(f"rs{l}_swap_halves", mine.swap()))
        dh, got_w_in = grad_h_blocks(f"dh_{l}", dp_blocks, wb, 1024, 1024, comm=mine.scatter([0]))
        dx, dnw, got_others = norm_bwd(f"norm_bwd_{l}", sv['x'], lp['nw'], dh, dxn, 256, comm=mine.scatter([1, 2]))
        got_mine = got_w_in + got_others
    grads = dict(
        norm_w=dnw[0], w_in_blocks=dw_in, q_norm_w=dqnw[0], k_norm_w=dknw[0],
        sinks=dsinks_row.reshape(cfg.AQH, A_HEAD)[:, 0],
        b_conv_w=dcw, b_conv_b=dcb[0], b_ln_w=dlw[0], b_ln_b=dlb[0], b_pw_w=dpw, b_pw_b=dpb[0],
        c_conv_w=dccw, c_a_log=dalog[0, :cfg.CH], c_dt_bias=ddtb[0, :cfg.CH], c_onorm_w=donw[0], w_out=dwo)
    return dx, grads, (got, mine, got_mine)


def rope_for(cfg, positions):
    n = positions.size
    inv_freq = ROPE_THETA ** (-np.arange(0, ROT_DIM, 2, dtype=np.float32) / ROT_DIM)
    freq_row = np.zeros((1, A_HEAD), np.float32)
    freq_row[0, :ROT_DIM] = np.concatenate([inv_freq, inv_freq])
    return rope_tables("rope_tables", positions.reshape(n, 1), jnp.asarray(freq_row))


def local_step(cfg, x, positions, prm, wps, wos, target):
    nseq = x.shape[0]
    n = nseq * cfg.T
    cos, sin = rope_for(cfg, positions)
    lps = [_layer_params(cfg, {k: v[l] for k, v in prm.items()}) for l in range(DEPTH)]
    saved = []
    xl = x.reshape(n, cfg.D)
    for l in range(DEPTH):
        xl, sv, _ = layer_forward(cfg, l, xl, lps[l], wps[l], wos[l], cos, sin,
                                  target=target.reshape(n, cfg.D) if l == DEPTH - 1 else None)
        saved.append(sv)
    dx, loss = xl
    grads = [None] * DEPTH
    for l in reversed(range(DEPTH)):
        groups = {k: wps[l][:, off:off + wd] for k, (off, wd) in cfg.g.items()}
        wb = chip_blocks(cfg, groups, N_CHIPS)
        dx, grads[l], _ = layer_backward(cfg, l, dx, saved[l], lps[l], wb, wos[l], cos, sin)
    return loss, dx.reshape(x.shape), grads


N_CHIPS = 4
N_DEV = 8


def _place():
    return lax.axis_index("x"), lax.axis_index("y"), lax.axis_index("c")


def _other_chips(x, y):
    return [(1 - x, y), (x, 1 - y), (1 - x, 1 - y)]


def _remote(src, dst, send, recv, to):
    return pltpu.make_async_remote_copy(src_ref=src, dst_ref=dst, send_sem=send, recv_sem=recv, device_id=to,
                                        device_id_type=MESH)


def gather_comm(arrs):
    n = len(arrs)

    def half(c):
        return [pl.ds(c * (a.shape[0] // 2), a.shape[0] // 2) for a in arrs]

    def first_copies(ins, outs, send, recv):
        x, y, c = _place()
        me = 2 * x + y
        mine = half(c)
        return [_remote(ins[i].at[mine[i]], outs[i].at[me, mine[i]], send.at[i, j], recv.at[i, j], (cx, cy, c))
                for i in range(n) for j, (cx, cy) in enumerate(_other_chips(x, y))]

    def start(ins, outs, sems):
        for cp in first_copies(ins, outs, *sems):
            cp.start()

    def finish(ins, outs, sems):
        send, recv = sems
        x, y, c = _place()
        chips = _other_chips(x, y)
        sib = (x, y, 1 - c)
        passed = []
        mine, other = half(c), half(1 - c)
        for i in range(n):
            for j, (cx, cy) in enumerate(chips):
                blk = outs[i].at[2 * cx + cy, mine[i]]
                _remote(blk, blk, send.at[i, j], recv.at[i, j], (x, y, c)).wait_recv()
                cp = _remote(blk, blk, send.at[i, 3 + j], recv.at[i, 3 + j], sib)
                cp.start()
                passed.append(cp)
        for i in range(n):
            for j, (cx, cy) in enumerate(chips):
                blk = outs[i].at[2 * cx + cy, other[i]]
                _remote(blk, blk, send.at[i, 3 + j], recv.at[i, 3 + j], sib).wait_recv()
        for cp in first_copies(ins, outs, send, recv) + passed:
            cp.wait_send()

    return Comm(arrs, [jax.ShapeDtypeStruct((N_CHIPS,) + a.shape, a.dtype) for a in arrs],
                [pltpu.SemaphoreType.DMA((n, 6)), pltpu.SemaphoreType.DMA((n, 6))], start, finish)


def fill_own(gathered, arrs):
    me = 2 * lax.axis_index("x") + lax.axis_index("y")
    return [lax.dynamic_update_index_in_dim(o, a, me, 0) for o, a in zip(gathered, arrs)]


def swap_comm(arrs):
    n = len(arrs)

    def copies(ins, outs, send, recv):
        x, y, c = _place()
        return [_remote(ins[i].at[:, 1 - c], outs[i], send.at[i], recv.at[i], (x, y, 1 - c)) for i in range(n)]

    def start(ins, outs, sems):
        for cp in copies(ins, outs, *sems):
            cp.start()

    def finish(ins, outs, sems):
        for cp in copies(ins, outs, *sems):
            cp.wait()

    return Comm(arrs, [jax.ShapeDtypeStruct((a.shape[0],) + a.shape[2:], a.dtype) for a in arrs],
                [pltpu.SemaphoreType.DMA((n,)), pltpu.SemaphoreType.DMA((n,))], start, finish)


def scatter_comm(arrs):
    n = len(arrs)

    def copies(ins, outs, send, recv):
        x, y, c = _place()
        return [_remote(ins[i].at[2 * cx + cy], outs[i].at[j], send.at[i, j], recv.at[i, j], (cx, cy, c))
                for i in range(n) for j, (cx, cy) in enumerate(_other_chips(x, y))]

    def start(ins, outs, sems):
        for cp in copies(ins, outs, *sems):
            cp.start()

    def finish(ins, outs, sems):
        send, recv = sems
        x, y, c = _place()
        for i in range(n):
            for j in range(3):
                blk = outs[i].at[j]
                _remote(blk, blk, send.at[i, j], recv.at[i, j], (x, y, c)).wait_recv()
        for cp in copies(ins, outs, send, recv):
            cp.wait_send()

    return Comm(arrs, [jax.ShapeDtypeStruct((3,) + a.shape[1:], a.dtype) for a in arrs],
                [pltpu.SemaphoreType.DMA((n, 3)), pltpu.SemaphoreType.DMA((n, 3))], start, finish)


def share_comm(arrs):
    n = len(arrs)

    def copies(outs, send, recv):
        x, y, c = _place()
        return [_remote(outs[i].at[c], outs[i].at[c], send.at[i], recv.at[i], (x, y, 1 - c)) for i in range(n)]

    def start(ins, outs, sems):
        for cp in copies(outs, *sems):
            cp.start()

    def finish(ins, outs, sems):
        send, recv = sems
        x, y, c = _place()
        for i in range(n):
            blk = outs[i].at[1 - c]
            _remote(blk, blk, send.at[i], recv.at[i], (x, y, c)).wait_recv()
        for cp in copies(outs, send, recv):
            cp.wait_send()

    return Comm(arrs, [jax.ShapeDtypeStruct(a.shape, a.dtype) for a in arrs],
                [pltpu.SemaphoreType.DMA((n,)), pltpu.SemaphoreType.DMA((n,))], start, finish,
                aliases={i: i for i in range(n)})


def all_reduce_small(name, packed):
    r = packed.shape[0]

    def body(in_ref, out_ref, buf, send, recv):
        x, y, c = _place()
        me = 4 * x + 2 * y + c
        buf[me] = in_ref[...]
        flips = [(fx, fy, fc) for fx in (0, 1) for fy in (0, 1) for fc in (0, 1) if (fx, fy, fc) != (0, 0, 0)]
        peers = [((x + fx) % 2, (y + fy) % 2, (c + fc) % 2) for fx, fy, fc in flips]
        cps = [_remote(in_ref, buf.at[me], send.at[k], recv.at[k], peer) for k, peer in enumerate(peers)]
        for cp in cps:
            cp.start()
        for k, (px, py, pc) in enumerate(peers):
            blk = buf.at[4 * px + 2 * py + pc]
            _remote(blk, blk, send.at[k], recv.at[k], (x, y, c)).wait_recv()
        for cp in cps:
            cp.wait_send()
        acc = buf[0]
        for d in range(1, N_DEV):
            acc = acc + buf[d]
        out_ref[...] = acc

    vm = pl.BlockSpec(memory_space=pltpu.VMEM)
    return pl.pallas_call(
        body, name=name, in_specs=[vm], out_specs=vm, out_shape=jax.ShapeDtypeStruct(packed.shape, F32),
        scratch_shapes=[pltpu.VMEM((N_DEV, r, LANE), F32), pltpu.SemaphoreType.DMA((N_DEV - 1,)),
                        pltpu.SemaphoreType.DMA((N_DEV - 1,))],
    )(packed)


def add_own_half(name, g, a, c_idx, tr):
    nch, _, r, cc = g.shape
    tr = min(tr, r)

    def body(c_ref, g_ref, a_ref, o_ref):
        o_ref[...] = (g_ref[0] + a_ref[...].astype(F32)).astype(o_ref.dtype)

    return pl.pallas_call(
        body, name=name,
        grid_spec=pltpu.PrefetchScalarGridSpec(
            num_scalar_prefetch=1, grid=(nch, r // tr),
            in_specs=[pl.BlockSpec((1, 1, tr, cc), lambda j, i, c_ref: (j, c_ref[0], i, 0)),
                      pl.BlockSpec((1, tr, cc), lambda j, i, c_ref: (j, i, 0))],
            out_specs=pl.BlockSpec((1, tr, cc), lambda j, i, c_ref: (j, i, 0))),
        out_shape=jax.ShapeDtypeStruct(a.shape, BF16),
        compiler_params=_cparams(("parallel", "parallel")),
    )(c_idx, g, a)


def sum_chips(name, p, b, idx, tr):
    _, r, cc = p.shape
    tr = min(tr, r)

    def body(idx_ref, p_ref, b_ref, o_ref):
        acc = p_ref[0].astype(F32)
        for k in range(3):
            acc = acc + b_ref[k].astype(F32)
        o_ref[0] = acc

    return pl.pallas_call(
        body, name=name,
        grid_spec=pltpu.PrefetchScalarGridSpec(
            num_scalar_prefetch=1, grid=(r // tr,),
            in_specs=[pl.BlockSpec((1, tr, cc), lambda i, s: (s[0], i, 0)),
                      pl.BlockSpec((3, tr, cc), lambda i, s: (0, i, 0))],
            out_specs=pl.BlockSpec((1, tr, cc), lambda i, s: (s[1], i, 0))),
        out_shape=jax.ShapeDtypeStruct((2, r, cc), F32),
        compiler_params=_cparams(("parallel",)),
    )(idx, p, b)


class GradReduce:
    def __init__(self, tag, parts, parts_lo, chip, c_idx):
        self.tag, self.c_idx = tag, c_idx
        halves = lambda p: p.reshape(p.shape[0], 2, p.shape[1] // 2, p.shape[2])
        self.parts = [halves(p) for p in parts]
        self.parts_lo = [halves(p) for p in parts_lo]
        self.idx = jnp.concatenate([chip.astype(jnp.int32).reshape(1), c_idx])

    def swap(self):
        return swap_comm(self.parts_lo)

    def add(self, received):
        self.part = [add_own_half(f"rs{self.tag}_add_sibling_{t}", g, a, self.c_idx, 128)
                     for t, (g, a) in enumerate(zip(self.parts, received))]

    def scatter(self, which=None):
        return scatter_comm(self.part if which is None else [self.part[t] for t in which])

    def finish(self, got):
        red = [sum_chips(f"rs{self.tag}_sum_chips_{t}", p, b, self.idx, 128) for t, (p, b) in enumerate(zip(self.part, got))]
        out = run_comm(f"rs{self.tag}_share_halves", share_comm(red))
        return [o.reshape(-1, o.shape[-1]) for o in out]


def adamw_many(name, ws, gs, ms, vs):
    n = len(ws)

    def body(*refs):
        for i in range(n):
            w_ref, g_ref, m_ref, v_ref = (refs[k * n + i] for k in range(4))
            d_ref, mo_ref, vo_ref = (refs[(4 + k) * n + i] for k in range(3))
            g = g_ref[...]
            m = ADAM_B1 * m_ref[...] + (1.0 - ADAM_B1) * g
            v = ADAM_B2 * v_ref[...] + (1.0 - ADAM_B2) * jnp.square(g)
            m_hat = m / (1.0 - ADAM_B1 ** ADAM_STEP)
            v_hat = v / (1.0 - ADAM_B2 ** ADAM_STEP)
            d_ref[...] = -ADAM_LR * (m_hat / (jnp.sqrt(v_hat) + ADAM_EPS) + ADAM_WD * w_ref[...])
            mo_ref[...] = m
            vo_ref[...] = v

    vm = pl.BlockSpec(memory_space=pltpu.VMEM)
    return pl.pallas_call(
        body, name=name, in_specs=[vm] * (4 * n), out_specs=[vm] * (3 * n),
        out_shape=[jax.ShapeDtypeStruct(a.shape, F32) for a in ws] * 3,
    )(*ws, *gs, *ms, *vs)


def adamw_layers(name, w, g0, g1, m, v, tb):
    _, r, cc = w.shape
    tb = min(tb, r)
    nb = r // tb

    def body(w_ref, g0_ref, g1_ref, m_ref, v_ref, g_ref, d_ref, mo_ref, vo_ref):
        g = jnp.where(pl.program_id(0) == 0, g0_ref[...], g1_ref[...])
        m = ADAM_B1 * m_ref[0] + (1.0 - ADAM_B1) * g
        v = ADAM_B2 * v_ref[0] + (1.0 - ADAM_B2) * jnp.square(g)
        m_hat = m / (1.0 - ADAM_B1 ** ADAM_STEP)
        v_hat = v / (1.0 - ADAM_B2 ** ADAM_STEP)
        g_ref[0] = g
        d_ref[0] = -ADAM_LR * (m_hat / (jnp.sqrt(v_hat) + ADAM_EPS) + ADAM_WD * w_ref[0])
        mo_ref[0] = m
        vo_ref[0] = v

    spec = pl.BlockSpec((1, tb, cc), lambda l, i: (l, i, 0))
    g0_spec = pl.BlockSpec((tb, cc), lambda l, i: (jnp.where(l == 0, i, nb - 1), 0))
    g1_spec = pl.BlockSpec((tb, cc), lambda l, i: (jnp.where(l == 1, i, 0), 0))
    return pl.pallas_call(
        body, name=name, grid=(2, nb), in_specs=[spec, g0_spec, g1_spec, spec, spec], out_specs=[spec] * 4,
        out_shape=[jax.ShapeDtypeStruct(w.shape, F32)] * 4,
        compiler_params=_cparams(("arbitrary", "arbitrary")),
    )(w, g0, g1, m, v)


def adamw_cols_major(name, w, g0, g1, m, v, tb=LANE):
    wt, mt, vt = (jnp.transpose(a, (2, 0, 1)) for a in (w, m, v))
    cc, _, r = wt.shape

    def body(w_ref, g0_ref, g1_ref, m_ref, v_ref, g_ref, d_ref, mo_ref, vo_ref):
        for l, gl_ref in enumerate((g0_ref, g1_ref)):
            g = gl_ref[...].T
            m = ADAM_B1 * m_ref[:, l, :] + (1.0 - ADAM_B1) * g
            v = ADAM_B2 * v_ref[:, l, :] + (1.0 - ADAM_B2) * jnp.square(g)
            m_hat = m / (1.0 - ADAM_B1 ** ADAM_STEP)
            v_hat = v / (1.0 - ADAM_B2 ** ADAM_STEP)
            g_ref[:, l, :] = g
            d_ref[:, l, :] = -ADAM_LR * (m_hat / (jnp.sqrt(v_hat) + ADAM_EPS) + ADAM_WD * w_ref[:, l, :])
            mo_ref[:, l, :] = m
            vo_ref[:, l, :] = v

    spec = pl.BlockSpec((tb, 2, r), lambda i: (i, 0, 0))
    gspec = pl.BlockSpec((r, tb), lambda i: (0, i))
    outs = pl.pallas_call(
        body, name=name, grid=(pl.cdiv(cc, tb),), in_specs=[spec, gspec, gspec, spec, spec], out_specs=[spec] * 4,
        out_shape=[jax.ShapeDtypeStruct(wt.shape, F32)] * 4,
        compiler_params=_cparams(("parallel",)),
    )(wt, g0, g1, mt, vt)
    return [jnp.transpose(o, (1, 2, 0)) for o in outs]


def _pack(arrs):
    flat = jnp.concatenate([a.reshape(-1).astype(F32) for a in arrs])
    pad = (-flat.shape[0]) % (8 * LANE)
    return jnp.pad(flat, (0, pad)).reshape(-1, LANE)


def _unpack(packed, shapes):
    flat = packed.reshape(-1)
    out, off = [], 0
    for s in shapes:
        size = math.prod(s)
        out.append(flat[off:off + size].reshape(s))
        off += size
    return out


BIG = ('w_in', 'w_out', 'b_pw_w')
SMALL = tuple(k for k in WEIGHTS if k not in BIG)
CHIP_SHARDED_SMALL = {'b_conv_w': 2, 'c_conv_w': 2}


def kernel(x, positions, norm_w, w_in, q_norm_w, k_norm_w, sinks, b_conv_w, b_conv_b, b_ln_w, b_ln_b, b_pw_w, b_pw_b, c_conv_w, c_a_log, c_dt_bias, c_onorm_w, w_out, loss_target, m_norm_w, m_w_in, m_q_norm_w, m_k_norm_w, m_sinks, m_b_conv_w, m_b_conv_b, m_b_ln_w, m_b_ln_b, m_b_pw_w, m_b_pw_b, m_c_conv_w, m_c_a_log, m_c_dt_bias, m_c_onorm_w, m_w_out, v_norm_w, v_w_in, v_q_norm_w, v_k_norm_w, v_sinks, v_b_conv_w, v_b_conv_b, v_b_ln_w, v_b_ln_b, v_b_pw_w, v_b_pw_b, v_c_conv_w, v_c_a_log, v_c_dt_bias, v_c_onorm_w, v_w_out):
    cfg = Cfg(x.shape[-1], x.shape[-2])
    w = dict(norm_w=norm_w, w_in=w_in, q_norm_w=q_norm_w, k_norm_w=k_norm_w, sinks=sinks, b_conv_w=b_conv_w,
             b_conv_b=b_conv_b, b_ln_w=b_ln_w, b_ln_b=b_ln_b, b_pw_w=b_pw_w, b_pw_b=b_pw_b, c_conv_w=c_conv_w,
             c_a_log=c_a_log, c_dt_bias=c_dt_bias, c_onorm_w=c_onorm_w, w_out=w_out)
    m = dict(norm_w=m_norm_w, w_in=m_w_in, q_norm_w=m_q_norm_w, k_norm_w=m_k_norm_w, sinks=m_sinks,
             b_conv_w=m_b_conv_w, b_conv_b=m_b_conv_b, b_ln_w=m_b_ln_w, b_ln_b=m_b_ln_b, b_pw_w=m_b_pw_w,
             b_pw_b=m_b_pw_b, c_conv_w=m_c_conv_w, c_a_log=m_c_a_log, c_dt_bias=m_c_dt_bias, c_onorm_w=m_c_onorm_w,
             w_out=m_w_out)
    v = dict(norm_w=v_norm_w, w_in=v_w_in, q_norm_w=v_q_norm_w, k_norm_w=v_k_norm_w, sinks=v_sinks,
             b_conv_w=v_b_conv_w, b_conv_b=v_b_conv_b, b_ln_w=v_b_ln_w, b_ln_b=v_b_ln_b, b_pw_w=v_b_pw_w,
             b_pw_b=v_b_pw_b, c_conv_w=v_c_conv_w, c_a_log=v_c_a_log, c_dt_bias=v_c_dt_bias, c_onorm_w=v_c_onorm_w,
             w_out=v_w_out)
    chip = 2 * lax.axis_index("x") + lax.axis_index("y")
    c_idx = lax.axis_index("c").astype(jnp.int32).reshape(1)
    D, T = cfg.D, cfg.T
    nseq = x.shape[0]
    n = nseq * T
    w_in_b, w_out_b = w_in.astype(BF16), w_out.astype(BF16)

    def permuted(g_in):
        return permute_w_in(cfg, jnp.concatenate(list(g_in), axis=1))

    def layer_prm(l, g_pw, g_bcw, g_ccw):
        prm = {k: w[k][l] for k in SMALL}
        prm['b_pw_w'] = g_pw.reshape(cfg.BW, cfg.BW)
        prm['b_conv_w'] = jnp.concatenate(list(g_bcw[:, l]), axis=1)
        prm['c_conv_w'] = jnp.concatenate(list(g_ccw[:, l]), axis=1)
        return _layer_params(cfg, prm)

    (g_in0,) = fill_own(run_comm("gather_weights_0", gather_comm([w_in_b[0]])), [w_in_b[0]])
    cos, sin = rope_for(cfg, positions)
    early = [w_out_b[0], b_pw_w[0], b_conv_w, c_conv_w]
    top, bottom = w_in_b[1][:D // 2], w_in_b[1][D // 2:]
    late = [w_out_b[1], b_pw_w[1]]
    conv_ws = {}

    def layer0_rest(rode):
        g_out0, g_pw0, conv_ws['b'], conv_ws['c'] = fill_own(rode, early)
        return layer_prm(0, g_pw0, conv_ws['b'], conv_ws['c']), g_out0.reshape(D, D)

    x1, sv0, rode = layer_forward(
        cfg, 0, x.reshape(n, D), (norm_w[0].reshape(1, -1), layer0_rest), permuted(g_in0), None, cos, sin,
        comms=dict(in_proj=gather_comm(early), attn=gather_comm([top]), conv=gather_comm(late),
                   intra=gather_comm([bottom])))
    lp0, wo0 = sv0['lp'], sv0['wo']
    (g_top,), (g_bottom,) = fill_own(rode['attn'], [top]), fill_own(rode['intra'], [bottom])
    g_out1, g_pw1 = fill_own(rode['conv'], late)
    g_in1 = jnp.concatenate([g_top, g_bottom], axis=1)
    wp1, wo1 = permuted(g_in1), g_out1.reshape(D, D)
    lp1 = layer_prm(1, g_pw1, conv_ws['b'], conv_ws['c'])
    (dx2, loss_local), sv1, _ = layer_forward(cfg, 1, x1, lp1, wp1, wo1, cos, sin, target=loss_target.reshape(n, D))

    def partials(gr):
        return [gr['w_in_blocks'], gr['w_out'].reshape(N_CHIPS, D // N_CHIPS, D),
                gr['b_pw_w'].reshape(N_CHIPS, cfg.BW // N_CHIPS, cfg.BW)]

    def reduce_scatter(tag, gr):
        f32 = partials(gr)
        return GradReduce(tag, f32, [gr['w_in_lo']] + [a.astype(BF16) for a in f32[1:]], chip, c_idx)

    dx1, gr1, (_, rs1, _) = layer_backward(cfg, 1, dx2, sv1, lp1, g_in1, wo1, cos, sin,
                                        own_rs=functools.partial(reduce_scatter, 1))
    dx0, gr0, (got1, rs0, got0) = layer_backward(cfg, 0, dx1, sv0, lp0, g_in0, wo0, cos, sin, rs=rs1,
                                           own_rs=functools.partial(reduce_scatter, 0))
    red1 = rs1.finish(got1)
    red0 = rs0.finish(got0)
    grad_x = dx0.reshape(x.shape)
    grads = [gr0, gr1]

    small_parts = [_stack([grads[l][k] for l in range(DEPTH)]) for k in SMALL] + [loss_local.reshape(1)]
    *small_red, loss = _unpack(all_reduce_small("all_reduce_small", _pack(small_parts)), [a.shape for a in small_parts])
    loss = loss.reshape(())
    g = {}
    for k, a in zip(SMALL, small_red):
        if k in CHIP_SHARDED_SMALL:
            ax = CHIP_SHARDED_SMALL[k]
            width = a.shape[ax] // N_CHIPS
            a = lax.dynamic_slice_in_dim(a, chip * width, width, axis=ax)
        g[k] = a

    delta, new_m, new_v = {}, {}, {}
    for k, g0, g1 in zip(BIG, red0, red1):
        update = adamw_layers if w[k].shape[-1] % LANE == 0 else adamw_cols_major
        g[k], delta[k], new_m[k], new_v[k] = update(f"adamw_{k}", w[k], g0, g1, m[k], v[k], 128)
    outs = adamw_many("adamw_small", *[[d[k] for k in SMALL] for d in (w, g, m, v)])
    for i, k in enumerate(SMALL):
        delta[k], new_m[k], new_v[k] = outs[i], outs[len(SMALL) + i], outs[2 * len(SMALL) + i]
    return (loss, grad_x, *[g[k] for k in WEIGHTS], *[delta[k] for k in WEIGHTS], *[new_m[k] for k in WEIGHTS],
            *[new_v[k] for k in WEIGHTS])
```

```python
import functools
import math

import numpy as np
import jax
import jax.numpy as jnp
from jax import lax
from jax.experimental import pallas as pl
from jax.experimental.pallas import tpu as pltpu

F32 = jnp.float32
BF16 = jnp.bfloat16
HI = lax.Precision.HIGHEST
MESH = pl.DeviceIdType.MESH

DEPTH = 2
A_HEAD = 64
A_GROUP = 3
ATTN_BLOCK = 128
ROT_DIM = 16
ROPE_THETA = 500000.0
B_CONV = 31
B_HALO = 32
C_HEAD = 128
C_CONV = 4
C_HALO = 8
CHUNK = 64
EPS = 1e-6
LANE = 128

ADAM_LR = 0.001
ADAM_B1 = 0.9
ADAM_B2 = 0.999
ADAM_EPS = 1e-08
ADAM_WD = 0.01
ADAM_STEP = 10

VMEM_LIMIT = 56 * 1024 * 1024

WEIGHTS = ['norm_w', 'w_in', 'q_norm_w', 'k_norm_w', 'sinks', 'b_conv_w', 'b_conv_b', 'b_ln_w', 'b_ln_b',
           'b_pw_w', 'b_pw_b', 'c_conv_w', 'c_a_log', 'c_dt_bias', 'c_onorm_w', 'w_out']


class Cfg:
    def __init__(self, d_model=2048, seq=2048):
        self.D = d_model
        self.T = seq
        self.AW = 3 * d_model // 8
        self.AQH = self.AW // A_HEAD
        self.AKH = self.AQH // A_GROUP
        self.AKW = self.AKH * A_HEAD
        self.BW = d_model // 4
        self.CH = (d_model - self.AW - self.BW) // C_HEAD
        self.CW = self.CH * C_HEAD
        AW, AKW, BW, CW, CH = self.AW, self.AKW, self.BW, self.CW, self.CH
        orig = [('qa', AW), ('ka', AKW), ('va', AKW), ('za', AW), ('ub', 2 * BW), ('zb', BW),
                ('qc', CW), ('kc', CW), ('vc', CW), ('bc', CH), ('ac', CH), ('zc', CW)]
        self.orig = {}
        off = 0
        for n, w in orig:
            self.orig[n] = (off, w)
            off += w
        self.IN_COLS = off
        order = ['qa', 'za', 'qc', 'kc', 'vc', 'zc', 'ka', 'va', 'ub', 'zb', 'bc', 'ac']
        self.order = order
        self.g = {}
        off = 0
        for n in order:
            w = self.orig[n][1]
            wp = LANE if n in ('bc', 'ac') else w
            assert off % wp == 0, (n, off, wp)
            self.g[n] = (off, wp)
            off += wp
        self.WP = off

    def blk(self, name):
        off, w = self.g[name]
        return off // w


def _cparams(sem, vmem=VMEM_LIMIT):
    return pltpu.CompilerParams(dimension_semantics=sem, vmem_limit_bytes=vmem)


def _silu(x):
    return x * jax.nn.sigmoid(x)


ANY = pl.BlockSpec(memory_space=pl.ANY)


class Comm:
    def __init__(self, ins, out_shapes, sems, start, finish, aliases=None):
        self.ins, self.out_shapes, self.sems = list(ins), list(out_shapes), list(sems)
        self.start, self.finish, self.aliases = start, finish, dict(aliases or {})


def call_with_comm(body, name, grid, in_specs, out_specs, out_shape, scratch_shapes, semantics, args, comm=None):
    in_specs, out_specs, out_shape, scratch_shapes = list(in_specs), list(out_specs), list(out_shape), list(scratch_shapes)
    if comm is None:
        outs = pl.pallas_call(body, name=name, grid=grid, in_specs=in_specs, out_specs=out_specs, out_shape=out_shape,
                              scratch_shapes=scratch_shapes, compiler_params=_cparams(semantics))(*args)
        return list(outs), []
    ni, no, ns = len(in_specs), len(out_specs), len(scratch_shapes)
    nci, nco = len(comm.ins), len(comm.out_shapes)

    def wrapped(*refs):
        h_in, c_in = refs[:ni], refs[ni:ni + nci]
        h_out, c_out = refs[ni + nci:ni + nci + no], refs[ni + nci + no:ni + nci + no + nco]
        h_scr, c_sems = refs[ni + nci + no + nco:ni + nci + no + nco + ns], refs[ni + nci + no + nco + ns:]
        ids = [pl.program_id(d) for d in range(len(grid))]
        first = functools.reduce(jnp.logical_and, [i == 0 for i in ids])
        last = functools.reduce(jnp.logical_and, [i == g - 1 for i, g in zip(ids, grid)])

        @pl.when(first)
        def _():
            comm.start(c_in, c_out, c_sems)

        body(*h_in, *h_out, *h_scr)

        @pl.when(last)
        def _():
            comm.finish(c_in, c_out, c_sems)

    outs = pl.pallas_call(
        wrapped, name=name, grid=grid, in_specs=in_specs + [ANY] * nci, out_specs=out_specs + [ANY] * nco,
        out_shape=out_shape + comm.out_shapes, scratch_shapes=scratch_shapes + comm.sems,
        input_output_aliases={ni + k: no + v for k, v in comm.aliases.items()},
        compiler_params=_cparams(("arbitrary",) * len(grid)),
    )(*args, *comm.ins)
    return list(outs[:no]), list(outs[no:])


def run_comm(name, comm):
    nci, nco = len(comm.ins), len(comm.out_shapes)

    def body(*refs):
        c_in, c_out, c_sems = refs[:nci], refs[nci:nci + nco], refs[nci + nco:]
        comm.start(c_in, c_out, c_sems)
        comm.finish(c_in, c_out, c_sems)

    return pl.pallas_call(
        body, name=name, in_specs=[ANY] * nci, out_specs=[ANY] * nco, out_shape=comm.out_shapes,
        scratch_shapes=comm.sems, input_output_aliases=comm.aliases,
    )(*comm.ins)


def _stack(xs, axis=0):
    assert axis == 0
    return jnp.concatenate([x[None] for x in xs], axis=0)


def _bdot(a, b, ca, cb, precision=HI):
    dims = (((ca,), (cb,)), ((0,), (0,)))
    if precision is HI and a.dtype == F32:
        ah = a.astype(BF16)
        bh = b.astype(BF16)
        al = (a - ah.astype(F32)).astype(BF16)
        bl = (b - bh.astype(F32)).astype(BF16)
        dg = lambda p, q: lax.dot_general(p, q, dims, preferred_element_type=F32)
        return dg(ah, bh) + (dg(ah, bl) + dg(al, bh))
    return lax.dot_general(a, b, dims, precision=precision, preferred_element_type=F32)


def _rope_matrix(nb):
    i = lax.broadcasted_iota(jnp.int32, (nb, A_HEAD, A_HEAD), 1)
    j = lax.broadcasted_iota(jnp.int32, (nb, A_HEAD, A_HEAD), 2)
    half = ROT_DIM // 2
    neg = (j < half) & (i == j + half)
    pos = (j >= half) & (j < ROT_DIM) & (i == j - half)
    return jnp.where(neg, -1.0, jnp.where(pos, 1.0, 0.0)).astype(F32)


def _norm_rope(xh, w, cos, sin):
    y = xh * lax.rsqrt(jnp.mean(xh * xh, axis=-1, keepdims=True) + EPS) * w
    return y * cos + _bdot(y, _rope_matrix(xh.shape[0]), 2, 1) * sin


def _sink_probs(s, sink):
    m = jnp.maximum(jnp.max(s, axis=-1, keepdims=True), sink)
    e = jnp.exp(s - m)
    es = jnp.exp(sink - m)
    den = jnp.sum(e, axis=-1, keepdims=True) + es
    return e / den, es / den


@jax.custom_vjp
def _sink_softmax(s, sink):
    return _sink_probs(s, sink)[0]


def _sink_softmax_fwd(s, sink):
    p, ps = _sink_probs(s, sink)
    return p, (p, ps)


def _sink_softmax_bwd(res, dp):
    p, ps = res
    inner = jnp.sum(p * dp, axis=-1, keepdims=True)
    return p * (dp - inner), -jnp.sum(ps * inner, axis=1, keepdims=True)


_sink_softmax.defvjp(_sink_softmax_fwd, _sink_softmax_bwd)


def attn_block(cfg, first, q, za, kc, vc, cosc, sinc, kp, vp, cosp, sinp, qnw, knw, sinks_row):
    blk = ATTN_BLOCK
    nq, nk = cfg.AQH, cfg.AKH
    qi = lax.broadcasted_iota(jnp.int32, (blk, 2 * blk), 0)
    kj = lax.broadcasted_iota(jnp.int32, (blk, 2 * blk), 1)
    dist = qi + blk - kj
    valid = ((dist >= 0) & (dist < blk) & (jnp.logical_not(first) | (kj >= blk)))[None]
    cos2 = jnp.concatenate([cosp, cosc], axis=0)
    sin2 = jnp.concatenate([sinp, sinc], axis=0)
    head = lambda x, h: x[:, A_HEAD * h:A_HEAD * (h + 1)]
    k2 = _stack([jnp.concatenate([head(kp, h), head(kc, h)], axis=0) for h in range(nk)], axis=0)
    v2 = _stack([jnp.concatenate([head(vp, h), head(vc, h)], axis=0) for h in range(nk)], axis=0)
    k2 = _norm_rope(k2, knw[None], cos2[None], sin2[None]).astype(BF16)
    v2 = v2.astype(BF16)
    k2 = _stack([k2[h // A_GROUP] for h in range(nq)], axis=0)
    v2 = _stack([v2[h // A_GROUP] for h in range(nq)], axis=0)
    qh = _stack([head(q, h) for h in range(nq)], axis=0)
    qh = _norm_rope(qh, qnw[None], cosc[None], sinc[None]).astype(BF16)
    s = _bdot(qh, k2, 2, 2, None) * (A_HEAD ** -0.5)
    s = jnp.where(valid, s, -1e30)
    sink = _stack([sinks_row[:, A_HEAD * h:A_HEAD * h + 1] for h in range(nq)], axis=0)
    o = _bdot(_sink_softmax(s, sink).astype(BF16), v2, 2, 1, None)
    return (jnp.concatenate([o[h] for h in range(nq)], axis=1) * _silu(za),)


def conv_block(cfg, first, u, zb, uh, cw, cb, lw, lb, pw, pb):
    BW = cfg.BW
    tb = u.shape[0]
    uu = jnp.concatenate([uh, u], axis=0)
    h = uu[:, :BW] * jax.nn.sigmoid(uu[:, BW:])
    row = lax.broadcasted_iota(jnp.int32, h.shape, 0)
    h = jnp.where(first & (row < B_HALO), 0.0, h)
    acc = jnp.zeros((tb, BW), F32) + cb
    base = B_HALO - (B_CONV - 1)
    for k in range(B_CONV):
        acc = acc + cw[k:k + 1, :] * h[base + k:base + k + tb, :]
    mu = jnp.mean(acc, axis=-1, keepdims=True)
    var = jnp.mean(jnp.square(acc - mu), axis=-1, keepdims=True)
    y = (acc - mu) * lax.rsqrt(var + EPS) * lw + lb
    s = _silu(y)
    o = jnp.dot(s.astype(BF16), pw.astype(BF16), preferred_element_type=F32) + pb
    return (o * _silu(zb),)


def gdn_prep_block(cfg, first, xq, xk, xv, braw, araw, hq, hk, hv, cw, alog, dtb):
    CW = cfg.CW
    tb = xq.shape[0]
    outs = []
    for idx, (x, xh) in enumerate(((xq, hq), (xk, hk), (xv, hv))):
        xx = jnp.concatenate([jnp.where(first, 0.0, xh), x], axis=0)
        w = cw[:, idx * CW:(idx + 1) * CW]
        acc = jnp.zeros((tb, CW), F32)
        base = C_HALO - (C_CONV - 1)
        for k in range(C_CONV):
            acc = acc + w[k:k + 1, :] * xx[base + k:base + k + tb, :]
        y = _silu(acc)
        if idx < 2:
            parts = []
            for h in range(cfg.CH):
                yh = y[:, C_HEAD * h:C_HEAD * (h + 1)]
                parts.append(yh * lax.rsqrt(jnp.sum(yh * yh, axis=-1, keepdims=True) + EPS))
            y = jnp.concatenate(parts, axis=1)
        outs.append(y)
    beta = jax.nn.sigmoid(braw)
    g = -jnp.exp(alog) * jax.nn.softplus(araw + dtb)
    return outs[0], outs[1], outs[2], g, beta


def _inverse_unit_lower(low, eye):
    pw = low
    inv = eye - low
    for _ in range(5):
        pwb = pw.astype(BF16)
        pw = _bdot(pwb, pwb, 2, 1, None)
        inv = inv + _bdot(inv.astype(BF16), pw.astype(BF16), 2, 1, None)
    ax = inv + _bdot(low, inv, 2, 1)
    return inv + _bdot(inv, eye - ax, 2, 1)


@jax.custom_vjp
def _saved_inverse(low, inv):
    return inv


def _saved_inverse_fwd(low, inv):
    return inv, inv


def _saved_inverse_bwd(inv, d):
    dlow = -_bdot(_bdot(inv, d, 1, 1), inv, 2, 2)
    return dlow, jnp.zeros_like(inv)


_saved_inverse.defvjp(_saved_inverse_fwd, _saved_inverse_bwd)


def gdn_intra_rows(cfg, first, qn, kn, v, g, beta, inv_saved=None):
    c = CHUNK
    CH = cfg.CH
    nchunk = qn.shape[0] // c
    i = lax.broadcasted_iota(jnp.int32, (c, c), 0)
    j = lax.broadcasted_iota(jnp.int32, (c, c), 1)
    incl = (i >= j)[None]
    strict = (i > j)[None]
    eye = (i == j).astype(F32)[None]
    tri = (i >= j).astype(F32)
    rows = [slice(c * ci, c * (ci + 1)) for ci in range(nchunk)]
    gcs = [jnp.dot(tri, g[r], precision=HI, preferred_element_type=F32) for r in rows]
    pairs = [(ci, h) for ci in range(nchunk) for h in range(CH)]
    heads = lambda x, wd: _stack([x[rows[ci], wd * h:wd * (h + 1)] for ci, h in pairs], axis=0)
    gch = _stack([gcs[ci][:, h:h + 1] for ci, h in pairs], axis=0)
    bh = _stack([beta[rows[ci], h:h + 1] for ci, h in pairs], axis=0)
    q = heads(qn, C_HEAD) * (C_HEAD ** -0.5)
    k = heads(kn, C_HEAD)
    vv = heads(v, C_HEAD)
    a = jnp.broadcast_to(gch, (len(pairs), c, c))
    diff = jnp.where(incl, a - jnp.swapaxes(a, 1, 2), 0.0)
    decay = jnp.where(incl, jnp.exp(diff), 0.0)
    kb = k * bh
    low = jnp.where(strict, _bdot(kb, k, 2, 2) * decay, 0.0)
    if inv_saved is None:
        inv = _inverse_unit_lower(low, eye)
    else:
        inv = _saved_inverse(low, heads(inv_saved, c))
    eg = jnp.exp(gch)
    sol = _bdot(inv, jnp.concatenate([vv * bh, kb * eg], axis=2), 2, 1)
    intra = jnp.where(incl, _bdot(q, k, 2, 2) * decay, 0.0)
    qg = q * eg
    kd = k * jnp.exp(gch[:, c - 1:c, :] - gch)
    glast = jnp.concatenate([jnp.broadcast_to(gc[c - 1:c, :], gc.shape) for gc in gcs], axis=0)

    def unstack(x):
        return jnp.concatenate([jnp.concatenate([x[ci * CH + h] for h in range(CH)], axis=1) for ci in range(nchunk)],
                               axis=0)

    outs = (unstack(sol[:, :, :C_HEAD]), unstack(sol[:, :, C_HEAD:]), unstack(qg), unstack(kd), unstack(intra), glast)
    return outs + (unstack(inv),) if inv_saved is None else outs


def gdn_state_step(S, u, w, qg, kd, intra, glast):
    v_new = u - _bdot(w, S, 2, 1)
    o = _bdot(qg, S, 2, 1) + _bdot(intra, v_new, 2, 1)
    S_next = S * jnp.exp(glast) + _bdot(kd, v_new, 1, 1)
    return o, S_next


def gdn_out_block(cfg, first, o, zc, onw):
    parts = []
    for h in range(cfg.CH):
        sl = slice(C_HEAD * h, C_HEAD * (h + 1))
        oh = o[:, sl]
        y = oh * lax.rsqrt(jnp.mean(oh * oh, axis=-1, keepdims=True) + EPS) * onw
        parts.append(y * _silu(zc[:, sl]))
    return (jnp.concatenate(parts, axis=1),)


def rms_block(x, nw):
    return x * lax.rsqrt(jnp.mean(x * x, axis=-1, keepdims=True) + EPS) * nw


class Row:
    def __init__(self, arr, width, colblk=0, grad=None):
        self.arr, self.width, self.colblk, self.grad = arr, width, colblk, grad


class Halo:
    def __init__(self, arr, width, colblk, hr, tie=None):
        self.arr, self.width, self.colblk, self.hr, self.tie = arr, width, colblk, hr, tie


def _row_specs(tb, rows, halos, params, pos):
    specs = [pl.BlockSpec((tb, r.width), lambda i, cb=r.colblk: (pos(i), cb)) for r in rows]
    specs += [pl.BlockSpec((h.hr, h.width),
                           lambda i, cb=h.colblk, m=tb // h.hr: (jnp.maximum(pos(i) * m - 1, 0), cb))
              for h in halos]
    specs += [pl.BlockSpec(p.shape, lambda i: (0, 0)) for p in params]
    return specs


def rb_fwd(name, fn, n, tb, bps, rows, halos, params, outs, comm=None):
    nr, nh, npar = len(rows), len(halos), len(params)

    def body(*refs):
        ins = refs[:nr + nh + npar]
        o_refs = refs[nr + nh + npar:]
        first = (pl.program_id(0) % bps) == 0
        res = fn(first, *[r[...] for r in ins])
        for ref, val in zip(o_refs, res):
            ref[...] = val.astype(ref.dtype)

    res, carried = call_with_comm(
        body, name, (n // tb,), _row_specs(tb, rows, halos, params, lambda i: i),
        [pl.BlockSpec((tb, w), lambda i: (i, 0)) for w, _ in outs],
        [jax.ShapeDtypeStruct((n, w), dt) for w, dt in outs], [], ("parallel",),
        [r.arr for r in rows] + [h.arr for h in halos] + list(params), comm)
    return (res, carried) if comm is not None else res


def rb_bwd(name, fn, n, tb, bps, rows, halos, params, douts, param_grads, comm=None):
    nr, nh, npar, nd = len(rows), len(halos), len(params), len(douts)
    nblk = n // tb
    grow = [k for k, r in enumerate(rows) if r.grad is not None]
    ghalo = [k for k, h in enumerate(halos) if h.tie is not None]
    gpar = [k for k, f in enumerate(param_grads) if f]
    pos = lambda i: nblk - 1 - i

    def body(*refs):
        ins = refs[:nr + nh + npar]
        d_refs = refs[nr + nh + npar:nr + nh + npar + nd]
        rest = refs[nr + nh + npar + nd:]
        grow_refs = rest[:len(grow)]
        gpar_refs = rest[len(grow):len(grow) + len(gpar)]
        carry_refs = rest[len(grow) + len(gpar):]
        i = pl.program_id(0)
        first = (pos(i) % bps) == 0
        vals = [r[...] for r in ins]
        diff_idx = grow + [nr + k for k in ghalo] + [nr + nh + k for k in gpar]

        def f(*dargs):
            full = list(vals)
            for k, a in zip(diff_idx, dargs):
                full[k] = a
            return fn(first, *full)

        res, vjp = jax.vjp(f, *[vals[k] for k in diff_idx])
        grads = vjp(tuple(d[...].astype(r.dtype) for d, r in zip(d_refs, res)))
        g_rows = list(grads[:len(grow)])
        g_halos = grads[len(grow):len(grow) + len(ghalo)]
        g_pars = grads[len(grow) + len(ghalo):]

        @pl.when(i == 0)
        def _():
            for c in carry_refs:
                c[...] = jnp.zeros_like(c)
            for p in gpar_refs:
                p[...] = jnp.zeros_like(p)

        for k, ref in enumerate(grow_refs):
            ref[...] = g_rows[k].astype(ref.dtype)
        for ci, hk in enumerate(ghalo):
            h = halos[hk]
            k = grow.index(h.tie)
            tail = g_rows[k][tb - h.hr:, :] + carry_refs[ci][...]
            grow_refs[k][tb - h.hr:, :] = tail.astype(grow_refs[k].dtype)
            carry_refs[ci][...] = g_halos[ci]
        for ref, gp in zip(gpar_refs, g_pars):
            ref[...] += gp

    out_specs = [pl.BlockSpec((tb, rows[k].width), lambda i: (pos(i), 0)) for k in grow]
    out_specs += [pl.BlockSpec(params[k].shape, lambda i: (0, 0)) for k in gpar]
    out_shape = [jax.ShapeDtypeStruct((n, rows[k].width), rows[k].grad) for k in grow]
    out_shape += [jax.ShapeDtypeStruct(params[k].shape, F32) for k in gpar]
    in_specs = _row_specs(tb, rows, halos, params, pos)
    in_specs += [pl.BlockSpec((tb, d.shape[1]), lambda i: (pos(i), 0)) for d in douts]
    res, carried = call_with_comm(
        body, name, (nblk,), in_specs, out_specs, out_shape,
        [pltpu.VMEM((halos[k].hr, halos[k].width), F32) for k in ghalo], ("arbitrary",),
        [r.arr for r in rows] + [h.arr for h in halos] + list(params) + list(douts), comm)
    return (res, carried) if comm is not None else res


_DIMS = {'nn': (((1,), (0,)), ((), ())), 'nt': (((1,), (1,)), ((), ())), 'tn': (((0,), (0,)), ((), ()))}


def matmul(name, a, b, mode, tm, tn, tk, out_dtype=F32, add=None, comm=None):
    if mode == 'tn':
        K, M = a.shape
    else:
        M, K = a.shape
    N = b.shape[0] if mode == 'nt' else b.shape[1]
    tm, tn, tk = min(tm, M), min(tn, N), min(tk, K)
    assert M % tm == 0 and N % tn == 0 and K % tk == 0, (name, M, N, K, tm, tn, tk)
    nk = K // tk
    a_spec = pl.BlockSpec((tk, tm), lambda i, j, k: (k, i)) if mode == 'tn' else pl.BlockSpec((tm, tk), lambda i, j, k: (i, k))
    b_spec = pl.BlockSpec((tn, tk), lambda i, j, k: (j, k)) if mode == 'nt' else pl.BlockSpec((tk, tn), lambda i, j, k: (k, j))
    o_spec = pl.BlockSpec((tm, tn), lambda i, j, k: (i, j))
    has_add = add is not None

    def body(*refs):
        a_ref, b_ref = refs[0], refs[1]
        add_ref = refs[2] if has_add else None
        o_ref = refs[-1]
        k = pl.program_id(2)
        part = lax.dot_general(a_ref[...].astype(BF16), b_ref[...].astype(BF16), _DIMS[mode], preferred_element_type=F32)

        @pl.when(k == 0)
        def _():
            o_ref[...] = ((part + add_ref[...]) if has_add else part).astype(o_ref.dtype)

        if nk > 1:
            @pl.when(k > 0)
            def _():
                o_ref[...] += part

    assert nk == 1 or out_dtype == F32
    ins = [a, b] + ([add] if has_add else [])
    in_specs = [a_spec, b_spec] + ([o_spec] if has_add else [])
    outs, couts = call_with_comm(body, name, (M // tm, N // tn, nk), in_specs, [o_spec],
                                 [jax.ShapeDtypeStruct((M, N), out_dtype)], [], ("parallel", "parallel", "arbitrary"),
                                 ins, comm)
    return (outs[0], couts) if comm is not None else outs[0]


def grad_w_blocks(name, h, dpb, tm, tk):
    n, d = h.shape
    nb, _, s = dpb.shape
    tm, tk = min(tm, d), min(tk, n)
    assert d % tm == 0 and n % tk == 0
    nk = n // tk

    def body(h_ref, b_ref, o_ref):
        k = pl.program_id(2)
        part = lax.dot_general(h_ref[...], b_ref[0], _DIMS['tn'], preferred_element_type=F32)

        @pl.when(k == 0)
        def _():
            o_ref[0] = part

        if nk > 1:
            @pl.when(k > 0)
            def _():
                o_ref[0] += part

    return pl.pallas_call(
        body, name=name, grid=(nb, d // tm, nk),
        in_specs=[pl.BlockSpec((tk, tm), lambda j, i, k: (k, i)), pl.BlockSpec((1, tk, s), lambda j, i, k: (j, k, 0))],
        out_specs=pl.BlockSpec((1, tm, s), lambda j, i, k: (j, i, 0)),
        out_shape=jax.ShapeDtypeStruct((nb, d, s), F32),
        compiler_params=_cparams(("parallel", "parallel", "arbitrary")),
    )(h, dpb)


def grad_h_blocks(name, dpb, wb, tm, tn, comm=None):
    nb, n, s = dpb.shape
    d = wb.shape[1]
    tm, tn = min(tm, n), min(tn, d)
    assert n % tm == 0 and d % tn == 0

    def body(a_ref, b_ref, o_ref):
        k = pl.program_id(2)
        part = lax.dot_general(a_ref[0], b_ref[0], _DIMS['nt'], preferred_element_type=F32)

        @pl.when(k == 0)
        def _():
            o_ref[...] = part

        @pl.when(k > 0)
        def _():
            o_ref[...] += part

    outs, carried = call_with_comm(
        body, name, (n // tm, d // tn, nb),
        [pl.BlockSpec((1, tm, s), lambda i, j, k: (k, i, 0)), pl.BlockSpec((1, tn, s), lambda i, j, k: (k, j, 0))],
        [pl.BlockSpec((tm, tn), lambda i, j, k: (i, j))], [jax.ShapeDtypeStruct((n, d), F32)], [],
        ("parallel", "parallel", "arbitrary"), [dpb, wb], comm)
    return (outs[0], carried) if comm is not None else outs[0]


def norm_in_proj(name, x, nw, wp, tm, tn, comm=None):
    n, d = x.shape
    wpc = wp.shape[1]
    tm, tn = min(tm, n), min(tn, wpc)
    assert n % tm == 0 and wpc % tn == 0

    def body(x_ref, nw_ref, w_ref, p_ref, h_ref):
        @pl.when(pl.program_id(1) == 0)
        def _():
            h_ref[...] = rms_block(x_ref[...], nw_ref[...]).astype(BF16)

        p_ref[...] = jnp.dot(h_ref[...], w_ref[...], preferred_element_type=F32)

    outs, couts = call_with_comm(
        body, name, (n // tm, wpc // tn),
        [pl.BlockSpec((tm, d), lambda i, j: (i, 0)), pl.BlockSpec((1, d), lambda i, j: (0, 0)),
         pl.BlockSpec((d, tn), lambda i, j: (0, j))],
        [pl.BlockSpec((tm, tn), lambda i, j: (i, j)), pl.BlockSpec((tm, d), lambda i, j: (i, 0))],
        [jax.ShapeDtypeStruct((n, wpc), F32), jax.ShapeDtypeStruct((n, d), BF16)], [], ("parallel", "arbitrary"),
        [x, nw, wp], comm)
    return (outs[0], outs[1], couts) if comm is not None else (outs[0], outs[1])


def norm_bwd(name, x, nw, dh, dres, tb, comm=None):
    n, d = x.shape
    tb = min(tb, n)

    def body(x_ref, nw_ref, dh_ref, dres_ref, dx_ref, dnw_ref):
        @pl.when(pl.program_id(0) == 0)
        def _():
            dnw_ref[...] = jnp.zeros_like(dnw_ref)

        _, vjp = jax.vjp(rms_block, x_ref[...], nw_ref[...])
        dx, dnw = vjp(dh_ref[...])
        dx_ref[...] = dx + dres_ref[...]
        dnw_ref[...] += dnw

    row = pl.BlockSpec((tb, d), lambda i: (i, 0))
    par = pl.BlockSpec((1, d), lambda i: (0, 0))
    outs, carried = call_with_comm(
        body, name, (n // tb,), [row, par, row, row], [row, par],
        [jax.ShapeDtypeStruct((n, d), F32), jax.ShapeDtypeStruct((1, d), F32)], [], ("arbitrary",),
        [x, nw, dh, dres], comm)
    return (outs[0], outs[1], carried) if comm is not None else (outs[0], outs[1])


def out_proj_loss(name, y, wo, x, target, tm):
    n, d = x.shape
    tm = min(tm, n)
    assert n % tm == 0

    def body(y_ref, w_ref, x_ref, t_ref, dz_ref, loss_ref):
        @pl.when(pl.program_id(0) == 0)
        def _():
            loss_ref[...] = jnp.zeros_like(loss_ref)

        z = x_ref[...] + jnp.dot(y_ref[...], w_ref[...], preferred_element_type=F32)
        err = z - t_ref[...]
        dz_ref[...] = err * (1.0 / d)
        part = 0.5 * jnp.sum(jnp.mean(err * err, axis=-1, keepdims=True), axis=0, keepdims=True)
        loss_ref[...] += jnp.broadcast_to(part, loss_ref.shape)

    row = pl.BlockSpec((tm, d), lambda i: (i, 0))
    dz, loss = pl.pallas_call(
        body, name=name, grid=(n // tm,),
        in_specs=[pl.BlockSpec((tm, y.shape[1]), lambda i: (i, 0)), pl.BlockSpec(wo.shape, lambda i: (0, 0)), row, row],
        out_specs=[row, pl.BlockSpec((8, LANE), lambda i: (0, 0))],
        out_shape=[jax.ShapeDtypeStruct((n, d), F32), jax.ShapeDtypeStruct((8, LANE), F32)],
        compiler_params=_cparams(("arbitrary",)),
    )(y, wo, x, target)
    return dz, loss[0, 0]


def rope_tables(name, pos_col, inv_freq_row):
    n = pos_col.shape[0]

    def body(p_ref, f_ref, c_ref, s_ref):
        ang = p_ref[...].astype(F32) * f_ref[...]
        lane = lax.broadcasted_iota(jnp.int32, ang.shape, 1)
        c_ref[...] = jnp.where(lane < ROT_DIM, jnp.cos(ang), 1.0)
        s_ref[...] = jnp.where(lane < ROT_DIM, jnp.sin(ang), 0.0)

    return pl.pallas_call(
        body, name=name, out_shape=[jax.ShapeDtypeStruct((n, A_HEAD), F32)] * 2,
    )(pos_col, inv_freq_row)


SCAN_CHUNKS = 2


def _scan_operands(cfg, nseq, r0, u_ref, w_ref, qg_ref, kd_ref, a_ref, gl_ref):
    pairs = [(b, h) for b in range(nseq) for h in range(cfg.CH)]
    st = lambda r, wd: _stack([r[b, r0:r0 + CHUNK, wd * h:wd * (h + 1)] for b, h in pairs], axis=0)
    gl = _stack([gl_ref[b, r0:r0 + 1, h:h + 1] for b, h in pairs], axis=0)
    return st(u_ref, C_HEAD), st(w_ref, C_HEAD), st(qg_ref, C_HEAD), st(kd_ref, C_HEAD), st(a_ref, CHUNK), gl


def gdn_scan_fwd(name, cfg, nseq, u, w, qg, kd, intra, glast):
    CH, CW, T = cfg.CH, cfg.CW, cfg.T
    nc = T // CHUNK
    per = SCAN_CHUNKS if nc % SCAN_CHUNKS == 0 else 1

    def body(u_ref, w_ref, qg_ref, kd_ref, a_ref, gl_ref, o_ref, sin_ref, s_ref):
        @pl.when(pl.program_id(0) == 0)
        def _():
            s_ref[...] = jnp.zeros_like(s_ref)

        S = s_ref[...]
        for ci in range(per):
            r0 = ci * CHUNK
            for b in range(nseq):
                sin_ref[b, ci] = S[b * CH:(b + 1) * CH]
            o, S = gdn_state_step(S, *_scan_operands(cfg, nseq, r0, u_ref, w_ref, qg_ref, kd_ref, a_ref, gl_ref))
            for b in range(nseq):
                o_ref[b, r0:r0 + CHUNK, :] = jnp.concatenate([o[b * CH + h] for h in range(CH)], axis=1)
        s_ref[...] = S

    row = lambda wd: pl.BlockSpec((nseq, per * CHUNK, wd), lambda c: (0, c, 0))
    widths = [CW, CW, CW, CW, CH * CHUNK, LANE]
    o, s_in = pl.pallas_call(
        body, name=name, grid=(nc // per,),
        in_specs=[row(x) for x in widths],
        out_specs=[row(CW), pl.BlockSpec((nseq, per, CH, C_HEAD, C_HEAD), lambda c: (0, c, 0, 0, 0))],
        out_shape=[jax.ShapeDtypeStruct((nseq, T, CW), F32),
                   jax.ShapeDtypeStruct((nseq, nc, CH, C_HEAD, C_HEAD), F32)],
        scratch_shapes=[pltpu.VMEM((nseq * CH, C_HEAD, C_HEAD), F32)],
        compiler_params=_cparams(("arbitrary",)),
    )(*[a.reshape(nseq, T, a.shape[1]) for a in (u, w, qg, kd, intra, glast)])
    return o.reshape(nseq * T, CW), s_in


def gdn_scan_bwd(name, cfg, nseq, u, w, qg, kd, intra, glast, s_in, do, comm=None):
    CH, CW, T = cfg.CH, cfg.CW, cfg.T
    nc = T // CHUNK
    per = SCAN_CHUNKS if nc % SCAN_CHUNKS == 0 else 1
    nsteps = nc // per

    def body(u_ref, w_ref, qg_ref, kd_ref, a_ref, gl_ref, sin_ref, do_ref,
             du_ref, dw_ref, dqg_ref, dkd_ref, da_ref, dgl_ref, ds_ref):
        @pl.when(pl.program_id(0) == 0)
        def _():
            ds_ref[...] = jnp.zeros_like(ds_ref)

        lane = lax.broadcasted_iota(jnp.int32, (CHUNK, LANE), 1)
        rowi = lax.broadcasted_iota(jnp.int32, (CHUNK, LANE), 0)
        dS = ds_ref[...]
        for ci in reversed(range(per)):
            r0 = ci * CHUNK
            rows = slice(r0, r0 + CHUNK)
            S = jnp.concatenate([sin_ref[b, ci] for b in range(nseq)], axis=0)
            dout = _stack([do_ref[b, rows, C_HEAD * h:C_HEAD * (h + 1)] for b in range(nseq) for h in range(CH)], axis=0)
            _, vjp = jax.vjp(gdn_state_step, S,
                             *_scan_operands(cfg, nseq, r0, u_ref, w_ref, qg_ref, kd_ref, a_ref, gl_ref))
            dS, du, dw, dqg, dkd, da, dg = vjp((dout, dS))
            for b in range(nseq):
                cat = lambda x: jnp.concatenate([x[b * CH + h] for h in range(CH)], axis=1)
                du_ref[b, rows, :] = cat(du)
                dw_ref[b, rows, :] = cat(dw)
                dqg_ref[b, rows, :] = cat(dqg)
                dkd_ref[b, rows, :] = cat(dkd)
                da_ref[b, rows, :] = cat(da)
                dgl = jnp.zeros((CHUNK, LANE), F32)
                for h in range(CH):
                    dgl = dgl + jnp.where((lane == h) & (rowi == 0), dg[b * CH + h], 0.0)
                dgl_ref[b, rows, :] = dgl
        ds_ref[...] = dS

    row = lambda wd: pl.BlockSpec((nseq, per * CHUNK, wd), lambda c: (0, nsteps - 1 - c, 0))
    widths = [CW, CW, CW, CW, CH * CHUNK, LANE]
    outs, carried = call_with_comm(
        body, name, (nsteps,),
        [row(x) for x in widths]
        + [pl.BlockSpec((nseq, per, CH, C_HEAD, C_HEAD), lambda c: (0, nsteps - 1 - c, 0, 0, 0)), row(CW)],
        [row(x) for x in widths], [jax.ShapeDtypeStruct((nseq, T, x), F32) for x in widths],
        [pltpu.VMEM((nseq * CH, C_HEAD, C_HEAD), F32)], ("arbitrary",),
        [a.reshape(nseq, T, a.shape[1]) for a in (u, w, qg, kd, intra, glast)] + [s_in, do.reshape(nseq, T, CW)], comm)
    return [o.reshape(nseq * T, o.shape[2]) for o in outs], carried


def _tile(total, cap, unit=LANE):
    best = None
    for t in range(unit, min(cap, total) + 1, unit):
        if total % t == 0:
            best = t
    assert best is not None, (total, cap, unit)
    return best


def _pad_lanes(v, width=LANE):
    return jnp.pad(v.reshape(1, -1), ((0, 0), (0, width - v.shape[-1])))


def permute_w_in(cfg, w):
    parts = []
    for n in cfg.order:
        off, wd = cfg.orig[n]
        blk = w[:, off:off + wd]
        if cfg.g[n][1] != wd:
            blk = jnp.pad(blk, ((0, 0), (0, cfg.g[n][1] - wd)))
        parts.append(blk)
    return jnp.concatenate(parts, axis=1)


def chip_blocks(cfg, groups, n_chips):
    s = cfg.IN_COLS // n_chips
    blocks = []
    for j in range(n_chips):
        lo, hi = j * s, (j + 1) * s
        pieces = []
        for name, (off, wd) in cfg.orig.items():
            a, b = max(lo, off), min(hi, off + wd)
            if a < b:
                pieces.append(groups[name][:, a - off:b - off])
        blocks.append(jnp.concatenate(pieces, axis=1))
    return jnp.stack(blocks)


def _layer_params(cfg, prm):
    return dict(
        nw=prm['norm_w'].reshape(1, -1),
        qnw=prm['q_norm_w'].reshape(1, -1), knw=prm['k_norm_w'].reshape(1, -1),
        sinks_row=jnp.repeat(prm['sinks'], A_HEAD).reshape(1, -1),
        cw=prm['b_conv_w'], cb=prm['b_conv_b'].reshape(1, -1),
        lw=prm['b_ln_w'].reshape(1, -1), lb=prm['b_ln_b'].reshape(1, -1),
        pw=prm['b_pw_w'], pb=prm['b_pw_b'].reshape(1, -1),
        ccw=prm['c_conv_w'], alog=_pad_lanes(prm['c_a_log']), dtb=_pad_lanes(prm['c_dt_bias']),
        onw=prm['c_onorm_w'].reshape(1, -1),
    )


def _attn_io(cfg, p, cos, sin, grads):
    gq = BF16 if grads else None
    rows = [Row(p, cfg.AW, cfg.blk('qa'), gq), Row(p, cfg.AW, cfg.blk('za'), gq),
            Row(p, cfg.AKW, cfg.blk('ka'), gq), Row(p, cfg.AKW, cfg.blk('va'), gq),
            Row(cos, A_HEAD), Row(sin, A_HEAD)]
    halos = [Halo(p, cfg.AKW, cfg.blk('ka'), ATTN_BLOCK, 2 if grads else None),
             Halo(p, cfg.AKW, cfg.blk('va'), ATTN_BLOCK, 3 if grads else None),
             Halo(cos, A_HEAD, 0, ATTN_BLOCK), Halo(sin, A_HEAD, 0, ATTN_BLOCK)]
    return rows, halos


def _conv_io(cfg, p, grads):
    gq = BF16 if grads else None
    rows = [Row(p, 2 * cfg.BW, cfg.blk('ub'), gq), Row(p, cfg.BW, cfg.blk('zb'), gq)]
    halos = [Halo(p, 2 * cfg.BW, cfg.blk('ub'), B_HALO, 0 if grads else None)]
    return rows, halos


def _prep_io(cfg, p, grads):
    gq = BF16 if grads else None
    rows = [Row(p, cfg.CW, cfg.blk(n), gq) for n in ('qc', 'kc', 'vc')]
    rows += [Row(p, LANE, cfg.blk('bc'), gq), Row(p, LANE, cfg.blk('ac'), gq)]
    halos = [Halo(p, cfg.CW, cfg.blk(n), C_HALO, k if grads else None) for k, n in enumerate(('qc', 'kc', 'vc'))]
    return rows, halos


TB_CONV = 128
TB_PREP = 256
TB_OUT = 512
TB_INTRA_FWD = 256
TB_INTRA_BWD = 256


def layer_forward(cfg, l, x, lp, wp, wo, cos, sin, comms=None, target=None):
    n = x.shape[0]
    nseq = n // cfg.T
    T = cfg.T
    comms = comms or {}
    carried = {}

    def hosted(key, res):
        if comms.get(key) is None:
            return res
        res, carried[key] = res
        return res

    nw = lp['nw'] if isinstance(lp, dict) else lp[0]
    p, h, *rode = norm_in_proj(f"in_proj_{l}", x, nw, wp, 1024, _tile(cfg.WP, 768), comm=comms.get('in_proj'))
    if rode:
        carried['in_proj'] = rode[0]
    if not isinstance(lp, dict):
        lp, wo = lp[1](carried['in_proj'])
    rows, halos = _attn_io(cfg, p, cos, sin, False)
    (oa,) = hosted('attn', rb_fwd(f"attn_fwd_{l}", functools.partial(attn_block, cfg), n, ATTN_BLOCK, T // ATTN_BLOCK,
                                  rows, halos, [lp['qnw'], lp['knw'], lp['sinks_row']], [(cfg.AW, BF16)],
                                  comm=comms.get('attn')))
    rows, halos = _conv_io(cfg, p, False)
    tbb = min(TB_CONV, T)
    (ob,) = hosted('conv', rb_fwd(f"conv_fwd_{l}", functools.partial(conv_block, cfg), n, tbb, T // tbb, rows, halos,
                                  [lp['cw'], lp['cb'], lp['lw'], lp['lb'], lp['pw'], lp['pb']], [(cfg.BW, BF16)],
                                  comm=comms.get('conv')))
    rows, halos = _prep_io(cfg, p, False)
    tbp = min(TB_PREP, T)
    qn, kn, v, g, beta = rb_fwd(f"gdn_prep_fwd_{l}", functools.partial(gdn_prep_block, cfg), n, tbp, T // tbp, rows,
                                halos, [lp['ccw'], lp['alog'], lp['dtb']],
                                [(cfg.CW, F32)] * 3 + [(LANE, F32)] * 2)
    intra_outs = hosted('intra', rb_fwd(
        f"gdn_intra_fwd_{l}", functools.partial(gdn_intra_rows, cfg), n, min(TB_INTRA_FWD, T), T // min(TB_INTRA_FWD, T),
        [Row(qn, cfg.CW), Row(kn, cfg.CW), Row(v, cfg.CW), Row(g, LANE), Row(beta, LANE)], [], [],
        [(cfg.CW, F32)] * 4 + [(cfg.CH * CHUNK, F32), (LANE, F32), (cfg.CH * CHUNK, F32)], comm=comms.get('intra')))
    intra_outs, inv = intra_outs[:6], intra_outs[6]
    o, s_in = gdn_scan_fwd(f"gdn_scan_fwd_{l}", cfg, nseq, *intra_outs)
    tbo = min(TB_OUT, T)
    (oc,) = rb_fwd(f"gdn_out_fwd_{l}", functools.partial(gdn_out_block, cfg), n, tbo, T // tbo,
                   [Row(o, cfg.CW), Row(p, cfg.CW, cfg.blk('zc'))], [], [lp['onw']], [(cfg.CW, BF16)])
    y = jnp.concatenate([oa, ob, oc], axis=1)
    if target is None:
        x_next = matmul(f"out_proj_{l}", y, wo, 'nn', 1024, 1024, cfg.D, add=x)
    else:
        x_next = out_proj_loss(f"out_proj_{l}", y, wo, x, target, 512)
    saved = dict(x=x, p=p, h=h, y=y, qn=qn, kn=kn, v=v, g=g, beta=beta, intra_outs=intra_outs, inv=inv, s_in=s_in, o=o,
                 lp=lp, wo=wo)
    return x_next, saved, carried


def layer_backward(cfg, l, dxn, sv, lp, wb, wo, cos, sin, rs=None, own_rs=None):
    n = dxn.shape[0]
    nseq = n // cfg.T
    T = cfg.T
    p = sv['p']
    AW, BW, CW = cfg.AW, cfg.BW, cfg.CW
    dy = matmul(f"dy_{l}", dxn, wo, 'nt', 1024, 1024, cfg.D)
    dwo = matmul(f"dwo_{l}", sv['y'], dxn, 'tn', 1024, 1024, 2048)
    doa, dob, doc = dy[:, :AW], dy[:, AW:AW + BW], dy[:, AW + BW:]
    tbo = min(TB_OUT, T)
    do, dzc, donw = rb_bwd(f"gdn_out_bwd_{l}", functools.partial(gdn_out_block, cfg), n, tbo, T // tbo,
                           [Row(sv['o'], CW, 0, F32), Row(p, CW, cfg.blk('zc'), BF16)], [], [lp['onw']], [doc], [True])
    dintra, got_rest = gdn_scan_bwd(f"gdn_scan_bwd_{l}", cfg, nseq, *sv['intra_outs'], sv['s_in'], do,
                                    comm=None if rs is None else rs.scatter([1, 2]))
    dqn, dkn, dv, dg, dbeta = rb_bwd(
        f"gdn_intra_bwd_{l}", functools.partial(gdn_intra_rows, cfg), n, min(TB_INTRA_BWD, T), T // min(TB_INTRA_BWD, T),
        [Row(sv['qn'], CW, 0, F32), Row(sv['kn'], CW, 0, F32), Row(sv['v'], CW, 0, F32), Row(sv['g'], LANE, 0, F32),
         Row(sv['beta'], LANE, 0, F32), Row(sv['inv'], cfg.CH * CHUNK)], [], [], list(dintra), [])
    rows, halos = _prep_io(cfg, p, True)
    tbp = min(TB_PREP, T)
    dqc, dkc, dvc, dbc, dac, dccw, dalog, ddtb = rb_bwd(
        f"gdn_prep_bwd_{l}", functools.partial(gdn_prep_block, cfg), n, tbp, T // tbp, rows, halos,
        [lp['ccw'], lp['alog'], lp['dtb']], [dqn, dkn, dv, dg, dbeta], [True] * 3)
    rows, halos = _conv_io(cfg, p, True)
    tbb = min(TB_CONV, T)
    conv_grads = rb_bwd(
        f"conv_bwd_{l}", functools.partial(conv_block, cfg), n, tbb, T // tbb, rows, halos,
        [lp['cw'], lp['cb'], lp['lw'], lp['lb'], lp['pw'], lp['pb']], [dob], [True] * 6,
        comm=None if rs is None else rs.scatter([0]))
    got = None
    if rs is not None:
        conv_grads, got_w_in = conv_grads
        got = got_w_in + got_rest
    dub, dzb, dcw, dcb, dlw, dlb, dpw, dpb = conv_grads
    rows, halos = _attn_io(cfg, p, cos, sin, True)
    dqa, dza, dka, dva, dqnw, dknw, dsinks_row = rb_bwd(
        f"attn_bwd_{l}", functools.partial(attn_block, cfg), n, ATTN_BLOCK, T // ATTN_BLOCK, rows, halos,
        [lp['qnw'], lp['knw'], lp['sinks_row']], [doa], [True] * 3)
    dgroups = dict(qa=dqa, za=dza, qc=dqc, kc=dkc, vc=dvc, zc=dzc, ka=dka, va=dva, ub=dub, zb=dzb, bc=dbc, ac=dac)
    dp_blocks = chip_blocks(cfg, dgroups, N_CHIPS)
    dw_in = grad_w_blocks(f"dwp_{l}", sv['h'], dp_blocks, 1024, 2048)
    mine = None if own_rs is None else own_rs(dict(w_in_blocks=dw_in, w_out=dwo, b_pw_w=dpw))
    got_mine = None
    if mine is None:
        dh = grad_h_blocks(f"dh_{l}", dp_blocks, wb, 1024, 1024)
        dx, dnw = norm_bwd(f"norm_bwd_{l}", sv['x'], lp['nw'], dh, dxn, 256)
    elif l > 0:
        dh, mine_received = grad_h_blocks(f"dh_{l}", dp_blocks, wb, 1024, 1024, comm=mine.swap())
        mine.add(mine_received)
        dx, dnw = norm_bwd(f"norm_bwd_{l}", sv['x'], lp['nw'], dh, dxn, 256)
    else:
        mine.add(run_comm(f"rs{l}_swap_halves", mine.swap()))
        dh, got_w_in = grad_h_blocks(f"dh_{l}", dp_blocks, wb, 1024, 1024, comm=mine.scatter([0]))
        dx, dnw, got_others = norm_bwd(f"norm_bwd_{l}", sv['x'], lp['nw'], dh, dxn, 256, comm=mine.scatter([1, 2]))
        got_mine = got_w_in + got_others
    grads = dict(
        norm_w=dnw[0], w_in_blocks=dw_in, q_norm_w=dqnw[0], k_norm_w=dknw[0],
        sinks=dsinks_row.reshape(cfg.AQH, A_HEAD)[:, 0],
        b_conv_w=dcw, b_conv_b=dcb[0], b_ln_w=dlw[0], b_ln_b=dlb[0], b_pw_w=dpw, b_pw_b=dpb[0],
        c_conv_w=dccw, c_a_log=dalog[0, :cfg.CH], c_dt_bias=ddtb[0, :cfg.CH], c_onorm_w=donw[0], w_out=dwo)
    return dx, grads, (got, mine, got_mine)


def rope_for(cfg, positions):
    n = positions.size
    inv_freq = ROPE_THETA ** (-np.arange(0, ROT_DIM, 2, dtype=np.float32) / ROT_DIM)
    freq_row = np.zeros((1, A_HEAD), np.float32)
    freq_row[0, :ROT_DIM] = np.concatenate([inv_freq, inv_freq])
    return rope_tables("rope_tables", positions.reshape(n, 1), jnp.asarray(freq_row))


def local_step(cfg, x, positions, prm, wps, wos, target):
    nseq = x.shape[0]
    n = nseq * cfg.T
    cos, sin = rope_for(cfg, positions)
    lps = [_layer_params(cfg, {k: v[l] for k, v in prm.items()}) for l in range(DEPTH)]
    saved = []
    xl = x.reshape(n, cfg.D)
    for l in range(DEPTH):
        xl, sv, _ = layer_forward(cfg, l, xl, lps[l], wps[l], wos[l], cos, sin,
                                  target=target.reshape(n, cfg.D) if l == DEPTH - 1 else None)
        saved.append(sv)
    dx, loss = xl
    grads = [None] * DEPTH
    for l in reversed(range(DEPTH)):
        groups = {k: wps[l][:, off:off + wd] for k, (off, wd) in cfg.g.items()}
        wb = chip_blocks(cfg, groups, N_CHIPS)
        dx, grads[l], _ = layer_backward(cfg, l, dx, saved[l], lps[l], wb, wos[l], cos, sin)
    return loss, dx.reshape(x.shape), grads


N_CHIPS = 4
N_DEV = 8


def _place():
    return lax.axis_index("x"), lax.axis_index("y"), lax.axis_index("c")


def _other_chips(x, y):
    return [(1 - x, y), (x, 1 - y), (1 - x, 1 - y)]


def _remote(src, dst, send, recv, to):
    return pltpu.make_async_remote_copy(src_ref=src, dst_ref=dst, send_sem=send, recv_sem=recv, device_id=to,
                                        device_id_type=MESH)


def gather_comm(arrs):
    n = len(arrs)

    def half(c):
        return [pl.ds(c * (a.shape[0] // 2), a.shape[0] // 2) for a in arrs]

    def first_copies(ins, outs, send, recv):
        x, y, c = _place()
        me = 2 * x + y
        mine = half(c)
        return [_remote(ins[i].at[mine[i]], outs[i].at[me, mine[i]], send.at[i, j], recv.at[i, j], (cx, cy, c))
                for i in range(n) for j, (cx, cy) in enumerate(_other_chips(x, y))]

    def start(ins, outs, sems):
        for cp in first_copies(ins, outs, *sems):
            cp.start()

    def finish(ins, outs, sems):
        send, recv = sems
        x, y, c = _place()
        chips = _other_chips(x, y)
        sib = (x, y, 1 - c)
        passed = []
        mine, other = half(c), half(1 - c)
        for i in range(n):
            for j, (cx, cy) in enumerate(chips):
                blk = outs[i].at[2 * cx + cy, mine[i]]
                _remote(blk, blk, send.at[i, j], recv.at[i, j], (x, y, c)).wait_recv()
                cp = _remote(blk, blk, send.at[i, 3 + j], recv.at[i, 3 + j], sib)
                cp.start()
                passed.append(cp)
        for i in range(n):
            for j, (cx, cy) in enumerate(chips):
                blk = outs[i].at[2 * cx + cy, other[i]]
                _remote(blk, blk, send.at[i, 3 + j], recv.at[i, 3 + j], sib).wait_recv()
        for cp in first_copies(ins, outs, send, recv) + passed:
            cp.wait_send()

    return Comm(arrs, [jax.ShapeDtypeStruct((N_CHIPS,) + a.shape, a.dtype) for a in arrs],
                [pltpu.SemaphoreType.DMA((n, 6)), pltpu.SemaphoreType.DMA((n, 6))], start, finish)


def fill_own(gathered, arrs):
    me = 2 * lax.axis_index("x") + lax.axis_index("y")
    return [lax.dynamic_update_index_in_dim(o, a, me, 0) for o, a in zip(gathered, arrs)]


def swap_comm(arrs):
    n = len(arrs)

    def copies(ins, outs, send, recv):
        x, y, c = _place()
        return [_remote(ins[i].at[:, 1 - c], outs[i], send.at[i], recv.at[i], (x, y, 1 - c)) for i in range(n)]

    def start(ins, outs, sems):
        for cp in copies(ins, outs, *sems):
            cp.start()

    def finish(ins, outs, sems):
        for cp in copies(ins, outs, *sems):
            cp.wait()

    return Comm(arrs, [jax.ShapeDtypeStruct((a.shape[0],) + a.shape[2:], a.dtype) for a in arrs],
                [pltpu.SemaphoreType.DMA((n,)), pltpu.SemaphoreType.DMA((n,))], start, finish)


def scatter_comm(arrs):
    n = len(arrs)

    def copies(ins, outs, send, recv):
        x, y, c = _place()
        return [_remote(ins[i].at[2 * cx + cy], outs[i].at[j], send.at[i, j], recv.at[i, j], (cx, cy, c))
                for i in range(n) for j, (cx, cy) in enumerate(_other_chips(x, y))]

    def start(ins, outs, sems):
        for cp in copies(ins, outs, *sems):
            cp.start()

    def finish(ins, outs, sems):
        send, recv = sems
        x, y, c = _place()
        for i in range(n):
            for j in range(3):
                blk = outs[i].at[j]
                _remote(blk, blk, send.at[i, j], recv.at[i, j], (x, y, c)).wait_recv()
        for cp in copies(ins, outs, send, recv):
            cp.wait_send()

    return Comm(arrs, [jax.ShapeDtypeStruct((3,) + a.shape[1:], a.dtype) for a in arrs],
                [pltpu.SemaphoreType.DMA((n, 3)), pltpu.SemaphoreType.DMA((n, 3))], start, finish)


def share_comm(arrs):
    n = len(arrs)

    def copies(outs, send, recv):
        x, y, c = _place()
        return [_remote(outs[i].at[c], outs[i].at[c], send.at[i], recv.at[i], (x, y, 1 - c)) for i in range(n)]

    def start(ins, outs, sems):
        for cp in copies(outs, *sems):
            cp.start()

    def finish(ins, outs, sems):
        send, recv = sems
        x, y, c = _place()
        for i in range(n):
            blk = outs[i].at[1 - c]
            _remote(blk, blk, send.at[i], recv.at[i], (x, y, c)).wait_recv()
        for cp in copies(outs, send, recv):
            cp.wait_send()

    return Comm(arrs, [jax.ShapeDtypeStruct(a.shape, a.dtype) for a in arrs],
                [pltpu.SemaphoreType.DMA((n,)), pltpu.SemaphoreType.DMA((n,))], start, finish,
                aliases={i: i for i in range(n)})


def all_reduce_small(name, packed):
    r = packed.shape[0]

    def body(in_ref, out_ref, buf, send, recv):
        x, y, c = _place()
        me = 4 * x + 2 * y + c
        buf[me] = in_ref[...]
        flips = [(fx, fy, fc) for fx in (0, 1) for fy in (0, 1) for fc in (0, 1) if (fx, fy, fc) != (0, 0, 0)]
        peers = [((x + fx) % 2, (y + fy) % 2, (c + fc) % 2) for fx, fy, fc in flips]
        cps = [_remote(in_ref, buf.at[me], send.at[k], recv.at[k], peer) for k, peer in enumerate(peers)]
        for cp in cps:
            cp.start()
        for k, (px, py, pc) in enumerate(peers):
            blk = buf.at[4 * px + 2 * py + pc]
            _remote(blk, blk, send.at[k], recv.at[k], (x, y, c)).wait_recv()
        for cp in cps:
            cp.wait_send()
        acc = buf[0]
        for d in range(1, N_DEV):
            acc = acc + buf[d]
        out_ref[...] = acc

    vm = pl.BlockSpec(memory_space=pltpu.VMEM)
    return pl.pallas_call(
        body, name=name, in_specs=[vm], out_specs=vm, out_shape=jax.ShapeDtypeStruct(packed.shape, F32),
        scratch_shapes=[pltpu.VMEM((N_DEV, r, LANE), F32), pltpu.SemaphoreType.DMA((N_DEV - 1,)),
                        pltpu.SemaphoreType.DMA((N_DEV - 1,))],
    )(packed)


def add_own_half(name, g, a, c_idx, tr):
    nch, _, r, cc = g.shape
    tr = min(tr, r)

    def body(c_ref, g_ref, a_ref, o_ref):
        o_ref[...] = (g_ref[0] + a_ref[...]).astype(o_ref.dtype)

    return pl.pallas_call(
        body, name=name,
        grid_spec=pltpu.PrefetchScalarGridSpec(
            num_scalar_prefetch=1, grid=(nch, r // tr),
            in_specs=[pl.BlockSpec((1, 1, tr, cc), lambda j, i, c_ref: (j, c_ref[0], i, 0)),
                      pl.BlockSpec((1, tr, cc), lambda j, i, c_ref: (j, i, 0))],
            out_specs=pl.BlockSpec((1, tr, cc), lambda j, i, c_ref: (j, i, 0))),
        out_shape=jax.ShapeDtypeStruct(a.shape, BF16),
        compiler_params=_cparams(("parallel", "parallel")),
    )(c_idx, g, a)


def sum_chips(name, p, b, idx, tr):
    _, r, cc = p.shape
    tr = min(tr, r)

    def body(idx_ref, p_ref, b_ref, o_ref):
        acc = p_ref[0].astype(F32)
        for k in range(3):
            acc = acc + b_ref[k].astype(F32)
        o_ref[0] = acc

    return pl.pallas_call(
        body, name=name,
        grid_spec=pltpu.PrefetchScalarGridSpec(
            num_scalar_prefetch=1, grid=(r // tr,),
            in_specs=[pl.BlockSpec((1, tr, cc), lambda i, s: (s[0], i, 0)),
                      pl.BlockSpec((3, tr, cc), lambda i, s: (0, i, 0))],
            out_specs=pl.BlockSpec((1, tr, cc), lambda i, s: (s[1], i, 0))),
        out_shape=jax.ShapeDtypeStruct((2, r, cc), F32),
        compiler_params=_cparams(("parallel",)),
    )(idx, p, b)


class GradReduce:
    def __init__(self, tag, parts, chip, c_idx):
        self.tag, self.c_idx = tag, c_idx
        self.parts = [p.reshape(p.shape[0], 2, p.shape[1] // 2, p.shape[2]) for p in parts]
        self.idx = jnp.concatenate([chip.astype(jnp.int32).reshape(1), c_idx])

    def swap(self):
        return swap_comm(self.parts)

    def add(self, received):
        self.part = [add_own_half(f"rs{self.tag}_add_sibling_{t}", g, a, self.c_idx, 128)
                     for t, (g, a) in enumerate(zip(self.parts, received))]

    def scatter(self, which=None):
        return scatter_comm(self.part if which is None else [self.part[t] for t in which])

    def finish(self, got):
        red = [sum_chips(f"rs{self.tag}_sum_chips_{t}", p, b, self.idx, 128) for t, (p, b) in enumerate(zip(self.part, got))]
        out = run_comm(f"rs{self.tag}_share_halves", share_comm(red))
        return [o.reshape(-1, o.shape[-1]) for o in out]


def adamw_many(name, ws, gs, ms, vs):
    n = len(ws)

    def body(*refs):
        for i in range(n):
            w_ref, g_ref, m_ref, v_ref = (refs[k * n + i] for k in range(4))
            d_ref, mo_ref, vo_ref = (refs[(4 + k) * n + i] for k in range(3))
            g = g_ref[...]
            m = ADAM_B1 * m_ref[...] + (1.0 - ADAM_B1) * g
            v = ADAM_B2 * v_ref[...] + (1.0 - ADAM_B2) * jnp.square(g)
            m_hat = m / (1.0 - ADAM_B1 ** ADAM_STEP)
            v_hat = v / (1.0 - ADAM_B2 ** ADAM_STEP)
            d_ref[...] = -ADAM_LR * (m_hat / (jnp.sqrt(v_hat) + ADAM_EPS) + ADAM_WD * w_ref[...])
            mo_ref[...] = m
            vo_ref[...] = v

    vm = pl.BlockSpec(memory_space=pltpu.VMEM)
    return pl.pallas_call(
        body, name=name, in_specs=[vm] * (4 * n), out_specs=[vm] * (3 * n),
        out_shape=[jax.ShapeDtypeStruct(a.shape, F32) for a in ws] * 3,
    )(*ws, *gs, *ms, *vs)


def adamw_layers(name, w, g0, g1, m, v, tb):
    _, r, cc = w.shape
    tb = min(tb, r)
    nb = r // tb

    def body(w_ref, g0_ref, g1_ref, m_ref, v_ref, g_ref, d_ref, mo_ref, vo_ref):
        g = jnp.where(pl.program_id(0) == 0, g0_ref[...], g1_ref[...])
        m = ADAM_B1 * m_ref[0] + (1.0 - ADAM_B1) * g
        v = ADAM_B2 * v_ref[0] + (1.0 - ADAM_B2) * jnp.square(g)
        m_hat = m / (1.0 - ADAM_B1 ** ADAM_STEP)
        v_hat = v / (1.0 - ADAM_B2 ** ADAM_STEP)
        g_ref[0] = g
        d_ref[0] = -ADAM_LR * (m_hat / (jnp.sqrt(v_hat) + ADAM_EPS) + ADAM_WD * w_ref[0])
        mo_ref[0] = m
        vo_ref[0] = v

    spec = pl.BlockSpec((1, tb, cc), lambda l, i: (l, i, 0))
    g0_spec = pl.BlockSpec((tb, cc), lambda l, i: (jnp.where(l == 0, i, nb - 1), 0))
    g1_spec = pl.BlockSpec((tb, cc), lambda l, i: (jnp.where(l == 1, i, 0), 0))
    return pl.pallas_call(
        body, name=name, grid=(2, nb), in_specs=[spec, g0_spec, g1_spec, spec, spec], out_specs=[spec] * 4,
        out_shape=[jax.ShapeDtypeStruct(w.shape, F32)] * 4,
        compiler_params=_cparams(("arbitrary", "arbitrary")),
    )(w, g0, g1, m, v)


def adamw_cols_major(name, w, g0, g1, m, v, tb=LANE):
    wt, mt, vt = (jnp.transpose(a, (2, 0, 1)) for a in (w, m, v))
    cc, _, r = wt.shape

    def body(w_ref, g0_ref, g1_ref, m_ref, v_ref, g_ref, d_ref, mo_ref, vo_ref):
        for l, gl_ref in enumerate((g0_ref, g1_ref)):
            g = gl_ref[...].T
            m = ADAM_B1 * m_ref[:, l, :] + (1.0 - ADAM_B1) * g
            v = ADAM_B2 * v_ref[:, l, :] + (1.0 - ADAM_B2) * jnp.square(g)
            m_hat = m / (1.0 - ADAM_B1 ** ADAM_STEP)
            v_hat = v / (1.0 - ADAM_B2 ** ADAM_STEP)
            g_ref[:, l, :] = g
            d_ref[:, l, :] = -ADAM_LR * (m_hat / (jnp.sqrt(v_hat) + ADAM_EPS) + ADAM_WD * w_ref[:, l, :])
            mo_ref[:, l, :] = m
            vo_ref[:, l, :] = v

    spec = pl.BlockSpec((tb, 2, r), lambda i: (i, 0, 0))
    gspec = pl.BlockSpec((r, tb), lambda i: (0, i))
    outs = pl.pallas_call(
        body, name=name, grid=(pl.cdiv(cc, tb),), in_specs=[spec, gspec, gspec, spec, spec], out_specs=[spec] * 4,
        out_shape=[jax.ShapeDtypeStruct(wt.shape, F32)] * 4,
        compiler_params=_cparams(("parallel",)),
    )(wt, g0, g1, mt, vt)
    return [jnp.transpose(o, (1, 2, 0)) for o in outs]


def _pack(arrs):
    flat = jnp.concatenate([a.reshape(-1).astype(F32) for a in arrs])
    pad = (-flat.shape[0]) % (8 * LANE)
    return jnp.pad(flat, (0, pad)).reshape(-1, LANE)


def _unpack(packed, shapes):
    flat = packed.reshape(-1)
    out, off = [], 0
    for s in shapes:
        size = math.prod(s)
        out.append(flat[off:off + size].reshape(s))
        off += size
    return out


BIG = ('w_in', 'w_out', 'b_pw_w')
SMALL = tuple(k for k in WEIGHTS if k not in BIG)
CHIP_SHARDED_SMALL = {'b_conv_w': 2, 'c_conv_w': 2}


def kernel(x, positions, norm_w, w_in, q_norm_w, k_norm_w, sinks, b_conv_w, b_conv_b, b_ln_w, b_ln_b, b_pw_w, b_pw_b, c_conv_w, c_a_log, c_dt_bias, c_onorm_w, w_out, loss_target, m_norm_w, m_w_in, m_q_norm_w, m_k_norm_w, m_sinks, m_b_conv_w, m_b_conv_b, m_b_ln_w, m_b_ln_b, m_b_pw_w, m_b_pw_b, m_c_conv_w, m_c_a_log, m_c_dt_bias, m_c_onorm_w, m_w_out, v_norm_w, v_w_in, v_q_norm_w, v_k_norm_w, v_sinks, v_b_conv_w, v_b_conv_b, v_b_ln_w, v_b_ln_b, v_b_pw_w, v_b_pw_b, v_c_conv_w, v_c_a_log, v_c_dt_bias, v_c_onorm_w, v_w_out):
    cfg = Cfg(x.shape[-1], x.shape[-2])
    w = dict(norm_w=norm_w, w_in=w_in, q_norm_w=q_norm_w, k_norm_w=k_norm_w, sinks=sinks, b_conv_w=b_conv_w,
             b_conv_b=b_conv_b, b_ln_w=b_ln_w, b_ln_b=b_ln_b, b_pw_w=b_pw_w, b_pw_b=b_pw_b, c_conv_w=c_conv_w,
             c_a_log=c_a_log, c_dt_bias=c_dt_bias, c_onorm_w=c_onorm_w, w_out=w_out)
    m = dict(norm_w=m_norm_w, w_in=m_w_in, q_norm_w=m_q_norm_w, k_norm_w=m_k_norm_w, sinks=m_sinks,
             b_conv_w=m_b_conv_w, b_conv_b=m_b_conv_b, b_ln_w=m_b_ln_w, b_ln_b=m_b_ln_b, b_pw_w=m_b_pw_w,
             b_pw_b=m_b_pw_b, c_conv_w=m_c_conv_w, c_a_log=m_c_a_log, c_dt_bias=m_c_dt_bias, c_onorm_w=m_c_onorm_w,
             w_out=m_w_out)
    v = dict(norm_w=v_norm_w, w_in=v_w_in, q_norm_w=v_q_norm_w, k_norm_w=v_k_norm_w, sinks=v_sinks,
             b_conv_w=v_b_conv_w, b_conv_b=v_b_conv_b, b_ln_w=v_b_ln_w, b_ln_b=v_b_ln_b, b_pw_w=v_b_pw_w,
             b_pw_b=v_b_pw_b, c_conv_w=v_c_conv_w, c_a_log=v_c_a_log, c_dt_bias=v_c_dt_bias, c_onorm_w=v_c_onorm_w,
             w_out=v_w_out)
    chip = 2 * lax.axis_index("x") + lax.axis_index("y")
    c_idx = lax.axis_index("c").astype(jnp.int32).reshape(1)
    D, T = cfg.D, cfg.T
    nseq = x.shape[0]
    n = nseq * T
    w_in_b, w_out_b = w_in.astype(BF16), w_out.astype(BF16)

    def permuted(g_in):
        return permute_w_in(cfg, jnp.concatenate(list(g_in), axis=1))

    def layer_prm(l, g_pw, g_bcw, g_ccw):
        prm = {k: w[k][l] for k in SMALL}
        prm['b_pw_w'] = g_pw.reshape(cfg.BW, cfg.BW)
        prm['b_conv_w'] = jnp.concatenate(list(g_bcw[:, l]), axis=1)
        prm['c_conv_w'] = jnp.concatenate(list(g_ccw[:, l]), axis=1)
        return _layer_params(cfg, prm)

    (g_in0,) = fill_own(run_comm("gather_weights_0", gather_comm([w_in_b[0]])), [w_in_b[0]])
    cos, sin = rope_for(cfg, positions)
    early = [w_out_b[0], b_pw_w[0], b_conv_w, c_conv_w]
    top, bottom = w_in_b[1][:D // 2], w_in_b[1][D // 2:]
    late = [w_out_b[1], b_pw_w[1]]
    conv_ws = {}

    def layer0_rest(rode):
        g_out0, g_pw0, conv_ws['b'], conv_ws['c'] = fill_own(rode, early)
        return layer_prm(0, g_pw0, conv_ws['b'], conv_ws['c']), g_out0.reshape(D, D)

    x1, sv0, rode = layer_forward(
        cfg, 0, x.reshape(n, D), (norm_w[0].reshape(1, -1), layer0_rest), permuted(g_in0), None, cos, sin,
        comms=dict(in_proj=gather_comm(early), attn=gather_comm([top]), conv=gather_comm(late),
                   intra=gather_comm([bottom])))
    lp0, wo0 = sv0['lp'], sv0['wo']
    (g_top,), (g_bottom,) = fill_own(rode['attn'], [top]), fill_own(rode['intra'], [bottom])
    g_out1, g_pw1 = fill_own(rode['conv'], late)
    g_in1 = jnp.concatenate([g_top, g_bottom], axis=1)
    wp1, wo1 = permuted(g_in1), g_out1.reshape(D, D)
    lp1 = layer_prm(1, g_pw1, conv_ws['b'], conv_ws['c'])
    (dx2, loss_local), sv1, _ = layer_forward(cfg, 1, x1, lp1, wp1, wo1, cos, sin, target=loss_target.reshape(n, D))

    def partials(gr):
        return [gr['w_in_blocks'], gr['w_out'].reshape(N_CHIPS, D // N_CHIPS, D),
                gr['b_pw_w'].reshape(N_CHIPS, cfg.BW // N_CHIPS, cfg.BW)]

    dx1, gr1, (_, rs1, _) = layer_backward(cfg, 1, dx2, sv1, lp1, g_in1, wo1, cos, sin,
                                        own_rs=lambda gr: GradReduce(1, partials(gr), chip, c_idx))
    dx0, gr0, (got1, rs0, got0) = layer_backward(cfg, 0, dx1, sv0, lp0, g_in0, wo0, cos, sin, rs=rs1,
                                           own_rs=lambda gr: GradReduce(0, partials(gr), chip, c_idx))
    red1 = rs1.finish(got1)
    red0 = rs0.finish(got0)
    grad_x = dx0.reshape(x.shape)
    grads = [gr0, gr1]

    small_parts = [_stack([grads[l][k] for l in range(DEPTH)]) for k in SMALL] + [loss_local.reshape(1)]
    *small_red, loss = _unpack(all_reduce_small("all_reduce_small", _pack(small_parts)), [a.shape for a in small_parts])
    loss = loss.reshape(())
    g = {}
    for k, a in zip(SMALL, small_red):
        if k in CHIP_SHARDED_SMALL:
            ax = CHIP_SHARDED_SMALL[k]
            width = a.shape[ax] // N_CHIPS
            a = lax.dynamic_slice_in_dim(a, chip * width, width, axis=ax)
        g[k] = a

    delta, new_m, new_v = {}, {}, {}
    for k, g0, g1 in zip(BIG, red0, red1):
        update = adamw_layers if w[k].shape[-1] % LANE == 0 else adamw_cols_major
        g[k], delta[k], new_m[k], new_v[k] = update(f"adamw_{k}", w[k], g0, g1, m[k], v[k], 128)
    outs = adamw_many("adamw_small", *[[d[k] for k in SMALL] for d in (w, g, m, v)])
    for i, k in enumerate(SMALL):
        delta[k], new_m[k], new_v[k] = outs[i], outs[len(SMALL) + i], outs[2 * len(SMALL) + i]
    return (loss, grad_x, *[g[k] for k in WEIGHTS], *[delta[k] for k in WEIGHTS], *[new_m[k] for k in WEIGHTS],
            *[new_v[k] for k in WEIGHTS])
```
